```python
import math
import jax, jax.numpy as jnp
from jax import lax
import numpy as np

D_MODEL = 1024
BATCH = 16
SEQ = 2048
DEPTH = 1

PLE_DIM = 256
HEAD_DIM = 64
ATTN_WIDTH = D_MODEL // 2
ATTN_HEADS = ATTN_WIDTH // HEAD_DIM
SSM_WIDTH = D_MODEL - ATTN_WIDTH
SSM_GROUP = 16
SSM_GROUPS = SSM_WIDTH // SSM_GROUP
SSM_STATE = 64
MIX_WIDTH = ATTN_WIDTH + SSM_WIDTH
IN_WIDTH = 4 * ATTN_WIDTH + 2 * SSM_WIDTH
DILATED_CONFIGS = ((128, 1), (512, 4), (2048, 16))
BLOCK_Q = 128
EPS = 1e-6
DT_MIN = 1e-3
DT_MAX = 1e-1

kernel_name = "hymba_s5_longnet_hybrid"


def rms_norm(x, gain):
    xf = x.astype(jnp.float32)
    y = xf * lax.rsqrt(jnp.mean(xf * xf, axis=-1, keepdims=True) + EPS) * gain.astype(jnp.float32)
    return y.astype(x.dtype)


def banded_causal_attention(q, k, v, w):
    n, l, dh = q.shape
    nb = -(-l // BLOCK_Q)
    lp = nb * BLOCK_Q
    nk = BLOCK_Q + w
    qb = jnp.pad(q, ((0, 0), (0, lp - l), (0, 0))).reshape(n, nb, BLOCK_Q, dh)
    kp = jnp.pad(k, ((0, 0), (w, lp - l), (0, 0)))
    vp = jnp.pad(v, ((0, 0), (w, lp - l), (0, 0)))
    block_start = jnp.arange(nb) * BLOCK_Q
    idx = block_start[:, None] + jnp.arange(nk)[None, :]
    kb = kp[:, idx]
    vb = vp[:, idx]
    s = jnp.einsum('nbqd,nbkd->nbqk', qb, kb)
    qi = jnp.arange(BLOCK_Q)[:, None]
    kj = jnp.arange(nk)[None, :]
    key_pos = block_start[:, None, None] + kj[None] - w
    valid = (kj >= qi)[None] & (kj <= qi + w)[None] & (key_pos >= 0)
    s = jnp.where(valid[None], s, -jnp.inf)
    m = jnp.max(s, axis=-1, keepdims=True)
    e = jnp.exp(s - m)
    den = jnp.sum(e, axis=-1)
    o = jnp.einsum('nbqk,nbkd->nbqd', e, vb) / den[..., None]
    lse = m[..., 0] + jnp.log(den)
    return o.reshape(n, lp, dh)[:, :l], lse.reshape(n, lp)[:, :l]


def dilated_window_attention(q, k, v, window, dilation):
    b, s, h, dh = q.shape
    l = s // dilation
    w = window // dilation

    def to_classes(t):
        return t.reshape(b, l, dilation, h, dh).transpose(0, 2, 3, 1, 4).reshape(b * dilation * h, l, dh)

    o, lse = banded_causal_attention(to_classes(q), to_classes(k), to_classes(v), w)
    o = o.reshape(b, dilation, h, l, dh).transpose(0, 3, 1, 2, 4).reshape(b, s, h, dh)
    lse = lse.reshape(b, dilation, h, l).transpose(0, 3, 1, 2).reshape(b, s, h)
    return o, lse


def dilated_mixture_attention(q, k, v, q_gain, k_gain):
    b, s, _ = q.shape

    def heads(t):
        return t.reshape(b, s, ATTN_HEADS, HEAD_DIM).astype(jnp.float32)

    qh = rms_norm(heads(q), q_gain) * (HEAD_DIM ** -0.5)
    kh = rms_norm(heads(k), k_gain)
    vh = heads(v)
    outs, lses = [], []
    for window, dilation in DILATED_CONFIGS:
        o, lse = dilated_window_attention(qh, kh, vh, window, dilation)
        outs.append(o)
        lses.append(lse)
    wts = jax.nn.softmax(jnp.stack(lses), axis=0)
    o = jnp.sum(wts[..., None] * jnp.stack(outs), axis=0)
    return o.reshape(b, s, ATTN_WIDTH).astype(q.dtype)


def s5_glu(u, lam_re, lam_im, log_dt, b_re, b_im, c_re, c_im, d_skip, w_glu, b_glu):
    f32 = jnp.float32
    bsz, s, _ = u.shape
    uf = u.astype(f32)
    ug = uf.reshape(bsz, s, SSM_GROUPS, SSM_GROUP)
    lam_re = lam_re.astype(f32); lam_im = lam_im.astype(f32)
    dt = jnp.exp(log_dt.astype(f32))[:, None]
    mag = jnp.exp(lam_re * dt)
    a_re = mag * jnp.cos(lam_im * dt)
    a_im = mag * jnp.sin(lam_im * dt)
    den = lam_re * lam_re + lam_im * lam_im
    num_re = a_re - 1.0
    coef_re = (num_re * lam_re + a_im * lam_im) / den
    coef_im = (a_im * lam_re - num_re * lam_im) / den
    b_re = b_re.astype(f32); b_im = b_im.astype(f32)
    bb_re = coef_re[..., None] * b_re - coef_im[..., None] * b_im
    bb_im = coef_re[..., None] * b_im + coef_im[..., None] * b_re
    bu_re = jnp.einsum('bsgc,gnc->bsgn', ug, bb_re)
    bu_im = jnp.einsum('bsgc,gnc->bsgn', ug, bb_im)
    ar = jnp.broadcast_to(a_re, bu_re.shape)
    ai = jnp.broadcast_to(a_im, bu_re.shape)

    def combine(left, right):
        ar1, ai1, br1, bi1 = left
        ar2, ai2, br2, bi2 = right
        return (ar2 * ar1 - ai2 * ai1,
                ar2 * ai1 + ai2 * ar1,
                ar2 * br1 - ai2 * bi1 + br2,
                ar2 * bi1 + ai2 * br1 + bi2)

    _, _, xr, xi = lax.associative_scan(combine, (ar, ai, bu_re, bu_im), axis=1)
    y = (jnp.einsum('bsgn,gcn->bsgc', xr, c_re.astype(f32))
         - jnp.einsum('bsgn,gcn->bsgc', xi, c_im.astype(f32))).reshape(bsz, s, SSM_WIDTH)
    y = y + d_skip.astype(f32) * uf
    yg = jax.nn.gelu(y, approximate=False)
    out = yg * jax.nn.sigmoid(yg @ w_glu.astype(f32) + b_glu.astype(f32))
    return out.astype(u.dtype)


def _fwd_setup_inputs(seed: int = 0) -> dict:
    key = jax.random.key(seed)
    ks = jax.random.split(key, 24)
    f32 = jnp.float32
    nrm = lambda k, shape, scale: jax.random.normal(k, shape, f32) * scale
    x = nrm(ks[0], (BATCH, SEQ, D_MODEL), 1.0)
    p = nrm(ks[1], (DEPTH, BATCH, SEQ, PLE_DIM), 1.0)
    mix_norm = 1.0 + nrm(ks[2], (DEPTH, D_MODEL), 0.02)
    w_in = nrm(ks[3], (DEPTH, D_MODEL, IN_WIDTH), D_MODEL ** -0.5)
    q_norm = 1.0 + nrm(ks[4], (DEPTH, HEAD_DIM), 0.02)
    k_norm = 1.0 + nrm(ks[5], (DEPTH, HEAD_DIM), 0.02)
    n_idx = jnp.arange(SSM_STATE, dtype=f32)
    lambda_re = -0.5 + nrm(ks[6], (DEPTH, SSM_GROUPS, SSM_STATE), 0.01)
    lambda_im = math.pi * n_idx + nrm(ks[7], (DEPTH, SSM_GROUPS, SSM_STATE), 0.01)
    log_dt = jax.random.uniform(ks[8], (DEPTH, SSM_GROUPS), f32, math.log(DT_MIN), math.log(DT_MAX))
    b_re = nrm(ks[9], (DEPTH, SSM_GROUPS, SSM_STATE, SSM_GROUP), (2.0 * SSM_GROUP) ** -0.5)
    b_im = nrm(ks[10], (DEPTH, SSM_GROUPS, SSM_STATE, SSM_GROUP), (2.0 * SSM_GROUP) ** -0.5)
    c_re = nrm(ks[11], (DEPTH, SSM_GROUPS, SSM_GROUP, SSM_STATE), (2.0 * SSM_STATE) ** -0.5)
    c_im = nrm(ks[12], (DEPTH, SSM_GROUPS, SSM_GROUP, SSM_STATE), (2.0 * SSM_STATE) ** -0.5)
    d_skip = nrm(ks[13], (DEPTH, SSM_WIDTH), 1.0)
    w_glu = nrm(ks[14], (DEPTH, SSM_WIDTH, SSM_WIDTH), SSM_WIDTH ** -0.5)
    b_glu = nrm(ks[15], (DEPTH, SSM_WIDTH), 0.02)
    w_out = nrm(ks[16], (DEPTH, MIX_WIDTH, D_MODEL), MIX_WIDTH ** -0.5)
    ple_norm = 1.0 + nrm(ks[17], (DEPTH, D_MODEL), 0.02)
    w_ple_gate = nrm(ks[18], (DEPTH, D_MODEL, D_MODEL), D_MODEL ** -0.5)
    w_ple_proj = nrm(ks[19], (DEPTH, PLE_DIM, D_MODEL), PLE_DIM ** -0.5)
    return {"x": x, "p": p, "mix_norm": mix_norm, "w_in": w_in, "q_norm": q_norm,
            "k_norm": k_norm, "lambda_re": lambda_re, "lambda_im": lambda_im, "log_dt": log_dt,
            "b_re": b_re, "b_im": b_im, "c_re": c_re, "c_im": c_im, "d_skip": d_skip,
            "w_glu": w_glu, "b_glu": b_glu, "w_out": w_out, "ple_norm": ple_norm,
            "w_ple_gate": w_ple_gate, "w_ple_proj": w_ple_proj}


def _fwd_reference(x, p, mix_norm, w_in, q_norm, k_norm, lambda_re, lambda_im, log_dt,
              b_re, b_im, c_re, c_im, d_skip, w_glu, b_glu, w_out, ple_norm,
              w_ple_gate, w_ple_proj):
    A = ATTN_WIDTH
    splits = [A, 2 * A, 3 * A, 4 * A, 4 * A + SSM_WIDTH]
    h = x
    for i in range(DEPTH):
        xn = rms_norm(h, mix_norm[i])
        z = xn @ w_in[i]
        q, k, v, gate_a, u, gate_s = jnp.split(z, splits, axis=-1)
        attn = dilated_mixture_attention(q, k, v, q_norm[i], k_norm[i]) * jax.nn.silu(gate_a)
        ssm = s5_glu(u, lambda_re[i], lambda_im[i], log_dt[i], b_re[i], b_im[i], c_re[i],
                     c_im[i], d_skip[i], w_glu[i], b_glu[i]) * jax.nn.silu(gate_s)
        h = h + jnp.concatenate([attn, ssm], axis=-1) @ w_out[i]
        gate = jax.nn.sigmoid(rms_norm(h, ple_norm[i]) @ w_ple_gate[i])
        h = h + gate * (p[i] @ w_ple_proj[i])
    return h


import jax as _jax
import jax.numpy as _jnp

TWIN_FORMAT = 'train_step'
FWD_PARAMS = ['x', 'p', 'mix_norm', 'w_in', 'q_norm', 'k_norm', 'lambda_re', 'lambda_im', 'log_dt', 'b_re', 'b_im', 'c_re', 'c_im', 'd_skip', 'w_glu', 'b_glu', 'w_out', 'ple_norm', 'w_ple_gate', 'w_ple_proj']
TWIN_WEIGHTS = ['mix_norm', 'w_in', 'q_norm', 'k_norm', 'lambda_re', 'lambda_im', 'log_dt', 'b_re', 'b_im', 'c_re', 'c_im', 'd_skip', 'w_glu', 'b_glu', 'w_out', 'ple_norm', 'w_ple_gate', 'w_ple_proj']
TWIN_DIFF_INPUT = 'x'
TWIN_INPUTS = ['x', 'p', 'mix_norm', 'w_in', 'q_norm', 'k_norm', 'lambda_re', 'lambda_im', 'log_dt', 'b_re', 'b_im', 'c_re', 'c_im', 'd_skip', 'w_glu', 'b_glu', 'w_out', 'ple_norm', 'w_ple_gate', 'w_ple_proj', 'loss_target', 'm_mix_norm', 'm_w_in', 'm_q_norm', 'm_k_norm', 'm_lambda_re', 'm_lambda_im', 'm_log_dt', 'm_b_re', 'm_b_im', 'm_c_re', 'm_c_im', 'm_d_skip', 'm_w_glu', 'm_b_glu', 'm_w_out', 'm_ple_norm', 'm_w_ple_gate', 'm_w_ple_proj', 'v_mix_norm', 'v_w_in', 'v_q_norm', 'v_k_norm', 'v_lambda_re', 'v_lambda_im', 'v_log_dt', 'v_b_re', 'v_b_im', 'v_c_re', 'v_c_im', 'v_d_skip', 'v_w_glu', 'v_b_glu', 'v_w_out', 'v_ple_norm', 'v_w_ple_gate', 'v_w_ple_proj']
TWIN_OUTPUTS = ['loss', 'grad_x', 'grad_mix_norm', 'grad_w_in', 'grad_q_norm', 'grad_k_norm', 'grad_lambda_re', 'grad_lambda_im', 'grad_log_dt', 'grad_b_re', 'grad_b_im', 'grad_c_re', 'grad_c_im', 'grad_d_skip', 'grad_w_glu', 'grad_b_glu', 'grad_w_out', 'grad_ple_norm', 'grad_w_ple_gate', 'grad_w_ple_proj', 'delta_mix_norm', 'delta_w_in', 'delta_q_norm', 'delta_k_norm', 'delta_lambda_re', 'delta_lambda_im', 'delta_log_dt', 'delta_b_re', 'delta_b_im', 'delta_c_re', 'delta_c_im', 'delta_d_skip', 'delta_w_glu', 'delta_b_glu', 'delta_w_out', 'delta_ple_norm', 'delta_w_ple_gate', 'delta_w_ple_proj', 'new_m_mix_norm', 'new_m_w_in', 'new_m_q_norm', 'new_m_k_norm', 'new_m_lambda_re', 'new_m_lambda_im', 'new_m_log_dt', 'new_m_b_re', 'new_m_b_im', 'new_m_c_re', 'new_m_c_im', 'new_m_d_skip', 'new_m_w_glu', 'new_m_b_glu', 'new_m_w_out', 'new_m_ple_norm', 'new_m_w_ple_gate', 'new_m_w_ple_proj', 'new_v_mix_norm', 'new_v_w_in', 'new_v_q_norm', 'new_v_k_norm', 'new_v_lambda_re', 'new_v_lambda_im', 'new_v_log_dt', 'new_v_b_re', 'new_v_b_im', 'new_v_c_re', 'new_v_c_im', 'new_v_d_skip', 'new_v_w_glu', 'new_v_b_glu', 'new_v_w_out', 'new_v_ple_norm', 'new_v_w_ple_gate', 'new_v_w_ple_proj']
TWIN_LEAF_KINDS = {'loss': 'loss', 'grad_x': 'grad_x', 'grad_mix_norm': 'grad_w', 'grad_w_in': 'grad_w', 'grad_q_norm': 'grad_w', 'grad_k_norm': 'grad_w', 'grad_lambda_re': 'grad_w', 'grad_lambda_im': 'grad_w', 'grad_log_dt': 'grad_w', 'grad_b_re': 'grad_w', 'grad_b_im': 'grad_w', 'grad_c_re': 'grad_w', 'grad_c_im': 'grad_w', 'grad_d_skip': 'grad_w', 'grad_w_glu': 'grad_w', 'grad_b_glu': 'grad_w', 'grad_w_out': 'grad_w', 'grad_ple_norm': 'grad_w', 'grad_w_ple_gate': 'grad_w', 'grad_w_ple_proj': 'grad_w', 'delta_mix_norm': 'delta_w', 'delta_w_in': 'delta_w', 'delta_q_norm': 'delta_w', 'delta_k_norm': 'delta_w', 'delta_lambda_re': 'delta_w', 'delta_lambda_im': 'delta_w', 'delta_log_dt': 'delta_w', 'delta_b_re': 'delta_w', 'delta_b_im': 'delta_w', 'delta_c_re': 'delta_w', 'delta_c_im': 'delta_w', 'delta_d_skip': 'delta_w', 'delta_w_glu': 'delta_w', 'delta_b_glu': 'delta_w', 'delta_w_out': 'delta_w', 'delta_ple_norm': 'delta_w', 'delta_w_ple_gate': 'delta_w', 'delta_w_ple_proj': 'delta_w', 'new_m_mix_norm': 'new_m', 'new_m_w_in': 'new_m', 'new_m_q_norm': 'new_m', 'new_m_k_norm': 'new_m', 'new_m_lambda_re': 'new_m', 'new_m_lambda_im': 'new_m', 'new_m_log_dt': 'new_m', 'new_m_b_re': 'new_m', 'new_m_b_im': 'new_m', 'new_m_c_re': 'new_m', 'new_m_c_im': 'new_m', 'new_m_d_skip': 'new_m', 'new_m_w_glu': 'new_m', 'new_m_b_glu': 'new_m', 'new_m_w_out': 'new_m', 'new_m_ple_norm': 'new_m', 'new_m_w_ple_gate': 'new_m', 'new_m_w_ple_proj': 'new_m', 'new_v_mix_norm': 'new_v', 'new_v_w_in': 'new_v', 'new_v_q_norm': 'new_v', 'new_v_k_norm': 'new_v', 'new_v_lambda_re': 'new_v', 'new_v_lambda_im': 'new_v', 'new_v_log_dt': 'new_v', 'new_v_b_re': 'new_v', 'new_v_b_im': 'new_v', 'new_v_c_re': 'new_v', 'new_v_c_im': 'new_v', 'new_v_d_skip': 'new_v', 'new_v_w_glu': 'new_v', 'new_v_b_glu': 'new_v', 'new_v_w_out': 'new_v', 'new_v_ple_norm': 'new_v', 'new_v_w_ple_gate': 'new_v', 'new_v_w_ple_proj': 'new_v'}


def _forward(args):
    return _fwd_reference(*[args[k] for k in FWD_PARAMS])


def _output_shape():
    out = _jax.eval_shape(lambda: _forward(_fwd_setup_inputs(0)))
    return out.shape, out.dtype

N_MICROBATCH = 1
ADAM_LR = 0.001
ADAM_B1 = 0.9
ADAM_B2 = 0.999
ADAM_EPS = 1e-08
ADAM_WD = 0.01
ADAM_STEP = 10
PER_EXAMPLE_BATCH_AXIS = {'x': 0, 'p': 1, 'loss_target': 0}
SHARED_INPUTS = []
_WEIGHT_DTYPES = {'mix_norm': _jnp.float32, 'w_in': _jnp.float32, 'q_norm': _jnp.float32, 'k_norm': _jnp.float32, 'lambda_re': _jnp.float32, 'lambda_im': _jnp.float32, 'log_dt': _jnp.float32, 'b_re': _jnp.float32, 'b_im': _jnp.float32, 'c_re': _jnp.float32, 'c_im': _jnp.float32, 'd_skip': _jnp.float32, 'w_glu': _jnp.float32, 'b_glu': _jnp.float32, 'w_out': _jnp.float32, 'ple_norm': _jnp.float32, 'w_ple_gate': _jnp.float32, 'w_ple_proj': _jnp.float32}
MOMENT_SCALE = {'mix_norm': 1.496122e+00, 'w_in': 7.091493e-02, 'q_norm': 6.438479e-01, 'k_norm': 6.367699e-01, 'lambda_re': 5.060203e-03, 'lambda_im': 4.059173e-03, 'log_dt': 5.239912e+00, 'b_re': 2.853973e-03, 'b_im': 3.051801e-03, 'c_re': 6.027674e-03, 'c_im': 5.703907e-03, 'd_skip': 1.700884e+00, 'w_glu': 3.561155e-01, 'b_glu': 1.062339e+00, 'w_out': 7.674774e-02, 'ple_norm': 9.527413e-01, 'w_ple_gate': 7.867010e-02, 'w_ple_proj': 4.241151e-01}


def _to_microbatches(a, axis):
    t = _jnp.moveaxis(a, axis, 0)
    t = t.reshape((N_MICROBATCH, t.shape[0] // N_MICROBATCH) + t.shape[1:])
    return _jnp.moveaxis(t, 1, axis + 1)


def setup_inputs(seed: int = 0) -> dict:
    inp = _fwd_setup_inputs(seed)
    key = _jax.random.fold_in(_jax.random.key(seed), 7919)
    shape, _ = _output_shape()
    out = dict(inp)
    out["loss_target"] = _jax.random.normal(_jax.random.fold_in(key, 0), shape, _jnp.float32)
    for i, name in enumerate(TWIN_WEIGHTS):
        w = inp[name].astype(_jnp.float32)
        if MOMENT_SCALE is None:
            s = _jnp.sqrt(_jnp.mean(_jnp.square(w)) + 1e-30)
        else:
            s = MOMENT_SCALE[name]
        km, kv = _jax.random.split(_jax.random.fold_in(key, i + 1))
        out[name] = w
        out["m_" + name] = s * _jax.random.normal(km, w.shape, _jnp.float32)
        out["v_" + name] = (s * s) * _jax.random.uniform(kv, w.shape, _jnp.float32, 0.5, 1.5)
    if N_MICROBATCH > 1:
        for name, axis in PER_EXAMPLE_BATCH_AXIS.items():
            out[name] = _to_microbatches(out[name], axis)
    return {'x': out['x'], 'p': out['p'], 'mix_norm': out['mix_norm'], 'w_in': out['w_in'], 'q_norm': out['q_norm'], 'k_norm': out['k_norm'], 'lambda_re': out['lambda_re'], 'lambda_im': out['lambda_im'], 'log_dt': out['log_dt'], 'b_re': out['b_re'], 'b_im': out['b_im'], 'c_re': out['c_re'], 'c_im': out['c_im'], 'd_skip': out['d_skip'], 'w_glu': out['w_glu'], 'b_glu': out['b_glu'], 'w_out': out['w_out'], 'ple_norm': out['ple_norm'], 'w_ple_gate': out['w_ple_gate'], 'w_ple_proj': out['w_ple_proj'], 'loss_target': out['loss_target'], 'm_mix_norm': out['m_mix_norm'], 'm_w_in': out['m_w_in'], 'm_q_norm': out['m_q_norm'], 'm_k_norm': out['m_k_norm'], 'm_lambda_re': out['m_lambda_re'], 'm_lambda_im': out['m_lambda_im'], 'm_log_dt': out['m_log_dt'], 'm_b_re': out['m_b_re'], 'm_b_im': out['m_b_im'], 'm_c_re': out['m_c_re'], 'm_c_im': out['m_c_im'], 'm_d_skip': out['m_d_skip'], 'm_w_glu': out['m_w_glu'], 'm_b_glu': out['m_b_glu'], 'm_w_out': out['m_w_out'], 'm_ple_norm': out['m_ple_norm'], 'm_w_ple_gate': out['m_w_ple_gate'], 'm_w_ple_proj': out['m_w_ple_proj'], 'v_mix_norm': out['v_mix_norm'], 'v_w_in': out['v_w_in'], 'v_q_norm': out['v_q_norm'], 'v_k_norm': out['v_k_norm'], 'v_lambda_re': out['v_lambda_re'], 'v_lambda_im': out['v_lambda_im'], 'v_log_dt': out['v_log_dt'], 'v_b_re': out['v_b_re'], 'v_b_im': out['v_b_im'], 'v_c_re': out['v_c_re'], 'v_c_im': out['v_c_im'], 'v_d_skip': out['v_d_skip'], 'v_w_glu': out['v_w_glu'], 'v_b_glu': out['v_b_glu'], 'v_w_out': out['v_w_out'], 'v_ple_norm': out['v_ple_norm'], 'v_w_ple_gate': out['v_w_ple_gate'], 'v_w_ple_proj': out['v_w_ple_proj']}


def _loss(weights, diff, rest, loss_target):
    with _jax.named_scope("forward"):
        args = {**rest, TWIN_DIFF_INPUT: diff, **{k: w.astype(_WEIGHT_DTYPES[k]) for k, w in weights.items()}}
        y = _forward(args)
    with _jax.named_scope("loss_head"):
        err = _jnp.square(y.astype(_jnp.float32) - loss_target)
        return 0.5 * _jnp.sum(_jnp.mean(err, axis=-1)) if err.ndim else 0.5 * err


def _adamw(w, g, m, v):
    m = ADAM_B1 * m + (1.0 - ADAM_B1) * g
    v = ADAM_B2 * v + (1.0 - ADAM_B2) * _jnp.square(g)
    m_hat = m / (1.0 - ADAM_B1 ** ADAM_STEP)
    v_hat = v / (1.0 - ADAM_B2 ** ADAM_STEP)
    delta = -ADAM_LR * (m_hat / (_jnp.sqrt(v_hat) + ADAM_EPS) + ADAM_WD * w)
    return delta, m, v


def reference(x, p, mix_norm, w_in, q_norm, k_norm, lambda_re, lambda_im, log_dt, b_re, b_im, c_re, c_im, d_skip, w_glu, b_glu, w_out, ple_norm, w_ple_gate, w_ple_proj, loss_target, m_mix_norm, m_w_in, m_q_norm, m_k_norm, m_lambda_re, m_lambda_im, m_log_dt, m_b_re, m_b_im, m_c_re, m_c_im, m_d_skip, m_w_glu, m_b_glu, m_w_out, m_ple_norm, m_w_ple_gate, m_w_ple_proj, v_mix_norm, v_w_in, v_q_norm, v_k_norm, v_lambda_re, v_lambda_im, v_log_dt, v_b_re, v_b_im, v_c_re, v_c_im, v_d_skip, v_w_glu, v_b_glu, v_w_out, v_ple_norm, v_w_ple_gate, v_w_ple_proj):
    given = dict(x=x, p=p, mix_norm=mix_norm, w_in=w_in, q_norm=q_norm, k_norm=k_norm, lambda_re=lambda_re, lambda_im=lambda_im, log_dt=log_dt, b_re=b_re, b_im=b_im, c_re=c_re, c_im=c_im, d_skip=d_skip, w_glu=w_glu, b_glu=b_glu, w_out=w_out, ple_norm=ple_norm, w_ple_gate=w_ple_gate, w_ple_proj=w_ple_proj, loss_target=loss_target, m_mix_norm=m_mix_norm, m_w_in=m_w_in, m_q_norm=m_q_norm, m_k_norm=m_k_norm, m_lambda_re=m_lambda_re, m_lambda_im=m_lambda_im, m_log_dt=m_log_dt, m_b_re=m_b_re, m_b_im=m_b_im, m_c_re=m_c_re, m_c_im=m_c_im, m_d_skip=m_d_skip, m_w_glu=m_w_glu, m_b_glu=m_b_glu, m_w_out=m_w_out, m_ple_norm=m_ple_norm, m_w_ple_gate=m_w_ple_gate, m_w_ple_proj=m_w_ple_proj, v_mix_norm=v_mix_norm, v_w_in=v_w_in, v_q_norm=v_q_norm, v_k_norm=v_k_norm, v_lambda_re=v_lambda_re, v_lambda_im=v_lambda_im, v_log_dt=v_log_dt, v_b_re=v_b_re, v_b_im=v_b_im, v_c_re=v_c_re, v_c_im=v_c_im, v_d_skip=v_d_skip, v_w_glu=v_w_glu, v_b_glu=v_b_glu, v_w_out=v_w_out, v_ple_norm=v_ple_norm, v_w_ple_gate=v_w_ple_gate, v_w_ple_proj=v_w_ple_proj)
    weights = {n: given[n] for n in TWIN_WEIGHTS}
    shared = {n: given[n] for n in SHARED_INPUTS}
    per_example = {n: given[n] for n in ['x', 'p']}
    grad_fn = _jax.value_and_grad(_loss, argnums=(0, 1))

    def one_microbatch(ex, loss_target):
        ex = dict(ex)
        diff = ex.pop(TWIN_DIFF_INPUT)
        return grad_fn(weights, diff, {**shared, **ex}, loss_target)

    if N_MICROBATCH == 1:
        loss, (grad_w, grad_x) = one_microbatch(per_example, given["loss_target"])
    else:
        def body(carry, xs):
            loss_sum, grad_sum = carry
            l_k, (gw_k, gx_k) = one_microbatch(xs[0], xs[1])
            with _jax.named_scope("update"):
                return (loss_sum + l_k, _jax.tree.map(_jnp.add, grad_sum, gw_k)), gx_k

        init = (_jnp.zeros((), _jnp.float32), _jax.tree.map(_jnp.zeros_like, weights))
        (loss, grad_w), grad_x = _jax.lax.scan(body, init, (per_example, given["loss_target"]))
    with _jax.named_scope("update"):
        delta_w, new_m, new_v = {}, {}, {}
        for n in TWIN_WEIGHTS:
            delta_w[n], new_m[n], new_v[n] = _adamw(weights[n], grad_w[n], given["m_" + n], given["v_" + n])
    return (loss, grad_x, *[grad_w[n] for n in TWIN_WEIGHTS], *[delta_w[n] for n in TWIN_WEIGHTS],
            *[new_m[n] for n in TWIN_WEIGHTS], *[new_v[n] for n in TWIN_WEIGHTS])
```

```python
import math

import jax
import jax.numpy as jnp
from jax import lax
from jax.experimental import pallas as pl
from jax.experimental.pallas import tpu as pltpu

F32 = jnp.float32
BF16 = jnp.bfloat16
MESH = pl.DeviceIdType.MESH
AXES = ("x", "y", "c")
N_DEV = 8

D_MODEL = 1024
HEAD_DIM = 64
ATTN_W = 512
SSM_W = 512
SSM_GROUPS = 32
SSM_GROUP = 16
SSM_STATE = 64
N_STATE = SSM_GROUPS * SSM_STATE
PLE_DIM = 256
IN_W = 3072
COL_W = IN_W // N_DEV
DILATED = ((128, 1), (512, 4), (2048, 16))
EPS = 1e-6
INV_SQRT2 = 1.0 / math.sqrt(2.0)
INV_SQRT_2PI = 1.0 / math.sqrt(2.0 * math.pi)

ADAM_LR, ADAM_B1, ADAM_B2, ADAM_EPS, ADAM_WD, ADAM_STEP = 0.001, 0.9, 0.999, 1e-08, 0.01, 10

VMEM_LIMIT_V7X = 56 * 1024 * 1024
SUBLANES = 8
LANES = 128


def _params(n_axes=None, vmem=None):
    kw = {}
    if n_axes:
        kw["dimension_semantics"] = ("arbitrary",) * n_axes
    if vmem:
        kw["vmem_limit_bytes"] = vmem
    return pltpu.CompilerParams(**kw)


def _dot(a, b):
    return jnp.dot(a, b, preferred_element_type=F32)


def _dot_nt(a, b):
    return lax.dot_general(a, b, (((1,), (1,)), ((), ())), preferred_element_type=F32)


def _dot_tn(a, b):
    return lax.dot_general(a, b, (((0,), (0,)), ((), ())), preferred_element_type=F32)


def _hdot(a, b):
    return jnp.dot(a, b, precision=lax.Precision.HIGHEST, preferred_element_type=F32)


def _sig(x):
    return 1.0 / (1.0 + jnp.exp(-x))


def _gelu_and_grad(y):
    cdf = 0.5 * (1.0 + lax.erf(y * INV_SQRT2))
    pdf = jnp.exp(-0.5 * y * y) * INV_SQRT_2PI
    return y * cdf, cdf + y * pdf


def _vmem():
    return pl.BlockSpec(memory_space=pltpu.VMEM)


def _full(shape):
    nd = len(shape)
    return pl.BlockSpec(shape, lambda *_: (0,) * nd)


def _all_gather(shards, out_dtypes, name):
    n = len(shards)

    def body(*refs):
        in_refs, out_refs = refs[:n], refs[n:2 * n]
        send_sems, recv_sems = refs[2 * n], refs[2 * n + 1]
        x, y, c = lax.axis_index("x"), lax.axis_index("y"), lax.axis_index("c")
        me, sibling = (x, y, c), (x, y, 1 - c)
        chips = [(1 - x, y), (x, 1 - y), (1 - x, 1 - y)]

        def idx(px, py, pc):
            return 4 * px + 2 * py + pc

        def copy(i, k, block, to):
            ref = out_refs[i].at[idx(*block)]
            return pltpu.make_async_remote_copy(
                src_ref=ref, dst_ref=ref, send_sem=send_sems.at[7 * i + k], recv_sem=recv_sems.at[7 * i + k],
                device_id=to, device_id_type=MESH)

        for i in range(n):
            out_refs[i][idx(*me)] = in_refs[i][...].astype(out_refs[i].dtype)
        first = []
        for i in range(n):
            first.append(copy(i, 0, me, sibling))
            first += [copy(i, 1 + j, me, (*chip, c)) for j, chip in enumerate(chips)]
        for cp in first:
            cp.start()
        passed = []
        for j, chip in enumerate(chips):
            for i in range(n):
                copy(i, 1 + j, (*chip, c), me).wait_recv()
                cp = copy(i, 4 + j, (*chip, c), sibling)
                cp.start()
                passed.append(cp)
        for i in range(n):
            copy(i, 0, sibling, me).wait_recv()
            for j, chip in enumerate(chips):
                copy(i, 4 + j, (*chip, 1 - c), me).wait_recv()
        for cp in first + passed:
            cp.wait_send()

    return pl.pallas_call(
        body, name=name,
        out_shape=[jax.ShapeDtypeStruct((N_DEV,) + s.shape, dt) for s, dt in zip(shards, out_dtypes)],
        in_specs=[_vmem()] * n, out_specs=[_vmem()] * n,
        scratch_shapes=[pltpu.SemaphoreType.DMA((7 * n,)), pltpu.SemaphoreType.DMA((7 * n,))],
        compiler_params=_params(vmem=VMEM_LIMIT_V7X),
    )(*shards)


def _row_chunks(rows):
    chunk = 64 if rows % 64 == 0 else rows
    return chunk, rows // chunk


def _reduce_scatter(parts, name):
    n = len(parts)

    def body(*refs):
        in_refs, out_refs, recv_refs = refs[:n], refs[n:2 * n], refs[2 * n:3 * n]
        send_sems, recv_sems = refs[3 * n], refs[3 * n + 1]
        x, y, c = lax.axis_index("x"), lax.axis_index("y"), lax.axis_index("c")
        my = 4 * x + 2 * y + c
        copies = []
        for i in range(n):
            for m in range(1, N_DEV):
                px = 1 - x if m & 4 else x
                py = 1 - y if m & 2 else y
                pc = 1 - c if m & 1 else c
                cp = pltpu.make_async_remote_copy(
                    src_ref=in_refs[i].at[4 * px + 2 * py + pc], dst_ref=recv_refs[i].at[m - 1],
                    send_sem=send_sems.at[7 * i + m - 1], recv_sem=recv_sems.at[7 * i + m - 1],
                    device_id=(px, py, pc), device_id_type=MESH)
                cp.start()
                copies.append(cp)
        for cp in copies:
            cp.wait_recv()
        for i in range(n):
            rows = parts[i].shape[1]
            chunk, steps = _row_chunks(rows)

            def step(s, carry, i=i, chunk=chunk):
                r = pl.ds(pl.multiple_of(s * chunk, chunk), chunk)
                acc = in_refs[i][my, r, :].astype(F32)
                for m in range(1, N_DEV):
                    acc = acc + recv_refs[i][m - 1, r, :].astype(F32)
                out_refs[i][r, :] = acc
                return carry

            lax.fori_loop(0, steps, step, 0)
        for cp in copies:
            cp.wait_send()

    return pl.pallas_call(
        body, name=name,
        out_shape=[jax.ShapeDtypeStruct(p.shape[1:], F32) for p in parts],
        in_specs=[_vmem()] * n, out_specs=[_vmem()] * n,
        scratch_shapes=[pltpu.VMEM((N_DEV - 1,) + p.shape[1:], p.dtype) for p in parts]
        + [pltpu.SemaphoreType.DMA((7 * n,)), pltpu.SemaphoreType.DMA((7 * n,))],
        compiler_params=_params(vmem=VMEM_LIMIT_V7X),
    )(*parts)


def _adamw_math(w, g, m, v):
    m = ADAM_B1 * m + (1.0 - ADAM_B1) * g
    v = ADAM_B2 * v + (1.0 - ADAM_B2) * (g * g)
    m_hat = m / (1.0 - ADAM_B1 ** ADAM_STEP)
    v_hat = v / (1.0 - ADAM_B2 ** ADAM_STEP)
    delta = -ADAM_LR * (m_hat / (jnp.sqrt(v_hat) + ADAM_EPS) + ADAM_WD * w)
    return delta, m, v


def _adamw_shards(gs, ws, ms, vs):
    n = len(gs)

    def body(*refs):
        g_refs, w_refs, m_refs, v_refs = (refs[k * n:(k + 1) * n] for k in range(4))
        d_out, m_out, v_out = (refs[(4 + k) * n:(5 + k) * n] for k in range(3))
        for i in range(n):
            chunk, steps = _row_chunks(gs[i].shape[0])

            def step(s, carry, i=i, chunk=chunk):
                r = pl.ds(pl.multiple_of(s * chunk, chunk), chunk)
                d, m, v = _adamw_math(w_refs[i][r, :], g_refs[i][r, :], m_refs[i][r, :], v_refs[i][r, :])
                d_out[i][r, :] = d
                m_out[i][r, :] = m
                v_out[i][r, :] = v
                return carry

            lax.fori_loop(0, steps, step, 0)

    shapes = [jax.ShapeDtypeStruct(g.shape, F32) for g in gs]
    outs = pl.pallas_call(
        body, name="adamw_shards", out_shape=shapes * 3,
        in_specs=[_vmem()] * (4 * n), out_specs=[_vmem()] * (3 * n),
        compiler_params=_params(vmem=VMEM_LIMIT_V7X),
    )(*gs, *ws, *ms, *vs)
    return outs[:n], outs[n:2 * n], outs[2 * n:]


def _small_update(gathered, w, m, v):
    rows = w.shape[0]
    chunk, steps = _row_chunks(rows)

    def body(ga_ref, w_ref, m_ref, v_ref, g_out, d_out, m_out, v_out):
        def step(s, carry):
            r = pl.ds(pl.multiple_of(s * chunk, chunk), chunk)
            g = ga_ref[0, r, :]
            for j in range(1, N_DEV):
                g = g + ga_ref[j, r, :]
            d, mm, vv = _adamw_math(w_ref[r, :], g, m_ref[r, :], v_ref[r, :])
            g_out[r, :] = g
            d_out[r, :] = d
            m_out[r, :] = mm
            v_out[r, :] = vv
            return carry

        lax.fori_loop(0, steps, step, 0)

    return pl.pallas_call(
        body, name="small_update", out_shape=[jax.ShapeDtypeStruct(w.shape, F32)] * 4,
        in_specs=[_vmem()] * 4, out_specs=[_vmem()] * 4,
    )(gathered, w, m, v)


def _zoh(lr, li, logdt, btr, bti):
    dt = jnp.exp(logdt)
    mag = jnp.exp(lr * dt)
    th = li * dt
    ar = mag * jnp.cos(th)
    ai = mag * jnp.sin(th)
    den = lr * lr + li * li
    nr = ar - 1.0
    cr = (nr * lr + ai * li) / den
    ci = (ai * lr - nr * li) / den
    return ar, ai, cr * btr - ci * bti, cr * bti + ci * btr


def _zoh_fwd(lr, li, logdt, btr, bti):
    def body(lr_ref, li_ref, dt_ref, br_ref, bi_ref, ar_ref, ai_ref, bbr_ref, bbi_ref):
        ar, ai, bbr, bbi = _zoh(lr_ref[...], li_ref[...], dt_ref[...], br_ref[...], bi_ref[...])
        ar_ref[...] = ar
        ai_ref[...] = ai
        bbr_ref[...] = bbr
        bbi_ref[...] = bbi

    s = jax.ShapeDtypeStruct
    return pl.pallas_call(
        body, name="zoh_fwd",
        out_shape=[s(lr.shape, F32), s(lr.shape, F32), s(btr.shape, F32), s(btr.shape, F32)],
        in_specs=[_vmem()] * 5, out_specs=[_vmem()] * 4,
    )(lr, li, logdt, btr, bti)


def _zoh_bwd(lr, li, logdt, btr, bti, dar, dai, dbbr, dbbi):
    def body(lr_ref, li_ref, dt_ref, br_ref, bi_ref, dar_ref, dai_ref, dbbr_ref, dbbi_ref,
             glr_ref, gli_ref, gdt_ref, gbr_ref, gbi_ref):
        _, vjp = jax.vjp(_zoh, lr_ref[...], li_ref[...], dt_ref[...], br_ref[...], bi_ref[...])
        glr, gli, gdt, gbr, gbi = vjp((dar_ref[...], dai_ref[...], dbbr_ref[...], dbbi_ref[...]))
        glr_ref[...] = glr
        gli_ref[...] = gli
        gdt_ref[...] = gdt
        gbr_ref[...] = gbr
        gbi_ref[...] = gbi

    s = jax.ShapeDtypeStruct
    return pl.pallas_call(
        body, name="zoh_bwd",
        out_shape=[s(lr.shape, F32), s(lr.shape, F32), s(logdt.shape, F32), s(btr.shape, F32), s(btr.shape, F32)],
        in_specs=[_vmem()] * 9, out_specs=[_vmem()] * 5,
    )(lr, li, logdt, btr, bti, dar, dai, dbbr, dbbi)


def _blockdiag(t):
    g, r, s = t.shape
    t = t.reshape(4, 8, r, s)
    out = jnp.einsum("jirs,ik->jirks", t, jnp.eye(8, dtype=t.dtype))
    return out.reshape(4, 8 * r, 8 * s)


def _blockdiag_extract(m, r, s):
    m = m.reshape(4, 8, r, 8, s)
    out = jnp.einsum("jirks,ik->jirs", m, jnp.eye(8, dtype=m.dtype))
    return out.reshape(32, r, s)


def _head_ones():
    r = jnp.arange(ATTN_W) // HEAD_DIM
    return (r[:, None] == r[None, :]).astype(F32)


def _in_proj(x2, g_mix, w_in_g, ones_bd, gq_t, gk_t):
    t_tok = x2.shape[0]
    tm = min(512, t_tok)

    def body(x_ref, g_ref, w_ref, ones_ref, gq_ref, gk_ref, z_ref, qh_ref, kh_ref, vb_ref, xn_ref):
        x = x_ref[...]
        r = lax.rsqrt(jnp.mean(x * x, axis=-1, keepdims=True) + EPS)
        xn = (x * r * g_ref[...]).astype(BF16)
        xn_ref[...] = xn
        for j in range(N_DEV):
            z_ref[:, j * COL_W:(j + 1) * COL_W] = _dot(xn, w_ref[j])
        ones = ones_ref[...]
        q = z_ref[:, 0:ATTN_W]
        rq = lax.rsqrt(_hdot(q * q, ones) * (1.0 / HEAD_DIM) + EPS)
        qh_ref[...] = (q * rq * gq_ref[...] * (HEAD_DIM ** -0.5)).astype(BF16)
        k = z_ref[:, ATTN_W:2 * ATTN_W]
        rk = lax.rsqrt(_hdot(k * k, ones) * (1.0 / HEAD_DIM) + EPS)
        kh_ref[...] = (k * rk * gk_ref[...]).astype(BF16)
        vb_ref[...] = z_ref[:, 2 * ATTN_W:3 * ATTN_W].astype(BF16)

    row = lambda i: (i, 0)
    s = jax.ShapeDtypeStruct
    return pl.pallas_call(
        body, name="in_proj", grid=(t_tok // tm,),
        out_shape=[s((t_tok, IN_W), F32), s((t_tok, ATTN_W), BF16), s((t_tok, ATTN_W), BF16),
                   s((t_tok, ATTN_W), BF16), s((t_tok, D_MODEL), BF16)],
        in_specs=[pl.BlockSpec((tm, D_MODEL), row), _full(g_mix.shape), _full(w_in_g.shape), _full(ones_bd.shape),
                  _full(gq_t.shape), _full(gk_t.shape)],
        out_specs=[pl.BlockSpec((tm, IN_W), row), pl.BlockSpec((tm, ATTN_W), row), pl.BlockSpec((tm, ATTN_W), row),
                   pl.BlockSpec((tm, ATTN_W), row), pl.BlockSpec((tm, D_MODEL), row)],
        compiler_params=_params(1, VMEM_LIMIT_V7X),
    )(x2, g_mix, w_in_g, ones_bd, gq_t, gk_t)


def _multiplicity(q0, tq, seq):
    row = q0 + lax.broadcasted_iota(jnp.int32, (tq, seq), 0)
    col = lax.broadcasted_iota(jnp.int32, (tq, seq), 1)
    d = row - col
    mult = jnp.zeros((tq, seq), F32)
    for window, dil in DILATED:
        hit = (d <= window) & ((d & (dil - 1)) == 0)
        mult = mult + jnp.where(hit, 1.0, 0.0)
    return jnp.where(d >= 0, mult, 0.0)


def _softmax_rows(s, mult):
    m = jnp.max(jnp.where(mult > 0.0, s, -1e30), axis=-1, keepdims=True)
    p = mult * jnp.exp(jnp.minimum(s - m, 0.0))
    return p, jnp.sum(p, axis=-1, keepdims=True)


def _attn_fwd(qh, kh, vb, z, nb, seq):
    t_tok = nb * seq
    tq = min(256, seq)
    nq = seq // tq

    def body(q_ref, k_ref, v_ref, ga_ref, o_ref, ag_ref):
        mult = _multiplicity(pl.program_id(2) * tq, tq, seq)
        lane = lax.broadcasted_iota(jnp.int32, (1, LANES), 1)
        q, k, v = q_ref[...], k_ref[...], v_ref[...]
        o = jnp.zeros((tq, LANES), F32)
        for h in range(LANES // HEAD_DIM):
            lm = (lane // HEAD_DIM) == h
            s = _dot_nt(jnp.where(lm, q, jnp.zeros_like(q)), k)
            p, den = _softmax_rows(s, mult)
            o = jnp.where(lm, _dot(p.astype(BF16), v) / den, o)
        o_ref[...] = o
        ga = ga_ref[...]
        ag_ref[...] = (o * ga * _sig(ga)).astype(BF16)

    qmap = lambda b, hp, qi: (b * nq + qi, hp)
    kmap = lambda b, hp, qi: (b, hp)
    gmap = lambda b, hp, qi: (b * nq + qi, 3 * ATTN_W // LANES + hp)
    s = jax.ShapeDtypeStruct
    return pl.pallas_call(
        body, name="attn_fwd", grid=(nb, ATTN_W // LANES, nq),
        out_shape=[s((t_tok, ATTN_W), F32), s((t_tok, ATTN_W), BF16)],
        in_specs=[pl.BlockSpec((tq, LANES), qmap), pl.BlockSpec((seq, LANES), kmap), pl.BlockSpec((seq, LANES), kmap),
                  pl.BlockSpec((tq, LANES), gmap)],
        out_specs=[pl.BlockSpec((tq, LANES), qmap), pl.BlockSpec((tq, LANES), qmap)],
        compiler_params=_params(3, VMEM_LIMIT_V7X),
    )(qh, kh, vb, z)


SCAN_COLS = 256


def _scan_chunk(re_ref, im_ref, a_re_ref, a_im_ref, carry_re, carry_im, rows, reverse, visit=None):
    nblk = rows // SUBLANES
    rowi = lax.broadcasted_iota(jnp.int32, (SUBLANES, SCAN_COLS), 0)
    edge = (SUBLANES - 1) if reverse else 0
    at_edge = rowi == edge

    def cmul(ar, ai, br, bi):
        return ar * br - ai * bi, ar * bi + ai * br

    for c0 in range(0, N_STATE, SCAN_COLS):
        cols = slice(c0, c0 + SCAN_COLS)
        a1r = jnp.broadcast_to(a_re_ref[:, cols], (SUBLANES, SCAN_COLS))
        a1i = jnp.broadcast_to(a_im_ref[:, cols], (SUBLANES, SCAN_COLS))
        if reverse:
            a1i = -a1i
        a2r, a2i = cmul(a1r, a1i, a1r, a1i)
        a4r, a4i = cmul(a2r, a2i, a2r, a2i)
        steps = []
        for sft, (pr, pi) in ((1, (a1r, a1i)), (2, (a2r, a2i)), (4, (a4r, a4i))):
            if reverse:
                steps.append((SUBLANES - sft, rowi < SUBLANES - sft, pr, pi))
            else:
                steps.append((sft, rowi >= sft, pr, pi))

        def block(i, carry, cols=cols, c0=c0, a1r=a1r, a1i=a1i, steps=steps):
            cr, ci, acc = carry
            blk = (nblk - 1 - i) if reverse else i
            off = pl.multiple_of(blk * SUBLANES, SUBLANES)
            r = re_ref[pl.ds(off, SUBLANES), cols]
            im = im_ref[pl.ds(off, SUBLANES), cols]
            inr, ini = cmul(a1r, a1i, cr, ci)
            r = r + jnp.where(at_edge, inr, 0.0)
            im = im + jnp.where(at_edge, ini, 0.0)
            for sft, keep, pr, pi in steps:
                rs = jnp.where(keep, pltpu.roll(r, sft, 0), 0.0)
                ims = jnp.where(keep, pltpu.roll(im, sft, 0), 0.0)
                dr, di = cmul(pr, pi, rs, ims)
                r, im = r + dr, im + di
            re_ref[pl.ds(off, SUBLANES), cols] = r
            im_ref[pl.ds(off, SUBLANES), cols] = im
            last = 0 if reverse else SUBLANES - 1
            cr = jnp.broadcast_to(r[last:last + 1, :], (SUBLANES, SCAN_COLS))
            ci = jnp.broadcast_to(im[last:last + 1, :], (SUBLANES, SCAN_COLS))
            if visit is not None:
                acc = visit(cols, blk, off, r, im, acc)
            return cr, ci, acc

        acc0 = (jnp.zeros((SUBLANES, SCAN_COLS), F32),) * 2
        cr, ci, acc = lax.fori_loop(0, nblk, block, (carry_re[:, cols], carry_im[:, cols], acc0))
        carry_re[:, cols] = cr
        carry_im[:, cols] = ci
        if visit is not None:
            visit(cols, None, None, None, None, acc)


def _ssm_fwd(z, a_re, a_im, bb_re, bb_im, cc_re, cc_im, d_skip, w_glu, b_glu, nb, seq):
    t_tok = nb * seq
    tc = min(256, seq)
    nch = seq // tc
    grp = N_STATE // 4

    def body(u_ref, gs_ref, ar_ref, ai_ref, bbr_ref, bbi_ref, ccr_ref, cci_ref, d_ref, wg_ref, bg_ref,
             xr_ref, xi_ref, y_ref, sg_ref, car_re, car_im):
        @pl.when(pl.program_id(1) == 0)
        def _():
            car_re[...] = jnp.zeros_like(car_re)
            car_im[...] = jnp.zeros_like(car_im)

        u = u_ref[...]
        ub = u.astype(BF16)
        for j in range(4):
            uj = ub[:, j * LANES:(j + 1) * LANES]
            xr_ref[:, j * grp:(j + 1) * grp] = _dot(uj, bbr_ref[j])
            xi_ref[:, j * grp:(j + 1) * grp] = _dot(uj, bbi_ref[j])
        _scan_chunk(xr_ref, xi_ref, ar_ref, ai_ref, car_re, car_im, tc, reverse=False)
        for j in range(4):
            xr = xr_ref[:, j * grp:(j + 1) * grp].astype(BF16)
            xi = xi_ref[:, j * grp:(j + 1) * grp].astype(BF16)
            y_ref[:, j * LANES:(j + 1) * LANES] = _dot(xr, ccr_ref[j]) - _dot(xi, cci_ref[j])
        y = y_ref[...] + d_ref[...] * u
        y_ref[...] = y
        yg, _ = _gelu_and_grad(y)
        gl = _dot(yg.astype(BF16), wg_ref[...]) + bg_ref[...]
        gs = gs_ref[...]
        sg_ref[...] = (yg * _sig(gl) * gs * _sig(gs)).astype(BF16)

    umap = lambda b, ch: (b * nch + ch, 4)
    gmap = lambda b, ch: (b * nch + ch, 5)
    row = lambda b, ch: (b * nch + ch, 0)
    s = jax.ShapeDtypeStruct
    consts = [a_re, a_im, bb_re, bb_im, cc_re, cc_im, d_skip, w_glu, b_glu]
    return pl.pallas_call(
        body, name="ssm_fwd", grid=(nb, nch),
        out_shape=[s((t_tok, N_STATE), F32), s((t_tok, N_STATE), F32), s((t_tok, SSM_W), F32),
                   s((t_tok, SSM_W), BF16)],
        in_specs=[pl.BlockSpec((tc, SSM_W), umap), pl.BlockSpec((tc, SSM_W), gmap)] + [_full(c.shape) for c in consts],
        out_specs=[pl.BlockSpec((tc, N_STATE), row), pl.BlockSpec((tc, N_STATE), row),
                   pl.BlockSpec((tc, SSM_W), row), pl.BlockSpec((tc, SSM_W), row)],
        scratch_shapes=[pltpu.VMEM((SUBLANES, N_STATE), F32), pltpu.VMEM((SUBLANES, N_STATE), F32)],
        compiler_params=_params(2, VMEM_LIMIT_V7X),
    )(z, z, *consts)


def _tail(x2, tg2, ag, sg, p2, w_out, w_g, w_p, g_ple):
    t_tok = x2.shape[0]
    tm = min(256, t_tok)
    nt = t_tok // tm
    half = ATTN_W

    def body(x_ref, tg_ref, ag_ref, sg_ref, p_ref, wo_ref, wg_ref, wp_ref, gp_ref,
             dmix_ref, dh1_ref, loss_ref, dgp_ref, dwo_ref, dwg_ref, dwp_ref, acc_o, acc_g, acc_p):
        i = pl.program_id(0)

        @pl.when(i == 0)
        def _():
            loss_ref[...] = jnp.zeros_like(loss_ref)
            dgp_ref[...] = jnp.zeros_like(dgp_ref)
            acc_o[...] = jnp.zeros_like(acc_o)
            acc_g[...] = jnp.zeros_like(acc_g)
            acc_p[...] = jnp.zeros_like(acc_p)

        ag_t, sg_t = ag_ref[...], sg_ref[...]
        h1 = x_ref[...] + _dot(ag_t, wo_ref[0:half, :]) + _dot(sg_t, wo_ref[half:2 * half, :])
        r2 = lax.rsqrt(jnp.mean(h1 * h1, axis=-1, keepdims=True) + EPS)
        hnorm = h1 * r2
        gp = gp_ref[...]
        hn = (hnorm * gp).astype(BF16)
        gate = _sig(_dot(hn, wg_ref[...]))
        pb = p_ref[...].astype(BF16)
        pp = jnp.concatenate([_dot(pb, wp_ref[j]) for j in range(N_DEV)], axis=-1)
        h2 = h1 + gate * pp
        err = h2 - tg_ref[...]
        loss_ref[...] += 0.5 * jnp.sum(err * err) * (1.0 / D_MODEL)
        dh2 = err * (1.0 / D_MODEL)
        dpp = (dh2 * gate).astype(BF16)
        dgpre = (dh2 * pp * gate * (1.0 - gate)).astype(BF16)
        acc_p[...] += _dot_tn(pb, dpp)
        acc_g[...] += _dot_tn(hn, dgpre)
        dhn = _dot_nt(dgpre, wg_ref[...])
        dgp_ref[...] += jnp.sum(dhn * hnorm, axis=0, keepdims=True)
        a = dhn * gp
        dh1 = dh2 + r2 * (a - hnorm * jnp.mean(a * hnorm, axis=-1, keepdims=True))
        dh1_ref[...] = dh1
        dh1b = dh1.astype(BF16)
        acc_o[0:half, :] += _dot_tn(ag_t, dh1b)
        acc_o[half:2 * half, :] += _dot_tn(sg_t, dh1b)
        dmix_ref[...] = _dot_nt(dh1b, wo_ref[...])

        @pl.when(i == nt - 1)
        def _():
            dwo_ref[...] = acc_o[...].astype(BF16)
            dwg_ref[...] = acc_g[...].astype(BF16)
            for j in range(N_DEV):
                dwp_ref[j] = acc_p[:, j * LANES:(j + 1) * LANES].astype(BF16)

    row = lambda i: (i, 0)
    s = jax.ShapeDtypeStruct
    return pl.pallas_call(
        body, name="tail_fwd_bwd", grid=(nt,),
        out_shape=[s((t_tok, D_MODEL), F32), s((t_tok, D_MODEL), F32), s((SUBLANES, LANES), F32),
                   s((1, D_MODEL), F32), s((D_MODEL, D_MODEL), BF16), s((D_MODEL, D_MODEL), BF16),
                   s((N_DEV, PLE_DIM, LANES), BF16)],
        in_specs=[pl.BlockSpec((tm, D_MODEL), row), pl.BlockSpec((tm, D_MODEL), row),
                  pl.BlockSpec((tm, half), row), pl.BlockSpec((tm, half), row), pl.BlockSpec((tm, PLE_DIM), row),
                  _full(w_out.shape), _full(w_g.shape), _full(w_p.shape), _full(g_ple.shape)],
        out_specs=[pl.BlockSpec((tm, D_MODEL), row), pl.BlockSpec((tm, D_MODEL), row), _full((SUBLANES, LANES)),
                   _full((1, D_MODEL)), _full((D_MODEL, D_MODEL)), _full((D_MODEL, D_MODEL)),
                   _full((N_DEV, PLE_DIM, LANES))],
        scratch_shapes=[pltpu.VMEM((D_MODEL, D_MODEL), F32), pltpu.VMEM((D_MODEL, D_MODEL), F32),
                        pltpu.VMEM((PLE_DIM, D_MODEL), F32)],
        compiler_params=_params(1, VMEM_LIMIT_V7X),
    )(x2, tg2, ag, sg, p2, w_out, w_g, w_p, g_ple)


def _attn_bwd(qh, kh, vb, z, o, dmix, nb, seq):
    t_tok = nb * seq
    tq = min(256, seq)
    nq = seq // tq

    def body(q_ref, k_ref, v_ref, ga_ref, o_ref, da_ref, dq_ref, dk_ref, dv_ref, dga_ref):
        @pl.when(pl.program_id(2) == 0)
        def _():
            dk_ref[...] = jnp.zeros_like(dk_ref)
            dv_ref[...] = jnp.zeros_like(dv_ref)

        mult = _multiplicity(pl.program_id(2) * tq, tq, seq)
        lane = lax.broadcasted_iota(jnp.int32, (1, LANES), 1)
        q, k, v = q_ref[...], k_ref[...], v_ref[...]
        ga, o_t, da = ga_ref[...], o_ref[...], da_ref[...]
        sga = _sig(ga)
        d_o = da * ga * sga
        dga_ref[...] = da * o_t * sga * (1.0 + ga * (1.0 - sga))
        d_oo = d_o * o_t
        dq = jnp.zeros((tq, LANES), F32)
        for h in range(LANES // HEAD_DIM):
            lm = (lane // HEAD_DIM) == h
            qm = jnp.where(lm, q, jnp.zeros_like(q))
            p, den = _softmax_rows(_dot_nt(qm, k), mult)
            pn = p / den
            dom = jnp.where(lm, d_o, 0.0).astype(BF16)
            dv_ref[...] += _dot_tn(pn.astype(BF16), dom)
            delta = jnp.sum(jnp.where(lm, d_oo, 0.0), axis=-1, keepdims=True)
            ds = (pn * (_dot_nt(dom, v) - delta)).astype(BF16)
            dq = jnp.where(lm, _dot(ds, k), dq)
            dk_ref[...] += _dot_tn(ds, qm)
        dq_ref[...] = dq

    qmap = lambda b, hp, qi: (b * nq + qi, hp)
    kmap = lambda b, hp, qi: (b, hp)
    gmap = lambda b, hp, qi: (b * nq + qi, 3 * ATTN_W // LANES + hp)
    s = jax.ShapeDtypeStruct
    return pl.pallas_call(
        body, name="attn_bwd", grid=(nb, ATTN_W // LANES, nq),
        out_shape=[s((t_tok, ATTN_W), F32)] * 4,
        in_specs=[pl.BlockSpec((tq, LANES), qmap), pl.BlockSpec((seq, LANES), kmap), pl.BlockSpec((seq, LANES), kmap),
                  pl.BlockSpec((tq, LANES), gmap), pl.BlockSpec((tq, LANES), qmap), pl.BlockSpec((tq, LANES), qmap)],
        out_specs=[pl.BlockSpec((tq, LANES), qmap), pl.BlockSpec((seq, LANES), kmap), pl.BlockSpec((seq, LANES), kmap),
                   pl.BlockSpec((tq, LANES), qmap)],
        compiler_params=_params(3, VMEM_LIMIT_V7X),
    )(qh, kh, vb, z, o, dmix)


def _ssm_bwd(z, dmix, y, x_re, x_im, a_re, a_im, bb_re, bb_im, cc_re, cc_im, d_skip, w_glu, b_glu, nb, seq):
    t_tok = nb * seq
    tc = min(256, seq)
    nch = seq // tc
    grp = N_STATE // 4

    def body(u_ref, gs_ref, ds_ref, y_ref, xr_ref, xi_ref, xpr_ref, xpi_ref,
             ar_ref, ai_ref, bbr_ref, bbi_ref, ccr_ref, cci_ref, d_ref, wg_ref, bg_ref,
             du_ref, dgs_ref, dwg_ref, dbg_ref, dd_ref, dar_ref, dai_ref, dbbr_ref, dbbi_ref, dccr_ref, dcci_ref,
             lam_re, lam_im, car_re, car_im, acc_wg):
        step = pl.program_id(1)
        first_chunk = step == nch - 1

        @pl.when((pl.program_id(0) == 0) & (step == 0))
        def _():
            acc_wg[...] = jnp.zeros_like(acc_wg)
            for ref in (dbg_ref, dd_ref, dar_ref, dai_ref, dbbr_ref, dbbi_ref, dccr_ref, dcci_ref):
                ref[...] = jnp.zeros_like(ref)

        @pl.when(step == 0)
        def _():
            car_re[...] = jnp.zeros_like(car_re)
            car_im[...] = jnp.zeros_like(car_im)

        u, gs, dssm, y = u_ref[...], gs_ref[...], ds_ref[...], y_ref[...]
        yg, dgelu = _gelu_and_grad(y)
        ygb = yg.astype(BF16)
        sgl = _sig(_dot(ygb, wg_ref[...]) + bg_ref[...])
        sgs = _sig(gs)
        dout = dssm * gs * sgs
        dgs_ref[...] = dssm * yg * sgl * sgs * (1.0 + gs * (1.0 - sgs))
        dgl = dout * yg * sgl * (1.0 - sgl)
        dglb = dgl.astype(BF16)
        dyg = dout * sgl + _dot_nt(dglb, wg_ref[...])
        acc_wg[...] += _dot_tn(ygb, dglb)
        dbg_ref[...] += jnp.sum(dgl, axis=0, keepdims=True)
        dy = dyg * dgelu
        dd_ref[...] += jnp.sum(dy * u, axis=0, keepdims=True)
        dyb = dy.astype(BF16)
        ub = u.astype(BF16)
        for j in range(4):
            dyj = dyb[:, j * LANES:(j + 1) * LANES]
            sl = slice(j * grp, (j + 1) * grp)
            lam_re[:, sl] = _dot_nt(dyj, ccr_ref[j])
            lam_im[:, sl] = -_dot_nt(dyj, cci_ref[j])
            dccr_ref[j] += _dot_tn(xr_ref[:, sl].astype(BF16), dyj)
            dcci_ref[j] -= _dot_tn(xi_ref[:, sl].astype(BF16), dyj)

        keep_prev = jnp.where(first_chunk, 0.0, 1.0)
        rowi = lax.broadcasted_iota(jnp.int32, (SUBLANES, SCAN_COLS), 0)

        def visit(cols, blk, off, lr, li, acc):
            if blk is None:
                dar_ref[:, cols] += jnp.sum(acc[0], axis=0, keepdims=True)
                dai_ref[:, cols] += jnp.sum(acc[1], axis=0, keepdims=True)
                return None
            xr = xr_ref[pl.ds(off, SUBLANES), cols]
            xi = xi_ref[pl.ds(off, SUBLANES), cols]
            poff = pl.multiple_of(jnp.maximum(blk - 1, 0) * SUBLANES, SUBLANES)
            inside = blk > 0
            pr = jnp.where(inside, xr_ref[pl.ds(poff, SUBLANES), cols], xpr_ref[:, cols] * keep_prev)
            pi = jnp.where(inside, xi_ref[pl.ds(poff, SUBLANES), cols], xpi_ref[:, cols] * keep_prev)
            last = SUBLANES - 1
            pr = jnp.broadcast_to(pr[last:last + 1, :], (SUBLANES, SCAN_COLS))
            pi = jnp.broadcast_to(pi[last:last + 1, :], (SUBLANES, SCAN_COLS))
            xpr = jnp.where(rowi == 0, pr, pltpu.roll(xr, 1, 0))
            xpi = jnp.where(rowi == 0, pi, pltpu.roll(xi, 1, 0))
            return acc[0] + lr * xpr + li * xpi, acc[1] + li * xpr - lr * xpi

        _scan_chunk(lam_re, lam_im, ar_ref, ai_ref, car_re, car_im, tc, reverse=True, visit=visit)

        for j in range(4):
            sl = slice(j * grp, (j + 1) * grp)
            lr = lam_re[:, sl].astype(BF16)
            li = lam_im[:, sl].astype(BF16)
            uj = ub[:, j * LANES:(j + 1) * LANES]
            du_ref[:, j * LANES:(j + 1) * LANES] = (
                _dot_nt(lr, bbr_ref[j]) + _dot_nt(li, bbi_ref[j])
                + dy[:, j * LANES:(j + 1) * LANES] * d_ref[:, j * LANES:(j + 1) * LANES])
            dbbr_ref[j] += _dot_tn(uj, lr)
            dbbi_ref[j] += _dot_tn(uj, li)

        @pl.when((pl.program_id(0) == nb - 1) & (step == nch - 1))
        def _():
            dwg_ref[...] = acc_wg[...].astype(BF16)

    rev = lambda b, ch: b * nch + (nch - 1 - ch)
    umap = lambda b, ch: (rev(b, ch), 4)
    gmap = lambda b, ch: (rev(b, ch), 5)
    smap = lambda b, ch: (rev(b, ch), 1)
    row = lambda b, ch: (rev(b, ch), 0)
    prev = lambda b, ch: (jnp.maximum(rev(b, ch) * (tc // SUBLANES) - 1, 0), 0)
    s = jax.ShapeDtypeStruct
    consts = [a_re, a_im, bb_re, bb_im, cc_re, cc_im, d_skip, w_glu, b_glu]
    acc_shapes = [s((1, SSM_W), F32), s((1, SSM_W), F32), s((1, N_STATE), F32), s((1, N_STATE), F32),
                  s(bb_re.shape, F32), s(bb_re.shape, F32), s(cc_re.shape, F32), s(cc_re.shape, F32)]
    return pl.pallas_call(
        body, name="ssm_bwd", grid=(nb, nch),
        out_shape=[s((t_tok, SSM_W), F32), s((t_tok, SSM_W), F32), s((SSM_W, SSM_W), BF16)] + acc_shapes,
        in_specs=[pl.BlockSpec((tc, SSM_W), umap), pl.BlockSpec((tc, SSM_W), gmap), pl.BlockSpec((tc, SSM_W), smap),
                  pl.BlockSpec((tc, SSM_W), row), pl.BlockSpec((tc, N_STATE), row), pl.BlockSpec((tc, N_STATE), row),
                  pl.BlockSpec((SUBLANES, N_STATE), prev), pl.BlockSpec((SUBLANES, N_STATE), prev)]
        + [_full(c.shape) for c in consts],
        out_specs=[pl.BlockSpec((tc, SSM_W), row), pl.BlockSpec((tc, SSM_W), row), _full((SSM_W, SSM_W))]
        + [_full(a.shape) for a in acc_shapes],
        scratch_shapes=[pltpu.VMEM((tc, N_STATE), F32), pltpu.VMEM((tc, N_STATE), F32),
                        pltpu.VMEM((SUBLANES, N_STATE), F32), pltpu.VMEM((SUBLANES, N_STATE), F32),
                        pltpu.VMEM((SSM_W, SSM_W), F32)],
        compiler_params=_params(2, VMEM_LIMIT_V7X),
    )(z, z, dmix, y, x_re, x_im, x_re, x_im, *consts)


def _dz_and_dx(x2, z, dqh, dkh, dvb, dga, du, dgs, dh1, w_in_g, g_mix, gq_t, gk_t, ones_bd, fold):
    t_tok = x2.shape[0]
    tm = min(256, t_tok)
    nt = t_tok // tm
    a_w = ATTN_W

    def head_norm_bwd(raw, d_hat, gain, scale, ones):
        r = lax.rsqrt(_hdot(raw * raw, ones) * (1.0 / HEAD_DIM) + EPS)
        n = raw * r
        a = d_hat * gain * scale
        d_raw = r * (a - n * (_hdot(a * n, ones) * (1.0 / HEAD_DIM)))
        return d_raw, jnp.sum(d_hat * n * scale, axis=0, keepdims=True)

    def body(x_ref, q_ref, k_ref, dq_ref, dk_ref, dv_ref, dga_ref, du_ref, dgs_ref, dh1_ref, w_ref, g_ref,
             gq_ref, gk_ref, ones_ref, fold_ref, dz_ref, gx_ref, dgm_ref, dgq_ref, dgk_ref, acc_q, acc_k):
        i = pl.program_id(0)

        @pl.when(i == 0)
        def _():
            dgm_ref[...] = jnp.zeros_like(dgm_ref)
            acc_q[...] = jnp.zeros_like(acc_q)
            acc_k[...] = jnp.zeros_like(acc_k)

        ones = ones_ref[...]
        dq, sq = head_norm_bwd(q_ref[...], dq_ref[...], gq_ref[...], HEAD_DIM ** -0.5, ones)
        dk, sk = head_norm_bwd(k_ref[...], dk_ref[...], gk_ref[...], 1.0, ones)
        acc_q[...] += jnp.broadcast_to(sq, acc_q.shape)
        acc_k[...] += jnp.broadcast_to(sk, acc_k.shape)
        parts = (dq, dk, dv_ref[...], dga_ref[...], du_ref[...], dgs_ref[...])
        for n, part in enumerate(parts):
            dz_ref[:, n * a_w:(n + 1) * a_w] = part.astype(BF16)
        dxn = jnp.zeros((tm, D_MODEL), F32)
        for j in range(N_DEV):
            dxn = dxn + _dot_nt(dz_ref[:, j * COL_W:(j + 1) * COL_W], w_ref[j])
        x = x_ref[...]
        r1 = lax.rsqrt(jnp.mean(x * x, axis=-1, keepdims=True) + EPS)
        xnorm = x * r1
        dgm_ref[...] += jnp.sum(dxn * xnorm, axis=0, keepdims=True)
        a = dxn * g_ref[...]
        gx_ref[...] = dh1_ref[...] + r1 * (a - xnorm * jnp.mean(a * xnorm, axis=-1, keepdims=True))

        @pl.when(i == nt - 1)
        def _():
            dgq_ref[...] = _hdot(acc_q[...], fold_ref[...])
            dgk_ref[...] = _hdot(acc_k[...], fold_ref[...])

    row = lambda i: (i, 0)
    col = lambda n: (lambda i: (i, n))
    s = jax.ShapeDtypeStruct
    half = pl.BlockSpec((tm, a_w), row)
    return pl.pallas_call(
        body, name="dz_dx", grid=(nt,),
        out_shape=[s((t_tok, IN_W), BF16), s((t_tok, D_MODEL), F32), s((1, D_MODEL), F32),
                   s((SUBLANES, HEAD_DIM), F32), s((SUBLANES, HEAD_DIM), F32)],
        in_specs=[pl.BlockSpec((tm, D_MODEL), row), pl.BlockSpec((tm, a_w), col(0)), pl.BlockSpec((tm, a_w), col(1)),
                  half, half, half, half, half, half, pl.BlockSpec((tm, D_MODEL), row),
                  _full(w_in_g.shape), _full(g_mix.shape), _full(gq_t.shape), _full(gk_t.shape),
                  _full(ones_bd.shape), _full(fold.shape)],
        out_specs=[pl.BlockSpec((tm, IN_W), row), pl.BlockSpec((tm, D_MODEL), row), _full((1, D_MODEL)),
                   _full((SUBLANES, HEAD_DIM)), _full((SUBLANES, HEAD_DIM))],
        scratch_shapes=[pltpu.VMEM((SUBLANES, a_w), F32), pltpu.VMEM((SUBLANES, a_w), F32)],
        compiler_params=_params(1, VMEM_LIMIT_V7X),
    )(x2, z, z, dqh, dkh, dvb, dga, du, dgs, dh1, w_in_g, g_mix, gq_t, gk_t, ones_bd, fold)


def _dw_in(xn, dz):
    t_tok = xn.shape[0]
    tk = min(1024, t_tok)
    nk = t_tok // tk

    def body(xn_ref, dz_ref, out_ref, acc):
        k = pl.program_id(1)

        @pl.when(k == 0)
        def _():
            acc[...] = jnp.zeros_like(acc)

        acc[...] += _dot_tn(xn_ref[...], dz_ref[...])

        @pl.when(k == nk - 1)
        def _():
            out_ref[0] = acc[...].astype(BF16)

    return pl.pallas_call(
        body, name="dw_in", grid=(N_DEV, nk),
        out_shape=jax.ShapeDtypeStruct((N_DEV, D_MODEL, COL_W), BF16),
        in_specs=[pl.BlockSpec((tk, D_MODEL), lambda j, k: (k, 0)), pl.BlockSpec((tk, COL_W), lambda j, k: (k, j))],
        out_specs=pl.BlockSpec((1, D_MODEL, COL_W), lambda j, k: (j, 0, 0)),
        scratch_shapes=[pltpu.VMEM((D_MODEL, COL_W), F32)],
        compiler_params=_params(2, VMEM_LIMIT_V7X),
    )(xn, dz)


SMALL = ("mix_norm", "q_norm", "k_norm", "lambda_re", "lambda_im", "log_dt", "b_re", "b_im", "c_re", "c_im",
         "d_skip", "b_glu", "ple_norm")
BIG = ("w_in", "w_glu", "w_out", "w_ple_gate", "w_ple_proj")
WEIGHTS = ("mix_norm", "w_in", "q_norm", "k_norm", "lambda_re", "lambda_im", "log_dt", "b_re", "b_im", "c_re",
           "c_im", "d_skip", "w_glu", "b_glu", "w_out", "ple_norm", "w_ple_gate", "w_ple_proj")


def _pack(arrs):
    flat = jnp.concatenate([a.reshape(-1).astype(F32) for a in arrs])
    rows = -(-flat.shape[0] // (64 * LANES)) * 64
    return jnp.pad(flat, (0, rows * LANES - flat.shape[0])).reshape(rows, LANES)


def _unpack(packed, shapes):
    flat = packed.reshape(-1)
    out, off = [], 0
    for shp in shapes:
        size = math.prod(shp)
        out.append(flat[off:off + size].reshape(shp))
        off += size
    return out


def kernel(x, p, mix_norm, w_in, q_norm, k_norm, lambda_re, lambda_im, log_dt, b_re, b_im, c_re, c_im, d_skip, w_glu, b_glu, w_out, ple_norm, w_ple_gate, w_ple_proj, loss_target, m_mix_norm, m_w_in, m_q_norm, m_k_norm, m_lambda_re, m_lambda_im, m_log_dt, m_b_re, m_b_im, m_c_re, m_c_im, m_d_skip, m_w_glu, m_b_glu, m_w_out, m_ple_norm, m_w_ple_gate, m_w_ple_proj, v_mix_norm, v_w_in, v_q_norm, v_k_norm, v_lambda_re, v_lambda_im, v_log_dt, v_b_re, v_b_im, v_c_re, v_c_im, v_d_skip, v_w_glu, v_b_glu, v_w_out, v_ple_norm, v_w_ple_gate, v_w_ple_proj):
    env = dict(locals())
    w = {n: env[n] for n in WEIGHTS}
    m = {n: env["m_" + n] for n in WEIGHTS}
    v = {n: env["v_" + n] for n in WEIGHTS}
    nb, seq, _ = x.shape
    t_tok = nb * seq
    x2 = x.reshape(t_tok, D_MODEL)
    tg2 = loss_target.reshape(t_tok, D_MODEL)
    p2 = p.reshape(t_tok, PLE_DIM)

    shard2d = {"w_in": (D_MODEL, COL_W), "w_glu": (SSM_W // N_DEV, SSM_W), "w_out": (D_MODEL // N_DEV, D_MODEL),
               "w_ple_gate": (D_MODEL // N_DEV, D_MODEL), "w_ple_proj": (PLE_DIM, D_MODEL // N_DEV)}
    w_sh = [w[n].reshape(shard2d[n]) for n in BIG]
    w_in_g, w_glu_g, w_out_g, w_g_g, w_p_g = _all_gather(w_sh, [BF16] * len(BIG), "gather_weights")
    w_glu_f = w_glu_g.reshape(SSM_W, SSM_W)
    w_out_f = w_out_g.reshape(D_MODEL, D_MODEL)
    w_g_f = w_g_g.reshape(D_MODEL, D_MODEL)

    g3 = (SSM_GROUPS, 1, SSM_STATE)
    lr3, li3 = lambda_re.reshape(g3), lambda_im.reshape(g3)
    dt3 = log_dt.reshape(SSM_GROUPS, 1, 1)
    btr = b_re[0].transpose(0, 2, 1)
    bti = b_im[0].transpose(0, 2, 1)
    a_re3, a_im3, bbr, bbi = _zoh_fwd(lr3, li3, dt3, btr, bti)
    a_re, a_im = a_re3.reshape(1, N_STATE), a_im3.reshape(1, N_STATE)
    bb_re, bb_im = _blockdiag(bbr).astype(BF16), _blockdiag(bbi).astype(BF16)
    cc_re = _blockdiag(c_re[0].transpose(0, 2, 1)).astype(BF16)
    cc_im = _blockdiag(c_im[0].transpose(0, 2, 1)).astype(BF16)

    ones_bd = _head_ones()
    fold = jnp.tile(jnp.eye(HEAD_DIM, dtype=F32), (ATTN_W // HEAD_DIM, 1))
    gq_t = jnp.tile(q_norm, (1, ATTN_W // HEAD_DIM))
    gk_t = jnp.tile(k_norm, (1, ATTN_W // HEAD_DIM))

    z, qh, kh, vb, xn = _in_proj(x2, mix_norm, w_in_g, ones_bd, gq_t, gk_t)
    o, ag = _attn_fwd(qh, kh, vb, z, nb, seq)
    x_re, x_im, y, sg = _ssm_fwd(z, a_re, a_im, bb_re, bb_im, cc_re, cc_im, d_skip, w_glu_f, b_glu, nb, seq)
    dmix, dh1, loss_t, d_ple, dw_out, dw_g, dw_p = _tail(x2, tg2, ag, sg, p2, w_out_f, w_g_f, w_p_g, ple_norm)

    dqh, dkh, dvb, dga = _attn_bwd(qh, kh, vb, z, o, dmix, nb, seq)
    (du, dgs, dw_glu, d_bglu, d_dskip, da_re, da_im, dbb_re, dbb_im, dcc_re, dcc_im) = _ssm_bwd(
        z, dmix, y, x_re, x_im, a_re, a_im, bb_re, bb_im, cc_re, cc_im, d_skip, w_glu_f, b_glu, nb, seq)
    dz, gx, d_mix, d_gq, d_gk = _dz_and_dx(x2, z, dqh, dkh, dvb, dga, du, dgs, dh1, w_in_g, mix_norm, gq_t, gk_t,
                                           ones_bd, fold)
    dw_in = _dw_in(xn, dz)

    d_lr, d_li, d_dt, d_btr, d_bti = _zoh_bwd(
        lr3, li3, dt3, btr, bti, da_re.reshape(g3), da_im.reshape(g3),
        _blockdiag_extract(dbb_re, SSM_GROUP, SSM_STATE), _blockdiag_extract(dbb_im, SSM_GROUP, SSM_STATE))
    small_g = {
        "mix_norm": d_mix, "q_norm": d_gq[0:1], "k_norm": d_gk[0:1], "lambda_re": d_lr, "lambda_im": d_li,
        "log_dt": d_dt, "b_re": d_btr.transpose(0, 2, 1), "b_im": d_bti.transpose(0, 2, 1),
        "c_re": _blockdiag_extract(dcc_re, SSM_STATE, SSM_GROUP).transpose(0, 2, 1),
        "c_im": _blockdiag_extract(dcc_im, SSM_STATE, SSM_GROUP).transpose(0, 2, 1),
        "d_skip": d_dskip, "b_glu": d_bglu, "ple_norm": d_ple}

    parts = [dw_in, dw_glu.reshape(N_DEV, SSM_W // N_DEV, SSM_W), dw_out.reshape(N_DEV, D_MODEL // N_DEV, D_MODEL),
             dw_g.reshape(N_DEV, D_MODEL // N_DEV, D_MODEL), dw_p]
    g_sh = _reduce_scatter(parts, "scatter_grads")
    d_sh, m_sh, v_sh = _adamw_shards(g_sh, w_sh, [m[n].reshape(shard2d[n]) for n in BIG],
                                     [v[n].reshape(shard2d[n]) for n in BIG])

    (gathered,) = _all_gather([_pack([small_g[n] for n in SMALL])], [F32], "gather_small_grads")
    g_pk, d_pk, m_pk, v_pk = _small_update(gathered, _pack([w[n] for n in SMALL]), _pack([m[n] for n in SMALL]),
                                           _pack([v[n] for n in SMALL]))

    grads, deltas, new_m, new_v = {}, {}, {}, {}
    small_shapes = [w[n].shape for n in SMALL]
    for dst, packed in ((grads, g_pk), (deltas, d_pk), (new_m, m_pk), (new_v, v_pk)):
        for n, a in zip(SMALL, _unpack(packed, small_shapes)):
            dst[n] = a
    for i, n in enumerate(BIG):
        grads[n] = g_sh[i].reshape(w[n].shape)
        deltas[n] = d_sh[i].reshape(w[n].shape)
        new_m[n] = m_sh[i].reshape(w[n].shape)
        new_v[n] = v_sh[i].reshape(w[n].shape)

    loss = lax.psum(loss_t[0, 0], AXES)
    return (loss, gx.reshape(x.shape), *[grads[n] for n in WEIGHTS], *[deltas[n] for n in WEIGHTS],
            *[new_m[n] for n in WEIGHTS], *[new_v[n] for n in WEIGHTS])
```

```python
import math

import jax
import jax.numpy as jnp
from jax import lax
from jax.experimental import pallas as pl
from jax.experimental.pallas import tpu as pltpu

F32 = jnp.float32
BF16 = jnp.bfloat16
MESH = pl.DeviceIdType.MESH
AXES = ("x", "y", "c")
N_DEV = 8

D_MODEL = 1024
HEAD_DIM = 64
ATTN_W = 512
SSM_W = 512
SSM_GROUPS = 32
SSM_GROUP = 16
SSM_STATE = 64
N_STATE = SSM_GROUPS * SSM_STATE
PLE_DIM = 256
IN_W = 3072
COL_W = IN_W // N_DEV
DILATED = ((128, 1), (512, 4), (2048, 16))
EPS = 1e-6
INV_SQRT2 = 1.0 / math.sqrt(2.0)
INV_SQRT_2PI = 1.0 / math.sqrt(2.0 * math.pi)

ADAM_LR, ADAM_B1, ADAM_B2, ADAM_EPS, ADAM_WD, ADAM_STEP = 0.001, 0.9, 0.999, 1e-08, 0.01, 10

VMEM_LIMIT_V7X = 56 * 1024 * 1024
SUBLANES = 8
LANES = 128


def _params(n_axes=None, vmem=None):
    kw = {}
    if n_axes:
        kw["dimension_semantics"] = ("arbitrary",) * n_axes
    if vmem:
        kw["vmem_limit_bytes"] = vmem
    return pltpu.CompilerParams(**kw)


def _dot(a, b):
    return jnp.dot(a, b, preferred_element_type=F32)


def _dot_nt(a, b):
    return lax.dot_general(a, b, (((1,), (1,)), ((), ())), preferred_element_type=F32)


def _dot_tn(a, b):
    return lax.dot_general(a, b, (((0,), (0,)), ((), ())), preferred_element_type=F32)


def _hdot(a, b):
    return jnp.dot(a, b, precision=lax.Precision.HIGHEST, preferred_element_type=F32)


def _sig(x):
    return 1.0 / (1.0 + jnp.exp(-x))


def _gelu_and_grad(y):
    cdf = 0.5 * (1.0 + lax.erf(y * INV_SQRT2))
    pdf = jnp.exp(-0.5 * y * y) * INV_SQRT_2PI
    return y * cdf, cdf + y * pdf


def _vmem():
    return pl.BlockSpec(memory_space=pltpu.VMEM)


def _full(shape):
    nd = len(shape)
    return pl.BlockSpec(shape, lambda *_: (0,) * nd)


def _all_gather(shards, out_dtypes, name):
    n = len(shards)

    def body(*refs):
        in_refs, out_refs = refs[:n], refs[n:2 * n]
        send_sems, recv_sems = refs[2 * n], refs[2 * n + 1]
        x, y, c = lax.axis_index("x"), lax.axis_index("y"), lax.axis_index("c")
        me, sibling = (x, y, c), (x, y, 1 - c)
        chips = [(1 - x, y), (x, 1 - y), (1 - x, 1 - y)]

        def idx(px, py, pc):
            return 4 * px + 2 * py + pc

        def copy(i, k, block, to):
            ref = out_refs[i].at[idx(*block)]
            return pltpu.make_async_remote_copy(
                src_ref=ref, dst_ref=ref, send_sem=send_sems.at[7 * i + k], recv_sem=recv_sems.at[7 * i + k],
                device_id=to, device_id_type=MESH)

        for i in range(n):
            out_refs[i][idx(*me)] = in_refs[i][...].astype(out_refs[i].dtype)
        first = []
        for i in range(n):
            first.append(copy(i, 0, me, sibling))
            first += [copy(i, 1 + j, me, (*chip, c)) for j, chip in enumerate(chips)]
        for cp in first:
            cp.start()
        passed = []
        for j, chip in enumerate(chips):
            for i in range(n):
                copy(i, 1 + j, (*chip, c), me).wait_recv()
                cp = copy(i, 4 + j, (*chip, c), sibling)
                cp.start()
                passed.append(cp)
        for i in range(n):
            copy(i, 0, sibling, me).wait_recv()
            for j, chip in enumerate(chips):
                copy(i, 4 + j, (*chip, 1 - c), me).wait_recv()
        for cp in first + passed:
            cp.wait_send()

    return pl.pallas_call(
        body, name=name,
        out_shape=[jax.ShapeDtypeStruct((N_DEV,) + s.shape, dt) for s, dt in zip(shards, out_dtypes)],
        in_specs=[_vmem()] * n, out_specs=[_vmem()] * n,
        scratch_shapes=[pltpu.SemaphoreType.DMA((7 * n,)), pltpu.SemaphoreType.DMA((7 * n,))],
        compiler_params=_params(vmem=VMEM_LIMIT_V7X),
    )(*shards)


def _row_chunks(rows):
    chunk = 64 if rows % 64 == 0 else rows
    return chunk, rows // chunk


def _reduce_scatter(parts, name):
    n = len(parts)

    def body(*refs):
        in_refs, out_refs, recv_refs = refs[:n], refs[n:2 * n], refs[2 * n:3 * n]
        send_sems, recv_sems = refs[3 * n], refs[3 * n + 1]
        x, y, c = lax.axis_index("x"), lax.axis_index("y"), lax.axis_index("c")
        my = 4 * x + 2 * y + c
        copies = []
        for i in range(n):
            for m in range(1, N_DEV):
                px = 1 - x if m & 4 else x
                py = 1 - y if m & 2 else y
                pc = 1 - c if m & 1 else c
                cp = pltpu.make_async_remote_copy(
                    src_ref=in_refs[i].at[4 * px + 2 * py + pc], dst_ref=recv_refs[i].at[m - 1],
                    send_sem=send_sems.at[7 * i + m - 1], recv_sem=recv_sems.at[7 * i + m - 1],
                    device_id=(px, py, pc), device_id_type=MESH)
                cp.start()
                copies.append(cp)
        for cp in copies:
            cp.wait_recv()
        for i in range(n):
            rows = parts[i].shape[1]
            chunk, steps = _row_chunks(rows)

            def step(s, carry, i=i, chunk=chunk):
                r = pl.ds(pl.multiple_of(s * chunk, chunk), chunk)
                acc = in_refs[i][my, r, :].astype(F32)
                for m in range(1, N_DEV):
                    acc = acc + recv_refs[i][m - 1, r, :].astype(F32)
                out_refs[i][r, :] = acc
                return carry

            lax.fori_loop(0, steps, step, 0)
        for cp in copies:
            cp.wait_send()

    return pl.pallas_call(
        body, name=name,
        out_shape=[jax.ShapeDtypeStruct(p.shape[1:], F32) for p in parts],
        in_specs=[_vmem()] * n, out_specs=[_vmem()] * n,
        scratch_shapes=[pltpu.VMEM((N_DEV - 1,) + p.shape[1:], p.dtype) for p in parts]
        + [pltpu.SemaphoreType.DMA((7 * n,)), pltpu.SemaphoreType.DMA((7 * n,))],
        compiler_params=_params(vmem=VMEM_LIMIT_V7X),
    )(*parts)


def _adamw_math(w, g, m, v):
    m = ADAM_B1 * m + (1.0 - ADAM_B1) * g
    v = ADAM_B2 * v + (1.0 - ADAM_B2) * (g * g)
    m_hat = m / (1.0 - ADAM_B1 ** ADAM_STEP)
    v_hat = v / (1.0 - ADAM_B2 ** ADAM_STEP)
    delta = -ADAM_LR * (m_hat / (jnp.sqrt(v_hat) + ADAM_EPS) + ADAM_WD * w)
    return delta, m, v


def _adamw_shards(gs, ws, ms, vs):
    n = len(gs)

    def body(*refs):
        g_refs, w_refs, m_refs, v_refs = (refs[k * n:(k + 1) * n] for k in range(4))
        d_out, m_out, v_out = (refs[(4 + k) * n:(5 + k) * n] for k in range(3))
        for i in range(n):
            chunk, steps = _row_chunks(gs[i].shape[0])

            def step(s, carry, i=i, chunk=chunk):
                r = pl.ds(pl.multiple_of(s * chunk, chunk), chunk)
                d, m, v = _adamw_math(w_refs[i][r, :], g_refs[i][r, :], m_refs[i][r, :], v_refs[i][r, :])
                d_out[i][r, :] = d
                m_out[i][r, :] = m
                v_out[i][r, :] = v
                return carry

            lax.fori_loop(0, steps, step, 0)

    shapes = [jax.ShapeDtypeStruct(g.shape, F32) for g in gs]
    outs = pl.pallas_call(
        body, name="adamw_shards", out_shape=shapes * 3,
        in_specs=[_vmem()] * (4 * n), out_specs=[_vmem()] * (3 * n),
        compiler_params=_params(vmem=VMEM_LIMIT_V7X),
    )(*gs, *ws, *ms, *vs)
    return outs[:n], outs[n:2 * n], outs[2 * n:]


def _small_update(gathered, w, m, v):
    rows = w.shape[0]
    chunk, steps = _row_chunks(rows)

    def body(ga_ref, w_ref, m_ref, v_ref, g_out, d_out, m_out, v_out):
        def step(s, carry):
            r = pl.ds(pl.multiple_of(s * chunk, chunk), chunk)
            g = ga_ref[0, r, :]
            for j in range(1, N_DEV):
                g = g + ga_ref[j, r, :]
            d, mm, vv = _adamw_math(w_ref[r, :], g, m_ref[r, :], v_ref[r, :])
            g_out[r, :] = g
            d_out[r, :] = d
            m_out[r, :] = mm
            v_out[r, :] = vv
            return carry

        lax.fori_loop(0, steps, step, 0)

    return pl.pallas_call(
        body, name="small_update", out_shape=[jax.ShapeDtypeStruct(w.shape, F32)] * 4,
        in_specs=[_vmem()] * 4, out_specs=[_vmem()] * 4,
    )(gathered, w, m, v)


def _zoh(lr, li, logdt, btr, bti):
    dt = jnp.exp(logdt)
    mag = jnp.exp(lr * dt)
    th = li * dt
    ar = mag * jnp.cos(th)
    ai = mag * jnp.sin(th)
    den = lr * lr + li * li
    nr = ar - 1.0
    cr = (nr * lr + ai * li) / den
    ci = (ai * lr - nr * li) / den
    return ar, ai, cr * btr - ci * bti, cr * bti + ci * btr


def _zoh_fwd(lr, li, logdt, btr, bti):
    def body(lr_ref, li_ref, dt_ref, br_ref, bi_ref, ar_ref, ai_ref, bbr_ref, bbi_ref):
        ar, ai, bbr, bbi = _zoh(lr_ref[...], li_ref[...], dt_ref[...], br_ref[...], bi_ref[...])
        ar_ref[...] = ar
        ai_ref[...] = ai
        bbr_ref[...] = bbr
        bbi_ref[...] = bbi

    s = jax.ShapeDtypeStruct
    return pl.pallas_call(
        body, name="zoh_fwd",
        out_shape=[s(lr.shape, F32), s(lr.shape, F32), s(btr.shape, F32), s(btr.shape, F32)],
        in_specs=[_vmem()] * 5, out_specs=[_vmem()] * 4,
    )(lr, li, logdt, btr, bti)


def _zoh_bwd(lr, li, logdt, btr, bti, dar, dai, dbbr, dbbi):
    def body(lr_ref, li_ref, dt_ref, br_ref, bi_ref, dar_ref, dai_ref, dbbr_ref, dbbi_ref,
             glr_ref, gli_ref, gdt_ref, gbr_ref, gbi_ref):
        _, vjp = jax.vjp(_zoh, lr_ref[...], li_ref[...], dt_ref[...], br_ref[...], bi_ref[...])
        glr, gli, gdt, gbr, gbi = vjp((dar_ref[...], dai_ref[...], dbbr_ref[...], dbbi_ref[...]))
        glr_ref[...] = glr
        gli_ref[...] = gli
        gdt_ref[...] = gdt
        gbr_ref[...] = gbr
        gbi_ref[...] = gbi

    s = jax.ShapeDtypeStruct
    return pl.pallas_call(
        body, name="zoh_bwd",
        out_shape=[s(lr.shape, F32), s(lr.shape, F32), s(logdt.shape, F32), s(btr.shape, F32), s(btr.shape, F32)],
        in_specs=[_vmem()] * 9, out_specs=[_vmem()] * 5,
    )(lr, li, logdt, btr, bti, dar, dai, dbbr, dbbi)


def _blockdiag(t):
    g, r, s = t.shape
    t = t.reshape(4, 8, r, s)
    out = jnp.einsum("jirs,ik->jirks", t, jnp.eye(8, dtype=t.dtype))
    return out.reshape(4, 8 * r, 8 * s)


def _blockdiag_extract(m, r, s):
    m = m.reshape(4, 8, r, 8, s)
    out = jnp.einsum("jirks,ik->jirs", m, jnp.eye(8, dtype=m.dtype))
    return out.reshape(32, r, s)


def _head_ones():
    r = jnp.arange(ATTN_W) // HEAD_DIM
    return (r[:, None] == r[None, :]).astype(F32)


def _in_proj(x2, g_mix, w_in_g, ones_bd, gq_t, gk_t):
    t_tok = x2.shape[0]
    tm = min(512, t_tok)

    def body(x_ref, g_ref, w_ref, ones_ref, gq_ref, gk_ref, z_ref, qh_ref, kh_ref, vb_ref, xn_ref):
        x = x_ref[...]
        r = lax.rsqrt(jnp.mean(x * x, axis=-1, keepdims=True) + EPS)
        xn = (x * r * g_ref[...]).astype(BF16)
        xn_ref[...] = xn
        for j in range(N_DEV):
            z_ref[:, j * COL_W:(j + 1) * COL_W] = _dot(xn, w_ref[j])
        ones = ones_ref[...]
        q = z_ref[:, 0:ATTN_W]
        rq = lax.rsqrt(_hdot(q * q, ones) * (1.0 / HEAD_DIM) + EPS)
        qh_ref[...] = (q * rq * gq_ref[...] * (HEAD_DIM ** -0.5)).astype(BF16)
        k = z_ref[:, ATTN_W:2 * ATTN_W]
        rk = lax.rsqrt(_hdot(k * k, ones) * (1.0 / HEAD_DIM) + EPS)
        kh_ref[...] = (k * rk * gk_ref[...]).astype(BF16)
        vb_ref[...] = z_ref[:, 2 * ATTN_W:3 * ATTN_W].astype(BF16)

    row = lambda i: (i, 0)
    s = jax.ShapeDtypeStruct
    return pl.pallas_call(
        body, name="in_proj", grid=(t_tok // tm,),
        out_shape=[s((t_tok, IN_W), F32), s((t_tok, ATTN_W), BF16), s((t_tok, ATTN_W), BF16),
                   s((t_tok, ATTN_W), BF16), s((t_tok, D_MODEL), BF16)],
        in_specs=[pl.BlockSpec((tm, D_MODEL), row), _full(g_mix.shape), _full(w_in_g.shape), _full(ones_bd.shape),
                  _full(gq_t.shape), _full(gk_t.shape)],
        out_specs=[pl.BlockSpec((tm, IN_W), row), pl.BlockSpec((tm, ATTN_W), row), pl.BlockSpec((tm, ATTN_W), row),
                   pl.BlockSpec((tm, ATTN_W), row), pl.BlockSpec((tm, D_MODEL), row)],
        compiler_params=_params(1, VMEM_LIMIT_V7X),
    )(x2, g_mix, w_in_g, ones_bd, gq_t, gk_t)


TQ = 128
NEG = -1e30


def _band_masks(tq):
    r = lax.broadcasted_iota(jnp.int32, (tq, 2 * tq), 0)
    c = lax.broadcasted_iota(jnp.int32, (tq, 2 * tq), 1)
    d = tq + r - c
    r1 = lax.broadcasted_iota(jnp.int32, (tq, tq), 0)
    c1 = lax.broadcasted_iota(jnp.int32, (tq, tq), 1)
    return (d >= 0) & (d <= tq), r1 >= c1


def _band_tiling(seq, dil, single):
    length = seq // dil
    tq = min(TQ, length)
    assert length % tq == 0
    nt = length // tq
    blocks = dil * (ATTN_W // LANES)
    g = 1 if single else min(blocks, max(1, 16 // nt))
    return length, tq, nt, blocks, g


def _head_col(t, lm):
    return jnp.max(jnp.where(lm, t, NEG), axis=-1, keepdims=True)


def _class_view(a, nb, seq, dil):
    return a.reshape(nb * (seq // dil), dil * a.shape[-1])


def _band_attn_fwd(qh, kh, vb, nb, seq, dil, merge=None):
    t_tok = nb * seq
    length, tq, nt, blocks, g = _band_tiling(seq, dil, merge is not None)
    n_in = 3 if merge is None else 8

    def body(*refs):
        q_ref, k_ref, v_ref = refs[:3]
        extra = refs[3:n_in]
        o_ref, l_ref = refs[n_in], refs[n_in + 1]
        lane = lax.broadcasted_iota(jnp.int32, (1, LANES), 1)
        lms = [(lane // HEAD_DIM) == h for h in range(LANES // HEAD_DIM)]
        main, first = _band_masks(tq)

        def tile(cols, off_q, off_k, nk, valid):
            rows = pl.ds(off_q, tq)
            q = q_ref[rows, cols]
            kk = k_ref[pl.ds(off_k, nk), cols]
            vv = v_ref[pl.ds(off_k, nk), cols]
            o = jnp.zeros((tq, LANES), F32)
            lse = jnp.zeros((tq, LANES), F32)
            for lm in lms:
                s = _dot_nt(jnp.where(lm, q, jnp.zeros_like(q)), kk)
                m = jnp.max(jnp.where(valid, s, NEG), axis=-1, keepdims=True)
                p = jnp.where(valid, jnp.exp(s - m), 0.0)
                den = jnp.sum(p, axis=-1, keepdims=True)
                o = jnp.where(lm, _dot(p.astype(BF16), vv) / den, o)
                lse = jnp.where(lm, m + jnp.log(den), lse)
            if merge is not None:
                o2_ref, l2_ref, o3_ref, l3_ref, ga_ref, ag_ref = (*extra, refs[n_in + 2])
                l2, l3 = l2_ref[rows, cols], l3_ref[rows, cols]
                mx = jnp.maximum(lse, jnp.maximum(l2, l3))
                e1, e2, e3 = jnp.exp(lse - mx), jnp.exp(l2 - mx), jnp.exp(l3 - mx)
                tot = e1 + e2 + e3
                o = (e1 * o + e2 * o2_ref[rows, cols] + e3 * o3_ref[rows, cols]) / tot
                lse = mx + jnp.log(tot)
                ga = ga_ref[rows, cols]
                ag_ref[rows, cols] = (o * ga * _sig(ga)).astype(BF16)
            o_ref[rows, cols] = o
            l_ref[rows, cols] = lse

        for cb in range(g):
            cols = slice(cb * LANES, (cb + 1) * LANES)
            tile(cols, 0, 0, tq, first)
            if nt > 1:
                def step(i, carry, cols=cols):
                    off = pl.multiple_of(i * tq, tq)
                    tile(cols, off, pl.multiple_of(off - tq, tq), 2 * tq, main)
                    return carry

                lax.fori_loop(1, nt, step, 0)

    bmap = lambda b, j: (b, j)
    blk = pl.BlockSpec((length, g * LANES), bmap)
    view = lambda a: _class_view(a, nb, seq, dil)
    ins = [view(qh), view(kh), view(vb)]
    in_specs = [blk] * 3
    s = jax.ShapeDtypeStruct
    out_shape = [s((nb * length, dil * ATTN_W), F32)] * 2
    out_specs = [blk] * 2
    if merge is not None:
        assert dil == 1
        ins += list(merge)
        in_specs += [blk] * 4 + [pl.BlockSpec((length, LANES), lambda b, j: (b, 3 * ATTN_W // LANES + j))]
        out_shape.append(s((t_tok, ATTN_W), BF16))
        out_specs.append(blk)
    outs = pl.pallas_call(
        body, name=f"attn_fwd_d{dil}", grid=(nb, blocks // g),
        out_shape=out_shape, in_specs=in_specs, out_specs=out_specs,
        compiler_params=_params(2, VMEM_LIMIT_V7X),
    )(*ins)
    return [a.reshape(t_tok, ATTN_W) for a in outs]


def _attn_fwd(qh, kh, vb, z, nb, seq):
    (w1, d1), (w2, d2), (w3, d3) = DILATED
    assert d1 == 1 and w1 == TQ and w2 == TQ * d2 and min(w3, seq) == min(TQ * d3, seq)
    o3, l3 = _band_attn_fwd(qh, kh, vb, nb, seq, d3)
    o2, l2 = _band_attn_fwd(qh, kh, vb, nb, seq, d2)
    return _band_attn_fwd(qh, kh, vb, nb, seq, d1, merge=(o2, l2, o3, l3, z))


SCAN_COLS = 256


def _scan_chunk(re_ref, im_ref, a_re_ref, a_im_ref, carry_re, carry_im, rows, reverse, visit=None):
    nblk = rows // SUBLANES
    rowi = lax.broadcasted_iota(jnp.int32, (SUBLANES, SCAN_COLS), 0)
    edge = (SUBLANES - 1) if reverse else 0
    at_edge = rowi == edge

    def cmul(ar, ai, br, bi):
        return ar * br - ai * bi, ar * bi + ai * br

    for c0 in range(0, N_STATE, SCAN_COLS):
        cols = slice(c0, c0 + SCAN_COLS)
        a1r = jnp.broadcast_to(a_re_ref[:, cols], (SUBLANES, SCAN_COLS))
        a1i = jnp.broadcast_to(a_im_ref[:, cols], (SUBLANES, SCAN_COLS))
        if reverse:
            a1i = -a1i
        a2r, a2i = cmul(a1r, a1i, a1r, a1i)
        a4r, a4i = cmul(a2r, a2i, a2r, a2i)
        steps = []
        for sft, (pr, pi) in ((1, (a1r, a1i)), (2, (a2r, a2i)), (4, (a4r, a4i))):
            if reverse:
                steps.append((SUBLANES - sft, rowi < SUBLANES - sft, pr, pi))
            else:
                steps.append((sft, rowi >= sft, pr, pi))

        def block(i, carry, cols=cols, c0=c0, a1r=a1r, a1i=a1i, steps=steps):
            cr, ci, acc = carry
            blk = (nblk - 1 - i) if reverse else i
            off = pl.multiple_of(blk * SUBLANES, SUBLANES)
            r = re_ref[pl.ds(off, SUBLANES), cols]
            im = im_ref[pl.ds(off, SUBLANES), cols]
            inr, ini = cmul(a1r, a1i, cr, ci)
            r = r + jnp.where(at_edge, inr, 0.0)
            im = im + jnp.where(at_edge, ini, 0.0)
            for sft, keep, pr, pi in steps:
                rs = jnp.where(keep, pltpu.roll(r, sft, 0), 0.0)
                ims = jnp.where(keep, pltpu.roll(im, sft, 0), 0.0)
                dr, di = cmul(pr, pi, rs, ims)
                r, im = r + dr, im + di
            re_ref[pl.ds(off, SUBLANES), cols] = r
            im_ref[pl.ds(off, SUBLANES), cols] = im
            last = 0 if reverse else SUBLANES - 1
            cr = jnp.broadcast_to(r[last:last + 1, :], (SUBLANES, SCAN_COLS))
            ci = jnp.broadcast_to(im[last:last + 1, :], (SUBLANES, SCAN_COLS))
            if visit is not None:
                acc = visit(cols, blk, off, r, im, acc)
            return cr, ci, acc

        acc0 = (jnp.zeros((SUBLANES, SCAN_COLS), F32),) * 2
        cr, ci, acc = lax.fori_loop(0, nblk, block, (carry_re[:, cols], carry_im[:, cols], acc0))
        carry_re[:, cols] = cr
        carry_im[:, cols] = ci
        if visit is not None:
            visit(cols, None, None, None, None, acc)


def _ssm_fwd(z, a_re, a_im, bb_re, bb_im, cc_re, cc_im, d_skip, w_glu, b_glu, nb, seq):
    t_tok = nb * seq
    tc = min(256, seq)
    nch = seq // tc
    grp = N_STATE // 4

    def body(u_ref, gs_ref, ar_ref, ai_ref, bbr_ref, bbi_ref, ccr_ref, cci_ref, d_ref, wg_ref, bg_ref,
             xr_ref, xi_ref, y_ref, sg_ref, car_re, car_im):
        @pl.when(pl.program_id(1) == 0)
        def _():
            car_re[...] = jnp.zeros_like(car_re)
            car_im[...] = jnp.zeros_like(car_im)

        u = u_ref[...]
        ub = u.astype(BF16)
        for j in range(4):
            uj = ub[:, j * LANES:(j + 1) * LANES]
            xr_ref[:, j * grp:(j + 1) * grp] = _dot(uj, bbr_ref[j])
            xi_ref[:, j * grp:(j + 1) * grp] = _dot(uj, bbi_ref[j])
        _scan_chunk(xr_ref, xi_ref, ar_ref, ai_ref, car_re, car_im, tc, reverse=False)
        for j in range(4):
            xr = xr_ref[:, j * grp:(j + 1) * grp].astype(BF16)
            xi = xi_ref[:, j * grp:(j + 1) * grp].astype(BF16)
            y_ref[:, j * LANES:(j + 1) * LANES] = _dot(xr, ccr_ref[j]) - _dot(xi, cci_ref[j])
        y = y_ref[...] + d_ref[...] * u
        y_ref[...] = y
        yg, _ = _gelu_and_grad(y)
        gl = _dot(yg.astype(BF16), wg_ref[...]) + bg_ref[...]
        gs = gs_ref[...]
        sg_ref[...] = (yg * _sig(gl) * gs * _sig(gs)).astype(BF16)

    umap = lambda b, ch: (b * nch + ch, 4)
    gmap = lambda b, ch: (b * nch + ch, 5)
    row = lambda b, ch: (b * nch + ch, 0)
    s = jax.ShapeDtypeStruct
    consts = [a_re, a_im, bb_re, bb_im, cc_re, cc_im, d_skip, w_glu, b_glu]
    return pl.pallas_call(
        body, name="ssm_fwd", grid=(nb, nch),
        out_shape=[s((t_tok, N_STATE), F32), s((t_tok, N_STATE), F32), s((t_tok, SSM_W), F32),
                   s((t_tok, SSM_W), BF16)],
        in_specs=[pl.BlockSpec((tc, SSM_W), umap), pl.BlockSpec((tc, SSM_W), gmap)] + [_full(c.shape) for c in consts],
        out_specs=[pl.BlockSpec((tc, N_STATE), row), pl.BlockSpec((tc, N_STATE), row),
                   pl.BlockSpec((tc, SSM_W), row), pl.BlockSpec((tc, SSM_W), row)],
        scratch_shapes=[pltpu.VMEM((SUBLANES, N_STATE), F32), pltpu.VMEM((SUBLANES, N_STATE), F32)],
        compiler_params=_params(2, VMEM_LIMIT_V7X),
    )(z, z, *consts)


def _tail(x2, tg2, ag, sg, p2, w_out, w_g, w_p, g_ple):
    t_tok = x2.shape[0]
    tm = min(256, t_tok)
    nt = t_tok // tm
    half = ATTN_W

    def body(x_ref, tg_ref, ag_ref, sg_ref, p_ref, wo_ref, wg_ref, wp_ref, gp_ref,
             dmix_ref, dh1_ref, loss_ref, dgp_ref, dwo_ref, dwg_ref, dwp_ref, acc_o, acc_g, acc_p):
        i = pl.program_id(0)

        @pl.when(i == 0)
        def _():
            loss_ref[...] = jnp.zeros_like(loss_ref)
            dgp_ref[...] = jnp.zeros_like(dgp_ref)
            acc_o[...] = jnp.zeros_like(acc_o)
            acc_g[...] = jnp.zeros_like(acc_g)
            acc_p[...] = jnp.zeros_like(acc_p)

        ag_t, sg_t = ag_ref[...], sg_ref[...]
        h1 = x_ref[...] + _dot(ag_t, wo_ref[0:half, :]) + _dot(sg_t, wo_ref[half:2 * half, :])
        r2 = lax.rsqrt(jnp.mean(h1 * h1, axis=-1, keepdims=True) + EPS)
        hnorm = h1 * r2
        gp = gp_ref[...]
        hn = (hnorm * gp).astype(BF16)
        gate = _sig(_dot(hn, wg_ref[...]))
        pb = p_ref[...].astype(BF16)
        pp = jnp.concatenate([_dot(pb, wp_ref[j]) for j in range(N_DEV)], axis=-1)
        h2 = h1 + gate * pp
        err = h2 - tg_ref[...]
        loss_ref[...] += 0.5 * jnp.sum(err * err) * (1.0 / D_MODEL)
        dh2 = err * (1.0 / D_MODEL)
        dpp = (dh2 * gate).astype(BF16)
        dgpre = (dh2 * pp * gate * (1.0 - gate)).astype(BF16)
        acc_p[...] += _dot_tn(pb, dpp)
        acc_g[...] += _dot_tn(hn, dgpre)
        dhn = _dot_nt(dgpre, wg_ref[...])
        dgp_ref[...] += jnp.sum(dhn * hnorm, axis=0, keepdims=True)
        a = dhn * gp
        dh1 = dh2 + r2 * (a - hnorm * jnp.mean(a * hnorm, axis=-1, keepdims=True))
        dh1_ref[...] = dh1
        dh1b = dh1.astype(BF16)
        acc_o[0:half, :] += _dot_tn(ag_t, dh1b)
        acc_o[half:2 * half, :] += _dot_tn(sg_t, dh1b)
        dmix_ref[...] = _dot_nt(dh1b, wo_ref[...])

        @pl.when(i == nt - 1)
        def _():
            dwo_ref[...] = acc_o[...].astype(BF16)
            dwg_ref[...] = acc_g[...].astype(BF16)
            for j in range(N_DEV):
                dwp_ref[j] = acc_p[:, j * LANES:(j + 1) * LANES].astype(BF16)

    row = lambda i: (i, 0)
    s = jax.ShapeDtypeStruct
    return pl.pallas_call(
        body, name="tail_fwd_bwd", grid=(nt,),
        out_shape=[s((t_tok, D_MODEL), F32), s((t_tok, D_MODEL), F32), s((SUBLANES, LANES), F32),
                   s((1, D_MODEL), F32), s((D_MODEL, D_MODEL), BF16), s((D_MODEL, D_MODEL), BF16),
                   s((N_DEV, PLE_DIM, LANES), BF16)],
        in_specs=[pl.BlockSpec((tm, D_MODEL), row), pl.BlockSpec((tm, D_MODEL), row),
                  pl.BlockSpec((tm, half), row), pl.BlockSpec((tm, half), row), pl.BlockSpec((tm, PLE_DIM), row),
                  _full(w_out.shape), _full(w_g.shape), _full(w_p.shape), _full(g_ple.shape)],
        out_specs=[pl.BlockSpec((tm, D_MODEL), row), pl.BlockSpec((tm, D_MODEL), row), _full((SUBLANES, LANES)),
                   _full((1, D_MODEL)), _full((D_MODEL, D_MODEL)), _full((D_MODEL, D_MODEL)),
                   _full((N_DEV, PLE_DIM, LANES))],
        scratch_shapes=[pltpu.VMEM((D_MODEL, D_MODEL), F32), pltpu.VMEM((D_MODEL, D_MODEL), F32),
                        pltpu.VMEM((PLE_DIM, D_MODEL), F32)],
        compiler_params=_params(1, VMEM_LIMIT_V7X),
    )(x2, tg2, ag, sg, p2, w_out, w_g, w_p, g_ple)


def _attn_bwd_prep(dmix, z, o, ones_bd):
    t_tok = o.shape[0]
    tm = min(512, t_tok)

    def body(da_ref, ga_ref, o_ref, ones_ref, do_ref, dl_ref, dga_ref):
        da, ga, o_t = da_ref[...], ga_ref[...], o_ref[...]
        sga = _sig(ga)
        d_o = da * ga * sga
        do_ref[...] = d_o.astype(BF16)
        dl_ref[...] = _hdot(d_o * o_t, ones_ref[...])
        dga_ref[...] = da * o_t * sga * (1.0 + ga * (1.0 - sga))

    row = lambda i: (i, 0)
    s = jax.ShapeDtypeStruct
    blk = pl.BlockSpec((tm, ATTN_W), row)
    return pl.pallas_call(
        body, name="attn_bwd_prep", grid=(t_tok // tm,),
        out_shape=[s((t_tok, ATTN_W), BF16), s((t_tok, ATTN_W), F32), s((t_tok, ATTN_W), F32)],
        in_specs=[blk, pl.BlockSpec((tm, ATTN_W), lambda i: (i, 3)), blk, _full(ones_bd.shape)],
        out_specs=[blk, blk, blk],
        compiler_params=_params(1, VMEM_LIMIT_V7X),
    )(dmix, z, o, ones_bd)


def _band_attn_bwd(qh, kh, vb, d_o, lse, delta, nb, seq, dil, prev=None):
    t_tok = nb * seq
    length, tq, nt, blocks, g = _band_tiling(seq, dil, False)
    n_in = 6 if prev is None else 9

    def body(*refs):
        q_ref, k_ref, v_ref, do_ref, l_ref, dl_ref = refs[:6]
        dq_ref, dk_ref, dv_ref = refs[n_in:n_in + 3]
        if prev is None:
            dk_ref[...] = jnp.zeros_like(dk_ref)
            dv_ref[...] = jnp.zeros_like(dv_ref)
        else:
            dk_ref[...] = refs[7][...]
            dv_ref[...] = refs[8][...]
        lane = lax.broadcasted_iota(jnp.int32, (1, LANES), 1)
        lms = [(lane // HEAD_DIM) == h for h in range(LANES // HEAD_DIM)]
        main, first = _band_masks(tq)

        def tile(cols, off_q, off_k, nk, valid):
            rows = pl.ds(off_q, tq)
            keys = pl.ds(off_k, nk)
            q, d_ot = q_ref[rows, cols], do_ref[rows, cols]
            lt, dl = l_ref[rows, cols], dl_ref[rows, cols]
            kk, vv = k_ref[keys, cols], v_ref[keys, cols]
            dq = jnp.zeros((tq, LANES), F32) if prev is None else refs[6][rows, cols]
            for lm in lms:
                qm = jnp.where(lm, q, jnp.zeros_like(q))
                dom = jnp.where(lm, d_ot, jnp.zeros_like(d_ot))
                p = jnp.where(valid, jnp.exp(_dot_nt(qm, kk) - _head_col(lt, lm)), 0.0)
                dv_ref[keys, cols] += _dot_tn(p.astype(BF16), dom)
                ds = (p * (_dot_nt(dom, vv) - _head_col(dl, lm))).astype(BF16)
                dq = dq + jnp.where(lm, _dot(ds, kk), 0.0)
                dk_ref[keys, cols] += _dot_tn(ds, qm)
            dq_ref[rows, cols] = dq

        for cb in range(g):
            cols = slice(cb * LANES, (cb + 1) * LANES)
            tile(cols, 0, 0, tq, first)
            if nt > 1:
                def step(i, carry, cols=cols):
                    off = pl.multiple_of(i * tq, tq)
                    tile(cols, off, pl.multiple_of(off - tq, tq), 2 * tq, main)
                    return carry

                lax.fori_loop(1, nt, step, 0)

    blk = pl.BlockSpec((length, g * LANES), lambda b, j: (b, j))
    view = lambda a: _class_view(a, nb, seq, dil)
    ins = [view(a) for a in (qh, kh, vb, d_o, lse, delta)]
    if prev is not None:
        ins += [view(a) for a in prev]
    outs = pl.pallas_call(
        body, name=f"attn_bwd_d{dil}", grid=(nb, blocks // g),
        out_shape=[jax.ShapeDtypeStruct((nb * length, dil * ATTN_W), F32)] * 3,
        in_specs=[blk] * n_in, out_specs=[blk] * 3,
        compiler_params=_params(2, VMEM_LIMIT_V7X),
    )(*ins)
    return [a.reshape(t_tok, ATTN_W) for a in outs]


def _attn_bwd(qh, kh, vb, z, o, lse, dmix, ones_bd, nb, seq):
    d_o, delta, dga = _attn_bwd_prep(dmix, z, o, ones_bd)
    grads = None
    for _, dil in reversed(DILATED):
        grads = _band_attn_bwd(qh, kh, vb, d_o, lse, delta, nb, seq, dil, prev=grads)
    return (*grads, dga)


def _ssm_bwd(z, dmix, y, x_re, x_im, a_re, a_im, bb_re, bb_im, cc_re, cc_im, d_skip, w_glu, b_glu, nb, seq):
    t_tok = nb * seq
    tc = min(256, seq)
    nch = seq // tc
    grp = N_STATE // 4

    def body(u_ref, gs_ref, ds_ref, y_ref, xr_ref, xi_ref, xpr_ref, xpi_ref,
             ar_ref, ai_ref, bbr_ref, bbi_ref, ccr_ref, cci_ref, d_ref, wg_ref, bg_ref,
             du_ref, dgs_ref, dwg_ref, dbg_ref, dd_ref, dar_ref, dai_ref, dbbr_ref, dbbi_ref, dccr_ref, dcci_ref,
             lam_re, lam_im, car_re, car_im, acc_wg):
        step = pl.program_id(1)
        first_chunk = step == nch - 1

        @pl.when((pl.program_id(0) == 0) & (step == 0))
        def _():
            acc_wg[...] = jnp.zeros_like(acc_wg)
            for ref in (dbg_ref, dd_ref, dar_ref, dai_ref, dbbr_ref, dbbi_ref, dccr_ref, dcci_ref):
                ref[...] = jnp.zeros_like(ref)

        @pl.when(step == 0)
        def _():
            car_re[...] = jnp.zeros_like(car_re)
            car_im[...] = jnp.zeros_like(car_im)

        u, gs, dssm, y = u_ref[...], gs_ref[...], ds_ref[...], y_ref[...]
        yg, dgelu = _gelu_and_grad(y)
        ygb = yg.astype(BF16)
        sgl = _sig(_dot(ygb, wg_ref[...]) + bg_ref[...])
        sgs = _sig(gs)
        dout = dssm * gs * sgs
        dgs_ref[...] = dssm * yg * sgl * sgs * (1.0 + gs * (1.0 - sgs))
        dgl = dout * yg * sgl * (1.0 - sgl)
        dglb = dgl.astype(BF16)
        dyg = dout * sgl + _dot_nt(dglb, wg_ref[...])
        acc_wg[...] += _dot_tn(ygb, dglb)
        dbg_ref[...] += jnp.sum(dgl, axis=0, keepdims=True)
        dy = dyg * dgelu
        dd_ref[...] += jnp.sum(dy * u, axis=0, keepdims=True)
        dyb = dy.astype(BF16)
        ub = u.astype(BF16)
        for j in range(4):
            dyj = dyb[:, j * LANES:(j + 1) * LANES]
            sl = slice(j * grp, (j + 1) * grp)
            lam_re[:, sl] = _dot_nt(dyj, ccr_ref[j])
            lam_im[:, sl] = -_dot_nt(dyj, cci_ref[j])
            dccr_ref[j] += _dot_tn(xr_ref[:, sl].astype(BF16), dyj)
            dcci_ref[j] -= _dot_tn(xi_ref[:, sl].astype(BF16), dyj)

        keep_prev = jnp.where(first_chunk, 0.0, 1.0)
        rowi = lax.broadcasted_iota(jnp.int32, (SUBLANES, SCAN_COLS), 0)

        def visit(cols, blk, off, lr, li, acc):
            if blk is None:
                dar_ref[:, cols] += jnp.sum(acc[0], axis=0, keepdims=True)
                dai_ref[:, cols] += jnp.sum(acc[1], axis=0, keepdims=True)
                return None
            xr = xr_ref[pl.ds(off, SUBLANES), cols]
            xi = xi_ref[pl.ds(off, SUBLANES), cols]
            poff = pl.multiple_of(jnp.maximum(blk - 1, 0) * SUBLANES, SUBLANES)
            inside = blk > 0
            pr = jnp.where(inside, xr_ref[pl.ds(poff, SUBLANES), cols], xpr_ref[:, cols] * keep_prev)
            pi = jnp.where(inside, xi_ref[pl.ds(poff, SUBLANES), cols], xpi_ref[:, cols] * keep_prev)
            last = SUBLANES - 1
            pr = jnp.broadcast_to(pr[last:last + 1, :], (SUBLANES, SCAN_COLS))
            pi = jnp.broadcast_to(pi[last:last + 1, :], (SUBLANES, SCAN_COLS))
            xpr = jnp.where(rowi == 0, pr, pltpu.roll(xr, 1, 0))
            xpi = jnp.where(rowi == 0, pi, pltpu.roll(xi, 1, 0))
            return acc[0] + lr * xpr + li * xpi, acc[1] + li * xpr - lr * xpi

        _scan_chunk(lam_re, lam_im, ar_ref, ai_ref, car_re, car_im, tc, reverse=True, visit=visit)

        for j in range(4):
            sl = slice(j * grp, (j + 1) * grp)
            lr = lam_re[:, sl].astype(BF16)
            li = lam_im[:, sl].astype(BF16)
            uj = ub[:, j * LANES:(j + 1) * LANES]
            du_ref[:, j * LANES:(j + 1) * LANES] = (
                _dot_nt(lr, bbr_ref[j]) + _dot_nt(li, bbi_ref[j])
                + dy[:, j * LANES:(j + 1) * LANES] * d_ref[:, j * LANES:(j + 1) * LANES])
            dbbr_ref[j] += _dot_tn(uj, lr)
            dbbi_ref[j] += _dot_tn(uj, li)

        @pl.when((pl.program_id(0) == nb - 1) & (step == nch - 1))
        def _():
            dwg_ref[...] = acc_wg[...].astype(BF16)

    rev = lambda b, ch: b * nch + (nch - 1 - ch)
    umap = lambda b, ch: (rev(b, ch), 4)
    gmap = lambda b, ch: (rev(b, ch), 5)
    smap = lambda b, ch: (rev(b, ch), 1)
    row = lambda b, ch: (rev(b, ch), 0)
    prev = lambda b, ch: (jnp.maximum(rev(b, ch) * (tc // SUBLANES) - 1, 0), 0)
    s = jax.ShapeDtypeStruct
    consts = [a_re, a_im, bb_re, bb_im, cc_re, cc_im, d_skip, w_glu, b_glu]
    acc_shapes = [s((1, SSM_W), F32), s((1, SSM_W), F32), s((1, N_STATE), F32), s((1, N_STATE), F32),
                  s(bb_re.shape, F32), s(bb_re.shape, F32), s(cc_re.shape, F32), s(cc_re.shape, F32)]
    return pl.pallas_call(
        body, name="ssm_bwd", grid=(nb, nch),
        out_shape=[s((t_tok, SSM_W), F32), s((t_tok, SSM_W), F32), s((SSM_W, SSM_W), BF16)] + acc_shapes,
        in_specs=[pl.BlockSpec((tc, SSM_W), umap), pl.BlockSpec((tc, SSM_W), gmap), pl.BlockSpec((tc, SSM_W), smap),
                  pl.BlockSpec((tc, SSM_W), row), pl.BlockSpec((tc, N_STATE), row), pl.BlockSpec((tc, N_STATE), row),
                  pl.BlockSpec((SUBLANES, N_STATE), prev), pl.BlockSpec((SUBLANES, N_STATE), prev)]
        + [_full(c.shape) for c in consts],
        out_specs=[pl.BlockSpec((tc, SSM_W), row), pl.BlockSpec((tc, SSM_W), row), _full((SSM_W, SSM_W))]
        + [_full(a.shape) for a in acc_shapes],
        scratch_shapes=[pltpu.VMEM((tc, N_STATE), F32), pltpu.VMEM((tc, N_STATE), F32),
                        pltpu.VMEM((SUBLANES, N_STATE), F32), pltpu.VMEM((SUBLANES, N_STATE), F32),
                        pltpu.VMEM((SSM_W, SSM_W), F32)],
        compiler_params=_params(2, VMEM_LIMIT_V7X),
    )(z, z, dmix, y, x_re, x_im, x_re, x_im, *consts)


def _dz_and_dx(x2, z, dqh, dkh, dvb, dga, du, dgs, dh1, w_in_g, g_mix, gq_t, gk_t, ones_bd, fold):
    t_tok = x2.shape[0]
    tm = min(256, t_tok)
    nt = t_tok // tm
    a_w = ATTN_W

    def head_norm_bwd(raw, d_hat, gain, scale, ones):
        r = lax.rsqrt(_hdot(raw * raw, ones) * (1.0 / HEAD_DIM) + EPS)
        n = raw * r
        a = d_hat * gain * scale
        d_raw = r * (a - n * (_hdot(a * n, ones) * (1.0 / HEAD_DIM)))
        return d_raw, jnp.sum(d_hat * n * scale, axis=0, keepdims=True)

    def body(x_ref, q_ref, k_ref, dq_ref, dk_ref, dv_ref, dga_ref, du_ref, dgs_ref, dh1_ref, w_ref, g_ref,
             gq_ref, gk_ref, ones_ref, fold_ref, dz_ref, gx_ref, dgm_ref, dgq_ref, dgk_ref, acc_q, acc_k):
        i = pl.program_id(0)

        @pl.when(i == 0)
        def _():
            dgm_ref[...] = jnp.zeros_like(dgm_ref)
            acc_q[...] = jnp.zeros_like(acc_q)
            acc_k[...] = jnp.zeros_like(acc_k)

        ones = ones_ref[...]
        dq, sq = head_norm_bwd(q_ref[...], dq_ref[...], gq_ref[...], HEAD_DIM ** -0.5, ones)
        dk, sk = head_norm_bwd(k_ref[...], dk_ref[...], gk_ref[...], 1.0, ones)
        acc_q[...] += jnp.broadcast_to(sq, acc_q.shape)
        acc_k[...] += jnp.broadcast_to(sk, acc_k.shape)
        parts = (dq, dk, dv_ref[...], dga_ref[...], du_ref[...], dgs_ref[...])
        for n, part in enumerate(parts):
            dz_ref[:, n * a_w:(n + 1) * a_w] = part.astype(BF16)
        dxn = jnp.zeros((tm, D_MODEL), F32)
        for j in range(N_DEV):
            dxn = dxn + _dot_nt(dz_ref[:, j * COL_W:(j + 1) * COL_W], w_ref[j])
        x = x_ref[...]
        r1 = lax.rsqrt(jnp.mean(x * x, axis=-1, keepdims=True) + EPS)
        xnorm = x * r1
        dgm_ref[...] += jnp.sum(dxn * xnorm, axis=0, keepdims=True)
        a = dxn * g_ref[...]
        gx_ref[...] = dh1_ref[...] + r1 * (a - xnorm * jnp.mean(a * xnorm, axis=-1, keepdims=True))

        @pl.when(i == nt - 1)
        def _():
            dgq_ref[...] = _hdot(acc_q[...], fold_ref[...])
            dgk_ref[...] = _hdot(acc_k[...], fold_ref[...])

    row = lambda i: (i, 0)
    col = lambda n: (lambda i: (i, n))
    s = jax.ShapeDtypeStruct
    half = pl.BlockSpec((tm, a_w), row)
    return pl.pallas_call(
        body, name="dz_dx", grid=(nt,),
        out_shape=[s((t_tok, IN_W), BF16), s((t_tok, D_MODEL), F32), s((1, D_MODEL), F32),
                   s((SUBLANES, HEAD_DIM), F32), s((SUBLANES, HEAD_DIM), F32)],
        in_specs=[pl.BlockSpec((tm, D_MODEL), row), pl.BlockSpec((tm, a_w), col(0)), pl.BlockSpec((tm, a_w), col(1)),
                  half, half, half, half, half, half, pl.BlockSpec((tm, D_MODEL), row),
                  _full(w_in_g.shape), _full(g_mix.shape), _full(gq_t.shape), _full(gk_t.shape),
                  _full(ones_bd.shape), _full(fold.shape)],
        out_specs=[pl.BlockSpec((tm, IN_W), row), pl.BlockSpec((tm, D_MODEL), row), _full((1, D_MODEL)),
                   _full((SUBLANES, HEAD_DIM)), _full((SUBLANES, HEAD_DIM))],
        scratch_shapes=[pltpu.VMEM((SUBLANES, a_w), F32), pltpu.VMEM((SUBLANES, a_w), F32)],
        compiler_params=_params(1, VMEM_LIMIT_V7X),
    )(x2, z, z, dqh, dkh, dvb, dga, du, dgs, dh1, w_in_g, g_mix, gq_t, gk_t, ones_bd, fold)


def _dw_in(xn, dz):
    t_tok = xn.shape[0]
    tk = min(1024, t_tok)
    nk = t_tok // tk

    def body(xn_ref, dz_ref, out_ref, acc):
        k = pl.program_id(1)

        @pl.when(k == 0)
        def _():
            acc[...] = jnp.zeros_like(acc)

        acc[...] += _dot_tn(xn_ref[...], dz_ref[...])

        @pl.when(k == nk - 1)
        def _():
            out_ref[0] = acc[...].astype(BF16)

    return pl.pallas_call(
        body, name="dw_in", grid=(N_DEV, nk),
        out_shape=jax.ShapeDtypeStruct((N_DEV, D_MODEL, COL_W), BF16),
        in_specs=[pl.BlockSpec((tk, D_MODEL), lambda j, k: (k, 0)), pl.BlockSpec((tk, COL_W), lambda j, k: (k, j))],
        out_specs=pl.BlockSpec((1, D_MODEL, COL_W), lambda j, k: (j, 0, 0)),
        scratch_shapes=[pltpu.VMEM((D_MODEL, COL_W), F32)],
        compiler_params=_params(2, VMEM_LIMIT_V7X),
    )(xn, dz)


SMALL = ("mix_norm", "q_norm", "k_norm", "lambda_re", "lambda_im", "log_dt", "b_re", "b_im", "c_re", "c_im",
         "d_skip", "b_glu", "ple_norm")
BIG = ("w_in", "w_glu", "w_out", "w_ple_gate", "w_ple_proj")
WEIGHTS = ("mix_norm", "w_in", "q_norm", "k_norm", "lambda_re", "lambda_im", "log_dt", "b_re", "b_im", "c_re",
           "c_im", "d_skip", "w_glu", "b_glu", "w_out", "ple_norm", "w_ple_gate", "w_ple_proj")


def _pack(arrs):
    flat = jnp.concatenate([a.reshape(-1).astype(F32) for a in arrs])
    rows = -(-flat.shape[0] // (64 * LANES)) * 64
    return jnp.pad(flat, (0, rows * LANES - flat.shape[0])).reshape(rows, LANES)


def _unpack(packed, shapes):
    flat = packed.reshape(-1)
    out, off = [], 0
    for shp in shapes:
        size = math.prod(shp)
        out.append(flat[off:off + size].reshape(shp))
        off += size
    return out


def kernel(x, p, mix_norm, w_in, q_norm, k_norm, lambda_re, lambda_im, log_dt, b_re, b_im, c_re, c_im, d_skip, w_glu, b_glu, w_out, ple_norm, w_ple_gate, w_ple_proj, loss_target, m_mix_norm, m_w_in, m_q_norm, m_k_norm, m_lambda_re, m_lambda_im, m_log_dt, m_b_re, m_b_im, m_c_re, m_c_im, m_d_skip, m_w_glu, m_b_glu, m_w_out, m_ple_norm, m_w_ple_gate, m_w_ple_proj, v_mix_norm, v_w_in, v_q_norm, v_k_norm, v_lambda_re, v_lambda_im, v_log_dt, v_b_re, v_b_im, v_c_re, v_c_im, v_d_skip, v_w_glu, v_b_glu, v_w_out, v_ple_norm, v_w_ple_gate, v_w_ple_proj):
    env = dict(locals())
    w = {n: env[n] for n in WEIGHTS}
    m = {n: env["m_" + n] for n in WEIGHTS}
    v = {n: env["v_" + n] for n in WEIGHTS}
    nb, seq, _ = x.shape
    t_tok = nb * seq
    x2 = x.reshape(t_tok, D_MODEL)
    tg2 = loss_target.reshape(t_tok, D_MODEL)
    p2 = p.reshape(t_tok, PLE_DIM)

    shard2d = {"w_in": (D_MODEL, COL_W), "w_glu": (SSM_W // N_DEV, SSM_W), "w_out": (D_MODEL // N_DEV, D_MODEL),
               "w_ple_gate": (D_MODEL // N_DEV, D_MODEL), "w_ple_proj": (PLE_DIM, D_MODEL // N_DEV)}
    w_sh = [w[n].reshape(shard2d[n]) for n in BIG]
    w_in_g, w_glu_g, w_out_g, w_g_g, w_p_g = _all_gather(w_sh, [BF16] * len(BIG), "gather_weights")
    w_glu_f = w_glu_g.reshape(SSM_W, SSM_W)
    w_out_f = w_out_g.reshape(D_MODEL, D_MODEL)
    w_g_f = w_g_g.reshape(D_MODEL, D_MODEL)

    g3 = (SSM_GROUPS, 1, SSM_STATE)
    lr3, li3 = lambda_re.reshape(g3), lambda_im.reshape(g3)
    dt3 = log_dt.reshape(SSM_GROUPS, 1, 1)
    btr = b_re[0].transpose(0, 2, 1)
    bti = b_im[0].transpose(0, 2, 1)
    a_re3, a_im3, bbr, bbi = _zoh_fwd(lr3, li3, dt3, btr, bti)
    a_re, a_im = a_re3.reshape(1, N_STATE), a_im3.reshape(1, N_STATE)
    bb_re, bb_im = _blockdiag(bbr).astype(BF16), _blockdiag(bbi).astype(BF16)
    cc_re = _blockdiag(c_re[0].transpose(0, 2, 1)).astype(BF16)
    cc_im = _blockdiag(c_im[0].transpose(0, 2, 1)).astype(BF16)

    ones_bd = _head_ones()
    fold = jnp.tile(jnp.eye(HEAD_DIM, dtype=F32), (ATTN_W // HEAD_DIM, 1))
    gq_t = jnp.tile(q_norm, (1, ATTN_W // HEAD_DIM))
    gk_t = jnp.tile(k_norm, (1, ATTN_W // HEAD_DIM))

    z, qh, kh, vb, xn = _in_proj(x2, mix_norm, w_in_g, ones_bd, gq_t, gk_t)
    o, lse, ag = _attn_fwd(qh, kh, vb, z, nb, seq)
    x_re, x_im, y, sg = _ssm_fwd(z, a_re, a_im, bb_re, bb_im, cc_re, cc_im, d_skip, w_glu_f, b_glu, nb, seq)
    dmix, dh1, loss_t, d_ple, dw_out, dw_g, dw_p = _tail(x2, tg2, ag, sg, p2, w_out_f, w_g_f, w_p_g, ple_norm)

    dqh, dkh, dvb, dga = _attn_bwd(qh, kh, vb, z, o, lse, dmix, ones_bd, nb, seq)
    (du, dgs, dw_glu, d_bglu, d_dskip, da_re, da_im, dbb_re, dbb_im, dcc_re, dcc_im) = _ssm_bwd(
        z, dmix, y, x_re, x_im, a_re, a_im, bb_re, bb_im, cc_re, cc_im, d_skip, w_glu_f, b_glu, nb, seq)
    dz, gx, d_mix, d_gq, d_gk = _dz_and_dx(x2, z, dqh, dkh, dvb, dga, du, dgs, dh1, w_in_g, mix_norm, gq_t, gk_t,
                                           ones_bd, fold)
    dw_in = _dw_in(xn, dz)

    d_lr, d_li, d_dt, d_btr, d_bti = _zoh_bwd(
        lr3, li3, dt3, btr, bti, da_re.reshape(g3), da_im.reshape(g3),
        _blockdiag_extract(dbb_re, SSM_GROUP, SSM_STATE), _blockdiag_extract(dbb_im, SSM_GROUP, SSM_STATE))
    small_g = {
        "mix_norm": d_mix, "q_norm": d_gq[0:1], "k_norm": d_gk[0:1], "lambda_re": d_lr, "lambda_im": d_li,
        "log_dt": d_dt, "b_re": d_btr.transpose(0, 2, 1), "b_im": d_bti.transpose(0, 2, 1),
        "c_re": _blockdiag_extract(dcc_re, SSM_STATE, SSM_GROUP).transpose(0, 2, 1),
        "c_im": _blockdiag_extract(dcc_im, SSM_STATE, SSM_GROUP).transpose(0, 2, 1),
        "d_skip": d_dskip, "b_glu": d_bglu, "ple_norm": d_ple}

    parts = [dw_in, dw_glu.reshape(N_DEV, SSM_W // N_DEV, SSM_W), dw_out.reshape(N_DEV, D_MODEL // N_DEV, D_MODEL),
             dw_g.reshape(N_DEV, D_MODEL // N_DEV, D_MODEL), dw_p]
    g_sh = _reduce_scatter(parts, "scatter_grads")
    d_sh, m_sh, v_sh = _adamw_shards(g_sh, w_sh, [m[n].reshape(shard2d[n]) for n in BIG],
                                     [v[n].reshape(shard2d[n]) for n in BIG])

    (gathered,) = _all_gather([_pack([small_g[n] for n in SMALL])], [F32], "gather_small_grads")
    g_pk, d_pk, m_pk, v_pk = _small_update(gathered, _pack([w[n] for n in SMALL]), _pack([m[n] for n in SMALL]),
                                           _pack([v[n] for n in SMALL]))

    grads, deltas, new_m, new_v = {}, {}, {}, {}
    small_shapes = [w[n].shape for n in SMALL]
    for dst, packed in ((grads, g_pk), (deltas, d_pk), (new_m, m_pk), (new_v, v_pk)):
        for n, a in zip(SMALL, _unpack(packed, small_shapes)):
            dst[n] = a
    for i, n in enumerate(BIG):
        grads[n] = g_sh[i].reshape(w[n].shape)
        deltas[n] = d_sh[i].reshape(w[n].shape)
        new_m[n] = m_sh[i].reshape(w[n].shape)
        new_v[n] = v_sh[i].reshape(w[n].shape)

    loss = lax.psum(loss_t[0, 0], AXES)
    return (loss, gx.reshape(x.shape), *[grads[n] for n in WEIGHTS], *[deltas[n] for n in WEIGHTS],
            *[new_m[n] for n in WEIGHTS], *[new_v[n] for n in WEIGHTS])
```

```python
import math

import jax
import jax.numpy as jnp
from jax import lax
from jax.experimental import pallas as pl
from jax.experimental.pallas import tpu as pltpu

F32 = jnp.float32
BF16 = jnp.bfloat16
MESH = pl.DeviceIdType.MESH
AXES = ("x", "y", "c")
N_DEV = 8

D_MODEL = 1024
HEAD_DIM = 64
ATTN_W = 512
SSM_W = 512
SSM_GROUPS = 32
SSM_GROUP = 16
SSM_STATE = 64
N_STATE = SSM_GROUPS * SSM_STATE
PLE_DIM = 256
IN_W = 3072
COL_W = IN_W // N_DEV
DILATED = ((128, 1), (512, 4), (2048, 16))
EPS = 1e-6
INV_SQRT2 = 1.0 / math.sqrt(2.0)
INV_SQRT_2PI = 1.0 / math.sqrt(2.0 * math.pi)

ADAM_LR, ADAM_B1, ADAM_B2, ADAM_EPS, ADAM_WD, ADAM_STEP = 0.001, 0.9, 0.999, 1e-08, 0.01, 10

VMEM_LIMIT_V7X = 56 * 1024 * 1024
SUBLANES = 8
LANES = 128


def _params(n_axes=None, vmem=None):
    kw = {}
    if n_axes:
        kw["dimension_semantics"] = ("arbitrary",) * n_axes
    if vmem:
        kw["vmem_limit_bytes"] = vmem
    return pltpu.CompilerParams(**kw)


def _dot(a, b):
    return jnp.dot(a, b, preferred_element_type=F32)


def _dot_nt(a, b):
    return lax.dot_general(a, b, (((1,), (1,)), ((), ())), preferred_element_type=F32)


def _dot_tn(a, b):
    return lax.dot_general(a, b, (((0,), (0,)), ((), ())), preferred_element_type=F32)


def _hdot(a, b):
    return jnp.dot(a, b, precision=lax.Precision.HIGHEST, preferred_element_type=F32)


def _sig(x):
    return 1.0 / (1.0 + jnp.exp(-x))


def _gelu_and_grad(y):
    cdf = 0.5 * (1.0 + lax.erf(y * INV_SQRT2))
    pdf = jnp.exp(-0.5 * y * y) * INV_SQRT_2PI
    return y * cdf, cdf + y * pdf


def _vmem():
    return pl.BlockSpec(memory_space=pltpu.VMEM)


def _full(shape):
    nd = len(shape)
    return pl.BlockSpec(shape, lambda *_: (0,) * nd)


def _all_gather(shards, out_dtypes, name):
    n = len(shards)

    def body(*refs):
        in_refs, out_refs = refs[:n], refs[n:2 * n]
        send_sems, recv_sems = refs[2 * n], refs[2 * n + 1]
        x, y, c = lax.axis_index("x"), lax.axis_index("y"), lax.axis_index("c")
        me, sibling = (x, y, c), (x, y, 1 - c)
        chips = [(1 - x, y), (x, 1 - y), (1 - x, 1 - y)]

        def idx(px, py, pc):
            return 4 * px + 2 * py + pc

        def copy(i, k, block, to):
            ref = out_refs[i].at[idx(*block)]
            return pltpu.make_async_remote_copy(
                src_ref=ref, dst_ref=ref, send_sem=send_sems.at[7 * i + k], recv_sem=recv_sems.at[7 * i + k],
                device_id=to, device_id_type=MESH)

        for i in range(n):
            out_refs[i][idx(*me)] = in_refs[i][...].astype(out_refs[i].dtype)
        first = []
        for i in range(n):
            first.append(copy(i, 0, me, sibling))
            first += [copy(i, 1 + j, me, (*chip, c)) for j, chip in enumerate(chips)]
        for cp in first:
            cp.start()
        passed = []
        for j, chip in enumerate(chips):
            for i in range(n):
                copy(i, 1 + j, (*chip, c), me).wait_recv()
                cp = copy(i, 4 + j, (*chip, c), sibling)
                cp.start()
                passed.append(cp)
        for i in range(n):
            copy(i, 0, sibling, me).wait_recv()
            for j, chip in enumerate(chips):
                copy(i, 4 + j, (*chip, 1 - c), me).wait_recv()
        for cp in first + passed:
            cp.wait_send()

    return pl.pallas_call(
        body, name=name,
        out_shape=[jax.ShapeDtypeStruct((N_DEV,) + s.shape, dt) for s, dt in zip(shards, out_dtypes)],
        in_specs=[_vmem()] * n, out_specs=[_vmem()] * n,
        scratch_shapes=[pltpu.SemaphoreType.DMA((7 * n,)), pltpu.SemaphoreType.DMA((7 * n,))],
        compiler_params=_params(vmem=VMEM_LIMIT_V7X),
    )(*shards)


def _row_chunks(rows):
    chunk = 64 if rows % 64 == 0 else rows
    return chunk, rows // chunk


def _reduce_scatter(parts, name):
    n = len(parts)

    def body(*refs):
        in_refs, out_refs, recv_refs = refs[:n], refs[n:2 * n], refs[2 * n:3 * n]
        send_sems, recv_sems = refs[3 * n], refs[3 * n + 1]
        x, y, c = lax.axis_index("x"), lax.axis_index("y"), lax.axis_index("c")
        my = 4 * x + 2 * y + c
        copies = []
        for i in range(n):
            for m in range(1, N_DEV):
                px = 1 - x if m & 4 else x
                py = 1 - y if m & 2 else y
                pc = 1 - c if m & 1 else c
                cp = pltpu.make_async_remote_copy(
                    src_ref=in_refs[i].at[4 * px + 2 * py + pc], dst_ref=recv_refs[i].at[m - 1],
                    send_sem=send_sems.at[7 * i + m - 1], recv_sem=recv_sems.at[7 * i + m - 1],
                    device_id=(px, py, pc), device_id_type=MESH)
                cp.start()
                copies.append(cp)
        for cp in copies:
            cp.wait_recv()
        for i in range(n):
            rows = parts[i].shape[1]
            chunk, steps = _row_chunks(rows)

            def step(s, carry, i=i, chunk=chunk):
                r = pl.ds(pl.multiple_of(s * chunk, chunk), chunk)
                acc = in_refs[i][my, r, :].astype(F32)
                for m in range(1, N_DEV):
                    acc = acc + recv_refs[i][m - 1, r, :].astype(F32)
                out_refs[i][r, :] = acc
                return carry

            lax.fori_loop(0, steps, step, 0)
        for cp in copies:
            cp.wait_send()

    return pl.pallas_call(
        body, name=name,
        out_shape=[jax.ShapeDtypeStruct(p.shape[1:], F32) for p in parts],
        in_specs=[_vmem()] * n, out_specs=[_vmem()] * n,
        scratch_shapes=[pltpu.VMEM((N_DEV - 1,) + p.shape[1:], p.dtype) for p in parts]
        + [pltpu.SemaphoreType.DMA((7 * n,)), pltpu.SemaphoreType.DMA((7 * n,))],
        compiler_params=_params(vmem=VMEM_LIMIT_V7X),
    )(*parts)


def _adamw_math(w, g, m, v):
    m = ADAM_B1 * m + (1.0 - ADAM_B1) * g
    v = ADAM_B2 * v + (1.0 - ADAM_B2) * (g * g)
    m_hat = m / (1.0 - ADAM_B1 ** ADAM_STEP)
    v_hat = v / (1.0 - ADAM_B2 ** ADAM_STEP)
    delta = -ADAM_LR * (m_hat / (jnp.sqrt(v_hat) + ADAM_EPS) + ADAM_WD * w)
    return delta, m, v


def _adamw_shards(gs, ws, ms, vs):
    n = len(gs)

    def body(*refs):
        g_refs, w_refs, m_refs, v_refs = (refs[k * n:(k + 1) * n] for k in range(4))
        d_out, m_out, v_out = (refs[(4 + k) * n:(5 + k) * n] for k in range(3))
        for i in range(n):
            chunk, steps = _row_chunks(gs[i].shape[0])

            def step(s, carry, i=i, chunk=chunk):
                r = pl.ds(pl.multiple_of(s * chunk, chunk), chunk)
                d, m, v = _adamw_math(w_refs[i][r, :], g_refs[i][r, :], m_refs[i][r, :], v_refs[i][r, :])
                d_out[i][r, :] = d
                m_out[i][r, :] = m
                v_out[i][r, :] = v
                return carry

            lax.fori_loop(0, steps, step, 0)

    shapes = [jax.ShapeDtypeStruct(g.shape, F32) for g in gs]
    outs = pl.pallas_call(
        body, name="adamw_shards", out_shape=shapes * 3,
        in_specs=[_vmem()] * (4 * n), out_specs=[_vmem()] * (3 * n),
        compiler_params=_params(vmem=VMEM_LIMIT_V7X),
    )(*gs, *ws, *ms, *vs)
    return outs[:n], outs[n:2 * n], outs[2 * n:]


def _small_update(gathered, w, m, v):
    rows = w.shape[0]
    chunk, steps = _row_chunks(rows)

    def body(ga_ref, w_ref, m_ref, v_ref, g_out, d_out, m_out, v_out):
        def step(s, carry):
            r = pl.ds(pl.multiple_of(s * chunk, chunk), chunk)
            g = ga_ref[0, r, :]
            for j in range(1, N_DEV):
                g = g + ga_ref[j, r, :]
            d, mm, vv = _adamw_math(w_ref[r, :], g, m_ref[r, :], v_ref[r, :])
            g_out[r, :] = g
            d_out[r, :] = d
            m_out[r, :] = mm
            v_out[r, :] = vv
            return carry

        lax.fori_loop(0, steps, step, 0)

    return pl.pallas_call(
        body, name="small_update", out_shape=[jax.ShapeDtypeStruct(w.shape, F32)] * 4,
        in_specs=[_vmem()] * 4, out_specs=[_vmem()] * 4,
    )(gathered, w, m, v)


def _zoh(lr, li, logdt, btr, bti):
    dt = jnp.exp(logdt)
    mag = jnp.exp(lr * dt)
    th = li * dt
    ar = mag * jnp.cos(th)
    ai = mag * jnp.sin(th)
    den = lr * lr + li * li
    nr = ar - 1.0
    cr = (nr * lr + ai * li) / den
    ci = (ai * lr - nr * li) / den
    return ar, ai, cr * btr - ci * bti, cr * bti + ci * btr


def _zoh_fwd(lr, li, logdt, btr, bti):
    def body(lr_ref, li_ref, dt_ref, br_ref, bi_ref, ar_ref, ai_ref, bbr_ref, bbi_ref):
        ar, ai, bbr, bbi = _zoh(lr_ref[...], li_ref[...], dt_ref[...], br_ref[...], bi_ref[...])
        ar_ref[...] = ar
        ai_ref[...] = ai
        bbr_ref[...] = bbr
        bbi_ref[...] = bbi

    s = jax.ShapeDtypeStruct
    return pl.pallas_call(
        body, name="zoh_fwd",
        out_shape=[s(lr.shape, F32), s(lr.shape, F32), s(btr.shape, F32), s(btr.shape, F32)],
        in_specs=[_vmem()] * 5, out_specs=[_vmem()] * 4,
    )(lr, li, logdt, btr, bti)


def _zoh_bwd(lr, li, logdt, btr, bti, dar, dai, dbbr, dbbi):
    def body(lr_ref, li_ref, dt_ref, br_ref, bi_ref, dar_ref, dai_ref, dbbr_ref, dbbi_ref,
             glr_ref, gli_ref, gdt_ref, gbr_ref, gbi_ref):
        _, vjp = jax.vjp(_zoh, lr_ref[...], li_ref[...], dt_ref[...], br_ref[...], bi_ref[...])
        glr, gli, gdt, gbr, gbi = vjp((dar_ref[...], dai_ref[...], dbbr_ref[...], dbbi_ref[...]))
        glr_ref[...] = glr
        gli_ref[...] = gli
        gdt_ref[...] = gdt
        gbr_ref[...] = gbr
        gbi_ref[...] = gbi

    s = jax.ShapeDtypeStruct
    return pl.pallas_call(
        body, name="zoh_bwd",
        out_shape=[s(lr.shape, F32), s(lr.shape, F32), s(logdt.shape, F32), s(btr.shape, F32), s(btr.shape, F32)],
        in_specs=[_vmem()] * 9, out_specs=[_vmem()] * 5,
    )(lr, li, logdt, btr, bti, dar, dai, dbbr, dbbi)


def _blockdiag(t):
    g, r, s = t.shape
    t = t.reshape(4, 8, r, s)
    out = jnp.einsum("jirs,ik->jirks", t, jnp.eye(8, dtype=t.dtype))
    return out.reshape(4, 8 * r, 8 * s)


def _blockdiag_extract(m, r, s):
    m = m.reshape(4, 8, r, 8, s)
    out = jnp.einsum("jirks,ik->jirs", m, jnp.eye(8, dtype=m.dtype))
    return out.reshape(32, r, s)


def _head_ones():
    r = jnp.arange(ATTN_W) // HEAD_DIM
    return (r[:, None] == r[None, :]).astype(F32)


def _in_proj(x2, g_mix, w_in_g, ones_bd, gq_t, gk_t):
    t_tok = x2.shape[0]
    tm = min(512, t_tok)

    def body(x_ref, g_ref, w_ref, ones_ref, gq_ref, gk_ref, z_ref, qh_ref, kh_ref, vb_ref, xn_ref):
        x = x_ref[...]
        r = lax.rsqrt(jnp.mean(x * x, axis=-1, keepdims=True) + EPS)
        xn = (x * r * g_ref[...]).astype(BF16)
        xn_ref[...] = xn
        for j in range(N_DEV):
            z_ref[:, j * COL_W:(j + 1) * COL_W] = _dot(xn, w_ref[j])
        ones = ones_ref[...]
        q = z_ref[:, 0:ATTN_W]
        rq = lax.rsqrt(_hdot(q * q, ones) * (1.0 / HEAD_DIM) + EPS)
        qh_ref[...] = (q * rq * gq_ref[...] * (HEAD_DIM ** -0.5)).astype(BF16)
        k = z_ref[:, ATTN_W:2 * ATTN_W]
        rk = lax.rsqrt(_hdot(k * k, ones) * (1.0 / HEAD_DIM) + EPS)
        kh_ref[...] = (k * rk * gk_ref[...]).astype(BF16)
        vb_ref[...] = z_ref[:, 2 * ATTN_W:3 * ATTN_W].astype(BF16)

    row = lambda i: (i, 0)
    s = jax.ShapeDtypeStruct
    return pl.pallas_call(
        body, name="in_proj", grid=(t_tok // tm,),
        out_shape=[s((t_tok, IN_W), F32), s((t_tok, ATTN_W), BF16), s((t_tok, ATTN_W), BF16),
                   s((t_tok, ATTN_W), BF16), s((t_tok, D_MODEL), BF16)],
        in_specs=[pl.BlockSpec((tm, D_MODEL), row), _full(g_mix.shape), _full(w_in_g.shape), _full(ones_bd.shape),
                  _full(gq_t.shape), _full(gk_t.shape)],
        out_specs=[pl.BlockSpec((tm, IN_W), row), pl.BlockSpec((tm, ATTN_W), row), pl.BlockSpec((tm, ATTN_W), row),
                   pl.BlockSpec((tm, ATTN_W), row), pl.BlockSpec((tm, D_MODEL), row)],
        compiler_params=_params(1, VMEM_LIMIT_V7X),
    )(x2, g_mix, w_in_g, ones_bd, gq_t, gk_t)


TQ = 128
NEG = -1e30


def _head_col(t, lm):
    return jnp.max(jnp.where(lm, t, NEG), axis=-1, keepdims=True)


def _head_masks():
    lane = lax.broadcasted_iota(jnp.int32, (1, 1, LANES), 2)
    return [(lane // HEAD_DIM) == h for h in range(LANES // HEAD_DIM)]


def _gather_classes(ref, dil, nt, tq, dtype):
    length = nt * tq
    if dil == 1:
        return ref[...].astype(dtype).reshape(nt, tq, LANES)
    parts = [ref[pl.ds(r, length, stride=dil), :].astype(dtype).reshape(nt, tq, LANES) for r in range(dil)]
    return jnp.concatenate(parts, axis=0)


def _scatter_classes(ref, val, dil, nt, tq, add):
    length = nt * tq
    for r in range(dil):
        rows = pl.ds(r, length, stride=dil) if dil > 1 else slice(None)
        part = val[r * nt:(r + 1) * nt].reshape(length, LANES)
        ref[rows, :] = ref[rows, :] + part if add else part


def _with_prev_tile(t3, dil, nt):
    parts = []
    for r in range(dil):
        t = t3[r * nt:(r + 1) * nt]
        parts.append(jnp.concatenate([t[:1], t[:-1]], axis=0))
    prev = parts[0] if dil == 1 else jnp.concatenate(parts, axis=0)
    return jnp.concatenate([prev, t3], axis=1)


def _band_valid(dil, nt, tq):
    if nt == 1:
        shape = (dil, tq, tq)
        return lax.broadcasted_iota(jnp.int32, shape, 1) >= lax.broadcasted_iota(jnp.int32, shape, 2)
    shape = (dil * nt, tq, 2 * tq)
    b = lax.broadcasted_iota(jnp.int32, shape, 0)
    c = lax.broadcasted_iota(jnp.int32, shape, 2)
    d = tq + lax.broadcasted_iota(jnp.int32, shape, 1) - c
    return (d >= 0) & (d <= tq) & (((b & (nt - 1)) != 0) | (c >= tq))


def _window_tiling(seq, window, dil):
    length = seq // dil
    tq = min(TQ, length)
    nt = length // tq
    assert length % tq == 0 and nt & (nt - 1) == 0 and (nt == 1 or window == tq * dil)
    return nt, tq


def _bqk(a, b):
    return jnp.einsum("bqd,bkd->bqk", a, b, preferred_element_type=F32)


def _bqd(a, b):
    return jnp.einsum("bqk,bkd->bqd", a, b, preferred_element_type=F32)


def _bkd(a, b):
    return jnp.einsum("bqk,bqd->bkd", a, b, preferred_element_type=F32)


def _attn_fwd(qh, kh, vb, z, nb, seq):
    t_tok = nb * seq
    n_win = len(DILATED)

    def body(q_ref, k_ref, v_ref, ga_ref, o_ref, l_ref, ag_ref, qf, kf, vf, oc, lc):
        qf[...] = q_ref[...].astype(F32)
        kf[...] = k_ref[...].astype(F32)
        vf[...] = v_ref[...].astype(F32)
        lms = _head_masks()
        for w, (window, dil) in enumerate(DILATED):
            nt, tq = _window_tiling(seq, window, dil)
            q3 = _gather_classes(qf, dil, nt, tq, BF16)
            k3 = _gather_classes(kf, dil, nt, tq, BF16)
            v3 = _gather_classes(vf, dil, nt, tq, BF16)
            if nt > 1:
                k3, v3 = _with_prev_tile(k3, dil, nt), _with_prev_tile(v3, dil, nt)
            valid = _band_valid(dil, nt, tq)
            o = jnp.zeros(q3.shape, F32)
            lse = jnp.zeros(q3.shape, F32)
            for lm in lms:
                s = _bqk(jnp.where(lm, q3, jnp.zeros_like(q3)), k3)
                m = jnp.max(jnp.where(valid, s, NEG), axis=-1, keepdims=True)
                p = jnp.where(valid, jnp.exp(s - m), 0.0)
                den = jnp.sum(p, axis=-1, keepdims=True)
                o = jnp.where(lm, _bqd(p.astype(BF16), v3) / den, o)
                lse = jnp.where(lm, m + jnp.log(den), lse)
            _scatter_classes(oc.at[w], o, dil, nt, tq, add=False)
            _scatter_classes(lc.at[w], lse, dil, nt, tq, add=False)
        mx = lc[0]
        for w in range(1, n_win):
            mx = jnp.maximum(mx, lc[w])
        tot = jnp.zeros_like(mx)
        o = jnp.zeros_like(mx)
        for w in range(n_win):
            e = jnp.exp(lc[w] - mx)
            tot = tot + e
            o = o + e * oc[w]
        o = o / tot
        o_ref[...] = o
        l_ref[...] = mx + jnp.log(tot)
        ga = ga_ref[...]
        ag_ref[...] = (o * ga * _sig(ga)).astype(BF16)

    blk = pl.BlockSpec((seq, LANES), lambda b, hp: (b, hp))
    s = jax.ShapeDtypeStruct
    return pl.pallas_call(
        body, name="attn_fwd", grid=(nb, ATTN_W // LANES),
        out_shape=[s((t_tok, ATTN_W), F32), s((t_tok, ATTN_W), F32), s((t_tok, ATTN_W), BF16)],
        in_specs=[blk, blk, blk, pl.BlockSpec((seq, LANES), lambda b, hp: (b, 3 * ATTN_W // LANES + hp))],
        out_specs=[blk, blk, blk],
        scratch_shapes=[pltpu.VMEM((seq, LANES), F32)] * 3 + [pltpu.VMEM((n_win, seq, LANES), F32)] * 2,
        compiler_params=_params(2, VMEM_LIMIT_V7X),
    )(qh, kh, vb, z)


SCAN_COLS = 256


def _scan_chunk(re_ref, im_ref, a_re_ref, a_im_ref, carry_re, carry_im, rows, reverse, visit=None):
    nblk = rows // SUBLANES
    rowi = lax.broadcasted_iota(jnp.int32, (SUBLANES, SCAN_COLS), 0)
    edge = (SUBLANES - 1) if reverse else 0
    at_edge = rowi == edge

    def cmul(ar, ai, br, bi):
        return ar * br - ai * bi, ar * bi + ai * br

    for c0 in range(0, N_STATE, SCAN_COLS):
        cols = slice(c0, c0 + SCAN_COLS)
        a1r = jnp.broadcast_to(a_re_ref[:, cols], (SUBLANES, SCAN_COLS))
        a1i = jnp.broadcast_to(a_im_ref[:, cols], (SUBLANES, SCAN_COLS))
        if reverse:
            a1i = -a1i
        a2r, a2i = cmul(a1r, a1i, a1r, a1i)
        a4r, a4i = cmul(a2r, a2i, a2r, a2i)
        steps = []
        for sft, (pr, pi) in ((1, (a1r, a1i)), (2, (a2r, a2i)), (4, (a4r, a4i))):
            if reverse:
                steps.append((SUBLANES - sft, rowi < SUBLANES - sft, pr, pi))
            else:
                steps.append((sft, rowi >= sft, pr, pi))

        def block(i, carry, cols=cols, c0=c0, a1r=a1r, a1i=a1i, steps=steps):
            cr, ci, acc = carry
            blk = (nblk - 1 - i) if reverse else i
            off = pl.multiple_of(blk * SUBLANES, SUBLANES)
            r = re_ref[pl.ds(off, SUBLANES), cols]
            im = im_ref[pl.ds(off, SUBLANES), cols]
            inr, ini = cmul(a1r, a1i, cr, ci)
            r = r + jnp.where(at_edge, inr, 0.0)
            im = im + jnp.where(at_edge, ini, 0.0)
            for sft, keep, pr, pi in steps:
                rs = jnp.where(keep, pltpu.roll(r, sft, 0), 0.0)
                ims = jnp.where(keep, pltpu.roll(im, sft, 0), 0.0)
                dr, di = cmul(pr, pi, rs, ims)
                r, im = r + dr, im + di
            re_ref[pl.ds(off, SUBLANES), cols] = r
            im_ref[pl.ds(off, SUBLANES), cols] = im
            last = 0 if reverse else SUBLANES - 1
            cr = jnp.broadcast_to(r[last:last + 1, :], (SUBLANES, SCAN_COLS))
            ci = jnp.broadcast_to(im[last:last + 1, :], (SUBLANES, SCAN_COLS))
            if visit is not None:
                acc = visit(cols, blk, off, r, im, acc)
            return cr, ci, acc

        acc0 = (jnp.zeros((SUBLANES, SCAN_COLS), F32),) * 2
        cr, ci, acc = lax.fori_loop(0, nblk, block, (carry_re[:, cols], carry_im[:, cols], acc0))
        carry_re[:, cols] = cr
        carry_im[:, cols] = ci
        if visit is not None:
            visit(cols, None, None, None, None, acc)


def _ssm_fwd(z, a_re, a_im, bb_re, bb_im, cc_re, cc_im, d_skip, w_glu, b_glu, nb, seq):
    t_tok = nb * seq
    tc = min(256, seq)
    nch = seq // tc
    grp = N_STATE // 4

    def body(u_ref, gs_ref, ar_ref, ai_ref, bbr_ref, bbi_ref, ccr_ref, cci_ref, d_ref, wg_ref, bg_ref,
             xr_ref, xi_ref, y_ref, sg_ref, car_re, car_im):
        @pl.when(pl.program_id(1) == 0)
        def _():
            car_re[...] = jnp.zeros_like(car_re)
            car_im[...] = jnp.zeros_like(car_im)

        u = u_ref[...]
        ub = u.astype(BF16)
        for j in range(4):
            uj = ub[:, j * LANES:(j + 1) * LANES]
            xr_ref[:, j * grp:(j + 1) * grp] = _dot(uj, bbr_ref[j])
            xi_ref[:, j * grp:(j + 1) * grp] = _dot(uj, bbi_ref[j])
        _scan_chunk(xr_ref, xi_ref, ar_ref, ai_ref, car_re, car_im, tc, reverse=False)
        for j in range(4):
            xr = xr_ref[:, j * grp:(j + 1) * grp].astype(BF16)
            xi = xi_ref[:, j * grp:(j + 1) * grp].astype(BF16)
            y_ref[:, j * LANES:(j + 1) * LANES] = _dot(xr, ccr_ref[j]) - _dot(xi, cci_ref[j])
        y = y_ref[...] + d_ref[...] * u
        y_ref[...] = y
        yg, _ = _gelu_and_grad(y)
        gl = _dot(yg.astype(BF16), wg_ref[...]) + bg_ref[...]
        gs = gs_ref[...]
        sg_ref[...] = (yg * _sig(gl) * gs * _sig(gs)).astype(BF16)

    umap = lambda b, ch: (b * nch + ch, 4)
    gmap = lambda b, ch: (b * nch + ch, 5)
    row = lambda b, ch: (b * nch + ch, 0)
    s = jax.ShapeDtypeStruct
    consts = [a_re, a_im, bb_re, bb_im, cc_re, cc_im, d_skip, w_glu, b_glu]
    return pl.pallas_call(
        body, name="ssm_fwd", grid=(nb, nch),
        out_shape=[s((t_tok, N_STATE), F32), s((t_tok, N_STATE), F32), s((t_tok, SSM_W), F32),
                   s((t_tok, SSM_W), BF16)],
        in_specs=[pl.BlockSpec((tc, SSM_W), umap), pl.BlockSpec((tc, SSM_W), gmap)] + [_full(c.shape) for c in consts],
        out_specs=[pl.BlockSpec((tc, N_STATE), row), pl.BlockSpec((tc, N_STATE), row),
                   pl.BlockSpec((tc, SSM_W), row), pl.BlockSpec((tc, SSM_W), row)],
        scratch_shapes=[pltpu.VMEM((SUBLANES, N_STATE), F32), pltpu.VMEM((SUBLANES, N_STATE), F32)],
        compiler_params=_params(2, VMEM_LIMIT_V7X),
    )(z, z, *consts)


def _tail(x2, tg2, ag, sg, p2, w_out, w_g, w_p, g_ple):
    t_tok = x2.shape[0]
    tm = min(256, t_tok)
    nt = t_tok // tm
    half = ATTN_W

    def body(x_ref, tg_ref, ag_ref, sg_ref, p_ref, wo_ref, wg_ref, wp_ref, gp_ref,
             dmix_ref, dh1_ref, loss_ref, dgp_ref, dwo_ref, dwg_ref, dwp_ref, acc_o, acc_g, acc_p):
        i = pl.program_id(0)

        @pl.when(i == 0)
        def _():
            loss_ref[...] = jnp.zeros_like(loss_ref)
            dgp_ref[...] = jnp.zeros_like(dgp_ref)
            acc_o[...] = jnp.zeros_like(acc_o)
            acc_g[...] = jnp.zeros_like(acc_g)
            acc_p[...] = jnp.zeros_like(acc_p)

        ag_t, sg_t = ag_ref[...], sg_ref[...]
        h1 = x_ref[...] + _dot(ag_t, wo_ref[0:half, :]) + _dot(sg_t, wo_ref[half:2 * half, :])
        r2 = lax.rsqrt(jnp.mean(h1 * h1, axis=-1, keepdims=True) + EPS)
        hnorm = h1 * r2
        gp = gp_ref[...]
        hn = (hnorm * gp).astype(BF16)
        gate = _sig(_dot(hn, wg_ref[...]))
        pb = p_ref[...].astype(BF16)
        pp = jnp.concatenate([_dot(pb, wp_ref[j]) for j in range(N_DEV)], axis=-1)
        h2 = h1 + gate * pp
        err = h2 - tg_ref[...]
        loss_ref[...] += 0.5 * jnp.sum(err * err) * (1.0 / D_MODEL)
        dh2 = err * (1.0 / D_MODEL)
        dpp = (dh2 * gate).astype(BF16)
        dgpre = (dh2 * pp * gate * (1.0 - gate)).astype(BF16)
        acc_p[...] += _dot_tn(pb, dpp)
        acc_g[...] += _dot_tn(hn, dgpre)
        dhn = _dot_nt(dgpre, wg_ref[...])
        dgp_ref[...] += jnp.sum(dhn * hnorm, axis=0, keepdims=True)
        a = dhn * gp
        dh1 = dh2 + r2 * (a - hnorm * jnp.mean(a * hnorm, axis=-1, keepdims=True))
        dh1_ref[...] = dh1
        dh1b = dh1.astype(BF16)
        acc_o[0:half, :] += _dot_tn(ag_t, dh1b)
        acc_o[half:2 * half, :] += _dot_tn(sg_t, dh1b)
        dmix_ref[...] = _dot_nt(dh1b, wo_ref[...])

        @pl.when(i == nt - 1)
        def _():
            dwo_ref[...] = acc_o[...].astype(BF16)
            dwg_ref[...] = acc_g[...].astype(BF16)
            for j in range(N_DEV):
                dwp_ref[j] = acc_p[:, j * LANES:(j + 1) * LANES].astype(BF16)

    row = lambda i: (i, 0)
    s = jax.ShapeDtypeStruct
    return pl.pallas_call(
        body, name="tail_fwd_bwd", grid=(nt,),
        out_shape=[s((t_tok, D_MODEL), F32), s((t_tok, D_MODEL), F32), s((SUBLANES, LANES), F32),
                   s((1, D_MODEL), F32), s((D_MODEL, D_MODEL), BF16), s((D_MODEL, D_MODEL), BF16),
                   s((N_DEV, PLE_DIM, LANES), BF16)],
        in_specs=[pl.BlockSpec((tm, D_MODEL), row), pl.BlockSpec((tm, D_MODEL), row),
                  pl.BlockSpec((tm, half), row), pl.BlockSpec((tm, half), row), pl.BlockSpec((tm, PLE_DIM), row),
                  _full(w_out.shape), _full(w_g.shape), _full(w_p.shape), _full(g_ple.shape)],
        out_specs=[pl.BlockSpec((tm, D_MODEL), row), pl.BlockSpec((tm, D_MODEL), row), _full((SUBLANES, LANES)),
                   _full((1, D_MODEL)), _full((D_MODEL, D_MODEL)), _full((D_MODEL, D_MODEL)),
                   _full((N_DEV, PLE_DIM, LANES))],
        scratch_shapes=[pltpu.VMEM((D_MODEL, D_MODEL), F32), pltpu.VMEM((D_MODEL, D_MODEL), F32),
                        pltpu.VMEM((PLE_DIM, D_MODEL), F32)],
        compiler_params=_params(1, VMEM_LIMIT_V7X),
    )(x2, tg2, ag, sg, p2, w_out, w_g, w_p, g_ple)


def _attn_bwd(qh, kh, vb, z, o, lse, dmix, nb, seq):
    t_tok = nb * seq

    def body(q_ref, k_ref, v_ref, ga_ref, o_ref, l_ref, da_ref, dq_ref, dk_ref, dv_ref, dga_ref,
             qf, kf, vf, dof, dlf):
        ga, o_t, da = ga_ref[...], o_ref[...], da_ref[...]
        sga = _sig(ga)
        d_o = da * ga * sga
        dga_ref[...] = da * o_t * sga * (1.0 + ga * (1.0 - sga))
        lane = lax.broadcasted_iota(jnp.int32, (1, LANES), 1)
        d_oo = d_o * o_t
        delta = jnp.zeros_like(d_oo)
        for h in range(LANES // HEAD_DIM):
            lm2 = (lane // HEAD_DIM) == h
            delta = jnp.where(lm2, jnp.sum(jnp.where(lm2, d_oo, 0.0), axis=-1, keepdims=True), delta)
        qf[...] = q_ref[...].astype(F32)
        kf[...] = k_ref[...].astype(F32)
        vf[...] = v_ref[...].astype(F32)
        dof[...] = d_o
        dlf[...] = delta
        dq_ref[...] = jnp.zeros_like(dq_ref)
        dk_ref[...] = jnp.zeros_like(dk_ref)
        dv_ref[...] = jnp.zeros_like(dv_ref)
        lms = _head_masks()
        for window, dil in DILATED:
            nt, tq = _window_tiling(seq, window, dil)
            q3 = _gather_classes(qf, dil, nt, tq, BF16)
            k3 = _gather_classes(kf, dil, nt, tq, BF16)
            v3 = _gather_classes(vf, dil, nt, tq, BF16)
            do3 = _gather_classes(dof, dil, nt, tq, BF16)
            lt3 = _gather_classes(l_ref, dil, nt, tq, F32)
            dl3 = _gather_classes(dlf, dil, nt, tq, F32)
            if nt > 1:
                k3, v3 = _with_prev_tile(k3, dil, nt), _with_prev_tile(v3, dil, nt)
            valid = _band_valid(dil, nt, tq)
            dq = jnp.zeros(q3.shape, F32)
            dk = jnp.zeros(k3.shape, F32)
            dv = jnp.zeros(k3.shape, F32)
            for lm in lms:
                qm = jnp.where(lm, q3, jnp.zeros_like(q3))
                dom = jnp.where(lm, do3, jnp.zeros_like(do3))
                p = jnp.where(valid, jnp.exp(_bqk(qm, k3) - _head_col(lt3, lm)), 0.0)
                dv = dv + _bkd(p.astype(BF16), dom)
                ds = (p * (_bqk(dom, v3) - _head_col(dl3, lm))).astype(BF16)
                dq = dq + jnp.where(lm, _bqd(ds, k3), 0.0)
                dk = dk + _bkd(ds, qm)
            _scatter_classes(dq_ref, dq, dil, nt, tq, add=True)
            for ref, g in ((dk_ref, dk), (dv_ref, dv)):
                if nt > 1:
                    own, prev = g[:, tq:, :], g[:, :tq, :]
                    parts = []
                    for r in range(dil):
                        t = prev[r * nt:(r + 1) * nt]
                        parts.append(jnp.concatenate([t[1:], jnp.zeros_like(t[:1])], axis=0))
                    g = own + (parts[0] if dil == 1 else jnp.concatenate(parts, axis=0))
                _scatter_classes(ref, g, dil, nt, tq, add=True)

    blk = pl.BlockSpec((seq, LANES), lambda b, hp: (b, hp))
    return pl.pallas_call(
        body, name="attn_bwd", grid=(nb, ATTN_W // LANES),
        out_shape=[jax.ShapeDtypeStruct((t_tok, ATTN_W), F32)] * 4,
        in_specs=[blk, blk, blk, pl.BlockSpec((seq, LANES), lambda b, hp: (b, 3 * ATTN_W // LANES + hp)), blk, blk,
                  blk],
        out_specs=[blk] * 4,
        scratch_shapes=[pltpu.VMEM((seq, LANES), F32)] * 5,
        compiler_params=_params(2, VMEM_LIMIT_V7X),
    )(qh, kh, vb, z, o, lse, dmix)


def _ssm_bwd(z, dmix, y, x_re, x_im, a_re, a_im, bb_re, bb_im, cc_re, cc_im, d_skip, w_glu, b_glu, nb, seq):
    t_tok = nb * seq
    tc = min(256, seq)
    nch = seq // tc
    grp = N_STATE // 4

    def body(u_ref, gs_ref, ds_ref, y_ref, xr_ref, xi_ref, xpr_ref, xpi_ref,
             ar_ref, ai_ref, bbr_ref, bbi_ref, ccr_ref, cci_ref, d_ref, wg_ref, bg_ref,
             du_ref, dgs_ref, dwg_ref, dbg_ref, dd_ref, dar_ref, dai_ref, dbbr_ref, dbbi_ref, dccr_ref, dcci_ref,
             lam_re, lam_im, car_re, car_im, acc_wg):
        step = pl.program_id(1)
        first_chunk = step == nch - 1

        @pl.when((pl.program_id(0) == 0) & (step == 0))
        def _():
            acc_wg[...] = jnp.zeros_like(acc_wg)
            for ref in (dbg_ref, dd_ref, dar_ref, dai_ref, dbbr_ref, dbbi_ref, dccr_ref, dcci_ref):
                ref[...] = jnp.zeros_like(ref)

        @pl.when(step == 0)
        def _():
            car_re[...] = jnp.zeros_like(car_re)
            car_im[...] = jnp.zeros_like(car_im)

        u, gs, dssm, y = u_ref[...], gs_ref[...], ds_ref[...], y_ref[...]
        yg, dgelu = _gelu_and_grad(y)
        ygb = yg.astype(BF16)
        sgl = _sig(_dot(ygb, wg_ref[...]) + bg_ref[...])
        sgs = _sig(gs)
        dout = dssm * gs * sgs
        dgs_ref[...] = dssm * yg * sgl * sgs * (1.0 + gs * (1.0 - sgs))
        dgl = dout * yg * sgl * (1.0 - sgl)
        dglb = dgl.astype(BF16)
        dyg = dout * sgl + _dot_nt(dglb, wg_ref[...])
        acc_wg[...] += _dot_tn(ygb, dglb)
        dbg_ref[...] += jnp.sum(dgl, axis=0, keepdims=True)
        dy = dyg * dgelu
        dd_ref[...] += jnp.sum(dy * u, axis=0, keepdims=True)
        dyb = dy.astype(BF16)
        ub = u.astype(BF16)
        for j in range(4):
            dyj = dyb[:, j * LANES:(j + 1) * LANES]
            sl = slice(j * grp, (j + 1) * grp)
            lam_re[:, sl] = _dot_nt(dyj, ccr_ref[j])
            lam_im[:, sl] = -_dot_nt(dyj, cci_ref[j])
            dccr_ref[j] += _dot_tn(xr_ref[:, sl].astype(BF16), dyj)
            dcci_ref[j] -= _dot_tn(xi_ref[:, sl].astype(BF16), dyj)

        keep_prev = jnp.where(first_chunk, 0.0, 1.0)
        rowi = lax.broadcasted_iota(jnp.int32, (SUBLANES, SCAN_COLS), 0)

        def visit(cols, blk, off, lr, li, acc):
            if blk is None:
                dar_ref[:, cols] += jnp.sum(acc[0], axis=0, keepdims=True)
                dai_ref[:, cols] += jnp.sum(acc[1], axis=0, keepdims=True)
                return None
            xr = xr_ref[pl.ds(off, SUBLANES), cols]
            xi = xi_ref[pl.ds(off, SUBLANES), cols]
            poff = pl.multiple_of(jnp.maximum(blk - 1, 0) * SUBLANES, SUBLANES)
            inside = blk > 0
            pr = jnp.where(inside, xr_ref[pl.ds(poff, SUBLANES), cols], xpr_ref[:, cols] * keep_prev)
            pi = jnp.where(inside, xi_ref[pl.ds(poff, SUBLANES), cols], xpi_ref[:, cols] * keep_prev)
            last = SUBLANES - 1
            pr = jnp.broadcast_to(pr[last:last + 1, :], (SUBLANES, SCAN_COLS))
            pi = jnp.broadcast_to(pi[last:last + 1, :], (SUBLANES, SCAN_COLS))
            xpr = jnp.where(rowi == 0, pr, pltpu.roll(xr, 1, 0))
            xpi = jnp.where(rowi == 0, pi, pltpu.roll(xi, 1, 0))
            return acc[0] + lr * xpr + li * xpi, acc[1] + li * xpr - lr * xpi

        _scan_chunk(lam_re, lam_im, ar_ref, ai_ref, car_re, car_im, tc, reverse=True, visit=visit)

        for j in range(4):
            sl = slice(j * grp, (j + 1) * grp)
            lr = lam_re[:, sl].astype(BF16)
            li = lam_im[:, sl].astype(BF16)
            uj = ub[:, j * LANES:(j + 1) * LANES]
            du_ref[:, j * LANES:(j + 1) * LANES] = (
                _dot_nt(lr, bbr_ref[j]) + _dot_nt(li, bbi_ref[j])
                + dy[:, j * LANES:(j + 1) * LANES] * d_ref[:, j * LANES:(j + 1) * LANES])
            dbbr_ref[j] += _dot_tn(uj, lr)
            dbbi_ref[j] += _dot_tn(uj, li)

        @pl.when((pl.program_id(0) == nb - 1) & (step == nch - 1))
        def _():
            dwg_ref[...] = acc_wg[...].astype(BF16)

    rev = lambda b, ch: b * nch + (nch - 1 - ch)
    umap = lambda b, ch: (rev(b, ch), 4)
    gmap = lambda b, ch: (rev(b, ch), 5)
    smap = lambda b, ch: (rev(b, ch), 1)
    row = lambda b, ch: (rev(b, ch), 0)
    prev = lambda b, ch: (jnp.maximum(rev(b, ch) * (tc // SUBLANES) - 1, 0), 0)
    s = jax.ShapeDtypeStruct
    consts = [a_re, a_im, bb_re, bb_im, cc_re, cc_im, d_skip, w_glu, b_glu]
    acc_shapes = [s((1, SSM_W), F32), s((1, SSM_W), F32), s((1, N_STATE), F32), s((1, N_STATE), F32),
                  s(bb_re.shape, F32), s(bb_re.shape, F32), s(cc_re.shape, F32), s(cc_re.shape, F32)]
    return pl.pallas_call(
        body, name="ssm_bwd", grid=(nb, nch),
        out_shape=[s((t_tok, SSM_W), F32), s((t_tok, SSM_W), F32), s((SSM_W, SSM_W), BF16)] + acc_shapes,
        in_specs=[pl.BlockSpec((tc, SSM_W), umap), pl.BlockSpec((tc, SSM_W), gmap), pl.BlockSpec((tc, SSM_W), smap),
                  pl.BlockSpec((tc, SSM_W), row), pl.BlockSpec((tc, N_STATE), row), pl.BlockSpec((tc, N_STATE), row),
                  pl.BlockSpec((SUBLANES, N_STATE), prev), pl.BlockSpec((SUBLANES, N_STATE), prev)]
        + [_full(c.shape) for c in consts],
        out_specs=[pl.BlockSpec((tc, SSM_W), row), pl.BlockSpec((tc, SSM_W), row), _full((SSM_W, SSM_W))]
        + [_full(a.shape) for a in acc_shapes],
        scratch_shapes=[pltpu.VMEM((tc, N_STATE), F32), pltpu.VMEM((tc, N_STATE), F32),
                        pltpu.VMEM((SUBLANES, N_STATE), F32), pltpu.VMEM((SUBLANES, N_STATE), F32),
                        pltpu.VMEM((SSM_W, SSM_W), F32)],
        compiler_params=_params(2, VMEM_LIMIT_V7X),
    )(z, z, dmix, y, x_re, x_im, x_re, x_im, *consts)


def _dz_and_dx(x2, z, dqh, dkh, dvb, dga, du, dgs, dh1, w_in_g, g_mix, gq_t, gk_t, ones_bd, fold):
    t_tok = x2.shape[0]
    tm = min(256, t_tok)
    nt = t_tok // tm
    a_w = ATTN_W

    def head_norm_bwd(raw, d_hat, gain, scale, ones):
        r = lax.rsqrt(_hdot(raw * raw, ones) * (1.0 / HEAD_DIM) + EPS)
        n = raw * r
        a = d_hat * gain * scale
        d_raw = r * (a - n * (_hdot(a * n, ones) * (1.0 / HEAD_DIM)))
        return d_raw, jnp.sum(d_hat * n * scale, axis=0, keepdims=True)

    def body(x_ref, q_ref, k_ref, dq_ref, dk_ref, dv_ref, dga_ref, du_ref, dgs_ref, dh1_ref, w_ref, g_ref,
             gq_ref, gk_ref, ones_ref, fold_ref, dz_ref, gx_ref, dgm_ref, dgq_ref, dgk_ref, acc_q, acc_k):
        i = pl.program_id(0)

        @pl.when(i == 0)
        def _():
            dgm_ref[...] = jnp.zeros_like(dgm_ref)
            acc_q[...] = jnp.zeros_like(acc_q)
            acc_k[...] = jnp.zeros_like(acc_k)

        ones = ones_ref[...]
        dq, sq = head_norm_bwd(q_ref[...], dq_ref[...], gq_ref[...], HEAD_DIM ** -0.5, ones)
        dk, sk = head_norm_bwd(k_ref[...], dk_ref[...], gk_ref[...], 1.0, ones)
        acc_q[...] += jnp.broadcast_to(sq, acc_q.shape)
        acc_k[...] += jnp.broadcast_to(sk, acc_k.shape)
        parts = (dq, dk, dv_ref[...], dga_ref[...], du_ref[...], dgs_ref[...])
        for n, part in enumerate(parts):
            dz_ref[:, n * a_w:(n + 1) * a_w] = part.astype(BF16)
        dxn = jnp.zeros((tm, D_MODEL), F32)
        for j in range(N_DEV):
            dxn = dxn + _dot_nt(dz_ref[:, j * COL_W:(j + 1) * COL_W], w_ref[j])
        x = x_ref[...]
        r1 = lax.rsqrt(jnp.mean(x * x, axis=-1, keepdims=True) + EPS)
        xnorm = x * r1
        dgm_ref[...] += jnp.sum(dxn * xnorm, axis=0, keepdims=True)
        a = dxn * g_ref[...]
        gx_ref[...] = dh1_ref[...] + r1 * (a - xnorm * jnp.mean(a * xnorm, axis=-1, keepdims=True))

        @pl.when(i == nt - 1)
        def _():
            dgq_ref[...] = _hdot(acc_q[...], fold_ref[...])
            dgk_ref[...] = _hdot(acc_k[...], fold_ref[...])

    row = lambda i: (i, 0)
    col = lambda n: (lambda i: (i, n))
    s = jax.ShapeDtypeStruct
    half = pl.BlockSpec((tm, a_w), row)
    return pl.pallas_call(
        body, name="dz_dx", grid=(nt,),
        out_shape=[s((t_tok, IN_W), BF16), s((t_tok, D_MODEL), F32), s((1, D_MODEL), F32),
                   s((SUBLANES, HEAD_DIM), F32), s((SUBLANES, HEAD_DIM), F32)],
        in_specs=[pl.BlockSpec((tm, D_MODEL), row), pl.BlockSpec((tm, a_w), col(0)), pl.BlockSpec((tm, a_w), col(1)),
                  half, half, half, half, half, half, pl.BlockSpec((tm, D_MODEL), row),
                  _full(w_in_g.shape), _full(g_mix.shape), _full(gq_t.shape), _full(gk_t.shape),
                  _full(ones_bd.shape), _full(fold.shape)],
        out_specs=[pl.BlockSpec((tm, IN_W), row), pl.BlockSpec((tm, D_MODEL), row), _full((1, D_MODEL)),
                   _full((SUBLANES, HEAD_DIM)), _full((SUBLANES, HEAD_DIM))],
        scratch_shapes=[pltpu.VMEM((SUBLANES, a_w), F32), pltpu.VMEM((SUBLANES, a_w), F32)],
        compiler_params=_params(1, VMEM_LIMIT_V7X),
    )(x2, z, z, dqh, dkh, dvb, dga, du, dgs, dh1, w_in_g, g_mix, gq_t, gk_t, ones_bd, fold)


def _dw_in(xn, dz):
    t_tok = xn.shape[0]
    tk = min(1024, t_tok)
    nk = t_tok // tk

    def body(xn_ref, dz_ref, out_ref, acc):
        k = pl.program_id(1)

        @pl.when(k == 0)
        def _():
            acc[...] = jnp.zeros_like(acc)

        acc[...] += _dot_tn(xn_ref[...], dz_ref[...])

        @pl.when(k == nk - 1)
        def _():
            out_ref[0] = acc[...].astype(BF16)

    return pl.pallas_call(
        body, name="dw_in", grid=(N_DEV, nk),
        out_shape=jax.ShapeDtypeStruct((N_DEV, D_MODEL, COL_W), BF16),
        in_specs=[pl.BlockSpec((tk, D_MODEL), lambda j, k: (k, 0)), pl.BlockSpec((tk, COL_W), lambda j, k: (k, j))],
        out_specs=pl.BlockSpec((1, D_MODEL, COL_W), lambda j, k: (j, 0, 0)),
        scratch_shapes=[pltpu.VMEM((D_MODEL, COL_W), F32)],
        compiler_params=_params(2, VMEM_LIMIT_V7X),
    )(xn, dz)


SMALL = ("mix_norm", "q_norm", "k_norm", "lambda_re", "lambda_im", "log_dt", "b_re", "b_im", "c_re", "c_im",
         "d_skip", "b_glu", "ple_norm")
BIG = ("w_in", "w_glu", "w_out", "w_ple_gate", "w_ple_proj")
WEIGHTS = ("mix_norm", "w_in", "q_norm", "k_norm", "lambda_re", "lambda_im", "log_dt", "b_re", "b_im", "c_re",
           "c_im", "d_skip", "w_glu", "b_glu", "w_out", "ple_norm", "w_ple_gate", "w_ple_proj")


def _pack(arrs):
    flat = jnp.concatenate([a.reshape(-1).astype(F32) for a in arrs])
    rows = -(-flat.shape[0] // (64 * LANES)) * 64
    return jnp.pad(flat, (0, rows * LANES - flat.shape[0])).reshape(rows, LANES)


def _unpack(packed, shapes):
    flat = packed.reshape(-1)
    out, off = [], 0
    for shp in shapes:
        size = math.prod(shp)
        out.append(flat[off:off + size].reshape(shp))
        off += size
    return out


def kernel(x, p, mix_norm, w_in, q_norm, k_norm, lambda_re, lambda_im, log_dt, b_re, b_im, c_re, c_im, d_skip, w_glu, b_glu, w_out, ple_norm, w_ple_gate, w_ple_proj, loss_target, m_mix_norm, m_w_in, m_q_norm, m_k_norm, m_lambda_re, m_lambda_im, m_log_dt, m_b_re, m_b_im, m_c_re, m_c_im, m_d_skip, m_w_glu, m_b_glu, m_w_out, m_ple_norm, m_w_ple_gate, m_w_ple_proj, v_mix_norm, v_w_in, v_q_norm, v_k_norm, v_lambda_re, v_lambda_im, v_log_dt, v_b_re, v_b_im, v_c_re, v_c_im, v_d_skip, v_w_glu, v_b_glu, v_w_out, v_ple_norm, v_w_ple_gate, v_w_ple_proj):
    env = dict(locals())
    w = {n: env[n] for n in WEIGHTS}
    m = {n: env["m_" + n] for n in WEIGHTS}
    v = {n: env["v_" + n] for n in WEIGHTS}
    nb, seq, _ = x.shape
    t_tok = nb * seq
    x2 = x.reshape(t_tok, D_MODEL)
    tg2 = loss_target.reshape(t_tok, D_MODEL)
    p2 = p.reshape(t_tok, PLE_DIM)

    shard2d = {"w_in": (D_MODEL, COL_W), "w_glu": (SSM_W // N_DEV, SSM_W), "w_out": (D_MODEL // N_DEV, D_MODEL),
               "w_ple_gate": (D_MODEL // N_DEV, D_MODEL), "w_ple_proj": (PLE_DIM, D_MODEL // N_DEV)}
    w_sh = [w[n].reshape(shard2d[n]) for n in BIG]
    w_in_g, w_glu_g, w_out_g, w_g_g, w_p_g = _all_gather(w_sh, [BF16] * len(BIG), "gather_weights")
    w_glu_f = w_glu_g.reshape(SSM_W, SSM_W)
    w_out_f = w_out_g.reshape(D_MODEL, D_MODEL)
    w_g_f = w_g_g.reshape(D_MODEL, D_MODEL)

    g3 = (SSM_GROUPS, 1, SSM_STATE)
    lr3, li3 = lambda_re.reshape(g3), lambda_im.reshape(g3)
    dt3 = log_dt.reshape(SSM_GROUPS, 1, 1)
    btr = b_re[0].transpose(0, 2, 1)
    bti = b_im[0].transpose(0, 2, 1)
    a_re3, a_im3, bbr, bbi = _zoh_fwd(lr3, li3, dt3, btr, bti)
    a_re, a_im = a_re3.reshape(1, N_STATE), a_im3.reshape(1, N_STATE)
    bb_re, bb_im = _blockdiag(bbr).astype(BF16), _blockdiag(bbi).astype(BF16)
    cc_re = _blockdiag(c_re[0].transpose(0, 2, 1)).astype(BF16)
    cc_im = _blockdiag(c_im[0].transpose(0, 2, 1)).astype(BF16)

    ones_bd = _head_ones()
    fold = jnp.tile(jnp.eye(HEAD_DIM, dtype=F32), (ATTN_W // HEAD_DIM, 1))
    gq_t = jnp.tile(q_norm, (1, ATTN_W // HEAD_DIM))
    gk_t = jnp.tile(k_norm, (1, ATTN_W // HEAD_DIM))

    z, qh, kh, vb, xn = _in_proj(x2, mix_norm, w_in_g, ones_bd, gq_t, gk_t)
    o, lse, ag = _attn_fwd(qh, kh, vb, z, nb, seq)
    x_re, x_im, y, sg = _ssm_fwd(z, a_re, a_im, bb_re, bb_im, cc_re, cc_im, d_skip, w_glu_f, b_glu, nb, seq)
    dmix, dh1, loss_t, d_ple, dw_out, dw_g, dw_p = _tail(x2, tg2, ag, sg, p2, w_out_f, w_g_f, w_p_g, ple_norm)

    dqh, dkh, dvb, dga = _attn_bwd(qh, kh, vb, z, o, lse, dmix, nb, seq)
    (du, dgs, dw_glu, d_bglu, d_dskip, da_re, da_im, dbb_re, dbb_im, dcc_re, dcc_im) = _ssm_bwd(
        z, dmix, y, x_re, x_im, a_re, a_im, bb_re, bb_im, cc_re, cc_im, d_skip, w_glu_f, b_glu, nb, seq)
    dz, gx, d_mix, d_gq, d_gk = _dz_and_dx(x2, z, dqh, dkh, dvb, dga, du, dgs, dh1, w_in_g, mix_norm, gq_t, gk_t,
                                           ones_bd, fold)
    dw_in = _dw_in(xn, dz)

    d_lr, d_li, d_dt, d_btr, d_bti = _zoh_bwd(
        lr3, li3, dt3, btr, bti, da_re.reshape(g3), da_im.reshape(g3),
        _blockdiag_extract(dbb_re, SSM_GROUP, SSM_STATE), _blockdiag_extract(dbb_im, SSM_GROUP, SSM_STATE))
    small_g = {
        "mix_norm": d_mix, "q_norm": d_gq[0:1], "k_norm": d_gk[0:1], "lambda_re": d_lr, "lambda_im": d_li,
        "log_dt": d_dt, "b_re": d_btr.transpose(0, 2, 1), "b_im": d_bti.transpose(0, 2, 1),
        "c_re": _blockdiag_extract(dcc_re, SSM_STATE, SSM_GROUP).transpose(0, 2, 1),
        "c_im": _blockdiag_extract(dcc_im, SSM_STATE, SSM_GROUP).transpose(0, 2, 1),
        "d_skip": d_dskip, "b_glu": d_bglu, "ple_norm": d_ple}

    parts = [dw_in, dw_glu.reshape(N_DEV, SSM_W // N_DEV, SSM_W), dw_out.reshape(N_DEV, D_MODEL // N_DEV, D_MODEL),
             dw_g.reshape(N_DEV, D_MODEL // N_DEV, D_MODEL), dw_p]
    g_sh = _reduce_scatter(parts, "scatter_grads")
    d_sh, m_sh, v_sh = _adamw_shards(g_sh, w_sh, [m[n].reshape(shard2d[n]) for n in BIG],
                                     [v[n].reshape(shard2d[n]) for n in BIG])

    (gathered,) = _all_gather([_pack([small_g[n] for n in SMALL])], [F32], "gather_small_grads")
    g_pk, d_pk, m_pk, v_pk = _small_update(gathered, _pack([w[n] for n in SMALL]), _pack([m[n] for n in SMALL]),
                                           _pack([v[n] for n in SMALL]))

    grads, deltas, new_m, new_v = {}, {}, {}, {}
    small_shapes = [w[n].shape for n in SMALL]
    for dst, packed in ((grads, g_pk), (deltas, d_pk), (new_m, m_pk), (new_v, v_pk)):
        for n, a in zip(SMALL, _unpack(packed, small_shapes)):
            dst[n] = a
    for i, n in enumerate(BIG):
        grads[n] = g_sh[i].reshape(w[n].shape)
        deltas[n] = d_sh[i].reshape(w[n].shape)
        new_m[n] = m_sh[i].reshape(w[n].shape)
        new_v[n] = v_sh[i].reshape(w[n].shape)

    loss = lax.psum(loss_t[0, 0], AXES)
    return (loss, gx.reshape(x.shape), *[grads[n] for n in WEIGHTS], *[deltas[n] for n in WEIGHTS],
            *[new_m[n] for n in WEIGHTS], *[new_v[n] for n in WEIGHTS])
```

```python
import math

import jax
import jax.numpy as jnp
from jax import lax
from jax.experimental import pallas as pl
from jax.experimental.pallas import tpu as pltpu

F32 = jnp.float32
BF16 = jnp.bfloat16
MESH = pl.DeviceIdType.MESH
AXES = ("x", "y", "c")
N_DEV = 8

D_MODEL = 1024
HEAD_DIM = 64
ATTN_W = 512
SSM_W = 512
SSM_GROUPS = 32
SSM_GROUP = 16
SSM_STATE = 64
N_STATE = SSM_GROUPS * SSM_STATE
PLE_DIM = 256
IN_W = 3072
COL_W = IN_W // N_DEV
DILATED = ((128, 1), (512, 4), (2048, 16))
EPS = 1e-6
INV_SQRT2 = 1.0 / math.sqrt(2.0)
INV_SQRT_2PI = 1.0 / math.sqrt(2.0 * math.pi)

ADAM_LR, ADAM_B1, ADAM_B2, ADAM_EPS, ADAM_WD, ADAM_STEP = 0.001, 0.9, 0.999, 1e-08, 0.01, 10

VMEM_LIMIT_V7X = 56 * 1024 * 1024
SUBLANES = 8
LANES = 128


def _params(n_axes=None, vmem=None):
    kw = {}
    if n_axes:
        kw["dimension_semantics"] = ("arbitrary",) * n_axes
    if vmem:
        kw["vmem_limit_bytes"] = vmem
    return pltpu.CompilerParams(**kw)


def _dot(a, b):
    return jnp.dot(a, b, preferred_element_type=F32)


def _dot_nt(a, b):
    return lax.dot_general(a, b, (((1,), (1,)), ((), ())), preferred_element_type=F32)


def _dot_tn(a, b):
    return lax.dot_general(a, b, (((0,), (0,)), ((), ())), preferred_element_type=F32)


def _hdot(a, b):
    return jnp.dot(a, b, precision=lax.Precision.HIGHEST, preferred_element_type=F32)


def _sig(x):
    return 1.0 / (1.0 + jnp.exp(-x))


def _gelu_and_grad(y):
    cdf = 0.5 * (1.0 + lax.erf(y * INV_SQRT2))
    pdf = jnp.exp(-0.5 * y * y) * INV_SQRT_2PI
    return y * cdf, cdf + y * pdf


def _vmem():
    return pl.BlockSpec(memory_space=pltpu.VMEM)


def _full(shape):
    nd = len(shape)
    return pl.BlockSpec(shape, lambda *_: (0,) * nd)


def _all_gather(shards, out_dtypes, name):
    n = len(shards)

    def body(*refs):
        in_refs, out_refs = refs[:n], refs[n:2 * n]
        send_sems, recv_sems = refs[2 * n], refs[2 * n + 1]
        x, y, c = lax.axis_index("x"), lax.axis_index("y"), lax.axis_index("c")
        me, sibling = (x, y, c), (x, y, 1 - c)
        chips = [(1 - x, y), (x, 1 - y), (1 - x, 1 - y)]

        def idx(px, py, pc):
            return 4 * px + 2 * py + pc

        def copy(i, k, block, to):
            ref = out_refs[i].at[idx(*block)]
            return pltpu.make_async_remote_copy(
                src_ref=ref, dst_ref=ref, send_sem=send_sems.at[7 * i + k], recv_sem=recv_sems.at[7 * i + k],
                device_id=to, device_id_type=MESH)

        for i in range(n):
            out_refs[i][idx(*me)] = in_refs[i][...].astype(out_refs[i].dtype)
        first = []
        for i in range(n):
            first.append(copy(i, 0, me, sibling))
            first += [copy(i, 1 + j, me, (*chip, c)) for j, chip in enumerate(chips)]
        for cp in first:
            cp.start()
        passed = []
        for j, chip in enumerate(chips):
            for i in range(n):
                copy(i, 1 + j, (*chip, c), me).wait_recv()
                cp = copy(i, 4 + j, (*chip, c), sibling)
                cp.start()
                passed.append(cp)
        for i in range(n):
            copy(i, 0, sibling, me).wait_recv()
            for j, chip in enumerate(chips):
                copy(i, 4 + j, (*chip, 1 - c), me).wait_recv()
        for cp in first + passed:
            cp.wait_send()

    return pl.pallas_call(
        body, name=name,
        out_shape=[jax.ShapeDtypeStruct((N_DEV,) + s.shape, dt) for s, dt in zip(shards, out_dtypes)],
        in_specs=[_vmem()] * n, out_specs=[_vmem()] * n,
        scratch_shapes=[pltpu.SemaphoreType.DMA((7 * n,)), pltpu.SemaphoreType.DMA((7 * n,))],
        compiler_params=_params(vmem=VMEM_LIMIT_V7X),
    )(*shards)


def _row_chunks(rows):
    chunk = 64 if rows % 64 == 0 else rows
    return chunk, rows // chunk


def _reduce_scatter(parts, name):
    n = len(parts)

    def body(*refs):
        in_refs, out_refs, recv_refs = refs[:n], refs[n:2 * n], refs[2 * n:3 * n]
        send_sems, recv_sems = refs[3 * n], refs[3 * n + 1]
        x, y, c = lax.axis_index("x"), lax.axis_index("y"), lax.axis_index("c")
        my = 4 * x + 2 * y + c
        copies = []
        for i in range(n):
            for m in range(1, N_DEV):
                px = 1 - x if m & 4 else x
                py = 1 - y if m & 2 else y
                pc = 1 - c if m & 1 else c
                cp = pltpu.make_async_remote_copy(
                    src_ref=in_refs[i].at[4 * px + 2 * py + pc], dst_ref=recv_refs[i].at[m - 1],
                    send_sem=send_sems.at[7 * i + m - 1], recv_sem=recv_sems.at[7 * i + m - 1],
                    device_id=(px, py, pc), device_id_type=MESH)
                cp.start()
                copies.append(cp)
        for cp in copies:
            cp.wait_recv()
        for i in range(n):
            rows = parts[i].shape[1]
            chunk, steps = _row_chunks(rows)

            def step(s, carry, i=i, chunk=chunk):
                r = pl.ds(pl.multiple_of(s * chunk, chunk), chunk)
                acc = in_refs[i][my, r, :].astype(F32)
                for m in range(1, N_DEV):
                    acc = acc + recv_refs[i][m - 1, r, :].astype(F32)
                out_refs[i][r, :] = acc
                return carry

            lax.fori_loop(0, steps, step, 0)
        for cp in copies:
            cp.wait_send()

    return pl.pallas_call(
        body, name=name,
        out_shape=[jax.ShapeDtypeStruct(p.shape[1:], F32) for p in parts],
        in_specs=[_vmem()] * n, out_specs=[_vmem()] * n,
        scratch_shapes=[pltpu.VMEM((N_DEV - 1,) + p.shape[1:], p.dtype) for p in parts]
        + [pltpu.SemaphoreType.DMA((7 * n,)), pltpu.SemaphoreType.DMA((7 * n,))],
        compiler_params=_params(vmem=VMEM_LIMIT_V7X),
    )(*parts)


def _adamw_math(w, g, m, v):
    m = ADAM_B1 * m + (1.0 - ADAM_B1) * g
    v = ADAM_B2 * v + (1.0 - ADAM_B2) * (g * g)
    m_hat = m / (1.0 - ADAM_B1 ** ADAM_STEP)
    v_hat = v / (1.0 - ADAM_B2 ** ADAM_STEP)
    delta = -ADAM_LR * (m_hat / (jnp.sqrt(v_hat) + ADAM_EPS) + ADAM_WD * w)
    return delta, m, v


def _adamw_shards(gs, ws, ms, vs):
    n = len(gs)

    def body(*refs):
        g_refs, w_refs, m_refs, v_refs = (refs[k * n:(k + 1) * n] for k in range(4))
        d_out, m_out, v_out = (refs[(4 + k) * n:(5 + k) * n] for k in range(3))
        for i in range(n):
            chunk, steps = _row_chunks(gs[i].shape[0])

            def step(s, carry, i=i, chunk=chunk):
                r = pl.ds(pl.multiple_of(s * chunk, chunk), chunk)
                d, m, v = _adamw_math(w_refs[i][r, :], g_refs[i][r, :], m_refs[i][r, :], v_refs[i][r, :])
                d_out[i][r, :] = d
                m_out[i][r, :] = m
                v_out[i][r, :] = v
                return carry

            lax.fori_loop(0, steps, step, 0)

    shapes = [jax.ShapeDtypeStruct(g.shape, F32) for g in gs]
    outs = pl.pallas_call(
        body, name="adamw_shards", out_shape=shapes * 3,
        in_specs=[_vmem()] * (4 * n), out_specs=[_vmem()] * (3 * n),
        compiler_params=_params(vmem=VMEM_LIMIT_V7X),
    )(*gs, *ws, *ms, *vs)
    return outs[:n], outs[n:2 * n], outs[2 * n:]


def _small_update(gathered, w, m, v):
    rows = w.shape[0]
    chunk, steps = _row_chunks(rows)

    def body(ga_ref, w_ref, m_ref, v_ref, g_out, d_out, m_out, v_out):
        def step(s, carry):
            r = pl.ds(pl.multiple_of(s * chunk, chunk), chunk)
            g = ga_ref[0, r, :]
            for j in range(1, N_DEV):
                g = g + ga_ref[j, r, :]
            d, mm, vv = _adamw_math(w_ref[r, :], g, m_ref[r, :], v_ref[r, :])
            g_out[r, :] = g
            d_out[r, :] = d
            m_out[r, :] = mm
            v_out[r, :] = vv
            return carry

        lax.fori_loop(0, steps, step, 0)

    return pl.pallas_call(
        body, name="small_update", out_shape=[jax.ShapeDtypeStruct(w.shape, F32)] * 4,
        in_specs=[_vmem()] * 4, out_specs=[_vmem()] * 4,
    )(gathered, w, m, v)


def _zoh(lr, li, logdt, btr, bti):
    dt = jnp.exp(logdt)
    mag = jnp.exp(lr * dt)
    th = li * dt
    ar = mag * jnp.cos(th)
    ai = mag * jnp.sin(th)
    den = lr * lr + li * li
    nr = ar - 1.0
    cr = (nr * lr + ai * li) / den
    ci = (ai * lr - nr * li) / den
    return ar, ai, cr * btr - ci * bti, cr * bti + ci * btr


def _zoh_fwd(lr, li, logdt, btr, bti):
    def body(lr_ref, li_ref, dt_ref, br_ref, bi_ref, ar_ref, ai_ref, bbr_ref, bbi_ref):
        ar, ai, bbr, bbi = _zoh(lr_ref[...], li_ref[...], dt_ref[...], br_ref[...], bi_ref[...])
        ar_ref[...] = ar
        ai_ref[...] = ai
        bbr_ref[...] = bbr
        bbi_ref[...] = bbi

    s = jax.ShapeDtypeStruct
    return pl.pallas_call(
        body, name="zoh_fwd",
        out_shape=[s(lr.shape, F32), s(lr.shape, F32), s(btr.shape, F32), s(btr.shape, F32)],
        in_specs=[_vmem()] * 5, out_specs=[_vmem()] * 4,
    )(lr, li, logdt, btr, bti)


def _zoh_bwd(lr, li, logdt, btr, bti, dar, dai, dbbr, dbbi):
    def body(lr_ref, li_ref, dt_ref, br_ref, bi_ref, dar_ref, dai_ref, dbbr_ref, dbbi_ref,
             glr_ref, gli_ref, gdt_ref, gbr_ref, gbi_ref):
        _, vjp = jax.vjp(_zoh, lr_ref[...], li_ref[...], dt_ref[...], br_ref[...], bi_ref[...])
        glr, gli, gdt, gbr, gbi = vjp((dar_ref[...], dai_ref[...], dbbr_ref[...], dbbi_ref[...]))
        glr_ref[...] = glr
        gli_ref[...] = gli
        gdt_ref[...] = gdt
        gbr_ref[...] = gbr
        gbi_ref[...] = gbi

    s = jax.ShapeDtypeStruct
    return pl.pallas_call(
        body, name="zoh_bwd",
        out_shape=[s(lr.shape, F32), s(lr.shape, F32), s(logdt.shape, F32), s(btr.shape, F32), s(btr.shape, F32)],
        in_specs=[_vmem()] * 9, out_specs=[_vmem()] * 5,
    )(lr, li, logdt, btr, bti, dar, dai, dbbr, dbbi)


def _blockdiag(t):
    g, r, s = t.shape
    t = t.reshape(4, 8, r, s)
    out = jnp.einsum("jirs,ik->jirks", t, jnp.eye(8, dtype=t.dtype))
    return out.reshape(4, 8 * r, 8 * s)


def _blockdiag_extract(m, r, s):
    m = m.reshape(4, 8, r, 8, s)
    out = jnp.einsum("jirks,ik->jirs", m, jnp.eye(8, dtype=m.dtype))
    return out.reshape(32, r, s)


def _head_ones():
    r = jnp.arange(ATTN_W) // HEAD_DIM
    return (r[:, None] == r[None, :]).astype(F32)


def _in_proj(x2, g_mix, w_in_g, ones_bd, gq_t, gk_t):
    t_tok = x2.shape[0]
    tm = min(512, t_tok)

    def body(x_ref, g_ref, w_ref, ones_ref, gq_ref, gk_ref, z_ref, qh_ref, kh_ref, vb_ref, xn_ref):
        x = x_ref[...]
        r = lax.rsqrt(jnp.mean(x * x, axis=-1, keepdims=True) + EPS)
        xn = (x * r * g_ref[...]).astype(BF16)
        xn_ref[...] = xn
        for j in range(N_DEV):
            z_ref[:, j * COL_W:(j + 1) * COL_W] = _dot(xn, w_ref[j])
        ones = ones_ref[...]
        q = z_ref[:, 0:ATTN_W]
        rq = lax.rsqrt(_hdot(q * q, ones) * (1.0 / HEAD_DIM) + EPS)
        qh_ref[...] = (q * rq * gq_ref[...] * (HEAD_DIM ** -0.5)).astype(BF16)
        k = z_ref[:, ATTN_W:2 * ATTN_W]
        rk = lax.rsqrt(_hdot(k * k, ones) * (1.0 / HEAD_DIM) + EPS)
        kh_ref[...] = (k * rk * gk_ref[...]).astype(BF16)
        vb_ref[...] = z_ref[:, 2 * ATTN_W:3 * ATTN_W].astype(BF16)

    row = lambda i: (i, 0)
    s = jax.ShapeDtypeStruct
    return pl.pallas_call(
        body, name="in_proj", grid=(t_tok // tm,),
        out_shape=[s((t_tok, IN_W), F32), s((t_tok, ATTN_W), BF16), s((t_tok, ATTN_W), BF16),
                   s((t_tok, ATTN_W), BF16), s((t_tok, D_MODEL), BF16)],
        in_specs=[pl.BlockSpec((tm, D_MODEL), row), _full(g_mix.shape), _full(w_in_g.shape), _full(ones_bd.shape),
                  _full(gq_t.shape), _full(gk_t.shape)],
        out_specs=[pl.BlockSpec((tm, IN_W), row), pl.BlockSpec((tm, ATTN_W), row), pl.BlockSpec((tm, ATTN_W), row),
                   pl.BlockSpec((tm, ATTN_W), row), pl.BlockSpec((tm, D_MODEL), row)],
        compiler_params=_params(1, VMEM_LIMIT_V7X),
    )(x2, g_mix, w_in_g, ones_bd, gq_t, gk_t)


TQ = 128
NEG = -1e30


def _head_col(t, lm):
    return jnp.max(jnp.where(lm, t, NEG), axis=-1, keepdims=True)


def _head_masks():
    lane = lax.broadcasted_iota(jnp.int32, (1, 1, LANES), 2)
    return [(lane // HEAD_DIM) == h for h in range(LANES // HEAD_DIM)]


def _gather_classes(ref, dil, nt, tq, dtype):
    length = nt * tq
    if dil == 1:
        return ref[...].astype(dtype).reshape(nt, tq, LANES)
    parts = [ref[pl.ds(r, length, stride=dil), :].astype(dtype).reshape(nt, tq, LANES) for r in range(dil)]
    return jnp.concatenate(parts, axis=0)


def _scatter_classes(ref, val, dil, nt, tq, add):
    length = nt * tq
    for r in range(dil):
        rows = pl.ds(r, length, stride=dil) if dil > 1 else slice(None)
        part = val[r * nt:(r + 1) * nt].reshape(length, LANES)
        ref[rows, :] = ref[rows, :] + part if add else part


def _with_prev_tile(t3, dil, nt):
    parts = []
    for r in range(dil):
        t = t3[r * nt:(r + 1) * nt]
        parts.append(jnp.concatenate([t[:1], t[:-1]], axis=0))
    prev = parts[0] if dil == 1 else jnp.concatenate(parts, axis=0)
    return jnp.concatenate([prev, t3], axis=1)


def _band_valid(dil, nt, tq):
    if nt == 1:
        shape = (dil, tq, tq)
        return lax.broadcasted_iota(jnp.int32, shape, 1) >= lax.broadcasted_iota(jnp.int32, shape, 2)
    shape = (dil * nt, tq, 2 * tq)
    b = lax.broadcasted_iota(jnp.int32, shape, 0)
    c = lax.broadcasted_iota(jnp.int32, shape, 2)
    d = tq + lax.broadcasted_iota(jnp.int32, shape, 1) - c
    return (d >= 0) & (d <= tq) & (((b & (nt - 1)) != 0) | (c >= tq))


def _window_tiling(seq, window, dil):
    length = seq // dil
    tq = min(TQ, length)
    nt = length // tq
    assert length % tq == 0 and nt & (nt - 1) == 0 and (nt == 1 or window == tq * dil)
    return nt, tq


def _bqk(a, b):
    return jnp.einsum("bqd,bkd->bqk", a, b, preferred_element_type=F32)


def _bqd(a, b):
    return jnp.einsum("bqk,bkd->bqd", a, b, preferred_element_type=F32)


def _bkd(a, b):
    return jnp.einsum("bqk,bqd->bkd", a, b, preferred_element_type=F32)


def _attn_fwd(qh, kh, vb, z, nb, seq):
    t_tok = nb * seq
    n_win = len(DILATED)

    def body(q_ref, k_ref, v_ref, ga_ref, o_ref, l_ref, ag_ref, qf, kf, vf, oc, lc):
        qf[...] = q_ref[...].astype(F32)
        kf[...] = k_ref[...].astype(F32)
        vf[...] = v_ref[...].astype(F32)
        lms = _head_masks()
        for w, (window, dil) in enumerate(DILATED):
            nt, tq = _window_tiling(seq, window, dil)
            q3 = _gather_classes(qf, dil, nt, tq, BF16)
            k3 = _gather_classes(kf, dil, nt, tq, BF16)
            v3 = _gather_classes(vf, dil, nt, tq, BF16)
            if nt > 1:
                k3, v3 = _with_prev_tile(k3, dil, nt), _with_prev_tile(v3, dil, nt)
            valid = _band_valid(dil, nt, tq)
            o = jnp.zeros(q3.shape, F32)
            lse = jnp.zeros(q3.shape, F32)
            for lm in lms:
                s = _bqk(jnp.where(lm, q3, jnp.zeros_like(q3)), k3)
                m = jnp.max(jnp.where(valid, s, NEG), axis=-1, keepdims=True)
                p = jnp.where(valid, jnp.exp(s - m), 0.0)
                den = jnp.sum(p, axis=-1, keepdims=True)
                o = jnp.where(lm, _bqd(p.astype(BF16), v3) / den, o)
                lse = jnp.where(lm, m + jnp.log(den), lse)
            _scatter_classes(oc.at[w], o, dil, nt, tq, add=False)
            _scatter_classes(lc.at[w], lse, dil, nt, tq, add=False)
        mx = lc[0]
        for w in range(1, n_win):
            mx = jnp.maximum(mx, lc[w])
        tot = jnp.zeros_like(mx)
        o = jnp.zeros_like(mx)
        for w in range(n_win):
            e = jnp.exp(lc[w] - mx)
            tot = tot + e
            o = o + e * oc[w]
        o = o / tot
        o_ref[...] = o
        l_ref[...] = mx + jnp.log(tot)
        ga = ga_ref[...]
        ag_ref[...] = (o * ga * _sig(ga)).astype(BF16)

    blk = pl.BlockSpec((seq, LANES), lambda b, hp: (b, hp))
    s = jax.ShapeDtypeStruct
    return pl.pallas_call(
        body, name="attn_fwd", grid=(nb, ATTN_W // LANES),
        out_shape=[s((t_tok, ATTN_W), F32), s((t_tok, ATTN_W), F32), s((t_tok, ATTN_W), BF16)],
        in_specs=[blk, blk, blk, pl.BlockSpec((seq, LANES), lambda b, hp: (b, 3 * ATTN_W // LANES + hp))],
        out_specs=[blk, blk, blk],
        scratch_shapes=[pltpu.VMEM((seq, LANES), F32)] * 3 + [pltpu.VMEM((n_win, seq, LANES), F32)] * 2,
        compiler_params=_params(2, VMEM_LIMIT_V7X),
    )(qh, kh, vb, z)


SCAN_COLS = 512


def _to_segments(dst_ref, val):
    seg = val.shape[0] // SUBLANES
    for n in range(dst_ref.shape[0]):
        for s in range(SUBLANES):
            dst_ref[n, pl.ds(s, seg, stride=SUBLANES), :] = val[s * seg:(s + 1) * seg, n * LANES:(n + 1) * LANES]


def _from_segments(src_ref):
    seg = src_ref.shape[1] // SUBLANES
    return jnp.concatenate(
        [jnp.concatenate([src_ref[n, pl.ds(s, seg, stride=SUBLANES), :] for s in range(SUBLANES)], axis=0)
         for n in range(src_ref.shape[0])], axis=1)


def _scan_chunk(re_ref, im_ref, a_re_ref, a_im_ref, carry_re, carry_im, rows, reverse, visit=None):
    seg = rows // SUBLANES
    assert seg & (seg - 1) == 0
    rowi = lax.broadcasted_iota(jnp.int32, (SUBLANES, SCAN_COLS), 0)
    edge = (SUBLANES - 1) if reverse else 0
    last = 0 if reverse else SUBLANES - 1
    at_edge = rowi == edge

    def cmul(ar, ai, br, bi):
        return ar * br - ai * bi, ar * bi + ai * br

    for c0 in range(0, N_STATE, SCAN_COLS):
        cols = slice(c0, c0 + SCAN_COLS)
        a1r = jnp.broadcast_to(a_re_ref[:, cols], (SUBLANES, SCAN_COLS))
        a1i = jnp.broadcast_to(a_im_ref[:, cols], (SUBLANES, SCAN_COLS))
        if reverse:
            a1i = -a1i

        def block_of(i):
            j = (seg - 1 - i) if reverse else i
            return j, pl.ds(pl.multiple_of(j * SUBLANES, SUBLANES), SUBLANES)

        def local(i, carry, cols=cols, a1r=a1r, a1i=a1i):
            xr, xi = carry
            _, blk = block_of(i)
            nr, ni = cmul(a1r, a1i, xr, xi)
            xr, xi = nr + re_ref[blk, cols], ni + im_ref[blk, cols]
            re_ref[blk, cols] = xr
            im_ref[blk, cols] = xi
            return xr, xi

        zero = jnp.zeros((SUBLANES, SCAN_COLS), F32)
        er, ei = lax.fori_loop(0, seg, local, (zero, zero))

        pr, pi = a1r, a1i
        for _ in range(seg.bit_length() - 1):
            pr, pi = cmul(pr, pi, pr, pi)
        cr, ci = carry_re[:, cols], carry_im[:, cols]
        inr, ini = cmul(pr, pi, cr, ci)
        er = er + jnp.where(at_edge, inr, 0.0)
        ei = ei + jnp.where(at_edge, ini, 0.0)
        for sft in (1, 2, 4):
            shift, keep = (SUBLANES - sft, rowi < SUBLANES - sft) if reverse else (sft, rowi >= sft)
            rs = jnp.where(keep, pltpu.roll(er, shift, 0), 0.0)
            ims = jnp.where(keep, pltpu.roll(ei, shift, 0), 0.0)
            dr, di = cmul(pr, pi, rs, ims)
            er, ei = er + dr, ei + di
            pr, pi = cmul(pr, pi, pr, pi)
        carry_re[:, cols] = jnp.broadcast_to(er[last:last + 1, :], (SUBLANES, SCAN_COLS))
        carry_im[:, cols] = jnp.broadcast_to(ei[last:last + 1, :], (SUBLANES, SCAN_COLS))
        one = (SUBLANES - 1) if reverse else 1
        kr = jnp.where(at_edge, cr, pltpu.roll(er, one, 0))
        ki = jnp.where(at_edge, ci, pltpu.roll(ei, one, 0))

        def fix(i, carry, cols=cols, a1r=a1r, a1i=a1i):
            kr, ki, acc = carry
            j, blk = block_of(i)
            kr, ki = cmul(a1r, a1i, kr, ki)
            xr, xi = re_ref[blk, cols] + kr, im_ref[blk, cols] + ki
            re_ref[blk, cols] = xr
            im_ref[blk, cols] = xi
            if visit is not None:
                acc = visit(cols, j, xr, xi, acc)
            return kr, ki, acc

        _, _, acc = lax.fori_loop(0, seg, fix, (kr, ki, (zero, zero)))
        if visit is not None:
            visit(cols, None, None, None, acc)


def _ssm_fwd(z, a_re, a_im, bb_re, bb_im, cc_re, cc_im, d_skip, w_glu, b_glu, nb, seq):
    t_tok = nb * seq
    tc = min(256, seq)
    nch = seq // tc
    grp = N_STATE // 4

    def body(u_ref, gs_ref, ar_ref, ai_ref, bbr_ref, bbi_ref, ccr_ref, cci_ref, d_ref, wg_ref, bg_ref,
             xr_ref, xi_ref, y_ref, sg_ref, car_re, car_im, seg_u, seg_y):
        @pl.when(pl.program_id(1) == 0)
        def _():
            car_re[...] = jnp.zeros_like(car_re)
            car_im[...] = jnp.zeros_like(car_im)

        u = u_ref[...]
        _to_segments(seg_u, u)
        for j in range(4):
            uj = seg_u[j].astype(BF16)
            xr_ref[:, j * grp:(j + 1) * grp] = _dot(uj, bbr_ref[j])
            xi_ref[:, j * grp:(j + 1) * grp] = _dot(uj, bbi_ref[j])
        _scan_chunk(xr_ref, xi_ref, ar_ref, ai_ref, car_re, car_im, tc, reverse=False)
        for j in range(4):
            xr = xr_ref[:, j * grp:(j + 1) * grp].astype(BF16)
            xi = xi_ref[:, j * grp:(j + 1) * grp].astype(BF16)
            seg_y[j] = _dot(xr, ccr_ref[j]) - _dot(xi, cci_ref[j])
        y = _from_segments(seg_y) + d_ref[...] * u
        y_ref[...] = y
        yg, _ = _gelu_and_grad(y)
        gl = _dot(yg.astype(BF16), wg_ref[...]) + bg_ref[...]
        gs = gs_ref[...]
        sg_ref[...] = (yg * _sig(gl) * gs * _sig(gs)).astype(BF16)

    umap = lambda b, ch: (b * nch + ch, 4)
    gmap = lambda b, ch: (b * nch + ch, 5)
    row = lambda b, ch: (b * nch + ch, 0)
    s = jax.ShapeDtypeStruct
    consts = [a_re, a_im, bb_re, bb_im, cc_re, cc_im, d_skip, w_glu, b_glu]
    return pl.pallas_call(
        body, name="ssm_fwd", grid=(nb, nch),
        out_shape=[s((t_tok, N_STATE), F32), s((t_tok, N_STATE), F32), s((t_tok, SSM_W), F32),
                   s((t_tok, SSM_W), BF16)],
        in_specs=[pl.BlockSpec((tc, SSM_W), umap), pl.BlockSpec((tc, SSM_W), gmap)] + [_full(c.shape) for c in consts],
        out_specs=[pl.BlockSpec((tc, N_STATE), row), pl.BlockSpec((tc, N_STATE), row),
                   pl.BlockSpec((tc, SSM_W), row), pl.BlockSpec((tc, SSM_W), row)],
        scratch_shapes=[pltpu.VMEM((SUBLANES, N_STATE), F32), pltpu.VMEM((SUBLANES, N_STATE), F32),
                        pltpu.VMEM((4, tc, LANES), F32), pltpu.VMEM((4, tc, LANES), F32)],
        compiler_params=_params(2, VMEM_LIMIT_V7X),
    )(z, z, *consts)


def _tail(x2, tg2, ag, sg, p2, w_out, w_g, w_p, g_ple):
    t_tok = x2.shape[0]
    tm = min(256, t_tok)
    nt = t_tok // tm
    half = ATTN_W

    def body(x_ref, tg_ref, ag_ref, sg_ref, p_ref, wo_ref, wg_ref, wp_ref, gp_ref,
             dmix_ref, dh1_ref, loss_ref, dgp_ref, dwo_ref, dwg_ref, dwp_ref, acc_o, acc_g, acc_p):
        i = pl.program_id(0)

        @pl.when(i == 0)
        def _():
            loss_ref[...] = jnp.zeros_like(loss_ref)
            dgp_ref[...] = jnp.zeros_like(dgp_ref)
            acc_o[...] = jnp.zeros_like(acc_o)
            acc_g[...] = jnp.zeros_like(acc_g)
            acc_p[...] = jnp.zeros_like(acc_p)

        ag_t, sg_t = ag_ref[...], sg_ref[...]
        h1 = x_ref[...] + _dot(ag_t, wo_ref[0:half, :]) + _dot(sg_t, wo_ref[half:2 * half, :])
        r2 = lax.rsqrt(jnp.mean(h1 * h1, axis=-1, keepdims=True) + EPS)
        hnorm = h1 * r2
        gp = gp_ref[...]
        hn = (hnorm * gp).astype(BF16)
        gate = _sig(_dot(hn, wg_ref[...]))
        pb = p_ref[...].astype(BF16)
        pp = jnp.concatenate([_dot(pb, wp_ref[j]) for j in range(N_DEV)], axis=-1)
        h2 = h1 + gate * pp
        err = h2 - tg_ref[...]
        loss_ref[...] += 0.5 * jnp.sum(err * err) * (1.0 / D_MODEL)
        dh2 = err * (1.0 / D_MODEL)
        dpp = (dh2 * gate).astype(BF16)
        dgpre = (dh2 * pp * gate * (1.0 - gate)).astype(BF16)
        acc_p[...] += _dot_tn(pb, dpp)
        acc_g[...] += _dot_tn(hn, dgpre)
        dhn = _dot_nt(dgpre, wg_ref[...])
        dgp_ref[...] += jnp.sum(dhn * hnorm, axis=0, keepdims=True)
        a = dhn * gp
        dh1 = dh2 + r2 * (a - hnorm * jnp.mean(a * hnorm, axis=-1, keepdims=True))
        dh1_ref[...] = dh1
        dh1b = dh1.astype(BF16)
        acc_o[0:half, :] += _dot_tn(ag_t, dh1b)
        acc_o[half:2 * half, :] += _dot_tn(sg_t, dh1b)
        dmix_ref[...] = _dot_nt(dh1b, wo_ref[...])

        @pl.when(i == nt - 1)
        def _():
            dwo_ref[...] = acc_o[...].astype(BF16)
            dwg_ref[...] = acc_g[...].astype(BF16)
            for j in range(N_DEV):
                dwp_ref[j] = acc_p[:, j * LANES:(j + 1) * LANES].astype(BF16)

    row = lambda i: (i, 0)
    s = jax.ShapeDtypeStruct
    return pl.pallas_call(
        body, name="tail_fwd_bwd", grid=(nt,),
        out_shape=[s((t_tok, D_MODEL), F32), s((t_tok, D_MODEL), F32), s((SUBLANES, LANES), F32),
                   s((1, D_MODEL), F32), s((D_MODEL, D_MODEL), BF16), s((D_MODEL, D_MODEL), BF16),
                   s((N_DEV, PLE_DIM, LANES), BF16)],
        in_specs=[pl.BlockSpec((tm, D_MODEL), row), pl.BlockSpec((tm, D_MODEL), row),
                  pl.BlockSpec((tm, half), row), pl.BlockSpec((tm, half), row), pl.BlockSpec((tm, PLE_DIM), row),
                  _full(w_out.shape), _full(w_g.shape), _full(w_p.shape), _full(g_ple.shape)],
        out_specs=[pl.BlockSpec((tm, D_MODEL), row), pl.BlockSpec((tm, D_MODEL), row), _full((SUBLANES, LANES)),
                   _full((1, D_MODEL)), _full((D_MODEL, D_MODEL)), _full((D_MODEL, D_MODEL)),
                   _full((N_DEV, PLE_DIM, LANES))],
        scratch_shapes=[pltpu.VMEM((D_MODEL, D_MODEL), F32), pltpu.VMEM((D_MODEL, D_MODEL), F32),
                        pltpu.VMEM((PLE_DIM, D_MODEL), F32)],
        compiler_params=_params(1, VMEM_LIMIT_V7X),
    )(x2, tg2, ag, sg, p2, w_out, w_g, w_p, g_ple)


def _attn_bwd(qh, kh, vb, z, o, lse, dmix, nb, seq):
    t_tok = nb * seq

    def body(q_ref, k_ref, v_ref, ga_ref, o_ref, l_ref, da_ref, dq_ref, dk_ref, dv_ref, dga_ref,
             qf, kf, vf, dof, dlf):
        ga, o_t, da = ga_ref[...], o_ref[...], da_ref[...]
        sga = _sig(ga)
        d_o = da * ga * sga
        dga_ref[...] = da * o_t * sga * (1.0 + ga * (1.0 - sga))
        lane = lax.broadcasted_iota(jnp.int32, (1, LANES), 1)
        d_oo = d_o * o_t
        delta = jnp.zeros_like(d_oo)
        for h in range(LANES // HEAD_DIM):
            lm2 = (lane // HEAD_DIM) == h
            delta = jnp.where(lm2, jnp.sum(jnp.where(lm2, d_oo, 0.0), axis=-1, keepdims=True), delta)
        qf[...] = q_ref[...].astype(F32)
        kf[...] = k_ref[...].astype(F32)
        vf[...] = v_ref[...].astype(F32)
        dof[...] = d_o
        dlf[...] = delta
        dq_ref[...] = jnp.zeros_like(dq_ref)
        dk_ref[...] = jnp.zeros_like(dk_ref)
        dv_ref[...] = jnp.zeros_like(dv_ref)
        lms = _head_masks()
        for window, dil in DILATED:
            nt, tq = _window_tiling(seq, window, dil)
            q3 = _gather_classes(qf, dil, nt, tq, BF16)
            k3 = _gather_classes(kf, dil, nt, tq, BF16)
            v3 = _gather_classes(vf, dil, nt, tq, BF16)
            do3 = _gather_classes(dof, dil, nt, tq, BF16)
            lt3 = _gather_classes(l_ref, dil, nt, tq, F32)
            dl3 = _gather_classes(dlf, dil, nt, tq, F32)
            if nt > 1:
                k3, v3 = _with_prev_tile(k3, dil, nt), _with_prev_tile(v3, dil, nt)
            valid = _band_valid(dil, nt, tq)
            dq = jnp.zeros(q3.shape, F32)
            dk = jnp.zeros(k3.shape, F32)
            dv = jnp.zeros(k3.shape, F32)
            for lm in lms:
                qm = jnp.where(lm, q3, jnp.zeros_like(q3))
                dom = jnp.where(lm, do3, jnp.zeros_like(do3))
                p = jnp.where(valid, jnp.exp(_bqk(qm, k3) - _head_col(lt3, lm)), 0.0)
                dv = dv + _bkd(p.astype(BF16), dom)
                ds = (p * (_bqk(dom, v3) - _head_col(dl3, lm))).astype(BF16)
                dq = dq + jnp.where(lm, _bqd(ds, k3), 0.0)
                dk = dk + _bkd(ds, qm)
            _scatter_classes(dq_ref, dq, dil, nt, tq, add=True)
            for ref, g in ((dk_ref, dk), (dv_ref, dv)):
                if nt > 1:
                    own, prev = g[:, tq:, :], g[:, :tq, :]
                    parts = []
                    for r in range(dil):
                        t = prev[r * nt:(r + 1) * nt]
                        parts.append(jnp.concatenate([t[1:], jnp.zeros_like(t[:1])], axis=0))
                    g = own + (parts[0] if dil == 1 else jnp.concatenate(parts, axis=0))
                _scatter_classes(ref, g, dil, nt, tq, add=True)

    blk = pl.BlockSpec((seq, LANES), lambda b, hp: (b, hp))
    return pl.pallas_call(
        body, name="attn_bwd", grid=(nb, ATTN_W // LANES),
        out_shape=[jax.ShapeDtypeStruct((t_tok, ATTN_W), F32)] * 4,
        in_specs=[blk, blk, blk, pl.BlockSpec((seq, LANES), lambda b, hp: (b, 3 * ATTN_W // LANES + hp)), blk, blk,
                  blk],
        out_specs=[blk] * 4,
        scratch_shapes=[pltpu.VMEM((seq, LANES), F32)] * 5,
        compiler_params=_params(2, VMEM_LIMIT_V7X),
    )(qh, kh, vb, z, o, lse, dmix)


def _ssm_bwd(z, dmix, y, x_re, x_im, a_re, a_im, bb_re, bb_im, cc_re, cc_im, d_skip, w_glu, b_glu, nb, seq):
    t_tok = nb * seq
    tc = min(256, seq)
    nch = seq // tc
    grp = N_STATE // 4

    def body(u_ref, gs_ref, ds_ref, y_ref, xr_ref, xi_ref, xpr_ref, xpi_ref,
             ar_ref, ai_ref, bbr_ref, bbi_ref, ccr_ref, cci_ref, d_ref, wg_ref, bg_ref,
             du_ref, dgs_ref, dwg_ref, dbg_ref, dd_ref, dar_ref, dai_ref, dbbr_ref, dbbi_ref, dccr_ref, dcci_ref,
             lam_re, lam_im, car_re, car_im, acc_wg, seg_a, seg_b, ent_re, ent_im):
        step = pl.program_id(1)
        first_chunk = step == nch - 1

        @pl.when((pl.program_id(0) == 0) & (step == 0))
        def _():
            acc_wg[...] = jnp.zeros_like(acc_wg)
            for ref in (dbg_ref, dd_ref, dar_ref, dai_ref, dbbr_ref, dbbi_ref, dccr_ref, dcci_ref):
                ref[...] = jnp.zeros_like(ref)

        @pl.when(step == 0)
        def _():
            car_re[...] = jnp.zeros_like(car_re)
            car_im[...] = jnp.zeros_like(car_im)

        u, gs, dssm, y = u_ref[...], gs_ref[...], ds_ref[...], y_ref[...]
        yg, dgelu = _gelu_and_grad(y)
        ygb = yg.astype(BF16)
        sgl = _sig(_dot(ygb, wg_ref[...]) + bg_ref[...])
        sgs = _sig(gs)
        dout = dssm * gs * sgs
        dgs_ref[...] = dssm * yg * sgl * sgs * (1.0 + gs * (1.0 - sgs))
        dgl = dout * yg * sgl * (1.0 - sgl)
        dglb = dgl.astype(BF16)
        dyg = dout * sgl + _dot_nt(dglb, wg_ref[...])
        acc_wg[...] += _dot_tn(ygb, dglb)
        dbg_ref[...] += jnp.sum(dgl, axis=0, keepdims=True)
        dy = dyg * dgelu
        dd_ref[...] += jnp.sum(dy * u, axis=0, keepdims=True)
        _to_segments(seg_a, dy)
        _to_segments(seg_b, u)
        for j in range(4):
            dyj = seg_a[j].astype(BF16)
            sl = slice(j * grp, (j + 1) * grp)
            lam_re[:, sl] = _dot_nt(dyj, ccr_ref[j])
            lam_im[:, sl] = -_dot_nt(dyj, cci_ref[j])
            dccr_ref[j] += _dot_tn(xr_ref[:, sl].astype(BF16), dyj)
            dcci_ref[j] -= _dot_tn(xi_ref[:, sl].astype(BF16), dyj)

        keep_prev = jnp.where(first_chunk, 0.0, 1.0)
        seg = tc // SUBLANES
        last_blk = pl.ds((seg - 1) * SUBLANES, SUBLANES)
        row0 = lax.broadcasted_iota(jnp.int32, (SUBLANES, N_STATE), 0) == 0
        for src, prev, dst in ((xr_ref, xpr_ref, ent_re), (xi_ref, xpi_ref, ent_im)):
            before = jnp.broadcast_to(prev[SUBLANES - 1:SUBLANES, :] * keep_prev, (SUBLANES, N_STATE))
            dst[...] = jnp.where(row0, before, pltpu.roll(src[last_blk, :], 1, 0))

        def visit(cols, j, lr, li, acc):
            if j is None:
                dar_ref[:, cols] += jnp.sum(acc[0], axis=0, keepdims=True)
                dai_ref[:, cols] += jnp.sum(acc[1], axis=0, keepdims=True)
                return None
            blk = pl.ds(pl.multiple_of(jnp.maximum(j - 1, 0) * SUBLANES, SUBLANES), SUBLANES)
            inside = j > 0
            xpr = jnp.where(inside, xr_ref[blk, cols], ent_re[:, cols])
            xpi = jnp.where(inside, xi_ref[blk, cols], ent_im[:, cols])
            return acc[0] + lr * xpr + li * xpi, acc[1] + li * xpr - lr * xpi

        _scan_chunk(lam_re, lam_im, ar_ref, ai_ref, car_re, car_im, tc, reverse=True, visit=visit)

        for j in range(4):
            sl = slice(j * grp, (j + 1) * grp)
            lr = lam_re[:, sl].astype(BF16)
            li = lam_im[:, sl].astype(BF16)
            uj = seg_b[j].astype(BF16)
            seg_a[j] = _dot_nt(lr, bbr_ref[j]) + _dot_nt(li, bbi_ref[j])
            dbbr_ref[j] += _dot_tn(uj, lr)
            dbbi_ref[j] += _dot_tn(uj, li)
        du_ref[...] = _from_segments(seg_a) + dy * d_ref[...]

        @pl.when((pl.program_id(0) == nb - 1) & (step == nch - 1))
        def _():
            dwg_ref[...] = acc_wg[...].astype(BF16)

    rev = lambda b, ch: b * nch + (nch - 1 - ch)
    umap = lambda b, ch: (rev(b, ch), 4)
    gmap = lambda b, ch: (rev(b, ch), 5)
    smap = lambda b, ch: (rev(b, ch), 1)
    row = lambda b, ch: (rev(b, ch), 0)
    prev = lambda b, ch: (jnp.maximum(rev(b, ch) * (tc // SUBLANES) - 1, 0), 0)
    s = jax.ShapeDtypeStruct
    consts = [a_re, a_im, bb_re, bb_im, cc_re, cc_im, d_skip, w_glu, b_glu]
    acc_shapes = [s((1, SSM_W), F32), s((1, SSM_W), F32), s((1, N_STATE), F32), s((1, N_STATE), F32),
                  s(bb_re.shape, F32), s(bb_re.shape, F32), s(cc_re.shape, F32), s(cc_re.shape, F32)]
    return pl.pallas_call(
        body, name="ssm_bwd", grid=(nb, nch),
        out_shape=[s((t_tok, SSM_W), F32), s((t_tok, SSM_W), F32), s((SSM_W, SSM_W), BF16)] + acc_shapes,
        in_specs=[pl.BlockSpec((tc, SSM_W), umap), pl.BlockSpec((tc, SSM_W), gmap), pl.BlockSpec((tc, SSM_W), smap),
                  pl.BlockSpec((tc, SSM_W), row), pl.BlockSpec((tc, N_STATE), row), pl.BlockSpec((tc, N_STATE), row),
                  pl.BlockSpec((SUBLANES, N_STATE), prev), pl.BlockSpec((SUBLANES, N_STATE), prev)]
        + [_full(c.shape) for c in consts],
        out_specs=[pl.BlockSpec((tc, SSM_W), row), pl.BlockSpec((tc, SSM_W), row), _full((SSM_W, SSM_W))]
        + [_full(a.shape) for a in acc_shapes],
        scratch_shapes=[pltpu.VMEM((tc, N_STATE), F32), pltpu.VMEM((tc, N_STATE), F32),
                        pltpu.VMEM((SUBLANES, N_STATE), F32), pltpu.VMEM((SUBLANES, N_STATE), F32),
                        pltpu.VMEM((SSM_W, SSM_W), F32), pltpu.VMEM((4, tc, LANES), F32),
                        pltpu.VMEM((4, tc, LANES), F32),
                        pltpu.VMEM((SUBLANES, N_STATE), F32), pltpu.VMEM((SUBLANES, N_STATE), F32)],
        compiler_params=_params(2, VMEM_LIMIT_V7X),
    )(z, z, dmix, y, x_re, x_im, x_re, x_im, *consts)


def _dz_and_dx(x2, z, dqh, dkh, dvb, dga, du, dgs, dh1, w_in_g, g_mix, gq_t, gk_t, ones_bd, fold):
    t_tok = x2.shape[0]
    tm = min(256, t_tok)
    nt = t_tok // tm
    a_w = ATTN_W

    def head_norm_bwd(raw, d_hat, gain, scale, ones):
        r = lax.rsqrt(_hdot(raw * raw, ones) * (1.0 / HEAD_DIM) + EPS)
        n = raw * r
        a = d_hat * gain * scale
        d_raw = r * (a - n * (_hdot(a * n, ones) * (1.0 / HEAD_DIM)))
        return d_raw, jnp.sum(d_hat * n * scale, axis=0, keepdims=True)

    def body(x_ref, q_ref, k_ref, dq_ref, dk_ref, dv_ref, dga_ref, du_ref, dgs_ref, dh1_ref, w_ref, g_ref,
             gq_ref, gk_ref, ones_ref, fold_ref, dz_ref, gx_ref, dgm_ref, dgq_ref, dgk_ref, acc_q, acc_k):
        i = pl.program_id(0)

        @pl.when(i == 0)
        def _():
            dgm_ref[...] = jnp.zeros_like(dgm_ref)
            acc_q[...] = jnp.zeros_like(acc_q)
            acc_k[...] = jnp.zeros_like(acc_k)

        ones = ones_ref[...]
        dq, sq = head_norm_bwd(q_ref[...], dq_ref[...], gq_ref[...], HEAD_DIM ** -0.5, ones)
        dk, sk = head_norm_bwd(k_ref[...], dk_ref[...], gk_ref[...], 1.0, ones)
        acc_q[...] += jnp.broadcast_to(sq, acc_q.shape)
        acc_k[...] += jnp.broadcast_to(sk, acc_k.shape)
        parts = (dq, dk, dv_ref[...], dga_ref[...], du_ref[...], dgs_ref[...])
        for n, part in enumerate(parts):
            dz_ref[:, n * a_w:(n + 1) * a_w] = part.astype(BF16)
        dxn = jnp.zeros((tm, D_MODEL), F32)
        for j in range(N_DEV):
            dxn = dxn + _dot_nt(dz_ref[:, j * COL_W:(j + 1) * COL_W], w_ref[j])
        x = x_ref[...]
        r1 = lax.rsqrt(jnp.mean(x * x, axis=-1, keepdims=True) + EPS)
        xnorm = x * r1
        dgm_ref[...] += jnp.sum(dxn * xnorm, axis=0, keepdims=True)
        a = dxn * g_ref[...]
        gx_ref[...] = dh1_ref[...] + r1 * (a - xnorm * jnp.mean(a * xnorm, axis=-1, keepdims=True))

        @pl.when(i == nt - 1)
        def _():
            dgq_ref[...] = _hdot(acc_q[...], fold_ref[...])
            dgk_ref[...] = _hdot(acc_k[...], fold_ref[...])

    row = lambda i: (i, 0)
    col = lambda n: (lambda i: (i, n))
    s = jax.ShapeDtypeStruct
    half = pl.BlockSpec((tm, a_w), row)
    return pl.pallas_call(
        body, name="dz_dx", grid=(nt,),
        out_shape=[s((t_tok, IN_W), BF16), s((t_tok, D_MODEL), F32), s((1, D_MODEL), F32),
                   s((SUBLANES, HEAD_DIM), F32), s((SUBLANES, HEAD_DIM), F32)],
        in_specs=[pl.BlockSpec((tm, D_MODEL), row), pl.BlockSpec((tm, a_w), col(0)), pl.BlockSpec((tm, a_w), col(1)),
                  half, half, half, half, half, half, pl.BlockSpec((tm, D_MODEL), row),
                  _full(w_in_g.shape), _full(g_mix.shape), _full(gq_t.shape), _full(gk_t.shape),
                  _full(ones_bd.shape), _full(fold.shape)],
        out_specs=[pl.BlockSpec((tm, IN_W), row), pl.BlockSpec((tm, D_MODEL), row), _full((1, D_MODEL)),
                   _full((SUBLANES, HEAD_DIM)), _full((SUBLANES, HEAD_DIM))],
        scratch_shapes=[pltpu.VMEM((SUBLANES, a_w), F32), pltpu.VMEM((SUBLANES, a_w), F32)],
        compiler_params=_params(1, VMEM_LIMIT_V7X),
    )(x2, z, z, dqh, dkh, dvb, dga, du, dgs, dh1, w_in_g, g_mix, gq_t, gk_t, ones_bd, fold)


def _dw_in(xn, dz):
    t_tok = xn.shape[0]
    tk = min(1024, t_tok)
    nk = t_tok // tk

    def body(xn_ref, dz_ref, out_ref, acc):
        k = pl.program_id(1)

        @pl.when(k == 0)
        def _():
            acc[...] = jnp.zeros_like(acc)

        acc[...] += _dot_tn(xn_ref[...], dz_ref[...])

        @pl.when(k == nk - 1)
        def _():
            out_ref[0] = acc[...].astype(BF16)

    return pl.pallas_call(
        body, name="dw_in", grid=(N_DEV, nk),
        out_shape=jax.ShapeDtypeStruct((N_DEV, D_MODEL, COL_W), BF16),
        in_specs=[pl.BlockSpec((tk, D_MODEL), lambda j, k: (k, 0)), pl.BlockSpec((tk, COL_W), lambda j, k: (k, j))],
        out_specs=pl.BlockSpec((1, D_MODEL, COL_W), lambda j, k: (j, 0, 0)),
        scratch_shapes=[pltpu.VMEM((D_MODEL, COL_W), F32)],
        compiler_params=_params(2, VMEM_LIMIT_V7X),
    )(xn, dz)


SMALL = ("mix_norm", "q_norm", "k_norm", "lambda_re", "lambda_im", "log_dt", "b_re", "b_im", "c_re", "c_im",
         "d_skip", "b_glu", "ple_norm")
BIG = ("w_in", "w_glu", "w_out", "w_ple_gate", "w_ple_proj")
WEIGHTS = ("mix_norm", "w_in", "q_norm", "k_norm", "lambda_re", "lambda_im", "log_dt", "b_re", "b_im", "c_re",
           "c_im", "d_skip", "w_glu", "b_glu", "w_out", "ple_norm", "w_ple_gate", "w_ple_proj")


def _pack(arrs):
    flat = jnp.concatenate([a.reshape(-1).astype(F32) for a in arrs])
    rows = -(-flat.shape[0] // (64 * LANES)) * 64
    return jnp.pad(flat, (0, rows * LANES - flat.shape[0])).reshape(rows, LANES)


def _unpack(packed, shapes):
    flat = packed.reshape(-1)
    out, off = [], 0
    for shp in shapes:
        size = math.prod(shp)
        out.append(flat[off:off + size].reshape(shp))
        off += size
    return out


def kernel(x, p, mix_norm, w_in, q_norm, k_norm, lambda_re, lambda_im, log_dt, b_re, b_im, c_re, c_im, d_skip, w_glu, b_glu, w_out, ple_norm, w_ple_gate, w_ple_proj, loss_target, m_mix_norm, m_w_in, m_q_norm, m_k_norm, m_lambda_re, m_lambda_im, m_log_dt, m_b_re, m_b_im, m_c_re, m_c_im, m_d_skip, m_w_glu, m_b_glu, m_w_out, m_ple_norm, m_w_ple_gate, m_w_ple_proj, v_mix_norm, v_w_in, v_q_norm, v_k_norm, v_lambda_re, v_lambda_im, v_log_dt, v_b_re, v_b_im, v_c_re, v_c_im, v_d_skip, v_w_glu, v_b_glu, v_w_out, v_ple_norm, v_w_ple_gate, v_w_ple_proj):
    env = dict(locals())
    w = {n: env[n] for n in WEIGHTS}
    m = {n: env["m_" + n] for n in WEIGHTS}
    v = {n: env["v_" + n] for n in WEIGHTS}
    nb, seq, _ = x.shape
    t_tok = nb * seq
    x2 = x.reshape(t_tok, D_MODEL)
    tg2 = loss_target.reshape(t_tok, D_MODEL)
    p2 = p.reshape(t_tok, PLE_DIM)

    shard2d = {"w_in": (D_MODEL, COL_W), "w_glu": (SSM_W // N_DEV, SSM_W), "w_out": (D_MODEL // N_DEV, D_MODEL),
               "w_ple_gate": (D_MODEL // N_DEV, D_MODEL), "w_ple_proj": (PLE_DIM, D_MODEL // N_DEV)}
    w_sh = [w[n].reshape(shard2d[n]) for n in BIG]
    w_in_g, w_glu_g, w_out_g, w_g_g, w_p_g = _all_gather(w_sh, [BF16] * len(BIG), "gather_weights")
    w_glu_f = w_glu_g.reshape(SSM_W, SSM_W)
    w_out_f = w_out_g.reshape(D_MODEL, D_MODEL)
    w_g_f = w_g_g.reshape(D_MODEL, D_MODEL)

    g3 = (SSM_GROUPS, 1, SSM_STATE)
    lr3, li3 = lambda_re.reshape(g3), lambda_im.reshape(g3)
    dt3 = log_dt.reshape(SSM_GROUPS, 1, 1)
    btr = b_re[0].transpose(0, 2, 1)
    bti = b_im[0].transpose(0, 2, 1)
    a_re3, a_im3, bbr, bbi = _zoh_fwd(lr3, li3, dt3, btr, bti)
    a_re, a_im = a_re3.reshape(1, N_STATE), a_im3.reshape(1, N_STATE)
    bb_re, bb_im = _blockdiag(bbr).astype(BF16), _blockdiag(bbi).astype(BF16)
    cc_re = _blockdiag(c_re[0].transpose(0, 2, 1)).astype(BF16)
    cc_im = _blockdiag(c_im[0].transpose(0, 2, 1)).astype(BF16)

    ones_bd = _head_ones()
    fold = jnp.tile(jnp.eye(HEAD_DIM, dtype=F32), (ATTN_W // HEAD_DIM, 1))
    gq_t = jnp.tile(q_norm, (1, ATTN_W // HEAD_DIM))
    gk_t = jnp.tile(k_norm, (1, ATTN_W // HEAD_DIM))

    z, qh, kh, vb, xn = _in_proj(x2, mix_norm, w_in_g, ones_bd, gq_t, gk_t)
    o, lse, ag = _attn_fwd(qh, kh, vb, z, nb, seq)
    x_re, x_im, y, sg = _ssm_fwd(z, a_re, a_im, bb_re, bb_im, cc_re, cc_im, d_skip, w_glu_f, b_glu, nb, seq)
    dmix, dh1, loss_t, d_ple, dw_out, dw_g, dw_p = _tail(x2, tg2, ag, sg, p2, w_out_f, w_g_f, w_p_g, ple_norm)

    dqh, dkh, dvb, dga = _attn_bwd(qh, kh, vb, z, o, lse, dmix, nb, seq)
    (du, dgs, dw_glu, d_bglu, d_dskip, da_re, da_im, dbb_re, dbb_im, dcc_re, dcc_im) = _ssm_bwd(
        z, dmix, y, x_re, x_im, a_re, a_im, bb_re, bb_im, cc_re, cc_im, d_skip, w_glu_f, b_glu, nb, seq)
    dz, gx, d_mix, d_gq, d_gk = _dz_and_dx(x2, z, dqh, dkh, dvb, dga, du, dgs, dh1, w_in_g, mix_norm, gq_t, gk_t,
                                           ones_bd, fold)
    dw_in = _dw_in(xn, dz)

    d_lr, d_li, d_dt, d_btr, d_bti = _zoh_bwd(
        lr3, li3, dt3, btr, bti, da_re.reshape(g3), da_im.reshape(g3),
        _blockdiag_extract(dbb_re, SSM_GROUP, SSM_STATE), _blockdiag_extract(dbb_im, SSM_GROUP, SSM_STATE))
    small_g = {
        "mix_norm": d_mix, "q_norm": d_gq[0:1], "k_norm": d_gk[0:1], "lambda_re": d_lr, "lambda_im": d_li,
        "log_dt": d_dt, "b_re": d_btr.transpose(0, 2, 1), "b_im": d_bti.transpose(0, 2, 1),
        "c_re": _blockdiag_extract(dcc_re, SSM_STATE, SSM_GROUP).transpose(0, 2, 1),
        "c_im": _blockdiag_extract(dcc_im, SSM_STATE, SSM_GROUP).transpose(0, 2, 1),
        "d_skip": d_dskip, "b_glu": d_bglu, "ple_norm": d_ple}

    parts = [dw_in, dw_glu.reshape(N_DEV, SSM_W // N_DEV, SSM_W), dw_out.reshape(N_DEV, D_MODEL // N_DEV, D_MODEL),
             dw_g.reshape(N_DEV, D_MODEL // N_DEV, D_MODEL), dw_p]
    g_sh = _reduce_scatter(parts, "scatter_grads")
    d_sh, m_sh, v_sh = _adamw_shards(g_sh, w_sh, [m[n].reshape(shard2d[n]) for n in BIG],
                                     [v[n].reshape(shard2d[n]) for n in BIG])

    (gathered,) = _all_gather([_pack([small_g[n] for n in SMALL])], [F32], "gather_small_grads")
    g_pk, d_pk, m_pk, v_pk = _small_update(gathered, _pack([w[n] for n in SMALL]), _pack([m[n] for n in SMALL]),
                                           _pack([v[n] for n in SMALL]))

    grads, deltas, new_m, new_v = {}, {}, {}, {}
    small_shapes = [w[n].shape for n in SMALL]
    for dst, packed in ((grads, g_pk), (deltas, d_pk), (new_m, m_pk), (new_v, v_pk)):
        for n, a in zip(SMALL, _unpack(packed, small_shapes)):
            dst[n] = a
    for i, n in enumerate(BIG):
        grads[n] = g_sh[i].reshape(w[n].shape)
        deltas[n] = d_sh[i].reshape(w[n].shape)
        new_m[n] = m_sh[i].reshape(w[n].shape)
        new_v[n] = v_sh[i].reshape(w[n].shape)

    loss = lax.psum(loss_t[0, 0], AXES)
    return (loss, gx.reshape(x.shape), *[grads[n] for n in WEIGHTS], *[deltas[n] for n in WEIGHTS],
            *[new_m[n] for n in WEIGHTS], *[new_v[n] for n in WEIGHTS])
```

```python
import math

import jax
import jax.numpy as jnp
from jax import lax
from jax.experimental import pallas as pl
from jax.experimental.pallas import tpu as pltpu

F32 = jnp.float32
BF16 = jnp.bfloat16
MESH = pl.DeviceIdType.MESH
AXES = ("x", "y", "c")
N_DEV = 8

D_MODEL = 1024
HEAD_DIM = 64
ATTN_W = 512
SSM_W = 512
SSM_GROUPS = 32
SSM_GROUP = 16
SSM_STATE = 64
N_STATE = SSM_GROUPS * SSM_STATE
PLE_DIM = 256
IN_W = 3072
COL_W = IN_W // N_DEV
DILATED = ((128, 1), (512, 4), (2048, 16))
EPS = 1e-6
INV_SQRT2 = 1.0 / math.sqrt(2.0)
INV_SQRT_2PI = 1.0 / math.sqrt(2.0 * math.pi)

ADAM_LR, ADAM_B1, ADAM_B2, ADAM_EPS, ADAM_WD, ADAM_STEP = 0.001, 0.9, 0.999, 1e-08, 0.01, 10

VMEM_LIMIT_V7X = 56 * 1024 * 1024
SUBLANES = 8
LANES = 128


def _params(n_axes=None, vmem=None):
    kw = {}
    if n_axes:
        kw["dimension_semantics"] = ("arbitrary",) * n_axes
    if vmem:
        kw["vmem_limit_bytes"] = vmem
    return pltpu.CompilerParams(**kw)


def _dot(a, b):
    return jnp.dot(a, b, preferred_element_type=F32)


def _dot_nt(a, b):
    return lax.dot_general(a, b, (((1,), (1,)), ((), ())), preferred_element_type=F32)


def _dot_tn(a, b):
    return lax.dot_general(a, b, (((0,), (0,)), ((), ())), preferred_element_type=F32)


def _hdot(a, b):
    return jnp.dot(a, b, precision=lax.Precision.HIGHEST, preferred_element_type=F32)


def _sig(x):
    return 1.0 / (1.0 + jnp.exp(-x))


def _gelu_and_grad(y):
    cdf = 0.5 * (1.0 + lax.erf(y * INV_SQRT2))
    pdf = jnp.exp(-0.5 * y * y) * INV_SQRT_2PI
    return y * cdf, cdf + y * pdf


def _vmem():
    return pl.BlockSpec(memory_space=pltpu.VMEM)


def _full(shape):
    nd = len(shape)
    return pl.BlockSpec(shape, lambda *_: (0,) * nd)


def _all_gather(shards, out_dtypes, name):
    n = len(shards)

    def body(*refs):
        in_refs, out_refs = refs[:n], refs[n:2 * n]
        send_sems, recv_sems = refs[2 * n], refs[2 * n + 1]
        x, y, c = lax.axis_index("x"), lax.axis_index("y"), lax.axis_index("c")
        me, sibling = (x, y, c), (x, y, 1 - c)
        chips = [(1 - x, y), (x, 1 - y), (1 - x, 1 - y)]

        def idx(px, py, pc):
            return 4 * px + 2 * py + pc

        def copy(i, k, block, to):
            ref = out_refs[i].at[idx(*block)]
            return pltpu.make_async_remote_copy(
                src_ref=ref, dst_ref=ref, send_sem=send_sems.at[7 * i + k], recv_sem=recv_sems.at[7 * i + k],
                device_id=to, device_id_type=MESH)

        for i in range(n):
            out_refs[i][idx(*me)] = in_refs[i][...].astype(out_refs[i].dtype)
        first = []
        for i in range(n):
            first.append(copy(i, 0, me, sibling))
            first += [copy(i, 1 + j, me, (*chip, c)) for j, chip in enumerate(chips)]
        for cp in first:
            cp.start()
        passed = []
        for j, chip in enumerate(chips):
            for i in range(n):
                copy(i, 1 + j, (*chip, c), me).wait_recv()
                cp = copy(i, 4 + j, (*chip, c), sibling)
                cp.start()
                passed.append(cp)
        for i in range(n):
            copy(i, 0, sibling, me).wait_recv()
            for j, chip in enumerate(chips):
                copy(i, 4 + j, (*chip, 1 - c), me).wait_recv()
        for cp in first + passed:
            cp.wait_send()

    return pl.pallas_call(
        body, name=name,
        out_shape=[jax.ShapeDtypeStruct((N_DEV,) + s.shape, dt) for s, dt in zip(shards, out_dtypes)],
        in_specs=[_vmem()] * n, out_specs=[_vmem()] * n,
        scratch_shapes=[pltpu.SemaphoreType.DMA((7 * n,)), pltpu.SemaphoreType.DMA((7 * n,))],
        compiler_params=_params(vmem=VMEM_LIMIT_V7X),
    )(*shards)


def _row_chunks(rows):
    chunk = 64 if rows % 64 == 0 else rows
    return chunk, rows // chunk


class _ReduceScatter:
    def __init__(self, shapes):
        self.shapes = shapes
        self.n = len(shapes)

    def scratch(self, dtype):
        return ([pltpu.VMEM(s, dtype) for s in self.shapes]
                + [pltpu.SemaphoreType.DMA((7 * self.n,)), pltpu.SemaphoreType.DMA((7 * self.n,)),
                   pltpu.SemaphoreType.DMA((self.n,))])

    def _copies(self, in_refs, land_refs, send_sems, recv_sems, own_sems):
        x, y, c = lax.axis_index("x"), lax.axis_index("y"), lax.axis_index("c")
        remote, own = [], []
        for i in range(self.n):
            for m in range(1, N_DEV):
                px = 1 - x if m & 4 else x
                py = 1 - y if m & 2 else y
                pc = 1 - c if m & 1 else c
                remote.append(pltpu.make_async_remote_copy(
                    src_ref=in_refs[i].at[4 * px + 2 * py + pc], dst_ref=land_refs[i].at[m - 1],
                    send_sem=send_sems.at[7 * i + m - 1], recv_sem=recv_sems.at[7 * i + m - 1],
                    device_id=(px, py, pc), device_id_type=MESH))
            own.append(pltpu.make_async_copy(in_refs[i].at[4 * x + 2 * y + c], land_refs[i].at[N_DEV - 1],
                                             own_sems.at[i]))
        return remote, own

    def start(self, in_refs, scratch):
        remote, own = self._copies(in_refs, scratch[:self.n], *scratch[self.n:])
        for cp in remote + own:
            cp.start()

    def finish(self, in_refs, scratch, out_refs):
        land_refs = scratch[:self.n]
        remote, own = self._copies(in_refs, land_refs, *scratch[self.n:])
        for cp in own:
            cp.wait()
        for cp in remote:
            cp.wait_recv()
        for i in range(self.n):
            chunk, steps = _row_chunks(self.shapes[i][1])

            def step(s, carry, i=i, chunk=chunk):
                r = pl.ds(pl.multiple_of(s * chunk, chunk), chunk)
                acc = land_refs[i][N_DEV - 1, r, :].astype(F32)
                for m in range(1, N_DEV):
                    acc = acc + land_refs[i][m - 1, r, :].astype(F32)
                out_refs[i][r, :] = acc
                return carry

            lax.fori_loop(0, steps, step, 0)
        for cp in remote:
            cp.wait_send()


def _reduce_scatter(parts, name):
    n = len(parts)
    rs = _ReduceScatter([p.shape for p in parts])

    def body(*refs):
        in_refs, out_refs, scratch = refs[:n], refs[n:2 * n], refs[2 * n:]
        rs.start(in_refs, scratch)
        rs.finish(in_refs, scratch, out_refs)

    return pl.pallas_call(
        body, name=name,
        out_shape=[jax.ShapeDtypeStruct(p.shape[1:], F32) for p in parts],
        in_specs=[_vmem()] * n, out_specs=[_vmem()] * n,
        scratch_shapes=rs.scratch(parts[0].dtype),
        compiler_params=_params(vmem=VMEM_LIMIT_V7X),
    )(*parts)


def _adamw_math(w, g, m, v):
    m = ADAM_B1 * m + (1.0 - ADAM_B1) * g
    v = ADAM_B2 * v + (1.0 - ADAM_B2) * (g * g)
    m_hat = m / (1.0 - ADAM_B1 ** ADAM_STEP)
    v_hat = v / (1.0 - ADAM_B2 ** ADAM_STEP)
    delta = -ADAM_LR * (m_hat / (jnp.sqrt(v_hat) + ADAM_EPS) + ADAM_WD * w)
    return delta, m, v


def _adamw_shards(gs, ws, ms, vs):
    n = len(gs)

    def body(*refs):
        g_refs, w_refs, m_refs, v_refs = (refs[k * n:(k + 1) * n] for k in range(4))
        d_out, m_out, v_out = (refs[(4 + k) * n:(5 + k) * n] for k in range(3))
        for i in range(n):
            chunk, steps = _row_chunks(gs[i].shape[0])

            def step(s, carry, i=i, chunk=chunk):
                r = pl.ds(pl.multiple_of(s * chunk, chunk), chunk)
                d, m, v = _adamw_math(w_refs[i][r, :], g_refs[i][r, :], m_refs[i][r, :], v_refs[i][r, :])
                d_out[i][r, :] = d
                m_out[i][r, :] = m
                v_out[i][r, :] = v
                return carry

            lax.fori_loop(0, steps, step, 0)

    shapes = [jax.ShapeDtypeStruct(g.shape, F32) for g in gs]
    outs = pl.pallas_call(
        body, name="adamw_shards", out_shape=shapes * 3,
        in_specs=[_vmem()] * (4 * n), out_specs=[_vmem()] * (3 * n),
        compiler_params=_params(vmem=VMEM_LIMIT_V7X),
    )(*gs, *ws, *ms, *vs)
    return outs[:n], outs[n:2 * n], outs[2 * n:]


def _small_update(gathered, w, m, v):
    rows = w.shape[0]
    chunk, steps = _row_chunks(rows)

    def body(ga_ref, w_ref, m_ref, v_ref, g_out, d_out, m_out, v_out):
        def step(s, carry):
            r = pl.ds(pl.multiple_of(s * chunk, chunk), chunk)
            g = ga_ref[0, r, :]
            for j in range(1, N_DEV):
                g = g + ga_ref[j, r, :]
            d, mm, vv = _adamw_math(w_ref[r, :], g, m_ref[r, :], v_ref[r, :])
            g_out[r, :] = g
            d_out[r, :] = d
            m_out[r, :] = mm
            v_out[r, :] = vv
            return carry

        lax.fori_loop(0, steps, step, 0)

    return pl.pallas_call(
        body, name="small_update", out_shape=[jax.ShapeDtypeStruct(w.shape, F32)] * 4,
        in_specs=[_vmem()] * 4, out_specs=[_vmem()] * 4,
    )(gathered, w, m, v)


def _zoh(lr, li, logdt, btr, bti):
    dt = jnp.exp(logdt)
    mag = jnp.exp(lr * dt)
    th = li * dt
    ar = mag * jnp.cos(th)
    ai = mag * jnp.sin(th)
    den = lr * lr + li * li
    nr = ar - 1.0
    cr = (nr * lr + ai * li) / den
    ci = (ai * lr - nr * li) / den
    return ar, ai, cr * btr - ci * bti, cr * bti + ci * btr


def _zoh_fwd(lr, li, logdt, btr, bti):
    def body(lr_ref, li_ref, dt_ref, br_ref, bi_ref, ar_ref, ai_ref, bbr_ref, bbi_ref):
        ar, ai, bbr, bbi = _zoh(lr_ref[...], li_ref[...], dt_ref[...], br_ref[...], bi_ref[...])
        ar_ref[...] = ar
        ai_ref[...] = ai
        bbr_ref[...] = bbr
        bbi_ref[...] = bbi

    s = jax.ShapeDtypeStruct
    return pl.pallas_call(
        body, name="zoh_fwd",
        out_shape=[s(lr.shape, F32), s(lr.shape, F32), s(btr.shape, F32), s(btr.shape, F32)],
        in_specs=[_vmem()] * 5, out_specs=[_vmem()] * 4,
    )(lr, li, logdt, btr, bti)


def _zoh_bwd(lr, li, logdt, btr, bti, dar, dai, dbbr, dbbi):
    def body(lr_ref, li_ref, dt_ref, br_ref, bi_ref, dar_ref, dai_ref, dbbr_ref, dbbi_ref,
             glr_ref, gli_ref, gdt_ref, gbr_ref, gbi_ref):
        _, vjp = jax.vjp(_zoh, lr_ref[...], li_ref[...], dt_ref[...], br_ref[...], bi_ref[...])
        glr, gli, gdt, gbr, gbi = vjp((dar_ref[...], dai_ref[...], dbbr_ref[...], dbbi_ref[...]))
        glr_ref[...] = glr
        gli_ref[...] = gli
        gdt_ref[...] = gdt
        gbr_ref[...] = gbr
        gbi_ref[...] = gbi

    s = jax.ShapeDtypeStruct
    return pl.pallas_call(
        body, name="zoh_bwd",
        out_shape=[s(lr.shape, F32), s(lr.shape, F32), s(logdt.shape, F32), s(btr.shape, F32), s(btr.shape, F32)],
        in_specs=[_vmem()] * 9, out_specs=[_vmem()] * 5,
    )(lr, li, logdt, btr, bti, dar, dai, dbbr, dbbi)


def _blockdiag(t):
    g, r, s = t.shape
    t = t.reshape(4, 8, r, s)
    out = jnp.einsum("jirs,ik->jirks", t, jnp.eye(8, dtype=t.dtype))
    return out.reshape(4, 8 * r, 8 * s)


def _blockdiag_extract(m, r, s):
    m = m.reshape(4, 8, r, 8, s)
    out = jnp.einsum("jirks,ik->jirs", m, jnp.eye(8, dtype=m.dtype))
    return out.reshape(32, r, s)


def _head_ones():
    r = jnp.arange(ATTN_W) // HEAD_DIM
    return (r[:, None] == r[None, :]).astype(F32)


def _in_proj(x2, g_mix, w_in_g, ones_bd, gq_t, gk_t):
    t_tok = x2.shape[0]
    tm = min(512, t_tok)

    def body(x_ref, g_ref, w_ref, ones_ref, gq_ref, gk_ref, z_ref, qh_ref, kh_ref, vb_ref, xn_ref):
        x = x_ref[...]
        r = lax.rsqrt(jnp.mean(x * x, axis=-1, keepdims=True) + EPS)
        xn = (x * r * g_ref[...]).astype(BF16)
        xn_ref[...] = xn
        for j in range(N_DEV):
            z_ref[:, j * COL_W:(j + 1) * COL_W] = _dot(xn, w_ref[j])
        ones = ones_ref[...]
        q = z_ref[:, 0:ATTN_W]
        rq = lax.rsqrt(_hdot(q * q, ones) * (1.0 / HEAD_DIM) + EPS)
        qh_ref[...] = (q * rq * gq_ref[...] * (HEAD_DIM ** -0.5)).astype(BF16)
        k = z_ref[:, ATTN_W:2 * ATTN_W]
        rk = lax.rsqrt(_hdot(k * k, ones) * (1.0 / HEAD_DIM) + EPS)
        kh_ref[...] = (k * rk * gk_ref[...]).astype(BF16)
        vb_ref[...] = z_ref[:, 2 * ATTN_W:3 * ATTN_W].astype(BF16)

    row = lambda i: (i, 0)
    s = jax.ShapeDtypeStruct
    return pl.pallas_call(
        body, name="in_proj", grid=(t_tok // tm,),
        out_shape=[s((t_tok, IN_W), F32), s((t_tok, ATTN_W), BF16), s((t_tok, ATTN_W), BF16),
                   s((t_tok, ATTN_W), BF16), s((t_tok, D_MODEL), BF16)],
        in_specs=[pl.BlockSpec((tm, D_MODEL), row), _full(g_mix.shape), _full(w_in_g.shape), _full(ones_bd.shape),
                  _full(gq_t.shape), _full(gk_t.shape)],
        out_specs=[pl.BlockSpec((tm, IN_W), row), pl.BlockSpec((tm, ATTN_W), row), pl.BlockSpec((tm, ATTN_W), row),
                   pl.BlockSpec((tm, ATTN_W), row), pl.BlockSpec((tm, D_MODEL), row)],
        compiler_params=_params(1, VMEM_LIMIT_V7X),
    )(x2, g_mix, w_in_g, ones_bd, gq_t, gk_t)


TQ = 128
NEG = -1e30


def _head_col(t, lm):
    return jnp.max(jnp.where(lm, t, NEG), axis=-1, keepdims=True)


def _head_masks():
    lane = lax.broadcasted_iota(jnp.int32, (1, 1, LANES), 2)
    return [(lane // HEAD_DIM) == h for h in range(LANES // HEAD_DIM)]


def _gather_classes(ref, dil, nt, tq, dtype):
    length = nt * tq
    if dil == 1:
        return ref[...].astype(dtype).reshape(nt, tq, LANES)
    parts = [ref[pl.ds(r, length, stride=dil), :].astype(dtype).reshape(nt, tq, LANES) for r in range(dil)]
    return jnp.concatenate(parts, axis=0)


def _scatter_classes(ref, val, dil, nt, tq, add):
    length = nt * tq
    for r in range(dil):
        rows = pl.ds(r, length, stride=dil) if dil > 1 else slice(None)
        part = val[r * nt:(r + 1) * nt].reshape(length, LANES)
        ref[rows, :] = ref[rows, :] + part if add else part


def _with_prev_tile(t3, dil, nt):
    parts = []
    for r in range(dil):
        t = t3[r * nt:(r + 1) * nt]
        parts.append(jnp.concatenate([t[:1], t[:-1]], axis=0))
    prev = parts[0] if dil == 1 else jnp.concatenate(parts, axis=0)
    return jnp.concatenate([prev, t3], axis=1)


def _band_valid(dil, nt, tq):
    if nt == 1:
        shape = (dil, tq, tq)
        return lax.broadcasted_iota(jnp.int32, shape, 1) >= lax.broadcasted_iota(jnp.int32, shape, 2)
    shape = (dil * nt, tq, 2 * tq)
    b = lax.broadcasted_iota(jnp.int32, shape, 0)
    c = lax.broadcasted_iota(jnp.int32, shape, 2)
    d = tq + lax.broadcasted_iota(jnp.int32, shape, 1) - c
    return (d >= 0) & (d <= tq) & (((b & (nt - 1)) != 0) | (c >= tq))


def _window_tiling(seq, window, dil):
    length = seq // dil
    tq = min(TQ, length)
    nt = length // tq
    assert length % tq == 0 and nt & (nt - 1) == 0 and (nt == 1 or window == tq * dil)
    return nt, tq


def _bqk(a, b):
    return jnp.einsum("bqd,bkd->bqk", a, b, preferred_element_type=F32)


def _bqd(a, b):
    return jnp.einsum("bqk,bkd->bqd", a, b, preferred_element_type=F32)


def _bkd(a, b):
    return jnp.einsum("bqk,bqd->bkd", a, b, preferred_element_type=F32)


def _attn_fwd(qh, kh, vb, z, nb, seq):
    t_tok = nb * seq
    n_win = len(DILATED)

    def body(q_ref, k_ref, v_ref, ga_ref, o_ref, l_ref, ag_ref, qf, kf, vf, oc, lc):
        qf[...] = q_ref[...].astype(F32)
        kf[...] = k_ref[...].astype(F32)
        vf[...] = v_ref[...].astype(F32)
        lms = _head_masks()
        for w, (window, dil) in enumerate(DILATED):
            nt, tq = _window_tiling(seq, window, dil)
            q3 = _gather_classes(qf, dil, nt, tq, BF16)
            k3 = _gather_classes(kf, dil, nt, tq, BF16)
            v3 = _gather_classes(vf, dil, nt, tq, BF16)
            if nt > 1:
                k3, v3 = _with_prev_tile(k3, dil, nt), _with_prev_tile(v3, dil, nt)
            valid = _band_valid(dil, nt, tq)
            o = jnp.zeros(q3.shape, F32)
            lse = jnp.zeros(q3.shape, F32)
            for lm in lms:
                s = _bqk(jnp.where(lm, q3, jnp.zeros_like(q3)), k3)
                m = jnp.max(jnp.where(valid, s, NEG), axis=-1, keepdims=True)
                p = jnp.where(valid, jnp.exp(s - m), 0.0)
                den = jnp.sum(p, axis=-1, keepdims=True)
                o = jnp.where(lm, _bqd(p.astype(BF16), v3) / den, o)
                lse = jnp.where(lm, m + jnp.log(den), lse)
            _scatter_classes(oc.at[w], o, dil, nt, tq, add=False)
            _scatter_classes(lc.at[w], lse, dil, nt, tq, add=False)
        mx = lc[0]
        for w in range(1, n_win):
            mx = jnp.maximum(mx, lc[w])
        tot = jnp.zeros_like(mx)
        o = jnp.zeros_like(mx)
        for w in range(n_win):
            e = jnp.exp(lc[w] - mx)
            tot = tot + e
            o = o + e * oc[w]
        o = o / tot
        o_ref[...] = o
        l_ref[...] = mx + jnp.log(tot)
        ga = ga_ref[...]
        ag_ref[...] = (o * ga * _sig(ga)).astype(BF16)

    blk = pl.BlockSpec((seq, LANES), lambda b, hp: (b, hp))
    s = jax.ShapeDtypeStruct
    return pl.pallas_call(
        body, name="attn_fwd", grid=(nb, ATTN_W // LANES),
        out_shape=[s((t_tok, ATTN_W), F32), s((t_tok, ATTN_W), F32), s((t_tok, ATTN_W), BF16)],
        in_specs=[blk, blk, blk, pl.BlockSpec((seq, LANES), lambda b, hp: (b, 3 * ATTN_W // LANES + hp))],
        out_specs=[blk, blk, blk],
        scratch_shapes=[pltpu.VMEM((seq, LANES), F32)] * 3 + [pltpu.VMEM((n_win, seq, LANES), F32)] * 2,
        compiler_params=_params(2, VMEM_LIMIT_V7X),
    )(qh, kh, vb, z)


SCAN_COLS = 512


def _to_segments(dst_ref, val):
    seg = val.shape[0] // SUBLANES
    for n in range(dst_ref.shape[0]):
        for s in range(SUBLANES):
            dst_ref[n, pl.ds(s, seg, stride=SUBLANES), :] = val[s * seg:(s + 1) * seg, n * LANES:(n + 1) * LANES]


def _from_segments(src_ref):
    seg = src_ref.shape[1] // SUBLANES
    return jnp.concatenate(
        [jnp.concatenate([src_ref[n, pl.ds(s, seg, stride=SUBLANES), :] for s in range(SUBLANES)], axis=0)
         for n in range(src_ref.shape[0])], axis=1)


def _scan_chunk(re_ref, im_ref, a_re_ref, a_im_ref, carry_re, carry_im, rows, reverse, visit=None):
    seg = rows // SUBLANES
    assert seg & (seg - 1) == 0
    rowi = lax.broadcasted_iota(jnp.int32, (SUBLANES, SCAN_COLS), 0)
    edge = (SUBLANES - 1) if reverse else 0
    last = 0 if reverse else SUBLANES - 1
    at_edge = rowi == edge

    def cmul(ar, ai, br, bi):
        return ar * br - ai * bi, ar * bi + ai * br

    for c0 in range(0, N_STATE, SCAN_COLS):
        cols = slice(c0, c0 + SCAN_COLS)
        a1r = jnp.broadcast_to(a_re_ref[:, cols], (SUBLANES, SCAN_COLS))
        a1i = jnp.broadcast_to(a_im_ref[:, cols], (SUBLANES, SCAN_COLS))
        if reverse:
            a1i = -a1i

        def block_of(i):
            j = (seg - 1 - i) if reverse else i
            return j, pl.ds(pl.multiple_of(j * SUBLANES, SUBLANES), SUBLANES)

        def local(i, carry, cols=cols, a1r=a1r, a1i=a1i):
            xr, xi = carry
            _, blk = block_of(i)
            nr, ni = cmul(a1r, a1i, xr, xi)
            xr, xi = nr + re_ref[blk, cols], ni + im_ref[blk, cols]
            re_ref[blk, cols] = xr
            im_ref[blk, cols] = xi
            return xr, xi

        zero = jnp.zeros((SUBLANES, SCAN_COLS), F32)
        er, ei = lax.fori_loop(0, seg, local, (zero, zero))

        pr, pi = a1r, a1i
        for _ in range(seg.bit_length() - 1):
            pr, pi = cmul(pr, pi, pr, pi)
        cr, ci = carry_re[:, cols], carry_im[:, cols]
        inr, ini = cmul(pr, pi, cr, ci)
        er = er + jnp.where(at_edge, inr, 0.0)
        ei = ei + jnp.where(at_edge, ini, 0.0)
        for sft in (1, 2, 4):
            shift, keep = (SUBLANES - sft, rowi < SUBLANES - sft) if reverse else (sft, rowi >= sft)
            rs = jnp.where(keep, pltpu.roll(er, shift, 0), 0.0)
            ims = jnp.where(keep, pltpu.roll(ei, shift, 0), 0.0)
            dr, di = cmul(pr, pi, rs, ims)
            er, ei = er + dr, ei + di
            pr, pi = cmul(pr, pi, pr, pi)
        carry_re[:, cols] = jnp.broadcast_to(er[last:last + 1, :], (SUBLANES, SCAN_COLS))
        carry_im[:, cols] = jnp.broadcast_to(ei[last:last + 1, :], (SUBLANES, SCAN_COLS))
        one = (SUBLANES - 1) if reverse else 1
        kr = jnp.where(at_edge, cr, pltpu.roll(er, one, 0))
        ki = jnp.where(at_edge, ci, pltpu.roll(ei, one, 0))

        def fix(i, carry, cols=cols, a1r=a1r, a1i=a1i):
            kr, ki, acc = carry
            j, blk = block_of(i)
            kr, ki = cmul(a1r, a1i, kr, ki)
            xr, xi = re_ref[blk, cols] + kr, im_ref[blk, cols] + ki
            re_ref[blk, cols] = xr
            im_ref[blk, cols] = xi
            if visit is not None:
                acc = visit(cols, j, xr, xi, acc)
            return kr, ki, acc

        _, _, acc = lax.fori_loop(0, seg, fix, (kr, ki, (zero, zero)))
        if visit is not None:
            visit(cols, None, None, None, acc)


def _ssm_fwd(z, a_re, a_im, bb_re, bb_im, cc_re, cc_im, d_skip, w_glu, b_glu, nb, seq):
    t_tok = nb * seq
    tc = min(256, seq)
    nch = seq // tc
    grp = N_STATE // 4

    def body(u_ref, gs_ref, ar_ref, ai_ref, bbr_ref, bbi_ref, ccr_ref, cci_ref, d_ref, wg_ref, bg_ref,
             xr_ref, xi_ref, y_ref, sg_ref, car_re, car_im, seg_u, seg_y):
        @pl.when(pl.program_id(1) == 0)
        def _():
            car_re[...] = jnp.zeros_like(car_re)
            car_im[...] = jnp.zeros_like(car_im)

        u = u_ref[...]
        _to_segments(seg_u, u)
        for j in range(4):
            uj = seg_u[j].astype(BF16)
            xr_ref[:, j * grp:(j + 1) * grp] = _dot(uj, bbr_ref[j])
            xi_ref[:, j * grp:(j + 1) * grp] = _dot(uj, bbi_ref[j])
        _scan_chunk(xr_ref, xi_ref, ar_ref, ai_ref, car_re, car_im, tc, reverse=False)
        for j in range(4):
            xr = xr_ref[:, j * grp:(j + 1) * grp].astype(BF16)
            xi = xi_ref[:, j * grp:(j + 1) * grp].astype(BF16)
            seg_y[j] = _dot(xr, ccr_ref[j]) - _dot(xi, cci_ref[j])
        y = _from_segments(seg_y) + d_ref[...] * u
        y_ref[...] = y
        yg, _ = _gelu_and_grad(y)
        gl = _dot(yg.astype(BF16), wg_ref[...]) + bg_ref[...]
        gs = gs_ref[...]
        sg_ref[...] = (yg * _sig(gl) * gs * _sig(gs)).astype(BF16)

    umap = lambda b, ch: (b * nch + ch, 4)
    gmap = lambda b, ch: (b * nch + ch, 5)
    row = lambda b, ch: (b * nch + ch, 0)
    s = jax.ShapeDtypeStruct
    consts = [a_re, a_im, bb_re, bb_im, cc_re, cc_im, d_skip, w_glu, b_glu]
    return pl.pallas_call(
        body, name="ssm_fwd", grid=(nb, nch),
        out_shape=[s((t_tok, N_STATE), F32), s((t_tok, N_STATE), F32), s((t_tok, SSM_W), F32),
                   s((t_tok, SSM_W), BF16)],
        in_specs=[pl.BlockSpec((tc, SSM_W), umap), pl.BlockSpec((tc, SSM_W), gmap)] + [_full(c.shape) for c in consts],
        out_specs=[pl.BlockSpec((tc, N_STATE), row), pl.BlockSpec((tc, N_STATE), row),
                   pl.BlockSpec((tc, SSM_W), row), pl.BlockSpec((tc, SSM_W), row)],
        scratch_shapes=[pltpu.VMEM((SUBLANES, N_STATE), F32), pltpu.VMEM((SUBLANES, N_STATE), F32),
                        pltpu.VMEM((4, tc, LANES), F32), pltpu.VMEM((4, tc, LANES), F32)],
        compiler_params=_params(2, VMEM_LIMIT_V7X),
    )(z, z, *consts)


def _tail(x2, tg2, ag, sg, p2, w_out, w_g, w_p, g_ple):
    t_tok = x2.shape[0]
    tm = min(256, t_tok)
    nt = t_tok // tm
    half = ATTN_W

    def body(x_ref, tg_ref, ag_ref, sg_ref, p_ref, wo_ref, wg_ref, wp_ref, gp_ref,
             dmix_ref, dh1_ref, loss_ref, dgp_ref, dwo_ref, dwg_ref, dwp_ref, acc_o, acc_g, acc_p):
        i = pl.program_id(0)

        @pl.when(i == 0)
        def _():
            loss_ref[...] = jnp.zeros_like(loss_ref)
            dgp_ref[...] = jnp.zeros_like(dgp_ref)
            acc_o[...] = jnp.zeros_like(acc_o)
            acc_g[...] = jnp.zeros_like(acc_g)
            acc_p[...] = jnp.zeros_like(acc_p)

        ag_t, sg_t = ag_ref[...], sg_ref[...]
        h1 = x_ref[...] + _dot(ag_t, wo_ref[0:half, :]) + _dot(sg_t, wo_ref[half:2 * half, :])
        r2 = lax.rsqrt(jnp.mean(h1 * h1, axis=-1, keepdims=True) + EPS)
        hnorm = h1 * r2
        gp = gp_ref[...]
        hn = (hnorm * gp).astype(BF16)
        gate = _sig(_dot(hn, wg_ref[...]))
        pb = p_ref[...].astype(BF16)
        pp = jnp.concatenate([_dot(pb, wp_ref[j]) for j in range(N_DEV)], axis=-1)
        h2 = h1 + gate * pp
        err = h2 - tg_ref[...]
        loss_ref[...] += 0.5 * jnp.sum(err * err) * (1.0 / D_MODEL)
        dh2 = err * (1.0 / D_MODEL)
        dpp = (dh2 * gate).astype(BF16)
        dgpre = (dh2 * pp * gate * (1.0 - gate)).astype(BF16)
        acc_p[...] += _dot_tn(pb, dpp)
        acc_g[...] += _dot_tn(hn, dgpre)
        dhn = _dot_nt(dgpre, wg_ref[...])
        dgp_ref[...] += jnp.sum(dhn * hnorm, axis=0, keepdims=True)
        a = dhn * gp
        dh1 = dh2 + r2 * (a - hnorm * jnp.mean(a * hnorm, axis=-1, keepdims=True))
        dh1_ref[...] = dh1
        dh1b = dh1.astype(BF16)
        acc_o[0:half, :] += _dot_tn(ag_t, dh1b)
        acc_o[half:2 * half, :] += _dot_tn(sg_t, dh1b)
        dmix_ref[...] = _dot_nt(dh1b, wo_ref[...])

        @pl.when(i == nt - 1)
        def _():
            dwo_ref[...] = acc_o[...].astype(BF16)
            dwg_ref[...] = acc_g[...].astype(BF16)
            for j in range(N_DEV):
                dwp_ref[j] = acc_p[:, j * LANES:(j + 1) * LANES].astype(BF16)

    row = lambda i: (i, 0)
    s = jax.ShapeDtypeStruct
    return pl.pallas_call(
        body, name="tail_fwd_bwd", grid=(nt,),
        out_shape=[s((t_tok, D_MODEL), F32), s((t_tok, D_MODEL), F32), s((SUBLANES, LANES), F32),
                   s((1, D_MODEL), F32), s((D_MODEL, D_MODEL), BF16), s((D_MODEL, D_MODEL), BF16),
                   s((N_DEV, PLE_DIM, LANES), BF16)],
        in_specs=[pl.BlockSpec((tm, D_MODEL), row), pl.BlockSpec((tm, D_MODEL), row),
                  pl.BlockSpec((tm, half), row), pl.BlockSpec((tm, half), row), pl.BlockSpec((tm, PLE_DIM), row),
                  _full(w_out.shape), _full(w_g.shape), _full(w_p.shape), _full(g_ple.shape)],
        out_specs=[pl.BlockSpec((tm, D_MODEL), row), pl.BlockSpec((tm, D_MODEL), row), _full((SUBLANES, LANES)),
                   _full((1, D_MODEL)), _full((D_MODEL, D_MODEL)), _full((D_MODEL, D_MODEL)),
                   _full((N_DEV, PLE_DIM, LANES))],
        scratch_shapes=[pltpu.VMEM((D_MODEL, D_MODEL), F32), pltpu.VMEM((D_MODEL, D_MODEL), F32),
                        pltpu.VMEM((PLE_DIM, D_MODEL), F32)],
        compiler_params=_params(1, VMEM_LIMIT_V7X),
    )(x2, tg2, ag, sg, p2, w_out, w_g, w_p, g_ple)


def _attn_bwd(qh, kh, vb, z, o, lse, dmix, nb, seq, parts):
    t_tok = nb * seq
    n_rs = len(parts)
    rs = _ReduceScatter([p.shape for p in parts])
    n_steps = (nb, ATTN_W // LANES)

    def body(*refs):
        (q_ref, k_ref, v_ref, ga_ref, o_ref, l_ref, da_ref), refs = refs[:7], refs[7:]
        part_refs, refs = refs[:n_rs], refs[n_rs:]
        (dq_ref, dk_ref, dv_ref, dga_ref), refs = refs[:4], refs[4:]
        g_refs, refs = refs[:n_rs], refs[n_rs:]
        (qf, kf, vf, dof, dlf), rs_scratch = refs[:5], refs[5:]
        b, hp = pl.program_id(0), pl.program_id(1)

        @pl.when((b == 0) & (hp == 0))
        def _():
            rs.start(part_refs, rs_scratch)

        ga, o_t, da = ga_ref[...], o_ref[...], da_ref[...]
        sga = _sig(ga)
        d_o = da * ga * sga
        dga_ref[...] = da * o_t * sga * (1.0 + ga * (1.0 - sga))
        lane = lax.broadcasted_iota(jnp.int32, (1, LANES), 1)
        d_oo = d_o * o_t
        delta = jnp.zeros_like(d_oo)
        for h in range(LANES // HEAD_DIM):
            lm2 = (lane // HEAD_DIM) == h
            delta = jnp.where(lm2, jnp.sum(jnp.where(lm2, d_oo, 0.0), axis=-1, keepdims=True), delta)
        qf[...] = q_ref[...].astype(F32)
        kf[...] = k_ref[...].astype(F32)
        vf[...] = v_ref[...].astype(F32)
        dof[...] = d_o
        dlf[...] = delta
        dq_ref[...] = jnp.zeros_like(dq_ref)
        dk_ref[...] = jnp.zeros_like(dk_ref)
        dv_ref[...] = jnp.zeros_like(dv_ref)
        lms = _head_masks()
        for window, dil in DILATED:
            nt, tq = _window_tiling(seq, window, dil)
            q3 = _gather_classes(qf, dil, nt, tq, BF16)
            k3 = _gather_classes(kf, dil, nt, tq, BF16)
            v3 = _gather_classes(vf, dil, nt, tq, BF16)
            do3 = _gather_classes(dof, dil, nt, tq, BF16)
            lt3 = _gather_classes(l_ref, dil, nt, tq, F32)
            dl3 = _gather_classes(dlf, dil, nt, tq, F32)
            if nt > 1:
                k3, v3 = _with_prev_tile(k3, dil, nt), _with_prev_tile(v3, dil, nt)
            valid = _band_valid(dil, nt, tq)
            dq = jnp.zeros(q3.shape, F32)
            dk = jnp.zeros(k3.shape, F32)
            dv = jnp.zeros(k3.shape, F32)
            for lm in lms:
                qm = jnp.where(lm, q3, jnp.zeros_like(q3))
                dom = jnp.where(lm, do3, jnp.zeros_like(do3))
                p = jnp.where(valid, jnp.exp(_bqk(qm, k3) - _head_col(lt3, lm)), 0.0)
                dv = dv + _bkd(p.astype(BF16), dom)
                ds = (p * (_bqk(dom, v3) - _head_col(dl3, lm))).astype(BF16)
                dq = dq + jnp.where(lm, _bqd(ds, k3), 0.0)
                dk = dk + _bkd(ds, qm)
            _scatter_classes(dq_ref, dq, dil, nt, tq, add=True)
            for ref, g in ((dk_ref, dk), (dv_ref, dv)):
                if nt > 1:
                    own, prev = g[:, tq:, :], g[:, :tq, :]
                    shifted = []
                    for r in range(dil):
                        t = prev[r * nt:(r + 1) * nt]
                        shifted.append(jnp.concatenate([t[1:], jnp.zeros_like(t[:1])], axis=0))
                    g = own + (shifted[0] if dil == 1 else jnp.concatenate(shifted, axis=0))
                _scatter_classes(ref, g, dil, nt, tq, add=True)

        @pl.when((b == n_steps[0] - 1) & (hp == n_steps[1] - 1))
        def _():
            rs.finish(part_refs, rs_scratch, g_refs)

    blk = pl.BlockSpec((seq, LANES), lambda b, hp: (b, hp))
    s = jax.ShapeDtypeStruct
    outs = pl.pallas_call(
        body, name="attn_bwd", grid=n_steps,
        out_shape=[s((t_tok, ATTN_W), F32)] * 4 + [s(p.shape[1:], F32) for p in parts],
        in_specs=[blk, blk, blk, pl.BlockSpec((seq, LANES), lambda b, hp: (b, 3 * ATTN_W // LANES + hp)), blk, blk,
                  blk] + [pl.BlockSpec(memory_space=pl.ANY)] * n_rs,
        out_specs=[blk] * 4 + [_full(p.shape[1:]) for p in parts],
        scratch_shapes=[pltpu.VMEM((seq, LANES), F32)] * 5 + rs.scratch(parts[0].dtype),
        compiler_params=_params(2, VMEM_LIMIT_V7X),
    )(qh, kh, vb, z, o, lse, dmix, *parts)
    return outs[:4], outs[4:]


def _ssm_bwd(z, dmix, y, x_re, x_im, a_re, a_im, bb_re, bb_im, cc_re, cc_im, d_skip, w_glu, b_glu, nb, seq):
    t_tok = nb * seq
    tc = min(256, seq)
    nch = seq // tc
    grp = N_STATE // 4

    def body(u_ref, gs_ref, ds_ref, y_ref, xr_ref, xi_ref, xpr_ref, xpi_ref,
             ar_ref, ai_ref, bbr_ref, bbi_ref, ccr_ref, cci_ref, d_ref, wg_ref, bg_ref,
             du_ref, dgs_ref, dwg_ref, dbg_ref, dd_ref, dar_ref, dai_ref, dbbr_ref, dbbi_ref, dccr_ref, dcci_ref,
             lam_re, lam_im, car_re, car_im, acc_wg, seg_a, seg_b, ent_re, ent_im):
        step = pl.program_id(1)
        first_chunk = step == nch - 1

        @pl.when((pl.program_id(0) == 0) & (step == 0))
        def _():
            acc_wg[...] = jnp.zeros_like(acc_wg)
            for ref in (dbg_ref, dd_ref, dar_ref, dai_ref, dbbr_ref, dbbi_ref, dccr_ref, dcci_ref):
                ref[...] = jnp.zeros_like(ref)

        @pl.when(step == 0)
        def _():
            car_re[...] = jnp.zeros_like(car_re)
            car_im[...] = jnp.zeros_like(car_im)

        u, gs, dssm, y = u_ref[...], gs_ref[...], ds_ref[...], y_ref[...]
        yg, dgelu = _gelu_and_grad(y)
        ygb = yg.astype(BF16)
        sgl = _sig(_dot(ygb, wg_ref[...]) + bg_ref[...])
        sgs = _sig(gs)
        dout = dssm * gs * sgs
        dgs_ref[...] = dssm * yg * sgl * sgs * (1.0 + gs * (1.0 - sgs))
        dgl = dout * yg * sgl * (1.0 - sgl)
        dglb = dgl.astype(BF16)
        dyg = dout * sgl + _dot_nt(dglb, wg_ref[...])
        acc_wg[...] += _dot_tn(ygb, dglb)
        dbg_ref[...] += jnp.sum(dgl, axis=0, keepdims=True)
        dy = dyg * dgelu
        dd_ref[...] += jnp.sum(dy * u, axis=0, keepdims=True)
        _to_segments(seg_a, dy)
        _to_segments(seg_b, u)
        for j in range(4):
            dyj = seg_a[j].astype(BF16)
            sl = slice(j * grp, (j + 1) * grp)
            lam_re[:, sl] = _dot_nt(dyj, ccr_ref[j])
            lam_im[:, sl] = -_dot_nt(dyj, cci_ref[j])
            dccr_ref[j] += _dot_tn(xr_ref[:, sl].astype(BF16), dyj)
            dcci_ref[j] -= _dot_tn(xi_ref[:, sl].astype(BF16), dyj)

        keep_prev = jnp.where(first_chunk, 0.0, 1.0)
        seg = tc // SUBLANES
        last_blk = pl.ds((seg - 1) * SUBLANES, SUBLANES)
        row0 = lax.broadcasted_iota(jnp.int32, (SUBLANES, N_STATE), 0) == 0
        for src, prev, dst in ((xr_ref, xpr_ref, ent_re), (xi_ref, xpi_ref, ent_im)):
            before = jnp.broadcast_to(prev[SUBLANES - 1:SUBLANES, :] * keep_prev, (SUBLANES, N_STATE))
            dst[...] = jnp.where(row0, before, pltpu.roll(src[last_blk, :], 1, 0))

        def visit(cols, j, lr, li, acc):
            if j is None:
                dar_ref[:, cols] += jnp.sum(acc[0], axis=0, keepdims=True)
                dai_ref[:, cols] += jnp.sum(acc[1], axis=0, keepdims=True)
                return None
            blk = pl.ds(pl.multiple_of(jnp.maximum(j - 1, 0) * SUBLANES, SUBLANES), SUBLANES)
            inside = j > 0
            xpr = jnp.where(inside, xr_ref[blk, cols], ent_re[:, cols])
            xpi = jnp.where(inside, xi_ref[blk, cols], ent_im[:, cols])
            return acc[0] + lr * xpr + li * xpi, acc[1] + li * xpr - lr * xpi

        _scan_chunk(lam_re, lam_im, ar_ref, ai_ref, car_re, car_im, tc, reverse=True, visit=visit)

        for j in range(4):
            sl = slice(j * grp, (j + 1) * grp)
            lr = lam_re[:, sl].astype(BF16)
            li = lam_im[:, sl].astype(BF16)
            uj = seg_b[j].astype(BF16)
            seg_a[j] = _dot_nt(lr, bbr_ref[j]) + _dot_nt(li, bbi_ref[j])
            dbbr_ref[j] += _dot_tn(uj, lr)
            dbbi_ref[j] += _dot_tn(uj, li)
        du_ref[...] = _from_segments(seg_a) + dy * d_ref[...]

        @pl.when((pl.program_id(0) == nb - 1) & (step == nch - 1))
        def _():
            dwg_ref[...] = acc_wg[...].astype(BF16)

    rev = lambda b, ch: b * nch + (nch - 1 - ch)
    umap = lambda b, ch: (rev(b, ch), 4)
    gmap = lambda b, ch: (rev(b, ch), 5)
    smap = lambda b, ch: (rev(b, ch), 1)
    row = lambda b, ch: (rev(b, ch), 0)
    prev = lambda b, ch: (jnp.maximum(rev(b, ch) * (tc // SUBLANES) - 1, 0), 0)
    s = jax.ShapeDtypeStruct
    consts = [a_re, a_im, bb_re, bb_im, cc_re, cc_im, d_skip, w_glu, b_glu]
    acc_shapes = [s((1, SSM_W), F32), s((1, SSM_W), F32), s((1, N_STATE), F32), s((1, N_STATE), F32),
                  s(bb_re.shape, F32), s(bb_re.shape, F32), s(cc_re.shape, F32), s(cc_re.shape, F32)]
    return pl.pallas_call(
        body, name="ssm_bwd", grid=(nb, nch),
        out_shape=[s((t_tok, SSM_W), F32), s((t_tok, SSM_W), F32), s((SSM_W, SSM_W), BF16)] + acc_shapes,
        in_specs=[pl.BlockSpec((tc, SSM_W), umap), pl.BlockSpec((tc, SSM_W), gmap), pl.BlockSpec((tc, SSM_W), smap),
                  pl.BlockSpec((tc, SSM_W), row), pl.BlockSpec((tc, N_STATE), row), pl.BlockSpec((tc, N_STATE), row),
                  pl.BlockSpec((SUBLANES, N_STATE), prev), pl.BlockSpec((SUBLANES, N_STATE), prev)]
        + [_full(c.shape) for c in consts],
        out_specs=[pl.BlockSpec((tc, SSM_W), row), pl.BlockSpec((tc, SSM_W), row), _full((SSM_W, SSM_W))]
        + [_full(a.shape) for a in acc_shapes],
        scratch_shapes=[pltpu.VMEM((tc, N_STATE), F32), pltpu.VMEM((tc, N_STATE), F32),
                        pltpu.VMEM((SUBLANES, N_STATE), F32), pltpu.VMEM((SUBLANES, N_STATE), F32),
                        pltpu.VMEM((SSM_W, SSM_W), F32), pltpu.VMEM((4, tc, LANES), F32),
                        pltpu.VMEM((4, tc, LANES), F32),
                        pltpu.VMEM((SUBLANES, N_STATE), F32), pltpu.VMEM((SUBLANES, N_STATE), F32)],
        compiler_params=_params(2, VMEM_LIMIT_V7X),
    )(z, z, dmix, y, x_re, x_im, x_re, x_im, *consts)


def _dz_and_dx(x2, z, dqh, dkh, dvb, dga, du, dgs, dh1, w_in_g, g_mix, gq_t, gk_t, ones_bd, fold):
    t_tok = x2.shape[0]
    tm = min(256, t_tok)
    nt = t_tok // tm
    a_w = ATTN_W

    def head_norm_bwd(raw, d_hat, gain, scale, ones):
        r = lax.rsqrt(_hdot(raw * raw, ones) * (1.0 / HEAD_DIM) + EPS)
        n = raw * r
        a = d_hat * gain * scale
        d_raw = r * (a - n * (_hdot(a * n, ones) * (1.0 / HEAD_DIM)))
        return d_raw, jnp.sum(d_hat * n * scale, axis=0, keepdims=True)

    def body(x_ref, q_ref, k_ref, dq_ref, dk_ref, dv_ref, dga_ref, du_ref, dgs_ref, dh1_ref, w_ref, g_ref,
             gq_ref, gk_ref, ones_ref, fold_ref, dz_ref, gx_ref, dgm_ref, dgq_ref, dgk_ref, acc_q, acc_k):
        i = pl.program_id(0)

        @pl.when(i == 0)
        def _():
            dgm_ref[...] = jnp.zeros_like(dgm_ref)
            acc_q[...] = jnp.zeros_like(acc_q)
            acc_k[...] = jnp.zeros_like(acc_k)

        ones = ones_ref[...]
        dq, sq = head_norm_bwd(q_ref[...], dq_ref[...], gq_ref[...], HEAD_DIM ** -0.5, ones)
        dk, sk = head_norm_bwd(k_ref[...], dk_ref[...], gk_ref[...], 1.0, ones)
        acc_q[...] += jnp.broadcast_to(sq, acc_q.shape)
        acc_k[...] += jnp.broadcast_to(sk, acc_k.shape)
        parts = (dq, dk, dv_ref[...], dga_ref[...], du_ref[...], dgs_ref[...])
        for n, part in enumerate(parts):
            dz_ref[:, n * a_w:(n + 1) * a_w] = part.astype(BF16)
        dxn = jnp.zeros((tm, D_MODEL), F32)
        for j in range(N_DEV):
            dxn = dxn + _dot_nt(dz_ref[:, j * COL_W:(j + 1) * COL_W], w_ref[j])
        x = x_ref[...]
        r1 = lax.rsqrt(jnp.mean(x * x, axis=-1, keepdims=True) + EPS)
        xnorm = x * r1
        dgm_ref[...] += jnp.sum(dxn * xnorm, axis=0, keepdims=True)
        a = dxn * g_ref[...]
        gx_ref[...] = dh1_ref[...] + r1 * (a - xnorm * jnp.mean(a * xnorm, axis=-1, keepdims=True))

        @pl.when(i == nt - 1)
        def _():
            dgq_ref[...] = _hdot(acc_q[...], fold_ref[...])
            dgk_ref[...] = _hdot(acc_k[...], fold_ref[...])

    row = lambda i: (i, 0)
    col = lambda n: (lambda i: (i, n))
    s = jax.ShapeDtypeStruct
    half = pl.BlockSpec((tm, a_w), row)
    return pl.pallas_call(
        body, name="dz_dx", grid=(nt,),
        out_shape=[s((t_tok, IN_W), BF16), s((t_tok, D_MODEL), F32), s((1, D_MODEL), F32),
                   s((SUBLANES, HEAD_DIM), F32), s((SUBLANES, HEAD_DIM), F32)],
        in_specs=[pl.BlockSpec((tm, D_MODEL), row), pl.BlockSpec((tm, a_w), col(0)), pl.BlockSpec((tm, a_w), col(1)),
                  half, half, half, half, half, half, pl.BlockSpec((tm, D_MODEL), row),
                  _full(w_in_g.shape), _full(g_mix.shape), _full(gq_t.shape), _full(gk_t.shape),
                  _full(ones_bd.shape), _full(fold.shape)],
        out_specs=[pl.BlockSpec((tm, IN_W), row), pl.BlockSpec((tm, D_MODEL), row), _full((1, D_MODEL)),
                   _full((SUBLANES, HEAD_DIM)), _full((SUBLANES, HEAD_DIM))],
        scratch_shapes=[pltpu.VMEM((SUBLANES, a_w), F32), pltpu.VMEM((SUBLANES, a_w), F32)],
        compiler_params=_params(1, VMEM_LIMIT_V7X),
    )(x2, z, z, dqh, dkh, dvb, dga, du, dgs, dh1, w_in_g, g_mix, gq_t, gk_t, ones_bd, fold)


def _dw_in(xn, dz):
    t_tok = xn.shape[0]
    tk = min(1024, t_tok)
    nk = t_tok // tk

    def body(xn_ref, dz_ref, out_ref, acc):
        k = pl.program_id(1)

        @pl.when(k == 0)
        def _():
            acc[...] = jnp.zeros_like(acc)

        acc[...] += _dot_tn(xn_ref[...], dz_ref[...])

        @pl.when(k == nk - 1)
        def _():
            out_ref[0] = acc[...].astype(BF16)

    return pl.pallas_call(
        body, name="dw_in", grid=(N_DEV, nk),
        out_shape=jax.ShapeDtypeStruct((N_DEV, D_MODEL, COL_W), BF16),
        in_specs=[pl.BlockSpec((tk, D_MODEL), lambda j, k: (k, 0)), pl.BlockSpec((tk, COL_W), lambda j, k: (k, j))],
        out_specs=pl.BlockSpec((1, D_MODEL, COL_W), lambda j, k: (j, 0, 0)),
        scratch_shapes=[pltpu.VMEM((D_MODEL, COL_W), F32)],
        compiler_params=_params(2, VMEM_LIMIT_V7X),
    )(xn, dz)


SMALL = ("mix_norm", "q_norm", "k_norm", "lambda_re", "lambda_im", "log_dt", "b_re", "b_im", "c_re", "c_im",
         "d_skip", "b_glu", "ple_norm")
BIG = ("w_in", "w_glu", "w_out", "w_ple_gate", "w_ple_proj")
WEIGHTS = ("mix_norm", "w_in", "q_norm", "k_norm", "lambda_re", "lambda_im", "log_dt", "b_re", "b_im", "c_re",
           "c_im", "d_skip", "w_glu", "b_glu", "w_out", "ple_norm", "w_ple_gate", "w_ple_proj")


def _pack(arrs):
    flat = jnp.concatenate([a.reshape(-1).astype(F32) for a in arrs])
    rows = -(-flat.shape[0] // (64 * LANES)) * 64
    return jnp.pad(flat, (0, rows * LANES - flat.shape[0])).reshape(rows, LANES)


def _unpack(packed, shapes):
    flat = packed.reshape(-1)
    out, off = [], 0
    for shp in shapes:
        size = math.prod(shp)
        out.append(flat[off:off + size].reshape(shp))
        off += size
    return out


def kernel(x, p, mix_norm, w_in, q_norm, k_norm, lambda_re, lambda_im, log_dt, b_re, b_im, c_re, c_im, d_skip, w_glu, b_glu, w_out, ple_norm, w_ple_gate, w_ple_proj, loss_target, m_mix_norm, m_w_in, m_q_norm, m_k_norm, m_lambda_re, m_lambda_im, m_log_dt, m_b_re, m_b_im, m_c_re, m_c_im, m_d_skip, m_w_glu, m_b_glu, m_w_out, m_ple_norm, m_w_ple_gate, m_w_ple_proj, v_mix_norm, v_w_in, v_q_norm, v_k_norm, v_lambda_re, v_lambda_im, v_log_dt, v_b_re, v_b_im, v_c_re, v_c_im, v_d_skip, v_w_glu, v_b_glu, v_w_out, v_ple_norm, v_w_ple_gate, v_w_ple_proj):
    env = dict(locals())
    w = {n: env[n] for n in WEIGHTS}
    m = {n: env["m_" + n] for n in WEIGHTS}
    v = {n: env["v_" + n] for n in WEIGHTS}
    nb, seq, _ = x.shape
    t_tok = nb * seq
    x2 = x.reshape(t_tok, D_MODEL)
    tg2 = loss_target.reshape(t_tok, D_MODEL)
    p2 = p.reshape(t_tok, PLE_DIM)

    shard2d = {"w_in": (D_MODEL, COL_W), "w_glu": (SSM_W // N_DEV, SSM_W), "w_out": (D_MODEL // N_DEV, D_MODEL),
               "w_ple_gate": (D_MODEL // N_DEV, D_MODEL), "w_ple_proj": (PLE_DIM, D_MODEL // N_DEV)}
    w_sh = [w[n].reshape(shard2d[n]) for n in BIG]
    w_in_g, w_glu_g, w_out_g, w_g_g, w_p_g = _all_gather(w_sh, [BF16] * len(BIG), "gather_weights")
    w_glu_f = w_glu_g.reshape(SSM_W, SSM_W)
    w_out_f = w_out_g.reshape(D_MODEL, D_MODEL)
    w_g_f = w_g_g.reshape(D_MODEL, D_MODEL)

    g3 = (SSM_GROUPS, 1, SSM_STATE)
    lr3, li3 = lambda_re.reshape(g3), lambda_im.reshape(g3)
    dt3 = log_dt.reshape(SSM_GROUPS, 1, 1)
    btr = b_re[0].transpose(0, 2, 1)
    bti = b_im[0].transpose(0, 2, 1)
    a_re3, a_im3, bbr, bbi = _zoh_fwd(lr3, li3, dt3, btr, bti)
    a_re, a_im = a_re3.reshape(1, N_STATE), a_im3.reshape(1, N_STATE)
    bb_re, bb_im = _blockdiag(bbr).astype(BF16), _blockdiag(bbi).astype(BF16)
    cc_re = _blockdiag(c_re[0].transpose(0, 2, 1)).astype(BF16)
    cc_im = _blockdiag(c_im[0].transpose(0, 2, 1)).astype(BF16)

    ones_bd = _head_ones()
    fold = jnp.tile(jnp.eye(HEAD_DIM, dtype=F32), (ATTN_W // HEAD_DIM, 1))
    gq_t = jnp.tile(q_norm, (1, ATTN_W // HEAD_DIM))
    gk_t = jnp.tile(k_norm, (1, ATTN_W // HEAD_DIM))

    z, qh, kh, vb, xn = _in_proj(x2, mix_norm, w_in_g, ones_bd, gq_t, gk_t)
    o, lse, ag = _attn_fwd(qh, kh, vb, z, nb, seq)
    x_re, x_im, y, sg = _ssm_fwd(z, a_re, a_im, bb_re, bb_im, cc_re, cc_im, d_skip, w_glu_f, b_glu, nb, seq)
    dmix, dh1, loss_t, d_ple, dw_out, dw_g, dw_p = _tail(x2, tg2, ag, sg, p2, w_out_f, w_g_f, w_p_g, ple_norm)

    early_parts = [dw_out.reshape(N_DEV, D_MODEL // N_DEV, D_MODEL), dw_g.reshape(N_DEV, D_MODEL // N_DEV, D_MODEL),
                   dw_p]
    (dqh, dkh, dvb, dga), (g_out, g_g, g_p) = _attn_bwd(qh, kh, vb, z, o, lse, dmix, nb, seq, early_parts)
    (du, dgs, dw_glu, d_bglu, d_dskip, da_re, da_im, dbb_re, dbb_im, dcc_re, dcc_im) = _ssm_bwd(
        z, dmix, y, x_re, x_im, a_re, a_im, bb_re, bb_im, cc_re, cc_im, d_skip, w_glu_f, b_glu, nb, seq)
    dz, gx, d_mix, d_gq, d_gk = _dz_and_dx(x2, z, dqh, dkh, dvb, dga, du, dgs, dh1, w_in_g, mix_norm, gq_t, gk_t,
                                           ones_bd, fold)
    dw_in = _dw_in(xn, dz)

    d_lr, d_li, d_dt, d_btr, d_bti = _zoh_bwd(
        lr3, li3, dt3, btr, bti, da_re.reshape(g3), da_im.reshape(g3),
        _blockdiag_extract(dbb_re, SSM_GROUP, SSM_STATE), _blockdiag_extract(dbb_im, SSM_GROUP, SSM_STATE))
    small_g = {
        "mix_norm": d_mix, "q_norm": d_gq[0:1], "k_norm": d_gk[0:1], "lambda_re": d_lr, "lambda_im": d_li,
        "log_dt": d_dt, "b_re": d_btr.transpose(0, 2, 1), "b_im": d_bti.transpose(0, 2, 1),
        "c_re": _blockdiag_extract(dcc_re, SSM_STATE, SSM_GROUP).transpose(0, 2, 1),
        "c_im": _blockdiag_extract(dcc_im, SSM_STATE, SSM_GROUP).transpose(0, 2, 1),
        "d_skip": d_dskip, "b_glu": d_bglu, "ple_norm": d_ple}

    g_in, g_glu = _reduce_scatter([dw_in, dw_glu.reshape(N_DEV, SSM_W // N_DEV, SSM_W)], "scatter_grads")
    g_sh = [g_in, g_glu, g_out, g_g, g_p]
    d_sh, m_sh, v_sh = _adamw_shards(g_sh, w_sh, [m[n].reshape(shard2d[n]) for n in BIG],
                                     [v[n].reshape(shard2d[n]) for n in BIG])

    (gathered,) = _all_gather([_pack([small_g[n] for n in SMALL] + [loss_t[0:1, 0:1]])], [F32], "gather_small_grads")
    g_pk, d_pk, m_pk, v_pk = _small_update(gathered, _pack([w[n] for n in SMALL]), _pack([m[n] for n in SMALL]),
                                           _pack([v[n] for n in SMALL]))

    grads, deltas, new_m, new_v = {}, {}, {}, {}
    small_shapes = [w[n].shape for n in SMALL]
    for dst, packed in ((grads, g_pk), (deltas, d_pk), (new_m, m_pk), (new_v, v_pk)):
        for n, a in zip(SMALL, _unpack(packed, small_shapes)):
            dst[n] = a
    for i, n in enumerate(BIG):
        grads[n] = g_sh[i].reshape(w[n].shape)
        deltas[n] = d_sh[i].reshape(w[n].shape)
        new_m[n] = m_sh[i].reshape(w[n].shape)
        new_v[n] = v_sh[i].reshape(w[n].shape)

    loss = _unpack(g_pk, small_shapes + [()])[-1]
    return (loss, gx.reshape(x.shape), *[grads[n] for n in WEIGHTS], *[deltas[n] for n in WEIGHTS],
            *[new_m[n] for n in WEIGHTS], *[new_v[n] for n in WEIGHTS])
```

```python
import math

import jax
import jax.numpy as jnp
from jax import lax
from jax.experimental import pallas as pl
from jax.experimental.pallas import tpu as pltpu

F32 = jnp.float32
BF16 = jnp.bfloat16
MESH = pl.DeviceIdType.MESH
AXES = ("x", "y", "c")
N_DEV = 8

D_MODEL = 1024
HEAD_DIM = 64
ATTN_W = 512
SSM_W = 512
SSM_GROUPS = 32
SSM_GROUP = 16
SSM_STATE = 64
N_STATE = SSM_GROUPS * SSM_STATE
PLE_DIM = 256
IN_W = 3072
COL_W = IN_W // N_DEV
DILATED = ((128, 1), (512, 4), (2048, 16))
EPS = 1e-6
INV_SQRT2 = 1.0 / math.sqrt(2.0)
INV_SQRT_2PI = 1.0 / math.sqrt(2.0 * math.pi)

ADAM_LR, ADAM_B1, ADAM_B2, ADAM_EPS, ADAM_WD, ADAM_STEP = 0.001, 0.9, 0.999, 1e-08, 0.01, 10

VMEM_LIMIT_V7X = 56 * 1024 * 1024
SUBLANES = 8
LANES = 128


def _params(n_axes=None, vmem=None):
    kw = {}
    if n_axes:
        kw["dimension_semantics"] = ("arbitrary",) * n_axes
    if vmem:
        kw["vmem_limit_bytes"] = vmem
    return pltpu.CompilerParams(**kw)


def _dot(a, b):
    return jnp.dot(a, b, preferred_element_type=F32)


def _dot_nt(a, b):
    return lax.dot_general(a, b, (((1,), (1,)), ((), ())), preferred_element_type=F32)


def _dot_tn(a, b):
    return lax.dot_general(a, b, (((0,), (0,)), ((), ())), preferred_element_type=F32)


def _hdot(a, b):
    return jnp.dot(a, b, precision=lax.Precision.HIGHEST, preferred_element_type=F32)


def _sig(x):
    return 1.0 / (1.0 + jnp.exp(-x))


def _gelu_and_grad(y):
    cdf = 0.5 * (1.0 + lax.erf(y * INV_SQRT2))
    pdf = jnp.exp(-0.5 * y * y) * INV_SQRT_2PI
    return y * cdf, cdf + y * pdf


def _vmem():
    return pl.BlockSpec(memory_space=pltpu.VMEM)


def _full(shape):
    nd = len(shape)
    return pl.BlockSpec(shape, lambda *_: (0,) * nd)


class _AllGather:
    def __init__(self, n, cast):
        self.n, self.cast = n, cast

    def scratch(self):
        n = self.n
        return [pltpu.SemaphoreType.DMA((7 * n,)), pltpu.SemaphoreType.DMA((7 * n,)), pltpu.SemaphoreType.DMA((n,))]

    def _plan(self, src_refs, out_refs, sems):
        send_sems, recv_sems, own_sems = sems
        x, y, c = lax.axis_index("x"), lax.axis_index("y"), lax.axis_index("c")
        me, sibling = (x, y, c), (x, y, 1 - c)
        chips = [(1 - x, y), (x, 1 - y), (1 - x, 1 - y)]

        def idx(px, py, pc):
            return 4 * px + 2 * py + pc

        def copy(i, k, block, to, own_src=False):
            ref = out_refs[i].at[idx(*block)]
            return pltpu.make_async_remote_copy(
                src_ref=src_refs[i] if own_src and not self.cast else ref, dst_ref=ref,
                send_sem=send_sems.at[7 * i + k], recv_sem=recv_sems.at[7 * i + k],
                device_id=to, device_id_type=MESH)

        first, passed, arrive_ici, arrive_d2d, own = [], [], [], [], []
        for i in range(self.n):
            first.append(copy(i, 0, me, sibling, own_src=True))
            first += [copy(i, 1 + j, me, (*chip, c), own_src=True) for j, chip in enumerate(chips)]
            arrive_ici += [copy(i, 1 + j, (*chip, c), me) for j, chip in enumerate(chips)]
            passed += [copy(i, 4 + j, (*chip, c), sibling) for j, chip in enumerate(chips)]
            arrive_d2d.append(copy(i, 0, sibling, me))
            arrive_d2d += [copy(i, 4 + j, (*chip, 1 - c), me) for j, chip in enumerate(chips)]
            if not self.cast:
                own.append(pltpu.make_async_copy(src_refs[i], out_refs[i].at[idx(*me)], own_sems.at[i]))
        return idx(*me), first, passed, arrive_ici, arrive_d2d, own

    def start(self, src_refs, out_refs, sems):
        my, first, _, _, _, own = self._plan(src_refs, out_refs, sems)
        if self.cast:
            for i in range(self.n):
                out_refs[i][my] = src_refs[i][...].astype(out_refs[i].dtype)
        for cp in own + first:
            cp.start()

    def forward(self, src_refs, out_refs, sems):
        _, _, passed, arrive_ici, _, _ = self._plan(src_refs, out_refs, sems)
        for cp in arrive_ici:
            cp.wait_recv()
        for cp in passed:
            cp.start()

    def finish(self, src_refs, out_refs, sems):
        _, first, passed, _, arrive_d2d, own = self._plan(src_refs, out_refs, sems)
        for cp in own:
            cp.wait()
        for cp in arrive_d2d:
            cp.wait_recv()
        for cp in first + passed:
            cp.wait_send()


def _all_gather(shards, out_dtypes, name):
    n = len(shards)
    ag = _AllGather(n, cast=True)

    def body(*refs):
        in_refs, out_refs, sems = refs[:n], refs[n:2 * n], refs[2 * n:]
        ag.start(in_refs, out_refs, sems)
        ag.forward(in_refs, out_refs, sems)
        ag.finish(in_refs, out_refs, sems)

    return pl.pallas_call(
        body, name=name,
        out_shape=[jax.ShapeDtypeStruct((N_DEV,) + s.shape, dt) for s, dt in zip(shards, out_dtypes)],
        in_specs=[_vmem()] * n, out_specs=[_vmem()] * n,
        scratch_shapes=ag.scratch(),
        compiler_params=_params(vmem=VMEM_LIMIT_V7X),
    )(*shards)


def _row_chunks(rows):
    chunk = 64 if rows % 64 == 0 else rows
    return chunk, rows // chunk


class _ReduceScatter:
    def __init__(self, shapes):
        self.shapes = shapes
        self.n = len(shapes)

    def scratch(self, dtype):
        return ([pltpu.VMEM(s, dtype) for s in self.shapes]
                + [pltpu.SemaphoreType.DMA((7 * self.n,)), pltpu.SemaphoreType.DMA((7 * self.n,)),
                   pltpu.SemaphoreType.DMA((self.n,))])

    def _copies(self, in_refs, land_refs, send_sems, recv_sems, own_sems):
        x, y, c = lax.axis_index("x"), lax.axis_index("y"), lax.axis_index("c")
        remote, own = [], []
        for i in range(self.n):
            for m in range(1, N_DEV):
                px = 1 - x if m & 4 else x
                py = 1 - y if m & 2 else y
                pc = 1 - c if m & 1 else c
                remote.append(pltpu.make_async_remote_copy(
                    src_ref=in_refs[i].at[4 * px + 2 * py + pc], dst_ref=land_refs[i].at[m - 1],
                    send_sem=send_sems.at[7 * i + m - 1], recv_sem=recv_sems.at[7 * i + m - 1],
                    device_id=(px, py, pc), device_id_type=MESH))
            own.append(pltpu.make_async_copy(in_refs[i].at[4 * x + 2 * y + c], land_refs[i].at[N_DEV - 1],
                                             own_sems.at[i]))
        return remote, own

    def start(self, in_refs, scratch):
        remote, own = self._copies(in_refs, scratch[:self.n], *scratch[self.n:])
        for cp in remote + own:
            cp.start()

    def finish(self, in_refs, scratch, out_refs):
        land_refs = scratch[:self.n]
        remote, own = self._copies(in_refs, land_refs, *scratch[self.n:])
        for cp in own:
            cp.wait()
        for cp in remote:
            cp.wait_recv()
        for i in range(self.n):
            chunk, steps = _row_chunks(self.shapes[i][1])

            def step(s, carry, i=i, chunk=chunk):
                r = pl.ds(pl.multiple_of(s * chunk, chunk), chunk)
                acc = land_refs[i][N_DEV - 1, r, :].astype(F32)
                for m in range(1, N_DEV):
                    acc = acc + land_refs[i][m - 1, r, :].astype(F32)
                out_refs[i][r, :] = acc
                return carry

            lax.fori_loop(0, steps, step, 0)
        for cp in remote:
            cp.wait_send()


def _reduce_scatter(parts, name):
    n = len(parts)
    rs = _ReduceScatter([p.shape for p in parts])

    def body(*refs):
        in_refs, out_refs, scratch = refs[:n], refs[n:2 * n], refs[2 * n:]
        rs.start(in_refs, scratch)
        rs.finish(in_refs, scratch, out_refs)

    return pl.pallas_call(
        body, name=name,
        out_shape=[jax.ShapeDtypeStruct(p.shape[1:], F32) for p in parts],
        in_specs=[_vmem()] * n, out_specs=[_vmem()] * n,
        scratch_shapes=rs.scratch(parts[0].dtype),
        compiler_params=_params(vmem=VMEM_LIMIT_V7X),
    )(*parts)


def _adamw_math(w, g, m, v):
    m = ADAM_B1 * m + (1.0 - ADAM_B1) * g
    v = ADAM_B2 * v + (1.0 - ADAM_B2) * (g * g)
    m_hat = m / (1.0 - ADAM_B1 ** ADAM_STEP)
    v_hat = v / (1.0 - ADAM_B2 ** ADAM_STEP)
    delta = -ADAM_LR * (m_hat / (jnp.sqrt(v_hat) + ADAM_EPS) + ADAM_WD * w)
    return delta, m, v


def _adamw_shards(gs, ws, ms, vs):
    n = len(gs)

    def body(*refs):
        g_refs, w_refs, m_refs, v_refs = (refs[k * n:(k + 1) * n] for k in range(4))
        d_out, m_out, v_out = (refs[(4 + k) * n:(5 + k) * n] for k in range(3))
        for i in range(n):
            chunk, steps = _row_chunks(gs[i].shape[0])

            def step(s, carry, i=i, chunk=chunk):
                r = pl.ds(pl.multiple_of(s * chunk, chunk), chunk)
                d, m, v = _adamw_math(w_refs[i][r, :], g_refs[i][r, :], m_refs[i][r, :], v_refs[i][r, :])
                d_out[i][r, :] = d
                m_out[i][r, :] = m
                v_out[i][r, :] = v
                return carry

            lax.fori_loop(0, steps, step, 0)

    shapes = [jax.ShapeDtypeStruct(g.shape, F32) for g in gs]
    outs = pl.pallas_call(
        body, name="adamw_shards", out_shape=shapes * 3,
        in_specs=[_vmem()] * (4 * n), out_specs=[_vmem()] * (3 * n),
        compiler_params=_params(vmem=VMEM_LIMIT_V7X),
    )(*gs, *ws, *ms, *vs)
    return outs[:n], outs[n:2 * n], outs[2 * n:]


def _small_update(gathered, w, m, v):
    rows = w.shape[0]
    chunk, steps = _row_chunks(rows)

    def body(ga_ref, w_ref, m_ref, v_ref, g_out, d_out, m_out, v_out):
        def step(s, carry):
            r = pl.ds(pl.multiple_of(s * chunk, chunk), chunk)
            g = ga_ref[0, r, :]
            for j in range(1, N_DEV):
                g = g + ga_ref[j, r, :]
            d, mm, vv = _adamw_math(w_ref[r, :], g, m_ref[r, :], v_ref[r, :])
            g_out[r, :] = g
            d_out[r, :] = d
            m_out[r, :] = mm
            v_out[r, :] = vv
            return carry

        lax.fori_loop(0, steps, step, 0)

    return pl.pallas_call(
        body, name="small_update", out_shape=[jax.ShapeDtypeStruct(w.shape, F32)] * 4,
        in_specs=[_vmem()] * 4, out_specs=[_vmem()] * 4,
    )(gathered, w, m, v)


def _zoh(lr, li, logdt, btr, bti):
    dt = jnp.exp(logdt)
    mag = jnp.exp(lr * dt)
    th = li * dt
    ar = mag * jnp.cos(th)
    ai = mag * jnp.sin(th)
    den = lr * lr + li * li
    nr = ar - 1.0
    cr = (nr * lr + ai * li) / den
    ci = (ai * lr - nr * li) / den
    return ar, ai, cr * btr - ci * bti, cr * bti + ci * btr


def _zoh_fwd(lr, li, logdt, btr, bti):
    def body(lr_ref, li_ref, dt_ref, br_ref, bi_ref, ar_ref, ai_ref, bbr_ref, bbi_ref):
        ar, ai, bbr, bbi = _zoh(lr_ref[...], li_ref[...], dt_ref[...], br_ref[...], bi_ref[...])
        ar_ref[...] = ar
        ai_ref[...] = ai
        bbr_ref[...] = bbr
        bbi_ref[...] = bbi

    s = jax.ShapeDtypeStruct
    return pl.pallas_call(
        body, name="zoh_fwd",
        out_shape=[s(lr.shape, F32), s(lr.shape, F32), s(btr.shape, F32), s(btr.shape, F32)],
        in_specs=[_vmem()] * 5, out_specs=[_vmem()] * 4,
    )(lr, li, logdt, btr, bti)


def _zoh_bwd(lr, li, logdt, btr, bti, dar, dai, dbbr, dbbi):
    def body(lr_ref, li_ref, dt_ref, br_ref, bi_ref, dar_ref, dai_ref, dbbr_ref, dbbi_ref,
             glr_ref, gli_ref, gdt_ref, gbr_ref, gbi_ref):
        _, vjp = jax.vjp(_zoh, lr_ref[...], li_ref[...], dt_ref[...], br_ref[...], bi_ref[...])
        glr, gli, gdt, gbr, gbi = vjp((dar_ref[...], dai_ref[...], dbbr_ref[...], dbbi_ref[...]))
        glr_ref[...] = glr
        gli_ref[...] = gli
        gdt_ref[...] = gdt
        gbr_ref[...] = gbr
        gbi_ref[...] = gbi

    s = jax.ShapeDtypeStruct
    return pl.pallas_call(
        body, name="zoh_bwd",
        out_shape=[s(lr.shape, F32), s(lr.shape, F32), s(logdt.shape, F32), s(btr.shape, F32), s(btr.shape, F32)],
        in_specs=[_vmem()] * 9, out_specs=[_vmem()] * 5,
    )(lr, li, logdt, btr, bti, dar, dai, dbbr, dbbi)


def _blockdiag(t):
    g, r, s = t.shape
    t = t.reshape(4, 8, r, s)
    out = jnp.einsum("jirs,ik->jirks", t, jnp.eye(8, dtype=t.dtype))
    return out.reshape(4, 8 * r, 8 * s)


def _blockdiag_extract(m, r, s):
    m = m.reshape(4, 8, r, 8, s)
    out = jnp.einsum("jirks,ik->jirs", m, jnp.eye(8, dtype=m.dtype))
    return out.reshape(32, r, s)


def _head_ones():
    r = jnp.arange(ATTN_W) // HEAD_DIM
    return (r[:, None] == r[None, :]).astype(F32)


def _in_proj(x2, g_mix, w_in_g, ones_bd, gq_t, gk_t):
    t_tok = x2.shape[0]
    tm = min(512, t_tok)

    def body(x_ref, g_ref, w_ref, ones_ref, gq_ref, gk_ref, z_ref, qh_ref, kh_ref, vb_ref, xn_ref):
        x = x_ref[...]
        r = lax.rsqrt(jnp.mean(x * x, axis=-1, keepdims=True) + EPS)
        xn = (x * r * g_ref[...]).astype(BF16)
        xn_ref[...] = xn
        for j in range(N_DEV):
            z_ref[:, j * COL_W:(j + 1) * COL_W] = _dot(xn, w_ref[j])
        ones = ones_ref[...]
        q = z_ref[:, 0:ATTN_W]
        rq = lax.rsqrt(_hdot(q * q, ones) * (1.0 / HEAD_DIM) + EPS)
        qh_ref[...] = (q * rq * gq_ref[...] * (HEAD_DIM ** -0.5)).astype(BF16)
        k = z_ref[:, ATTN_W:2 * ATTN_W]
        rk = lax.rsqrt(_hdot(k * k, ones) * (1.0 / HEAD_DIM) + EPS)
        kh_ref[...] = (k * rk * gk_ref[...]).astype(BF16)
        vb_ref[...] = z_ref[:, 2 * ATTN_W:3 * ATTN_W].astype(BF16)

    row = lambda i: (i, 0)
    s = jax.ShapeDtypeStruct
    return pl.pallas_call(
        body, name="in_proj", grid=(t_tok // tm,),
        out_shape=[s((t_tok, IN_W), F32), s((t_tok, ATTN_W), BF16), s((t_tok, ATTN_W), BF16),
                   s((t_tok, ATTN_W), BF16), s((t_tok, D_MODEL), BF16)],
        in_specs=[pl.BlockSpec((tm, D_MODEL), row), _full(g_mix.shape), _full(w_in_g.shape), _full(ones_bd.shape),
                  _full(gq_t.shape), _full(gk_t.shape)],
        out_specs=[pl.BlockSpec((tm, IN_W), row), pl.BlockSpec((tm, ATTN_W), row), pl.BlockSpec((tm, ATTN_W), row),
                   pl.BlockSpec((tm, ATTN_W), row), pl.BlockSpec((tm, D_MODEL), row)],
        compiler_params=_params(1, VMEM_LIMIT_V7X),
    )(x2, g_mix, w_in_g, ones_bd, gq_t, gk_t)


TQ = 128
NEG = -1e30


def _head_col(t, lm):
    return jnp.max(jnp.where(lm, t, NEG), axis=-1, keepdims=True)


def _head_masks():
    lane = lax.broadcasted_iota(jnp.int32, (1, 1, LANES), 2)
    return [(lane // HEAD_DIM) == h for h in range(LANES // HEAD_DIM)]


def _gather_classes(ref, dil, nt, tq, dtype):
    length = nt * tq
    if dil == 1:
        return ref[...].astype(dtype).reshape(nt, tq, LANES)
    parts = [ref[pl.ds(r, length, stride=dil), :].astype(dtype).reshape(nt, tq, LANES) for r in range(dil)]
    return jnp.concatenate(parts, axis=0)


def _scatter_classes(ref, val, dil, nt, tq, add):
    length = nt * tq
    for r in range(dil):
        rows = pl.ds(r, length, stride=dil) if dil > 1 else slice(None)
        part = val[r * nt:(r + 1) * nt].reshape(length, LANES)
        ref[rows, :] = ref[rows, :] + part if add else part


def _with_prev_tile(t3, dil, nt):
    parts = []
    for r in range(dil):
        t = t3[r * nt:(r + 1) * nt]
        parts.append(jnp.concatenate([t[:1], t[:-1]], axis=0))
    prev = parts[0] if dil == 1 else jnp.concatenate(parts, axis=0)
    return jnp.concatenate([prev, t3], axis=1)


def _band_valid(dil, nt, tq):
    if nt == 1:
        shape = (dil, tq, tq)
        return lax.broadcasted_iota(jnp.int32, shape, 1) >= lax.broadcasted_iota(jnp.int32, shape, 2)
    shape = (dil * nt, tq, 2 * tq)
    b = lax.broadcasted_iota(jnp.int32, shape, 0)
    c = lax.broadcasted_iota(jnp.int32, shape, 2)
    d = tq + lax.broadcasted_iota(jnp.int32, shape, 1) - c
    return (d >= 0) & (d <= tq) & (((b & (nt - 1)) != 0) | (c >= tq))


def _window_tiling(seq, window, dil):
    length = seq // dil
    tq = min(TQ, length)
    nt = length // tq
    assert length % tq == 0 and nt & (nt - 1) == 0 and (nt == 1 or window == tq * dil)
    return nt, tq


def _bqk(a, b):
    return jnp.einsum("bqd,bkd->bqk", a, b, preferred_element_type=F32)


def _bqd(a, b):
    return jnp.einsum("bqk,bkd->bqd", a, b, preferred_element_type=F32)


def _bkd(a, b):
    return jnp.einsum("bqk,bqd->bkd", a, b, preferred_element_type=F32)


def _attn_fwd(qh, kh, vb, z, nb, seq):
    t_tok = nb * seq
    n_win = len(DILATED)

    def body(q_ref, k_ref, v_ref, ga_ref, o_ref, l_ref, ag_ref, qf, kf, vf, oc, lc):
        qf[...] = q_ref[...].astype(F32)
        kf[...] = k_ref[...].astype(F32)
        vf[...] = v_ref[...].astype(F32)
        lms = _head_masks()
        for w, (window, dil) in enumerate(DILATED):
            nt, tq = _window_tiling(seq, window, dil)
            q3 = _gather_classes(qf, dil, nt, tq, BF16)
            k3 = _gather_classes(kf, dil, nt, tq, BF16)
            v3 = _gather_classes(vf, dil, nt, tq, BF16)
            if nt > 1:
                k3, v3 = _with_prev_tile(k3, dil, nt), _with_prev_tile(v3, dil, nt)
            valid = _band_valid(dil, nt, tq)
            o = jnp.zeros(q3.shape, F32)
            lse = jnp.zeros(q3.shape, F32)
            for lm in lms:
                s = _bqk(jnp.where(lm, q3, jnp.zeros_like(q3)), k3)
                m = jnp.max(jnp.where(valid, s, NEG), axis=-1, keepdims=True)
                p = jnp.where(valid, jnp.exp(s - m), 0.0)
                den = jnp.sum(p, axis=-1, keepdims=True)
                o = jnp.where(lm, _bqd(p.astype(BF16), v3) / den, o)
                lse = jnp.where(lm, m + jnp.log(den), lse)
            _scatter_classes(oc.at[w], o, dil, nt, tq, add=False)
            _scatter_classes(lc.at[w], lse, dil, nt, tq, add=False)
        mx = lc[0]
        for w in range(1, n_win):
            mx = jnp.maximum(mx, lc[w])
        tot = jnp.zeros_like(mx)
        o = jnp.zeros_like(mx)
        for w in range(n_win):
            e = jnp.exp(lc[w] - mx)
            tot = tot + e
            o = o + e * oc[w]
        o = o / tot
        o_ref[...] = o
        l_ref[...] = mx + jnp.log(tot)
        ga = ga_ref[...]
        ag_ref[...] = (o * ga * _sig(ga)).astype(BF16)

    blk = pl.BlockSpec((seq, LANES), lambda b, hp: (b, hp))
    s = jax.ShapeDtypeStruct
    return pl.pallas_call(
        body, name="attn_fwd", grid=(nb, ATTN_W // LANES),
        out_shape=[s((t_tok, ATTN_W), F32), s((t_tok, ATTN_W), F32), s((t_tok, ATTN_W), BF16)],
        in_specs=[blk, blk, blk, pl.BlockSpec((seq, LANES), lambda b, hp: (b, 3 * ATTN_W // LANES + hp))],
        out_specs=[blk, blk, blk],
        scratch_shapes=[pltpu.VMEM((seq, LANES), F32)] * 3 + [pltpu.VMEM((n_win, seq, LANES), F32)] * 2,
        compiler_params=_params(2, VMEM_LIMIT_V7X),
    )(qh, kh, vb, z)


SCAN_COLS = 512


def _to_segments(dst_ref, val):
    seg = val.shape[0] // SUBLANES
    for n in range(dst_ref.shape[0]):
        for s in range(SUBLANES):
            dst_ref[n, pl.ds(s, seg, stride=SUBLANES), :] = val[s * seg:(s + 1) * seg, n * LANES:(n + 1) * LANES]


def _from_segments(src_ref):
    seg = src_ref.shape[1] // SUBLANES
    return jnp.concatenate(
        [jnp.concatenate([src_ref[n, pl.ds(s, seg, stride=SUBLANES), :] for s in range(SUBLANES)], axis=0)
         for n in range(src_ref.shape[0])], axis=1)


def _scan_chunk(re_ref, im_ref, a_re_ref, a_im_ref, carry_re, carry_im, rows, reverse, visit=None):
    seg = rows // SUBLANES
    assert seg & (seg - 1) == 0
    rowi = lax.broadcasted_iota(jnp.int32, (SUBLANES, SCAN_COLS), 0)
    edge = (SUBLANES - 1) if reverse else 0
    last = 0 if reverse else SUBLANES - 1
    at_edge = rowi == edge

    def cmul(ar, ai, br, bi):
        return ar * br - ai * bi, ar * bi + ai * br

    for c0 in range(0, N_STATE, SCAN_COLS):
        cols = slice(c0, c0 + SCAN_COLS)
        a1r = jnp.broadcast_to(a_re_ref[:, cols], (SUBLANES, SCAN_COLS))
        a1i = jnp.broadcast_to(a_im_ref[:, cols], (SUBLANES, SCAN_COLS))
        if reverse:
            a1i = -a1i

        def block_of(i):
            j = (seg - 1 - i) if reverse else i
            return j, pl.ds(pl.multiple_of(j * SUBLANES, SUBLANES), SUBLANES)

        def local(i, carry, cols=cols, a1r=a1r, a1i=a1i):
            xr, xi = carry
            _, blk = block_of(i)
            nr, ni = cmul(a1r, a1i, xr, xi)
            xr, xi = nr + re_ref[blk, cols], ni + im_ref[blk, cols]
            re_ref[blk, cols] = xr
            im_ref[blk, cols] = xi
            return xr, xi

        zero = jnp.zeros((SUBLANES, SCAN_COLS), F32)
        er, ei = lax.fori_loop(0, seg, local, (zero, zero))

        pr, pi = a1r, a1i
        for _ in range(seg.bit_length() - 1):
            pr, pi = cmul(pr, pi, pr, pi)
        cr, ci = carry_re[:, cols], carry_im[:, cols]
        inr, ini = cmul(pr, pi, cr, ci)
        er = er + jnp.where(at_edge, inr, 0.0)
        ei = ei + jnp.where(at_edge, ini, 0.0)
        for sft in (1, 2, 4):
            shift, keep = (SUBLANES - sft, rowi < SUBLANES - sft) if reverse else (sft, rowi >= sft)
            rs = jnp.where(keep, pltpu.roll(er, shift, 0), 0.0)
            ims = jnp.where(keep, pltpu.roll(ei, shift, 0), 0.0)
            dr, di = cmul(pr, pi, rs, ims)
            er, ei = er + dr, ei + di
            pr, pi = cmul(pr, pi, pr, pi)
        carry_re[:, cols] = jnp.broadcast_to(er[last:last + 1, :], (SUBLANES, SCAN_COLS))
        carry_im[:, cols] = jnp.broadcast_to(ei[last:last + 1, :], (SUBLANES, SCAN_COLS))
        one = (SUBLANES - 1) if reverse else 1
        kr = jnp.where(at_edge, cr, pltpu.roll(er, one, 0))
        ki = jnp.where(at_edge, ci, pltpu.roll(ei, one, 0))

        def fix(i, carry, cols=cols, a1r=a1r, a1i=a1i):
            kr, ki, acc = carry
            j, blk = block_of(i)
            kr, ki = cmul(a1r, a1i, kr, ki)
            xr, xi = re_ref[blk, cols] + kr, im_ref[blk, cols] + ki
            re_ref[blk, cols] = xr
            im_ref[blk, cols] = xi
            if visit is not None:
                acc = visit(cols, j, xr, xi, acc)
            return kr, ki, acc

        _, _, acc = lax.fori_loop(0, seg, fix, (kr, ki, (zero, zero)))
        if visit is not None:
            visit(cols, None, None, None, acc)


def _ssm_fwd(z, a_re, a_im, bb_re, bb_im, cc_re, cc_im, d_skip, w_glu, b_glu, nb, seq):
    t_tok = nb * seq
    tc = min(256, seq)
    nch = seq // tc
    grp = N_STATE // 4

    def body(u_ref, gs_ref, ar_ref, ai_ref, bbr_ref, bbi_ref, ccr_ref, cci_ref, d_ref, wg_ref, bg_ref,
             xr_ref, xi_ref, y_ref, sg_ref, car_re, car_im, seg_u, seg_y):
        @pl.when(pl.program_id(1) == 0)
        def _():
            car_re[...] = jnp.zeros_like(car_re)
            car_im[...] = jnp.zeros_like(car_im)

        u = u_ref[...]
        _to_segments(seg_u, u)
        for j in range(4):
            uj = seg_u[j].astype(BF16)
            xr_ref[:, j * grp:(j + 1) * grp] = _dot(uj, bbr_ref[j])
            xi_ref[:, j * grp:(j + 1) * grp] = _dot(uj, bbi_ref[j])
        _scan_chunk(xr_ref, xi_ref, ar_ref, ai_ref, car_re, car_im, tc, reverse=False)
        for j in range(4):
            xr = xr_ref[:, j * grp:(j + 1) * grp].astype(BF16)
            xi = xi_ref[:, j * grp:(j + 1) * grp].astype(BF16)
            seg_y[j] = _dot(xr, ccr_ref[j]) - _dot(xi, cci_ref[j])
        y = _from_segments(seg_y) + d_ref[...] * u
        y_ref[...] = y
        yg, _ = _gelu_and_grad(y)
        gl = _dot(yg.astype(BF16), wg_ref[...]) + bg_ref[...]
        gs = gs_ref[...]
        sg_ref[...] = (yg * _sig(gl) * gs * _sig(gs)).astype(BF16)

    umap = lambda b, ch: (b * nch + ch, 4)
    gmap = lambda b, ch: (b * nch + ch, 5)
    row = lambda b, ch: (b * nch + ch, 0)
    s = jax.ShapeDtypeStruct
    consts = [a_re, a_im, bb_re, bb_im, cc_re, cc_im, d_skip, w_glu, b_glu]
    return pl.pallas_call(
        body, name="ssm_fwd", grid=(nb, nch),
        out_shape=[s((t_tok, N_STATE), F32), s((t_tok, N_STATE), F32), s((t_tok, SSM_W), F32),
                   s((t_tok, SSM_W), BF16)],
        in_specs=[pl.BlockSpec((tc, SSM_W), umap), pl.BlockSpec((tc, SSM_W), gmap)] + [_full(c.shape) for c in consts],
        out_specs=[pl.BlockSpec((tc, N_STATE), row), pl.BlockSpec((tc, N_STATE), row),
                   pl.BlockSpec((tc, SSM_W), row), pl.BlockSpec((tc, SSM_W), row)],
        scratch_shapes=[pltpu.VMEM((SUBLANES, N_STATE), F32), pltpu.VMEM((SUBLANES, N_STATE), F32),
                        pltpu.VMEM((4, tc, LANES), F32), pltpu.VMEM((4, tc, LANES), F32)],
        compiler_params=_params(2, VMEM_LIMIT_V7X),
    )(z, z, *consts)


def _tail(x2, tg2, ag, sg, p2, w_out, w_g, w_p, g_ple):
    t_tok = x2.shape[0]
    tm = min(256, t_tok)
    nt = t_tok // tm
    half = ATTN_W

    def body(x_ref, tg_ref, ag_ref, sg_ref, p_ref, wo_ref, wg_ref, wp_ref, gp_ref,
             dmix_ref, dh1_ref, loss_ref, dgp_ref, dwo_ref, dwg_ref, dwp_ref, acc_o, acc_g, acc_p):
        i = pl.program_id(0)

        @pl.when(i == 0)
        def _():
            loss_ref[...] = jnp.zeros_like(loss_ref)
            dgp_ref[...] = jnp.zeros_like(dgp_ref)
            acc_o[...] = jnp.zeros_like(acc_o)
            acc_g[...] = jnp.zeros_like(acc_g)
            acc_p[...] = jnp.zeros_like(acc_p)

        ag_t, sg_t = ag_ref[...], sg_ref[...]
        h1 = x_ref[...] + _dot(ag_t, wo_ref[0:half, :]) + _dot(sg_t, wo_ref[half:2 * half, :])
        r2 = lax.rsqrt(jnp.mean(h1 * h1, axis=-1, keepdims=True) + EPS)
        hnorm = h1 * r2
        gp = gp_ref[...]
        hn = (hnorm * gp).astype(BF16)
        gate = _sig(_dot(hn, wg_ref[...]))
        pb = p_ref[...].astype(BF16)
        pp = jnp.concatenate([_dot(pb, wp_ref[j]) for j in range(N_DEV)], axis=-1)
        h2 = h1 + gate * pp
        err = h2 - tg_ref[...]
        loss_ref[...] += 0.5 * jnp.sum(err * err) * (1.0 / D_MODEL)
        dh2 = err * (1.0 / D_MODEL)
        dpp = (dh2 * gate).astype(BF16)
        dgpre = (dh2 * pp * gate * (1.0 - gate)).astype(BF16)
        acc_p[...] += _dot_tn(pb, dpp)
        acc_g[...] += _dot_tn(hn, dgpre)
        dhn = _dot_nt(dgpre, wg_ref[...])
        dgp_ref[...] += jnp.sum(dhn * hnorm, axis=0, keepdims=True)
        a = dhn * gp
        dh1 = dh2 + r2 * (a - hnorm * jnp.mean(a * hnorm, axis=-1, keepdims=True))
        dh1_ref[...] = dh1
        dh1b = dh1.astype(BF16)
        acc_o[0:half, :] += _dot_tn(ag_t, dh1b)
        acc_o[half:2 * half, :] += _dot_tn(sg_t, dh1b)
        dmix_ref[...] = _dot_nt(dh1b, wo_ref[...])

        @pl.when(i == nt - 1)
        def _():
            dwo_ref[...] = acc_o[...].astype(BF16)
            dwg_ref[...] = acc_g[...].astype(BF16)
            for j in range(N_DEV):
                dwp_ref[j] = acc_p[:, j * LANES:(j + 1) * LANES].astype(BF16)

    row = lambda i: (i, 0)
    s = jax.ShapeDtypeStruct
    return pl.pallas_call(
        body, name="tail_fwd_bwd", grid=(nt,),
        out_shape=[s((t_tok, D_MODEL), F32), s((t_tok, D_MODEL), F32), s((SUBLANES, LANES), F32),
                   s((1, D_MODEL), F32), s((D_MODEL, D_MODEL), BF16), s((D_MODEL, D_MODEL), BF16),
                   s((N_DEV, PLE_DIM, LANES), BF16)],
        in_specs=[pl.BlockSpec((tm, D_MODEL), row), pl.BlockSpec((tm, D_MODEL), row),
                  pl.BlockSpec((tm, half), row), pl.BlockSpec((tm, half), row), pl.BlockSpec((tm, PLE_DIM), row),
                  _full(w_out.shape), _full(w_g.shape), _full(w_p.shape), _full(g_ple.shape)],
        out_specs=[pl.BlockSpec((tm, D_MODEL), row), pl.BlockSpec((tm, D_MODEL), row), _full((SUBLANES, LANES)),
                   _full((1, D_MODEL)), _full((D_MODEL, D_MODEL)), _full((D_MODEL, D_MODEL)),
                   _full((N_DEV, PLE_DIM, LANES))],
        scratch_shapes=[pltpu.VMEM((D_MODEL, D_MODEL), F32), pltpu.VMEM((D_MODEL, D_MODEL), F32),
                        pltpu.VMEM((PLE_DIM, D_MODEL), F32)],
        compiler_params=_params(1, VMEM_LIMIT_V7X),
    )(x2, tg2, ag, sg, p2, w_out, w_g, w_p, g_ple)


def _attn_bwd(qh, kh, vb, z, o, lse, dmix, nb, seq, parts):
    t_tok = nb * seq
    n_rs = len(parts)
    rs = _ReduceScatter([p.shape for p in parts])
    n_steps = (nb, ATTN_W // LANES)

    def body(*refs):
        (q_ref, k_ref, v_ref, ga_ref, o_ref, l_ref, da_ref), refs = refs[:7], refs[7:]
        part_refs, refs = refs[:n_rs], refs[n_rs:]
        (dq_ref, dk_ref, dv_ref, dga_ref), refs = refs[:4], refs[4:]
        g_refs, refs = refs[:n_rs], refs[n_rs:]
        (qf, kf, vf, dof, dlf), rs_scratch = refs[:5], refs[5:]
        b, hp = pl.program_id(0), pl.program_id(1)

        @pl.when((b == 0) & (hp == 0))
        def _():
            rs.start(part_refs, rs_scratch)

        ga, o_t, da = ga_ref[...], o_ref[...], da_ref[...]
        sga = _sig(ga)
        d_o = da * ga * sga
        dga_ref[...] = da * o_t * sga * (1.0 + ga * (1.0 - sga))
        lane = lax.broadcasted_iota(jnp.int32, (1, LANES), 1)
        d_oo = d_o * o_t
        delta = jnp.zeros_like(d_oo)
        for h in range(LANES // HEAD_DIM):
            lm2 = (lane // HEAD_DIM) == h
            delta = jnp.where(lm2, jnp.sum(jnp.where(lm2, d_oo, 0.0), axis=-1, keepdims=True), delta)
        qf[...] = q_ref[...].astype(F32)
        kf[...] = k_ref[...].astype(F32)
        vf[...] = v_ref[...].astype(F32)
        dof[...] = d_o
        dlf[...] = delta
        dq_ref[...] = jnp.zeros_like(dq_ref)
        dk_ref[...] = jnp.zeros_like(dk_ref)
        dv_ref[...] = jnp.zeros_like(dv_ref)
        lms = _head_masks()
        for window, dil in DILATED:
            nt, tq = _window_tiling(seq, window, dil)
            q3 = _gather_classes(qf, dil, nt, tq, BF16)
            k3 = _gather_classes(kf, dil, nt, tq, BF16)
            v3 = _gather_classes(vf, dil, nt, tq, BF16)
            do3 = _gather_classes(dof, dil, nt, tq, BF16)
            lt3 = _gather_classes(l_ref, dil, nt, tq, F32)
            dl3 = _gather_classes(dlf, dil, nt, tq, F32)
            if nt > 1:
                k3, v3 = _with_prev_tile(k3, dil, nt), _with_prev_tile(v3, dil, nt)
            valid = _band_valid(dil, nt, tq)
            dq = jnp.zeros(q3.shape, F32)
            dk = jnp.zeros(k3.shape, F32)
            dv = jnp.zeros(k3.shape, F32)
            for lm in lms:
                qm = jnp.where(lm, q3, jnp.zeros_like(q3))
                dom = jnp.where(lm, do3, jnp.zeros_like(do3))
                p = jnp.where(valid, jnp.exp(_bqk(qm, k3) - _head_col(lt3, lm)), 0.0)
                dv = dv + _bkd(p.astype(BF16), dom)
                ds = (p * (_bqk(dom, v3) - _head_col(dl3, lm))).astype(BF16)
                dq = dq + jnp.where(lm, _bqd(ds, k3), 0.0)
                dk = dk + _bkd(ds, qm)
            _scatter_classes(dq_ref, dq, dil, nt, tq, add=True)
            for ref, g in ((dk_ref, dk), (dv_ref, dv)):
                if nt > 1:
                    own, prev = g[:, tq:, :], g[:, :tq, :]
                    shifted = []
                    for r in range(dil):
                        t = prev[r * nt:(r + 1) * nt]
                        shifted.append(jnp.concatenate([t[1:], jnp.zeros_like(t[:1])], axis=0))
                    g = own + (shifted[0] if dil == 1 else jnp.concatenate(shifted, axis=0))
                _scatter_classes(ref, g, dil, nt, tq, add=True)

        @pl.when((b == n_steps[0] - 1) & (hp == n_steps[1] - 1))
        def _():
            rs.finish(part_refs, rs_scratch, g_refs)

    blk = pl.BlockSpec((seq, LANES), lambda b, hp: (b, hp))
    s = jax.ShapeDtypeStruct
    outs = pl.pallas_call(
        body, name="attn_bwd", grid=n_steps,
        out_shape=[s((t_tok, ATTN_W), F32)] * 4 + [s(p.shape[1:], F32) for p in parts],
        in_specs=[blk, blk, blk, pl.BlockSpec((seq, LANES), lambda b, hp: (b, 3 * ATTN_W // LANES + hp)), blk, blk,
                  blk] + [pl.BlockSpec(memory_space=pl.ANY)] * n_rs,
        out_specs=[blk] * 4 + [_full(p.shape[1:]) for p in parts],
        scratch_shapes=[pltpu.VMEM((seq, LANES), F32)] * 5 + rs.scratch(parts[0].dtype),
        compiler_params=_params(2, VMEM_LIMIT_V7X),
    )(qh, kh, vb, z, o, lse, dmix, *parts)
    return outs[:4], outs[4:]


def _ssm_bwd(z, dmix, y, x_re, x_im, a_re, a_im, bb_re, bb_im, cc_re, cc_im, d_skip, w_glu, b_glu, nb, seq):
    t_tok = nb * seq
    tc = min(256, seq)
    nch = seq // tc
    grp = N_STATE // 4

    def body(u_ref, gs_ref, ds_ref, y_ref, xr_ref, xi_ref, xpr_ref, xpi_ref,
             ar_ref, ai_ref, bbr_ref, bbi_ref, ccr_ref, cci_ref, d_ref, wg_ref, bg_ref,
             du_ref, dgs_ref, dwg_ref, dbg_ref, dd_ref, dar_ref, dai_ref, dbbr_ref, dbbi_ref, dccr_ref, dcci_ref,
             lam_re, lam_im, car_re, car_im, acc_wg, seg_a, seg_b, ent_re, ent_im):
        step = pl.program_id(1)
        first_chunk = step == nch - 1

        @pl.when((pl.program_id(0) == 0) & (step == 0))
        def _():
            acc_wg[...] = jnp.zeros_like(acc_wg)
            for ref in (dbg_ref, dd_ref, dar_ref, dai_ref, dbbr_ref, dbbi_ref, dccr_ref, dcci_ref):
                ref[...] = jnp.zeros_like(ref)

        @pl.when(step == 0)
        def _():
            car_re[...] = jnp.zeros_like(car_re)
            car_im[...] = jnp.zeros_like(car_im)

        u, gs, dssm, y = u_ref[...], gs_ref[...], ds_ref[...], y_ref[...]
        yg, dgelu = _gelu_and_grad(y)
        ygb = yg.astype(BF16)
        sgl = _sig(_dot(ygb, wg_ref[...]) + bg_ref[...])
        sgs = _sig(gs)
        dout = dssm * gs * sgs
        dgs_ref[...] = dssm * yg * sgl * sgs * (1.0 + gs * (1.0 - sgs))
        dgl = dout * yg * sgl * (1.0 - sgl)
        dglb = dgl.astype(BF16)
        dyg = dout * sgl + _dot_nt(dglb, wg_ref[...])
        acc_wg[...] += _dot_tn(ygb, dglb)
        dbg_ref[...] += jnp.sum(dgl, axis=0, keepdims=True)
        dy = dyg * dgelu
        dd_ref[...] += jnp.sum(dy * u, axis=0, keepdims=True)
        _to_segments(seg_a, dy)
        _to_segments(seg_b, u)
        for j in range(4):
            dyj = seg_a[j].astype(BF16)
            sl = slice(j * grp, (j + 1) * grp)
            lam_re[:, sl] = _dot_nt(dyj, ccr_ref[j])
            lam_im[:, sl] = -_dot_nt(dyj, cci_ref[j])
            dccr_ref[j] += _dot_tn(xr_ref[:, sl].astype(BF16), dyj)
            dcci_ref[j] -= _dot_tn(xi_ref[:, sl].astype(BF16), dyj)

        keep_prev = jnp.where(first_chunk, 0.0, 1.0)
        seg = tc // SUBLANES
        last_blk = pl.ds((seg - 1) * SUBLANES, SUBLANES)
        row0 = lax.broadcasted_iota(jnp.int32, (SUBLANES, N_STATE), 0) == 0
        for src, prev, dst in ((xr_ref, xpr_ref, ent_re), (xi_ref, xpi_ref, ent_im)):
            before = jnp.broadcast_to(prev[SUBLANES - 1:SUBLANES, :] * keep_prev, (SUBLANES, N_STATE))
            dst[...] = jnp.where(row0, before, pltpu.roll(src[last_blk, :], 1, 0))

        def visit(cols, j, lr, li, acc):
            if j is None:
                dar_ref[:, cols] += jnp.sum(acc[0], axis=0, keepdims=True)
                dai_ref[:, cols] += jnp.sum(acc[1], axis=0, keepdims=True)
                return None
            blk = pl.ds(pl.multiple_of(jnp.maximum(j - 1, 0) * SUBLANES, SUBLANES), SUBLANES)
            inside = j > 0
            xpr = jnp.where(inside, xr_ref[blk, cols], ent_re[:, cols])
            xpi = jnp.where(inside, xi_ref[blk, cols], ent_im[:, cols])
            return acc[0] + lr * xpr + li * xpi, acc[1] + li * xpr - lr * xpi

        _scan_chunk(lam_re, lam_im, ar_ref, ai_ref, car_re, car_im, tc, reverse=True, visit=visit)

        for j in range(4):
            sl = slice(j * grp, (j + 1) * grp)
            lr = lam_re[:, sl].astype(BF16)
            li = lam_im[:, sl].astype(BF16)
            uj = seg_b[j].astype(BF16)
            seg_a[j] = _dot_nt(lr, bbr_ref[j]) + _dot_nt(li, bbi_ref[j])
            dbbr_ref[j] += _dot_tn(uj, lr)
            dbbi_ref[j] += _dot_tn(uj, li)
        du_ref[...] = _from_segments(seg_a) + dy * d_ref[...]

        @pl.when((pl.program_id(0) == nb - 1) & (step == nch - 1))
        def _():
            dwg_ref[...] = acc_wg[...].astype(BF16)

    rev = lambda b, ch: b * nch + (nch - 1 - ch)
    umap = lambda b, ch: (rev(b, ch), 4)
    gmap = lambda b, ch: (rev(b, ch), 5)
    smap = lambda b, ch: (rev(b, ch), 1)
    row = lambda b, ch: (rev(b, ch), 0)
    prev = lambda b, ch: (jnp.maximum(rev(b, ch) * (tc // SUBLANES) - 1, 0), 0)
    s = jax.ShapeDtypeStruct
    consts = [a_re, a_im, bb_re, bb_im, cc_re, cc_im, d_skip, w_glu, b_glu]
    acc_shapes = [s((1, SSM_W), F32), s((1, SSM_W), F32), s((1, N_STATE), F32), s((1, N_STATE), F32),
                  s(bb_re.shape, F32), s(bb_re.shape, F32), s(cc_re.shape, F32), s(cc_re.shape, F32)]
    return pl.pallas_call(
        body, name="ssm_bwd", grid=(nb, nch),
        out_shape=[s((t_tok, SSM_W), F32), s((t_tok, SSM_W), F32), s((SSM_W, SSM_W), BF16)] + acc_shapes,
        in_specs=[pl.BlockSpec((tc, SSM_W), umap), pl.BlockSpec((tc, SSM_W), gmap), pl.BlockSpec((tc, SSM_W), smap),
                  pl.BlockSpec((tc, SSM_W), row), pl.BlockSpec((tc, N_STATE), row), pl.BlockSpec((tc, N_STATE), row),
                  pl.BlockSpec((SUBLANES, N_STATE), prev), pl.BlockSpec((SUBLANES, N_STATE), prev)]
        + [_full(c.shape) for c in consts],
        out_specs=[pl.BlockSpec((tc, SSM_W), row), pl.BlockSpec((tc, SSM_W), row), _full((SSM_W, SSM_W))]
        + [_full(a.shape) for a in acc_shapes],
        scratch_shapes=[pltpu.VMEM((tc, N_STATE), F32), pltpu.VMEM((tc, N_STATE), F32),
                        pltpu.VMEM((SUBLANES, N_STATE), F32), pltpu.VMEM((SUBLANES, N_STATE), F32),
                        pltpu.VMEM((SSM_W, SSM_W), F32), pltpu.VMEM((4, tc, LANES), F32),
                        pltpu.VMEM((4, tc, LANES), F32),
                        pltpu.VMEM((SUBLANES, N_STATE), F32), pltpu.VMEM((SUBLANES, N_STATE), F32)],
        compiler_params=_params(2, VMEM_LIMIT_V7X),
    )(z, z, dmix, y, x_re, x_im, x_re, x_im, *consts)


def _dz_and_dx(x2, z, dqh, dkh, dvb, dga, du, dgs, dh1, w_in_g, g_mix, gq_t, gk_t, ones_bd, fold):
    t_tok = x2.shape[0]
    tm = min(256, t_tok)
    nt = t_tok // tm
    a_w = ATTN_W

    def head_norm_bwd(raw, d_hat, gain, scale, ones):
        r = lax.rsqrt(_hdot(raw * raw, ones) * (1.0 / HEAD_DIM) + EPS)
        n = raw * r
        a = d_hat * gain * scale
        d_raw = r * (a - n * (_hdot(a * n, ones) * (1.0 / HEAD_DIM)))
        return d_raw, jnp.sum(d_hat * n * scale, axis=0, keepdims=True)

    def body(x_ref, q_ref, k_ref, dq_ref, dk_ref, dv_ref, dga_ref, du_ref, dgs_ref, dh1_ref, w_ref, g_ref,
             gq_ref, gk_ref, ones_ref, fold_ref, dz_ref, gx_ref, dgm_ref, dgq_ref, dgk_ref, acc_q, acc_k):
        i = pl.program_id(0)

        @pl.when(i == 0)
        def _():
            dgm_ref[...] = jnp.zeros_like(dgm_ref)
            acc_q[...] = jnp.zeros_like(acc_q)
            acc_k[...] = jnp.zeros_like(acc_k)

        ones = ones_ref[...]
        dq, sq = head_norm_bwd(q_ref[...], dq_ref[...], gq_ref[...], HEAD_DIM ** -0.5, ones)
        dk, sk = head_norm_bwd(k_ref[...], dk_ref[...], gk_ref[...], 1.0, ones)
        acc_q[...] += jnp.broadcast_to(sq, acc_q.shape)
        acc_k[...] += jnp.broadcast_to(sk, acc_k.shape)
        parts = (dq, dk, dv_ref[...], dga_ref[...], du_ref[...], dgs_ref[...])
        for n, part in enumerate(parts):
            dz_ref[:, n * a_w:(n + 1) * a_w] = part.astype(BF16)
        dxn = jnp.zeros((tm, D_MODEL), F32)
        for j in range(N_DEV):
            dxn = dxn + _dot_nt(dz_ref[:, j * COL_W:(j + 1) * COL_W], w_ref[j])
        x = x_ref[...]
        r1 = lax.rsqrt(jnp.mean(x * x, axis=-1, keepdims=True) + EPS)
        xnorm = x * r1
        dgm_ref[...] += jnp.sum(dxn * xnorm, axis=0, keepdims=True)
        a = dxn * g_ref[...]
        gx_ref[...] = dh1_ref[...] + r1 * (a - xnorm * jnp.mean(a * xnorm, axis=-1, keepdims=True))

        @pl.when(i == nt - 1)
        def _():
            dgq_ref[...] = _hdot(acc_q[...], fold_ref[...])
            dgk_ref[...] = _hdot(acc_k[...], fold_ref[...])

    row = lambda i: (i, 0)
    col = lambda n: (lambda i: (i, n))
    s = jax.ShapeDtypeStruct
    half = pl.BlockSpec((tm, a_w), row)
    return pl.pallas_call(
        body, name="dz_dx", grid=(nt,),
        out_shape=[s((t_tok, IN_W), BF16), s((t_tok, D_MODEL), F32), s((1, D_MODEL), F32),
                   s((SUBLANES, HEAD_DIM), F32), s((SUBLANES, HEAD_DIM), F32)],
        in_specs=[pl.BlockSpec((tm, D_MODEL), row), pl.BlockSpec((tm, a_w), col(0)), pl.BlockSpec((tm, a_w), col(1)),
                  half, half, half, half, half, half, pl.BlockSpec((tm, D_MODEL), row),
                  _full(w_in_g.shape), _full(g_mix.shape), _full(gq_t.shape), _full(gk_t.shape),
                  _full(ones_bd.shape), _full(fold.shape)],
        out_specs=[pl.BlockSpec((tm, IN_W), row), pl.BlockSpec((tm, D_MODEL), row), _full((1, D_MODEL)),
                   _full((SUBLANES, HEAD_DIM)), _full((SUBLANES, HEAD_DIM))],
        scratch_shapes=[pltpu.VMEM((SUBLANES, a_w), F32), pltpu.VMEM((SUBLANES, a_w), F32)],
        compiler_params=_params(1, VMEM_LIMIT_V7X),
    )(x2, z, z, dqh, dkh, dvb, dga, du, dgs, dh1, w_in_g, g_mix, gq_t, gk_t, ones_bd, fold)


def _dw_in(xn, dz, glu_parts, small):
    t_tok = xn.shape[0]
    tk = min(1024, t_tok)
    nk = t_tok // tk
    rs = _ReduceScatter([glu_parts.shape])
    ag = _AllGather(1, cast=False)
    n_rs = len(rs.scratch(BF16))

    def my_index():
        return 4 * lax.axis_index("x") + 2 * lax.axis_index("y") + lax.axis_index("c")

    def target(i):
        return (my_index() + 1 + i) % N_DEV

    def body(xn_ref, dz_ref, glu_ref, small_ref, gin_ref, gglu_ref, gath_ref, acc, stage, land, send_sems, recv_sems,
             *rest):
        rs_scratch, ag_sems = rest[:n_rs], rest[n_rs:]
        i, k = pl.program_id(0), pl.program_id(1)

        def push(step):
            j = target(step)
            return pltpu.make_async_remote_copy(
                src_ref=stage.at[step], dst_ref=land.at[step], send_sem=send_sems.at[step],
                recv_sem=recv_sems.at[step], device_id=(j // 4, (j // 2) % 2, j % 2), device_id_type=MESH)

        @pl.when((i == 0) & (k == 0))
        def _():
            rs.start([glu_ref], rs_scratch)
            ag.start([small_ref], [gath_ref], ag_sems)

        @pl.when((i == N_DEV // 2) & (k == 0))
        def _():
            ag.forward([small_ref], [gath_ref], ag_sems)

        @pl.when(k == 0)
        def _():
            acc[...] = jnp.zeros_like(acc)

        acc[...] += _dot_tn(xn_ref[...], dz_ref[...])

        @pl.when((k == nk - 1) & (i < N_DEV - 1))
        def _():
            stage[i] = acc[...].astype(BF16)
            push(i).start()

        @pl.when((k == nk - 1) & (i == N_DEV - 1))
        def _():
            for step in range(N_DEV - 1):
                push(step).wait_recv()
            chunk, steps = _row_chunks(D_MODEL)

            def add(s, carry):
                r = pl.ds(pl.multiple_of(s * chunk, chunk), chunk)
                total = acc[r, :]
                for step in range(N_DEV - 1):
                    total = total + land[step, r, :].astype(F32)
                gin_ref[r, :] = total
                return carry

            lax.fori_loop(0, steps, add, 0)
            for step in range(N_DEV - 1):
                push(step).wait_send()
            rs.finish([glu_ref], rs_scratch, [gglu_ref])
            ag.finish([small_ref], [gath_ref], ag_sems)

    any_spec = pl.BlockSpec(memory_space=pl.ANY)
    s = jax.ShapeDtypeStruct
    return pl.pallas_call(
        body, name="dw_in", grid=(N_DEV, nk),
        out_shape=[s((D_MODEL, COL_W), F32), s(glu_parts.shape[1:], F32), s((N_DEV,) + small.shape, F32)],
        in_specs=[pl.BlockSpec((tk, D_MODEL), lambda i, k: (k, 0)),
                  pl.BlockSpec((tk, COL_W), lambda i, k: (k, target(i))), any_spec, any_spec],
        out_specs=[_full((D_MODEL, COL_W)), _full(glu_parts.shape[1:]), any_spec],
        scratch_shapes=[pltpu.VMEM((D_MODEL, COL_W), F32), pltpu.VMEM((N_DEV - 1, D_MODEL, COL_W), BF16),
                        pltpu.VMEM((N_DEV - 1, D_MODEL, COL_W), BF16), pltpu.SemaphoreType.DMA((N_DEV - 1,)),
                        pltpu.SemaphoreType.DMA((N_DEV - 1,))] + rs.scratch(BF16) + ag.scratch(),
        compiler_params=_params(2, VMEM_LIMIT_V7X),
    )(xn, dz, glu_parts, small)


SMALL = ("mix_norm", "q_norm", "k_norm", "lambda_re", "lambda_im", "log_dt", "b_re", "b_im", "c_re", "c_im",
         "d_skip", "b_glu", "ple_norm")
BIG = ("w_in", "w_glu", "w_out", "w_ple_gate", "w_ple_proj")
WEIGHTS = ("mix_norm", "w_in", "q_norm", "k_norm", "lambda_re", "lambda_im", "log_dt", "b_re", "b_im", "c_re",
           "c_im", "d_skip", "w_glu", "b_glu", "w_out", "ple_norm", "w_ple_gate", "w_ple_proj")


def _pack(arrs):
    flat = jnp.concatenate([a.reshape(-1).astype(F32) for a in arrs])
    rows = -(-flat.shape[0] // (64 * LANES)) * 64
    return jnp.pad(flat, (0, rows * LANES - flat.shape[0])).reshape(rows, LANES)


def _unpack(packed, shapes):
    flat = packed.reshape(-1)
    out, off = [], 0
    for shp in shapes:
        size = math.prod(shp)
        out.append(flat[off:off + size].reshape(shp))
        off += size
    return out


def kernel(x, p, mix_norm, w_in, q_norm, k_norm, lambda_re, lambda_im, log_dt, b_re, b_im, c_re, c_im, d_skip, w_glu, b_glu, w_out, ple_norm, w_ple_gate, w_ple_proj, loss_target, m_mix_norm, m_w_in, m_q_norm, m_k_norm, m_lambda_re, m_lambda_im, m_log_dt, m_b_re, m_b_im, m_c_re, m_c_im, m_d_skip, m_w_glu, m_b_glu, m_w_out, m_ple_norm, m_w_ple_gate, m_w_ple_proj, v_mix_norm, v_w_in, v_q_norm, v_k_norm, v_lambda_re, v_lambda_im, v_log_dt, v_b_re, v_b_im, v_c_re, v_c_im, v_d_skip, v_w_glu, v_b_glu, v_w_out, v_ple_norm, v_w_ple_gate, v_w_ple_proj):
    env = dict(locals())
    w = {n: env[n] for n in WEIGHTS}
    m = {n: env["m_" + n] for n in WEIGHTS}
    v = {n: env["v_" + n] for n in WEIGHTS}
    nb, seq, _ = x.shape
    t_tok = nb * seq
    x2 = x.reshape(t_tok, D_MODEL)
    tg2 = loss_target.reshape(t_tok, D_MODEL)
    p2 = p.reshape(t_tok, PLE_DIM)

    shard2d = {"w_in": (D_MODEL, COL_W), "w_glu": (SSM_W // N_DEV, SSM_W), "w_out": (D_MODEL // N_DEV, D_MODEL),
               "w_ple_gate": (D_MODEL // N_DEV, D_MODEL), "w_ple_proj": (PLE_DIM, D_MODEL // N_DEV)}
    w_sh = [w[n].reshape(shard2d[n]) for n in BIG]
    w_in_g, w_glu_g, w_out_g, w_g_g, w_p_g = _all_gather(w_sh, [BF16] * len(BIG), "gather_weights")
    w_glu_f = w_glu_g.reshape(SSM_W, SSM_W)
    w_out_f = w_out_g.reshape(D_MODEL, D_MODEL)
    w_g_f = w_g_g.reshape(D_MODEL, D_MODEL)

    g3 = (SSM_GROUPS, 1, SSM_STATE)
    lr3, li3 = lambda_re.reshape(g3), lambda_im.reshape(g3)
    dt3 = log_dt.reshape(SSM_GROUPS, 1, 1)
    btr = b_re[0].transpose(0, 2, 1)
    bti = b_im[0].transpose(0, 2, 1)
    a_re3, a_im3, bbr, bbi = _zoh_fwd(lr3, li3, dt3, btr, bti)
    a_re, a_im = a_re3.reshape(1, N_STATE), a_im3.reshape(1, N_STATE)
    bb_re, bb_im = _blockdiag(bbr).astype(BF16), _blockdiag(bbi).astype(BF16)
    cc_re = _blockdiag(c_re[0].transpose(0, 2, 1)).astype(BF16)
    cc_im = _blockdiag(c_im[0].transpose(0, 2, 1)).astype(BF16)

    ones_bd = _head_ones()
    fold = jnp.tile(jnp.eye(HEAD_DIM, dtype=F32), (ATTN_W // HEAD_DIM, 1))
    gq_t = jnp.tile(q_norm, (1, ATTN_W // HEAD_DIM))
    gk_t = jnp.tile(k_norm, (1, ATTN_W // HEAD_DIM))

    z, qh, kh, vb, xn = _in_proj(x2, mix_norm, w_in_g, ones_bd, gq_t, gk_t)
    o, lse, ag = _attn_fwd(qh, kh, vb, z, nb, seq)
    x_re, x_im, y, sg = _ssm_fwd(z, a_re, a_im, bb_re, bb_im, cc_re, cc_im, d_skip, w_glu_f, b_glu, nb, seq)
    dmix, dh1, loss_t, d_ple, dw_out, dw_g, dw_p = _tail(x2, tg2, ag, sg, p2, w_out_f, w_g_f, w_p_g, ple_norm)

    early_parts = [dw_out.reshape(N_DEV, D_MODEL // N_DEV, D_MODEL), dw_g.reshape(N_DEV, D_MODEL // N_DEV, D_MODEL),
                   dw_p]
    (dqh, dkh, dvb, dga), (g_out, g_g, g_p) = _attn_bwd(qh, kh, vb, z, o, lse, dmix, nb, seq, early_parts)
    (du, dgs, dw_glu, d_bglu, d_dskip, da_re, da_im, dbb_re, dbb_im, dcc_re, dcc_im) = _ssm_bwd(
        z, dmix, y, x_re, x_im, a_re, a_im, bb_re, bb_im, cc_re, cc_im, d_skip, w_glu_f, b_glu, nb, seq)
    dz, gx, d_mix, d_gq, d_gk = _dz_and_dx(x2, z, dqh, dkh, dvb, dga, du, dgs, dh1, w_in_g, mix_norm, gq_t, gk_t,
                                           ones_bd, fold)
    d_lr, d_li, d_dt, d_btr, d_bti = _zoh_bwd(
        lr3, li3, dt3, btr, bti, da_re.reshape(g3), da_im.reshape(g3),
        _blockdiag_extract(dbb_re, SSM_GROUP, SSM_STATE), _blockdiag_extract(dbb_im, SSM_GROUP, SSM_STATE))
    small_g = {
        "mix_norm": d_mix, "q_norm": d_gq[0:1], "k_norm": d_gk[0:1], "lambda_re": d_lr, "lambda_im": d_li,
        "log_dt": d_dt, "b_re": d_btr.transpose(0, 2, 1), "b_im": d_bti.transpose(0, 2, 1),
        "c_re": _blockdiag_extract(dcc_re, SSM_STATE, SSM_GROUP).transpose(0, 2, 1),
        "c_im": _blockdiag_extract(dcc_im, SSM_STATE, SSM_GROUP).transpose(0, 2, 1),
        "d_skip": d_dskip, "b_glu": d_bglu, "ple_norm": d_ple}

    g_in, g_glu, gathered = _dw_in(xn, dz, dw_glu.reshape(N_DEV, SSM_W // N_DEV, SSM_W),
                                   _pack([small_g[n] for n in SMALL] + [loss_t[0:1, 0:1]]))
    g_sh = [g_in, g_glu, g_out, g_g, g_p]
    d_sh, m_sh, v_sh = _adamw_shards(g_sh, w_sh, [m[n].reshape(shard2d[n]) for n in BIG],
                                     [v[n].reshape(shard2d[n]) for n in BIG])

    g_pk, d_pk, m_pk, v_pk = _small_update(gathered, _pack([w[n] for n in SMALL]), _pack([m[n] for n in SMALL]),
                                           _pack([v[n] for n in SMALL]))

    grads, deltas, new_m, new_v = {}, {}, {}, {}
    small_shapes = [w[n].shape for n in SMALL]
    for dst, packed in ((grads, g_pk), (deltas, d_pk), (new_m, m_pk), (new_v, v_pk)):
        for n, a in zip(SMALL, _unpack(packed, small_shapes)):
            dst[n] = a
    for i, n in enumerate(BIG):
        grads[n] = g_sh[i].reshape(w[n].shape)
        deltas[n] = d_sh[i].reshape(w[n].shape)
        new_m[n] = m_sh[i].reshape(w[n].shape)
        new_v[n] = v_sh[i].reshape(w[n].shape)

    loss = _unpack(g_pk, small_shapes + [()])[-1]
    return (loss, gx.reshape(x.shape), *[grads[n] for n in WEIGHTS], *[deltas[n] for n in WEIGHTS],
            *[new_m[n] for n in WEIGHTS], *[new_v[n] for n in WEIGHTS])
```

```python
import math

import jax
import jax.numpy as jnp
from jax import lax
from jax.experimental import pallas as pl
from jax.experimental.pallas import tpu as pltpu

F32 = jnp.float32
BF16 = jnp.bfloat16
MESH = pl.DeviceIdType.MESH
AXES = ("x", "y", "c")
N_DEV = 8

D_MODEL = 1024
HEAD_DIM = 64
ATTN_W = 512
SSM_W = 512
SSM_GROUPS = 32
SSM_GROUP = 16
SSM_STATE = 64
N_STATE = SSM_GROUPS * SSM_STATE
PLE_DIM = 256
IN_W = 3072
COL_W = IN_W // N_DEV
DILATED = ((128, 1), (512, 4), (2048, 16))
EPS = 1e-6
INV_SQRT2 = 1.0 / math.sqrt(2.0)
INV_SQRT_2PI = 1.0 / math.sqrt(2.0 * math.pi)

ADAM_LR, ADAM_B1, ADAM_B2, ADAM_EPS, ADAM_WD, ADAM_STEP = 0.001, 0.9, 0.999, 1e-08, 0.01, 10

VMEM_LIMIT_V7X = 56 * 1024 * 1024
SUBLANES = 8
LANES = 128


def _params(n_axes=None, vmem=None):
    kw = {}
    if n_axes:
        kw["dimension_semantics"] = ("arbitrary",) * n_axes
    if vmem:
        kw["vmem_limit_bytes"] = vmem
    return pltpu.CompilerParams(**kw)


def _dot(a, b):
    return jnp.dot(a, b, preferred_element_type=F32)


def _dot_nt(a, b):
    return lax.dot_general(a, b, (((1,), (1,)), ((), ())), preferred_element_type=F32)


def _dot_tn(a, b):
    return lax.dot_general(a, b, (((0,), (0,)), ((), ())), preferred_element_type=F32)


def _hdot(a, b):
    return jnp.dot(a, b, precision=lax.Precision.HIGHEST, preferred_element_type=F32)


def _sig(x):
    return 1.0 / (1.0 + jnp.exp(-x))


def _gelu_and_grad(y):
    cdf = 0.5 * (1.0 + lax.erf(y * INV_SQRT2))
    pdf = jnp.exp(-0.5 * y * y) * INV_SQRT_2PI
    return y * cdf, cdf + y * pdf


def _vmem():
    return pl.BlockSpec(memory_space=pltpu.VMEM)


def _full(shape):
    nd = len(shape)
    return pl.BlockSpec(shape, lambda *_: (0,) * nd)


class _AllGather:
    def __init__(self, n, cast):
        self.n, self.cast = n, cast

    def scratch(self):
        n = self.n
        return [pltpu.SemaphoreType.DMA((7 * n,)), pltpu.SemaphoreType.DMA((7 * n,)), pltpu.SemaphoreType.DMA((n,))]

    def _plan(self, src_refs, out_refs, sems):
        send_sems, recv_sems, own_sems = sems
        x, y, c = lax.axis_index("x"), lax.axis_index("y"), lax.axis_index("c")
        me, sibling = (x, y, c), (x, y, 1 - c)
        chips = [(1 - x, y), (x, 1 - y), (1 - x, 1 - y)]

        def idx(px, py, pc):
            return 4 * px + 2 * py + pc

        def copy(i, k, block, to, own_src=False):
            ref = out_refs[i].at[idx(*block)]
            return pltpu.make_async_remote_copy(
                src_ref=src_refs[i] if own_src and not self.cast else ref, dst_ref=ref,
                send_sem=send_sems.at[7 * i + k], recv_sem=recv_sems.at[7 * i + k],
                device_id=to, device_id_type=MESH)

        first, passed, arrive_ici, arrive_d2d, own = [], [], [], [], []
        for i in range(self.n):
            first.append(copy(i, 0, me, sibling, own_src=True))
            first += [copy(i, 1 + j, me, (*chip, c), own_src=True) for j, chip in enumerate(chips)]
            arrive_ici += [copy(i, 1 + j, (*chip, c), me) for j, chip in enumerate(chips)]
            passed += [copy(i, 4 + j, (*chip, c), sibling) for j, chip in enumerate(chips)]
            arrive_d2d.append(copy(i, 0, sibling, me))
            arrive_d2d += [copy(i, 4 + j, (*chip, 1 - c), me) for j, chip in enumerate(chips)]
            if not self.cast:
                own.append(pltpu.make_async_copy(src_refs[i], out_refs[i].at[idx(*me)], own_sems.at[i]))
        return idx(*me), first, passed, arrive_ici, arrive_d2d, own

    def start(self, src_refs, out_refs, sems):
        my, first, _, _, _, own = self._plan(src_refs, out_refs, sems)
        if self.cast:
            for i in range(self.n):
                out_refs[i][my] = src_refs[i][...].astype(out_refs[i].dtype)
        for cp in own + first:
            cp.start()

    def forward(self, src_refs, out_refs, sems):
        _, _, passed, arrive_ici, _, _ = self._plan(src_refs, out_refs, sems)
        for cp in arrive_ici:
            cp.wait_recv()
        for cp in passed:
            cp.start()

    def finish(self, src_refs, out_refs, sems):
        _, first, passed, _, arrive_d2d, own = self._plan(src_refs, out_refs, sems)
        for cp in own:
            cp.wait()
        for cp in arrive_d2d:
            cp.wait_recv()
        for cp in first + passed:
            cp.wait_send()


def _all_gather(shards, out_dtypes, name):
    n = len(shards)
    ag = _AllGather(n, cast=True)

    def body(*refs):
        in_refs, out_refs, sems = refs[:n], refs[n:2 * n], refs[2 * n:]
        ag.start(in_refs, out_refs, sems)
        ag.forward(in_refs, out_refs, sems)
        ag.finish(in_refs, out_refs, sems)

    return pl.pallas_call(
        body, name=name,
        out_shape=[jax.ShapeDtypeStruct((N_DEV,) + s.shape, dt) for s, dt in zip(shards, out_dtypes)],
        in_specs=[_vmem()] * n, out_specs=[_vmem()] * n,
        scratch_shapes=ag.scratch(),
        compiler_params=_params(vmem=VMEM_LIMIT_V7X),
    )(*shards)


def _row_chunks(rows):
    chunk = 64 if rows % 64 == 0 else rows
    return chunk, rows // chunk


class _ReduceScatter:
    def __init__(self, shapes):
        self.shapes = shapes
        self.n = len(shapes)

    def scratch(self, dtype):
        return ([pltpu.VMEM(s, dtype) for s in self.shapes]
                + [pltpu.SemaphoreType.DMA((7 * self.n,)), pltpu.SemaphoreType.DMA((7 * self.n,)),
                   pltpu.SemaphoreType.DMA((self.n,))])

    def _copies(self, in_refs, land_refs, send_sems, recv_sems, own_sems):
        x, y, c = lax.axis_index("x"), lax.axis_index("y"), lax.axis_index("c")
        remote, own = [], []
        for i in range(self.n):
            for m in range(1, N_DEV):
                px = 1 - x if m & 4 else x
                py = 1 - y if m & 2 else y
                pc = 1 - c if m & 1 else c
                remote.append(pltpu.make_async_remote_copy(
                    src_ref=in_refs[i].at[4 * px + 2 * py + pc], dst_ref=land_refs[i].at[m - 1],
                    send_sem=send_sems.at[7 * i + m - 1], recv_sem=recv_sems.at[7 * i + m - 1],
                    device_id=(px, py, pc), device_id_type=MESH))
            own.append(pltpu.make_async_copy(in_refs[i].at[4 * x + 2 * y + c], land_refs[i].at[N_DEV - 1],
                                             own_sems.at[i]))
        return remote, own

    def start(self, in_refs, scratch):
        remote, own = self._copies(in_refs, scratch[:self.n], *scratch[self.n:])
        for cp in remote + own:
            cp.start()

    def finish(self, in_refs, scratch, out_refs):
        land_refs = scratch[:self.n]
        remote, own = self._copies(in_refs, land_refs, *scratch[self.n:])
        for cp in own:
            cp.wait()
        for cp in remote:
            cp.wait_recv()
        for i in range(self.n):
            chunk, steps = _row_chunks(self.shapes[i][1])

            def step(s, carry, i=i, chunk=chunk):
                r = pl.ds(pl.multiple_of(s * chunk, chunk), chunk)
                acc = land_refs[i][N_DEV - 1, r, :].astype(F32)
                for m in range(1, N_DEV):
                    acc = acc + land_refs[i][m - 1, r, :].astype(F32)
                out_refs[i][r, :] = acc
                return carry

            lax.fori_loop(0, steps, step, 0)
        for cp in remote:
            cp.wait_send()


def _reduce_scatter(parts, name):
    n = len(parts)
    rs = _ReduceScatter([p.shape for p in parts])

    def body(*refs):
        in_refs, out_refs, scratch = refs[:n], refs[n:2 * n], refs[2 * n:]
        rs.start(in_refs, scratch)
        rs.finish(in_refs, scratch, out_refs)

    return pl.pallas_call(
        body, name=name,
        out_shape=[jax.ShapeDtypeStruct(p.shape[1:], F32) for p in parts],
        in_specs=[_vmem()] * n, out_specs=[_vmem()] * n,
        scratch_shapes=rs.scratch(parts[0].dtype),
        compiler_params=_params(vmem=VMEM_LIMIT_V7X),
    )(*parts)


def _adamw_math(w, g, m, v):
    m = ADAM_B1 * m + (1.0 - ADAM_B1) * g
    v = ADAM_B2 * v + (1.0 - ADAM_B2) * (g * g)
    m_hat = m / (1.0 - ADAM_B1 ** ADAM_STEP)
    v_hat = v / (1.0 - ADAM_B2 ** ADAM_STEP)
    delta = -ADAM_LR * (m_hat / (jnp.sqrt(v_hat) + ADAM_EPS) + ADAM_WD * w)
    return delta, m, v


def _adamw_shards(gs, ws, ms, vs):
    n = len(gs)

    def body(*refs):
        g_refs, w_refs, m_refs, v_refs = (refs[k * n:(k + 1) * n] for k in range(4))
        d_out, m_out, v_out = (refs[(4 + k) * n:(5 + k) * n] for k in range(3))
        for i in range(n):
            chunk, steps = _row_chunks(gs[i].shape[0])

            def step(s, carry, i=i, chunk=chunk):
                r = pl.ds(pl.multiple_of(s * chunk, chunk), chunk)
                d, m, v = _adamw_math(w_refs[i][r, :], g_refs[i][r, :], m_refs[i][r, :], v_refs[i][r, :])
                d_out[i][r, :] = d
                m_out[i][r, :] = m
                v_out[i][r, :] = v
                return carry

            lax.fori_loop(0, steps, step, 0)

    shapes = [jax.ShapeDtypeStruct(g.shape, F32) for g in gs]
    outs = pl.pallas_call(
        body, name="adamw_shards", out_shape=shapes * 3,
        in_specs=[_vmem()] * (4 * n), out_specs=[_vmem()] * (3 * n),
        compiler_params=_params(vmem=VMEM_LIMIT_V7X),
    )(*gs, *ws, *ms, *vs)
    return outs[:n], outs[n:2 * n], outs[2 * n:]


def _small_update(gathered, w, m, v):
    rows = w.shape[0]
    chunk, steps = _row_chunks(rows)

    def body(ga_ref, w_ref, m_ref, v_ref, g_out, d_out, m_out, v_out):
        def step(s, carry):
            r = pl.ds(pl.multiple_of(s * chunk, chunk), chunk)
            g = ga_ref[0, r, :]
            for j in range(1, N_DEV):
                g = g + ga_ref[j, r, :]
            d, mm, vv = _adamw_math(w_ref[r, :], g, m_ref[r, :], v_ref[r, :])
            g_out[r, :] = g
            d_out[r, :] = d
            m_out[r, :] = mm
            v_out[r, :] = vv
            return carry

        lax.fori_loop(0, steps, step, 0)

    return pl.pallas_call(
        body, name="small_update", out_shape=[jax.ShapeDtypeStruct(w.shape, F32)] * 4,
        in_specs=[_vmem()] * 4, out_specs=[_vmem()] * 4,
    )(gathered, w, m, v)


def _zoh(lr, li, logdt, btr, bti):
    dt = jnp.exp(logdt)
    mag = jnp.exp(lr * dt)
    th = li * dt
    ar = mag * jnp.cos(th)
    ai = mag * jnp.sin(th)
    den = lr * lr + li * li
    nr = ar - 1.0
    cr = (nr * lr + ai * li) / den
    ci = (ai * lr - nr * li) / den
    return ar, ai, cr * btr - ci * bti, cr * bti + ci * btr


def _zoh_fwd(lr, li, logdt, btr, bti):
    def body(lr_ref, li_ref, dt_ref, br_ref, bi_ref, ar_ref, ai_ref, bbr_ref, bbi_ref):
        ar, ai, bbr, bbi = _zoh(lr_ref[...], li_ref[...], dt_ref[...], br_ref[...], bi_ref[...])
        ar_ref[...] = ar
        ai_ref[...] = ai
        bbr_ref[...] = bbr
        bbi_ref[...] = bbi

    s = jax.ShapeDtypeStruct
    return pl.pallas_call(
        body, name="zoh_fwd",
        out_shape=[s(lr.shape, F32), s(lr.shape, F32), s(btr.shape, F32), s(btr.shape, F32)],
        in_specs=[_vmem()] * 5, out_specs=[_vmem()] * 4,
    )(lr, li, logdt, btr, bti)


def _zoh_bwd(lr, li, logdt, btr, bti, dar, dai, dbbr, dbbi):
    def body(lr_ref, li_ref, dt_ref, br_ref, bi_ref, dar_ref, dai_ref, dbbr_ref, dbbi_ref,
             glr_ref, gli_ref, gdt_ref, gbr_ref, gbi_ref):
        _, vjp = jax.vjp(_zoh, lr_ref[...], li_ref[...], dt_ref[...], br_ref[...], bi_ref[...])
        glr, gli, gdt, gbr, gbi = vjp((dar_ref[...], dai_ref[...], dbbr_ref[...], dbbi_ref[...]))
        glr_ref[...] = glr
        gli_ref[...] = gli
        gdt_ref[...] = gdt
        gbr_ref[...] = gbr
        gbi_ref[...] = gbi

    s = jax.ShapeDtypeStruct
    return pl.pallas_call(
        body, name="zoh_bwd",
        out_shape=[s(lr.shape, F32), s(lr.shape, F32), s(logdt.shape, F32), s(btr.shape, F32), s(btr.shape, F32)],
        in_specs=[_vmem()] * 9, out_specs=[_vmem()] * 5,
    )(lr, li, logdt, btr, bti, dar, dai, dbbr, dbbi)


def _blockdiag(t):
    g, r, s = t.shape
    t = t.reshape(4, 8, r, s)
    out = jnp.einsum("jirs,ik->jirks", t, jnp.eye(8, dtype=t.dtype))
    return out.reshape(4, 8 * r, 8 * s)


def _blockdiag_extract(m, r, s):
    m = m.reshape(4, 8, r, 8, s)
    out = jnp.einsum("jirks,ik->jirs", m, jnp.eye(8, dtype=m.dtype))
    return out.reshape(32, r, s)


def _head_ones():
    r = jnp.arange(ATTN_W) // HEAD_DIM
    return (r[:, None] == r[None, :]).astype(F32)


def _in_proj(x2, g_mix, w_in_sh, late_sh):
    t_tok = x2.shape[0]
    tm = min(1024, t_tok)
    nt = t_tok // tm
    n_late = len(late_sh)
    ag_w = _AllGather(1, cast=True)
    ag_l = _AllGather(n_late, cast=True)
    n_sem = len(ag_w.scratch())

    def owner(i):
        x, y, c = lax.axis_index("x"), lax.axis_index("y"), lax.axis_index("c")
        rel = jnp.where(i < 2, 0, (i - 2) % 3 + 1)
        px = jnp.where((rel == 1) | (rel == 3), 1 - x, x)
        py = jnp.where((rel == 2) | (rel == 3), 1 - y, y)
        pc = jnp.where((i == 1) | (i >= 5), 1 - c, c)
        return 4 * px + 2 * py + pc

    def body(*refs):
        (x_ref, g_ref, w_ref), refs = refs[:3], refs[3:]
        late_refs, refs = refs[:n_late], refs[n_late:]
        (z_ref, xn_ref, wg_ref), refs = refs[:3], refs[3:]
        lateg_refs, refs = refs[:n_late], refs[n_late:]
        (xn_scr, w_land), refs = refs[:2], refs[2:]
        late_land, refs = refs[:n_late], refs[n_late:]
        sems_w, sems_l, out_sems = refs[:n_sem], refs[n_sem:2 * n_sem], refs[2 * n_sem]
        i, t = pl.program_id(0), pl.program_id(1)
        _, first, passed, arrive_ici, arrive_d2d, _ = ag_w._plan([w_ref], [w_land], sems_w)

        @pl.when((i == 0) & (t == 0))
        def _():
            ag_w.start([w_ref], [w_land], sems_w)
            ag_l.start(late_refs, late_land, sems_l)

        @pl.when((i == 1) & (t == 0))
        def _():
            arrive_d2d[0].wait_recv()

        for n in range(3):
            @pl.when((i == 2 + n) & (t == 0))
            def _(n=n):
                arrive_ici[n].wait_recv()
                passed[n].start()

            @pl.when((i == 5 + n) & (t == 0))
            def _(n=n):
                arrive_d2d[1 + n].wait_recv()

        @pl.when((i == N_DEV // 2) & (t == 0))
        def _():
            ag_l.forward(late_refs, late_land, sems_l)

        @pl.when(i == 0)
        def _():
            x = x_ref[...]
            r = lax.rsqrt(jnp.mean(x * x, axis=-1, keepdims=True) + EPS)
            xn = (x * r * g_ref[...]).astype(BF16)
            xn_ref[...] = xn
            xn_scr[t] = xn

        z_ref[...] = _dot(xn_scr[t], w_land[owner(i)])

        @pl.when((i == N_DEV - 1) & (t == nt - 1))
        def _():
            for cp in first + passed:
                cp.wait_send()
            ag_l.finish(late_refs, late_land, sems_l)
            outs = [pltpu.make_async_copy(w_land, wg_ref, out_sems.at[0])]
            outs += [pltpu.make_async_copy(late_land[n], lateg_refs[n], out_sems.at[1 + n]) for n in range(n_late)]
            for cp in outs:
                cp.start()
            for cp in outs:
                cp.wait()

    any_spec = pl.BlockSpec(memory_space=pl.ANY)
    s = jax.ShapeDtypeStruct
    xmap = lambda i, t: (jnp.where(i == 0, t, nt - 1), 0)
    gathered = [s((N_DEV,) + a.shape, BF16) for a in [w_in_sh] + late_sh]
    return pl.pallas_call(
        body, name="in_proj", grid=(N_DEV, nt),
        out_shape=[s((t_tok, IN_W), F32), s((t_tok, D_MODEL), BF16)] + gathered,
        in_specs=[pl.BlockSpec((tm, D_MODEL), xmap), _full(g_mix.shape), _full(w_in_sh.shape)]
        + [_full(a.shape) for a in late_sh],
        out_specs=[pl.BlockSpec((tm, COL_W), lambda i, t: (t, owner(i))), pl.BlockSpec((tm, D_MODEL), xmap)]
        + [any_spec] * (1 + n_late),
        scratch_shapes=[pltpu.VMEM((nt, tm, D_MODEL), BF16)]
        + [pltpu.VMEM(g.shape, BF16) for g in gathered] + ag_w.scratch() + ag_l.scratch()
        + [pltpu.SemaphoreType.DMA((1 + n_late,))],
        compiler_params=_params(2, VMEM_LIMIT_V7X),
    )(x2, g_mix, w_in_sh, *late_sh)


TQ = 128
NEG = -1e30


def _head_col(t, lm):
    return jnp.max(jnp.where(lm, t, NEG), axis=-1, keepdims=True)


def _head_masks():
    lane = lax.broadcasted_iota(jnp.int32, (1, 1, LANES), 2)
    return [(lane // HEAD_DIM) == h for h in range(LANES // HEAD_DIM)]


def _gather_classes(ref, dil, nt, tq, dtype):
    length = nt * tq
    if dil == 1:
        return ref[...].astype(dtype).reshape(nt, tq, LANES)
    parts = [ref[pl.ds(r, length, stride=dil), :].astype(dtype).reshape(nt, tq, LANES) for r in range(dil)]
    return jnp.concatenate(parts, axis=0)


def _scatter_classes(ref, val, dil, nt, tq, add):
    length = nt * tq
    for r in range(dil):
        rows = pl.ds(r, length, stride=dil) if dil > 1 else slice(None)
        part = val[r * nt:(r + 1) * nt].reshape(length, LANES)
        ref[rows, :] = ref[rows, :] + part if add else part


def _with_prev_tile(t3, dil, nt):
    parts = []
    for r in range(dil):
        t = t3[r * nt:(r + 1) * nt]
        parts.append(jnp.concatenate([t[:1], t[:-1]], axis=0))
    prev = parts[0] if dil == 1 else jnp.concatenate(parts, axis=0)
    return jnp.concatenate([prev, t3], axis=1)


def _band_valid(dil, nt, tq):
    if nt == 1:
        shape = (dil, tq, tq)
        return lax.broadcasted_iota(jnp.int32, shape, 1) >= lax.broadcasted_iota(jnp.int32, shape, 2)
    shape = (dil * nt, tq, 2 * tq)
    b = lax.broadcasted_iota(jnp.int32, shape, 0)
    c = lax.broadcasted_iota(jnp.int32, shape, 2)
    d = tq + lax.broadcasted_iota(jnp.int32, shape, 1) - c
    return (d >= 0) & (d <= tq) & (((b & (nt - 1)) != 0) | (c >= tq))


def _window_tiling(seq, window, dil):
    length = seq // dil
    tq = min(TQ, length)
    nt = length // tq
    assert length % tq == 0 and nt & (nt - 1) == 0 and (nt == 1 or window == tq * dil)
    return nt, tq


def _bqk(a, b):
    return jnp.einsum("bqd,bkd->bqk", a, b, preferred_element_type=F32)


def _bqd(a, b):
    return jnp.einsum("bqk,bkd->bqd", a, b, preferred_element_type=F32)


def _bkd(a, b):
    return jnp.einsum("bqk,bqd->bkd", a, b, preferred_element_type=F32)


def _qk_hat(q_ref, k_ref, gq_ref, gk_ref):
    lane = lax.broadcasted_iota(jnp.int32, (1, LANES), 1)

    def norm(raw, gain, scale):
        sq = raw * raw
        r = jnp.zeros_like(raw)
        for h in range(LANES // HEAD_DIM):
            lm = (lane // HEAD_DIM) == h
            ms = jnp.sum(jnp.where(lm, sq, 0.0), axis=-1, keepdims=True) * (1.0 / HEAD_DIM)
            r = jnp.where(lm, lax.rsqrt(ms + EPS), r)
        return raw * r * gain * scale

    return norm(q_ref[...], gq_ref[...], HEAD_DIM ** -0.5), norm(k_ref[...], gk_ref[...], 1.0)


def _zblock(seq, group):
    return pl.BlockSpec((seq, LANES), lambda b, hp: (b, group * (ATTN_W // LANES) + hp))


def _attn_fwd(z, gq2, gk2, nb, seq):
    t_tok = nb * seq
    n_win = len(DILATED)

    def body(q_ref, k_ref, v_ref, ga_ref, gq_ref, gk_ref, o_ref, l_ref, ag_ref, qf, kf, oc, lc):
        qf[...], kf[...] = _qk_hat(q_ref, k_ref, gq_ref, gk_ref)
        lms = _head_masks()
        for w, (window, dil) in enumerate(DILATED):
            nt, tq = _window_tiling(seq, window, dil)
            q3 = _gather_classes(qf, dil, nt, tq, BF16)
            k3 = _gather_classes(kf, dil, nt, tq, BF16)
            v3 = _gather_classes(v_ref, dil, nt, tq, BF16)
            if nt > 1:
                k3, v3 = _with_prev_tile(k3, dil, nt), _with_prev_tile(v3, dil, nt)
            valid = _band_valid(dil, nt, tq)
            o = jnp.zeros(q3.shape, F32)
            lse = jnp.zeros(q3.shape, F32)
            for lm in lms:
                s = _bqk(jnp.where(lm, q3, jnp.zeros_like(q3)), k3)
                m = jnp.max(jnp.where(valid, s, NEG), axis=-1, keepdims=True)
                p = jnp.where(valid, jnp.exp(s - m), 0.0)
                den = jnp.sum(p, axis=-1, keepdims=True)
                o = jnp.where(lm, _bqd(p.astype(BF16), v3) / den, o)
                lse = jnp.where(lm, m + jnp.log(den), lse)
            _scatter_classes(oc.at[w], o, dil, nt, tq, add=False)
            _scatter_classes(lc.at[w], lse, dil, nt, tq, add=False)
        mx = lc[0]
        for w in range(1, n_win):
            mx = jnp.maximum(mx, lc[w])
        tot = jnp.zeros_like(mx)
        o = jnp.zeros_like(mx)
        for w in range(n_win):
            e = jnp.exp(lc[w] - mx)
            tot = tot + e
            o = o + e * oc[w]
        o = o / tot
        o_ref[...] = o
        l_ref[...] = mx + jnp.log(tot)
        ga = ga_ref[...]
        ag_ref[...] = (o * ga * _sig(ga)).astype(BF16)

    blk = pl.BlockSpec((seq, LANES), lambda b, hp: (b, hp))
    s = jax.ShapeDtypeStruct
    return pl.pallas_call(
        body, name="attn_fwd", grid=(nb, ATTN_W // LANES),
        out_shape=[s((t_tok, ATTN_W), F32), s((t_tok, ATTN_W), F32), s((t_tok, ATTN_W), BF16)],
        in_specs=[_zblock(seq, 0), _zblock(seq, 1), _zblock(seq, 2), _zblock(seq, 3), _full(gq2.shape),
                  _full(gk2.shape)],
        out_specs=[blk, blk, blk],
        scratch_shapes=[pltpu.VMEM((seq, LANES), F32)] * 2 + [pltpu.VMEM((n_win, seq, LANES), F32)] * 2,
        compiler_params=_params(2, VMEM_LIMIT_V7X),
    )(z, z, z, z, gq2, gk2)


SCAN_COLS = 512


def _to_segments(dst_ref, val):
    seg = val.shape[0] // SUBLANES
    for n in range(dst_ref.shape[0]):
        for s in range(SUBLANES):
            dst_ref[n, pl.ds(s, seg, stride=SUBLANES), :] = val[s * seg:(s + 1) * seg, n * LANES:(n + 1) * LANES]


def _from_segments(src_ref):
    seg = src_ref.shape[1] // SUBLANES
    return jnp.concatenate(
        [jnp.concatenate([src_ref[n, pl.ds(s, seg, stride=SUBLANES), :] for s in range(SUBLANES)], axis=0)
         for n in range(src_ref.shape[0])], axis=1)


def _scan_chunk(re_ref, im_ref, a_re_ref, a_im_ref, carry_re, carry_im, rows, reverse, visit=None):
    seg = rows // SUBLANES
    assert seg & (seg - 1) == 0
    rowi = lax.broadcasted_iota(jnp.int32, (SUBLANES, SCAN_COLS), 0)
    edge = (SUBLANES - 1) if reverse else 0
    last = 0 if reverse else SUBLANES - 1
    at_edge = rowi == edge

    def cmul(ar, ai, br, bi):
        return ar * br - ai * bi, ar * bi + ai * br

    for c0 in range(0, N_STATE, SCAN_COLS):
        cols = slice(c0, c0 + SCAN_COLS)
        a1r = jnp.broadcast_to(a_re_ref[:, cols], (SUBLANES, SCAN_COLS))
        a1i = jnp.broadcast_to(a_im_ref[:, cols], (SUBLANES, SCAN_COLS))
        if reverse:
            a1i = -a1i

        def block_of(i):
            j = (seg - 1 - i) if reverse else i
            return j, pl.ds(pl.multiple_of(j * SUBLANES, SUBLANES), SUBLANES)

        def local(i, carry, cols=cols, a1r=a1r, a1i=a1i):
            xr, xi = carry
            _, blk = block_of(i)
            nr, ni = cmul(a1r, a1i, xr, xi)
            xr, xi = nr + re_ref[blk, cols], ni + im_ref[blk, cols]
            re_ref[blk, cols] = xr
            im_ref[blk, cols] = xi
            return xr, xi

        zero = jnp.zeros((SUBLANES, SCAN_COLS), F32)
        er, ei = lax.fori_loop(0, seg, local, (zero, zero))

        pr, pi = a1r, a1i
        for _ in range(seg.bit_length() - 1):
            pr, pi = cmul(pr, pi, pr, pi)
        cr, ci = carry_re[:, cols], carry_im[:, cols]
        inr, ini = cmul(pr, pi, cr, ci)
        er = er + jnp.where(at_edge, inr, 0.0)
        ei = ei + jnp.where(at_edge, ini, 0.0)
        for sft in (1, 2, 4):
            shift, keep = (SUBLANES - sft, rowi < SUBLANES - sft) if reverse else (sft, rowi >= sft)
            rs = jnp.where(keep, pltpu.roll(er, shift, 0), 0.0)
            ims = jnp.where(keep, pltpu.roll(ei, shift, 0), 0.0)
            dr, di = cmul(pr, pi, rs, ims)
            er, ei = er + dr, ei + di
            pr, pi = cmul(pr, pi, pr, pi)
        carry_re[:, cols] = jnp.broadcast_to(er[last:last + 1, :], (SUBLANES, SCAN_COLS))
        carry_im[:, cols] = jnp.broadcast_to(ei[last:last + 1, :], (SUBLANES, SCAN_COLS))
        one = (SUBLANES - 1) if reverse else 1
        kr = jnp.where(at_edge, cr, pltpu.roll(er, one, 0))
        ki = jnp.where(at_edge, ci, pltpu.roll(ei, one, 0))

        def fix(i, carry, cols=cols, a1r=a1r, a1i=a1i):
            kr, ki, acc = carry
            j, blk = block_of(i)
            kr, ki = cmul(a1r, a1i, kr, ki)
            xr, xi = re_ref[blk, cols] + kr, im_ref[blk, cols] + ki
            re_ref[blk, cols] = xr
            im_ref[blk, cols] = xi
            if visit is not None:
                acc = visit(cols, j, xr, xi, acc)
            return kr, ki, acc

        _, _, acc = lax.fori_loop(0, seg, fix, (kr, ki, (zero, zero)))
        if visit is not None:
            visit(cols, None, None, None, acc)


def _ssm_fwd(z, a_re, a_im, bb_re, bb_im, cc_re, cc_im, d_skip, w_glu, b_glu, nb, seq):
    t_tok = nb * seq
    tc = min(256, seq)
    nch = seq // tc
    grp = N_STATE // 4

    def body(u_ref, gs_ref, ar_ref, ai_ref, bbr_ref, bbi_ref, ccr_ref, cci_ref, d_ref, wg_ref, bg_ref,
             xr_ref, xi_ref, y_ref, sg_ref, car_re, car_im, seg_u, seg_y):
        @pl.when(pl.program_id(1) == 0)
        def _():
            car_re[...] = jnp.zeros_like(car_re)
            car_im[...] = jnp.zeros_like(car_im)

        u = u_ref[...]
        _to_segments(seg_u, u)
        for j in range(4):
            uj = seg_u[j].astype(BF16)
            xr_ref[:, j * grp:(j + 1) * grp] = _dot(uj, bbr_ref[j])
            xi_ref[:, j * grp:(j + 1) * grp] = _dot(uj, bbi_ref[j])
        _scan_chunk(xr_ref, xi_ref, ar_ref, ai_ref, car_re, car_im, tc, reverse=False)
        for j in range(4):
            xr = xr_ref[:, j * grp:(j + 1) * grp].astype(BF16)
            xi = xi_ref[:, j * grp:(j + 1) * grp].astype(BF16)
            seg_y[j] = _dot(xr, ccr_ref[j]) - _dot(xi, cci_ref[j])
        y = _from_segments(seg_y) + d_ref[...] * u
        y_ref[...] = y
        yg, _ = _gelu_and_grad(y)
        gl = _dot(yg.astype(BF16), wg_ref[...]) + bg_ref[...]
        gs = gs_ref[...]
        sg_ref[...] = (yg * _sig(gl) * gs * _sig(gs)).astype(BF16)

    umap = lambda b, ch: (b * nch + ch, 4)
    gmap = lambda b, ch: (b * nch + ch, 5)
    row = lambda b, ch: (b * nch + ch, 0)
    s = jax.ShapeDtypeStruct
    consts = [a_re, a_im, bb_re, bb_im, cc_re, cc_im, d_skip, w_glu, b_glu]
    return pl.pallas_call(
        body, name="ssm_fwd", grid=(nb, nch),
        out_shape=[s((t_tok, N_STATE), F32), s((t_tok, N_STATE), F32), s((t_tok, SSM_W), F32),
                   s((t_tok, SSM_W), BF16)],
        in_specs=[pl.BlockSpec((tc, SSM_W), umap), pl.BlockSpec((tc, SSM_W), gmap)] + [_full(c.shape) for c in consts],
        out_specs=[pl.BlockSpec((tc, N_STATE), row), pl.BlockSpec((tc, N_STATE), row),
                   pl.BlockSpec((tc, SSM_W), row), pl.BlockSpec((tc, SSM_W), row)],
        scratch_shapes=[pltpu.VMEM((SUBLANES, N_STATE), F32), pltpu.VMEM((SUBLANES, N_STATE), F32),
                        pltpu.VMEM((4, tc, LANES), F32), pltpu.VMEM((4, tc, LANES), F32)],
        compiler_params=_params(2, VMEM_LIMIT_V7X),
    )(z, z, *consts)


def _tail(x2, tg2, ag, sg, p2, w_out, w_g, w_p, g_ple):
    t_tok = x2.shape[0]
    tm = min(256, t_tok)
    nt = t_tok // tm
    half = ATTN_W

    def body(x_ref, tg_ref, ag_ref, sg_ref, p_ref, wo_ref, wg_ref, wp_ref, gp_ref,
             dmix_ref, dh1_ref, loss_ref, dgp_ref, dwo_ref, dwg_ref, dwp_ref, acc_o, acc_g, acc_p):
        i = pl.program_id(0)

        @pl.when(i == 0)
        def _():
            loss_ref[...] = jnp.zeros_like(loss_ref)
            dgp_ref[...] = jnp.zeros_like(dgp_ref)
            acc_o[...] = jnp.zeros_like(acc_o)
            acc_g[...] = jnp.zeros_like(acc_g)
            acc_p[...] = jnp.zeros_like(acc_p)

        ag_t, sg_t = ag_ref[...], sg_ref[...]
        h1 = x_ref[...] + _dot(ag_t, wo_ref[0:half, :]) + _dot(sg_t, wo_ref[half:2 * half, :])
        r2 = lax.rsqrt(jnp.mean(h1 * h1, axis=-1, keepdims=True) + EPS)
        hnorm = h1 * r2
        gp = gp_ref[...]
        hn = (hnorm * gp).astype(BF16)
        gate = _sig(_dot(hn, wg_ref[...]))
        pb = p_ref[...].astype(BF16)
        pp = jnp.concatenate([_dot(pb, wp_ref[j]) for j in range(N_DEV)], axis=-1)
        h2 = h1 + gate * pp
        err = h2 - tg_ref[...]
        loss_ref[...] += 0.5 * jnp.sum(err * err) * (1.0 / D_MODEL)
        dh2 = err * (1.0 / D_MODEL)
        dpp = (dh2 * gate).astype(BF16)
        dgpre = (dh2 * pp * gate * (1.0 - gate)).astype(BF16)
        acc_p[...] += _dot_tn(pb, dpp)
        acc_g[...] += _dot_tn(hn, dgpre)
        dhn = _dot_nt(dgpre, wg_ref[...])
        dgp_ref[...] += jnp.sum(dhn * hnorm, axis=0, keepdims=True)
        a = dhn * gp
        dh1 = dh2 + r2 * (a - hnorm * jnp.mean(a * hnorm, axis=-1, keepdims=True))
        dh1_ref[...] = dh1
        dh1b = dh1.astype(BF16)
        acc_o[0:half, :] += _dot_tn(ag_t, dh1b)
        acc_o[half:2 * half, :] += _dot_tn(sg_t, dh1b)
        dmix_ref[...] = _dot_nt(dh1b, wo_ref[...])

        @pl.when(i == nt - 1)
        def _():
            dwo_ref[...] = acc_o[...].astype(BF16)
            dwg_ref[...] = acc_g[...].astype(BF16)
            for j in range(N_DEV):
                dwp_ref[j] = acc_p[:, j * LANES:(j + 1) * LANES].astype(BF16)

    row = lambda i: (i, 0)
    s = jax.ShapeDtypeStruct
    return pl.pallas_call(
        body, name="tail_fwd_bwd", grid=(nt,),
        out_shape=[s((t_tok, D_MODEL), F32), s((t_tok, D_MODEL), F32), s((SUBLANES, LANES), F32),
                   s((1, D_MODEL), F32), s((D_MODEL, D_MODEL), BF16), s((D_MODEL, D_MODEL), BF16),
                   s((N_DEV, PLE_DIM, LANES), BF16)],
        in_specs=[pl.BlockSpec((tm, D_MODEL), row), pl.BlockSpec((tm, D_MODEL), row),
                  pl.BlockSpec((tm, half), row), pl.BlockSpec((tm, half), row), pl.BlockSpec((tm, PLE_DIM), row),
                  _full(w_out.shape), _full(w_g.shape), _full(w_p.shape), _full(g_ple.shape)],
        out_specs=[pl.BlockSpec((tm, D_MODEL), row), pl.BlockSpec((tm, D_MODEL), row), _full((SUBLANES, LANES)),
                   _full((1, D_MODEL)), _full((D_MODEL, D_MODEL)), _full((D_MODEL, D_MODEL)),
                   _full((N_DEV, PLE_DIM, LANES))],
        scratch_shapes=[pltpu.VMEM((D_MODEL, D_MODEL), F32), pltpu.VMEM((D_MODEL, D_MODEL), F32),
                        pltpu.VMEM((PLE_DIM, D_MODEL), F32)],
        compiler_params=_params(1, VMEM_LIMIT_V7X),
    )(x2, tg2, ag, sg, p2, w_out, w_g, w_p, g_ple)


def _attn_bwd(z, gq2, gk2, o, lse, dmix, nb, seq, parts):
    t_tok = nb * seq
    n_rs = len(parts)
    rs = _ReduceScatter([p.shape for p in parts])
    n_steps = (nb, ATTN_W // LANES)

    def body(*refs):
        (q_ref, k_ref, v_ref, ga_ref, gq_ref, gk_ref, o_ref, l_ref, da_ref), refs = refs[:9], refs[9:]
        part_refs, refs = refs[:n_rs], refs[n_rs:]
        (dq_ref, dk_ref, dv_ref, dga_ref), refs = refs[:4], refs[4:]
        g_refs, refs = refs[:n_rs], refs[n_rs:]
        (qf, kf, dof, dlf), rs_scratch = refs[:4], refs[4:]
        b, hp = pl.program_id(0), pl.program_id(1)

        @pl.when((b == 0) & (hp == 0))
        def _():
            rs.start(part_refs, rs_scratch)

        ga, o_t, da = ga_ref[...], o_ref[...], da_ref[...]
        sga = _sig(ga)
        d_o = da * ga * sga
        dga_ref[...] = da * o_t * sga * (1.0 + ga * (1.0 - sga))
        lane = lax.broadcasted_iota(jnp.int32, (1, LANES), 1)
        d_oo = d_o * o_t
        delta = jnp.zeros_like(d_oo)
        for h in range(LANES // HEAD_DIM):
            lm2 = (lane // HEAD_DIM) == h
            delta = jnp.where(lm2, jnp.sum(jnp.where(lm2, d_oo, 0.0), axis=-1, keepdims=True), delta)
        qf[...], kf[...] = _qk_hat(q_ref, k_ref, gq_ref, gk_ref)
        dof[...] = d_o
        dlf[...] = delta
        dq_ref[...] = jnp.zeros_like(dq_ref)
        dk_ref[...] = jnp.zeros_like(dk_ref)
        dv_ref[...] = jnp.zeros_like(dv_ref)
        lms = _head_masks()
        for window, dil in DILATED:
            nt, tq = _window_tiling(seq, window, dil)
            q3 = _gather_classes(qf, dil, nt, tq, BF16)
            k3 = _gather_classes(kf, dil, nt, tq, BF16)
            v3 = _gather_classes(v_ref, dil, nt, tq, BF16)
            do3 = _gather_classes(dof, dil, nt, tq, BF16)
            lt3 = _gather_classes(l_ref, dil, nt, tq, F32)
            dl3 = _gather_classes(dlf, dil, nt, tq, F32)
            if nt > 1:
                k3, v3 = _with_prev_tile(k3, dil, nt), _with_prev_tile(v3, dil, nt)
            valid = _band_valid(dil, nt, tq)
            dq = jnp.zeros(q3.shape, F32)
            dk = jnp.zeros(k3.shape, F32)
            dv = jnp.zeros(k3.shape, F32)
            for lm in lms:
                qm = jnp.where(lm, q3, jnp.zeros_like(q3))
                dom = jnp.where(lm, do3, jnp.zeros_like(do3))
                p = jnp.where(valid, jnp.exp(_bqk(qm, k3) - _head_col(lt3, lm)), 0.0)
                dv = dv + _bkd(p.astype(BF16), dom)
                ds = (p * (_bqk(dom, v3) - _head_col(dl3, lm))).astype(BF16)
                dq = dq + jnp.where(lm, _bqd(ds, k3), 0.0)
                dk = dk + _bkd(ds, qm)
            _scatter_classes(dq_ref, dq, dil, nt, tq, add=True)
            for ref, g in ((dk_ref, dk), (dv_ref, dv)):
                if nt > 1:
                    own, prev = g[:, tq:, :], g[:, :tq, :]
                    shifted = []
                    for r in range(dil):
                        t = prev[r * nt:(r + 1) * nt]
                        shifted.append(jnp.concatenate([t[1:], jnp.zeros_like(t[:1])], axis=0))
                    g = own + (shifted[0] if dil == 1 else jnp.concatenate(shifted, axis=0))
                _scatter_classes(ref, g, dil, nt, tq, add=True)

        @pl.when((b == n_steps[0] - 1) & (hp == n_steps[1] - 1))
        def _():
            rs.finish(part_refs, rs_scratch, g_refs)

    blk = pl.BlockSpec((seq, LANES), lambda b, hp: (b, hp))
    s = jax.ShapeDtypeStruct
    outs = pl.pallas_call(
        body, name="attn_bwd", grid=n_steps,
        out_shape=[s((t_tok, ATTN_W), F32)] * 4 + [s(p.shape[1:], F32) for p in parts],
        in_specs=[_zblock(seq, 0), _zblock(seq, 1), _zblock(seq, 2), _zblock(seq, 3), _full(gq2.shape),
                  _full(gk2.shape), blk, blk, blk] + [pl.BlockSpec(memory_space=pl.ANY)] * n_rs,
        out_specs=[blk] * 4 + [_full(p.shape[1:]) for p in parts],
        scratch_shapes=[pltpu.VMEM((seq, LANES), F32)] * 4 + rs.scratch(parts[0].dtype),
        compiler_params=_params(2, VMEM_LIMIT_V7X),
    )(z, z, z, z, gq2, gk2, o, lse, dmix, *parts)
    return outs[:4], outs[4:]


def _ssm_bwd(z, dmix, y, x_re, x_im, a_re, a_im, bb_re, bb_im, cc_re, cc_im, d_skip, w_glu, b_glu, nb, seq):
    t_tok = nb * seq
    tc = min(256, seq)
    nch = seq // tc
    grp = N_STATE // 4

    def body(u_ref, gs_ref, ds_ref, y_ref, xr_ref, xi_ref, xpr_ref, xpi_ref,
             ar_ref, ai_ref, bbr_ref, bbi_ref, ccr_ref, cci_ref, d_ref, wg_ref, bg_ref,
             du_ref, dgs_ref, dwg_ref, dbg_ref, dd_ref, dar_ref, dai_ref, dbbr_ref, dbbi_ref, dccr_ref, dcci_ref,
             lam_re, lam_im, car_re, car_im, acc_wg, seg_a, seg_b, ent_re, ent_im):
        step = pl.program_id(1)
        first_chunk = step == nch - 1

        @pl.when((pl.program_id(0) == 0) & (step == 0))
        def _():
            acc_wg[...] = jnp.zeros_like(acc_wg)
            for ref in (dbg_ref, dd_ref, dar_ref, dai_ref, dbbr_ref, dbbi_ref, dccr_ref, dcci_ref):
                ref[...] = jnp.zeros_like(ref)

        @pl.when(step == 0)
        def _():
            car_re[...] = jnp.zeros_like(car_re)
            car_im[...] = jnp.zeros_like(car_im)

        u, gs, dssm, y = u_ref[...], gs_ref[...], ds_ref[...], y_ref[...]
        yg, dgelu = _gelu_and_grad(y)
        ygb = yg.astype(BF16)
        sgl = _sig(_dot(ygb, wg_ref[...]) + bg_ref[...])
        sgs = _sig(gs)
        dout = dssm * gs * sgs
        dgs_ref[...] = dssm * yg * sgl * sgs * (1.0 + gs * (1.0 - sgs))
        dgl = dout * yg * sgl * (1.0 - sgl)
        dglb = dgl.astype(BF16)
        dyg = dout * sgl + _dot_nt(dglb, wg_ref[...])
        acc_wg[...] += _dot_tn(ygb, dglb)
        dbg_ref[...] += jnp.sum(dgl, axis=0, keepdims=True)
        dy = dyg * dgelu
        dd_ref[...] += jnp.sum(dy * u, axis=0, keepdims=True)
        _to_segments(seg_a, dy)
        _to_segments(seg_b, u)
        for j in range(4):
            dyj = seg_a[j].astype(BF16)
            sl = slice(j * grp, (j + 1) * grp)
            lam_re[:, sl] = _dot_nt(dyj, ccr_ref[j])
            lam_im[:, sl] = -_dot_nt(dyj, cci_ref[j])
            dccr_ref[j] += _dot_tn(xr_ref[:, sl].astype(BF16), dyj)
            dcci_ref[j] -= _dot_tn(xi_ref[:, sl].astype(BF16), dyj)

        keep_prev = jnp.where(first_chunk, 0.0, 1.0)
        seg = tc // SUBLANES
        last_blk = pl.ds((seg - 1) * SUBLANES, SUBLANES)
        row0 = lax.broadcasted_iota(jnp.int32, (SUBLANES, N_STATE), 0) == 0
        for src, prev, dst in ((xr_ref, xpr_ref, ent_re), (xi_ref, xpi_ref, ent_im)):
            before = jnp.broadcast_to(prev[SUBLANES - 1:SUBLANES, :] * keep_prev, (SUBLANES, N_STATE))
            dst[...] = jnp.where(row0, before, pltpu.roll(src[last_blk, :], 1, 0))

        def visit(cols, j, lr, li, acc):
            if j is None:
                dar_ref[:, cols] += jnp.sum(acc[0], axis=0, keepdims=True)
                dai_ref[:, cols] += jnp.sum(acc[1], axis=0, keepdims=True)
                return None
            blk = pl.ds(pl.multiple_of(jnp.maximum(j - 1, 0) * SUBLANES, SUBLANES), SUBLANES)
            inside = j > 0
            xpr = jnp.where(inside, xr_ref[blk, cols], ent_re[:, cols])
            xpi = jnp.where(inside, xi_ref[blk, cols], ent_im[:, cols])
            return acc[0] + lr * xpr + li * xpi, acc[1] + li * xpr - lr * xpi

        _scan_chunk(lam_re, lam_im, ar_ref, ai_ref, car_re, car_im, tc, reverse=True, visit=visit)

        for j in range(4):
            sl = slice(j * grp, (j + 1) * grp)
            lr = lam_re[:, sl].astype(BF16)
            li = lam_im[:, sl].astype(BF16)
            uj = seg_b[j].astype(BF16)
            seg_a[j] = _dot_nt(lr, bbr_ref[j]) + _dot_nt(li, bbi_ref[j])
            dbbr_ref[j] += _dot_tn(uj, lr)
            dbbi_ref[j] += _dot_tn(uj, li)
        du_ref[...] = _from_segments(seg_a) + dy * d_ref[...]

        @pl.when((pl.program_id(0) == nb - 1) & (step == nch - 1))
        def _():
            dwg_ref[...] = acc_wg[...].astype(BF16)

    rev = lambda b, ch: b * nch + (nch - 1 - ch)
    umap = lambda b, ch: (rev(b, ch), 4)
    gmap = lambda b, ch: (rev(b, ch), 5)
    smap = lambda b, ch: (rev(b, ch), 1)
    row = lambda b, ch: (rev(b, ch), 0)
    prev = lambda b, ch: (jnp.maximum(rev(b, ch) * (tc // SUBLANES) - 1, 0), 0)
    s = jax.ShapeDtypeStruct
    consts = [a_re, a_im, bb_re, bb_im, cc_re, cc_im, d_skip, w_glu, b_glu]
    acc_shapes = [s((1, SSM_W), F32), s((1, SSM_W), F32), s((1, N_STATE), F32), s((1, N_STATE), F32),
                  s(bb_re.shape, F32), s(bb_re.shape, F32), s(cc_re.shape, F32), s(cc_re.shape, F32)]
    return pl.pallas_call(
        body, name="ssm_bwd", grid=(nb, nch),
        out_shape=[s((t_tok, SSM_W), F32), s((t_tok, SSM_W), F32), s((SSM_W, SSM_W), BF16)] + acc_shapes,
        in_specs=[pl.BlockSpec((tc, SSM_W), umap), pl.BlockSpec((tc, SSM_W), gmap), pl.BlockSpec((tc, SSM_W), smap),
                  pl.BlockSpec((tc, SSM_W), row), pl.BlockSpec((tc, N_STATE), row), pl.BlockSpec((tc, N_STATE), row),
                  pl.BlockSpec((SUBLANES, N_STATE), prev), pl.BlockSpec((SUBLANES, N_STATE), prev)]
        + [_full(c.shape) for c in consts],
        out_specs=[pl.BlockSpec((tc, SSM_W), row), pl.BlockSpec((tc, SSM_W), row), _full((SSM_W, SSM_W))]
        + [_full(a.shape) for a in acc_shapes],
        scratch_shapes=[pltpu.VMEM((tc, N_STATE), F32), pltpu.VMEM((tc, N_STATE), F32),
                        pltpu.VMEM((SUBLANES, N_STATE), F32), pltpu.VMEM((SUBLANES, N_STATE), F32),
                        pltpu.VMEM((SSM_W, SSM_W), F32), pltpu.VMEM((4, tc, LANES), F32),
                        pltpu.VMEM((4, tc, LANES), F32),
                        pltpu.VMEM((SUBLANES, N_STATE), F32), pltpu.VMEM((SUBLANES, N_STATE), F32)],
        compiler_params=_params(2, VMEM_LIMIT_V7X),
    )(z, z, dmix, y, x_re, x_im, x_re, x_im, *consts)


def _dz_and_dx(x2, z, dqh, dkh, dvb, dga, du, dgs, dh1, w_in_g, g_mix, gq_t, gk_t, ones_bd, fold):
    t_tok = x2.shape[0]
    tm = min(256, t_tok)
    nt = t_tok // tm
    a_w = ATTN_W

    def head_norm_bwd(raw, d_hat, gain, scale, ones):
        r = lax.rsqrt(_hdot(raw * raw, ones) * (1.0 / HEAD_DIM) + EPS)
        n = raw * r
        a = d_hat * gain * scale
        d_raw = r * (a - n * (_hdot(a * n, ones) * (1.0 / HEAD_DIM)))
        return d_raw, jnp.sum(d_hat * n * scale, axis=0, keepdims=True)

    def body(x_ref, q_ref, k_ref, dq_ref, dk_ref, dv_ref, dga_ref, du_ref, dgs_ref, dh1_ref, w_ref, g_ref,
             gq_ref, gk_ref, ones_ref, fold_ref, dz_ref, gx_ref, dgm_ref, dgq_ref, dgk_ref, acc_q, acc_k):
        i = pl.program_id(0)

        @pl.when(i == 0)
        def _():
            dgm_ref[...] = jnp.zeros_like(dgm_ref)
            acc_q[...] = jnp.zeros_like(acc_q)
            acc_k[...] = jnp.zeros_like(acc_k)

        ones = ones_ref[...]
        dq, sq = head_norm_bwd(q_ref[...], dq_ref[...], gq_ref[...], HEAD_DIM ** -0.5, ones)
        dk, sk = head_norm_bwd(k_ref[...], dk_ref[...], gk_ref[...], 1.0, ones)
        acc_q[...] += jnp.broadcast_to(sq, acc_q.shape)
        acc_k[...] += jnp.broadcast_to(sk, acc_k.shape)
        parts = (dq, dk, dv_ref[...], dga_ref[...], du_ref[...], dgs_ref[...])
        for n, part in enumerate(parts):
            dz_ref[:, n * a_w:(n + 1) * a_w] = part.astype(BF16)
        dxn = jnp.zeros((tm, D_MODEL), F32)
        for j in range(N_DEV):
            dxn = dxn + _dot_nt(dz_ref[:, j * COL_W:(j + 1) * COL_W], w_ref[j])
        x = x_ref[...]
        r1 = lax.rsqrt(jnp.mean(x * x, axis=-1, keepdims=True) + EPS)
        xnorm = x * r1
        dgm_ref[...] += jnp.sum(dxn * xnorm, axis=0, keepdims=True)
        a = dxn * g_ref[...]
        gx_ref[...] = dh1_ref[...] + r1 * (a - xnorm * jnp.mean(a * xnorm, axis=-1, keepdims=True))

        @pl.when(i == nt - 1)
        def _():
            dgq_ref[...] = _hdot(acc_q[...], fold_ref[...])
            dgk_ref[...] = _hdot(acc_k[...], fold_ref[...])

    row = lambda i: (i, 0)
    col = lambda n: (lambda i: (i, n))
    s = jax.ShapeDtypeStruct
    half = pl.BlockSpec((tm, a_w), row)
    return pl.pallas_call(
        body, name="dz_dx", grid=(nt,),
        out_shape=[s((t_tok, IN_W), BF16), s((t_tok, D_MODEL), F32), s((1, D_MODEL), F32),
                   s((SUBLANES, HEAD_DIM), F32), s((SUBLANES, HEAD_DIM), F32)],
        in_specs=[pl.BlockSpec((tm, D_MODEL), row), pl.BlockSpec((tm, a_w), col(0)), pl.BlockSpec((tm, a_w), col(1)),
                  half, half, half, half, half, half, pl.BlockSpec((tm, D_MODEL), row),
                  _full(w_in_g.shape), _full(g_mix.shape), _full(gq_t.shape), _full(gk_t.shape),
                  _full(ones_bd.shape), _full(fold.shape)],
        out_specs=[pl.BlockSpec((tm, IN_W), row), pl.BlockSpec((tm, D_MODEL), row), _full((1, D_MODEL)),
                   _full((SUBLANES, HEAD_DIM)), _full((SUBLANES, HEAD_DIM))],
        scratch_shapes=[pltpu.VMEM((SUBLANES, a_w), F32), pltpu.VMEM((SUBLANES, a_w), F32)],
        compiler_params=_params(1, VMEM_LIMIT_V7X),
    )(x2, z, z, dqh, dkh, dvb, dga, du, dgs, dh1, w_in_g, g_mix, gq_t, gk_t, ones_bd, fold)


def _dw_in(xn, dz, glu_parts, small):
    t_tok = xn.shape[0]
    tk = min(1024, t_tok)
    nk = t_tok // tk
    rs = _ReduceScatter([glu_parts.shape])
    ag = _AllGather(1, cast=False)
    n_rs = len(rs.scratch(BF16))

    def my_index():
        return 4 * lax.axis_index("x") + 2 * lax.axis_index("y") + lax.axis_index("c")

    def target(i):
        return (my_index() + 1 + i) % N_DEV

    def body(xn_ref, dz_ref, glu_ref, small_ref, gin_ref, gglu_ref, gath_ref, acc, stage, land, send_sems, recv_sems,
             *rest):
        rs_scratch, ag_sems = rest[:n_rs], rest[n_rs:]
        i, k = pl.program_id(0), pl.program_id(1)

        def push(step):
            j = target(step)
            return pltpu.make_async_remote_copy(
                src_ref=stage.at[step], dst_ref=land.at[step], send_sem=send_sems.at[step],
                recv_sem=recv_sems.at[step], device_id=(j // 4, (j // 2) % 2, j % 2), device_id_type=MESH)

        @pl.when((i == 0) & (k == 0))
        def _():
            rs.start([glu_ref], rs_scratch)
            ag.start([small_ref], [gath_ref], ag_sems)

        @pl.when((i == N_DEV // 2) & (k == 0))
        def _():
            ag.forward([small_ref], [gath_ref], ag_sems)

        @pl.when(k == 0)
        def _():
            acc[...] = jnp.zeros_like(acc)

        acc[...] += _dot_tn(xn_ref[...], dz_ref[...])

        @pl.when((k == nk - 1) & (i < N_DEV - 1))
        def _():
            stage[i] = acc[...].astype(BF16)
            push(i).start()

        @pl.when((k == nk - 1) & (i == N_DEV - 1))
        def _():
            for step in range(N_DEV - 1):
                push(step).wait_recv()
            chunk, steps = _row_chunks(D_MODEL)

            def add(s, carry):
                r = pl.ds(pl.multiple_of(s * chunk, chunk), chunk)
                total = acc[r, :]
                for step in range(N_DEV - 1):
                    total = total + land[step, r, :].astype(F32)
                gin_ref[r, :] = total
                return carry

            lax.fori_loop(0, steps, add, 0)
            for step in range(N_DEV - 1):
                push(step).wait_send()
            rs.finish([glu_ref], rs_scratch, [gglu_ref])
            ag.finish([small_ref], [gath_ref], ag_sems)

    any_spec = pl.BlockSpec(memory_space=pl.ANY)
    s = jax.ShapeDtypeStruct
    return pl.pallas_call(
        body, name="dw_in", grid=(N_DEV, nk),
        out_shape=[s((D_MODEL, COL_W), F32), s(glu_parts.shape[1:], F32), s((N_DEV,) + small.shape, F32)],
        in_specs=[pl.BlockSpec((tk, D_MODEL), lambda i, k: (k, 0)),
                  pl.BlockSpec((tk, COL_W), lambda i, k: (k, target(i))), any_spec, any_spec],
        out_specs=[_full((D_MODEL, COL_W)), _full(glu_parts.shape[1:]), any_spec],
        scratch_shapes=[pltpu.VMEM((D_MODEL, COL_W), F32), pltpu.VMEM((N_DEV - 1, D_MODEL, COL_W), BF16),
                        pltpu.VMEM((N_DEV - 1, D_MODEL, COL_W), BF16), pltpu.SemaphoreType.DMA((N_DEV - 1,)),
                        pltpu.SemaphoreType.DMA((N_DEV - 1,))] + rs.scratch(BF16) + ag.scratch(),
        compiler_params=_params(2, VMEM_LIMIT_V7X),
    )(xn, dz, glu_parts, small)


SMALL = ("mix_norm", "q_norm", "k_norm", "lambda_re", "lambda_im", "log_dt", "b_re", "b_im", "c_re", "c_im",
         "d_skip", "b_glu", "ple_norm")
BIG = ("w_in", "w_glu", "w_out", "w_ple_gate", "w_ple_proj")
WEIGHTS = ("mix_norm", "w_in", "q_norm", "k_norm", "lambda_re", "lambda_im", "log_dt", "b_re", "b_im", "c_re",
           "c_im", "d_skip", "w_glu", "b_glu", "w_out", "ple_norm", "w_ple_gate", "w_ple_proj")


def _pack(arrs):
    flat = jnp.concatenate([a.reshape(-1).astype(F32) for a in arrs])
    rows = -(-flat.shape[0] // (64 * LANES)) * 64
    return jnp.pad(flat, (0, rows * LANES - flat.shape[0])).reshape(rows, LANES)


def _unpack(packed, shapes):
    flat = packed.reshape(-1)
    out, off = [], 0
    for shp in shapes:
        size = math.prod(shp)
        out.append(flat[off:off + size].reshape(shp))
        off += size
    return out


def kernel(x, p, mix_norm, w_in, q_norm, k_norm, lambda_re, lambda_im, log_dt, b_re, b_im, c_re, c_im, d_skip, w_glu, b_glu, w_out, ple_norm, w_ple_gate, w_ple_proj, loss_target, m_mix_norm, m_w_in, m_q_norm, m_k_norm, m_lambda_re, m_lambda_im, m_log_dt, m_b_re, m_b_im, m_c_re, m_c_im, m_d_skip, m_w_glu, m_b_glu, m_w_out, m_ple_norm, m_w_ple_gate, m_w_ple_proj, v_mix_norm, v_w_in, v_q_norm, v_k_norm, v_lambda_re, v_lambda_im, v_log_dt, v_b_re, v_b_im, v_c_re, v_c_im, v_d_skip, v_w_glu, v_b_glu, v_w_out, v_ple_norm, v_w_ple_gate, v_w_ple_proj):
    env = dict(locals())
    w = {n: env[n] for n in WEIGHTS}
    m = {n: env["m_" + n] for n in WEIGHTS}
    v = {n: env["v_" + n] for n in WEIGHTS}
    nb, seq, _ = x.shape
    t_tok = nb * seq
    x2 = x.reshape(t_tok, D_MODEL)
    tg2 = loss_target.reshape(t_tok, D_MODEL)
    p2 = p.reshape(t_tok, PLE_DIM)

    shard2d = {"w_in": (D_MODEL, COL_W), "w_glu": (SSM_W // N_DEV, SSM_W), "w_out": (D_MODEL // N_DEV, D_MODEL),
               "w_ple_gate": (D_MODEL // N_DEV, D_MODEL), "w_ple_proj": (PLE_DIM, D_MODEL // N_DEV)}
    w_sh = [w[n].reshape(shard2d[n]) for n in BIG]

    g3 = (SSM_GROUPS, 1, SSM_STATE)
    lr3, li3 = lambda_re.reshape(g3), lambda_im.reshape(g3)
    dt3 = log_dt.reshape(SSM_GROUPS, 1, 1)
    btr = b_re[0].transpose(0, 2, 1)
    bti = b_im[0].transpose(0, 2, 1)
    a_re3, a_im3, bbr, bbi = _zoh_fwd(lr3, li3, dt3, btr, bti)
    a_re, a_im = a_re3.reshape(1, N_STATE), a_im3.reshape(1, N_STATE)
    bb_re, bb_im = _blockdiag(bbr).astype(BF16), _blockdiag(bbi).astype(BF16)
    cc_re = _blockdiag(c_re[0].transpose(0, 2, 1)).astype(BF16)
    cc_im = _blockdiag(c_im[0].transpose(0, 2, 1)).astype(BF16)

    ones_bd = _head_ones()
    fold = jnp.tile(jnp.eye(HEAD_DIM, dtype=F32), (ATTN_W // HEAD_DIM, 1))
    gq_t = jnp.tile(q_norm, (1, ATTN_W // HEAD_DIM))
    gk_t = jnp.tile(k_norm, (1, ATTN_W // HEAD_DIM))

    gq2 = jnp.tile(q_norm, (1, LANES // HEAD_DIM))
    gk2 = jnp.tile(k_norm, (1, LANES // HEAD_DIM))

    z, xn, w_in_g, w_glu_g, w_out_g, w_g_g, w_p_g = _in_proj(x2, mix_norm, w_sh[0], w_sh[1:])
    w_glu_f = w_glu_g.reshape(SSM_W, SSM_W)
    w_out_f = w_out_g.reshape(D_MODEL, D_MODEL)
    w_g_f = w_g_g.reshape(D_MODEL, D_MODEL)
    o, lse, ag = _attn_fwd(z, gq2, gk2, nb, seq)
    x_re, x_im, y, sg = _ssm_fwd(z, a_re, a_im, bb_re, bb_im, cc_re, cc_im, d_skip, w_glu_f, b_glu, nb, seq)
    dmix, dh1, loss_t, d_ple, dw_out, dw_g, dw_p = _tail(x2, tg2, ag, sg, p2, w_out_f, w_g_f, w_p_g, ple_norm)

    early_parts = [dw_out.reshape(N_DEV, D_MODEL // N_DEV, D_MODEL), dw_g.reshape(N_DEV, D_MODEL // N_DEV, D_MODEL),
                   dw_p]
    (dqh, dkh, dvb, dga), (g_out, g_g, g_p) = _attn_bwd(z, gq2, gk2, o, lse, dmix, nb, seq, early_parts)
    (du, dgs, dw_glu, d_bglu, d_dskip, da_re, da_im, dbb_re, dbb_im, dcc_re, dcc_im) = _ssm_bwd(
        z, dmix, y, x_re, x_im, a_re, a_im, bb_re, bb_im, cc_re, cc_im, d_skip, w_glu_f, b_glu, nb, seq)
    dz, gx, d_mix, d_gq, d_gk = _dz_and_dx(x2, z, dqh, dkh, dvb, dga, du, dgs, dh1, w_in_g, mix_norm, gq_t, gk_t,
                                           ones_bd, fold)
    d_lr, d_li, d_dt, d_btr, d_bti = _zoh_bwd(
        lr3, li3, dt3, btr, bti, da_re.reshape(g3), da_im.reshape(g3),
        _blockdiag_extract(dbb_re, SSM_GROUP, SSM_STATE), _blockdiag_extract(dbb_im, SSM_GROUP, SSM_STATE))
    small_g = {
        "mix_norm": d_mix, "q_norm": d_gq[0:1], "k_norm": d_gk[0:1], "lambda_re": d_lr, "lambda_im": d_li,
        "log_dt": d_dt, "b_re": d_btr.transpose(0, 2, 1), "b_im": d_bti.transpose(0, 2, 1),
        "c_re": _blockdiag_extract(dcc_re, SSM_STATE, SSM_GROUP).transpose(0, 2, 1),
        "c_im": _blockdiag_extract(dcc_im, SSM_STATE, SSM_GROUP).transpose(0, 2, 1),
        "d_skip": d_dskip, "b_glu": d_bglu, "ple_norm": d_ple}

    g_in, g_glu, gathered = _dw_in(xn, dz, dw_glu.reshape(N_DEV, SSM_W // N_DEV, SSM_W),
                                   _pack([small_g[n] for n in SMALL] + [loss_t[0:1, 0:1]]))
    g_sh = [g_in, g_glu, g_out, g_g, g_p]
    d_sh, m_sh, v_sh = _adamw_shards(g_sh, w_sh, [m[n].reshape(shard2d[n]) for n in BIG],
                                     [v[n].reshape(shard2d[n]) for n in BIG])

    g_pk, d_pk, m_pk, v_pk = _small_update(gathered, _pack([w[n] for n in SMALL]), _pack([m[n] for n in SMALL]),
                                           _pack([v[n] for n in SMALL]))

    grads, deltas, new_m, new_v = {}, {}, {}, {}
    small_shapes = [w[n].shape for n in SMALL]
    for dst, packed in ((grads, g_pk), (deltas, d_pk), (new_m, m_pk), (new_v, v_pk)):
        for n, a in zip(SMALL, _unpack(packed, small_shapes)):
            dst[n] = a
    for i, n in enumerate(BIG):
        grads[n] = g_sh[i].reshape(w[n].shape)
        deltas[n] = d_sh[i].reshape(w[n].shape)
        new_m[n] = m_sh[i].reshape(w[n].shape)
        new_v[n] = v_sh[i].reshape(w[n].shape)

    loss = _unpack(g_pk, small_shapes + [()])[-1]
    return (loss, gx.reshape(x.shape), *[grads[n] for n in WEIGHTS], *[deltas[n] for n in WEIGHTS],
            *[new_m[n] for n in WEIGHTS], *[new_v[n] for n in WEIGHTS])
```

```python
import math

import jax
import jax.numpy as jnp
from jax import lax
from jax.experimental import pallas as pl
from jax.experimental.pallas import tpu as pltpu

F32 = jnp.float32
BF16 = jnp.bfloat16
MESH = pl.DeviceIdType.MESH
AXES = ("x", "y", "c")
N_DEV = 8

D_MODEL = 1024
HEAD_DIM = 64
ATTN_W = 512
SSM_W = 512
SSM_GROUPS = 32
SSM_GROUP = 16
SSM_STATE = 64
N_STATE = SSM_GROUPS * SSM_STATE
PLE_DIM = 256
IN_W = 3072
COL_W = IN_W // N_DEV
DILATED = ((128, 1), (512, 4), (2048, 16))
EPS = 1e-6
INV_SQRT2 = 1.0 / math.sqrt(2.0)
INV_SQRT_2PI = 1.0 / math.sqrt(2.0 * math.pi)

ADAM_LR, ADAM_B1, ADAM_B2, ADAM_EPS, ADAM_WD, ADAM_STEP = 0.001, 0.9, 0.999, 1e-08, 0.01, 10

VMEM_LIMIT_V7X = 56 * 1024 * 1024
SUBLANES = 8
LANES = 128


def _params(n_axes=None, vmem=None):
    kw = {}
    if n_axes:
        kw["dimension_semantics"] = ("arbitrary",) * n_axes
    if vmem:
        kw["vmem_limit_bytes"] = vmem
    return pltpu.CompilerParams(**kw)


def _dot(a, b):
    return jnp.dot(a, b, preferred_element_type=F32)


def _dot_nt(a, b):
    return lax.dot_general(a, b, (((1,), (1,)), ((), ())), preferred_element_type=F32)


def _dot_tn(a, b):
    return lax.dot_general(a, b, (((0,), (0,)), ((), ())), preferred_element_type=F32)


def _hdot(a, b):
    return jnp.dot(a, b, precision=lax.Precision.HIGHEST, preferred_element_type=F32)


def _sig(x):
    return 1.0 / (1.0 + jnp.exp(-x))


def _gelu_and_grad(y):
    cdf = 0.5 * (1.0 + lax.erf(y * INV_SQRT2))
    pdf = jnp.exp(-0.5 * y * y) * INV_SQRT_2PI
    return y * cdf, cdf + y * pdf


def _vmem():
    return pl.BlockSpec(memory_space=pltpu.VMEM)


def _full(shape):
    nd = len(shape)
    return pl.BlockSpec(shape, lambda *_: (0,) * nd)


class _AllGather:
    def __init__(self, n, cast):
        self.n, self.cast = n, cast

    def scratch(self):
        n = self.n
        return [pltpu.SemaphoreType.DMA((7 * n,)), pltpu.SemaphoreType.DMA((7 * n,)), pltpu.SemaphoreType.DMA((n,))]

    def _plan(self, src_refs, out_refs, sems):
        send_sems, recv_sems, own_sems = sems
        x, y, c = lax.axis_index("x"), lax.axis_index("y"), lax.axis_index("c")
        me, sibling = (x, y, c), (x, y, 1 - c)
        chips = [(1 - x, y), (x, 1 - y), (1 - x, 1 - y)]

        def idx(px, py, pc):
            return 4 * px + 2 * py + pc

        def copy(i, k, block, to, own_src=False):
            ref = out_refs[i].at[idx(*block)]
            return pltpu.make_async_remote_copy(
                src_ref=src_refs[i] if own_src and not self.cast else ref, dst_ref=ref,
                send_sem=send_sems.at[7 * i + k], recv_sem=recv_sems.at[7 * i + k],
                device_id=to, device_id_type=MESH)

        first, passed, arrive_ici, arrive_d2d, own = [], [], [], [], []
        for i in range(self.n):
            first.append(copy(i, 0, me, sibling, own_src=True))
            first += [copy(i, 1 + j, me, (*chip, c), own_src=True) for j, chip in enumerate(chips)]
            arrive_ici += [copy(i, 1 + j, (*chip, c), me) for j, chip in enumerate(chips)]
            passed += [copy(i, 4 + j, (*chip, c), sibling) for j, chip in enumerate(chips)]
            arrive_d2d.append(copy(i, 0, sibling, me))
            arrive_d2d += [copy(i, 4 + j, (*chip, 1 - c), me) for j, chip in enumerate(chips)]
            if not self.cast:
                own.append(pltpu.make_async_copy(src_refs[i], out_refs[i].at[idx(*me)], own_sems.at[i]))
        return idx(*me), first, passed, arrive_ici, arrive_d2d, own

    def start(self, src_refs, out_refs, sems):
        my, first, _, _, _, own = self._plan(src_refs, out_refs, sems)
        if self.cast:
            for i in range(self.n):
                out_refs[i][my] = src_refs[i][...].astype(out_refs[i].dtype)
        for cp in own + first:
            cp.start()

    def forward(self, src_refs, out_refs, sems):
        _, _, passed, arrive_ici, _, _ = self._plan(src_refs, out_refs, sems)
        for cp in arrive_ici:
            cp.wait_recv()
        for cp in passed:
            cp.start()

    def finish(self, src_refs, out_refs, sems):
        _, first, passed, _, arrive_d2d, own = self._plan(src_refs, out_refs, sems)
        for cp in own:
            cp.wait()
        for cp in arrive_d2d:
            cp.wait_recv()
        for cp in first + passed:
            cp.wait_send()


class _HostedGather:
    def __init__(self, shards):
        self.shapes = [(N_DEV,) + a.shape for a in shards]
        self.n = len(shards)
        self.ag = _AllGather(self.n, cast=True)

    def out_shape(self):
        return [jax.ShapeDtypeStruct(s, BF16) for s in self.shapes]

    def scratch(self):
        return [pltpu.VMEM(s, BF16) for s in self.shapes] + self.ag.scratch() + [pltpu.SemaphoreType.DMA((self.n,))]

    def _split(self, scratch):
        return scratch[:self.n], scratch[self.n:-1], scratch[-1]

    def start(self, src_refs, scratch):
        land, sems, _ = self._split(scratch)
        self.ag.start(src_refs, land, sems)

    def forward(self, src_refs, scratch):
        land, sems, _ = self._split(scratch)
        self.ag.forward(src_refs, land, sems)

    def finish(self, src_refs, scratch, out_refs):
        land, sems, out_sems = self._split(scratch)
        self.ag.finish(src_refs, land, sems)
        outs = [pltpu.make_async_copy(land[n], out_refs[n], out_sems.at[n]) for n in range(self.n)]
        for cp in outs:
            cp.start()
        for cp in outs:
            cp.wait()


def _all_gather(shards, out_dtypes, name):
    n = len(shards)
    ag = _AllGather(n, cast=True)

    def body(*refs):
        in_refs, out_refs, sems = refs[:n], refs[n:2 * n], refs[2 * n:]
        ag.start(in_refs, out_refs, sems)
        ag.forward(in_refs, out_refs, sems)
        ag.finish(in_refs, out_refs, sems)

    return pl.pallas_call(
        body, name=name,
        out_shape=[jax.ShapeDtypeStruct((N_DEV,) + s.shape, dt) for s, dt in zip(shards, out_dtypes)],
        in_specs=[_vmem()] * n, out_specs=[_vmem()] * n,
        scratch_shapes=ag.scratch(),
        compiler_params=_params(vmem=VMEM_LIMIT_V7X),
    )(*shards)


def _row_chunks(rows):
    chunk = 64 if rows % 64 == 0 else rows
    return chunk, rows // chunk


class _ReduceScatter:
    def __init__(self, shapes):
        self.shapes = shapes
        self.n = len(shapes)

    def scratch(self, dtype):
        return ([pltpu.VMEM(s, dtype) for s in self.shapes]
                + [pltpu.SemaphoreType.DMA((7 * self.n,)), pltpu.SemaphoreType.DMA((7 * self.n,)),
                   pltpu.SemaphoreType.DMA((self.n,))])

    def _copies(self, in_refs, land_refs, send_sems, recv_sems, own_sems):
        x, y, c = lax.axis_index("x"), lax.axis_index("y"), lax.axis_index("c")
        remote, own = [], []
        for i in range(self.n):
            for m in range(1, N_DEV):
                px = 1 - x if m & 4 else x
                py = 1 - y if m & 2 else y
                pc = 1 - c if m & 1 else c
                remote.append(pltpu.make_async_remote_copy(
                    src_ref=in_refs[i].at[4 * px + 2 * py + pc], dst_ref=land_refs[i].at[m - 1],
                    send_sem=send_sems.at[7 * i + m - 1], recv_sem=recv_sems.at[7 * i + m - 1],
                    device_id=(px, py, pc), device_id_type=MESH))
            own.append(pltpu.make_async_copy(in_refs[i].at[4 * x + 2 * y + c], land_refs[i].at[N_DEV - 1],
                                             own_sems.at[i]))
        return remote, own

    def start(self, in_refs, scratch):
        remote, own = self._copies(in_refs, scratch[:self.n], *scratch[self.n:])
        for cp in remote + own:
            cp.start()

    def finish(self, in_refs, scratch, out_refs):
        land_refs = scratch[:self.n]
        remote, own = self._copies(in_refs, land_refs, *scratch[self.n:])
        for cp in own:
            cp.wait()
        for cp in remote:
            cp.wait_recv()
        for i in range(self.n):
            chunk, steps = _row_chunks(self.shapes[i][1])

            def step(s, carry, i=i, chunk=chunk):
                r = pl.ds(pl.multiple_of(s * chunk, chunk), chunk)
                acc = land_refs[i][N_DEV - 1, r, :].astype(F32)
                for m in range(1, N_DEV):
                    acc = acc + land_refs[i][m - 1, r, :].astype(F32)
                out_refs[i][r, :] = acc
                return carry

            lax.fori_loop(0, steps, step, 0)
        for cp in remote:
            cp.wait_send()


def _reduce_scatter(parts, name):
    n = len(parts)
    rs = _ReduceScatter([p.shape for p in parts])

    def body(*refs):
        in_refs, out_refs, scratch = refs[:n], refs[n:2 * n], refs[2 * n:]
        rs.start(in_refs, scratch)
        rs.finish(in_refs, scratch, out_refs)

    return pl.pallas_call(
        body, name=name,
        out_shape=[jax.ShapeDtypeStruct(p.shape[1:], F32) for p in parts],
        in_specs=[_vmem()] * n, out_specs=[_vmem()] * n,
        scratch_shapes=rs.scratch(parts[0].dtype),
        compiler_params=_params(vmem=VMEM_LIMIT_V7X),
    )(*parts)


def _adamw_math(w, g, m, v):
    m = ADAM_B1 * m + (1.0 - ADAM_B1) * g
    v = ADAM_B2 * v + (1.0 - ADAM_B2) * (g * g)
    m_hat = m / (1.0 - ADAM_B1 ** ADAM_STEP)
    v_hat = v / (1.0 - ADAM_B2 ** ADAM_STEP)
    delta = -ADAM_LR * (m_hat / (jnp.sqrt(v_hat) + ADAM_EPS) + ADAM_WD * w)
    return delta, m, v


def _adamw_shards(gs, ws, ms, vs):
    n = len(gs)

    def body(*refs):
        g_refs, w_refs, m_refs, v_refs = (refs[k * n:(k + 1) * n] for k in range(4))
        d_out, m_out, v_out = (refs[(4 + k) * n:(5 + k) * n] for k in range(3))
        for i in range(n):
            chunk, steps = _row_chunks(gs[i].shape[0])

            def step(s, carry, i=i, chunk=chunk):
                r = pl.ds(pl.multiple_of(s * chunk, chunk), chunk)
                d, m, v = _adamw_math(w_refs[i][r, :], g_refs[i][r, :], m_refs[i][r, :], v_refs[i][r, :])
                d_out[i][r, :] = d
                m_out[i][r, :] = m
                v_out[i][r, :] = v
                return carry

            lax.fori_loop(0, steps, step, 0)

    shapes = [jax.ShapeDtypeStruct(g.shape, F32) for g in gs]
    outs = pl.pallas_call(
        body, name="adamw_shards", out_shape=shapes * 3,
        in_specs=[_vmem()] * (4 * n), out_specs=[_vmem()] * (3 * n),
        compiler_params=_params(vmem=VMEM_LIMIT_V7X),
    )(*gs, *ws, *ms, *vs)
    return outs[:n], outs[n:2 * n], outs[2 * n:]


def _small_update(gathered, w, m, v):
    rows = w.shape[0]
    chunk, steps = _row_chunks(rows)

    def body(ga_ref, w_ref, m_ref, v_ref, g_out, d_out, m_out, v_out):
        def step(s, carry):
            r = pl.ds(pl.multiple_of(s * chunk, chunk), chunk)
            g = ga_ref[0, r, :]
            for j in range(1, N_DEV):
                g = g + ga_ref[j, r, :]
            d, mm, vv = _adamw_math(w_ref[r, :], g, m_ref[r, :], v_ref[r, :])
            g_out[r, :] = g
            d_out[r, :] = d
            m_out[r, :] = mm
            v_out[r, :] = vv
            return carry

        lax.fori_loop(0, steps, step, 0)

    return pl.pallas_call(
        body, name="small_update", out_shape=[jax.ShapeDtypeStruct(w.shape, F32)] * 4,
        in_specs=[_vmem()] * 4, out_specs=[_vmem()] * 4,
    )(gathered, w, m, v)


def _zoh(lr, li, logdt, btr, bti):
    dt = jnp.exp(logdt)
    mag = jnp.exp(lr * dt)
    th = li * dt
    ar = mag * jnp.cos(th)
    ai = mag * jnp.sin(th)
    den = lr * lr + li * li
    nr = ar - 1.0
    cr = (nr * lr + ai * li) / den
    ci = (ai * lr - nr * li) / den
    return ar, ai, cr * btr - ci * bti, cr * bti + ci * btr


def _zoh_fwd(lr, li, logdt, btr, bti):
    def body(lr_ref, li_ref, dt_ref, br_ref, bi_ref, ar_ref, ai_ref, bbr_ref, bbi_ref):
        ar, ai, bbr, bbi = _zoh(lr_ref[...], li_ref[...], dt_ref[...], br_ref[...], bi_ref[...])
        ar_ref[...] = ar
        ai_ref[...] = ai
        bbr_ref[...] = bbr
        bbi_ref[...] = bbi

    s = jax.ShapeDtypeStruct
    return pl.pallas_call(
        body, name="zoh_fwd",
        out_shape=[s(lr.shape, F32), s(lr.shape, F32), s(btr.shape, F32), s(btr.shape, F32)],
        in_specs=[_vmem()] * 5, out_specs=[_vmem()] * 4,
    )(lr, li, logdt, btr, bti)


def _zoh_bwd(lr, li, logdt, btr, bti, dar, dai, dbbr, dbbi):
    def body(lr_ref, li_ref, dt_ref, br_ref, bi_ref, dar_ref, dai_ref, dbbr_ref, dbbi_ref,
             glr_ref, gli_ref, gdt_ref, gbr_ref, gbi_ref):
        _, vjp = jax.vjp(_zoh, lr_ref[...], li_ref[...], dt_ref[...], br_ref[...], bi_ref[...])
        glr, gli, gdt, gbr, gbi = vjp((dar_ref[...], dai_ref[...], dbbr_ref[...], dbbi_ref[...]))
        glr_ref[...] = glr
        gli_ref[...] = gli
        gdt_ref[...] = gdt
        gbr_ref[...] = gbr
        gbi_ref[...] = gbi

    s = jax.ShapeDtypeStruct
    return pl.pallas_call(
        body, name="zoh_bwd",
        out_shape=[s(lr.shape, F32), s(lr.shape, F32), s(logdt.shape, F32), s(btr.shape, F32), s(btr.shape, F32)],
        in_specs=[_vmem()] * 9, out_specs=[_vmem()] * 5,
    )(lr, li, logdt, btr, bti, dar, dai, dbbr, dbbi)


def _blockdiag(t):
    g, r, s = t.shape
    t = t.reshape(4, 8, r, s)
    out = jnp.einsum("jirs,ik->jirks", t, jnp.eye(8, dtype=t.dtype))
    return out.reshape(4, 8 * r, 8 * s)


def _blockdiag_extract(m, r, s):
    m = m.reshape(4, 8, r, 8, s)
    out = jnp.einsum("jirks,ik->jirs", m, jnp.eye(8, dtype=m.dtype))
    return out.reshape(32, r, s)


def _head_ones():
    r = jnp.arange(ATTN_W) // HEAD_DIM
    return (r[:, None] == r[None, :]).astype(F32)


def _in_proj(x2, g_mix, w_in_sh):
    t_tok = x2.shape[0]
    tm = min(1024, t_tok)
    nt = t_tok // tm
    ag_w = _AllGather(1, cast=True)
    n_sem = len(ag_w.scratch())

    def owner(i):
        x, y, c = lax.axis_index("x"), lax.axis_index("y"), lax.axis_index("c")
        rel = jnp.where(i < 2, 0, (i - 2) % 3 + 1)
        px = jnp.where((rel == 1) | (rel == 3), 1 - x, x)
        py = jnp.where((rel == 2) | (rel == 3), 1 - y, y)
        pc = jnp.where((i == 1) | (i >= 5), 1 - c, c)
        return 4 * px + 2 * py + pc

    def body(*refs):
        x_ref, g_ref, w_ref, z_ref, xn_ref, wg_ref, xn_scr, w_land = refs[:8]
        sems_w, out_sem = refs[8:8 + n_sem], refs[8 + n_sem]
        i, t = pl.program_id(0), pl.program_id(1)
        _, first, passed, arrive_ici, arrive_d2d, _ = ag_w._plan([w_ref], [w_land], sems_w)

        @pl.when((i == 0) & (t == 0))
        def _():
            ag_w.start([w_ref], [w_land], sems_w)

        @pl.when((i == 1) & (t == 0))
        def _():
            arrive_d2d[0].wait_recv()

        for n in range(3):
            @pl.when((i == 2 + n) & (t == 0))
            def _(n=n):
                arrive_ici[n].wait_recv()
                passed[n].start()

            @pl.when((i == 5 + n) & (t == 0))
            def _(n=n):
                arrive_d2d[1 + n].wait_recv()

        @pl.when(i == 0)
        def _():
            x = x_ref[...]
            r = lax.rsqrt(jnp.mean(x * x, axis=-1, keepdims=True) + EPS)
            xn = (x * r * g_ref[...]).astype(BF16)
            xn_ref[...] = xn
            xn_scr[t] = xn

        z_ref[...] = _dot(xn_scr[t], w_land[owner(i)])

        @pl.when((i == N_DEV - 1) & (t == nt - 1))
        def _():
            for cp in first + passed:
                cp.wait_send()
            out = pltpu.make_async_copy(w_land, wg_ref, out_sem)
            out.start()
            out.wait()

    s = jax.ShapeDtypeStruct
    xmap = lambda i, t: (jnp.where(i == 0, t, nt - 1), 0)
    gathered = s((N_DEV,) + w_in_sh.shape, BF16)
    return pl.pallas_call(
        body, name="in_proj", grid=(N_DEV, nt),
        out_shape=[s((t_tok, IN_W), F32), s((t_tok, D_MODEL), BF16), gathered],
        in_specs=[pl.BlockSpec((tm, D_MODEL), xmap), _full(g_mix.shape), _full(w_in_sh.shape)],
        out_specs=[pl.BlockSpec((tm, COL_W), lambda i, t: (t, owner(i))), pl.BlockSpec((tm, D_MODEL), xmap),
                   pl.BlockSpec(memory_space=pl.ANY)],
        scratch_shapes=[pltpu.VMEM((nt, tm, D_MODEL), BF16), pltpu.VMEM(gathered.shape, BF16)] + ag_w.scratch()
        + [pltpu.SemaphoreType.DMA],
        compiler_params=_params(2, VMEM_LIMIT_V7X),
    )(x2, g_mix, w_in_sh)


TQ = 128
NEG = -1e30


def _head_col(t, lm):
    return jnp.max(jnp.where(lm, t, NEG), axis=-1, keepdims=True)


def _head_masks():
    lane = lax.broadcasted_iota(jnp.int32, (1, 1, LANES), 2)
    return [(lane // HEAD_DIM) == h for h in range(LANES // HEAD_DIM)]


def _gather_classes(ref, dil, nt, tq, dtype):
    length = nt * tq
    if dil == 1:
        return ref[...].astype(dtype).reshape(nt, tq, LANES)
    parts = [ref[pl.ds(r, length, stride=dil), :].astype(dtype).reshape(nt, tq, LANES) for r in range(dil)]
    return jnp.concatenate(parts, axis=0)


def _scatter_classes(ref, val, dil, nt, tq, add):
    length = nt * tq
    for r in range(dil):
        rows = pl.ds(r, length, stride=dil) if dil > 1 else slice(None)
        part = val[r * nt:(r + 1) * nt].reshape(length, LANES)
        ref[rows, :] = ref[rows, :] + part if add else part


def _with_prev_tile(t3, dil, nt):
    parts = []
    for r in range(dil):
        t = t3[r * nt:(r + 1) * nt]
        parts.append(jnp.concatenate([t[:1], t[:-1]], axis=0))
    prev = parts[0] if dil == 1 else jnp.concatenate(parts, axis=0)
    return jnp.concatenate([prev, t3], axis=1)


def _band_valid(dil, nt, tq):
    if nt == 1:
        shape = (dil, tq, tq)
        return lax.broadcasted_iota(jnp.int32, shape, 1) >= lax.broadcasted_iota(jnp.int32, shape, 2)
    shape = (dil * nt, tq, 2 * tq)
    b = lax.broadcasted_iota(jnp.int32, shape, 0)
    c = lax.broadcasted_iota(jnp.int32, shape, 2)
    d = tq + lax.broadcasted_iota(jnp.int32, shape, 1) - c
    return (d >= 0) & (d <= tq) & (((b & (nt - 1)) != 0) | (c >= tq))


def _window_tiling(seq, window, dil):
    length = seq // dil
    tq = min(TQ, length)
    nt = length // tq
    assert length % tq == 0 and nt & (nt - 1) == 0 and (nt == 1 or window == tq * dil)
    return nt, tq


def _bqk(a, b):
    return jnp.einsum("bqd,bkd->bqk", a, b, preferred_element_type=F32)


def _bqd(a, b):
    return jnp.einsum("bqk,bkd->bqd", a, b, preferred_element_type=F32)


def _bkd(a, b):
    return jnp.einsum("bqk,bqd->bkd", a, b, preferred_element_type=F32)


def _qk_hat(q_ref, k_ref, gq_ref, gk_ref):
    lane = lax.broadcasted_iota(jnp.int32, (1, LANES), 1)

    def norm(raw, gain, scale):
        sq = raw * raw
        r = jnp.zeros_like(raw)
        for h in range(LANES // HEAD_DIM):
            lm = (lane // HEAD_DIM) == h
            ms = jnp.sum(jnp.where(lm, sq, 0.0), axis=-1, keepdims=True) * (1.0 / HEAD_DIM)
            r = jnp.where(lm, lax.rsqrt(ms + EPS), r)
        return raw * r * gain * scale

    return norm(q_ref[...], gq_ref[...], HEAD_DIM ** -0.5), norm(k_ref[...], gk_ref[...], 1.0)


def _zblock(seq, group):
    return pl.BlockSpec((seq, LANES), lambda b, hp: (b, group * (ATTN_W // LANES) + hp))


def _attn_fwd(z, gq2, gk2, nb, seq, late_sh):
    t_tok = nb * seq
    n_win = len(DILATED)
    host = _HostedGather(late_sh)
    n_late = host.n
    n_steps = (nb, ATTN_W // LANES)

    def body(*refs):
        (q_ref, k_ref, v_ref, ga_ref, gq_ref, gk_ref), refs = refs[:6], refs[6:]
        late_refs, refs = refs[:n_late], refs[n_late:]
        (o_ref, l_ref, ag_ref), refs = refs[:3], refs[3:]
        lateg_refs, refs = refs[:n_late], refs[n_late:]
        (qf, kf, oc, lc), host_scratch = refs[:4], refs[4:]
        step = pl.program_id(0) * n_steps[1] + pl.program_id(1)
        total = n_steps[0] * n_steps[1]

        @pl.when(step == 0)
        def _():
            host.start(late_refs, host_scratch)

        @pl.when(step == total // 2)
        def _():
            host.forward(late_refs, host_scratch)

        qf[...], kf[...] = _qk_hat(q_ref, k_ref, gq_ref, gk_ref)
        lms = _head_masks()
        for w, (window, dil) in enumerate(DILATED):
            nt, tq = _window_tiling(seq, window, dil)
            q3 = _gather_classes(qf, dil, nt, tq, BF16)
            k3 = _gather_classes(kf, dil, nt, tq, BF16)
            v3 = _gather_classes(v_ref, dil, nt, tq, BF16)
            if nt > 1:
                k3, v3 = _with_prev_tile(k3, dil, nt), _with_prev_tile(v3, dil, nt)
            valid = _band_valid(dil, nt, tq)
            o = jnp.zeros(q3.shape, F32)
            lse = jnp.zeros(q3.shape, F32)
            for lm in lms:
                s = _bqk(jnp.where(lm, q3, jnp.zeros_like(q3)), k3)
                m = jnp.max(jnp.where(valid, s, NEG), axis=-1, keepdims=True)
                p = jnp.where(valid, jnp.exp(s - m), 0.0)
                den = jnp.sum(p, axis=-1, keepdims=True)
                o = jnp.where(lm, _bqd(p.astype(BF16), v3) / den, o)
                lse = jnp.where(lm, m + jnp.log(den), lse)
            _scatter_classes(oc.at[w], o, dil, nt, tq, add=False)
            _scatter_classes(lc.at[w], lse, dil, nt, tq, add=False)
        mx = lc[0]
        for w in range(1, n_win):
            mx = jnp.maximum(mx, lc[w])
        tot = jnp.zeros_like(mx)
        o = jnp.zeros_like(mx)
        for w in range(n_win):
            e = jnp.exp(lc[w] - mx)
            tot = tot + e
            o = o + e * oc[w]
        o = o / tot
        o_ref[...] = o
        l_ref[...] = mx + jnp.log(tot)
        ga = ga_ref[...]
        ag_ref[...] = (o * ga * _sig(ga)).astype(BF16)

        @pl.when(step == total - 1)
        def _():
            host.finish(late_refs, host_scratch, lateg_refs)

    blk = pl.BlockSpec((seq, LANES), lambda b, hp: (b, hp))
    s = jax.ShapeDtypeStruct
    outs = pl.pallas_call(
        body, name="attn_fwd", grid=n_steps,
        out_shape=[s((t_tok, ATTN_W), F32), s((t_tok, ATTN_W), F32), s((t_tok, ATTN_W), BF16)] + host.out_shape(),
        in_specs=[_zblock(seq, 0), _zblock(seq, 1), _zblock(seq, 2), _zblock(seq, 3), _full(gq2.shape),
                  _full(gk2.shape)] + [_full(a.shape) for a in late_sh],
        out_specs=[blk, blk, blk] + [pl.BlockSpec(memory_space=pl.ANY)] * n_late,
        scratch_shapes=[pltpu.VMEM((seq, LANES), F32)] * 2 + [pltpu.VMEM((n_win, seq, LANES), F32)] * 2
        + host.scratch(),
        compiler_params=_params(2, VMEM_LIMIT_V7X),
    )(z, z, z, z, gq2, gk2, *late_sh)
    return outs[:3], outs[3:]


SCAN_COLS = 512


def _to_segments(dst_ref, val):
    seg = val.shape[0] // SUBLANES
    for n in range(dst_ref.shape[0]):
        for s in range(SUBLANES):
            dst_ref[n, pl.ds(s, seg, stride=SUBLANES), :] = val[s * seg:(s + 1) * seg, n * LANES:(n + 1) * LANES]


def _from_segments(src_ref):
    seg = src_ref.shape[1] // SUBLANES
    return jnp.concatenate(
        [jnp.concatenate([src_ref[n, pl.ds(s, seg, stride=SUBLANES), :] for s in range(SUBLANES)], axis=0)
         for n in range(src_ref.shape[0])], axis=1)


def _scan_chunk(re_ref, im_ref, a_re_ref, a_im_ref, carry_re, carry_im, rows, reverse, visit=None):
    seg = rows // SUBLANES
    assert seg & (seg - 1) == 0
    rowi = lax.broadcasted_iota(jnp.int32, (SUBLANES, SCAN_COLS), 0)
    edge = (SUBLANES - 1) if reverse else 0
    last = 0 if reverse else SUBLANES - 1
    at_edge = rowi == edge

    def cmul(ar, ai, br, bi):
        return ar * br - ai * bi, ar * bi + ai * br

    for c0 in range(0, N_STATE, SCAN_COLS):
        cols = slice(c0, c0 + SCAN_COLS)
        a1r = jnp.broadcast_to(a_re_ref[:, cols], (SUBLANES, SCAN_COLS))
        a1i = jnp.broadcast_to(a_im_ref[:, cols], (SUBLANES, SCAN_COLS))
        if reverse:
            a1i = -a1i

        def block_of(i):
            j = (seg - 1 - i) if reverse else i
            return j, pl.ds(pl.multiple_of(j * SUBLANES, SUBLANES), SUBLANES)

        def local(i, carry, cols=cols, a1r=a1r, a1i=a1i):
            xr, xi = carry
            _, blk = block_of(i)
            nr, ni = cmul(a1r, a1i, xr, xi)
            xr, xi = nr + re_ref[blk, cols], ni + im_ref[blk, cols]
            re_ref[blk, cols] = xr
            im_ref[blk, cols] = xi
            return xr, xi

        zero = jnp.zeros((SUBLANES, SCAN_COLS), F32)
        er, ei = lax.fori_loop(0, seg, local, (zero, zero))

        pr, pi = a1r, a1i
        for _ in range(seg.bit_length() - 1):
            pr, pi = cmul(pr, pi, pr, pi)
        cr, ci = carry_re[:, cols], carry_im[:, cols]
        inr, ini = cmul(pr, pi, cr, ci)
        er = er + jnp.where(at_edge, inr, 0.0)
        ei = ei + jnp.where(at_edge, ini, 0.0)
        for sft in (1, 2, 4):
            shift, keep = (SUBLANES - sft, rowi < SUBLANES - sft) if reverse else (sft, rowi >= sft)
            rs = jnp.where(keep, pltpu.roll(er, shift, 0), 0.0)
            ims = jnp.where(keep, pltpu.roll(ei, shift, 0), 0.0)
            dr, di = cmul(pr, pi, rs, ims)
            er, ei = er + dr, ei + di
            pr, pi = cmul(pr, pi, pr, pi)
        carry_re[:, cols] = jnp.broadcast_to(er[last:last + 1, :], (SUBLANES, SCAN_COLS))
        carry_im[:, cols] = jnp.broadcast_to(ei[last:last + 1, :], (SUBLANES, SCAN_COLS))
        one = (SUBLANES - 1) if reverse else 1
        kr = jnp.where(at_edge, cr, pltpu.roll(er, one, 0))
        ki = jnp.where(at_edge, ci, pltpu.roll(ei, one, 0))

        def fix(i, carry, cols=cols, a1r=a1r, a1i=a1i):
            kr, ki, acc = carry
            j, blk = block_of(i)
            kr, ki = cmul(a1r, a1i, kr, ki)
            xr, xi = re_ref[blk, cols] + kr, im_ref[blk, cols] + ki
            re_ref[blk, cols] = xr
            im_ref[blk, cols] = xi
            if visit is not None:
                acc = visit(cols, j, xr, xi, acc)
            return kr, ki, acc

        _, _, acc = lax.fori_loop(0, seg, fix, (kr, ki, (zero, zero)))
        if visit is not None:
            visit(cols, None, None, None, acc)


def _ssm_fwd(z, a_re, a_im, bb_re, bb_im, cc_re, cc_im, d_skip, w_glu, b_glu, nb, seq):
    t_tok = nb * seq
    tc = min(256, seq)
    nch = seq // tc
    grp = N_STATE // 4

    def body(u_ref, gs_ref, ar_ref, ai_ref, bbr_ref, bbi_ref, ccr_ref, cci_ref, d_ref, wg_ref, bg_ref,
             xr_ref, xi_ref, y_ref, sg_ref, car_re, car_im, seg_u, seg_y):
        @pl.when(pl.program_id(1) == 0)
        def _():
            car_re[...] = jnp.zeros_like(car_re)
            car_im[...] = jnp.zeros_like(car_im)

        u = u_ref[...]
        _to_segments(seg_u, u)
        for j in range(4):
            uj = seg_u[j].astype(BF16)
            xr_ref[:, j * grp:(j + 1) * grp] = _dot(uj, bbr_ref[j])
            xi_ref[:, j * grp:(j + 1) * grp] = _dot(uj, bbi_ref[j])
        _scan_chunk(xr_ref, xi_ref, ar_ref, ai_ref, car_re, car_im, tc, reverse=False)
        for j in range(4):
            xr = xr_ref[:, j * grp:(j + 1) * grp].astype(BF16)
            xi = xi_ref[:, j * grp:(j + 1) * grp].astype(BF16)
            seg_y[j] = _dot(xr, ccr_ref[j]) - _dot(xi, cci_ref[j])
        y = _from_segments(seg_y) + d_ref[...] * u
        y_ref[...] = y
        yg, _ = _gelu_and_grad(y)
        gl = _dot(yg.astype(BF16), wg_ref[...]) + bg_ref[...]
        gs = gs_ref[...]
        sg_ref[...] = (yg * _sig(gl) * gs * _sig(gs)).astype(BF16)

    umap = lambda b, ch: (b * nch + ch, 4)
    gmap = lambda b, ch: (b * nch + ch, 5)
    row = lambda b, ch: (b * nch + ch, 0)
    s = jax.ShapeDtypeStruct
    consts = [a_re, a_im, bb_re, bb_im, cc_re, cc_im, d_skip, w_glu, b_glu]
    return pl.pallas_call(
        body, name="ssm_fwd", grid=(nb, nch),
        out_shape=[s((t_tok, N_STATE), F32), s((t_tok, N_STATE), F32), s((t_tok, SSM_W), F32),
                   s((t_tok, SSM_W), BF16)],
        in_specs=[pl.BlockSpec((tc, SSM_W), umap), pl.BlockSpec((tc, SSM_W), gmap)] + [_full(c.shape) for c in consts],
        out_specs=[pl.BlockSpec((tc, N_STATE), row), pl.BlockSpec((tc, N_STATE), row),
                   pl.BlockSpec((tc, SSM_W), row), pl.BlockSpec((tc, SSM_W), row)],
        scratch_shapes=[pltpu.VMEM((SUBLANES, N_STATE), F32), pltpu.VMEM((SUBLANES, N_STATE), F32),
                        pltpu.VMEM((4, tc, LANES), F32), pltpu.VMEM((4, tc, LANES), F32)],
        compiler_params=_params(2, VMEM_LIMIT_V7X),
    )(z, z, *consts)


def _tail(x2, tg2, ag, sg, p2, w_out, w_g, w_p, g_ple):
    t_tok = x2.shape[0]
    tm = min(256, t_tok)
    nt = t_tok // tm
    half = ATTN_W

    def body(x_ref, tg_ref, ag_ref, sg_ref, p_ref, wo_ref, wg_ref, wp_ref, gp_ref,
             dmix_ref, dh1_ref, loss_ref, dgp_ref, dwo_ref, dwg_ref, dwp_ref, acc_o, acc_g, acc_p):
        i = pl.program_id(0)

        @pl.when(i == 0)
        def _():
            loss_ref[...] = jnp.zeros_like(loss_ref)
            dgp_ref[...] = jnp.zeros_like(dgp_ref)
            acc_o[...] = jnp.zeros_like(acc_o)
            acc_g[...] = jnp.zeros_like(acc_g)
            acc_p[...] = jnp.zeros_like(acc_p)

        ag_t, sg_t = ag_ref[...], sg_ref[...]
        h1 = x_ref[...] + _dot(ag_t, wo_ref[0:half, :]) + _dot(sg_t, wo_ref[half:2 * half, :])
        r2 = lax.rsqrt(jnp.mean(h1 * h1, axis=-1, keepdims=True) + EPS)
        hnorm = h1 * r2
        gp = gp_ref[...]
        hn = (hnorm * gp).astype(BF16)
        gate = _sig(_dot(hn, wg_ref[...]))
        pb = p_ref[...].astype(BF16)
        pp = jnp.concatenate([_dot(pb, wp_ref[j]) for j in range(N_DEV)], axis=-1)
        h2 = h1 + gate * pp
        err = h2 - tg_ref[...]
        loss_ref[...] += 0.5 * jnp.sum(err * err) * (1.0 / D_MODEL)
        dh2 = err * (1.0 / D_MODEL)
        dpp = (dh2 * gate).astype(BF16)
        dgpre = (dh2 * pp * gate * (1.0 - gate)).astype(BF16)
        acc_p[...] += _dot_tn(pb, dpp)
        acc_g[...] += _dot_tn(hn, dgpre)
        dhn = _dot_nt(dgpre, wg_ref[...])
        dgp_ref[...] += jnp.sum(dhn * hnorm, axis=0, keepdims=True)
        a = dhn * gp
        dh1 = dh2 + r2 * (a - hnorm * jnp.mean(a * hnorm, axis=-1, keepdims=True))
        dh1_ref[...] = dh1
        dh1b = dh1.astype(BF16)
        acc_o[0:half, :] += _dot_tn(ag_t, dh1b)
        acc_o[half:2 * half, :] += _dot_tn(sg_t, dh1b)
        dmix_ref[...] = _dot_nt(dh1b, wo_ref[...])

        @pl.when(i == nt - 1)
        def _():
            dwo_ref[...] = acc_o[...].astype(BF16)
            dwg_ref[...] = acc_g[...].astype(BF16)
            for j in range(N_DEV):
                dwp_ref[j] = acc_p[:, j * LANES:(j + 1) * LANES].astype(BF16)

    row = lambda i: (i, 0)
    s = jax.ShapeDtypeStruct
    return pl.pallas_call(
        body, name="tail_fwd_bwd", grid=(nt,),
        out_shape=[s((t_tok, D_MODEL), F32), s((t_tok, D_MODEL), F32), s((SUBLANES, LANES), F32),
                   s((1, D_MODEL), F32), s((D_MODEL, D_MODEL), BF16), s((D_MODEL, D_MODEL), BF16),
                   s((N_DEV, PLE_DIM, LANES), BF16)],
        in_specs=[pl.BlockSpec((tm, D_MODEL), row), pl.BlockSpec((tm, D_MODEL), row),
                  pl.BlockSpec((tm, half), row), pl.BlockSpec((tm, half), row), pl.BlockSpec((tm, PLE_DIM), row),
                  _full(w_out.shape), _full(w_g.shape), _full(w_p.shape), _full(g_ple.shape)],
        out_specs=[pl.BlockSpec((tm, D_MODEL), row), pl.BlockSpec((tm, D_MODEL), row), _full((SUBLANES, LANES)),
                   _full((1, D_MODEL)), _full((D_MODEL, D_MODEL)), _full((D_MODEL, D_MODEL)),
                   _full((N_DEV, PLE_DIM, LANES))],
        scratch_shapes=[pltpu.VMEM((D_MODEL, D_MODEL), F32), pltpu.VMEM((D_MODEL, D_MODEL), F32),
                        pltpu.VMEM((PLE_DIM, D_MODEL), F32)],
        compiler_params=_params(1, VMEM_LIMIT_V7X),
    )(x2, tg2, ag, sg, p2, w_out, w_g, w_p, g_ple)


def _attn_bwd(z, gq2, gk2, o, lse, dmix, nb, seq, parts):
    t_tok = nb * seq
    n_rs = len(parts)
    rs = _ReduceScatter([p.shape for p in parts])
    n_steps = (nb, ATTN_W // LANES)

    def body(*refs):
        (q_ref, k_ref, v_ref, ga_ref, gq_ref, gk_ref, o_ref, l_ref, da_ref), refs = refs[:9], refs[9:]
        part_refs, refs = refs[:n_rs], refs[n_rs:]
        (dq_ref, dk_ref, dv_ref, dga_ref), refs = refs[:4], refs[4:]
        g_refs, refs = refs[:n_rs], refs[n_rs:]
        (qf, kf, dof, dlf), rs_scratch = refs[:4], refs[4:]
        b, hp = pl.program_id(0), pl.program_id(1)

        @pl.when((b == 0) & (hp == 0))
        def _():
            rs.start(part_refs, rs_scratch)

        ga, o_t, da = ga_ref[...], o_ref[...], da_ref[...]
        sga = _sig(ga)
        d_o = da * ga * sga
        dga_ref[...] = da * o_t * sga * (1.0 + ga * (1.0 - sga))
        lane = lax.broadcasted_iota(jnp.int32, (1, LANES), 1)
        d_oo = d_o * o_t
        delta = jnp.zeros_like(d_oo)
        for h in range(LANES // HEAD_DIM):
            lm2 = (lane // HEAD_DIM) == h
            delta = jnp.where(lm2, jnp.sum(jnp.where(lm2, d_oo, 0.0), axis=-1, keepdims=True), delta)
        qf[...], kf[...] = _qk_hat(q_ref, k_ref, gq_ref, gk_ref)
        dof[...] = d_o
        dlf[...] = delta
        dq_ref[...] = jnp.zeros_like(dq_ref)
        dk_ref[...] = jnp.zeros_like(dk_ref)
        dv_ref[...] = jnp.zeros_like(dv_ref)
        lms = _head_masks()
        for window, dil in DILATED:
            nt, tq = _window_tiling(seq, window, dil)
            q3 = _gather_classes(qf, dil, nt, tq, BF16)
            k3 = _gather_classes(kf, dil, nt, tq, BF16)
            v3 = _gather_classes(v_ref, dil, nt, tq, BF16)
            do3 = _gather_classes(dof, dil, nt, tq, BF16)
            lt3 = _gather_classes(l_ref, dil, nt, tq, F32)
            dl3 = _gather_classes(dlf, dil, nt, tq, F32)
            if nt > 1:
                k3, v3 = _with_prev_tile(k3, dil, nt), _with_prev_tile(v3, dil, nt)
            valid = _band_valid(dil, nt, tq)
            dq = jnp.zeros(q3.shape, F32)
            dk = jnp.zeros(k3.shape, F32)
            dv = jnp.zeros(k3.shape, F32)
            for lm in lms:
                qm = jnp.where(lm, q3, jnp.zeros_like(q3))
                dom = jnp.where(lm, do3, jnp.zeros_like(do3))
                p = jnp.where(valid, jnp.exp(_bqk(qm, k3) - _head_col(lt3, lm)), 0.0)
                dv = dv + _bkd(p.astype(BF16), dom)
                ds = (p * (_bqk(dom, v3) - _head_col(dl3, lm))).astype(BF16)
                dq = dq + jnp.where(lm, _bqd(ds, k3), 0.0)
                dk = dk + _bkd(ds, qm)
            _scatter_classes(dq_ref, dq, dil, nt, tq, add=True)
            for ref, g in ((dk_ref, dk), (dv_ref, dv)):
                if nt > 1:
                    own, prev = g[:, tq:, :], g[:, :tq, :]
                    shifted = []
                    for r in range(dil):
                        t = prev[r * nt:(r + 1) * nt]
                        shifted.append(jnp.concatenate([t[1:], jnp.zeros_like(t[:1])], axis=0))
                    g = own + (shifted[0] if dil == 1 else jnp.concatenate(shifted, axis=0))
                _scatter_classes(ref, g, dil, nt, tq, add=True)

        @pl.when((b == n_steps[0] - 1) & (hp == n_steps[1] - 1))
        def _():
            rs.finish(part_refs, rs_scratch, g_refs)

    blk = pl.BlockSpec((seq, LANES), lambda b, hp: (b, hp))
    s = jax.ShapeDtypeStruct
    outs = pl.pallas_call(
        body, name="attn_bwd", grid=n_steps,
        out_shape=[s((t_tok, ATTN_W), F32)] * 4 + [s(p.shape[1:], F32) for p in parts],
        in_specs=[_zblock(seq, 0), _zblock(seq, 1), _zblock(seq, 2), _zblock(seq, 3), _full(gq2.shape),
                  _full(gk2.shape), blk, blk, blk] + [pl.BlockSpec(memory_space=pl.ANY)] * n_rs,
        out_specs=[blk] * 4 + [_full(p.shape[1:]) for p in parts],
        scratch_shapes=[pltpu.VMEM((seq, LANES), F32)] * 4 + rs.scratch(parts[0].dtype),
        compiler_params=_params(2, VMEM_LIMIT_V7X),
    )(z, z, z, z, gq2, gk2, o, lse, dmix, *parts)
    return outs[:4], outs[4:]


def _ssm_bwd(z, dmix, y, x_re, x_im, a_re, a_im, bb_re, bb_im, cc_re, cc_im, d_skip, w_glu, b_glu, nb, seq):
    t_tok = nb * seq
    tc = min(256, seq)
    nch = seq // tc
    grp = N_STATE // 4

    def body(u_ref, gs_ref, ds_ref, y_ref, xr_ref, xi_ref, xpr_ref, xpi_ref,
             ar_ref, ai_ref, bbr_ref, bbi_ref, ccr_ref, cci_ref, d_ref, wg_ref, bg_ref,
             du_ref, dgs_ref, dwg_ref, dbg_ref, dd_ref, dar_ref, dai_ref, dbbr_ref, dbbi_ref, dccr_ref, dcci_ref,
             lam_re, lam_im, car_re, car_im, acc_wg, seg_a, seg_b, ent_re, ent_im):
        step = pl.program_id(1)
        first_chunk = step == nch - 1

        @pl.when((pl.program_id(0) == 0) & (step == 0))
        def _():
            acc_wg[...] = jnp.zeros_like(acc_wg)
            for ref in (dbg_ref, dd_ref, dar_ref, dai_ref, dbbr_ref, dbbi_ref, dccr_ref, dcci_ref):
                ref[...] = jnp.zeros_like(ref)

        @pl.when(step == 0)
        def _():
            car_re[...] = jnp.zeros_like(car_re)
            car_im[...] = jnp.zeros_like(car_im)

        u, gs, dssm, y = u_ref[...], gs_ref[...], ds_ref[...], y_ref[...]
        yg, dgelu = _gelu_and_grad(y)
        ygb = yg.astype(BF16)
        sgl = _sig(_dot(ygb, wg_ref[...]) + bg_ref[...])
        sgs = _sig(gs)
        dout = dssm * gs * sgs
        dgs_ref[...] = dssm * yg * sgl * sgs * (1.0 + gs * (1.0 - sgs))
        dgl = dout * yg * sgl * (1.0 - sgl)
        dglb = dgl.astype(BF16)
        dyg = dout * sgl + _dot_nt(dglb, wg_ref[...])
        acc_wg[...] += _dot_tn(ygb, dglb)
        dbg_ref[...] += jnp.sum(dgl, axis=0, keepdims=True)
        dy = dyg * dgelu
        dd_ref[...] += jnp.sum(dy * u, axis=0, keepdims=True)
        _to_segments(seg_a, dy)
        _to_segments(seg_b, u)
        for j in range(4):
            dyj = seg_a[j].astype(BF16)
            sl = slice(j * grp, (j + 1) * grp)
            lam_re[:, sl] = _dot_nt(dyj, ccr_ref[j])
            lam_im[:, sl] = -_dot_nt(dyj, cci_ref[j])
            dccr_ref[j] += _dot_tn(xr_ref[:, sl].astype(BF16), dyj)
            dcci_ref[j] -= _dot_tn(xi_ref[:, sl].astype(BF16), dyj)

        keep_prev = jnp.where(first_chunk, 0.0, 1.0)
        seg = tc // SUBLANES
        last_blk = pl.ds((seg - 1) * SUBLANES, SUBLANES)
        row0 = lax.broadcasted_iota(jnp.int32, (SUBLANES, N_STATE), 0) == 0
        for src, prev, dst in ((xr_ref, xpr_ref, ent_re), (xi_ref, xpi_ref, ent_im)):
            before = jnp.broadcast_to(prev[SUBLANES - 1:SUBLANES, :] * keep_prev, (SUBLANES, N_STATE))
            dst[...] = jnp.where(row0, before, pltpu.roll(src[last_blk, :], 1, 0))

        def visit(cols, j, lr, li, acc):
            if j is None:
                dar_ref[:, cols] += jnp.sum(acc[0], axis=0, keepdims=True)
                dai_ref[:, cols] += jnp.sum(acc[1], axis=0, keepdims=True)
                return None
            blk = pl.ds(pl.multiple_of(jnp.maximum(j - 1, 0) * SUBLANES, SUBLANES), SUBLANES)
            inside = j > 0
            xpr = jnp.where(inside, xr_ref[blk, cols], ent_re[:, cols])
            xpi = jnp.where(inside, xi_ref[blk, cols], ent_im[:, cols])
            return acc[0] + lr * xpr + li * xpi, acc[1] + li * xpr - lr * xpi

        _scan_chunk(lam_re, lam_im, ar_ref, ai_ref, car_re, car_im, tc, reverse=True, visit=visit)

        for j in range(4):
            sl = slice(j * grp, (j + 1) * grp)
            lr = lam_re[:, sl].astype(BF16)
            li = lam_im[:, sl].astype(BF16)
            uj = seg_b[j].astype(BF16)
            seg_a[j] = _dot_nt(lr, bbr_ref[j]) + _dot_nt(li, bbi_ref[j])
            dbbr_ref[j] += _dot_tn(uj, lr)
            dbbi_ref[j] += _dot_tn(uj, li)
        du_ref[...] = _from_segments(seg_a) + dy * d_ref[...]

        @pl.when((pl.program_id(0) == nb - 1) & (step == nch - 1))
        def _():
            dwg_ref[...] = acc_wg[...].astype(BF16)

    rev = lambda b, ch: b * nch + (nch - 1 - ch)
    umap = lambda b, ch: (rev(b, ch), 4)
    gmap = lambda b, ch: (rev(b, ch), 5)
    smap = lambda b, ch: (rev(b, ch), 1)
    row = lambda b, ch: (rev(b, ch), 0)
    prev = lambda b, ch: (jnp.maximum(rev(b, ch) * (tc // SUBLANES) - 1, 0), 0)
    s = jax.ShapeDtypeStruct
    consts = [a_re, a_im, bb_re, bb_im, cc_re, cc_im, d_skip, w_glu, b_glu]
    acc_shapes = [s((1, SSM_W), F32), s((1, SSM_W), F32), s((1, N_STATE), F32), s((1, N_STATE), F32),
                  s(bb_re.shape, F32), s(bb_re.shape, F32), s(cc_re.shape, F32), s(cc_re.shape, F32)]
    return pl.pallas_call(
        body, name="ssm_bwd", grid=(nb, nch),
        out_shape=[s((t_tok, SSM_W), F32), s((t_tok, SSM_W), F32), s((SSM_W, SSM_W), BF16)] + acc_shapes,
        in_specs=[pl.BlockSpec((tc, SSM_W), umap), pl.BlockSpec((tc, SSM_W), gmap), pl.BlockSpec((tc, SSM_W), smap),
                  pl.BlockSpec((tc, SSM_W), row), pl.BlockSpec((tc, N_STATE), row), pl.BlockSpec((tc, N_STATE), row),
                  pl.BlockSpec((SUBLANES, N_STATE), prev), pl.BlockSpec((SUBLANES, N_STATE), prev)]
        + [_full(c.shape) for c in consts],
        out_specs=[pl.BlockSpec((tc, SSM_W), row), pl.BlockSpec((tc, SSM_W), row), _full((SSM_W, SSM_W))]
        + [_full(a.shape) for a in acc_shapes],
        scratch_shapes=[pltpu.VMEM((tc, N_STATE), F32), pltpu.VMEM((tc, N_STATE), F32),
                        pltpu.VMEM((SUBLANES, N_STATE), F32), pltpu.VMEM((SUBLANES, N_STATE), F32),
                        pltpu.VMEM((SSM_W, SSM_W), F32), pltpu.VMEM((4, tc, LANES), F32),
                        pltpu.VMEM((4, tc, LANES), F32),
                        pltpu.VMEM((SUBLANES, N_STATE), F32), pltpu.VMEM((SUBLANES, N_STATE), F32)],
        compiler_params=_params(2, VMEM_LIMIT_V7X),
    )(z, z, dmix, y, x_re, x_im, x_re, x_im, *consts)


def _dz_and_dx(x2, z, dqh, dkh, dvb, dga, du, dgs, dh1, w_in_g, g_mix, gq_t, gk_t, ones_bd, fold):
    t_tok = x2.shape[0]
    tm = min(256, t_tok)
    nt = t_tok // tm
    a_w = ATTN_W

    def head_norm_bwd(raw, d_hat, gain, scale, ones):
        r = lax.rsqrt(_hdot(raw * raw, ones) * (1.0 / HEAD_DIM) + EPS)
        n = raw * r
        a = d_hat * gain * scale
        d_raw = r * (a - n * (_hdot(a * n, ones) * (1.0 / HEAD_DIM)))
        return d_raw, jnp.sum(d_hat * n * scale, axis=0, keepdims=True)

    def body(x_ref, q_ref, k_ref, dq_ref, dk_ref, dv_ref, dga_ref, du_ref, dgs_ref, dh1_ref, w_ref, g_ref,
             gq_ref, gk_ref, ones_ref, fold_ref, dz_ref, gx_ref, dgm_ref, dgq_ref, dgk_ref, acc_q, acc_k):
        i = pl.program_id(0)

        @pl.when(i == 0)
        def _():
            dgm_ref[...] = jnp.zeros_like(dgm_ref)
            acc_q[...] = jnp.zeros_like(acc_q)
            acc_k[...] = jnp.zeros_like(acc_k)

        ones = ones_ref[...]
        dq, sq = head_norm_bwd(q_ref[...], dq_ref[...], gq_ref[...], HEAD_DIM ** -0.5, ones)
        dk, sk = head_norm_bwd(k_ref[...], dk_ref[...], gk_ref[...], 1.0, ones)
        acc_q[...] += jnp.broadcast_to(sq, acc_q.shape)
        acc_k[...] += jnp.broadcast_to(sk, acc_k.shape)
        parts = (dq, dk, dv_ref[...], dga_ref[...], du_ref[...], dgs_ref[...])
        for n, part in enumerate(parts):
            dz_ref[:, n * a_w:(n + 1) * a_w] = part.astype(BF16)
        dxn = jnp.zeros((tm, D_MODEL), F32)
        for j in range(N_DEV):
            dxn = dxn + _dot_nt(dz_ref[:, j * COL_W:(j + 1) * COL_W], w_ref[j])
        x = x_ref[...]
        r1 = lax.rsqrt(jnp.mean(x * x, axis=-1, keepdims=True) + EPS)
        xnorm = x * r1
        dgm_ref[...] += jnp.sum(dxn * xnorm, axis=0, keepdims=True)
        a = dxn * g_ref[...]
        gx_ref[...] = dh1_ref[...] + r1 * (a - xnorm * jnp.mean(a * xnorm, axis=-1, keepdims=True))

        @pl.when(i == nt - 1)
        def _():
            dgq_ref[...] = _hdot(acc_q[...], fold_ref[...])
            dgk_ref[...] = _hdot(acc_k[...], fold_ref[...])

    row = lambda i: (i, 0)
    col = lambda n: (lambda i: (i, n))
    s = jax.ShapeDtypeStruct
    half = pl.BlockSpec((tm, a_w), row)
    return pl.pallas_call(
        body, name="dz_dx", grid=(nt,),
        out_shape=[s((t_tok, IN_W), BF16), s((t_tok, D_MODEL), F32), s((1, D_MODEL), F32),
                   s((SUBLANES, HEAD_DIM), F32), s((SUBLANES, HEAD_DIM), F32)],
        in_specs=[pl.BlockSpec((tm, D_MODEL), row), pl.BlockSpec((tm, a_w), col(0)), pl.BlockSpec((tm, a_w), col(1)),
                  half, half, half, half, half, half, pl.BlockSpec((tm, D_MODEL), row),
                  _full(w_in_g.shape), _full(g_mix.shape), _full(gq_t.shape), _full(gk_t.shape),
                  _full(ones_bd.shape), _full(fold.shape)],
        out_specs=[pl.BlockSpec((tm, IN_W), row), pl.BlockSpec((tm, D_MODEL), row), _full((1, D_MODEL)),
                   _full((SUBLANES, HEAD_DIM)), _full((SUBLANES, HEAD_DIM))],
        scratch_shapes=[pltpu.VMEM((SUBLANES, a_w), F32), pltpu.VMEM((SUBLANES, a_w), F32)],
        compiler_params=_params(1, VMEM_LIMIT_V7X),
    )(x2, z, z, dqh, dkh, dvb, dga, du, dgs, dh1, w_in_g, g_mix, gq_t, gk_t, ones_bd, fold)


def _dw_in(xn, dz, glu_parts, small):
    t_tok = xn.shape[0]
    tk = min(1024, t_tok)
    nk = t_tok // tk
    rs = _ReduceScatter([glu_parts.shape])
    ag = _AllGather(1, cast=False)
    n_rs = len(rs.scratch(BF16))

    def place():
        x, y, c = lax.axis_index("x"), lax.axis_index("y"), lax.axis_index("c")
        return x, y, c, [(1 - x, y), (x, 1 - y), (1 - x, 1 - y)]

    def target(i):
        x, y, c, _ = place()
        rel = jnp.where(i < 6, i // 2 + 1, 0)
        px = jnp.where((rel == 1) | (rel == 3), 1 - x, x)
        py = jnp.where((rel == 2) | (rel == 3), 1 - y, y)
        pc = jnp.where(i % 2 == 0, 1 - c, c)
        return 4 * px + 2 * py + pc

    chunk, chunks = _row_chunks(D_MODEL)

    def body(xn_ref, dz_ref, glu_ref, small_ref, gin_ref, gglu_ref, gath_ref, acc, stage, land, send_sems, recv_sems,
             *rest):
        rs_scratch, ag_sems = rest[:n_rs], rest[n_rs:]
        i, k = pl.program_id(0), pl.program_id(1)
        x, y, c, chips = place()

        def push(slot, to):
            return pltpu.make_async_remote_copy(
                src_ref=stage.at[slot], dst_ref=land.at[slot], send_sem=send_sems.at[slot],
                recv_sem=recv_sems.at[slot], device_id=to, device_id_type=MESH)

        pushes = [push(n, (x, y, 1 - c)) for n in range(4)] + [push(4 + n, (*chips[n], c)) for n in range(3)]

        def staged(slot, plus=None):
            def put(s, carry):
                r = pl.ds(pl.multiple_of(s * chunk, chunk), chunk)
                val = acc[r, :]
                if plus is not None:
                    val = val + land[plus, r, :].astype(F32)
                stage[slot, r, :] = val.astype(BF16)
                return carry

            lax.fori_loop(0, chunks, put, 0)

        @pl.when((i == 0) & (k == 0))
        def _():
            rs.start([glu_ref], rs_scratch)
            ag.start([small_ref], [gath_ref], ag_sems)

        @pl.when((i == N_DEV // 2) & (k == 0))
        def _():
            ag.forward([small_ref], [gath_ref], ag_sems)

        @pl.when(k == 0)
        def _():
            acc[...] = jnp.zeros_like(acc)

        acc[...] += _dot_tn(xn_ref[...], dz_ref[...])

        for n in range(4):
            @pl.when((k == nk - 1) & (i == 2 * n))
            def _(n=n):
                staged(n)
                pushes[n].start()

        for n in range(3):
            @pl.when((k == nk - 1) & (i == 2 * n + 1))
            def _(n=n):
                pushes[n].wait_recv()
                staged(4 + n, plus=n)
                pushes[4 + n].start()

        @pl.when((k == nk - 1) & (i == N_DEV - 1))
        def _():
            for slot in range(3, N_DEV - 1):
                pushes[slot].wait_recv()

            def add(s, carry):
                r = pl.ds(pl.multiple_of(s * chunk, chunk), chunk)
                total = acc[r, :]
                for slot in range(3, N_DEV - 1):
                    total = total + land[slot, r, :].astype(F32)
                gin_ref[r, :] = total
                return carry

            lax.fori_loop(0, chunks, add, 0)
            for cp in pushes:
                cp.wait_send()
            rs.finish([glu_ref], rs_scratch, [gglu_ref])
            ag.finish([small_ref], [gath_ref], ag_sems)

    any_spec = pl.BlockSpec(memory_space=pl.ANY)
    s = jax.ShapeDtypeStruct
    return pl.pallas_call(
        body, name="dw_in", grid=(N_DEV, nk),
        out_shape=[s((D_MODEL, COL_W), F32), s(glu_parts.shape[1:], F32), s((N_DEV,) + small.shape, F32)],
        in_specs=[pl.BlockSpec((tk, D_MODEL), lambda i, k: (k, 0)),
                  pl.BlockSpec((tk, COL_W), lambda i, k: (k, target(i))), any_spec, any_spec],
        out_specs=[_full((D_MODEL, COL_W)), _full(glu_parts.shape[1:]), any_spec],
        scratch_shapes=[pltpu.VMEM((D_MODEL, COL_W), F32), pltpu.VMEM((N_DEV - 1, D_MODEL, COL_W), BF16),
                        pltpu.VMEM((N_DEV - 1, D_MODEL, COL_W), BF16), pltpu.SemaphoreType.DMA((N_DEV - 1,)),
                        pltpu.SemaphoreType.DMA((N_DEV - 1,))] + rs.scratch(BF16) + ag.scratch(),
        compiler_params=_params(2, VMEM_LIMIT_V7X),
    )(xn, dz, glu_parts, small)


SMALL = ("mix_norm", "q_norm", "k_norm", "lambda_re", "lambda_im", "log_dt", "b_re", "b_im", "c_re", "c_im",
         "d_skip", "b_glu", "ple_norm")
BIG = ("w_in", "w_glu", "w_out", "w_ple_gate", "w_ple_proj")
WEIGHTS = ("mix_norm", "w_in", "q_norm", "k_norm", "lambda_re", "lambda_im", "log_dt", "b_re", "b_im", "c_re",
           "c_im", "d_skip", "w_glu", "b_glu", "w_out", "ple_norm", "w_ple_gate", "w_ple_proj")


def _pack(arrs):
    flat = jnp.concatenate([a.reshape(-1).astype(F32) for a in arrs])
    rows = -(-flat.shape[0] // (64 * LANES)) * 64
    return jnp.pad(flat, (0, rows * LANES - flat.shape[0])).reshape(rows, LANES)


def _unpack(packed, shapes):
    flat = packed.reshape(-1)
    out, off = [], 0
    for shp in shapes:
        size = math.prod(shp)
        out.append(flat[off:off + size].reshape(shp))
        off += size
    return out


def kernel(x, p, mix_norm, w_in, q_norm, k_norm, lambda_re, lambda_im, log_dt, b_re, b_im, c_re, c_im, d_skip, w_glu, b_glu, w_out, ple_norm, w_ple_gate, w_ple_proj, loss_target, m_mix_norm, m_w_in, m_q_norm, m_k_norm, m_lambda_re, m_lambda_im, m_log_dt, m_b_re, m_b_im, m_c_re, m_c_im, m_d_skip, m_w_glu, m_b_glu, m_w_out, m_ple_norm, m_w_ple_gate, m_w_ple_proj, v_mix_norm, v_w_in, v_q_norm, v_k_norm, v_lambda_re, v_lambda_im, v_log_dt, v_b_re, v_b_im, v_c_re, v_c_im, v_d_skip, v_w_glu, v_b_glu, v_w_out, v_ple_norm, v_w_ple_gate, v_w_ple_proj):
    env = dict(locals())
    w = {n: env[n] for n in WEIGHTS}
    m = {n: env["m_" + n] for n in WEIGHTS}
    v = {n: env["v_" + n] for n in WEIGHTS}
    nb, seq, _ = x.shape
    t_tok = nb * seq
    x2 = x.reshape(t_tok, D_MODEL)
    tg2 = loss_target.reshape(t_tok, D_MODEL)
    p2 = p.reshape(t_tok, PLE_DIM)

    shard2d = {"w_in": (D_MODEL, COL_W), "w_glu": (SSM_W // N_DEV, SSM_W), "w_out": (D_MODEL // N_DEV, D_MODEL),
               "w_ple_gate": (D_MODEL // N_DEV, D_MODEL), "w_ple_proj": (PLE_DIM, D_MODEL // N_DEV)}
    w_sh = [w[n].reshape(shard2d[n]) for n in BIG]

    g3 = (SSM_GROUPS, 1, SSM_STATE)
    lr3, li3 = lambda_re.reshape(g3), lambda_im.reshape(g3)
    dt3 = log_dt.reshape(SSM_GROUPS, 1, 1)
    btr = b_re[0].transpose(0, 2, 1)
    bti = b_im[0].transpose(0, 2, 1)
    a_re3, a_im3, bbr, bbi = _zoh_fwd(lr3, li3, dt3, btr, bti)
    a_re, a_im = a_re3.reshape(1, N_STATE), a_im3.reshape(1, N_STATE)
    bb_re, bb_im = _blockdiag(bbr).astype(BF16), _blockdiag(bbi).astype(BF16)
    cc_re = _blockdiag(c_re[0].transpose(0, 2, 1)).astype(BF16)
    cc_im = _blockdiag(c_im[0].transpose(0, 2, 1)).astype(BF16)

    ones_bd = _head_ones()
    fold = jnp.tile(jnp.eye(HEAD_DIM, dtype=F32), (ATTN_W // HEAD_DIM, 1))
    gq_t = jnp.tile(q_norm, (1, ATTN_W // HEAD_DIM))
    gk_t = jnp.tile(k_norm, (1, ATTN_W // HEAD_DIM))

    gq2 = jnp.tile(q_norm, (1, LANES // HEAD_DIM))
    gk2 = jnp.tile(k_norm, (1, LANES // HEAD_DIM))

    z, xn, w_in_g = _in_proj(x2, mix_norm, w_sh[0])
    (o, lse, ag), (w_glu_g, w_out_g, w_g_g, w_p_g) = _attn_fwd(z, gq2, gk2, nb, seq, w_sh[1:])
    w_glu_f = w_glu_g.reshape(SSM_W, SSM_W)
    w_out_f = w_out_g.reshape(D_MODEL, D_MODEL)
    w_g_f = w_g_g.reshape(D_MODEL, D_MODEL)
    x_re, x_im, y, sg = _ssm_fwd(z, a_re, a_im, bb_re, bb_im, cc_re, cc_im, d_skip, w_glu_f, b_glu, nb, seq)
    dmix, dh1, loss_t, d_ple, dw_out, dw_g, dw_p = _tail(x2, tg2, ag, sg, p2, w_out_f, w_g_f, w_p_g, ple_norm)

    early_parts = [dw_out.reshape(N_DEV, D_MODEL // N_DEV, D_MODEL), dw_g.reshape(N_DEV, D_MODEL // N_DEV, D_MODEL),
                   dw_p]
    (dqh, dkh, dvb, dga), (g_out, g_g, g_p) = _attn_bwd(z, gq2, gk2, o, lse, dmix, nb, seq, early_parts)
    (du, dgs, dw_glu, d_bglu, d_dskip, da_re, da_im, dbb_re, dbb_im, dcc_re, dcc_im) = _ssm_bwd(
        z, dmix, y, x_re, x_im, a_re, a_im, bb_re, bb_im, cc_re, cc_im, d_skip, w_glu_f, b_glu, nb, seq)
    dz, gx, d_mix, d_gq, d_gk = _dz_and_dx(x2, z, dqh, dkh, dvb, dga, du, dgs, dh1, w_in_g, mix_norm, gq_t, gk_t,
                                           ones_bd, fold)
    d_lr, d_li, d_dt, d_btr, d_bti = _zoh_bwd(
        lr3, li3, dt3, btr, bti, da_re.reshape(g3), da_im.reshape(g3),
        _blockdiag_extract(dbb_re, SSM_GROUP, SSM_STATE), _blockdiag_extract(dbb_im, SSM_GROUP, SSM_STATE))
    small_g = {
        "mix_norm": d_mix, "q_norm": d_gq[0:1], "k_norm": d_gk[0:1], "lambda_re": d_lr, "lambda_im": d_li,
        "log_dt": d_dt, "b_re": d_btr.transpose(0, 2, 1), "b_im": d_bti.transpose(0, 2, 1),
        "c_re": _blockdiag_extract(dcc_re, SSM_STATE, SSM_GROUP).transpose(0, 2, 1),
        "c_im": _blockdiag_extract(dcc_im, SSM_STATE, SSM_GROUP).transpose(0, 2, 1),
        "d_skip": d_dskip, "b_glu": d_bglu, "ple_norm": d_ple}

    g_in, g_glu, gathered = _dw_in(xn, dz, dw_glu.reshape(N_DEV, SSM_W // N_DEV, SSM_W),
                                   _pack([small_g[n] for n in SMALL] + [loss_t[0:1, 0:1]]))
    g_sh = [g_in, g_glu, g_out, g_g, g_p]
    d_sh, m_sh, v_sh = _adamw_shards(g_sh, w_sh, [m[n].reshape(shard2d[n]) for n in BIG],
                                     [v[n].reshape(shard2d[n]) for n in BIG])

    g_pk, d_pk, m_pk, v_pk = _small_update(gathered, _pack([w[n] for n in SMALL]), _pack([m[n] for n in SMALL]),
                                           _pack([v[n] for n in SMALL]))

    grads, deltas, new_m, new_v = {}, {}, {}, {}
    small_shapes = [w[n].shape for n in SMALL]
    for dst, packed in ((grads, g_pk), (deltas, d_pk), (new_m, m_pk), (new_v, v_pk)):
        for n, a in zip(SMALL, _unpack(packed, small_shapes)):
            dst[n] = a
    for i, n in enumerate(BIG):
        grads[n] = g_sh[i].reshape(w[n].shape)
        deltas[n] = d_sh[i].reshape(w[n].shape)
        new_m[n] = m_sh[i].reshape(w[n].shape)
        new_v[n] = v_sh[i].reshape(w[n].shape)

    loss = _unpack(g_pk, small_shapes + [()])[-1]
    return (loss, gx.reshape(x.shape), *[grads[n] for n in WEIGHTS], *[deltas[n] for n in WEIGHTS],
            *[new_m[n] for n in WEIGHTS], *[new_v[n] for n in WEIGHTS])
```

```python
import math

import numpy as np
import jax
import jax.numpy as jnp
from jax import lax
from jax.experimental import pallas as pl
from jax.experimental.pallas import tpu as pltpu

F32 = jnp.float32
BF16 = jnp.bfloat16
MESH = pl.DeviceIdType.MESH
AXES = ("x", "y", "c")
N_DEV = 8

D_MODEL = 1024
HEAD_DIM = 64
ATTN_W = 512
SSM_W = 512
SSM_GROUPS = 32
SSM_GROUP = 16
SSM_STATE = 64
N_STATE = SSM_GROUPS * SSM_STATE
PLE_DIM = 256
IN_W = 3072
COL_W = IN_W // N_DEV
DILATED = ((128, 1), (512, 4), (2048, 16))
EPS = 1e-6
INV_SQRT2 = 1.0 / math.sqrt(2.0)
INV_SQRT_2PI = 1.0 / math.sqrt(2.0 * math.pi)

ADAM_LR, ADAM_B1, ADAM_B2, ADAM_EPS, ADAM_WD, ADAM_STEP = 0.001, 0.9, 0.999, 1e-08, 0.01, 10

VMEM_LIMIT_V7X = 56 * 1024 * 1024
SUBLANES = 8
LANES = 128


def _params(n_axes=None, vmem=None):
    kw = {}
    if n_axes:
        kw["dimension_semantics"] = ("arbitrary",) * n_axes
    if vmem:
        kw["vmem_limit_bytes"] = vmem
    return pltpu.CompilerParams(**kw)


def _dot(a, b):
    return jnp.dot(a, b, preferred_element_type=F32)


def _dot_nt(a, b):
    return lax.dot_general(a, b, (((1,), (1,)), ((), ())), preferred_element_type=F32)


def _dot_tn(a, b):
    return lax.dot_general(a, b, (((0,), (0,)), ((), ())), preferred_element_type=F32)


def _hdot(a, ones):
    hi = a.astype(BF16)
    lo = (a - hi.astype(F32)).astype(BF16)
    return _dot(hi, ones) + _dot(lo, ones)


def _sig(x):
    return 1.0 / (1.0 + jnp.exp(-x))


def _gelu_and_grad(y):
    cdf = 0.5 * (1.0 + lax.erf(y * INV_SQRT2))
    pdf = jnp.exp(-0.5 * y * y) * INV_SQRT_2PI
    return y * cdf, cdf + y * pdf


def _vmem():
    return pl.BlockSpec(memory_space=pltpu.VMEM)


def _full(shape):
    nd = len(shape)
    return pl.BlockSpec(shape, lambda *_: (0,) * nd)


class _AllGather:
    def __init__(self, n, cast):
        self.n, self.cast = n, cast

    def scratch(self):
        n = self.n
        return [pltpu.SemaphoreType.DMA((7 * n,)), pltpu.SemaphoreType.DMA((7 * n,)), pltpu.SemaphoreType.DMA((n,))]

    def _plan(self, src_refs, out_refs, sems):
        send_sems, recv_sems, own_sems = sems
        x, y, c = lax.axis_index("x"), lax.axis_index("y"), lax.axis_index("c")
        me, sibling = (x, y, c), (x, y, 1 - c)
        chips = [(1 - x, y), (x, 1 - y), (1 - x, 1 - y)]

        def idx(px, py, pc):
            return 4 * px + 2 * py + pc

        def copy(i, k, block, to, own_src=False):
            ref = out_refs[i].at[idx(*block)]
            return pltpu.make_async_remote_copy(
                src_ref=src_refs[i] if own_src and not self.cast else ref, dst_ref=ref,
                send_sem=send_sems.at[7 * i + k], recv_sem=recv_sems.at[7 * i + k],
                device_id=to, device_id_type=MESH)

        first, passed, arrive_ici, arrive_d2d, own = [], [], [], [], []
        for i in range(self.n):
            first.append(copy(i, 0, me, sibling, own_src=True))
            first += [copy(i, 1 + j, me, (*chip, c), own_src=True) for j, chip in enumerate(chips)]
            arrive_ici += [copy(i, 1 + j, (*chip, c), me) for j, chip in enumerate(chips)]
            passed += [copy(i, 4 + j, (*chip, c), sibling) for j, chip in enumerate(chips)]
            arrive_d2d.append(copy(i, 0, sibling, me))
            arrive_d2d += [copy(i, 4 + j, (*chip, 1 - c), me) for j, chip in enumerate(chips)]
            if not self.cast:
                own.append(pltpu.make_async_copy(src_refs[i], out_refs[i].at[idx(*me)], own_sems.at[i]))
        return idx(*me), first, passed, arrive_ici, arrive_d2d, own

    def start(self, src_refs, out_refs, sems):
        my, first, _, _, _, own = self._plan(src_refs, out_refs, sems)
        if self.cast:
            for i in range(self.n):
                out_refs[i][my] = src_refs[i][...].astype(out_refs[i].dtype)
        for cp in own + first:
            cp.start()

    def forward(self, src_refs, out_refs, sems):
        _, _, passed, arrive_ici, _, _ = self._plan(src_refs, out_refs, sems)
        for cp in arrive_ici:
            cp.wait_recv()
        for cp in passed:
            cp.start()

    def finish(self, src_refs, out_refs, sems):
        _, first, passed, _, arrive_d2d, own = self._plan(src_refs, out_refs, sems)
        for cp in own:
            cp.wait()
        for cp in arrive_d2d:
            cp.wait_recv()
        for cp in first + passed:
            cp.wait_send()


class _HostedGather:
    def __init__(self, shards):
        self.shapes = [(N_DEV,) + a.shape for a in shards]
        self.n = len(shards)
        self.ag = _AllGather(self.n, cast=True)

    def out_shape(self):
        return [jax.ShapeDtypeStruct(s, BF16) for s in self.shapes]

    def scratch(self):
        return [pltpu.VMEM(s, BF16) for s in self.shapes] + self.ag.scratch() + [pltpu.SemaphoreType.DMA((self.n,))]

    def _split(self, scratch):
        return scratch[:self.n], scratch[self.n:-1], scratch[-1]

    def start(self, src_refs, scratch):
        land, sems, _ = self._split(scratch)
        self.ag.start(src_refs, land, sems)

    def forward(self, src_refs, scratch):
        land, sems, _ = self._split(scratch)
        self.ag.forward(src_refs, land, sems)

    def finish(self, src_refs, scratch, out_refs):
        land, sems, out_sems = self._split(scratch)
        self.ag.finish(src_refs, land, sems)
        outs = [pltpu.make_async_copy(land[n], out_refs[n], out_sems.at[n]) for n in range(self.n)]
        for cp in outs:
            cp.start()
        for cp in outs:
            cp.wait()


def _all_gather(shards, out_dtypes, name):
    n = len(shards)
    ag = _AllGather(n, cast=True)

    def body(*refs):
        in_refs, out_refs, sems = refs[:n], refs[n:2 * n], refs[2 * n:]
        ag.start(in_refs, out_refs, sems)
        ag.forward(in_refs, out_refs, sems)
        ag.finish(in_refs, out_refs, sems)

    return pl.pallas_call(
        body, name=name,
        out_shape=[jax.ShapeDtypeStruct((N_DEV,) + s.shape, dt) for s, dt in zip(shards, out_dtypes)],
        in_specs=[_vmem()] * n, out_specs=[_vmem()] * n,
        scratch_shapes=ag.scratch(),
        compiler_params=_params(vmem=VMEM_LIMIT_V7X),
    )(*shards)


def _row_chunks(rows):
    chunk = 64 if rows % 64 == 0 else rows
    return chunk, rows // chunk


class _ReduceScatter:
    def __init__(self, shapes):
        self.shapes = shapes
        self.n = len(shapes)

    def scratch(self, dtype):
        return ([pltpu.VMEM(s, dtype) for s in self.shapes]
                + [pltpu.SemaphoreType.DMA((7 * self.n,)), pltpu.SemaphoreType.DMA((7 * self.n,)),
                   pltpu.SemaphoreType.DMA((self.n,))])

    def _copies(self, in_refs, land_refs, send_sems, recv_sems, own_sems):
        x, y, c = lax.axis_index("x"), lax.axis_index("y"), lax.axis_index("c")
        remote, own = [], []
        for i in range(self.n):
            for m in range(1, N_DEV):
                px = 1 - x if m & 4 else x
                py = 1 - y if m & 2 else y
                pc = 1 - c if m & 1 else c
                remote.append(pltpu.make_async_remote_copy(
                    src_ref=in_refs[i].at[4 * px + 2 * py + pc], dst_ref=land_refs[i].at[m - 1],
                    send_sem=send_sems.at[7 * i + m - 1], recv_sem=recv_sems.at[7 * i + m - 1],
                    device_id=(px, py, pc), device_id_type=MESH))
            own.append(pltpu.make_async_copy(in_refs[i].at[4 * x + 2 * y + c], land_refs[i].at[N_DEV - 1],
                                             own_sems.at[i]))
        return remote, own

    def start(self, in_refs, scratch):
        remote, own = self._copies(in_refs, scratch[:self.n], *scratch[self.n:])
        for cp in remote + own:
            cp.start()

    def finish(self, in_refs, scratch, out_refs):
        land_refs = scratch[:self.n]
        remote, own = self._copies(in_refs, land_refs, *scratch[self.n:])
        for cp in own:
            cp.wait()
        for cp in remote:
            cp.wait_recv()
        for i in range(self.n):
            chunk, steps = _row_chunks(self.shapes[i][1])

            def step(s, carry, i=i, chunk=chunk):
                r = pl.ds(pl.multiple_of(s * chunk, chunk), chunk)
                acc = land_refs[i][N_DEV - 1, r, :].astype(F32)
                for m in range(1, N_DEV):
                    acc = acc + land_refs[i][m - 1, r, :].astype(F32)
                out_refs[i][r, :] = acc
                return carry

            lax.fori_loop(0, steps, step, 0)
        for cp in remote:
            cp.wait_send()


def _reduce_scatter(parts, name):
    n = len(parts)
    rs = _ReduceScatter([p.shape for p in parts])

    def body(*refs):
        in_refs, out_refs, scratch = refs[:n], refs[n:2 * n], refs[2 * n:]
        rs.start(in_refs, scratch)
        rs.finish(in_refs, scratch, out_refs)

    return pl.pallas_call(
        body, name=name,
        out_shape=[jax.ShapeDtypeStruct(p.shape[1:], F32) for p in parts],
        in_specs=[_vmem()] * n, out_specs=[_vmem()] * n,
        scratch_shapes=rs.scratch(parts[0].dtype),
        compiler_params=_params(vmem=VMEM_LIMIT_V7X),
    )(*parts)


def _adamw_math(w, g, m, v):
    m = ADAM_B1 * m + (1.0 - ADAM_B1) * g
    v = ADAM_B2 * v + (1.0 - ADAM_B2) * (g * g)
    m_hat = m / (1.0 - ADAM_B1 ** ADAM_STEP)
    v_hat = v / (1.0 - ADAM_B2 ** ADAM_STEP)
    delta = -ADAM_LR * (m_hat / (jnp.sqrt(v_hat) + ADAM_EPS) + ADAM_WD * w)
    return delta, m, v


def _adamw_shards(gs, ws, ms, vs):
    n = len(gs)

    def body(*refs):
        g_refs, w_refs, m_refs, v_refs = (refs[k * n:(k + 1) * n] for k in range(4))
        d_out, m_out, v_out = (refs[(4 + k) * n:(5 + k) * n] for k in range(3))
        for i in range(n):
            chunk, steps = _row_chunks(gs[i].shape[0])

            def step(s, carry, i=i, chunk=chunk):
                r = pl.ds(pl.multiple_of(s * chunk, chunk), chunk)
                d, m, v = _adamw_math(w_refs[i][r, :], g_refs[i][r, :], m_refs[i][r, :], v_refs[i][r, :])
                d_out[i][r, :] = d
                m_out[i][r, :] = m
                v_out[i][r, :] = v
                return carry

            lax.fori_loop(0, steps, step, 0)

    shapes = [jax.ShapeDtypeStruct(g.shape, F32) for g in gs]
    outs = pl.pallas_call(
        body, name="adamw_shards", out_shape=shapes * 3,
        in_specs=[_vmem()] * (4 * n), out_specs=[_vmem()] * (3 * n),
        compiler_params=_params(vmem=VMEM_LIMIT_V7X),
    )(*gs, *ws, *ms, *vs)
    return outs[:n], outs[n:2 * n], outs[2 * n:]


def _small_sum(gathered):
    rows = gathered.shape[1]
    chunk, steps = _row_chunks(rows)

    def body(ga_ref, g_out):
        def step(s, carry):
            r = pl.ds(pl.multiple_of(s * chunk, chunk), chunk)
            g = ga_ref[0, r, :]
            for j in range(1, N_DEV):
                g = g + ga_ref[j, r, :]
            g_out[r, :] = g
            return carry

        lax.fori_loop(0, steps, step, 0)

    return pl.pallas_call(
        body, name="small_sum", out_shape=jax.ShapeDtypeStruct(gathered.shape[1:], F32),
        in_specs=[_vmem()], out_specs=_vmem(),
    )(gathered)


def _adamw_small(gs, ws, ms, vs):
    n = len(gs)

    def body(*refs):
        g_refs, w_refs, m_refs, v_refs = (refs[k * n:(k + 1) * n] for k in range(4))
        d_out, m_out, v_out = (refs[(4 + k) * n:(5 + k) * n] for k in range(3))
        for i in range(n):
            def update(idx, i=i):
                d, mm, vv = _adamw_math(w_refs[i][idx], g_refs[i][idx], m_refs[i][idx], v_refs[i][idx])
                d_out[i][idx] = d
                m_out[i][idx] = mm
                v_out[i][idx] = vv

            if len(gs[i].shape) == 3:
                def step(s, carry, update=update):
                    update(s)
                    return carry

                lax.fori_loop(0, gs[i].shape[0], step, 0)
            else:
                update(Ellipsis)

    shapes = [jax.ShapeDtypeStruct(g.shape, F32) for g in gs]
    outs = pl.pallas_call(
        body, name="adamw_small", out_shape=shapes * 3,
        in_specs=[_vmem()] * (4 * n), out_specs=[_vmem()] * (3 * n),
        compiler_params=_params(vmem=VMEM_LIMIT_V7X),
    )(*gs, *ws, *ms, *vs)
    return outs[:n], outs[n:2 * n], outs[2 * n:]


def _zoh(lr, li, logdt, btr, bti):
    dt = jnp.exp(logdt)
    mag = jnp.exp(lr * dt)
    th = li * dt
    ar = mag * jnp.cos(th)
    ai = mag * jnp.sin(th)
    den = lr * lr + li * li
    nr = ar - 1.0
    cr = (nr * lr + ai * li) / den
    ci = (ai * lr - nr * li) / den
    return ar, ai, cr * btr - ci * bti, cr * bti + ci * btr


def _zoh_fwd(lr, li, logdt, btr, bti):
    def body(lr_ref, li_ref, dt_ref, br_ref, bi_ref, ar_ref, ai_ref, bbr_ref, bbi_ref):
        ar, ai, bbr, bbi = _zoh(lr_ref[...], li_ref[...], dt_ref[...], br_ref[...], bi_ref[...])
        ar_ref[...] = ar
        ai_ref[...] = ai
        bbr_ref[...] = bbr
        bbi_ref[...] = bbi

    s = jax.ShapeDtypeStruct
    return pl.pallas_call(
        body, name="zoh_fwd",
        out_shape=[s(lr.shape, F32), s(lr.shape, F32), s(btr.shape, F32), s(btr.shape, F32)],
        in_specs=[_vmem()] * 5, out_specs=[_vmem()] * 4,
    )(lr, li, logdt, btr, bti)


def _zoh_bwd(lr, li, logdt, btr, bti, dar, dai, dbbr, dbbi):
    def body(lr_ref, li_ref, dt_ref, br_ref, bi_ref, dar_ref, dai_ref, dbbr_ref, dbbi_ref,
             glr_ref, gli_ref, gdt_ref, gbr_ref, gbi_ref):
        _, vjp = jax.vjp(_zoh, lr_ref[...], li_ref[...], dt_ref[...], br_ref[...], bi_ref[...])
        glr, gli, gdt, gbr, gbi = vjp((dar_ref[...], dai_ref[...], dbbr_ref[...], dbbi_ref[...]))
        glr_ref[...] = glr
        gli_ref[...] = gli
        gdt_ref[...] = gdt
        gbr_ref[...] = gbr
        gbi_ref[...] = gbi

    s = jax.ShapeDtypeStruct
    return pl.pallas_call(
        body, name="zoh_bwd",
        out_shape=[s(lr.shape, F32), s(lr.shape, F32), s(logdt.shape, F32), s(btr.shape, F32), s(btr.shape, F32)],
        in_specs=[_vmem()] * 9, out_specs=[_vmem()] * 5,
    )(lr, li, logdt, btr, bti, dar, dai, dbbr, dbbi)


def _blockdiag(t):
    g, r, s = t.shape
    t = t.reshape(4, 8, r, s)
    out = jnp.einsum("jirs,ik->jirks", t, jnp.eye(8, dtype=t.dtype))
    return out.reshape(4, 8 * r, 8 * s)


def _blockdiag_extract(m, r, s):
    m = m.reshape(4, 8, r, 8, s)
    out = jnp.einsum("jirks,ik->jirs", m, jnp.eye(8, dtype=m.dtype))
    return out.reshape(32, r, s)


def _head_ones():
    r = np.arange(ATTN_W) // HEAD_DIM
    return jnp.asarray(r[:, None] == r[None, :], dtype=BF16)


def _head_fold():
    return jnp.asarray(np.tile(np.eye(HEAD_DIM), (ATTN_W // HEAD_DIM, 1)), dtype=BF16)


def _in_proj(x2, g_mix, w_in_sh):
    t_tok = x2.shape[0]
    tm = min(1024, t_tok)
    nt = t_tok // tm
    ag_w = _AllGather(1, cast=True)
    n_sem = len(ag_w.scratch())

    def owner(i):
        x, y, c = lax.axis_index("x"), lax.axis_index("y"), lax.axis_index("c")
        rel = jnp.where(i < 2, 0, (i - 2) % 3 + 1)
        px = jnp.where((rel == 1) | (rel == 3), 1 - x, x)
        py = jnp.where((rel == 2) | (rel == 3), 1 - y, y)
        pc = jnp.where((i == 1) | (i >= 5), 1 - c, c)
        return 4 * px + 2 * py + pc

    def body(*refs):
        x_ref, g_ref, w_ref, z_ref, xn_ref, wg_ref, xn_scr, w_land = refs[:8]
        sems_w, out_sem = refs[8:8 + n_sem], refs[8 + n_sem]
        i, t = pl.program_id(0), pl.program_id(1)
        _, first, passed, arrive_ici, arrive_d2d, _ = ag_w._plan([w_ref], [w_land], sems_w)

        @pl.when((i == 0) & (t == 0))
        def _():
            ag_w.start([w_ref], [w_land], sems_w)

        @pl.when((i == 1) & (t == 0))
        def _():
            arrive_d2d[0].wait_recv()

        for n in range(3):
            @pl.when((i == 2 + n) & (t == 0))
            def _(n=n):
                arrive_ici[n].wait_recv()
                passed[n].start()

            @pl.when((i == 5 + n) & (t == 0))
            def _(n=n):
                arrive_d2d[1 + n].wait_recv()

        @pl.when(i == 0)
        def _():
            x = x_ref[...]
            r = lax.rsqrt(jnp.mean(x * x, axis=-1, keepdims=True) + EPS)
            xn = (x * r * g_ref[...]).astype(BF16)
            xn_ref[...] = xn
            xn_scr[t] = xn

        z_ref[...] = _dot(xn_scr[t], w_land[owner(i)])

        @pl.when((i == N_DEV - 1) & (t == nt - 1))
        def _():
            for cp in first + passed:
                cp.wait_send()
            out = pltpu.make_async_copy(w_land, wg_ref, out_sem)
            out.start()
            out.wait()

    s = jax.ShapeDtypeStruct
    xmap = lambda i, t: (jnp.where(i == 0, t, nt - 1), 0)
    gathered = s((N_DEV,) + w_in_sh.shape, BF16)
    return pl.pallas_call(
        body, name="in_proj", grid=(N_DEV, nt),
        out_shape=[s((t_tok, IN_W), F32), s((t_tok, D_MODEL), BF16), gathered],
        in_specs=[pl.BlockSpec((tm, D_MODEL), xmap), _full(g_mix.shape), _full(w_in_sh.shape)],
        out_specs=[pl.BlockSpec((tm, COL_W), lambda i, t: (t, owner(i))), pl.BlockSpec((tm, D_MODEL), xmap),
                   pl.BlockSpec(memory_space=pl.ANY)],
        scratch_shapes=[pltpu.VMEM((nt, tm, D_MODEL), BF16), pltpu.VMEM(gathered.shape, BF16)] + ag_w.scratch()
        + [pltpu.SemaphoreType.DMA],
        compiler_params=_params(2, VMEM_LIMIT_V7X),
    )(x2, g_mix, w_in_sh)


TQ = 128
NEG = -1e30


def _head_col(t, lm):
    return jnp.max(jnp.where(lm, t, NEG), axis=-1, keepdims=True)


def _head_masks():
    lane = lax.broadcasted_iota(jnp.int32, (1, 1, LANES), 2)
    return [(lane // HEAD_DIM) == h for h in range(LANES // HEAD_DIM)]


def _gather_classes(ref, dil, nt, tq, dtype):
    length = nt * tq
    if dil == 1:
        return ref[...].astype(dtype).reshape(nt, tq, LANES)
    parts = [ref[pl.ds(r, length, stride=dil), :].astype(dtype).reshape(nt, tq, LANES) for r in range(dil)]
    return jnp.concatenate(parts, axis=0)


def _scatter_classes(ref, val, dil, nt, tq, add):
    length = nt * tq
    for r in range(dil):
        rows = pl.ds(r, length, stride=dil) if dil > 1 else slice(None)
        part = val[r * nt:(r + 1) * nt].reshape(length, LANES)
        ref[rows, :] = ref[rows, :] + part if add else part


def _with_prev_tile(t3, dil, nt):
    parts = []
    for r in range(dil):
        t = t3[r * nt:(r + 1) * nt]
        parts.append(jnp.concatenate([t[:1], t[:-1]], axis=0))
    prev = parts[0] if dil == 1 else jnp.concatenate(parts, axis=0)
    return jnp.concatenate([prev, t3], axis=1)


def _band_valid(dil, nt, tq):
    if nt == 1:
        shape = (dil, tq, tq)
        return lax.broadcasted_iota(jnp.int32, shape, 1) >= lax.broadcasted_iota(jnp.int32, shape, 2)
    shape = (dil * nt, tq, 2 * tq)
    b = lax.broadcasted_iota(jnp.int32, shape, 0)
    c = lax.broadcasted_iota(jnp.int32, shape, 2)
    d = tq + lax.broadcasted_iota(jnp.int32, shape, 1) - c
    return (d >= 0) & (d <= tq) & (((b & (nt - 1)) != 0) | (c >= tq))


def _window_tiling(seq, window, dil):
    length = seq // dil
    tq = min(TQ, length)
    nt = length // tq
    assert length % tq == 0 and nt & (nt - 1) == 0 and (nt == 1 or window == tq * dil)
    return nt, tq


def _bqk(a, b):
    return jnp.einsum("bqd,bkd->bqk", a, b, preferred_element_type=F32)


def _bqd(a, b):
    return jnp.einsum("bqk,bkd->bqd", a, b, preferred_element_type=F32)


def _bkd(a, b):
    return jnp.einsum("bqk,bqd->bkd", a, b, preferred_element_type=F32)


def _qk_hat(q_ref, k_ref, gq_ref, gk_ref):
    lane = lax.broadcasted_iota(jnp.int32, (1, LANES), 1)

    def norm(raw, gain, scale):
        sq = raw * raw
        r = jnp.zeros_like(raw)
        for h in range(LANES // HEAD_DIM):
            lm = (lane // HEAD_DIM) == h
            ms = jnp.sum(jnp.where(lm, sq, 0.0), axis=-1, keepdims=True) * (1.0 / HEAD_DIM)
            r = jnp.where(lm, lax.rsqrt(ms + EPS), r)
        return raw * r * gain * scale

    return norm(q_ref[...], gq_ref[...], HEAD_DIM ** -0.5), norm(k_ref[...], gk_ref[...], 1.0)


def _zblock(seq, group):
    return pl.BlockSpec((seq, LANES), lambda b, hp: (b, group * (ATTN_W // LANES) + hp))


def _attn_fwd(z, gq2, gk2, nb, seq, late_sh):
    t_tok = nb * seq
    n_win = len(DILATED)
    host = _HostedGather(late_sh)
    n_late = host.n
    n_steps = (nb, ATTN_W // LANES)

    def body(*refs):
        (q_ref, k_ref, v_ref, ga_ref, gq_ref, gk_ref), refs = refs[:6], refs[6:]
        late_refs, refs = refs[:n_late], refs[n_late:]
        (o_ref, l_ref, ag_ref), refs = refs[:3], refs[3:]
        lateg_refs, refs = refs[:n_late], refs[n_late:]
        (qf, kf, oc, lc), host_scratch = refs[:4], refs[4:]
        step = pl.program_id(0) * n_steps[1] + pl.program_id(1)
        total = n_steps[0] * n_steps[1]

        @pl.when(step == 0)
        def _():
            host.start(late_refs, host_scratch)

        @pl.when(step == total // 2)
        def _():
            host.forward(late_refs, host_scratch)

        qf[...], kf[...] = _qk_hat(q_ref, k_ref, gq_ref, gk_ref)
        lms = _head_masks()
        for w, (window, dil) in enumerate(DILATED):
            nt, tq = _window_tiling(seq, window, dil)
            q3 = _gather_classes(qf, dil, nt, tq, BF16)
            k3 = _gather_classes(kf, dil, nt, tq, BF16)
            v3 = _gather_classes(v_ref, dil, nt, tq, BF16)
            if nt > 1:
                k3, v3 = _with_prev_tile(k3, dil, nt), _with_prev_tile(v3, dil, nt)
            valid = _band_valid(dil, nt, tq)
            o = jnp.zeros(q3.shape, F32)
            lse = jnp.zeros(q3.shape, F32)
            for lm in lms:
                s = _bqk(jnp.where(lm, q3, jnp.zeros_like(q3)), k3)
                m = jnp.max(jnp.where(valid, s, NEG), axis=-1, keepdims=True)
                p = jnp.where(valid, jnp.exp(s - m), 0.0)
                den = jnp.sum(p, axis=-1, keepdims=True)
                o = jnp.where(lm, _bqd(p.astype(BF16), v3) / den, o)
                lse = jnp.where(lm, m + jnp.log(den), lse)
            _scatter_classes(oc.at[w], o, dil, nt, tq, add=False)
            _scatter_classes(lc.at[w], lse, dil, nt, tq, add=False)
        mx = lc[0]
        for w in range(1, n_win):
            mx = jnp.maximum(mx, lc[w])
        tot = jnp.zeros_like(mx)
        o = jnp.zeros_like(mx)
        for w in range(n_win):
            e = jnp.exp(lc[w] - mx)
            tot = tot + e
            o = o + e * oc[w]
        o = o / tot
        o_ref[...] = o
        l_ref[...] = mx + jnp.log(tot)
        ga = ga_ref[...]
        ag_ref[...] = (o * ga * _sig(ga)).astype(BF16)

        @pl.when(step == total - 1)
        def _():
            host.finish(late_refs, host_scratch, lateg_refs)

    blk = pl.BlockSpec((seq, LANES), lambda b, hp: (b, hp))
    s = jax.ShapeDtypeStruct
    outs = pl.pallas_call(
        body, name="attn_fwd", grid=n_steps,
        out_shape=[s((t_tok, ATTN_W), F32), s((t_tok, ATTN_W), F32), s((t_tok, ATTN_W), BF16)] + host.out_shape(),
        in_specs=[_zblock(seq, 0), _zblock(seq, 1), _zblock(seq, 2), _zblock(seq, 3), _full(gq2.shape),
                  _full(gk2.shape)] + [_full(a.shape) for a in late_sh],
        out_specs=[blk, blk, blk] + [pl.BlockSpec(memory_space=pl.ANY)] * n_late,
        scratch_shapes=[pltpu.VMEM((seq, LANES), F32)] * 2 + [pltpu.VMEM((n_win, seq, LANES), F32)] * 2
        + host.scratch(),
        compiler_params=_params(2, VMEM_LIMIT_V7X),
    )(z, z, z, z, gq2, gk2, *late_sh)
    return outs[:3], outs[3:]


SCAN_COLS = 512


def _to_segments(dst_ref, val):
    seg = val.shape[0] // SUBLANES
    for n in range(dst_ref.shape[0]):
        for s in range(SUBLANES):
            dst_ref[n, pl.ds(s, seg, stride=SUBLANES), :] = val[s * seg:(s + 1) * seg, n * LANES:(n + 1) * LANES]


def _from_segments(src_ref):
    seg = src_ref.shape[1] // SUBLANES
    return jnp.concatenate(
        [jnp.concatenate([src_ref[n, pl.ds(s, seg, stride=SUBLANES), :] for s in range(SUBLANES)], axis=0)
         for n in range(src_ref.shape[0])], axis=1)


def _scan_chunk(re_ref, im_ref, a_re_ref, a_im_ref, carry_re, carry_im, rows, reverse, visit=None):
    seg = rows // SUBLANES
    assert seg & (seg - 1) == 0
    rowi = lax.broadcasted_iota(jnp.int32, (SUBLANES, SCAN_COLS), 0)
    edge = (SUBLANES - 1) if reverse else 0
    last = 0 if reverse else SUBLANES - 1
    at_edge = rowi == edge

    def cmul(ar, ai, br, bi):
        return ar * br - ai * bi, ar * bi + ai * br

    for c0 in range(0, N_STATE, SCAN_COLS):
        cols = slice(c0, c0 + SCAN_COLS)
        a1r = jnp.broadcast_to(a_re_ref[:, cols], (SUBLANES, SCAN_COLS))
        a1i = jnp.broadcast_to(a_im_ref[:, cols], (SUBLANES, SCAN_COLS))
        if reverse:
            a1i = -a1i

        def block_of(i):
            j = (seg - 1 - i) if reverse else i
            return j, pl.ds(pl.multiple_of(j * SUBLANES, SUBLANES), SUBLANES)

        def local(i, carry, cols=cols, a1r=a1r, a1i=a1i):
            xr, xi = carry
            _, blk = block_of(i)
            nr, ni = cmul(a1r, a1i, xr, xi)
            xr, xi = nr + re_ref[blk, cols], ni + im_ref[blk, cols]
            re_ref[blk, cols] = xr
            im_ref[blk, cols] = xi
            return xr, xi

        zero = jnp.zeros((SUBLANES, SCAN_COLS), F32)
        er, ei = lax.fori_loop(0, seg, local, (zero, zero))

        pr, pi = a1r, a1i
        for _ in range(seg.bit_length() - 1):
            pr, pi = cmul(pr, pi, pr, pi)
        cr, ci = carry_re[:, cols], carry_im[:, cols]
        inr, ini = cmul(pr, pi, cr, ci)
        er = er + jnp.where(at_edge, inr, 0.0)
        ei = ei + jnp.where(at_edge, ini, 0.0)
        for sft in (1, 2, 4):
            shift, keep = (SUBLANES - sft, rowi < SUBLANES - sft) if reverse else (sft, rowi >= sft)
            rs = jnp.where(keep, pltpu.roll(er, shift, 0), 0.0)
            ims = jnp.where(keep, pltpu.roll(ei, shift, 0), 0.0)
            dr, di = cmul(pr, pi, rs, ims)
            er, ei = er + dr, ei + di
            pr, pi = cmul(pr, pi, pr, pi)
        carry_re[:, cols] = jnp.broadcast_to(er[last:last + 1, :], (SUBLANES, SCAN_COLS))
        carry_im[:, cols] = jnp.broadcast_to(ei[last:last + 1, :], (SUBLANES, SCAN_COLS))
        one = (SUBLANES - 1) if reverse else 1
        kr = jnp.where(at_edge, cr, pltpu.roll(er, one, 0))
        ki = jnp.where(at_edge, ci, pltpu.roll(ei, one, 0))

        def fix(i, carry, cols=cols, a1r=a1r, a1i=a1i):
            kr, ki, acc = carry
            j, blk = block_of(i)
            kr, ki = cmul(a1r, a1i, kr, ki)
            xr, xi = re_ref[blk, cols] + kr, im_ref[blk, cols] + ki
            re_ref[blk, cols] = xr
            im_ref[blk, cols] = xi
            if visit is not None:
                acc = visit(cols, j, xr, xi, acc)
            return kr, ki, acc

        _, _, acc = lax.fori_loop(0, seg, fix, (kr, ki, (zero, zero)))
        if visit is not None:
            visit(cols, None, None, None, acc)


def _ssm_fwd(z, a_re, a_im, bb_re, bb_im, cc_re, cc_im, d_skip, w_glu, b_glu, nb, seq):
    t_tok = nb * seq
    tc = min(256, seq)
    nch = seq // tc
    grp = N_STATE // 4

    def body(u_ref, gs_ref, ar_ref, ai_ref, bbr_ref, bbi_ref, ccr_ref, cci_ref, d_ref, wg_ref, bg_ref,
             xr_ref, xi_ref, y_ref, sg_ref, car_re, car_im, seg_u, seg_y):
        @pl.when(pl.program_id(1) == 0)
        def _():
            car_re[...] = jnp.zeros_like(car_re)
            car_im[...] = jnp.zeros_like(car_im)

        u = u_ref[...]
        _to_segments(seg_u, u)
        for j in range(4):
            uj = seg_u[j].astype(BF16)
            xr_ref[:, j * grp:(j + 1) * grp] = _dot(uj, bbr_ref[j])
            xi_ref[:, j * grp:(j + 1) * grp] = _dot(uj, bbi_ref[j])
        _scan_chunk(xr_ref, xi_ref, ar_ref, ai_ref, car_re, car_im, tc, reverse=False)
        for j in range(4):
            xr = xr_ref[:, j * grp:(j + 1) * grp].astype(BF16)
            xi = xi_ref[:, j * grp:(j + 1) * grp].astype(BF16)
            seg_y[j] = _dot(xr, ccr_ref[j]) - _dot(xi, cci_ref[j])
        y = _from_segments(seg_y) + d_ref[...] * u
        y_ref[...] = y
        yg, _ = _gelu_and_grad(y)
        gl = _dot(yg.astype(BF16), wg_ref[...]) + bg_ref[...]
        gs = gs_ref[...]
        sg_ref[...] = (yg * _sig(gl) * gs * _sig(gs)).astype(BF16)

    umap = lambda b, ch: (b * nch + ch, 4)
    gmap = lambda b, ch: (b * nch + ch, 5)
    row = lambda b, ch: (b * nch + ch, 0)
    s = jax.ShapeDtypeStruct
    consts = [a_re, a_im, bb_re, bb_im, cc_re, cc_im, d_skip, w_glu, b_glu]
    return pl.pallas_call(
        body, name="ssm_fwd", grid=(nb, nch),
        out_shape=[s((t_tok, N_STATE), F32), s((t_tok, N_STATE), F32), s((t_tok, SSM_W), F32),
                   s((t_tok, SSM_W), BF16)],
        in_specs=[pl.BlockSpec((tc, SSM_W), umap), pl.BlockSpec((tc, SSM_W), gmap)] + [_full(c.shape) for c in consts],
        out_specs=[pl.BlockSpec((tc, N_STATE), row), pl.BlockSpec((tc, N_STATE), row),
                   pl.BlockSpec((tc, SSM_W), row), pl.BlockSpec((tc, SSM_W), row)],
        scratch_shapes=[pltpu.VMEM((SUBLANES, N_STATE), F32), pltpu.VMEM((SUBLANES, N_STATE), F32),
                        pltpu.VMEM((4, tc, LANES), F32), pltpu.VMEM((4, tc, LANES), F32)],
        compiler_params=_params(2, VMEM_LIMIT_V7X),
    )(z, z, *consts)


def _tail(x2, tg2, ag, sg, p2, w_out, w_g, w_p, g_ple):
    t_tok = x2.shape[0]
    tm = min(256, t_tok)
    nt = t_tok // tm
    half = ATTN_W

    def body(x_ref, tg_ref, ag_ref, sg_ref, p_ref, wo_ref, wg_ref, wp_ref, gp_ref,
             dmix_ref, dh1_ref, loss_ref, dgp_ref, dwo_ref, dwg_ref, dwp_ref, acc_o, acc_g, acc_p):
        i = pl.program_id(0)

        @pl.when(i == 0)
        def _():
            loss_ref[...] = jnp.zeros_like(loss_ref)
            dgp_ref[...] = jnp.zeros_like(dgp_ref)
            acc_o[...] = jnp.zeros_like(acc_o)
            acc_g[...] = jnp.zeros_like(acc_g)
            acc_p[...] = jnp.zeros_like(acc_p)

        ag_t, sg_t = ag_ref[...], sg_ref[...]
        h1 = x_ref[...] + _dot(ag_t, wo_ref[0:half, :]) + _dot(sg_t, wo_ref[half:2 * half, :])
        r2 = lax.rsqrt(jnp.mean(h1 * h1, axis=-1, keepdims=True) + EPS)
        hnorm = h1 * r2
        gp = gp_ref[...]
        hn = (hnorm * gp).astype(BF16)
        gate = _sig(_dot(hn, wg_ref[...]))
        pb = p_ref[...].astype(BF16)
        pp = jnp.concatenate([_dot(pb, wp_ref[j]) for j in range(N_DEV)], axis=-1)
        h2 = h1 + gate * pp
        err = h2 - tg_ref[...]
        loss_ref[...] += 0.5 * jnp.sum(err * err) * (1.0 / D_MODEL)
        dh2 = err * (1.0 / D_MODEL)
        dpp = (dh2 * gate).astype(BF16)
        dgpre = (dh2 * pp * gate * (1.0 - gate)).astype(BF16)
        acc_p[...] += _dot_tn(pb, dpp)
        acc_g[...] += _dot_tn(hn, dgpre)
        dhn = _dot_nt(dgpre, wg_ref[...])
        dgp_ref[...] += jnp.sum(dhn * hnorm, axis=0, keepdims=True)
        a = dhn * gp
        dh1 = dh2 + r2 * (a - hnorm * jnp.mean(a * hnorm, axis=-1, keepdims=True))
        dh1_ref[...] = dh1
        dh1b = dh1.astype(BF16)
        acc_o[0:half, :] += _dot_tn(ag_t, dh1b)
        acc_o[half:2 * half, :] += _dot_tn(sg_t, dh1b)
        dmix_ref[...] = _dot_nt(dh1b, wo_ref[...])

        @pl.when(i == nt - 1)
        def _():
            dwo_ref[...] = acc_o[...].astype(BF16)
            dwg_ref[...] = acc_g[...].astype(BF16)
            for j in range(N_DEV):
                dwp_ref[j] = acc_p[:, j * LANES:(j + 1) * LANES].astype(BF16)

    row = lambda i: (i, 0)
    s = jax.ShapeDtypeStruct
    return pl.pallas_call(
        body, name="tail_fwd_bwd", grid=(nt,),
        out_shape=[s((t_tok, D_MODEL), F32), s((t_tok, D_MODEL), F32), s((SUBLANES, LANES), F32),
                   s((1, D_MODEL), F32), s((D_MODEL, D_MODEL), BF16), s((D_MODEL, D_MODEL), BF16),
                   s((N_DEV, PLE_DIM, LANES), BF16)],
        in_specs=[pl.BlockSpec((tm, D_MODEL), row), pl.BlockSpec((tm, D_MODEL), row),
                  pl.BlockSpec((tm, half), row), pl.BlockSpec((tm, half), row), pl.BlockSpec((tm, PLE_DIM), row),
                  _full(w_out.shape), _full(w_g.shape), _full(w_p.shape), _full(g_ple.shape)],
        out_specs=[pl.BlockSpec((tm, D_MODEL), row), pl.BlockSpec((tm, D_MODEL), row), _full((SUBLANES, LANES)),
                   _full((1, D_MODEL)), _full((D_MODEL, D_MODEL)), _full((D_MODEL, D_MODEL)),
                   _full((N_DEV, PLE_DIM, LANES))],
        scratch_shapes=[pltpu.VMEM((D_MODEL, D_MODEL), F32), pltpu.VMEM((D_MODEL, D_MODEL), F32),
                        pltpu.VMEM((PLE_DIM, D_MODEL), F32)],
        compiler_params=_params(1, VMEM_LIMIT_V7X),
    )(x2, tg2, ag, sg, p2, w_out, w_g, w_p, g_ple)


def _attn_bwd(z, gq2, gk2, o, lse, dmix, nb, seq, parts):
    t_tok = nb * seq
    n_rs = len(parts)
    rs = _ReduceScatter([p.shape for p in parts])
    n_steps = (nb, ATTN_W // LANES)

    def body(*refs):
        (q_ref, k_ref, v_ref, ga_ref, gq_ref, gk_ref, o_ref, l_ref, da_ref), refs = refs[:9], refs[9:]
        part_refs, refs = refs[:n_rs], refs[n_rs:]
        (dq_ref, dk_ref, dv_ref, dga_ref), refs = refs[:4], refs[4:]
        g_refs, refs = refs[:n_rs], refs[n_rs:]
        (qf, kf, dof, dlf), rs_scratch = refs[:4], refs[4:]
        b, hp = pl.program_id(0), pl.program_id(1)

        @pl.when((b == 0) & (hp == 0))
        def _():
            rs.start(part_refs, rs_scratch)

        ga, o_t, da = ga_ref[...], o_ref[...], da_ref[...]
        sga = _sig(ga)
        d_o = da * ga * sga
        dga_ref[...] = da * o_t * sga * (1.0 + ga * (1.0 - sga))
        lane = lax.broadcasted_iota(jnp.int32, (1, LANES), 1)
        d_oo = d_o * o_t
        delta = jnp.zeros_like(d_oo)
        for h in range(LANES // HEAD_DIM):
            lm2 = (lane // HEAD_DIM) == h
            delta = jnp.where(lm2, jnp.sum(jnp.where(lm2, d_oo, 0.0), axis=-1, keepdims=True), delta)
        qf[...], kf[...] = _qk_hat(q_ref, k_ref, gq_ref, gk_ref)
        dof[...] = d_o
        dlf[...] = delta
        dq_ref[...] = jnp.zeros_like(dq_ref)
        dk_ref[...] = jnp.zeros_like(dk_ref)
        dv_ref[...] = jnp.zeros_like(dv_ref)
        lms = _head_masks()
        for window, dil in DILATED:
            nt, tq = _window_tiling(seq, window, dil)
            q3 = _gather_classes(qf, dil, nt, tq, BF16)
            k3 = _gather_classes(kf, dil, nt, tq, BF16)
            v3 = _gather_classes(v_ref, dil, nt, tq, BF16)
            do3 = _gather_classes(dof, dil, nt, tq, BF16)
            lt3 = _gather_classes(l_ref, dil, nt, tq, F32)
            dl3 = _gather_classes(dlf, dil, nt, tq, F32)
            if nt > 1:
                k3, v3 = _with_prev_tile(k3, dil, nt), _with_prev_tile(v3, dil, nt)
            valid = _band_valid(dil, nt, tq)
            dq = jnp.zeros(q3.shape, F32)
            dk = jnp.zeros(k3.shape, F32)
            dv = jnp.zeros(k3.shape, F32)
            for lm in lms:
                qm = jnp.where(lm, q3, jnp.zeros_like(q3))
                dom = jnp.where(lm, do3, jnp.zeros_like(do3))
                p = jnp.where(valid, jnp.exp(_bqk(qm, k3) - _head_col(lt3, lm)), 0.0)
                dv = dv + _bkd(p.astype(BF16), dom)
                ds = (p * (_bqk(dom, v3) - _head_col(dl3, lm))).astype(BF16)
                dq = dq + jnp.where(lm, _bqd(ds, k3), 0.0)
                dk = dk + _bkd(ds, qm)
            _scatter_classes(dq_ref, dq, dil, nt, tq, add=True)
            for ref, g in ((dk_ref, dk), (dv_ref, dv)):
                if nt > 1:
                    own, prev = g[:, tq:, :], g[:, :tq, :]
                    shifted = []
                    for r in range(dil):
                        t = prev[r * nt:(r + 1) * nt]
                        shifted.append(jnp.concatenate([t[1:], jnp.zeros_like(t[:1])], axis=0))
                    g = own + (shifted[0] if dil == 1 else jnp.concatenate(shifted, axis=0))
                _scatter_classes(ref, g, dil, nt, tq, add=True)

        @pl.when((b == n_steps[0] - 1) & (hp == n_steps[1] - 1))
        def _():
            rs.finish(part_refs, rs_scratch, g_refs)

    blk = pl.BlockSpec((seq, LANES), lambda b, hp: (b, hp))
    s = jax.ShapeDtypeStruct
    outs = pl.pallas_call(
        body, name="attn_bwd", grid=n_steps,
        out_shape=[s((t_tok, ATTN_W), F32)] * 4 + [s(p.shape[1:], F32) for p in parts],
        in_specs=[_zblock(seq, 0), _zblock(seq, 1), _zblock(seq, 2), _zblock(seq, 3), _full(gq2.shape),
                  _full(gk2.shape), blk, blk, blk] + [pl.BlockSpec(memory_space=pl.ANY)] * n_rs,
        out_specs=[blk] * 4 + [_full(p.shape[1:]) for p in parts],
        scratch_shapes=[pltpu.VMEM((seq, LANES), F32)] * 4 + rs.scratch(parts[0].dtype),
        compiler_params=_params(2, VMEM_LIMIT_V7X),
    )(z, z, z, z, gq2, gk2, o, lse, dmix, *parts)
    return outs[:4], outs[4:]


def _ssm_bwd(z, dmix, y, x_re, x_im, a_re, a_im, bb_re, bb_im, cc_re, cc_im, d_skip, w_glu, b_glu, nb, seq):
    t_tok = nb * seq
    tc = min(256, seq)
    nch = seq // tc
    grp = N_STATE // 4

    def body(u_ref, gs_ref, ds_ref, y_ref, xr_ref, xi_ref, xpr_ref, xpi_ref,
             ar_ref, ai_ref, bbr_ref, bbi_ref, ccr_ref, cci_ref, d_ref, wg_ref, bg_ref,
             du_ref, dgs_ref, dwg_ref, dbg_ref, dd_ref, dar_ref, dai_ref, dbbr_ref, dbbi_ref, dccr_ref, dcci_ref,
             lam_re, lam_im, car_re, car_im, acc_wg, seg_a, seg_b, ent_re, ent_im):
        step = pl.program_id(1)
        first_chunk = step == nch - 1

        @pl.when((pl.program_id(0) == 0) & (step == 0))
        def _():
            acc_wg[...] = jnp.zeros_like(acc_wg)
            for ref in (dbg_ref, dd_ref, dar_ref, dai_ref, dbbr_ref, dbbi_ref, dccr_ref, dcci_ref):
                ref[...] = jnp.zeros_like(ref)

        @pl.when(step == 0)
        def _():
            car_re[...] = jnp.zeros_like(car_re)
            car_im[...] = jnp.zeros_like(car_im)

        u, gs, dssm, y = u_ref[...], gs_ref[...], ds_ref[...], y_ref[...]
        yg, dgelu = _gelu_and_grad(y)
        ygb = yg.astype(BF16)
        sgl = _sig(_dot(ygb, wg_ref[...]) + bg_ref[...])
        sgs = _sig(gs)
        dout = dssm * gs * sgs
        dgs_ref[...] = dssm * yg * sgl * sgs * (1.0 + gs * (1.0 - sgs))
        dgl = dout * yg * sgl * (1.0 - sgl)
        dglb = dgl.astype(BF16)
        dyg = dout * sgl + _dot_nt(dglb, wg_ref[...])
        acc_wg[...] += _dot_tn(ygb, dglb)
        dbg_ref[...] += jnp.sum(dgl, axis=0, keepdims=True)
        dy = dyg * dgelu
        dd_ref[...] += jnp.sum(dy * u, axis=0, keepdims=True)
        _to_segments(seg_a, dy)
        _to_segments(seg_b, u)
        for j in range(4):
            dyj = seg_a[j].astype(BF16)
            sl = slice(j * grp, (j + 1) * grp)
            lam_re[:, sl] = _dot_nt(dyj, ccr_ref[j])
            lam_im[:, sl] = -_dot_nt(dyj, cci_ref[j])
            dccr_ref[j] += _dot_tn(xr_ref[:, sl].astype(BF16), dyj)
            dcci_ref[j] -= _dot_tn(xi_ref[:, sl].astype(BF16), dyj)

        keep_prev = jnp.where(first_chunk, 0.0, 1.0)
        seg = tc // SUBLANES
        last_blk = pl.ds((seg - 1) * SUBLANES, SUBLANES)
        row0 = lax.broadcasted_iota(jnp.int32, (SUBLANES, N_STATE), 0) == 0
        for src, prev, dst in ((xr_ref, xpr_ref, ent_re), (xi_ref, xpi_ref, ent_im)):
            before = jnp.broadcast_to(prev[SUBLANES - 1:SUBLANES, :] * keep_prev, (SUBLANES, N_STATE))
            dst[...] = jnp.where(row0, before, pltpu.roll(src[last_blk, :], 1, 0))

        def visit(cols, j, lr, li, acc):
            if j is None:
                dar_ref[:, cols] += jnp.sum(acc[0], axis=0, keepdims=True)
                dai_ref[:, cols] += jnp.sum(acc[1], axis=0, keepdims=True)
                return None
            blk = pl.ds(pl.multiple_of(jnp.maximum(j - 1, 0) * SUBLANES, SUBLANES), SUBLANES)
            inside = j > 0
            xpr = jnp.where(inside, xr_ref[blk, cols], ent_re[:, cols])
            xpi = jnp.where(inside, xi_ref[blk, cols], ent_im[:, cols])
            return acc[0] + lr * xpr + li * xpi, acc[1] + li * xpr - lr * xpi

        _scan_chunk(lam_re, lam_im, ar_ref, ai_ref, car_re, car_im, tc, reverse=True, visit=visit)

        for j in range(4):
            sl = slice(j * grp, (j + 1) * grp)
            lr = lam_re[:, sl].astype(BF16)
            li = lam_im[:, sl].astype(BF16)
            uj = seg_b[j].astype(BF16)
            seg_a[j] = _dot_nt(lr, bbr_ref[j]) + _dot_nt(li, bbi_ref[j])
            dbbr_ref[j] += _dot_tn(uj, lr)
            dbbi_ref[j] += _dot_tn(uj, li)
        du_ref[...] = _from_segments(seg_a) + dy * d_ref[...]

        @pl.when((pl.program_id(0) == nb - 1) & (step == nch - 1))
        def _():
            dwg_ref[...] = acc_wg[...].astype(BF16)

    rev = lambda b, ch: b * nch + (nch - 1 - ch)
    umap = lambda b, ch: (rev(b, ch), 4)
    gmap = lambda b, ch: (rev(b, ch), 5)
    smap = lambda b, ch: (rev(b, ch), 1)
    row = lambda b, ch: (rev(b, ch), 0)
    prev = lambda b, ch: (jnp.maximum(rev(b, ch) * (tc // SUBLANES) - 1, 0), 0)
    s = jax.ShapeDtypeStruct
    consts = [a_re, a_im, bb_re, bb_im, cc_re, cc_im, d_skip, w_glu, b_glu]
    acc_shapes = [s((1, SSM_W), F32), s((1, SSM_W), F32), s((1, N_STATE), F32), s((1, N_STATE), F32),
                  s(bb_re.shape, F32), s(bb_re.shape, F32), s(cc_re.shape, F32), s(cc_re.shape, F32)]
    return pl.pallas_call(
        body, name="ssm_bwd", grid=(nb, nch),
        out_shape=[s((t_tok, SSM_W), F32), s((t_tok, SSM_W), F32), s((SSM_W, SSM_W), BF16)] + acc_shapes,
        in_specs=[pl.BlockSpec((tc, SSM_W), umap), pl.BlockSpec((tc, SSM_W), gmap), pl.BlockSpec((tc, SSM_W), smap),
                  pl.BlockSpec((tc, SSM_W), row), pl.BlockSpec((tc, N_STATE), row), pl.BlockSpec((tc, N_STATE), row),
                  pl.BlockSpec((SUBLANES, N_STATE), prev), pl.BlockSpec((SUBLANES, N_STATE), prev)]
        + [_full(c.shape) for c in consts],
        out_specs=[pl.BlockSpec((tc, SSM_W), row), pl.BlockSpec((tc, SSM_W), row), _full((SSM_W, SSM_W))]
        + [_full(a.shape) for a in acc_shapes],
        scratch_shapes=[pltpu.VMEM((tc, N_STATE), F32), pltpu.VMEM((tc, N_STATE), F32),
                        pltpu.VMEM((SUBLANES, N_STATE), F32), pltpu.VMEM((SUBLANES, N_STATE), F32),
                        pltpu.VMEM((SSM_W, SSM_W), F32), pltpu.VMEM((4, tc, LANES), F32),
                        pltpu.VMEM((4, tc, LANES), F32),
                        pltpu.VMEM((SUBLANES, N_STATE), F32), pltpu.VMEM((SUBLANES, N_STATE), F32)],
        compiler_params=_params(2, VMEM_LIMIT_V7X),
    )(z, z, dmix, y, x_re, x_im, x_re, x_im, *consts)


def _dz_and_dx(x2, z, dqh, dkh, dvb, dga, du, dgs, dh1, w_in_g, g_mix, gq_t, gk_t, ones_bd, fold):
    t_tok = x2.shape[0]
    tm = min(256, t_tok)
    nt = t_tok // tm
    a_w = ATTN_W

    def head_norm_bwd(raw, d_hat, gain, scale, ones):
        r = lax.rsqrt(_hdot(raw * raw, ones) * (1.0 / HEAD_DIM) + EPS)
        n = raw * r
        a = d_hat * gain * scale
        d_raw = r * (a - n * (_hdot(a * n, ones) * (1.0 / HEAD_DIM)))
        return d_raw, jnp.sum(d_hat * n * scale, axis=0, keepdims=True)

    def body(x_ref, q_ref, k_ref, dq_ref, dk_ref, dv_ref, dga_ref, du_ref, dgs_ref, dh1_ref, w_ref, g_ref,
             gq_ref, gk_ref, ones_ref, fold_ref, dz_ref, gx_ref, dgm_ref, dgq_ref, dgk_ref, acc_q, acc_k):
        i = pl.program_id(0)

        @pl.when(i == 0)
        def _():
            dgm_ref[...] = jnp.zeros_like(dgm_ref)
            acc_q[...] = jnp.zeros_like(acc_q)
            acc_k[...] = jnp.zeros_like(acc_k)

        ones = ones_ref[...]
        dq, sq = head_norm_bwd(q_ref[...], dq_ref[...], gq_ref[...], HEAD_DIM ** -0.5, ones)
        dk, sk = head_norm_bwd(k_ref[...], dk_ref[...], gk_ref[...], 1.0, ones)
        acc_q[...] += jnp.broadcast_to(sq, acc_q.shape)
        acc_k[...] += jnp.broadcast_to(sk, acc_k.shape)
        parts = (dq, dk, dv_ref[...], dga_ref[...], du_ref[...], dgs_ref[...])
        for n, part in enumerate(parts):
            dz_ref[:, n * a_w:(n + 1) * a_w] = part.astype(BF16)
        dxn = jnp.zeros((tm, D_MODEL), F32)
        for j in range(N_DEV):
            dxn = dxn + _dot_nt(dz_ref[:, j * COL_W:(j + 1) * COL_W], w_ref[j])
        x = x_ref[...]
        r1 = lax.rsqrt(jnp.mean(x * x, axis=-1, keepdims=True) + EPS)
        xnorm = x * r1
        dgm_ref[...] += jnp.sum(dxn * xnorm, axis=0, keepdims=True)
        a = dxn * g_ref[...]
        gx_ref[...] = dh1_ref[...] + r1 * (a - xnorm * jnp.mean(a * xnorm, axis=-1, keepdims=True))

        @pl.when(i == nt - 1)
        def _():
            dgq_ref[...] = _hdot(acc_q[...], fold_ref[...])
            dgk_ref[...] = _hdot(acc_k[...], fold_ref[...])

    row = lambda i: (i, 0)
    col = lambda n: (lambda i: (i, n))
    s = jax.ShapeDtypeStruct
    half = pl.BlockSpec((tm, a_w), row)
    return pl.pallas_call(
        body, name="dz_dx", grid=(nt,),
        out_shape=[s((t_tok, IN_W), BF16), s((t_tok, D_MODEL), F32), s((1, D_MODEL), F32),
                   s((SUBLANES, HEAD_DIM), F32), s((SUBLANES, HEAD_DIM), F32)],
        in_specs=[pl.BlockSpec((tm, D_MODEL), row), pl.BlockSpec((tm, a_w), col(0)), pl.BlockSpec((tm, a_w), col(1)),
                  half, half, half, half, half, half, pl.BlockSpec((tm, D_MODEL), row),
                  _full(w_in_g.shape), _full(g_mix.shape), _full(gq_t.shape), _full(gk_t.shape),
                  _full(ones_bd.shape), _full(fold.shape)],
        out_specs=[pl.BlockSpec((tm, IN_W), row), pl.BlockSpec((tm, D_MODEL), row), _full((1, D_MODEL)),
                   _full((SUBLANES, HEAD_DIM)), _full((SUBLANES, HEAD_DIM))],
        scratch_shapes=[pltpu.VMEM((SUBLANES, a_w), F32), pltpu.VMEM((SUBLANES, a_w), F32)],
        compiler_params=_params(1, VMEM_LIMIT_V7X),
    )(x2, z, z, dqh, dkh, dvb, dga, du, dgs, dh1, w_in_g, g_mix, gq_t, gk_t, ones_bd, fold)


def _dw_in(xn, dz, glu_parts, small):
    t_tok = xn.shape[0]
    tk = min(1024, t_tok)
    nk = t_tok // tk
    rs = _ReduceScatter([glu_parts.shape])
    ag = _AllGather(1, cast=False)
    n_rs = len(rs.scratch(BF16))

    def place():
        x, y, c = lax.axis_index("x"), lax.axis_index("y"), lax.axis_index("c")
        return x, y, c, [(1 - x, y), (x, 1 - y), (1 - x, 1 - y)]

    def target(i):
        x, y, c, _ = place()
        rel = jnp.where(i < 6, i // 2 + 1, 0)
        px = jnp.where((rel == 1) | (rel == 3), 1 - x, x)
        py = jnp.where((rel == 2) | (rel == 3), 1 - y, y)
        pc = jnp.where(i % 2 == 0, 1 - c, c)
        return 4 * px + 2 * py + pc

    chunk, chunks = _row_chunks(D_MODEL)

    def body(xn_ref, dz_ref, glu_ref, small_ref, gin_ref, gglu_ref, gath_ref, acc, stage, land, send_sems, recv_sems,
             *rest):
        rs_scratch, ag_sems = rest[:n_rs], rest[n_rs:]
        i, k = pl.program_id(0), pl.program_id(1)
        x, y, c, chips = place()

        def push(slot, to):
            return pltpu.make_async_remote_copy(
                src_ref=stage.at[slot], dst_ref=land.at[slot], send_sem=send_sems.at[slot],
                recv_sem=recv_sems.at[slot], device_id=to, device_id_type=MESH)

        pushes = [push(n, (x, y, 1 - c)) for n in range(4)] + [push(4 + n, (*chips[n], c)) for n in range(3)]

        def staged(slot, plus=None):
            def put(s, carry):
                r = pl.ds(pl.multiple_of(s * chunk, chunk), chunk)
                val = acc[r, :]
                if plus is not None:
                    val = val + land[plus, r, :].astype(F32)
                stage[slot, r, :] = val.astype(BF16)
                return carry

            lax.fori_loop(0, chunks, put, 0)

        @pl.when((i == 0) & (k == 0))
        def _():
            rs.start([glu_ref], rs_scratch)
            ag.start([small_ref], [gath_ref], ag_sems)

        @pl.when((i == N_DEV // 2) & (k == 0))
        def _():
            ag.forward([small_ref], [gath_ref], ag_sems)

        @pl.when(k == 0)
        def _():
            acc[...] = jnp.zeros_like(acc)

        acc[...] += _dot_tn(xn_ref[...], dz_ref[...])

        for n in range(4):
            @pl.when((k == nk - 1) & (i == 2 * n))
            def _(n=n):
                staged(n)
                pushes[n].start()

        for n in range(3):
            @pl.when((k == nk - 1) & (i == 2 * n + 1))
            def _(n=n):
                pushes[n].wait_recv()
                staged(4 + n, plus=n)
                pushes[4 + n].start()

        @pl.when((k == nk - 1) & (i == N_DEV - 1))
        def _():
            for slot in range(3, N_DEV - 1):
                pushes[slot].wait_recv()

            def add(s, carry):
                r = pl.ds(pl.multiple_of(s * chunk, chunk), chunk)
                total = acc[r, :]
                for slot in range(3, N_DEV - 1):
                    total = total + land[slot, r, :].astype(F32)
                gin_ref[r, :] = total
                return carry

            lax.fori_loop(0, chunks, add, 0)
            for cp in pushes:
                cp.wait_send()
            rs.finish([glu_ref], rs_scratch, [gglu_ref])
            ag.finish([small_ref], [gath_ref], ag_sems)

    any_spec = pl.BlockSpec(memory_space=pl.ANY)
    s = jax.ShapeDtypeStruct
    return pl.pallas_call(
        body, name="dw_in", grid=(N_DEV, nk),
        out_shape=[s((D_MODEL, COL_W), F32), s(glu_parts.shape[1:], F32), s((N_DEV,) + small.shape, F32)],
        in_specs=[pl.BlockSpec((tk, D_MODEL), lambda i, k: (k, 0)),
                  pl.BlockSpec((tk, COL_W), lambda i, k: (k, target(i))), any_spec, any_spec],
        out_specs=[_full((D_MODEL, COL_W)), _full(glu_parts.shape[1:]), any_spec],
        scratch_shapes=[pltpu.VMEM((D_MODEL, COL_W), F32), pltpu.VMEM((N_DEV - 1, D_MODEL, COL_W), BF16),
                        pltpu.VMEM((N_DEV - 1, D_MODEL, COL_W), BF16), pltpu.SemaphoreType.DMA((N_DEV - 1,)),
                        pltpu.SemaphoreType.DMA((N_DEV - 1,))] + rs.scratch(BF16) + ag.scratch(),
        compiler_params=_params(2, VMEM_LIMIT_V7X),
    )(xn, dz, glu_parts, small)


SMALL = ("mix_norm", "q_norm", "k_norm", "lambda_re", "lambda_im", "log_dt", "b_re", "b_im", "c_re", "c_im",
         "d_skip", "b_glu", "ple_norm")
BIG = ("w_in", "w_glu", "w_out", "w_ple_gate", "w_ple_proj")
WEIGHTS = ("mix_norm", "w_in", "q_norm", "k_norm", "lambda_re", "lambda_im", "log_dt", "b_re", "b_im", "c_re",
           "c_im", "d_skip", "w_glu", "b_glu", "w_out", "ple_norm", "w_ple_gate", "w_ple_proj")


def _pack(arrs):
    flat = jnp.concatenate([a.reshape(-1).astype(F32) for a in arrs])
    rows = -(-flat.shape[0] // (64 * LANES)) * 64
    return jnp.pad(flat, (0, rows * LANES - flat.shape[0])).reshape(rows, LANES)


def _unpack(packed, shapes):
    flat = packed.reshape(-1)
    out, off = [], 0
    for shp in shapes:
        size = math.prod(shp)
        out.append(flat[off:off + size].reshape(shp))
        off += size
    return out


def kernel(x, p, mix_norm, w_in, q_norm, k_norm, lambda_re, lambda_im, log_dt, b_re, b_im, c_re, c_im, d_skip, w_glu, b_glu, w_out, ple_norm, w_ple_gate, w_ple_proj, loss_target, m_mix_norm, m_w_in, m_q_norm, m_k_norm, m_lambda_re, m_lambda_im, m_log_dt, m_b_re, m_b_im, m_c_re, m_c_im, m_d_skip, m_w_glu, m_b_glu, m_w_out, m_ple_norm, m_w_ple_gate, m_w_ple_proj, v_mix_norm, v_w_in, v_q_norm, v_k_norm, v_lambda_re, v_lambda_im, v_log_dt, v_b_re, v_b_im, v_c_re, v_c_im, v_d_skip, v_w_glu, v_b_glu, v_w_out, v_ple_norm, v_w_ple_gate, v_w_ple_proj):
    env = dict(locals())
    w = {n: env[n] for n in WEIGHTS}
    m = {n: env["m_" + n] for n in WEIGHTS}
    v = {n: env["v_" + n] for n in WEIGHTS}
    nb, seq, _ = x.shape
    t_tok = nb * seq
    x2 = x.reshape(t_tok, D_MODEL)
    tg2 = loss_target.reshape(t_tok, D_MODEL)
    p2 = p.reshape(t_tok, PLE_DIM)

    shard2d = {"w_in": (D_MODEL, COL_W), "w_glu": (SSM_W // N_DEV, SSM_W), "w_out": (D_MODEL // N_DEV, D_MODEL),
               "w_ple_gate": (D_MODEL // N_DEV, D_MODEL), "w_ple_proj": (PLE_DIM, D_MODEL // N_DEV)}
    w_sh = [w[n].reshape(shard2d[n]) for n in BIG]

    g3 = (SSM_GROUPS, 1, SSM_STATE)
    lr3, li3 = lambda_re.reshape(g3), lambda_im.reshape(g3)
    dt3 = log_dt.reshape(SSM_GROUPS, 1, 1)
    btr = b_re[0].transpose(0, 2, 1)
    bti = b_im[0].transpose(0, 2, 1)
    a_re3, a_im3, bbr, bbi = _zoh_fwd(lr3, li3, dt3, btr, bti)
    a_re, a_im = a_re3.reshape(1, N_STATE), a_im3.reshape(1, N_STATE)
    bb_re, bb_im = _blockdiag(bbr).astype(BF16), _blockdiag(bbi).astype(BF16)
    cc_re = _blockdiag(c_re[0].transpose(0, 2, 1)).astype(BF16)
    cc_im = _blockdiag(c_im[0].transpose(0, 2, 1)).astype(BF16)

    ones_bd = _head_ones()
    fold = _head_fold()
    gq_t = jnp.tile(q_norm, (1, ATTN_W // HEAD_DIM))
    gk_t = jnp.tile(k_norm, (1, ATTN_W // HEAD_DIM))

    gq2 = jnp.tile(q_norm, (1, LANES // HEAD_DIM))
    gk2 = jnp.tile(k_norm, (1, LANES // HEAD_DIM))

    z, xn, w_in_g = _in_proj(x2, mix_norm, w_sh[0])
    (o, lse, ag), (w_glu_g, w_out_g, w_g_g, w_p_g) = _attn_fwd(z, gq2, gk2, nb, seq, w_sh[1:])
    w_glu_f = w_glu_g.reshape(SSM_W, SSM_W)
    w_out_f = w_out_g.reshape(D_MODEL, D_MODEL)
    w_g_f = w_g_g.reshape(D_MODEL, D_MODEL)
    x_re, x_im, y, sg = _ssm_fwd(z, a_re, a_im, bb_re, bb_im, cc_re, cc_im, d_skip, w_glu_f, b_glu, nb, seq)
    dmix, dh1, loss_t, d_ple, dw_out, dw_g, dw_p = _tail(x2, tg2, ag, sg, p2, w_out_f, w_g_f, w_p_g, ple_norm)

    early_parts = [dw_out.reshape(N_DEV, D_MODEL // N_DEV, D_MODEL), dw_g.reshape(N_DEV, D_MODEL // N_DEV, D_MODEL),
                   dw_p]
    (dqh, dkh, dvb, dga), (g_out, g_g, g_p) = _attn_bwd(z, gq2, gk2, o, lse, dmix, nb, seq, early_parts)
    (du, dgs, dw_glu, d_bglu, d_dskip, da_re, da_im, dbb_re, dbb_im, dcc_re, dcc_im) = _ssm_bwd(
        z, dmix, y, x_re, x_im, a_re, a_im, bb_re, bb_im, cc_re, cc_im, d_skip, w_glu_f, b_glu, nb, seq)
    dz, gx, d_mix, d_gq, d_gk = _dz_and_dx(x2, z, dqh, dkh, dvb, dga, du, dgs, dh1, w_in_g, mix_norm, gq_t, gk_t,
                                           ones_bd, fold)
    d_lr, d_li, d_dt, d_btr, d_bti = _zoh_bwd(
        lr3, li3, dt3, btr, bti, da_re.reshape(g3), da_im.reshape(g3),
        _blockdiag_extract(dbb_re, SSM_GROUP, SSM_STATE), _blockdiag_extract(dbb_im, SSM_GROUP, SSM_STATE))
    small_g = {
        "mix_norm": d_mix, "q_norm": d_gq[0:1], "k_norm": d_gk[0:1], "lambda_re": d_lr, "lambda_im": d_li,
        "log_dt": d_dt, "b_re": d_btr.transpose(0, 2, 1), "b_im": d_bti.transpose(0, 2, 1),
        "c_re": _blockdiag_extract(dcc_re, SSM_STATE, SSM_GROUP).transpose(0, 2, 1),
        "c_im": _blockdiag_extract(dcc_im, SSM_STATE, SSM_GROUP).transpose(0, 2, 1),
        "d_skip": d_dskip, "b_glu": d_bglu, "ple_norm": d_ple}

    g_in, g_glu, gathered = _dw_in(xn, dz, dw_glu.reshape(N_DEV, SSM_W // N_DEV, SSM_W),
                                   _pack([small_g[n] for n in SMALL] + [loss_t[0:1, 0:1]]))
    g_sh = [g_in, g_glu, g_out, g_g, g_p]
    d_sh, m_sh, v_sh = _adamw_shards(g_sh, w_sh, [m[n].reshape(shard2d[n]) for n in BIG],
                                     [v[n].reshape(shard2d[n]) for n in BIG])

    g_pk = _small_sum(gathered)
    small_shapes = [w[n].shape for n in SMALL]
    own = [s[1:] if len(s) > 2 else s for s in small_shapes]
    g_small = _unpack(g_pk, own)
    d_small, m_small, v_small = _adamw_small(
        g_small, *[[src[n].reshape(s) for n, s in zip(SMALL, own)] for src in (w, m, v)])

    grads, deltas, new_m, new_v = {}, {}, {}, {}
    for dst, arrs in ((grads, g_small), (deltas, d_small), (new_m, m_small), (new_v, v_small)):
        for n, a in zip(SMALL, arrs):
            dst[n] = a.reshape(w[n].shape)
    for i, n in enumerate(BIG):
        grads[n] = g_sh[i].reshape(w[n].shape)
        deltas[n] = d_sh[i].reshape(w[n].shape)
        new_m[n] = m_sh[i].reshape(w[n].shape)
        new_v[n] = v_sh[i].reshape(w[n].shape)

    loss = _unpack(g_pk, small_shapes + [()])[-1]
    return (loss, gx.reshape(x.shape), *[grads[n] for n in WEIGHTS], *[deltas[n] for n in WEIGHTS],
            *[new_m[n] for n in WEIGHTS], *[new_v[n] for n in WEIGHTS])
```

```python
import math

import numpy as np
import jax
import jax.numpy as jnp
from jax import lax
from jax.experimental import pallas as pl
from jax.experimental.pallas import tpu as pltpu

F32 = jnp.float32
BF16 = jnp.bfloat16
MESH = pl.DeviceIdType.MESH
AXES = ("x", "y", "c")
N_DEV = 8

D_MODEL = 1024
HEAD_DIM = 64
ATTN_W = 512
SSM_W = 512
SSM_GROUPS = 32
SSM_GROUP = 16
SSM_STATE = 64
N_STATE = SSM_GROUPS * SSM_STATE
PLE_DIM = 256
IN_W = 3072
COL_W = IN_W // N_DEV
DILATED = ((128, 1), (512, 4), (2048, 16))
EPS = 1e-6
INV_SQRT2 = 1.0 / math.sqrt(2.0)
INV_SQRT_2PI = 1.0 / math.sqrt(2.0 * math.pi)

ADAM_LR, ADAM_B1, ADAM_B2, ADAM_EPS, ADAM_WD, ADAM_STEP = 0.001, 0.9, 0.999, 1e-08, 0.01, 10

VMEM_LIMIT_V7X = 56 * 1024 * 1024
SUBLANES = 8
LANES = 128


def _params(n_axes=None, vmem=None):
    kw = {}
    if n_axes:
        kw["dimension_semantics"] = ("arbitrary",) * n_axes
    if vmem:
        kw["vmem_limit_bytes"] = vmem
    return pltpu.CompilerParams(**kw)


def _dot(a, b):
    return jnp.dot(a, b, preferred_element_type=F32)


def _dot_nt(a, b):
    return lax.dot_general(a, b, (((1,), (1,)), ((), ())), preferred_element_type=F32)


def _dot_tn(a, b):
    return lax.dot_general(a, b, (((0,), (0,)), ((), ())), preferred_element_type=F32)


def _hdot(a, ones):
    hi = a.astype(BF16)
    lo = (a - hi.astype(F32)).astype(BF16)
    return _dot(hi, ones) + _dot(lo, ones)


def _sig(x):
    return 1.0 / (1.0 + jnp.exp(-x))


def _gelu_and_grad(y):
    cdf = 0.5 * (1.0 + lax.erf(y * INV_SQRT2))
    pdf = jnp.exp(-0.5 * y * y) * INV_SQRT_2PI
    return y * cdf, cdf + y * pdf


def _vmem():
    return pl.BlockSpec(memory_space=pltpu.VMEM)


def _full(shape):
    nd = len(shape)
    return pl.BlockSpec(shape, lambda *_: (0,) * nd)


class _AllGather:
    def __init__(self, n, cast):
        self.n, self.cast = n, cast

    def scratch(self):
        n = self.n
        return [pltpu.SemaphoreType.DMA((7 * n,)), pltpu.SemaphoreType.DMA((7 * n,)), pltpu.SemaphoreType.DMA((n,))]

    def _plan(self, src_refs, out_refs, sems):
        send_sems, recv_sems, own_sems = sems
        x, y, c = lax.axis_index("x"), lax.axis_index("y"), lax.axis_index("c")
        me, sibling = (x, y, c), (x, y, 1 - c)
        chips = [(1 - x, y), (x, 1 - y), (1 - x, 1 - y)]

        def idx(px, py, pc):
            return 4 * px + 2 * py + pc

        def copy(i, k, block, to, own_src=False):
            ref = out_refs[i].at[idx(*block)]
            return pltpu.make_async_remote_copy(
                src_ref=src_refs[i] if own_src and not self.cast else ref, dst_ref=ref,
                send_sem=send_sems.at[7 * i + k], recv_sem=recv_sems.at[7 * i + k],
                device_id=to, device_id_type=MESH)

        first, passed, arrive_ici, arrive_d2d, own = [], [], [], [], []
        for i in range(self.n):
            first.append(copy(i, 0, me, sibling, own_src=True))
            first += [copy(i, 1 + j, me, (*chip, c), own_src=True) for j, chip in enumerate(chips)]
            arrive_ici += [copy(i, 1 + j, (*chip, c), me) for j, chip in enumerate(chips)]
            passed += [copy(i, 4 + j, (*chip, c), sibling) for j, chip in enumerate(chips)]
            arrive_d2d.append(copy(i, 0, sibling, me))
            arrive_d2d += [copy(i, 4 + j, (*chip, 1 - c), me) for j, chip in enumerate(chips)]
            if not self.cast:
                own.append(pltpu.make_async_copy(src_refs[i], out_refs[i].at[idx(*me)], own_sems.at[i]))
        return idx(*me), first, passed, arrive_ici, arrive_d2d, own

    def start(self, src_refs, out_refs, sems):
        my, first, _, _, _, own = self._plan(src_refs, out_refs, sems)
        if self.cast:
            for i in range(self.n):
                out_refs[i][my] = src_refs[i][...].astype(out_refs[i].dtype)
        for cp in own + first:
            cp.start()

    def forward(self, src_refs, out_refs, sems):
        _, _, passed, arrive_ici, _, _ = self._plan(src_refs, out_refs, sems)
        for cp in arrive_ici:
            cp.wait_recv()
        for cp in passed:
            cp.start()

    def finish(self, src_refs, out_refs, sems):
        _, first, passed, _, arrive_d2d, own = self._plan(src_refs, out_refs, sems)
        for cp in own:
            cp.wait()
        for cp in arrive_d2d:
            cp.wait_recv()
        for cp in first + passed:
            cp.wait_send()


class _HostedGather:
    def __init__(self, shards):
        self.shapes = [(N_DEV,) + a.shape for a in shards]
        self.n = len(shards)
        self.ag = _AllGather(self.n, cast=True)

    def out_shape(self):
        return [jax.ShapeDtypeStruct(s, BF16) for s in self.shapes]

    def scratch(self):
        return [pltpu.VMEM(s, BF16) for s in self.shapes] + self.ag.scratch() + [pltpu.SemaphoreType.DMA((self.n,))]

    def _split(self, scratch):
        return scratch[:self.n], scratch[self.n:-1], scratch[-1]

    def start(self, src_refs, scratch):
        land, sems, _ = self._split(scratch)
        self.ag.start(src_refs, land, sems)

    def forward(self, src_refs, scratch):
        land, sems, _ = self._split(scratch)
        self.ag.forward(src_refs, land, sems)

    def finish(self, src_refs, scratch, out_refs):
        land, sems, out_sems = self._split(scratch)
        self.ag.finish(src_refs, land, sems)
        outs = [pltpu.make_async_copy(land[n], out_refs[n], out_sems.at[n]) for n in range(self.n)]
        for cp in outs:
            cp.start()
        for cp in outs:
            cp.wait()


def _all_gather(shards, out_dtypes, name):
    n = len(shards)
    ag = _AllGather(n, cast=True)

    def body(*refs):
        in_refs, out_refs, sems = refs[:n], refs[n:2 * n], refs[2 * n:]
        ag.start(in_refs, out_refs, sems)
        ag.forward(in_refs, out_refs, sems)
        ag.finish(in_refs, out_refs, sems)

    return pl.pallas_call(
        body, name=name,
        out_shape=[jax.ShapeDtypeStruct((N_DEV,) + s.shape, dt) for s, dt in zip(shards, out_dtypes)],
        in_specs=[_vmem()] * n, out_specs=[_vmem()] * n,
        scratch_shapes=ag.scratch(),
        compiler_params=_params(vmem=VMEM_LIMIT_V7X),
    )(*shards)


def _row_chunks(rows):
    chunk = 64 if rows % 64 == 0 else rows
    return chunk, rows // chunk


class _ReduceScatter:
    def __init__(self, shapes):
        self.shapes = shapes
        self.n = len(shapes)

    def scratch(self, dtype):
        return ([pltpu.VMEM(s, dtype) for s in self.shapes]
                + [pltpu.SemaphoreType.DMA((7 * self.n,)), pltpu.SemaphoreType.DMA((7 * self.n,)),
                   pltpu.SemaphoreType.DMA((self.n,))])

    def _copies(self, in_refs, land_refs, send_sems, recv_sems, own_sems):
        x, y, c = lax.axis_index("x"), lax.axis_index("y"), lax.axis_index("c")
        remote, own = [], []
        for i in range(self.n):
            for m in range(1, N_DEV):
                px = 1 - x if m & 4 else x
                py = 1 - y if m & 2 else y
                pc = 1 - c if m & 1 else c
                remote.append(pltpu.make_async_remote_copy(
                    src_ref=in_refs[i].at[4 * px + 2 * py + pc], dst_ref=land_refs[i].at[m - 1],
                    send_sem=send_sems.at[7 * i + m - 1], recv_sem=recv_sems.at[7 * i + m - 1],
                    device_id=(px, py, pc), device_id_type=MESH))
            own.append(pltpu.make_async_copy(in_refs[i].at[4 * x + 2 * y + c], land_refs[i].at[N_DEV - 1],
                                             own_sems.at[i]))
        return remote, own

    def start(self, in_refs, scratch):
        remote, own = self._copies(in_refs, scratch[:self.n], *scratch[self.n:])
        for cp in remote + own:
            cp.start()

    def finish(self, in_refs, scratch, out_refs):
        land_refs = scratch[:self.n]
        remote, own = self._copies(in_refs, land_refs, *scratch[self.n:])
        for cp in own:
            cp.wait()
        for cp in remote:
            cp.wait_recv()
        for i in range(self.n):
            chunk, steps = _row_chunks(self.shapes[i][1])

            def step(s, carry, i=i, chunk=chunk):
                r = pl.ds(pl.multiple_of(s * chunk, chunk), chunk)
                acc = land_refs[i][N_DEV - 1, r, :].astype(F32)
                for m in range(1, N_DEV):
                    acc = acc + land_refs[i][m - 1, r, :].astype(F32)
                out_refs[i][r, :] = acc
                return carry

            lax.fori_loop(0, steps, step, 0)
        for cp in remote:
            cp.wait_send()


def _reduce_scatter(parts, name):
    n = len(parts)
    rs = _ReduceScatter([p.shape for p in parts])

    def body(*refs):
        in_refs, out_refs, scratch = refs[:n], refs[n:2 * n], refs[2 * n:]
        rs.start(in_refs, scratch)
        rs.finish(in_refs, scratch, out_refs)

    return pl.pallas_call(
        body, name=name,
        out_shape=[jax.ShapeDtypeStruct(p.shape[1:], F32) for p in parts],
        in_specs=[_vmem()] * n, out_specs=[_vmem()] * n,
        scratch_shapes=rs.scratch(parts[0].dtype),
        compiler_params=_params(vmem=VMEM_LIMIT_V7X),
    )(*parts)


def _adamw_math(w, g, m, v):
    m = ADAM_B1 * m + (1.0 - ADAM_B1) * g
    v = ADAM_B2 * v + (1.0 - ADAM_B2) * (g * g)
    m_hat = m / (1.0 - ADAM_B1 ** ADAM_STEP)
    v_hat = v / (1.0 - ADAM_B2 ** ADAM_STEP)
    delta = -ADAM_LR * (m_hat / (jnp.sqrt(v_hat) + ADAM_EPS) + ADAM_WD * w)
    return delta, m, v


def _adamw_shards(gs, ws, ms, vs):
    n = len(gs)

    def body(*refs):
        g_refs, w_refs, m_refs, v_refs = (refs[k * n:(k + 1) * n] for k in range(4))
        d_out, m_out, v_out = (refs[(4 + k) * n:(5 + k) * n] for k in range(3))
        for i in range(n):
            chunk, steps = _row_chunks(gs[i].shape[0])

            def step(s, carry, i=i, chunk=chunk):
                r = pl.ds(pl.multiple_of(s * chunk, chunk), chunk)
                d, m, v = _adamw_math(w_refs[i][r, :], g_refs[i][r, :], m_refs[i][r, :], v_refs[i][r, :])
                d_out[i][r, :] = d
                m_out[i][r, :] = m
                v_out[i][r, :] = v
                return carry

            lax.fori_loop(0, steps, step, 0)

    shapes = [jax.ShapeDtypeStruct(g.shape, F32) for g in gs]
    outs = pl.pallas_call(
        body, name="adamw_shards", out_shape=shapes * 3,
        in_specs=[_vmem()] * (4 * n), out_specs=[_vmem()] * (3 * n),
        compiler_params=_params(vmem=VMEM_LIMIT_V7X),
    )(*gs, *ws, *ms, *vs)
    return outs[:n], outs[n:2 * n], outs[2 * n:]


def _small_sum(gathered):
    rows = gathered.shape[1]
    chunk, steps = _row_chunks(rows)

    def body(ga_ref, g_out):
        def step(s, carry):
            r = pl.ds(pl.multiple_of(s * chunk, chunk), chunk)
            g = ga_ref[0, r, :]
            for j in range(1, N_DEV):
                g = g + ga_ref[j, r, :]
            g_out[r, :] = g
            return carry

        lax.fori_loop(0, steps, step, 0)

    return pl.pallas_call(
        body, name="small_sum", out_shape=jax.ShapeDtypeStruct(gathered.shape[1:], F32),
        in_specs=[_vmem()], out_specs=_vmem(),
    )(gathered)


def _adamw_small(gs, ws, ms, vs):
    n = len(gs)

    def body(*refs):
        g_refs, w_refs, m_refs, v_refs = (refs[k * n:(k + 1) * n] for k in range(4))
        d_out, m_out, v_out = (refs[(4 + k) * n:(5 + k) * n] for k in range(3))
        for i in range(n):
            def update(idx, i=i):
                d, mm, vv = _adamw_math(w_refs[i][idx], g_refs[i][idx], m_refs[i][idx], v_refs[i][idx])
                d_out[i][idx] = d
                m_out[i][idx] = mm
                v_out[i][idx] = vv

            if len(gs[i].shape) == 3:
                def step(s, carry, update=update):
                    update(s)
                    return carry

                lax.fori_loop(0, gs[i].shape[0], step, 0)
            else:
                update(Ellipsis)

    shapes = [jax.ShapeDtypeStruct(g.shape, F32) for g in gs]
    outs = pl.pallas_call(
        body, name="adamw_small", out_shape=shapes * 3,
        in_specs=[_vmem()] * (4 * n), out_specs=[_vmem()] * (3 * n),
        compiler_params=_params(vmem=VMEM_LIMIT_V7X),
    )(*gs, *ws, *ms, *vs)
    return outs[:n], outs[n:2 * n], outs[2 * n:]


def _zoh(lr, li, logdt, btr, bti):
    dt = jnp.exp(logdt)
    mag = jnp.exp(lr * dt)
    th = li * dt
    ar = mag * jnp.cos(th)
    ai = mag * jnp.sin(th)
    den = lr * lr + li * li
    nr = ar - 1.0
    cr = (nr * lr + ai * li) / den
    ci = (ai * lr - nr * li) / den
    return ar, ai, cr * btr - ci * bti, cr * bti + ci * btr


BD_GROUPS = 8
BD_ROWS = BD_GROUPS * SSM_GROUP
BD_COLS = BD_GROUPS * SSM_STATE
N_BD = SSM_GROUPS // BD_GROUPS


def _bd_mask():
    r = lax.broadcasted_iota(jnp.int32, (BD_ROWS, BD_COLS), 0) // SSM_GROUP
    c = lax.broadcasted_iota(jnp.int32, (BD_ROWS, BD_COLS), 1) // SSM_STATE
    return r == c


def _blockdiag_store(out_ref, t):
    mask = _bd_mask()
    for j in range(N_BD):
        rows = t[j * BD_GROUPS:(j + 1) * BD_GROUPS].reshape(BD_ROWS, SSM_STATE)
        out_ref[j] = jnp.where(mask, jnp.tile(rows, (1, BD_GROUPS)), 0.0).astype(out_ref.dtype)


def _blockdiag_load(m_ref, fold):
    mask = _bd_mask()
    parts = [_hdot(jnp.where(mask, m_ref[j], 0.0), fold).reshape(BD_GROUPS, SSM_GROUP, SSM_STATE)
             for j in range(N_BD)]
    return jnp.concatenate(parts, axis=0)


def _zoh_fwd(lr, li, logdt, btr, bti, c_re, c_im):
    def body(lr_ref, li_ref, dt_ref, br_ref, bi_ref, cr_ref, ci_ref, ar_ref, ai_ref, bbr_ref, bbi_ref, ccr_ref,
             cci_ref):
        ar, ai, bbr, bbi = _zoh(lr_ref[...], li_ref[...], dt_ref[...], br_ref[...], bi_ref[...])
        ar_ref[...] = ar
        ai_ref[...] = ai
        _blockdiag_store(bbr_ref, bbr)
        _blockdiag_store(bbi_ref, bbi)
        _blockdiag_store(ccr_ref, cr_ref[...])
        _blockdiag_store(cci_ref, ci_ref[...])

    s = jax.ShapeDtypeStruct
    bd = s((N_BD, BD_ROWS, BD_COLS), BF16)
    return pl.pallas_call(
        body, name="zoh_fwd", out_shape=[s(lr.shape, F32), s(lr.shape, F32), bd, bd, bd, bd],
        in_specs=[_vmem()] * 7, out_specs=[_vmem()] * 6,
    )(lr, li, logdt, btr, bti, c_re, c_im)


def _zoh_bwd(lr, li, logdt, btr, bti, dar, dai, dbb_re, dbb_im, dcc_re, dcc_im, fold):
    def body(lr_ref, li_ref, dt_ref, br_ref, bi_ref, dar_ref, dai_ref, dbbr_ref, dbbi_ref, dccr_ref, dcci_ref,
             fold_ref, glr_ref, gli_ref, gdt_ref, gbr_ref, gbi_ref, gcr_ref, gci_ref):
        fold_m = fold_ref[...]
        _, vjp = jax.vjp(_zoh, lr_ref[...], li_ref[...], dt_ref[...], br_ref[...], bi_ref[...])
        glr, gli, gdt, gbr, gbi = vjp((dar_ref[...], dai_ref[...], _blockdiag_load(dbbr_ref, fold_m),
                                       _blockdiag_load(dbbi_ref, fold_m)))
        glr_ref[...] = glr
        gli_ref[...] = gli
        gdt_ref[...] = gdt
        gbr_ref[...] = gbr
        gbi_ref[...] = gbi
        gcr_ref[...] = _blockdiag_load(dccr_ref, fold_m)
        gci_ref[...] = _blockdiag_load(dcci_ref, fold_m)

    s = jax.ShapeDtypeStruct
    return pl.pallas_call(
        body, name="zoh_bwd",
        out_shape=[s(lr.shape, F32), s(lr.shape, F32), s(logdt.shape, F32)] + [s(btr.shape, F32)] * 4,
        in_specs=[_vmem()] * 12, out_specs=[_vmem()] * 7,
    )(lr, li, logdt, btr, bti, dar, dai, dbb_re, dbb_im, dcc_re, dcc_im, fold)


def _head_ones():
    r = np.arange(ATTN_W) // HEAD_DIM
    return jnp.asarray(r[:, None] == r[None, :], dtype=BF16)


def _head_fold():
    return jnp.asarray(np.tile(np.eye(HEAD_DIM), (ATTN_W // HEAD_DIM, 1)), dtype=BF16)


def _in_proj(x2, g_mix, w_in_sh):
    t_tok = x2.shape[0]
    tm = min(1024, t_tok)
    nt = t_tok // tm
    ag_w = _AllGather(1, cast=True)
    n_sem = len(ag_w.scratch())

    def owner(i):
        x, y, c = lax.axis_index("x"), lax.axis_index("y"), lax.axis_index("c")
        rel = jnp.where(i < 2, 0, (i - 2) % 3 + 1)
        px = jnp.where((rel == 1) | (rel == 3), 1 - x, x)
        py = jnp.where((rel == 2) | (rel == 3), 1 - y, y)
        pc = jnp.where((i == 1) | (i >= 5), 1 - c, c)
        return 4 * px + 2 * py + pc

    def body(*refs):
        x_ref, g_ref, w_ref, z_ref, xn_ref, wg_ref, xn_scr, w_land = refs[:8]
        sems_w, out_sem = refs[8:8 + n_sem], refs[8 + n_sem]
        i, t = pl.program_id(0), pl.program_id(1)
        _, first, passed, arrive_ici, arrive_d2d, _ = ag_w._plan([w_ref], [w_land], sems_w)

        @pl.when((i == 0) & (t == 0))
        def _():
            ag_w.start([w_ref], [w_land], sems_w)

        @pl.when((i == 1) & (t == 0))
        def _():
            arrive_d2d[0].wait_recv()

        for n in range(3):
            @pl.when((i == 2 + n) & (t == 0))
            def _(n=n):
                arrive_ici[n].wait_recv()
                passed[n].start()

            @pl.when((i == 5 + n) & (t == 0))
            def _(n=n):
                arrive_d2d[1 + n].wait_recv()

        @pl.when(i == 0)
        def _():
            x = x_ref[...]
            r = lax.rsqrt(jnp.mean(x * x, axis=-1, keepdims=True) + EPS)
            xn = (x * r * g_ref[...]).astype(BF16)
            xn_ref[...] = xn
            xn_scr[t] = xn

        z_ref[...] = _dot(xn_scr[t], w_land[owner(i)])

        @pl.when((i == N_DEV - 1) & (t == nt - 1))
        def _():
            for cp in first + passed:
                cp.wait_send()
            out = pltpu.make_async_copy(w_land, wg_ref, out_sem)
            out.start()
            out.wait()

    s = jax.ShapeDtypeStruct
    xmap = lambda i, t: (jnp.where(i == 0, t, nt - 1), 0)
    gathered = s((N_DEV,) + w_in_sh.shape, BF16)
    return pl.pallas_call(
        body, name="in_proj", grid=(N_DEV, nt),
        out_shape=[s((t_tok, IN_W), F32), s((t_tok, D_MODEL), BF16), gathered],
        in_specs=[pl.BlockSpec((tm, D_MODEL), xmap), _full(g_mix.shape), _full(w_in_sh.shape)],
        out_specs=[pl.BlockSpec((tm, COL_W), lambda i, t: (t, owner(i))), pl.BlockSpec((tm, D_MODEL), xmap),
                   pl.BlockSpec(memory_space=pl.ANY)],
        scratch_shapes=[pltpu.VMEM((nt, tm, D_MODEL), BF16), pltpu.VMEM(gathered.shape, BF16)] + ag_w.scratch()
        + [pltpu.SemaphoreType.DMA],
        compiler_params=_params(2, VMEM_LIMIT_V7X),
    )(x2, g_mix, w_in_sh)


TQ = 128
NEG = -1e30


def _head_col(t, lm):
    return jnp.max(jnp.where(lm, t, NEG), axis=-1, keepdims=True)


def _head_masks():
    lane = lax.broadcasted_iota(jnp.int32, (1, 1, LANES), 2)
    return [(lane // HEAD_DIM) == h for h in range(LANES // HEAD_DIM)]


def _gather_classes(ref, dil, nt, tq, dtype):
    length = nt * tq
    if dil == 1:
        return ref[...].astype(dtype).reshape(nt, tq, LANES)
    parts = [ref[pl.ds(r, length, stride=dil), :].astype(dtype).reshape(nt, tq, LANES) for r in range(dil)]
    return jnp.concatenate(parts, axis=0)


def _scatter_classes(ref, val, dil, nt, tq, add):
    length = nt * tq
    for r in range(dil):
        rows = pl.ds(r, length, stride=dil) if dil > 1 else slice(None)
        part = val[r * nt:(r + 1) * nt].reshape(length, LANES)
        ref[rows, :] = ref[rows, :] + part if add else part


def _with_prev_tile(t3, dil, nt):
    parts = []
    for r in range(dil):
        t = t3[r * nt:(r + 1) * nt]
        parts.append(jnp.concatenate([t[:1], t[:-1]], axis=0))
    prev = parts[0] if dil == 1 else jnp.concatenate(parts, axis=0)
    return jnp.concatenate([prev, t3], axis=1)


def _band_valid(dil, nt, tq):
    if nt == 1:
        shape = (dil, tq, tq)
        return lax.broadcasted_iota(jnp.int32, shape, 1) >= lax.broadcasted_iota(jnp.int32, shape, 2)
    shape = (dil * nt, tq, 2 * tq)
    b = lax.broadcasted_iota(jnp.int32, shape, 0)
    c = lax.broadcasted_iota(jnp.int32, shape, 2)
    d = tq + lax.broadcasted_iota(jnp.int32, shape, 1) - c
    return (d >= 0) & (d <= tq) & (((b & (nt - 1)) != 0) | (c >= tq))


def _window_tiling(seq, window, dil):
    length = seq // dil
    tq = min(TQ, length)
    nt = length // tq
    assert length % tq == 0 and nt & (nt - 1) == 0 and (nt == 1 or window == tq * dil)
    return nt, tq


def _bqk(a, b):
    return jnp.einsum("bqd,bkd->bqk", a, b, preferred_element_type=F32)


def _bqd(a, b):
    return jnp.einsum("bqk,bkd->bqd", a, b, preferred_element_type=F32)


def _bkd(a, b):
    return jnp.einsum("bqk,bqd->bkd", a, b, preferred_element_type=F32)


def _qk_hat(q_ref, k_ref, gq_ref, gk_ref):
    lane = lax.broadcasted_iota(jnp.int32, (1, LANES), 1)

    def norm(raw, gain, scale):
        sq = raw * raw
        r = jnp.zeros_like(raw)
        for h in range(LANES // HEAD_DIM):
            lm = (lane // HEAD_DIM) == h
            ms = jnp.sum(jnp.where(lm, sq, 0.0), axis=-1, keepdims=True) * (1.0 / HEAD_DIM)
            r = jnp.where(lm, lax.rsqrt(ms + EPS), r)
        return raw * r * gain * scale

    return norm(q_ref[...], gq_ref[...], HEAD_DIM ** -0.5), norm(k_ref[...], gk_ref[...], 1.0)


def _zblock(seq, group):
    return pl.BlockSpec((seq, LANES), lambda b, hp: (b, group * (ATTN_W // LANES) + hp))


def _attn_fwd(z, gq2, gk2, nb, seq, late_sh):
    t_tok = nb * seq
    n_win = len(DILATED)
    host = _HostedGather(late_sh)
    n_late = host.n
    n_steps = (nb, ATTN_W // LANES)

    def body(*refs):
        (q_ref, k_ref, v_ref, ga_ref, gq_ref, gk_ref), refs = refs[:6], refs[6:]
        late_refs, refs = refs[:n_late], refs[n_late:]
        (o_ref, l_ref, ag_ref), refs = refs[:3], refs[3:]
        lateg_refs, refs = refs[:n_late], refs[n_late:]
        (qf, kf, oc, lc), host_scratch = refs[:4], refs[4:]
        step = pl.program_id(0) * n_steps[1] + pl.program_id(1)
        total = n_steps[0] * n_steps[1]

        @pl.when(step == 0)
        def _():
            host.start(late_refs, host_scratch)

        @pl.when(step == total // 2)
        def _():
            host.forward(late_refs, host_scratch)

        qf[...], kf[...] = _qk_hat(q_ref, k_ref, gq_ref, gk_ref)
        lms = _head_masks()
        for w, (window, dil) in enumerate(DILATED):
            nt, tq = _window_tiling(seq, window, dil)
            q3 = _gather_classes(qf, dil, nt, tq, BF16)
            k3 = _gather_classes(kf, dil, nt, tq, BF16)
            v3 = _gather_classes(v_ref, dil, nt, tq, BF16)
            if nt > 1:
                k3, v3 = _with_prev_tile(k3, dil, nt), _with_prev_tile(v3, dil, nt)
            valid = _band_valid(dil, nt, tq)
            o = jnp.zeros(q3.shape, F32)
            lse = jnp.zeros(q3.shape, F32)
            for lm in lms:
                s = _bqk(jnp.where(lm, q3, jnp.zeros_like(q3)), k3)
                m = jnp.max(jnp.where(valid, s, NEG), axis=-1, keepdims=True)
                p = jnp.where(valid, jnp.exp(s - m), 0.0)
                den = jnp.sum(p, axis=-1, keepdims=True)
                o = jnp.where(lm, _bqd(p.astype(BF16), v3) / den, o)
                lse = jnp.where(lm, m + jnp.log(den), lse)
            _scatter_classes(oc.at[w], o, dil, nt, tq, add=False)
            _scatter_classes(lc.at[w], lse, dil, nt, tq, add=False)
        mx = lc[0]
        for w in range(1, n_win):
            mx = jnp.maximum(mx, lc[w])
        tot = jnp.zeros_like(mx)
        o = jnp.zeros_like(mx)
        for w in range(n_win):
            e = jnp.exp(lc[w] - mx)
            tot = tot + e
            o = o + e * oc[w]
        o = o / tot
        o_ref[...] = o
        l_ref[...] = mx + jnp.log(tot)
        ga = ga_ref[...]
        ag_ref[...] = (o * ga * _sig(ga)).astype(BF16)

        @pl.when(step == total - 1)
        def _():
            host.finish(late_refs, host_scratch, lateg_refs)

    blk = pl.BlockSpec((seq, LANES), lambda b, hp: (b, hp))
    s = jax.ShapeDtypeStruct
    outs = pl.pallas_call(
        body, name="attn_fwd", grid=n_steps,
        out_shape=[s((t_tok, ATTN_W), F32), s((t_tok, ATTN_W), F32), s((t_tok, ATTN_W), BF16)] + host.out_shape(),
        in_specs=[_zblock(seq, 0), _zblock(seq, 1), _zblock(seq, 2), _zblock(seq, 3), _full(gq2.shape),
                  _full(gk2.shape)] + [_full(a.shape) for a in late_sh],
        out_specs=[blk, blk, blk] + [pl.BlockSpec(memory_space=pl.ANY)] * n_late,
        scratch_shapes=[pltpu.VMEM((seq, LANES), F32)] * 2 + [pltpu.VMEM((n_win, seq, LANES), F32)] * 2
        + host.scratch(),
        compiler_params=_params(2, VMEM_LIMIT_V7X),
    )(z, z, z, z, gq2, gk2, *late_sh)
    return outs[:3], outs[3:]


SCAN_COLS = 512


def _to_segments(dst_ref, val):
    seg = val.shape[0] // SUBLANES
    for n in range(dst_ref.shape[0]):
        for s in range(SUBLANES):
            dst_ref[n, pl.ds(s, seg, stride=SUBLANES), :] = val[s * seg:(s + 1) * seg, n * LANES:(n + 1) * LANES]


def _from_segments(src_ref):
    seg = src_ref.shape[1] // SUBLANES
    return jnp.concatenate(
        [jnp.concatenate([src_ref[n, pl.ds(s, seg, stride=SUBLANES), :] for s in range(SUBLANES)], axis=0)
         for n in range(src_ref.shape[0])], axis=1)


def _scan_chunk(re_ref, im_ref, a_re_ref, a_im_ref, carry_re, carry_im, rows, reverse, visit=None):
    seg = rows // SUBLANES
    assert seg & (seg - 1) == 0
    rowi = lax.broadcasted_iota(jnp.int32, (SUBLANES, SCAN_COLS), 0)
    edge = (SUBLANES - 1) if reverse else 0
    last = 0 if reverse else SUBLANES - 1
    at_edge = rowi == edge

    def cmul(ar, ai, br, bi):
        return ar * br - ai * bi, ar * bi + ai * br

    for c0 in range(0, N_STATE, SCAN_COLS):
        cols = slice(c0, c0 + SCAN_COLS)
        a1r = jnp.broadcast_to(a_re_ref[:, cols], (SUBLANES, SCAN_COLS))
        a1i = jnp.broadcast_to(a_im_ref[:, cols], (SUBLANES, SCAN_COLS))
        if reverse:
            a1i = -a1i

        def block_of(i):
            j = (seg - 1 - i) if reverse else i
            return j, pl.ds(pl.multiple_of(j * SUBLANES, SUBLANES), SUBLANES)

        def local(i, carry, cols=cols, a1r=a1r, a1i=a1i):
            xr, xi = carry
            _, blk = block_of(i)
            nr, ni = cmul(a1r, a1i, xr, xi)
            xr, xi = nr + re_ref[blk, cols], ni + im_ref[blk, cols]
            re_ref[blk, cols] = xr
            im_ref[blk, cols] = xi
            return xr, xi

        zero = jnp.zeros((SUBLANES, SCAN_COLS), F32)
        er, ei = lax.fori_loop(0, seg, local, (zero, zero))

        pr, pi = a1r, a1i
        for _ in range(seg.bit_length() - 1):
            pr, pi = cmul(pr, pi, pr, pi)
        cr, ci = carry_re[:, cols], carry_im[:, cols]
        inr, ini = cmul(pr, pi, cr, ci)
        er = er + jnp.where(at_edge, inr, 0.0)
        ei = ei + jnp.where(at_edge, ini, 0.0)
        for sft in (1, 2, 4):
            shift, keep = (SUBLANES - sft, rowi < SUBLANES - sft) if reverse else (sft, rowi >= sft)
            rs = jnp.where(keep, pltpu.roll(er, shift, 0), 0.0)
            ims = jnp.where(keep, pltpu.roll(ei, shift, 0), 0.0)
            dr, di = cmul(pr, pi, rs, ims)
            er, ei = er + dr, ei + di
            pr, pi = cmul(pr, pi, pr, pi)
        carry_re[:, cols] = jnp.broadcast_to(er[last:last + 1, :], (SUBLANES, SCAN_COLS))
        carry_im[:, cols] = jnp.broadcast_to(ei[last:last + 1, :], (SUBLANES, SCAN_COLS))
        one = (SUBLANES - 1) if reverse else 1
        kr = jnp.where(at_edge, cr, pltpu.roll(er, one, 0))
        ki = jnp.where(at_edge, ci, pltpu.roll(ei, one, 0))

        def fix(i, carry, cols=cols, a1r=a1r, a1i=a1i):
            kr, ki, acc = carry
            j, blk = block_of(i)
            kr, ki = cmul(a1r, a1i, kr, ki)
            xr, xi = re_ref[blk, cols] + kr, im_ref[blk, cols] + ki
            re_ref[blk, cols] = xr
            im_ref[blk, cols] = xi
            if visit is not None:
                acc = visit(cols, j, xr, xi, acc)
            return kr, ki, acc

        _, _, acc = lax.fori_loop(0, seg, fix, (kr, ki, (zero, zero)))
        if visit is not None:
            visit(cols, None, None, None, acc)


def _ssm_fwd(z, a_re, a_im, bb_re, bb_im, cc_re, cc_im, d_skip, w_glu, b_glu, nb, seq):
    t_tok = nb * seq
    tc = min(256, seq)
    nch = seq // tc
    grp = N_STATE // 4

    def body(u_ref, gs_ref, ar_ref, ai_ref, bbr_ref, bbi_ref, ccr_ref, cci_ref, d_ref, wg_ref, bg_ref,
             xr_ref, xi_ref, y_ref, sg_ref, car_re, car_im, seg_u, seg_y):
        @pl.when(pl.program_id(1) == 0)
        def _():
            car_re[...] = jnp.zeros_like(car_re)
            car_im[...] = jnp.zeros_like(car_im)

        u = u_ref[...]
        _to_segments(seg_u, u)
        for j in range(4):
            uj = seg_u[j].astype(BF16)
            xr_ref[:, j * grp:(j + 1) * grp] = _dot(uj, bbr_ref[j])
            xi_ref[:, j * grp:(j + 1) * grp] = _dot(uj, bbi_ref[j])
        _scan_chunk(xr_ref, xi_ref, ar_ref, ai_ref, car_re, car_im, tc, reverse=False)
        for j in range(4):
            xr = xr_ref[:, j * grp:(j + 1) * grp].astype(BF16)
            xi = xi_ref[:, j * grp:(j + 1) * grp].astype(BF16)
            seg_y[j] = _dot_nt(xr, ccr_ref[j]) - _dot_nt(xi, cci_ref[j])
        y = _from_segments(seg_y) + d_ref[...] * u
        y_ref[...] = y
        yg, _ = _gelu_and_grad(y)
        gl = _dot(yg.astype(BF16), wg_ref[...]) + bg_ref[...]
        gs = gs_ref[...]
        sg_ref[...] = (yg * _sig(gl) * gs * _sig(gs)).astype(BF16)

    umap = lambda b, ch: (b * nch + ch, 4)
    gmap = lambda b, ch: (b * nch + ch, 5)
    row = lambda b, ch: (b * nch + ch, 0)
    s = jax.ShapeDtypeStruct
    consts = [a_re, a_im, bb_re, bb_im, cc_re, cc_im, d_skip, w_glu, b_glu]
    return pl.pallas_call(
        body, name="ssm_fwd", grid=(nb, nch),
        out_shape=[s((t_tok, N_STATE), F32), s((t_tok, N_STATE), F32), s((t_tok, SSM_W), F32),
                   s((t_tok, SSM_W), BF16)],
        in_specs=[pl.BlockSpec((tc, SSM_W), umap), pl.BlockSpec((tc, SSM_W), gmap)] + [_full(c.shape) for c in consts],
        out_specs=[pl.BlockSpec((tc, N_STATE), row), pl.BlockSpec((tc, N_STATE), row),
                   pl.BlockSpec((tc, SSM_W), row), pl.BlockSpec((tc, SSM_W), row)],
        scratch_shapes=[pltpu.VMEM((SUBLANES, N_STATE), F32), pltpu.VMEM((SUBLANES, N_STATE), F32),
                        pltpu.VMEM((4, tc, LANES), F32), pltpu.VMEM((4, tc, LANES), F32)],
        compiler_params=_params(2, VMEM_LIMIT_V7X),
    )(z, z, *consts)


def _tail(x2, tg2, ag, sg, p2, w_out, w_g, w_p, g_ple):
    t_tok = x2.shape[0]
    tm = min(256, t_tok)
    nt = t_tok // tm
    half = ATTN_W

    def body(x_ref, tg_ref, ag_ref, sg_ref, p_ref, wo_ref, wg_ref, wp_ref, gp_ref,
             dmix_ref, dh1_ref, loss_ref, dgp_ref, dwo_ref, dwg_ref, dwp_ref, acc_o, acc_g, acc_p):
        i = pl.program_id(0)

        @pl.when(i == 0)
        def _():
            loss_ref[...] = jnp.zeros_like(loss_ref)
            dgp_ref[...] = jnp.zeros_like(dgp_ref)
            acc_o[...] = jnp.zeros_like(acc_o)
            acc_g[...] = jnp.zeros_like(acc_g)
            acc_p[...] = jnp.zeros_like(acc_p)

        ag_t, sg_t = ag_ref[...], sg_ref[...]
        h1 = x_ref[...] + _dot(ag_t, wo_ref[0:half, :]) + _dot(sg_t, wo_ref[half:2 * half, :])
        r2 = lax.rsqrt(jnp.mean(h1 * h1, axis=-1, keepdims=True) + EPS)
        hnorm = h1 * r2
        gp = gp_ref[...]
        hn = (hnorm * gp).astype(BF16)
        gate = _sig(_dot(hn, wg_ref[...]))
        pb = p_ref[...].astype(BF16)
        pp = jnp.concatenate([_dot(pb, wp_ref[j]) for j in range(N_DEV)], axis=-1)
        h2 = h1 + gate * pp
        err = h2 - tg_ref[...]
        loss_ref[...] += 0.5 * jnp.sum(err * err) * (1.0 / D_MODEL)
        dh2 = err * (1.0 / D_MODEL)
        dpp = (dh2 * gate).astype(BF16)
        dgpre = (dh2 * pp * gate * (1.0 - gate)).astype(BF16)
        acc_p[...] += _dot_tn(pb, dpp)
        acc_g[...] += _dot_tn(hn, dgpre)
        dhn = _dot_nt(dgpre, wg_ref[...])
        dgp_ref[...] += jnp.sum(dhn * hnorm, axis=0, keepdims=True)
        a = dhn * gp
        dh1 = dh2 + r2 * (a - hnorm * jnp.mean(a * hnorm, axis=-1, keepdims=True))
        dh1_ref[...] = dh1
        dh1b = dh1.astype(BF16)
        acc_o[0:half, :] += _dot_tn(ag_t, dh1b)
        acc_o[half:2 * half, :] += _dot_tn(sg_t, dh1b)
        dmix_ref[...] = _dot_nt(dh1b, wo_ref[...])

        @pl.when(i == nt - 1)
        def _():
            dwo_ref[...] = acc_o[...].astype(BF16)
            dwg_ref[...] = acc_g[...].astype(BF16)
            for j in range(N_DEV):
                dwp_ref[j] = acc_p[:, j * LANES:(j + 1) * LANES].astype(BF16)

    row = lambda i: (i, 0)
    s = jax.ShapeDtypeStruct
    return pl.pallas_call(
        body, name="tail_fwd_bwd", grid=(nt,),
        out_shape=[s((t_tok, D_MODEL), F32), s((t_tok, D_MODEL), F32), s((SUBLANES, LANES), F32),
                   s((1, D_MODEL), F32), s((D_MODEL, D_MODEL), BF16), s((D_MODEL, D_MODEL), BF16),
                   s((N_DEV, PLE_DIM, LANES), BF16)],
        in_specs=[pl.BlockSpec((tm, D_MODEL), row), pl.BlockSpec((tm, D_MODEL), row),
                  pl.BlockSpec((tm, half), row), pl.BlockSpec((tm, half), row), pl.BlockSpec((tm, PLE_DIM), row),
                  _full(w_out.shape), _full(w_g.shape), _full(w_p.shape), _full(g_ple.shape)],
        out_specs=[pl.BlockSpec((tm, D_MODEL), row), pl.BlockSpec((tm, D_MODEL), row), _full((SUBLANES, LANES)),
                   _full((1, D_MODEL)), _full((D_MODEL, D_MODEL)), _full((D_MODEL, D_MODEL)),
                   _full((N_DEV, PLE_DIM, LANES))],
        scratch_shapes=[pltpu.VMEM((D_MODEL, D_MODEL), F32), pltpu.VMEM((D_MODEL, D_MODEL), F32),
                        pltpu.VMEM((PLE_DIM, D_MODEL), F32)],
        compiler_params=_params(1, VMEM_LIMIT_V7X),
    )(x2, tg2, ag, sg, p2, w_out, w_g, w_p, g_ple)


def _attn_bwd(z, gq2, gk2, o, lse, dmix, nb, seq, parts):
    t_tok = nb * seq
    n_rs = len(parts)
    rs = _ReduceScatter([p.shape for p in parts])
    n_steps = (nb, ATTN_W // LANES)

    def body(*refs):
        (q_ref, k_ref, v_ref, ga_ref, gq_ref, gk_ref, o_ref, l_ref, da_ref), refs = refs[:9], refs[9:]
        part_refs, refs = refs[:n_rs], refs[n_rs:]
        (dq_ref, dk_ref, dv_ref, dga_ref), refs = refs[:4], refs[4:]
        g_refs, refs = refs[:n_rs], refs[n_rs:]
        (qf, kf, dof, dlf), rs_scratch = refs[:4], refs[4:]
        b, hp = pl.program_id(0), pl.program_id(1)

        @pl.when((b == 0) & (hp == 0))
        def _():
            rs.start(part_refs, rs_scratch)

        ga, o_t, da = ga_ref[...], o_ref[...], da_ref[...]
        sga = _sig(ga)
        d_o = da * ga * sga
        dga_ref[...] = da * o_t * sga * (1.0 + ga * (1.0 - sga))
        lane = lax.broadcasted_iota(jnp.int32, (1, LANES), 1)
        d_oo = d_o * o_t
        delta = jnp.zeros_like(d_oo)
        for h in range(LANES // HEAD_DIM):
            lm2 = (lane // HEAD_DIM) == h
            delta = jnp.where(lm2, jnp.sum(jnp.where(lm2, d_oo, 0.0), axis=-1, keepdims=True), delta)
        qf[...], kf[...] = _qk_hat(q_ref, k_ref, gq_ref, gk_ref)
        dof[...] = d_o
        dlf[...] = delta
        dq_ref[...] = jnp.zeros_like(dq_ref)
        dk_ref[...] = jnp.zeros_like(dk_ref)
        dv_ref[...] = jnp.zeros_like(dv_ref)
        lms = _head_masks()
        for window, dil in DILATED:
            nt, tq = _window_tiling(seq, window, dil)
            q3 = _gather_classes(qf, dil, nt, tq, BF16)
            k3 = _gather_classes(kf, dil, nt, tq, BF16)
            v3 = _gather_classes(v_ref, dil, nt, tq, BF16)
            do3 = _gather_classes(dof, dil, nt, tq, BF16)
            lt3 = _gather_classes(l_ref, dil, nt, tq, F32)
            dl3 = _gather_classes(dlf, dil, nt, tq, F32)
            if nt > 1:
                k3, v3 = _with_prev_tile(k3, dil, nt), _with_prev_tile(v3, dil, nt)
            valid = _band_valid(dil, nt, tq)
            dq = jnp.zeros(q3.shape, F32)
            dk = jnp.zeros(k3.shape, F32)
            dv = jnp.zeros(k3.shape, F32)
            for lm in lms:
                qm = jnp.where(lm, q3, jnp.zeros_like(q3))
                dom = jnp.where(lm, do3, jnp.zeros_like(do3))
                p = jnp.where(valid, jnp.exp(_bqk(qm, k3) - _head_col(lt3, lm)), 0.0)
                dv = dv + _bkd(p.astype(BF16), dom)
                ds = (p * (_bqk(dom, v3) - _head_col(dl3, lm))).astype(BF16)
                dq = dq + jnp.where(lm, _bqd(ds, k3), 0.0)
                dk = dk + _bkd(ds, qm)
            _scatter_classes(dq_ref, dq, dil, nt, tq, add=True)
            for ref, g in ((dk_ref, dk), (dv_ref, dv)):
                if nt > 1:
                    own, prev = g[:, tq:, :], g[:, :tq, :]
                    shifted = []
                    for r in range(dil):
                        t = prev[r * nt:(r + 1) * nt]
                        shifted.append(jnp.concatenate([t[1:], jnp.zeros_like(t[:1])], axis=0))
                    g = own + (shifted[0] if dil == 1 else jnp.concatenate(shifted, axis=0))
                _scatter_classes(ref, g, dil, nt, tq, add=True)

        @pl.when((b == n_steps[0] - 1) & (hp == n_steps[1] - 1))
        def _():
            rs.finish(part_refs, rs_scratch, g_refs)

    blk = pl.BlockSpec((seq, LANES), lambda b, hp: (b, hp))
    s = jax.ShapeDtypeStruct
    outs = pl.pallas_call(
        body, name="attn_bwd", grid=n_steps,
        out_shape=[s((t_tok, ATTN_W), F32)] * 4 + [s(p.shape[1:], F32) for p in parts],
        in_specs=[_zblock(seq, 0), _zblock(seq, 1), _zblock(seq, 2), _zblock(seq, 3), _full(gq2.shape),
                  _full(gk2.shape), blk, blk, blk] + [pl.BlockSpec(memory_space=pl.ANY)] * n_rs,
        out_specs=[blk] * 4 + [_full(p.shape[1:]) for p in parts],
        scratch_shapes=[pltpu.VMEM((seq, LANES), F32)] * 4 + rs.scratch(parts[0].dtype),
        compiler_params=_params(2, VMEM_LIMIT_V7X),
    )(z, z, z, z, gq2, gk2, o, lse, dmix, *parts)
    return outs[:4], outs[4:]


def _ssm_bwd(z, dmix, y, x_re, x_im, a_re, a_im, bb_re, bb_im, cc_re, cc_im, d_skip, w_glu, b_glu, nb, seq):
    t_tok = nb * seq
    tc = min(256, seq)
    nch = seq // tc
    grp = N_STATE // 4

    def body(u_ref, gs_ref, ds_ref, y_ref, xr_ref, xi_ref, xpr_ref, xpi_ref,
             ar_ref, ai_ref, bbr_ref, bbi_ref, ccr_ref, cci_ref, d_ref, wg_ref, bg_ref,
             du_ref, dgs_ref, dwg_ref, dbg_ref, dd_ref, dar_ref, dai_ref, dbbr_ref, dbbi_ref, dccr_ref, dcci_ref,
             lam_re, lam_im, car_re, car_im, acc_wg, seg_a, seg_b, ent_re, ent_im):
        step = pl.program_id(1)
        first_chunk = step == nch - 1

        @pl.when((pl.program_id(0) == 0) & (step == 0))
        def _():
            acc_wg[...] = jnp.zeros_like(acc_wg)
            for ref in (dbg_ref, dd_ref, dar_ref, dai_ref, dbbr_ref, dbbi_ref, dccr_ref, dcci_ref):
                ref[...] = jnp.zeros_like(ref)

        @pl.when(step == 0)
        def _():
            car_re[...] = jnp.zeros_like(car_re)
            car_im[...] = jnp.zeros_like(car_im)

        u, gs, dssm, y = u_ref[...], gs_ref[...], ds_ref[...], y_ref[...]
        yg, dgelu = _gelu_and_grad(y)
        ygb = yg.astype(BF16)
        sgl = _sig(_dot(ygb, wg_ref[...]) + bg_ref[...])
        sgs = _sig(gs)
        dout = dssm * gs * sgs
        dgs_ref[...] = dssm * yg * sgl * sgs * (1.0 + gs * (1.0 - sgs))
        dgl = dout * yg * sgl * (1.0 - sgl)
        dglb = dgl.astype(BF16)
        dyg = dout * sgl + _dot_nt(dglb, wg_ref[...])
        acc_wg[...] += _dot_tn(ygb, dglb)
        dbg_ref[...] += jnp.sum(dgl, axis=0, keepdims=True)
        dy = dyg * dgelu
        dd_ref[...] += jnp.sum(dy * u, axis=0, keepdims=True)
        _to_segments(seg_a, dy)
        _to_segments(seg_b, u)
        for j in range(4):
            dyj = seg_a[j].astype(BF16)
            sl = slice(j * grp, (j + 1) * grp)
            lam_re[:, sl] = _dot(dyj, ccr_ref[j])
            lam_im[:, sl] = -_dot(dyj, cci_ref[j])
            dccr_ref[j] += _dot_tn(dyj, xr_ref[:, sl].astype(BF16))
            dcci_ref[j] -= _dot_tn(dyj, xi_ref[:, sl].astype(BF16))

        keep_prev = jnp.where(first_chunk, 0.0, 1.0)
        seg = tc // SUBLANES
        last_blk = pl.ds((seg - 1) * SUBLANES, SUBLANES)
        row0 = lax.broadcasted_iota(jnp.int32, (SUBLANES, N_STATE), 0) == 0
        for src, prev, dst in ((xr_ref, xpr_ref, ent_re), (xi_ref, xpi_ref, ent_im)):
            before = jnp.broadcast_to(prev[SUBLANES - 1:SUBLANES, :] * keep_prev, (SUBLANES, N_STATE))
            dst[...] = jnp.where(row0, before, pltpu.roll(src[last_blk, :], 1, 0))

        def visit(cols, j, lr, li, acc):
            if j is None:
                dar_ref[:, cols] += jnp.sum(acc[0], axis=0, keepdims=True)
                dai_ref[:, cols] += jnp.sum(acc[1], axis=0, keepdims=True)
                return None
            blk = pl.ds(pl.multiple_of(jnp.maximum(j - 1, 0) * SUBLANES, SUBLANES), SUBLANES)
            inside = j > 0
            xpr = jnp.where(inside, xr_ref[blk, cols], ent_re[:, cols])
            xpi = jnp.where(inside, xi_ref[blk, cols], ent_im[:, cols])
            return acc[0] + lr * xpr + li * xpi, acc[1] + li * xpr - lr * xpi

        _scan_chunk(lam_re, lam_im, ar_ref, ai_ref, car_re, car_im, tc, reverse=True, visit=visit)

        for j in range(4):
            sl = slice(j * grp, (j + 1) * grp)
            lr = lam_re[:, sl].astype(BF16)
            li = lam_im[:, sl].astype(BF16)
            uj = seg_b[j].astype(BF16)
            seg_a[j] = _dot_nt(lr, bbr_ref[j]) + _dot_nt(li, bbi_ref[j])
            dbbr_ref[j] += _dot_tn(uj, lr)
            dbbi_ref[j] += _dot_tn(uj, li)
        du_ref[...] = _from_segments(seg_a) + dy * d_ref[...]

        @pl.when((pl.program_id(0) == nb - 1) & (step == nch - 1))
        def _():
            dwg_ref[...] = acc_wg[...].astype(BF16)

    rev = lambda b, ch: b * nch + (nch - 1 - ch)
    umap = lambda b, ch: (rev(b, ch), 4)
    gmap = lambda b, ch: (rev(b, ch), 5)
    smap = lambda b, ch: (rev(b, ch), 1)
    row = lambda b, ch: (rev(b, ch), 0)
    prev = lambda b, ch: (jnp.maximum(rev(b, ch) * (tc // SUBLANES) - 1, 0), 0)
    s = jax.ShapeDtypeStruct
    consts = [a_re, a_im, bb_re, bb_im, cc_re, cc_im, d_skip, w_glu, b_glu]
    acc_shapes = [s((1, SSM_W), F32), s((1, SSM_W), F32), s((1, N_STATE), F32), s((1, N_STATE), F32),
                  s(bb_re.shape, F32), s(bb_re.shape, F32), s(cc_re.shape, F32), s(cc_re.shape, F32)]
    return pl.pallas_call(
        body, name="ssm_bwd", grid=(nb, nch),
        out_shape=[s((t_tok, SSM_W), F32), s((t_tok, SSM_W), F32), s((SSM_W, SSM_W), BF16)] + acc_shapes,
        in_specs=[pl.BlockSpec((tc, SSM_W), umap), pl.BlockSpec((tc, SSM_W), gmap), pl.BlockSpec((tc, SSM_W), smap),
                  pl.BlockSpec((tc, SSM_W), row), pl.BlockSpec((tc, N_STATE), row), pl.BlockSpec((tc, N_STATE), row),
                  pl.BlockSpec((SUBLANES, N_STATE), prev), pl.BlockSpec((SUBLANES, N_STATE), prev)]
        + [_full(c.shape) for c in consts],
        out_specs=[pl.BlockSpec((tc, SSM_W), row), pl.BlockSpec((tc, SSM_W), row), _full((SSM_W, SSM_W))]
        + [_full(a.shape) for a in acc_shapes],
        scratch_shapes=[pltpu.VMEM((tc, N_STATE), F32), pltpu.VMEM((tc, N_STATE), F32),
                        pltpu.VMEM((SUBLANES, N_STATE), F32), pltpu.VMEM((SUBLANES, N_STATE), F32),
                        pltpu.VMEM((SSM_W, SSM_W), F32), pltpu.VMEM((4, tc, LANES), F32),
                        pltpu.VMEM((4, tc, LANES), F32),
                        pltpu.VMEM((SUBLANES, N_STATE), F32), pltpu.VMEM((SUBLANES, N_STATE), F32)],
        compiler_params=_params(2, VMEM_LIMIT_V7X),
    )(z, z, dmix, y, x_re, x_im, x_re, x_im, *consts)


def _dz_and_dx(x2, z, dqh, dkh, dvb, dga, du, dgs, dh1, w_in_g, g_mix, gq_t, gk_t, ones_bd, fold):
    t_tok = x2.shape[0]
    tm = min(256, t_tok)
    nt = t_tok // tm
    a_w = ATTN_W

    def head_norm_bwd(raw, d_hat, gain, scale, ones):
        r = lax.rsqrt(_hdot(raw * raw, ones) * (1.0 / HEAD_DIM) + EPS)
        n = raw * r
        a = d_hat * gain * scale
        d_raw = r * (a - n * (_hdot(a * n, ones) * (1.0 / HEAD_DIM)))
        return d_raw, jnp.sum(d_hat * n * scale, axis=0, keepdims=True)

    def body(x_ref, q_ref, k_ref, dq_ref, dk_ref, dv_ref, dga_ref, du_ref, dgs_ref, dh1_ref, w_ref, g_ref,
             gq_ref, gk_ref, ones_ref, fold_ref, dz_ref, gx_ref, dgm_ref, dgq_ref, dgk_ref, acc_q, acc_k):
        i = pl.program_id(0)

        @pl.when(i == 0)
        def _():
            dgm_ref[...] = jnp.zeros_like(dgm_ref)
            acc_q[...] = jnp.zeros_like(acc_q)
            acc_k[...] = jnp.zeros_like(acc_k)

        ones = ones_ref[...]
        dq, sq = head_norm_bwd(q_ref[...], dq_ref[...], gq_ref[...], HEAD_DIM ** -0.5, ones)
        dk, sk = head_norm_bwd(k_ref[...], dk_ref[...], gk_ref[...], 1.0, ones)
        acc_q[...] += jnp.broadcast_to(sq, acc_q.shape)
        acc_k[...] += jnp.broadcast_to(sk, acc_k.shape)
        parts = (dq, dk, dv_ref[...], dga_ref[...], du_ref[...], dgs_ref[...])
        for n, part in enumerate(parts):
            dz_ref[:, n * a_w:(n + 1) * a_w] = part.astype(BF16)
        dxn = jnp.zeros((tm, D_MODEL), F32)
        for j in range(N_DEV):
            dxn = dxn + _dot_nt(dz_ref[:, j * COL_W:(j + 1) * COL_W], w_ref[j])
        x = x_ref[...]
        r1 = lax.rsqrt(jnp.mean(x * x, axis=-1, keepdims=True) + EPS)
        xnorm = x * r1
        dgm_ref[...] += jnp.sum(dxn * xnorm, axis=0, keepdims=True)
        a = dxn * g_ref[...]
        gx_ref[...] = dh1_ref[...] + r1 * (a - xnorm * jnp.mean(a * xnorm, axis=-1, keepdims=True))

        @pl.when(i == nt - 1)
        def _():
            dgq_ref[...] = _hdot(acc_q[...], fold_ref[...])
            dgk_ref[...] = _hdot(acc_k[...], fold_ref[...])

    row = lambda i: (i, 0)
    col = lambda n: (lambda i: (i, n))
    s = jax.ShapeDtypeStruct
    half = pl.BlockSpec((tm, a_w), row)
    return pl.pallas_call(
        body, name="dz_dx", grid=(nt,),
        out_shape=[s((t_tok, IN_W), BF16), s((t_tok, D_MODEL), F32), s((1, D_MODEL), F32),
                   s((SUBLANES, HEAD_DIM), F32), s((SUBLANES, HEAD_DIM), F32)],
        in_specs=[pl.BlockSpec((tm, D_MODEL), row), pl.BlockSpec((tm, a_w), col(0)), pl.BlockSpec((tm, a_w), col(1)),
                  half, half, half, half, half, half, pl.BlockSpec((tm, D_MODEL), row),
                  _full(w_in_g.shape), _full(g_mix.shape), _full(gq_t.shape), _full(gk_t.shape),
                  _full(ones_bd.shape), _full(fold.shape)],
        out_specs=[pl.BlockSpec((tm, IN_W), row), pl.BlockSpec((tm, D_MODEL), row), _full((1, D_MODEL)),
                   _full((SUBLANES, HEAD_DIM)), _full((SUBLANES, HEAD_DIM))],
        scratch_shapes=[pltpu.VMEM((SUBLANES, a_w), F32), pltpu.VMEM((SUBLANES, a_w), F32)],
        compiler_params=_params(1, VMEM_LIMIT_V7X),
    )(x2, z, z, dqh, dkh, dvb, dga, du, dgs, dh1, w_in_g, g_mix, gq_t, gk_t, ones_bd, fold)


def _dw_in(xn, dz, glu_parts, small):
    t_tok = xn.shape[0]
    tk = min(1024, t_tok)
    nk = t_tok // tk
    rs = _ReduceScatter([glu_parts.shape])
    ag = _AllGather(1, cast=False)
    n_rs = len(rs.scratch(BF16))

    def place():
        x, y, c = lax.axis_index("x"), lax.axis_index("y"), lax.axis_index("c")
        return x, y, c, [(1 - x, y), (x, 1 - y), (1 - x, 1 - y)]

    def target(i):
        x, y, c, _ = place()
        rel = jnp.where(i < 6, i // 2 + 1, 0)
        px = jnp.where((rel == 1) | (rel == 3), 1 - x, x)
        py = jnp.where((rel == 2) | (rel == 3), 1 - y, y)
        pc = jnp.where(i % 2 == 0, 1 - c, c)
        return 4 * px + 2 * py + pc

    chunk, chunks = _row_chunks(D_MODEL)

    def body(xn_ref, dz_ref, glu_ref, small_ref, gin_ref, gglu_ref, gath_ref, acc, stage, land, send_sems, recv_sems,
             *rest):
        rs_scratch, ag_sems = rest[:n_rs], rest[n_rs:]
        i, k = pl.program_id(0), pl.program_id(1)
        x, y, c, chips = place()

        def push(slot, to):
            return pltpu.make_async_remote_copy(
                src_ref=stage.at[slot], dst_ref=land.at[slot], send_sem=send_sems.at[slot],
                recv_sem=recv_sems.at[slot], device_id=to, device_id_type=MESH)

        pushes = [push(n, (x, y, 1 - c)) for n in range(4)] + [push(4 + n, (*chips[n], c)) for n in range(3)]

        def staged(slot, plus=None):
            def put(s, carry):
                r = pl.ds(pl.multiple_of(s * chunk, chunk), chunk)
                val = acc[r, :]
                if plus is not None:
                    val = val + land[plus, r, :].astype(F32)
                stage[slot, r, :] = val.astype(BF16)
                return carry

            lax.fori_loop(0, chunks, put, 0)

        @pl.when((i == 0) & (k == 0))
        def _():
            rs.start([glu_ref], rs_scratch)
            ag.start([small_ref], [gath_ref], ag_sems)

        @pl.when((i == N_DEV // 2) & (k == 0))
        def _():
            ag.forward([small_ref], [gath_ref], ag_sems)

        @pl.when(k == 0)
        def _():
            acc[...] = jnp.zeros_like(acc)

        acc[...] += _dot_tn(xn_ref[...], dz_ref[...])

        for n in range(4):
            @pl.when((k == nk - 1) & (i == 2 * n))
            def _(n=n):
                staged(n)
                pushes[n].start()

        for n in range(3):
            @pl.when((k == nk - 1) & (i == 2 * n + 1))
            def _(n=n):
                pushes[n].wait_recv()
                staged(4 + n, plus=n)
                pushes[4 + n].start()

        @pl.when((k == nk - 1) & (i == N_DEV - 1))
        def _():
            for slot in range(3, N_DEV - 1):
                pushes[slot].wait_recv()

            def add(s, carry):
                r = pl.ds(pl.multiple_of(s * chunk, chunk), chunk)
                total = acc[r, :]
                for slot in range(3, N_DEV - 1):
                    total = total + land[slot, r, :].astype(F32)
                gin_ref[r, :] = total
                return carry

            lax.fori_loop(0, chunks, add, 0)
            for cp in pushes:
                cp.wait_send()
            rs.finish([glu_ref], rs_scratch, [gglu_ref])
            ag.finish([small_ref], [gath_ref], ag_sems)

    any_spec = pl.BlockSpec(memory_space=pl.ANY)
    s = jax.ShapeDtypeStruct
    return pl.pallas_call(
        body, name="dw_in", grid=(N_DEV, nk),
        out_shape=[s((D_MODEL, COL_W), F32), s(glu_parts.shape[1:], F32), s((N_DEV,) + small.shape, F32)],
        in_specs=[pl.BlockSpec((tk, D_MODEL), lambda i, k: (k, 0)),
                  pl.BlockSpec((tk, COL_W), lambda i, k: (k, target(i))), any_spec, any_spec],
        out_specs=[_full((D_MODEL, COL_W)), _full(glu_parts.shape[1:]), any_spec],
        scratch_shapes=[pltpu.VMEM((D_MODEL, COL_W), F32), pltpu.VMEM((N_DEV - 1, D_MODEL, COL_W), BF16),
                        pltpu.VMEM((N_DEV - 1, D_MODEL, COL_W), BF16), pltpu.SemaphoreType.DMA((N_DEV - 1,)),
                        pltpu.SemaphoreType.DMA((N_DEV - 1,))] + rs.scratch(BF16) + ag.scratch(),
        compiler_params=_params(2, VMEM_LIMIT_V7X),
    )(xn, dz, glu_parts, small)


SMALL = ("mix_norm", "q_norm", "k_norm", "lambda_re", "lambda_im", "log_dt", "b_re", "b_im", "c_re", "c_im",
         "d_skip", "b_glu", "ple_norm")
BIG = ("w_in", "w_glu", "w_out", "w_ple_gate", "w_ple_proj")
WEIGHTS = ("mix_norm", "w_in", "q_norm", "k_norm", "lambda_re", "lambda_im", "log_dt", "b_re", "b_im", "c_re",
           "c_im", "d_skip", "w_glu", "b_glu", "w_out", "ple_norm", "w_ple_gate", "w_ple_proj")


def _pack(arrs):
    flat = jnp.concatenate([a.reshape(-1).astype(F32) for a in arrs])
    rows = -(-flat.shape[0] // (64 * LANES)) * 64
    return jnp.pad(flat, (0, rows * LANES - flat.shape[0])).reshape(rows, LANES)


def _unpack(packed, shapes):
    flat = packed.reshape(-1)
    out, off = [], 0
    for shp in shapes:
        size = math.prod(shp)
        out.append(flat[off:off + size].reshape(shp))
        off += size
    return out


def kernel(x, p, mix_norm, w_in, q_norm, k_norm, lambda_re, lambda_im, log_dt, b_re, b_im, c_re, c_im, d_skip, w_glu, b_glu, w_out, ple_norm, w_ple_gate, w_ple_proj, loss_target, m_mix_norm, m_w_in, m_q_norm, m_k_norm, m_lambda_re, m_lambda_im, m_log_dt, m_b_re, m_b_im, m_c_re, m_c_im, m_d_skip, m_w_glu, m_b_glu, m_w_out, m_ple_norm, m_w_ple_gate, m_w_ple_proj, v_mix_norm, v_w_in, v_q_norm, v_k_norm, v_lambda_re, v_lambda_im, v_log_dt, v_b_re, v_b_im, v_c_re, v_c_im, v_d_skip, v_w_glu, v_b_glu, v_w_out, v_ple_norm, v_w_ple_gate, v_w_ple_proj):
    env = dict(locals())
    w = {n: env[n] for n in WEIGHTS}
    m = {n: env["m_" + n] for n in WEIGHTS}
    v = {n: env["v_" + n] for n in WEIGHTS}
    nb, seq, _ = x.shape
    t_tok = nb * seq
    x2 = x.reshape(t_tok, D_MODEL)
    tg2 = loss_target.reshape(t_tok, D_MODEL)
    p2 = p.reshape(t_tok, PLE_DIM)

    shard2d = {"w_in": (D_MODEL, COL_W), "w_glu": (SSM_W // N_DEV, SSM_W), "w_out": (D_MODEL // N_DEV, D_MODEL),
               "w_ple_gate": (D_MODEL // N_DEV, D_MODEL), "w_ple_proj": (PLE_DIM, D_MODEL // N_DEV)}
    w_sh = [w[n].reshape(shard2d[n]) for n in BIG]

    g3 = (SSM_GROUPS, 1, SSM_STATE)
    lr3, li3 = lambda_re.reshape(g3), lambda_im.reshape(g3)
    dt3 = log_dt.reshape(SSM_GROUPS, 1, 1)
    btr = b_re[0].transpose(0, 2, 1)
    bti = b_im[0].transpose(0, 2, 1)
    a_re3, a_im3, bb_re, bb_im, cc_re, cc_im = _zoh_fwd(lr3, li3, dt3, btr, bti, c_re[0], c_im[0])
    a_re, a_im = a_re3.reshape(1, N_STATE), a_im3.reshape(1, N_STATE)

    ones_bd = _head_ones()
    fold = _head_fold()
    gq_t = jnp.tile(q_norm, (1, ATTN_W // HEAD_DIM))
    gk_t = jnp.tile(k_norm, (1, ATTN_W // HEAD_DIM))

    gq2 = jnp.tile(q_norm, (1, LANES // HEAD_DIM))
    gk2 = jnp.tile(k_norm, (1, LANES // HEAD_DIM))

    z, xn, w_in_g = _in_proj(x2, mix_norm, w_sh[0])
    (o, lse, ag), (w_glu_g, w_out_g, w_g_g, w_p_g) = _attn_fwd(z, gq2, gk2, nb, seq, w_sh[1:])
    w_glu_f = w_glu_g.reshape(SSM_W, SSM_W)
    w_out_f = w_out_g.reshape(D_MODEL, D_MODEL)
    w_g_f = w_g_g.reshape(D_MODEL, D_MODEL)
    x_re, x_im, y, sg = _ssm_fwd(z, a_re, a_im, bb_re, bb_im, cc_re, cc_im, d_skip, w_glu_f, b_glu, nb, seq)
    dmix, dh1, loss_t, d_ple, dw_out, dw_g, dw_p = _tail(x2, tg2, ag, sg, p2, w_out_f, w_g_f, w_p_g, ple_norm)

    early_parts = [dw_out.reshape(N_DEV, D_MODEL // N_DEV, D_MODEL), dw_g.reshape(N_DEV, D_MODEL // N_DEV, D_MODEL),
                   dw_p]
    (dqh, dkh, dvb, dga), (g_out, g_g, g_p) = _attn_bwd(z, gq2, gk2, o, lse, dmix, nb, seq, early_parts)
    (du, dgs, dw_glu, d_bglu, d_dskip, da_re, da_im, dbb_re, dbb_im, dcc_re, dcc_im) = _ssm_bwd(
        z, dmix, y, x_re, x_im, a_re, a_im, bb_re, bb_im, cc_re, cc_im, d_skip, w_glu_f, b_glu, nb, seq)
    dz, gx, d_mix, d_gq, d_gk = _dz_and_dx(x2, z, dqh, dkh, dvb, dga, du, dgs, dh1, w_in_g, mix_norm, gq_t, gk_t,
                                           ones_bd, fold)
    d_lr, d_li, d_dt, d_btr, d_bti, d_cr, d_ci = _zoh_bwd(
        lr3, li3, dt3, btr, bti, da_re.reshape(g3), da_im.reshape(g3), dbb_re, dbb_im, dcc_re, dcc_im, fold)
    small_g = {
        "mix_norm": d_mix, "q_norm": d_gq[0:1], "k_norm": d_gk[0:1], "lambda_re": d_lr, "lambda_im": d_li,
        "log_dt": d_dt, "b_re": d_btr, "b_im": d_bti, "c_re": d_cr, "c_im": d_ci,
        "d_skip": d_dskip, "b_glu": d_bglu, "ple_norm": d_ple}

    g_in, g_glu, gathered = _dw_in(xn, dz, dw_glu.reshape(N_DEV, SSM_W // N_DEV, SSM_W),
                                   _pack([small_g[n] for n in SMALL] + [loss_t[0:1, 0:1]]))
    g_sh = [g_in, g_glu, g_out, g_g, g_p]
    d_sh, m_sh, v_sh = _adamw_shards(g_sh, w_sh, [m[n].reshape(shard2d[n]) for n in BIG],
                                     [v[n].reshape(shard2d[n]) for n in BIG])

    g_pk = _small_sum(gathered)
    small_shapes = [w[n].shape for n in SMALL]
    swapped = ("b_re", "b_im")

    def to_own(n, a):
        a = a.reshape(a.shape[1:]) if a.ndim > 2 else a
        return a.transpose(0, 2, 1) if n in swapped else a

    def from_own(n, a):
        a = a.transpose(0, 2, 1) if n in swapped else a
        return a.reshape(w[n].shape)

    own = [to_own(n, w[n]).shape for n in SMALL]
    g_small = _unpack(g_pk, own)
    d_small, m_small, v_small = _adamw_small(
        g_small, *[[to_own(n, src[n]) for n in SMALL] for src in (w, m, v)])

    grads, deltas, new_m, new_v = {}, {}, {}, {}
    for dst, arrs in ((grads, g_small), (deltas, d_small), (new_m, m_small), (new_v, v_small)):
        for n, a in zip(SMALL, arrs):
            dst[n] = from_own(n, a)
    for i, n in enumerate(BIG):
        grads[n] = g_sh[i].reshape(w[n].shape)
        deltas[n] = d_sh[i].reshape(w[n].shape)
        new_m[n] = m_sh[i].reshape(w[n].shape)
        new_v[n] = v_sh[i].reshape(w[n].shape)

    loss = _unpack(g_pk, small_shapes + [()])[-1]
    return (loss, gx.reshape(x.shape), *[grads[n] for n in WEIGHTS], *[deltas[n] for n in WEIGHTS],
            *[new_m[n] for n in WEIGHTS], *[new_v[n] for n in WEIGHTS])
```

```python
import math

import numpy as np
import jax
import jax.numpy as jnp
from jax import lax
from jax.experimental import pallas as pl
from jax.experimental.pallas import tpu as pltpu

F32 = jnp.float32
BF16 = jnp.bfloat16
MESH = pl.DeviceIdType.MESH
AXES = ("x", "y", "c")
N_DEV = 8

D_MODEL = 1024
HEAD_DIM = 64
ATTN_W = 512
SSM_W = 512
SSM_GROUPS = 32
SSM_GROUP = 16
SSM_STATE = 64
N_STATE = SSM_GROUPS * SSM_STATE
PLE_DIM = 256
IN_W = 3072
COL_W = IN_W // N_DEV
DILATED = ((128, 1), (512, 4), (2048, 16))
EPS = 1e-6
INV_SQRT2 = 1.0 / math.sqrt(2.0)
INV_SQRT_2PI = 1.0 / math.sqrt(2.0 * math.pi)

ADAM_LR, ADAM_B1, ADAM_B2, ADAM_EPS, ADAM_WD, ADAM_STEP = 0.001, 0.9, 0.999, 1e-08, 0.01, 10

VMEM_LIMIT_V7X = 56 * 1024 * 1024
SUBLANES = 8
LANES = 128


def _params(n_axes=None, vmem=None):
    kw = {}
    if n_axes:
        kw["dimension_semantics"] = ("arbitrary",) * n_axes
    if vmem:
        kw["vmem_limit_bytes"] = vmem
    return pltpu.CompilerParams(**kw)


def _dot(a, b):
    return jnp.dot(a, b, preferred_element_type=F32)


def _dot_nt(a, b):
    return lax.dot_general(a, b, (((1,), (1,)), ((), ())), preferred_element_type=F32)


def _dot_tn(a, b):
    return lax.dot_general(a, b, (((0,), (0,)), ((), ())), preferred_element_type=F32)


def _hdot(a, ones):
    hi = a.astype(BF16)
    lo = (a - hi.astype(F32)).astype(BF16)
    return _dot(hi, ones) + _dot(lo, ones)


def _sig(x):
    return 1.0 / (1.0 + jnp.exp(-x))


def _gelu_and_grad(y):
    cdf = 0.5 * (1.0 + lax.erf(y * INV_SQRT2))
    pdf = jnp.exp(-0.5 * y * y) * INV_SQRT_2PI
    return y * cdf, cdf + y * pdf


def _vmem():
    return pl.BlockSpec(memory_space=pltpu.VMEM)


def _full(shape):
    nd = len(shape)
    return pl.BlockSpec(shape, lambda *_: (0,) * nd)


class _AllGather:
    def __init__(self, n, cast):
        self.n, self.cast = n, cast

    def scratch(self):
        n = self.n
        return [pltpu.SemaphoreType.DMA((7 * n,)), pltpu.SemaphoreType.DMA((7 * n,)), pltpu.SemaphoreType.DMA((n,))]

    def _plan(self, src_refs, out_refs, sems):
        send_sems, recv_sems, own_sems = sems
        x, y, c = lax.axis_index("x"), lax.axis_index("y"), lax.axis_index("c")
        me, sibling = (x, y, c), (x, y, 1 - c)
        chips = [(1 - x, y), (x, 1 - y), (1 - x, 1 - y)]

        def idx(px, py, pc):
            return 4 * px + 2 * py + pc

        def copy(i, k, block, to, own_src=False):
            ref = out_refs[i].at[idx(*block)]
            return pltpu.make_async_remote_copy(
                src_ref=src_refs[i] if own_src and not self.cast else ref, dst_ref=ref,
                send_sem=send_sems.at[7 * i + k], recv_sem=recv_sems.at[7 * i + k],
                device_id=to, device_id_type=MESH)

        first, passed, arrive_ici, arrive_d2d, own = [], [], [], [], []
        for i in range(self.n):
            first.append(copy(i, 0, me, sibling, own_src=True))
            first += [copy(i, 1 + j, me, (*chip, c), own_src=True) for j, chip in enumerate(chips)]
            arrive_ici += [copy(i, 1 + j, (*chip, c), me) for j, chip in enumerate(chips)]
            passed += [copy(i, 4 + j, (*chip, c), sibling) for j, chip in enumerate(chips)]
            arrive_d2d.append(copy(i, 0, sibling, me))
            arrive_d2d += [copy(i, 4 + j, (*chip, 1 - c), me) for j, chip in enumerate(chips)]
            if not self.cast:
                own.append(pltpu.make_async_copy(src_refs[i], out_refs[i].at[idx(*me)], own_sems.at[i]))
        return idx(*me), first, passed, arrive_ici, arrive_d2d, own

    def start(self, src_refs, out_refs, sems):
        my, first, _, _, _, own = self._plan(src_refs, out_refs, sems)
        if self.cast:
            for i in range(self.n):
                out_refs[i][my] = src_refs[i][...].astype(out_refs[i].dtype)
        for cp in own + first:
            cp.start()

    def forward(self, src_refs, out_refs, sems):
        _, _, passed, arrive_ici, _, _ = self._plan(src_refs, out_refs, sems)
        for cp in arrive_ici:
            cp.wait_recv()
        for cp in passed:
            cp.start()

    def finish(self, src_refs, out_refs, sems):
        _, first, passed, _, arrive_d2d, own = self._plan(src_refs, out_refs, sems)
        for cp in own:
            cp.wait()
        for cp in arrive_d2d:
            cp.wait_recv()
        for cp in first + passed:
            cp.wait_send()


class _HostedGather:
    def __init__(self, shards):
        self.shapes = [(N_DEV,) + a.shape for a in shards]
        self.n = len(shards)
        self.ag = _AllGather(self.n, cast=True)

    def out_shape(self):
        return [jax.ShapeDtypeStruct(s, BF16) for s in self.shapes]

    def scratch(self):
        return [pltpu.VMEM(s, BF16) for s in self.shapes] + self.ag.scratch() + [pltpu.SemaphoreType.DMA((self.n,))]

    def _split(self, scratch):
        return scratch[:self.n], scratch[self.n:-1], scratch[-1]

    def start(self, src_refs, scratch):
        land, sems, _ = self._split(scratch)
        self.ag.start(src_refs, land, sems)

    def forward(self, src_refs, scratch):
        land, sems, _ = self._split(scratch)
        self.ag.forward(src_refs, land, sems)

    def finish(self, src_refs, scratch, out_refs):
        land, sems, out_sems = self._split(scratch)
        self.ag.finish(src_refs, land, sems)
        outs = [pltpu.make_async_copy(land[n], out_refs[n], out_sems.at[n]) for n in range(self.n)]
        for cp in outs:
            cp.start()
        for cp in outs:
            cp.wait()


def _all_gather(shards, out_dtypes, name):
    n = len(shards)
    ag = _AllGather(n, cast=True)

    def body(*refs):
        in_refs, out_refs, sems = refs[:n], refs[n:2 * n], refs[2 * n:]
        ag.start(in_refs, out_refs, sems)
        ag.forward(in_refs, out_refs, sems)
        ag.finish(in_refs, out_refs, sems)

    return pl.pallas_call(
        body, name=name,
        out_shape=[jax.ShapeDtypeStruct((N_DEV,) + s.shape, dt) for s, dt in zip(shards, out_dtypes)],
        in_specs=[_vmem()] * n, out_specs=[_vmem()] * n,
        scratch_shapes=ag.scratch(),
        compiler_params=_params(vmem=VMEM_LIMIT_V7X),
    )(*shards)


def _row_chunks(rows):
    chunk = 64 if rows % 64 == 0 else rows
    return chunk, rows // chunk


class _ReduceScatter:
    def __init__(self, shapes):
        self.shapes = shapes
        self.n = len(shapes)

    def scratch(self, dtype):
        return ([pltpu.VMEM(s, dtype) for s in self.shapes]
                + [pltpu.SemaphoreType.DMA((7 * self.n,)), pltpu.SemaphoreType.DMA((7 * self.n,)),
                   pltpu.SemaphoreType.DMA((self.n,))])

    def _copies(self, in_refs, land_refs, send_sems, recv_sems, own_sems):
        x, y, c = lax.axis_index("x"), lax.axis_index("y"), lax.axis_index("c")
        remote, own = [], []
        for i in range(self.n):
            for m in range(1, N_DEV):
                px = 1 - x if m & 4 else x
                py = 1 - y if m & 2 else y
                pc = 1 - c if m & 1 else c
                remote.append(pltpu.make_async_remote_copy(
                    src_ref=in_refs[i].at[4 * px + 2 * py + pc], dst_ref=land_refs[i].at[m - 1],
                    send_sem=send_sems.at[7 * i + m - 1], recv_sem=recv_sems.at[7 * i + m - 1],
                    device_id=(px, py, pc), device_id_type=MESH))
            own.append(pltpu.make_async_copy(in_refs[i].at[4 * x + 2 * y + c], land_refs[i].at[N_DEV - 1],
                                             own_sems.at[i]))
        return remote, own

    def start(self, in_refs, scratch):
        remote, own = self._copies(in_refs, scratch[:self.n], *scratch[self.n:])
        for cp in remote + own:
            cp.start()

    def finish(self, in_refs, scratch, out_refs):
        land_refs = scratch[:self.n]
        remote, own = self._copies(in_refs, land_refs, *scratch[self.n:])
        for cp in own:
            cp.wait()
        for cp in remote:
            cp.wait_recv()
        for i in range(self.n):
            chunk, steps = _row_chunks(self.shapes[i][1])

            def step(s, carry, i=i, chunk=chunk):
                r = pl.ds(pl.multiple_of(s * chunk, chunk), chunk)
                acc = land_refs[i][N_DEV - 1, r, :].astype(F32)
                for m in range(1, N_DEV):
                    acc = acc + land_refs[i][m - 1, r, :].astype(F32)
                out_refs[i][r, :] = acc
                return carry

            lax.fori_loop(0, steps, step, 0)
        for cp in remote:
            cp.wait_send()


def _reduce_scatter(parts, name):
    n = len(parts)
    rs = _ReduceScatter([p.shape for p in parts])

    def body(*refs):
        in_refs, out_refs, scratch = refs[:n], refs[n:2 * n], refs[2 * n:]
        rs.start(in_refs, scratch)
        rs.finish(in_refs, scratch, out_refs)

    return pl.pallas_call(
        body, name=name,
        out_shape=[jax.ShapeDtypeStruct(p.shape[1:], F32) for p in parts],
        in_specs=[_vmem()] * n, out_specs=[_vmem()] * n,
        scratch_shapes=rs.scratch(parts[0].dtype),
        compiler_params=_params(vmem=VMEM_LIMIT_V7X),
    )(*parts)


def _adamw_math(w, g, m, v):
    m = ADAM_B1 * m + (1.0 - ADAM_B1) * g
    v = ADAM_B2 * v + (1.0 - ADAM_B2) * (g * g)
    m_hat = m / (1.0 - ADAM_B1 ** ADAM_STEP)
    v_hat = v / (1.0 - ADAM_B2 ** ADAM_STEP)
    delta = -ADAM_LR * (m_hat / (jnp.sqrt(v_hat) + ADAM_EPS) + ADAM_WD * w)
    return delta, m, v


def _adamw_shards(gs, ws, ms, vs):
    n = len(gs)

    def body(*refs):
        g_refs, w_refs, m_refs, v_refs = (refs[k * n:(k + 1) * n] for k in range(4))
        d_out, m_out, v_out = (refs[(4 + k) * n:(5 + k) * n] for k in range(3))
        for i in range(n):
            chunk, steps = _row_chunks(gs[i].shape[0])

            def step(s, carry, i=i, chunk=chunk):
                r = pl.ds(pl.multiple_of(s * chunk, chunk), chunk)
                d, m, v = _adamw_math(w_refs[i][r, :], g_refs[i][r, :], m_refs[i][r, :], v_refs[i][r, :])
                d_out[i][r, :] = d
                m_out[i][r, :] = m
                v_out[i][r, :] = v
                return carry

            lax.fori_loop(0, steps, step, 0)

    shapes = [jax.ShapeDtypeStruct(g.shape, F32) for g in gs]
    outs = pl.pallas_call(
        body, name="adamw_shards", out_shape=shapes * 3,
        in_specs=[_vmem()] * (4 * n), out_specs=[_vmem()] * (3 * n),
        compiler_params=_params(vmem=VMEM_LIMIT_V7X),
    )(*gs, *ws, *ms, *vs)
    return outs[:n], outs[n:2 * n], outs[2 * n:]


def _small_sum(gathered):
    rows = gathered.shape[1]
    chunk, steps = _row_chunks(rows)

    def body(ga_ref, g_out):
        def step(s, carry):
            r = pl.ds(pl.multiple_of(s * chunk, chunk), chunk)
            g = ga_ref[0, r, :]
            for j in range(1, N_DEV):
                g = g + ga_ref[j, r, :]
            g_out[r, :] = g
            return carry

        lax.fori_loop(0, steps, step, 0)

    return pl.pallas_call(
        body, name="small_sum", out_shape=jax.ShapeDtypeStruct(gathered.shape[1:], F32),
        in_specs=[_vmem()], out_specs=_vmem(),
    )(gathered)


def _adamw_small(gs, ws, ms, vs):
    n = len(gs)

    def body(*refs):
        g_refs, w_refs, m_refs, v_refs = (refs[k * n:(k + 1) * n] for k in range(4))
        d_out, m_out, v_out = (refs[(4 + k) * n:(5 + k) * n] for k in range(3))
        for i in range(n):
            def update(idx, i=i):
                d, mm, vv = _adamw_math(w_refs[i][idx], g_refs[i][idx], m_refs[i][idx], v_refs[i][idx])
                d_out[i][idx] = d
                m_out[i][idx] = mm
                v_out[i][idx] = vv

            if len(gs[i].shape) == 3:
                def step(s, carry, update=update):
                    update(s)
                    return carry

                lax.fori_loop(0, gs[i].shape[0], step, 0)
            else:
                update(Ellipsis)

    shapes = [jax.ShapeDtypeStruct(g.shape, F32) for g in gs]
    outs = pl.pallas_call(
        body, name="adamw_small", out_shape=shapes * 3,
        in_specs=[_vmem()] * (4 * n), out_specs=[_vmem()] * (3 * n),
        compiler_params=_params(vmem=VMEM_LIMIT_V7X),
    )(*gs, *ws, *ms, *vs)
    return outs[:n], outs[n:2 * n], outs[2 * n:]


def _zoh(lr, li, logdt, btr, bti):
    dt = jnp.exp(logdt)
    mag = jnp.exp(lr * dt)
    th = li * dt
    ar = mag * jnp.cos(th)
    ai = mag * jnp.sin(th)
    den = lr * lr + li * li
    nr = ar - 1.0
    cr = (nr * lr + ai * li) / den
    ci = (ai * lr - nr * li) / den
    return ar, ai, cr * btr - ci * bti, cr * bti + ci * btr


BD_GROUPS = 8
BD_ROWS = BD_GROUPS * SSM_GROUP
BD_COLS = BD_GROUPS * SSM_STATE
N_BD = SSM_GROUPS // BD_GROUPS


def _bd_mask():
    r = lax.broadcasted_iota(jnp.int32, (BD_ROWS, BD_COLS), 0) // SSM_GROUP
    c = lax.broadcasted_iota(jnp.int32, (BD_ROWS, BD_COLS), 1) // SSM_STATE
    return r == c


def _blockdiag_store(out_ref, t):
    mask = _bd_mask()
    for j in range(N_BD):
        rows = t[j * BD_GROUPS:(j + 1) * BD_GROUPS].reshape(BD_ROWS, SSM_STATE)
        out_ref[j] = jnp.where(mask, jnp.tile(rows, (1, BD_GROUPS)), 0.0).astype(out_ref.dtype)


def _blockdiag_load(m_ref, fold):
    mask = _bd_mask()
    parts = [_hdot(jnp.where(mask, m_ref[j], 0.0), fold).reshape(BD_GROUPS, SSM_GROUP, SSM_STATE)
             for j in range(N_BD)]
    return jnp.concatenate(parts, axis=0)


def _zoh_fwd(lr, li, logdt, btr, bti, c_re, c_im):
    def body(lr_ref, li_ref, dt_ref, br_ref, bi_ref, cr_ref, ci_ref, ar_ref, ai_ref, bbr_ref, bbi_ref, ccr_ref,
             cci_ref):
        ar, ai, bbr, bbi = _zoh(lr_ref[...], li_ref[...], dt_ref[...], br_ref[...], bi_ref[...])
        ar_ref[...] = ar
        ai_ref[...] = ai
        _blockdiag_store(bbr_ref, bbr)
        _blockdiag_store(bbi_ref, bbi)
        _blockdiag_store(ccr_ref, cr_ref[...])
        _blockdiag_store(cci_ref, ci_ref[...])

    s = jax.ShapeDtypeStruct
    bd = s((N_BD, BD_ROWS, BD_COLS), BF16)
    return pl.pallas_call(
        body, name="zoh_fwd", out_shape=[s(lr.shape, F32), s(lr.shape, F32), bd, bd, bd, bd],
        in_specs=[_vmem()] * 7, out_specs=[_vmem()] * 6,
    )(lr, li, logdt, btr, bti, c_re, c_im)


def _zoh_bwd(lr, li, logdt, btr, bti, dar, dai, dbb_re, dbb_im, dcc_re, dcc_im, fold):
    def body(lr_ref, li_ref, dt_ref, br_ref, bi_ref, dar_ref, dai_ref, dbbr_ref, dbbi_ref, dccr_ref, dcci_ref,
             fold_ref, glr_ref, gli_ref, gdt_ref, gbr_ref, gbi_ref, gcr_ref, gci_ref):
        fold_m = fold_ref[...]
        _, vjp = jax.vjp(_zoh, lr_ref[...], li_ref[...], dt_ref[...], br_ref[...], bi_ref[...])
        glr, gli, gdt, gbr, gbi = vjp((dar_ref[...], dai_ref[...], _blockdiag_load(dbbr_ref, fold_m),
                                       _blockdiag_load(dbbi_ref, fold_m)))
        glr_ref[...] = glr
        gli_ref[...] = gli
        gdt_ref[...] = gdt
        gbr_ref[...] = gbr
        gbi_ref[...] = gbi
        gcr_ref[...] = _blockdiag_load(dccr_ref, fold_m)
        gci_ref[...] = _blockdiag_load(dcci_ref, fold_m)

    s = jax.ShapeDtypeStruct
    return pl.pallas_call(
        body, name="zoh_bwd",
        out_shape=[s(lr.shape, F32), s(lr.shape, F32), s(logdt.shape, F32)] + [s(btr.shape, F32)] * 4,
        in_specs=[_vmem()] * 12, out_specs=[_vmem()] * 7,
    )(lr, li, logdt, btr, bti, dar, dai, dbb_re, dbb_im, dcc_re, dcc_im, fold)


def _head_ones():
    r = np.arange(ATTN_W) // HEAD_DIM
    return jnp.asarray(r[:, None] == r[None, :], dtype=BF16)


def _head_fold():
    return jnp.asarray(np.tile(np.eye(HEAD_DIM), (ATTN_W // HEAD_DIM, 1)), dtype=BF16)


def _in_proj(x2, g_mix, w_in_sh):
    t_tok = x2.shape[0]
    tm = min(1024, t_tok)
    nt = t_tok // tm
    ag_w = _AllGather(1, cast=True)
    n_sem = len(ag_w.scratch())

    def owner(i):
        x, y, c = lax.axis_index("x"), lax.axis_index("y"), lax.axis_index("c")
        rel = jnp.where(i < 2, 0, (i - 2) % 3 + 1)
        px = jnp.where((rel == 1) | (rel == 3), 1 - x, x)
        py = jnp.where((rel == 2) | (rel == 3), 1 - y, y)
        pc = jnp.where((i == 1) | (i >= 5), 1 - c, c)
        return 4 * px + 2 * py + pc

    def body(*refs):
        x_ref, g_ref, w_ref, z_ref, xn_ref, wg_ref, xn_scr, w_land = refs[:8]
        sems_w, out_sem = refs[8:8 + n_sem], refs[8 + n_sem]
        i, t = pl.program_id(0), pl.program_id(1)
        _, first, passed, arrive_ici, arrive_d2d, _ = ag_w._plan([w_ref], [w_land], sems_w)

        @pl.when((i == 0) & (t == 0))
        def _():
            ag_w.start([w_ref], [w_land], sems_w)

        @pl.when((i == 1) & (t == 0))
        def _():
            arrive_d2d[0].wait_recv()

        for n in range(3):
            @pl.when((i == 2 + n) & (t == 0))
            def _(n=n):
                arrive_ici[n].wait_recv()
                passed[n].start()

            @pl.when((i == 5 + n) & (t == 0))
            def _(n=n):
                arrive_d2d[1 + n].wait_recv()

        @pl.when(i == 0)
        def _():
            x = x_ref[...]
            r = lax.rsqrt(jnp.mean(x * x, axis=-1, keepdims=True) + EPS)
            xn = (x * r * g_ref[...]).astype(BF16)
            xn_ref[...] = xn
            xn_scr[t] = xn

        z_ref[...] = _dot(xn_scr[t], w_land[owner(i)])

        @pl.when((i == N_DEV - 1) & (t == nt - 1))
        def _():
            for cp in first + passed:
                cp.wait_send()
            out = pltpu.make_async_copy(w_land, wg_ref, out_sem)
            out.start()
            out.wait()

    s = jax.ShapeDtypeStruct
    xmap = lambda i, t: (jnp.where(i == 0, t, nt - 1), 0)
    gathered = s((N_DEV,) + w_in_sh.shape, BF16)
    return pl.pallas_call(
        body, name="in_proj", grid=(N_DEV, nt),
        out_shape=[s((t_tok, IN_W), F32), s((t_tok, D_MODEL), BF16), gathered],
        in_specs=[pl.BlockSpec((tm, D_MODEL), xmap), _full(g_mix.shape), _full(w_in_sh.shape)],
        out_specs=[pl.BlockSpec((tm, COL_W), lambda i, t: (t, owner(i))), pl.BlockSpec((tm, D_MODEL), xmap),
                   pl.BlockSpec(memory_space=pl.ANY)],
        scratch_shapes=[pltpu.VMEM((nt, tm, D_MODEL), BF16), pltpu.VMEM(gathered.shape, BF16)] + ag_w.scratch()
        + [pltpu.SemaphoreType.DMA],
        compiler_params=_params(2, VMEM_LIMIT_V7X),
    )(x2, g_mix, w_in_sh)


TQ = 128
NEG = -1e30


def _head_col(t, lm):
    return jnp.max(jnp.where(lm, t, NEG), axis=-1, keepdims=True)


def _head_masks():
    lane = lax.broadcasted_iota(jnp.int32, (1, 1, LANES), 2)
    return [(lane // HEAD_DIM) == h for h in range(LANES // HEAD_DIM)]


def _stack_heads(t3, lms):
    return jnp.concatenate([jnp.where(lm, t3, jnp.zeros_like(t3)) for lm in lms], axis=1)


def _unstack_heads(t2, lms, tq):
    out = t2[:, :tq]
    for h in range(1, len(lms)):
        out = jnp.where(lms[h], t2[:, h * tq:(h + 1) * tq], out)
    return out


def _gather_classes(ref, dil, nt, tq, dtype):
    length = nt * tq
    if dil == 1:
        return ref[...].astype(dtype).reshape(nt, tq, LANES)
    parts = [ref[pl.ds(r, length, stride=dil), :].astype(dtype).reshape(nt, tq, LANES) for r in range(dil)]
    return jnp.concatenate(parts, axis=0)


def _scatter_classes(ref, val, dil, nt, tq, add):
    length = nt * tq
    for r in range(dil):
        rows = pl.ds(r, length, stride=dil) if dil > 1 else slice(None)
        part = val[r * nt:(r + 1) * nt].reshape(length, LANES)
        ref[rows, :] = ref[rows, :] + part if add else part


def _with_prev_tile(t3, dil, nt):
    parts = []
    for r in range(dil):
        t = t3[r * nt:(r + 1) * nt]
        parts.append(jnp.concatenate([t[:1], t[:-1]], axis=0))
    prev = parts[0] if dil == 1 else jnp.concatenate(parts, axis=0)
    return jnp.concatenate([prev, t3], axis=1)


def _band_valid(dil, nt, tq):
    if nt == 1:
        shape = (dil, tq, tq)
        return lax.broadcasted_iota(jnp.int32, shape, 1) >= lax.broadcasted_iota(jnp.int32, shape, 2)
    shape = (dil * nt, tq, 2 * tq)
    b = lax.broadcasted_iota(jnp.int32, shape, 0)
    c = lax.broadcasted_iota(jnp.int32, shape, 2)
    d = tq + lax.broadcasted_iota(jnp.int32, shape, 1) - c
    return (d >= 0) & (d <= tq) & (((b & (nt - 1)) != 0) | (c >= tq))


def _window_tiling(seq, window, dil):
    length = seq // dil
    tq = min(TQ, length)
    nt = length // tq
    assert length % tq == 0 and nt & (nt - 1) == 0 and (nt == 1 or window == tq * dil)
    return nt, tq


def _bqk(a, b):
    return jnp.einsum("bqd,bkd->bqk", a, b, preferred_element_type=F32)


def _bqd(a, b):
    return jnp.einsum("bqk,bkd->bqd", a, b, preferred_element_type=F32)


def _bkd(a, b):
    return jnp.einsum("bqk,bqd->bkd", a, b, preferred_element_type=F32)


def _qk_hat(q_ref, k_ref, gq_ref, gk_ref):
    lane = lax.broadcasted_iota(jnp.int32, (1, LANES), 1)

    def norm(raw, gain, scale):
        sq = raw * raw
        r = jnp.zeros_like(raw)
        for h in range(LANES // HEAD_DIM):
            lm = (lane // HEAD_DIM) == h
            ms = jnp.sum(jnp.where(lm, sq, 0.0), axis=-1, keepdims=True) * (1.0 / HEAD_DIM)
            r = jnp.where(lm, lax.rsqrt(ms + EPS), r)
        return raw * r * gain * scale

    return norm(q_ref[...], gq_ref[...], HEAD_DIM ** -0.5), norm(k_ref[...], gk_ref[...], 1.0)


def _zblock(seq, group):
    return pl.BlockSpec((seq, LANES), lambda b, hp: (b, group * (ATTN_W // LANES) + hp))


def _attn_fwd(z, gq2, gk2, nb, seq, late_sh):
    t_tok = nb * seq
    n_win = len(DILATED)
    host = _HostedGather(late_sh)
    n_late = host.n
    n_steps = (nb, ATTN_W // LANES)

    def body(*refs):
        (q_ref, k_ref, v_ref, ga_ref, gq_ref, gk_ref), refs = refs[:6], refs[6:]
        late_refs, refs = refs[:n_late], refs[n_late:]
        (o_ref, l_ref, ag_ref), refs = refs[:3], refs[3:]
        lateg_refs, refs = refs[:n_late], refs[n_late:]
        (qf, kf, oc, lc), host_scratch = refs[:4], refs[4:]
        step = pl.program_id(0) * n_steps[1] + pl.program_id(1)
        total = n_steps[0] * n_steps[1]

        @pl.when(step == 0)
        def _():
            host.start(late_refs, host_scratch)

        @pl.when(step == total // 2)
        def _():
            host.forward(late_refs, host_scratch)

        qf[...], kf[...] = _qk_hat(q_ref, k_ref, gq_ref, gk_ref)
        lms = _head_masks()
        for w, (window, dil) in enumerate(DILATED):
            nt, tq = _window_tiling(seq, window, dil)
            q3 = _gather_classes(qf, dil, nt, tq, BF16)
            k3 = _gather_classes(kf, dil, nt, tq, BF16)
            v3 = _gather_classes(v_ref, dil, nt, tq, BF16)
            if nt > 1:
                k3, v3 = _with_prev_tile(k3, dil, nt), _with_prev_tile(v3, dil, nt)
            valid = _band_valid(dil, nt, tq)
            valid = jnp.concatenate([valid] * len(lms), axis=1)
            s = _bqk(_stack_heads(q3, lms), k3)
            m = jnp.max(jnp.where(valid, s, NEG), axis=-1, keepdims=True)
            p = jnp.where(valid, jnp.exp(s - m), 0.0)
            den = jnp.sum(p, axis=-1, keepdims=True)
            o = _unstack_heads(_bqd(p.astype(BF16), v3) / den, lms, tq)
            lse = _unstack_heads(jnp.broadcast_to(m + jnp.log(den), s.shape[:2] + (LANES,)), lms, tq)
            _scatter_classes(oc.at[w], o, dil, nt, tq, add=False)
            _scatter_classes(lc.at[w], lse, dil, nt, tq, add=False)
        mx = lc[0]
        for w in range(1, n_win):
            mx = jnp.maximum(mx, lc[w])
        tot = jnp.zeros_like(mx)
        o = jnp.zeros_like(mx)
        for w in range(n_win):
            e = jnp.exp(lc[w] - mx)
            tot = tot + e
            o = o + e * oc[w]
        o = o / tot
        o_ref[...] = o
        l_ref[...] = mx + jnp.log(tot)
        ga = ga_ref[...]
        ag_ref[...] = (o * ga * _sig(ga)).astype(BF16)

        @pl.when(step == total - 1)
        def _():
            host.finish(late_refs, host_scratch, lateg_refs)

    blk = pl.BlockSpec((seq, LANES), lambda b, hp: (b, hp))
    s = jax.ShapeDtypeStruct
    outs = pl.pallas_call(
        body, name="attn_fwd", grid=n_steps,
        out_shape=[s((t_tok, ATTN_W), F32), s((t_tok, ATTN_W), F32), s((t_tok, ATTN_W), BF16)] + host.out_shape(),
        in_specs=[_zblock(seq, 0), _zblock(seq, 1), _zblock(seq, 2), _zblock(seq, 3), _full(gq2.shape),
                  _full(gk2.shape)] + [_full(a.shape) for a in late_sh],
        out_specs=[blk, blk, blk] + [pl.BlockSpec(memory_space=pl.ANY)] * n_late,
        scratch_shapes=[pltpu.VMEM((seq, LANES), F32)] * 2 + [pltpu.VMEM((n_win, seq, LANES), F32)] * 2
        + host.scratch(),
        compiler_params=_params(2, VMEM_LIMIT_V7X),
    )(z, z, z, z, gq2, gk2, *late_sh)
    return outs[:3], outs[3:]


SCAN_COLS = 512


def _to_segments(dst_ref, val):
    seg = val.shape[0] // SUBLANES
    for n in range(dst_ref.shape[0]):
        for s in range(SUBLANES):
            dst_ref[n, pl.ds(s, seg, stride=SUBLANES), :] = val[s * seg:(s + 1) * seg, n * LANES:(n + 1) * LANES]


def _from_segments(src_ref):
    seg = src_ref.shape[1] // SUBLANES
    return jnp.concatenate(
        [jnp.concatenate([src_ref[n, pl.ds(s, seg, stride=SUBLANES), :] for s in range(SUBLANES)], axis=0)
         for n in range(src_ref.shape[0])], axis=1)


def _scan_chunk(re_ref, im_ref, a_re_ref, a_im_ref, carry_re, carry_im, rows, reverse, visit=None):
    seg = rows // SUBLANES
    assert seg & (seg - 1) == 0
    rowi = lax.broadcasted_iota(jnp.int32, (SUBLANES, SCAN_COLS), 0)
    edge = (SUBLANES - 1) if reverse else 0
    last = 0 if reverse else SUBLANES - 1
    at_edge = rowi == edge

    def cmul(ar, ai, br, bi):
        return ar * br - ai * bi, ar * bi + ai * br

    for c0 in range(0, N_STATE, SCAN_COLS):
        cols = slice(c0, c0 + SCAN_COLS)
        a1r = jnp.broadcast_to(a_re_ref[:, cols], (SUBLANES, SCAN_COLS))
        a1i = jnp.broadcast_to(a_im_ref[:, cols], (SUBLANES, SCAN_COLS))
        if reverse:
            a1i = -a1i

        def block_of(i):
            j = (seg - 1 - i) if reverse else i
            return j, pl.ds(pl.multiple_of(j * SUBLANES, SUBLANES), SUBLANES)

        def local(i, carry, cols=cols, a1r=a1r, a1i=a1i):
            xr, xi = carry
            _, blk = block_of(i)
            nr, ni = cmul(a1r, a1i, xr, xi)
            xr, xi = nr + re_ref[blk, cols], ni + im_ref[blk, cols]
            re_ref[blk, cols] = xr
            im_ref[blk, cols] = xi
            return xr, xi

        zero = jnp.zeros((SUBLANES, SCAN_COLS), F32)
        er, ei = lax.fori_loop(0, seg, local, (zero, zero))

        pr, pi = a1r, a1i
        for _ in range(seg.bit_length() - 1):
            pr, pi = cmul(pr, pi, pr, pi)
        cr, ci = carry_re[:, cols], carry_im[:, cols]
        inr, ini = cmul(pr, pi, cr, ci)
        er = er + jnp.where(at_edge, inr, 0.0)
        ei = ei + jnp.where(at_edge, ini, 0.0)
        for sft in (1, 2, 4):
            shift, keep = (SUBLANES - sft, rowi < SUBLANES - sft) if reverse else (sft, rowi >= sft)
            rs = jnp.where(keep, pltpu.roll(er, shift, 0), 0.0)
            ims = jnp.where(keep, pltpu.roll(ei, shift, 0), 0.0)
            dr, di = cmul(pr, pi, rs, ims)
            er, ei = er + dr, ei + di
            pr, pi = cmul(pr, pi, pr, pi)
        carry_re[:, cols] = jnp.broadcast_to(er[last:last + 1, :], (SUBLANES, SCAN_COLS))
        carry_im[:, cols] = jnp.broadcast_to(ei[last:last + 1, :], (SUBLANES, SCAN_COLS))
        one = (SUBLANES - 1) if reverse else 1
        kr = jnp.where(at_edge, cr, pltpu.roll(er, one, 0))
        ki = jnp.where(at_edge, ci, pltpu.roll(ei, one, 0))

        def fix(i, carry, cols=cols, a1r=a1r, a1i=a1i):
            kr, ki, acc = carry
            j, blk = block_of(i)
            kr, ki = cmul(a1r, a1i, kr, ki)
            xr, xi = re_ref[blk, cols] + kr, im_ref[blk, cols] + ki
            re_ref[blk, cols] = xr
            im_ref[blk, cols] = xi
            if visit is not None:
                acc = visit(cols, j, xr, xi, acc)
            return kr, ki, acc

        _, _, acc = lax.fori_loop(0, seg, fix, (kr, ki, (zero, zero)))
        if visit is not None:
            visit(cols, None, None, None, acc)


SSM_CHUNK = 512


def _ssm_fwd(z, a_re, a_im, bb_re, bb_im, cc_re, cc_im, d_skip, w_glu, b_glu, nb, seq):
    t_tok = nb * seq
    tc = min(SSM_CHUNK, seq)
    nch = seq // tc
    grp = N_STATE // 4

    def body(u_ref, gs_ref, ar_ref, ai_ref, bbr_ref, bbi_ref, ccr_ref, cci_ref, d_ref, wg_ref, bg_ref,
             xr_ref, xi_ref, y_ref, sg_ref, car_re, car_im, seg_u, seg_y):
        @pl.when(pl.program_id(1) == 0)
        def _():
            car_re[...] = jnp.zeros_like(car_re)
            car_im[...] = jnp.zeros_like(car_im)

        u = u_ref[...]
        _to_segments(seg_u, u)
        for j in range(4):
            uj = seg_u[j].astype(BF16)
            xr_ref[:, j * grp:(j + 1) * grp] = _dot(uj, bbr_ref[j])
            xi_ref[:, j * grp:(j + 1) * grp] = _dot(uj, bbi_ref[j])
        _scan_chunk(xr_ref, xi_ref, ar_ref, ai_ref, car_re, car_im, tc, reverse=False)
        for j in range(4):
            xr = xr_ref[:, j * grp:(j + 1) * grp].astype(BF16)
            xi = xi_ref[:, j * grp:(j + 1) * grp].astype(BF16)
            seg_y[j] = _dot_nt(xr, ccr_ref[j]) - _dot_nt(xi, cci_ref[j])
        y = _from_segments(seg_y) + d_ref[...] * u
        y_ref[...] = y
        yg, _ = _gelu_and_grad(y)
        gl = _dot(yg.astype(BF16), wg_ref[...]) + bg_ref[...]
        gs = gs_ref[...]
        sg_ref[...] = (yg * _sig(gl) * gs * _sig(gs)).astype(BF16)

    umap = lambda b, ch: (b * nch + ch, 4)
    gmap = lambda b, ch: (b * nch + ch, 5)
    row = lambda b, ch: (b * nch + ch, 0)
    s = jax.ShapeDtypeStruct
    consts = [a_re, a_im, bb_re, bb_im, cc_re, cc_im, d_skip, w_glu, b_glu]
    return pl.pallas_call(
        body, name="ssm_fwd", grid=(nb, nch),
        out_shape=[s((t_tok, N_STATE), F32), s((t_tok, N_STATE), F32), s((t_tok, SSM_W), F32),
                   s((t_tok, SSM_W), BF16)],
        in_specs=[pl.BlockSpec((tc, SSM_W), umap), pl.BlockSpec((tc, SSM_W), gmap)] + [_full(c.shape) for c in consts],
        out_specs=[pl.BlockSpec((tc, N_STATE), row), pl.BlockSpec((tc, N_STATE), row),
                   pl.BlockSpec((tc, SSM_W), row), pl.BlockSpec((tc, SSM_W), row)],
        scratch_shapes=[pltpu.VMEM((SUBLANES, N_STATE), F32), pltpu.VMEM((SUBLANES, N_STATE), F32),
                        pltpu.VMEM((4, tc, LANES), F32), pltpu.VMEM((4, tc, LANES), F32)],
        compiler_params=_params(2, VMEM_LIMIT_V7X),
    )(z, z, *consts)


def _tail(x2, tg2, ag, sg, p2, w_out, w_g, w_p, g_ple):
    t_tok = x2.shape[0]
    tm = min(256, t_tok)
    nt = t_tok // tm
    half = ATTN_W

    def body(x_ref, tg_ref, ag_ref, sg_ref, p_ref, wo_ref, wg_ref, wp_ref, gp_ref,
             dmix_ref, dh1_ref, loss_ref, dgp_ref, dwo_ref, dwg_ref, dwp_ref, acc_o, acc_g, acc_p):
        i = pl.program_id(0)

        @pl.when(i == 0)
        def _():
            loss_ref[...] = jnp.zeros_like(loss_ref)
            dgp_ref[...] = jnp.zeros_like(dgp_ref)
            acc_o[...] = jnp.zeros_like(acc_o)
            acc_g[...] = jnp.zeros_like(acc_g)
            acc_p[...] = jnp.zeros_like(acc_p)

        ag_t, sg_t = ag_ref[...], sg_ref[...]
        h1 = x_ref[...] + _dot(ag_t, wo_ref[0:half, :]) + _dot(sg_t, wo_ref[half:2 * half, :])
        r2 = lax.rsqrt(jnp.mean(h1 * h1, axis=-1, keepdims=True) + EPS)
        hnorm = h1 * r2
        gp = gp_ref[...]
        hn = (hnorm * gp).astype(BF16)
        gate = _sig(_dot(hn, wg_ref[...]))
        pb = p_ref[...].astype(BF16)
        pp = jnp.concatenate([_dot(pb, wp_ref[j]) for j in range(N_DEV)], axis=-1)
        h2 = h1 + gate * pp
        err = h2 - tg_ref[...]
        loss_ref[...] += 0.5 * jnp.sum(err * err) * (1.0 / D_MODEL)
        dh2 = err * (1.0 / D_MODEL)
        dpp = (dh2 * gate).astype(BF16)
        dgpre = (dh2 * pp * gate * (1.0 - gate)).astype(BF16)
        acc_p[...] += _dot_tn(pb, dpp)
        acc_g[...] += _dot_tn(hn, dgpre)
        dhn = _dot_nt(dgpre, wg_ref[...])
        dgp_ref[...] += jnp.sum(dhn * hnorm, axis=0, keepdims=True)
        a = dhn * gp
        dh1 = dh2 + r2 * (a - hnorm * jnp.mean(a * hnorm, axis=-1, keepdims=True))
        dh1_ref[...] = dh1
        dh1b = dh1.astype(BF16)
        acc_o[0:half, :] += _dot_tn(ag_t, dh1b)
        acc_o[half:2 * half, :] += _dot_tn(sg_t, dh1b)
        dmix_ref[...] = _dot_nt(dh1b, wo_ref[...])

        @pl.when(i == nt - 1)
        def _():
            dwo_ref[...] = acc_o[...].astype(BF16)
            dwg_ref[...] = acc_g[...].astype(BF16)
            for j in range(N_DEV):
                dwp_ref[j] = acc_p[:, j * LANES:(j + 1) * LANES].astype(BF16)

    row = lambda i: (i, 0)
    s = jax.ShapeDtypeStruct
    return pl.pallas_call(
        body, name="tail_fwd_bwd", grid=(nt,),
        out_shape=[s((t_tok, D_MODEL), F32), s((t_tok, D_MODEL), F32), s((SUBLANES, LANES), F32),
                   s((1, D_MODEL), F32), s((D_MODEL, D_MODEL), BF16), s((D_MODEL, D_MODEL), BF16),
                   s((N_DEV, PLE_DIM, LANES), BF16)],
        in_specs=[pl.BlockSpec((tm, D_MODEL), row), pl.BlockSpec((tm, D_MODEL), row),
                  pl.BlockSpec((tm, half), row), pl.BlockSpec((tm, half), row), pl.BlockSpec((tm, PLE_DIM), row),
                  _full(w_out.shape), _full(w_g.shape), _full(w_p.shape), _full(g_ple.shape)],
        out_specs=[pl.BlockSpec((tm, D_MODEL), row), pl.BlockSpec((tm, D_MODEL), row), _full((SUBLANES, LANES)),
                   _full((1, D_MODEL)), _full((D_MODEL, D_MODEL)), _full((D_MODEL, D_MODEL)),
                   _full((N_DEV, PLE_DIM, LANES))],
        scratch_shapes=[pltpu.VMEM((D_MODEL, D_MODEL), F32), pltpu.VMEM((D_MODEL, D_MODEL), F32),
                        pltpu.VMEM((PLE_DIM, D_MODEL), F32)],
        compiler_params=_params(1, VMEM_LIMIT_V7X),
    )(x2, tg2, ag, sg, p2, w_out, w_g, w_p, g_ple)


def _attn_bwd(z, gq2, gk2, o, lse, dmix, nb, seq, parts):
    t_tok = nb * seq
    n_rs = len(parts)
    rs = _ReduceScatter([p.shape for p in parts])
    n_steps = (nb, ATTN_W // LANES)

    def body(*refs):
        (q_ref, k_ref, v_ref, ga_ref, gq_ref, gk_ref, o_ref, l_ref, da_ref), refs = refs[:9], refs[9:]
        part_refs, refs = refs[:n_rs], refs[n_rs:]
        (dq_ref, dk_ref, dv_ref, dga_ref), refs = refs[:4], refs[4:]
        g_refs, refs = refs[:n_rs], refs[n_rs:]
        (qf, kf, dof, dlf), rs_scratch = refs[:4], refs[4:]
        b, hp = pl.program_id(0), pl.program_id(1)

        @pl.when((b == 0) & (hp == 0))
        def _():
            rs.start(part_refs, rs_scratch)

        ga, o_t, da = ga_ref[...], o_ref[...], da_ref[...]
        sga = _sig(ga)
        d_o = da * ga * sga
        dga_ref[...] = da * o_t * sga * (1.0 + ga * (1.0 - sga))
        lane = lax.broadcasted_iota(jnp.int32, (1, LANES), 1)
        d_oo = d_o * o_t
        delta = jnp.zeros_like(d_oo)
        for h in range(LANES // HEAD_DIM):
            lm2 = (lane // HEAD_DIM) == h
            delta = jnp.where(lm2, jnp.sum(jnp.where(lm2, d_oo, 0.0), axis=-1, keepdims=True), delta)
        qf[...], kf[...] = _qk_hat(q_ref, k_ref, gq_ref, gk_ref)
        dof[...] = d_o
        dlf[...] = delta
        dq_ref[...] = jnp.zeros_like(dq_ref)
        dk_ref[...] = jnp.zeros_like(dk_ref)
        dv_ref[...] = jnp.zeros_like(dv_ref)
        lms = _head_masks()
        for window, dil in DILATED:
            nt, tq = _window_tiling(seq, window, dil)
            q3 = _gather_classes(qf, dil, nt, tq, BF16)
            k3 = _gather_classes(kf, dil, nt, tq, BF16)
            v3 = _gather_classes(v_ref, dil, nt, tq, BF16)
            do3 = _gather_classes(dof, dil, nt, tq, BF16)
            lt3 = _gather_classes(l_ref, dil, nt, tq, F32)
            dl3 = _gather_classes(dlf, dil, nt, tq, F32)
            if nt > 1:
                k3, v3 = _with_prev_tile(k3, dil, nt), _with_prev_tile(v3, dil, nt)
            valid = _band_valid(dil, nt, tq)
            dq = jnp.zeros(q3.shape, F32)
            dk = jnp.zeros(k3.shape, F32)
            dv = jnp.zeros(k3.shape, F32)
            for lm in lms:
                qm = jnp.where(lm, q3, jnp.zeros_like(q3))
                dom = jnp.where(lm, do3, jnp.zeros_like(do3))
                p = jnp.where(valid, jnp.exp(_bqk(qm, k3) - _head_col(lt3, lm)), 0.0)
                dv = dv + _bkd(p.astype(BF16), dom)
                ds = (p * (_bqk(dom, v3) - _head_col(dl3, lm))).astype(BF16)
                dq = dq + jnp.where(lm, _bqd(ds, k3), 0.0)
                dk = dk + _bkd(ds, qm)
            _scatter_classes(dq_ref, dq, dil, nt, tq, add=True)
            for ref, g in ((dk_ref, dk), (dv_ref, dv)):
                if nt > 1:
                    own, prev = g[:, tq:, :], g[:, :tq, :]
                    shifted = []
                    for r in range(dil):
                        t = prev[r * nt:(r + 1) * nt]
                        shifted.append(jnp.concatenate([t[1:], jnp.zeros_like(t[:1])], axis=0))
                    g = own + (shifted[0] if dil == 1 else jnp.concatenate(shifted, axis=0))
                _scatter_classes(ref, g, dil, nt, tq, add=True)

        @pl.when((b == n_steps[0] - 1) & (hp == n_steps[1] - 1))
        def _():
            rs.finish(part_refs, rs_scratch, g_refs)

    blk = pl.BlockSpec((seq, LANES), lambda b, hp: (b, hp))
    s = jax.ShapeDtypeStruct
    outs = pl.pallas_call(
        body, name="attn_bwd", grid=n_steps,
        out_shape=[s((t_tok, ATTN_W), F32)] * 4 + [s(p.shape[1:], F32) for p in parts],
        in_specs=[_zblock(seq, 0), _zblock(seq, 1), _zblock(seq, 2), _zblock(seq, 3), _full(gq2.shape),
                  _full(gk2.shape), blk, blk, blk] + [pl.BlockSpec(memory_space=pl.ANY)] * n_rs,
        out_specs=[blk] * 4 + [_full(p.shape[1:]) for p in parts],
        scratch_shapes=[pltpu.VMEM((seq, LANES), F32)] * 4 + rs.scratch(parts[0].dtype),
        compiler_params=_params(2, VMEM_LIMIT_V7X),
    )(z, z, z, z, gq2, gk2, o, lse, dmix, *parts)
    return outs[:4], outs[4:]


def _ssm_bwd(z, dmix, y, x_re, x_im, a_re, a_im, bb_re, bb_im, cc_re, cc_im, d_skip, w_glu, b_glu, nb, seq):
    t_tok = nb * seq
    tc = min(SSM_CHUNK, seq)
    nch = seq // tc
    grp = N_STATE // 4

    def body(u_ref, gs_ref, ds_ref, y_ref, xr_ref, xi_ref, xpr_ref, xpi_ref,
             ar_ref, ai_ref, bbr_ref, bbi_ref, ccr_ref, cci_ref, d_ref, wg_ref, bg_ref,
             du_ref, dgs_ref, dwg_ref, dbg_ref, dd_ref, dar_ref, dai_ref, dbbr_ref, dbbi_ref, dccr_ref, dcci_ref,
             lam_re, lam_im, car_re, car_im, acc_wg, seg_a, seg_b, ent_re, ent_im):
        step = pl.program_id(1)
        first_chunk = step == nch - 1

        @pl.when((pl.program_id(0) == 0) & (step == 0))
        def _():
            acc_wg[...] = jnp.zeros_like(acc_wg)
            for ref in (dbg_ref, dd_ref, dar_ref, dai_ref, dbbr_ref, dbbi_ref, dccr_ref, dcci_ref):
                ref[...] = jnp.zeros_like(ref)

        @pl.when(step == 0)
        def _():
            car_re[...] = jnp.zeros_like(car_re)
            car_im[...] = jnp.zeros_like(car_im)

        u, gs, dssm, y = u_ref[...], gs_ref[...], ds_ref[...], y_ref[...]
        yg, dgelu = _gelu_and_grad(y)
        ygb = yg.astype(BF16)
        sgl = _sig(_dot(ygb, wg_ref[...]) + bg_ref[...])
        sgs = _sig(gs)
        dout = dssm * gs * sgs
        dgs_ref[...] = dssm * yg * sgl * sgs * (1.0 + gs * (1.0 - sgs))
        dgl = dout * yg * sgl * (1.0 - sgl)
        dglb = dgl.astype(BF16)
        dyg = dout * sgl + _dot_nt(dglb, wg_ref[...])
        acc_wg[...] += _dot_tn(ygb, dglb)
        dbg_ref[...] += jnp.sum(dgl, axis=0, keepdims=True)
        dy = dyg * dgelu
        dd_ref[...] += jnp.sum(dy * u, axis=0, keepdims=True)
        _to_segments(seg_a, dy)
        _to_segments(seg_b, u)
        for j in range(4):
            dyj = seg_a[j].astype(BF16)
            sl = slice(j * grp, (j + 1) * grp)
            lam_re[:, sl] = _dot(dyj, ccr_ref[j])
            lam_im[:, sl] = -_dot(dyj, cci_ref[j])
            dccr_ref[j] += _dot_tn(dyj, xr_ref[:, sl].astype(BF16))
            dcci_ref[j] -= _dot_tn(dyj, xi_ref[:, sl].astype(BF16))

        keep_prev = jnp.where(first_chunk, 0.0, 1.0)
        seg = tc // SUBLANES
        last_blk = pl.ds((seg - 1) * SUBLANES, SUBLANES)
        row0 = lax.broadcasted_iota(jnp.int32, (SUBLANES, N_STATE), 0) == 0
        for src, prev, dst in ((xr_ref, xpr_ref, ent_re), (xi_ref, xpi_ref, ent_im)):
            before = jnp.broadcast_to(prev[SUBLANES - 1:SUBLANES, :] * keep_prev, (SUBLANES, N_STATE))
            dst[...] = jnp.where(row0, before, pltpu.roll(src[last_blk, :], 1, 0))

        def visit(cols, j, lr, li, acc):
            if j is None:
                dar_ref[:, cols] += jnp.sum(acc[0], axis=0, keepdims=True)
                dai_ref[:, cols] += jnp.sum(acc[1], axis=0, keepdims=True)
                return None
            blk = pl.ds(pl.multiple_of(jnp.maximum(j - 1, 0) * SUBLANES, SUBLANES), SUBLANES)
            inside = j > 0
            xpr = jnp.where(inside, xr_ref[blk, cols], ent_re[:, cols])
            xpi = jnp.where(inside, xi_ref[blk, cols], ent_im[:, cols])
            return acc[0] + lr * xpr + li * xpi, acc[1] + li * xpr - lr * xpi

        _scan_chunk(lam_re, lam_im, ar_ref, ai_ref, car_re, car_im, tc, reverse=True, visit=visit)

        for j in range(4):
            sl = slice(j * grp, (j + 1) * grp)
            lr = lam_re[:, sl].astype(BF16)
            li = lam_im[:, sl].astype(BF16)
            uj = seg_b[j].astype(BF16)
            seg_a[j] = _dot_nt(lr, bbr_ref[j]) + _dot_nt(li, bbi_ref[j])
            dbbr_ref[j] += _dot_tn(uj, lr)
            dbbi_ref[j] += _dot_tn(uj, li)
        du_ref[...] = _from_segments(seg_a) + dy * d_ref[...]

        @pl.when((pl.program_id(0) == nb - 1) & (step == nch - 1))
        def _():
            dwg_ref[...] = acc_wg[...].astype(BF16)

    rev = lambda b, ch: b * nch + (nch - 1 - ch)
    umap = lambda b, ch: (rev(b, ch), 4)
    gmap = lambda b, ch: (rev(b, ch), 5)
    smap = lambda b, ch: (rev(b, ch), 1)
    row = lambda b, ch: (rev(b, ch), 0)
    prev = lambda b, ch: (jnp.maximum(rev(b, ch) * (tc // SUBLANES) - 1, 0), 0)
    s = jax.ShapeDtypeStruct
    consts = [a_re, a_im, bb_re, bb_im, cc_re, cc_im, d_skip, w_glu, b_glu]
    acc_shapes = [s((1, SSM_W), F32), s((1, SSM_W), F32), s((1, N_STATE), F32), s((1, N_STATE), F32),
                  s(bb_re.shape, F32), s(bb_re.shape, F32), s(cc_re.shape, F32), s(cc_re.shape, F32)]
    return pl.pallas_call(
        body, name="ssm_bwd", grid=(nb, nch),
        out_shape=[s((t_tok, SSM_W), F32), s((t_tok, SSM_W), F32), s((SSM_W, SSM_W), BF16)] + acc_shapes,
        in_specs=[pl.BlockSpec((tc, SSM_W), umap), pl.BlockSpec((tc, SSM_W), gmap), pl.BlockSpec((tc, SSM_W), smap),
                  pl.BlockSpec((tc, SSM_W), row), pl.BlockSpec((tc, N_STATE), row), pl.BlockSpec((tc, N_STATE), row),
                  pl.BlockSpec((SUBLANES, N_STATE), prev), pl.BlockSpec((SUBLANES, N_STATE), prev)]
        + [_full(c.shape) for c in consts],
        out_specs=[pl.BlockSpec((tc, SSM_W), row), pl.BlockSpec((tc, SSM_W), row), _full((SSM_W, SSM_W))]
        + [_full(a.shape) for a in acc_shapes],
        scratch_shapes=[pltpu.VMEM((tc, N_STATE), F32), pltpu.VMEM((tc, N_STATE), F32),
                        pltpu.VMEM((SUBLANES, N_STATE), F32), pltpu.VMEM((SUBLANES, N_STATE), F32),
                        pltpu.VMEM((SSM_W, SSM_W), F32), pltpu.VMEM((4, tc, LANES), F32),
                        pltpu.VMEM((4, tc, LANES), F32),
                        pltpu.VMEM((SUBLANES, N_STATE), F32), pltpu.VMEM((SUBLANES, N_STATE), F32)],
        compiler_params=_params(2, VMEM_LIMIT_V7X),
    )(z, z, dmix, y, x_re, x_im, x_re, x_im, *consts)


def _dz_and_dx(x2, z, dqh, dkh, dvb, dga, du, dgs, dh1, w_in_g, g_mix, gq_t, gk_t, ones_bd, fold):
    t_tok = x2.shape[0]
    tm = min(256, t_tok)
    nt = t_tok // tm
    a_w = ATTN_W

    def head_norm_bwd(raw, d_hat, gain, scale, ones):
        r = lax.rsqrt(_hdot(raw * raw, ones) * (1.0 / HEAD_DIM) + EPS)
        n = raw * r
        a = d_hat * gain * scale
        d_raw = r * (a - n * (_hdot(a * n, ones) * (1.0 / HEAD_DIM)))
        return d_raw, jnp.sum(d_hat * n * scale, axis=0, keepdims=True)

    def body(x_ref, q_ref, k_ref, dq_ref, dk_ref, dv_ref, dga_ref, du_ref, dgs_ref, dh1_ref, w_ref, g_ref,
             gq_ref, gk_ref, ones_ref, fold_ref, dz_ref, gx_ref, dgm_ref, dgq_ref, dgk_ref, acc_q, acc_k):
        i = pl.program_id(0)

        @pl.when(i == 0)
        def _():
            dgm_ref[...] = jnp.zeros_like(dgm_ref)
            acc_q[...] = jnp.zeros_like(acc_q)
            acc_k[...] = jnp.zeros_like(acc_k)

        ones = ones_ref[...]
        dq, sq = head_norm_bwd(q_ref[...], dq_ref[...], gq_ref[...], HEAD_DIM ** -0.5, ones)
        dk, sk = head_norm_bwd(k_ref[...], dk_ref[...], gk_ref[...], 1.0, ones)
        acc_q[...] += jnp.broadcast_to(sq, acc_q.shape)
        acc_k[...] += jnp.broadcast_to(sk, acc_k.shape)
        parts = (dq, dk, dv_ref[...], dga_ref[...], du_ref[...], dgs_ref[...])
        for n, part in enumerate(parts):
            dz_ref[:, n * a_w:(n + 1) * a_w] = part.astype(BF16)
        dxn = jnp.zeros((tm, D_MODEL), F32)
        for j in range(N_DEV):
            dxn = dxn + _dot_nt(dz_ref[:, j * COL_W:(j + 1) * COL_W], w_ref[j])
        x = x_ref[...]
        r1 = lax.rsqrt(jnp.mean(x * x, axis=-1, keepdims=True) + EPS)
        xnorm = x * r1
        dgm_ref[...] += jnp.sum(dxn * xnorm, axis=0, keepdims=True)
        a = dxn * g_ref[...]
        gx_ref[...] = dh1_ref[...] + r1 * (a - xnorm * jnp.mean(a * xnorm, axis=-1, keepdims=True))

        @pl.when(i == nt - 1)
        def _():
            dgq_ref[...] = _hdot(acc_q[...], fold_ref[...])
            dgk_ref[...] = _hdot(acc_k[...], fold_ref[...])

    row = lambda i: (i, 0)
    col = lambda n: (lambda i: (i, n))
    s = jax.ShapeDtypeStruct
    half = pl.BlockSpec((tm, a_w), row)
    return pl.pallas_call(
        body, name="dz_dx", grid=(nt,),
        out_shape=[s((t_tok, IN_W), BF16), s((t_tok, D_MODEL), F32), s((1, D_MODEL), F32),
                   s((SUBLANES, HEAD_DIM), F32), s((SUBLANES, HEAD_DIM), F32)],
        in_specs=[pl.BlockSpec((tm, D_MODEL), row), pl.BlockSpec((tm, a_w), col(0)), pl.BlockSpec((tm, a_w), col(1)),
                  half, half, half, half, half, half, pl.BlockSpec((tm, D_MODEL), row),
                  _full(w_in_g.shape), _full(g_mix.shape), _full(gq_t.shape), _full(gk_t.shape),
                  _full(ones_bd.shape), _full(fold.shape)],
        out_specs=[pl.BlockSpec((tm, IN_W), row), pl.BlockSpec((tm, D_MODEL), row), _full((1, D_MODEL)),
                   _full((SUBLANES, HEAD_DIM)), _full((SUBLANES, HEAD_DIM))],
        scratch_shapes=[pltpu.VMEM((SUBLANES, a_w), F32), pltpu.VMEM((SUBLANES, a_w), F32)],
        compiler_params=_params(1, VMEM_LIMIT_V7X),
    )(x2, z, z, dqh, dkh, dvb, dga, du, dgs, dh1, w_in_g, g_mix, gq_t, gk_t, ones_bd, fold)


def _dw_in(xn, dz, glu_parts, small):
    t_tok = xn.shape[0]
    tk = min(1024, t_tok)
    nk = t_tok // tk
    rs = _ReduceScatter([glu_parts.shape])
    ag = _AllGather(1, cast=False)
    n_rs = len(rs.scratch(BF16))

    def place():
        x, y, c = lax.axis_index("x"), lax.axis_index("y"), lax.axis_index("c")
        return x, y, c, [(1 - x, y), (x, 1 - y), (1 - x, 1 - y)]

    def target(i):
        x, y, c, _ = place()
        rel = jnp.where(i < 6, i // 2 + 1, 0)
        px = jnp.where((rel == 1) | (rel == 3), 1 - x, x)
        py = jnp.where((rel == 2) | (rel == 3), 1 - y, y)
        pc = jnp.where(i % 2 == 0, 1 - c, c)
        return 4 * px + 2 * py + pc

    chunk, chunks = _row_chunks(D_MODEL)

    def body(xn_ref, dz_ref, glu_ref, small_ref, gin_ref, gglu_ref, gath_ref, acc, stage, land, send_sems, recv_sems,
             *rest):
        rs_scratch, ag_sems = rest[:n_rs], rest[n_rs:]
        i, k = pl.program_id(0), pl.program_id(1)
        x, y, c, chips = place()

        def push(slot, to):
            return pltpu.make_async_remote_copy(
                src_ref=stage.at[slot], dst_ref=land.at[slot], send_sem=send_sems.at[slot],
                recv_sem=recv_sems.at[slot], device_id=to, device_id_type=MESH)

        pushes = [push(n, (x, y, 1 - c)) for n in range(4)] + [push(4 + n, (*chips[n], c)) for n in range(3)]

        def staged(slot, plus=None):
            def put(s, carry):
                r = pl.ds(pl.multiple_of(s * chunk, chunk), chunk)
                val = acc[r, :]
                if plus is not None:
                    val = val + land[plus, r, :].astype(F32)
                stage[slot, r, :] = val.astype(BF16)
                return carry

            lax.fori_loop(0, chunks, put, 0)

        @pl.when((i == 0) & (k == 0))
        def _():
            rs.start([glu_ref], rs_scratch)
            ag.start([small_ref], [gath_ref], ag_sems)

        @pl.when((i == N_DEV // 2) & (k == 0))
        def _():
            ag.forward([small_ref], [gath_ref], ag_sems)

        @pl.when(k == 0)
        def _():
            acc[...] = jnp.zeros_like(acc)

        acc[...] += _dot_tn(xn_ref[...], dz_ref[...])

        for n in range(4):
            @pl.when((k == nk - 1) & (i == 2 * n))
            def _(n=n):
                staged(n)
                pushes[n].start()

        for n in range(3):
            @pl.when((k == nk - 1) & (i == 2 * n + 1))
            def _(n=n):
                pushes[n].wait_recv()
                staged(4 + n, plus=n)
                pushes[4 + n].start()

        @pl.when((k == nk - 1) & (i == N_DEV - 1))
        def _():
            for slot in range(3, N_DEV - 1):
                pushes[slot].wait_recv()

            def add(s, carry):
                r = pl.ds(pl.multiple_of(s * chunk, chunk), chunk)
                total = acc[r, :]
                for slot in range(3, N_DEV - 1):
                    total = total + land[slot, r, :].astype(F32)
                gin_ref[r, :] = total
                return carry

            lax.fori_loop(0, chunks, add, 0)
            for cp in pushes:
                cp.wait_send()
            rs.finish([glu_ref], rs_scratch, [gglu_ref])
            ag.finish([small_ref], [gath_ref], ag_sems)

    any_spec = pl.BlockSpec(memory_space=pl.ANY)
    s = jax.ShapeDtypeStruct
    return pl.pallas_call(
        body, name="dw_in", grid=(N_DEV, nk),
        out_shape=[s((D_MODEL, COL_W), F32), s(glu_parts.shape[1:], F32), s((N_DEV,) + small.shape, F32)],
        in_specs=[pl.BlockSpec((tk, D_MODEL), lambda i, k: (k, 0)),
                  pl.BlockSpec((tk, COL_W), lambda i, k: (k, target(i))), any_spec, any_spec],
        out_specs=[_full((D_MODEL, COL_W)), _full(glu_parts.shape[1:]), any_spec],
        scratch_shapes=[pltpu.VMEM((D_MODEL, COL_W), F32), pltpu.VMEM((N_DEV - 1, D_MODEL, COL_W), BF16),
                        pltpu.VMEM((N_DEV - 1, D_MODEL, COL_W), BF16), pltpu.SemaphoreType.DMA((N_DEV - 1,)),
                        pltpu.SemaphoreType.DMA((N_DEV - 1,))] + rs.scratch(BF16) + ag.scratch(),
        compiler_params=_params(2, VMEM_LIMIT_V7X),
    )(xn, dz, glu_parts, small)


SMALL = ("mix_norm", "q_norm", "k_norm", "lambda_re", "lambda_im", "log_dt", "b_re", "b_im", "c_re", "c_im",
         "d_skip", "b_glu", "ple_norm")
BIG = ("w_in", "w_glu", "w_out", "w_ple_gate", "w_ple_proj")
WEIGHTS = ("mix_norm", "w_in", "q_norm", "k_norm", "lambda_re", "lambda_im", "log_dt", "b_re", "b_im", "c_re",
           "c_im", "d_skip", "w_glu", "b_glu", "w_out", "ple_norm", "w_ple_gate", "w_ple_proj")


def _pack(arrs):
    flat = jnp.concatenate([a.reshape(-1).astype(F32) for a in arrs])
    rows = -(-flat.shape[0] // (64 * LANES)) * 64
    return jnp.pad(flat, (0, rows * LANES - flat.shape[0])).reshape(rows, LANES)


def _unpack(packed, shapes):
    flat = packed.reshape(-1)
    out, off = [], 0
    for shp in shapes:
        size = math.prod(shp)
        out.append(flat[off:off + size].reshape(shp))
        off += size
    return out


def kernel(x, p, mix_norm, w_in, q_norm, k_norm, lambda_re, lambda_im, log_dt, b_re, b_im, c_re, c_im, d_skip, w_glu, b_glu, w_out, ple_norm, w_ple_gate, w_ple_proj, loss_target, m_mix_norm, m_w_in, m_q_norm, m_k_norm, m_lambda_re, m_lambda_im, m_log_dt, m_b_re, m_b_im, m_c_re, m_c_im, m_d_skip, m_w_glu, m_b_glu, m_w_out, m_ple_norm, m_w_ple_gate, m_w_ple_proj, v_mix_norm, v_w_in, v_q_norm, v_k_norm, v_lambda_re, v_lambda_im, v_log_dt, v_b_re, v_b_im, v_c_re, v_c_im, v_d_skip, v_w_glu, v_b_glu, v_w_out, v_ple_norm, v_w_ple_gate, v_w_ple_proj):
    env = dict(locals())
    w = {n: env[n] for n in WEIGHTS}
    m = {n: env["m_" + n] for n in WEIGHTS}
    v = {n: env["v_" + n] for n in WEIGHTS}
    nb, seq, _ = x.shape
    t_tok = nb * seq
    x2 = x.reshape(t_tok, D_MODEL)
    tg2 = loss_target.reshape(t_tok, D_MODEL)
    p2 = p.reshape(t_tok, PLE_DIM)

    shard2d = {"w_in": (D_MODEL, COL_W), "w_glu": (SSM_W // N_DEV, SSM_W), "w_out": (D_MODEL // N_DEV, D_MODEL),
               "w_ple_gate": (D_MODEL // N_DEV, D_MODEL), "w_ple_proj": (PLE_DIM, D_MODEL // N_DEV)}
    w_sh = [w[n].reshape(shard2d[n]) for n in BIG]

    g3 = (SSM_GROUPS, 1, SSM_STATE)
    lr3, li3 = lambda_re.reshape(g3), lambda_im.reshape(g3)
    dt3 = log_dt.reshape(SSM_GROUPS, 1, 1)
    btr = b_re[0].transpose(0, 2, 1)
    bti = b_im[0].transpose(0, 2, 1)
    a_re3, a_im3, bb_re, bb_im, cc_re, cc_im = _zoh_fwd(lr3, li3, dt3, btr, bti, c_re[0], c_im[0])
    a_re, a_im = a_re3.reshape(1, N_STATE), a_im3.reshape(1, N_STATE)

    ones_bd = _head_ones()
    fold = _head_fold()
    gq_t = jnp.tile(q_norm, (1, ATTN_W // HEAD_DIM))
    gk_t = jnp.tile(k_norm, (1, ATTN_W // HEAD_DIM))

    gq2 = jnp.tile(q_norm, (1, LANES // HEAD_DIM))
    gk2 = jnp.tile(k_norm, (1, LANES // HEAD_DIM))

    z, xn, w_in_g = _in_proj(x2, mix_norm, w_sh[0])
    (o, lse, ag), (w_glu_g, w_out_g, w_g_g, w_p_g) = _attn_fwd(z, gq2, gk2, nb, seq, w_sh[1:])
    w_glu_f = w_glu_g.reshape(SSM_W, SSM_W)
    w_out_f = w_out_g.reshape(D_MODEL, D_MODEL)
    w_g_f = w_g_g.reshape(D_MODEL, D_MODEL)
    x_re, x_im, y, sg = _ssm_fwd(z, a_re, a_im, bb_re, bb_im, cc_re, cc_im, d_skip, w_glu_f, b_glu, nb, seq)
    dmix, dh1, loss_t, d_ple, dw_out, dw_g, dw_p = _tail(x2, tg2, ag, sg, p2, w_out_f, w_g_f, w_p_g, ple_norm)

    early_parts = [dw_out.reshape(N_DEV, D_MODEL // N_DEV, D_MODEL), dw_g.reshape(N_DEV, D_MODEL // N_DEV, D_MODEL),
                   dw_p]
    (dqh, dkh, dvb, dga), (g_out, g_g, g_p) = _attn_bwd(z, gq2, gk2, o, lse, dmix, nb, seq, early_parts)
    (du, dgs, dw_glu, d_bglu, d_dskip, da_re, da_im, dbb_re, dbb_im, dcc_re, dcc_im) = _ssm_bwd(
        z, dmix, y, x_re, x_im, a_re, a_im, bb_re, bb_im, cc_re, cc_im, d_skip, w_glu_f, b_glu, nb, seq)
    dz, gx, d_mix, d_gq, d_gk = _dz_and_dx(x2, z, dqh, dkh, dvb, dga, du, dgs, dh1, w_in_g, mix_norm, gq_t, gk_t,
                                           ones_bd, fold)
    d_lr, d_li, d_dt, d_btr, d_bti, d_cr, d_ci = _zoh_bwd(
        lr3, li3, dt3, btr, bti, da_re.reshape(g3), da_im.reshape(g3), dbb_re, dbb_im, dcc_re, dcc_im, fold)
    small_g = {
        "mix_norm": d_mix, "q_norm": d_gq[0:1], "k_norm": d_gk[0:1], "lambda_re": d_lr, "lambda_im": d_li,
        "log_dt": d_dt, "b_re": d_btr, "b_im": d_bti, "c_re": d_cr, "c_im": d_ci,
        "d_skip": d_dskip, "b_glu": d_bglu, "ple_norm": d_ple}

    g_in, g_glu, gathered = _dw_in(xn, dz, dw_glu.reshape(N_DEV, SSM_W // N_DEV, SSM_W),
                                   _pack([small_g[n] for n in SMALL] + [loss_t[0:1, 0:1]]))
    g_sh = [g_in, g_glu, g_out, g_g, g_p]
    d_sh, m_sh, v_sh = _adamw_shards(g_sh, w_sh, [m[n].reshape(shard2d[n]) for n in BIG],
                                     [v[n].reshape(shard2d[n]) for n in BIG])

    g_pk = _small_sum(gathered)
    small_shapes = [w[n].shape for n in SMALL]
    swapped = ("b_re", "b_im")

    def to_own(n, a):
        a = a.reshape(a.shape[1:]) if a.ndim > 2 else a
        return a.transpose(0, 2, 1) if n in swapped else a

    def from_own(n, a):
        a = a.transpose(0, 2, 1) if n in swapped else a
        return a.reshape(w[n].shape)

    own = [to_own(n, w[n]).shape for n in SMALL]
    g_small = _unpack(g_pk, own)
    d_small, m_small, v_small = _adamw_small(
        g_small, *[[to_own(n, src[n]) for n in SMALL] for src in (w, m, v)])

    grads, deltas, new_m, new_v = {}, {}, {}, {}
    for dst, arrs in ((grads, g_small), (deltas, d_small), (new_m, m_small), (new_v, v_small)):
        for n, a in zip(SMALL, arrs):
            dst[n] = from_own(n, a)
    for i, n in enumerate(BIG):
        grads[n] = g_sh[i].reshape(w[n].shape)
        deltas[n] = d_sh[i].reshape(w[n].shape)
        new_m[n] = m_sh[i].reshape(w[n].shape)
        new_v[n] = v_sh[i].reshape(w[n].shape)

    loss = _unpack(g_pk, small_shapes + [()])[-1]
    return (loss, gx.reshape(x.shape), *[grads[n] for n in WEIGHTS], *[deltas[n] for n in WEIGHTS],
            *[new_m[n] for n in WEIGHTS], *[new_v[n] for n in WEIGHTS])
```

```python
import math

import numpy as np
import jax
import jax.numpy as jnp
from jax import lax
from jax.experimental import pallas as pl
from jax.experimental.pallas import tpu as pltpu

F32 = jnp.float32
BF16 = jnp.bfloat16
MESH = pl.DeviceIdType.MESH
AXES = ("x", "y", "c")
N_DEV = 8

D_MODEL = 1024
HEAD_DIM = 64
ATTN_W = 512
SSM_W = 512
SSM_GROUPS = 32
SSM_GROUP = 16
SSM_STATE = 64
N_STATE = SSM_GROUPS * SSM_STATE
PLE_DIM = 256
IN_W = 3072
COL_W = IN_W // N_DEV
DILATED = ((128, 1), (512, 4), (2048, 16))
EPS = 1e-6
INV_SQRT2 = 1.0 / math.sqrt(2.0)
INV_SQRT_2PI = 1.0 / math.sqrt(2.0 * math.pi)

ADAM_LR, ADAM_B1, ADAM_B2, ADAM_EPS, ADAM_WD, ADAM_STEP = 0.001, 0.9, 0.999, 1e-08, 0.01, 10

VMEM_LIMIT_V7X = 56 * 1024 * 1024
SUBLANES = 8
LANES = 128


def _params(n_axes=None, vmem=None):
    kw = {}
    if n_axes:
        kw["dimension_semantics"] = ("arbitrary",) * n_axes
    if vmem:
        kw["vmem_limit_bytes"] = vmem
    return pltpu.CompilerParams(**kw)


def _dot(a, b):
    return jnp.dot(a, b, preferred_element_type=F32)


def _dot_nt(a, b):
    return lax.dot_general(a, b, (((1,), (1,)), ((), ())), preferred_element_type=F32)


def _dot_tn(a, b):
    return lax.dot_general(a, b, (((0,), (0,)), ((), ())), preferred_element_type=F32)


def _hdot(a, ones):
    hi = a.astype(BF16)
    lo = (a - hi.astype(F32)).astype(BF16)
    return _dot(hi, ones) + _dot(lo, ones)


def _sig(x):
    return 1.0 / (1.0 + jnp.exp(-x))


def _gelu_and_grad(y):
    cdf = 0.5 * (1.0 + lax.erf(y * INV_SQRT2))
    pdf = jnp.exp(-0.5 * y * y) * INV_SQRT_2PI
    return y * cdf, cdf + y * pdf


def _vmem():
    return pl.BlockSpec(memory_space=pltpu.VMEM)


def _full(shape):
    nd = len(shape)
    return pl.BlockSpec(shape, lambda *_: (0,) * nd)


class _AllGather:
    def __init__(self, n, cast):
        self.n, self.cast = n, cast

    def scratch(self):
        n = self.n
        return [pltpu.SemaphoreType.DMA((7 * n,)), pltpu.SemaphoreType.DMA((7 * n,)), pltpu.SemaphoreType.DMA((n,))]

    def _plan(self, src_refs, out_refs, sems):
        send_sems, recv_sems, own_sems = sems
        x, y, c = lax.axis_index("x"), lax.axis_index("y"), lax.axis_index("c")
        me, sibling = (x, y, c), (x, y, 1 - c)
        chips = [(1 - x, y), (x, 1 - y), (1 - x, 1 - y)]

        def idx(px, py, pc):
            return 4 * px + 2 * py + pc

        def copy(i, k, block, to, own_src=False):
            ref = out_refs[i].at[idx(*block)]
            return pltpu.make_async_remote_copy(
                src_ref=src_refs[i] if own_src and not self.cast else ref, dst_ref=ref,
                send_sem=send_sems.at[7 * i + k], recv_sem=recv_sems.at[7 * i + k],
                device_id=to, device_id_type=MESH)

        first, passed, arrive_ici, arrive_d2d, own = [], [], [], [], []
        for i in range(self.n):
            first.append(copy(i, 0, me, sibling, own_src=True))
            first += [copy(i, 1 + j, me, (*chip, c), own_src=True) for j, chip in enumerate(chips)]
            arrive_ici += [copy(i, 1 + j, (*chip, c), me) for j, chip in enumerate(chips)]
            passed += [copy(i, 4 + j, (*chip, c), sibling) for j, chip in enumerate(chips)]
            arrive_d2d.append(copy(i, 0, sibling, me))
            arrive_d2d += [copy(i, 4 + j, (*chip, 1 - c), me) for j, chip in enumerate(chips)]
            if not self.cast:
                own.append(pltpu.make_async_copy(src_refs[i], out_refs[i].at[idx(*me)], own_sems.at[i]))
        return idx(*me), first, passed, arrive_ici, arrive_d2d, own

    def start(self, src_refs, out_refs, sems):
        my, first, _, _, _, own = self._plan(src_refs, out_refs, sems)
        if self.cast:
            for i in range(self.n):
                out_refs[i][my] = src_refs[i][...].astype(out_refs[i].dtype)
        for cp in own + first:
            cp.start()

    def forward(self, src_refs, out_refs, sems):
        _, _, passed, arrive_ici, _, _ = self._plan(src_refs, out_refs, sems)
        for cp in arrive_ici:
            cp.wait_recv()
        for cp in passed:
            cp.start()

    def finish(self, src_refs, out_refs, sems):
        _, first, passed, _, arrive_d2d, own = self._plan(src_refs, out_refs, sems)
        for cp in own:
            cp.wait()
        for cp in arrive_d2d:
            cp.wait_recv()
        for cp in first + passed:
            cp.wait_send()


class _HostedGather:
    def __init__(self, shards):
        self.shapes = [(N_DEV,) + a.shape for a in shards]
        self.n = len(shards)
        self.ag = _AllGather(self.n, cast=True)

    def out_shape(self):
        return [jax.ShapeDtypeStruct(s, BF16) for s in self.shapes]

    def scratch(self):
        return [pltpu.VMEM(s, BF16) for s in self.shapes] + self.ag.scratch() + [pltpu.SemaphoreType.DMA((self.n,))]

    def _split(self, scratch):
        return scratch[:self.n], scratch[self.n:-1], scratch[-1]

    def start(self, src_refs, scratch):
        land, sems, _ = self._split(scratch)
        self.ag.start(src_refs, land, sems)

    def forward(self, src_refs, scratch):
        land, sems, _ = self._split(scratch)
        self.ag.forward(src_refs, land, sems)

    def finish(self, src_refs, scratch, out_refs):
        land, sems, out_sems = self._split(scratch)
        self.ag.finish(src_refs, land, sems)
        outs = [pltpu.make_async_copy(land[n], out_refs[n], out_sems.at[n]) for n in range(self.n)]
        for cp in outs:
            cp.start()
        for cp in outs:
            cp.wait()


def _all_gather(shards, out_dtypes, name):
    n = len(shards)
    ag = _AllGather(n, cast=True)

    def body(*refs):
        in_refs, out_refs, sems = refs[:n], refs[n:2 * n], refs[2 * n:]
        ag.start(in_refs, out_refs, sems)
        ag.forward(in_refs, out_refs, sems)
        ag.finish(in_refs, out_refs, sems)

    return pl.pallas_call(
        body, name=name,
        out_shape=[jax.ShapeDtypeStruct((N_DEV,) + s.shape, dt) for s, dt in zip(shards, out_dtypes)],
        in_specs=[_vmem()] * n, out_specs=[_vmem()] * n,
        scratch_shapes=ag.scratch(),
        compiler_params=_params(vmem=VMEM_LIMIT_V7X),
    )(*shards)


def _row_chunks(rows):
    chunk = 64 if rows % 64 == 0 else rows
    return chunk, rows // chunk


class _ReduceScatter:
    def __init__(self, shapes):
        self.shapes = shapes
        self.n = len(shapes)

    def scratch(self, dtype):
        return ([pltpu.VMEM(s, dtype) for s in self.shapes]
                + [pltpu.SemaphoreType.DMA((7 * self.n,)), pltpu.SemaphoreType.DMA((7 * self.n,)),
                   pltpu.SemaphoreType.DMA((self.n,))])

    def _copies(self, in_refs, land_refs, send_sems, recv_sems, own_sems):
        x, y, c = lax.axis_index("x"), lax.axis_index("y"), lax.axis_index("c")
        remote, own = [], []
        for i in range(self.n):
            for m in range(1, N_DEV):
                px = 1 - x if m & 4 else x
                py = 1 - y if m & 2 else y
                pc = 1 - c if m & 1 else c
                remote.append(pltpu.make_async_remote_copy(
                    src_ref=in_refs[i].at[4 * px + 2 * py + pc], dst_ref=land_refs[i].at[m - 1],
                    send_sem=send_sems.at[7 * i + m - 1], recv_sem=recv_sems.at[7 * i + m - 1],
                    device_id=(px, py, pc), device_id_type=MESH))
            own.append(pltpu.make_async_copy(in_refs[i].at[4 * x + 2 * y + c], land_refs[i].at[N_DEV - 1],
                                             own_sems.at[i]))
        return remote, own

    def start(self, in_refs, scratch):
        remote, own = self._copies(in_refs, scratch[:self.n], *scratch[self.n:])
        for cp in remote + own:
            cp.start()

    def finish(self, in_refs, scratch, out_refs):
        land_refs = scratch[:self.n]
        remote, own = self._copies(in_refs, land_refs, *scratch[self.n:])
        for cp in own:
            cp.wait()
        for cp in remote:
            cp.wait_recv()
        for i in range(self.n):
            chunk, steps = _row_chunks(self.shapes[i][1])

            def step(s, carry, i=i, chunk=chunk):
                r = pl.ds(pl.multiple_of(s * chunk, chunk), chunk)
                acc = land_refs[i][N_DEV - 1, r, :].astype(F32)
                for m in range(1, N_DEV):
                    acc = acc + land_refs[i][m - 1, r, :].astype(F32)
                out_refs[i][r, :] = acc
                return carry

            lax.fori_loop(0, steps, step, 0)
        for cp in remote:
            cp.wait_send()


def _reduce_scatter(parts, name):
    n = len(parts)
    rs = _ReduceScatter([p.shape for p in parts])

    def body(*refs):
        in_refs, out_refs, scratch = refs[:n], refs[n:2 * n], refs[2 * n:]
        rs.start(in_refs, scratch)
        rs.finish(in_refs, scratch, out_refs)

    return pl.pallas_call(
        body, name=name,
        out_shape=[jax.ShapeDtypeStruct(p.shape[1:], F32) for p in parts],
        in_specs=[_vmem()] * n, out_specs=[_vmem()] * n,
        scratch_shapes=rs.scratch(parts[0].dtype),
        compiler_params=_params(vmem=VMEM_LIMIT_V7X),
    )(*parts)


def _adamw_math(w, g, m, v):
    m = ADAM_B1 * m + (1.0 - ADAM_B1) * g
    v = ADAM_B2 * v + (1.0 - ADAM_B2) * (g * g)
    m_hat = m / (1.0 - ADAM_B1 ** ADAM_STEP)
    v_hat = v / (1.0 - ADAM_B2 ** ADAM_STEP)
    delta = -ADAM_LR * (m_hat / (jnp.sqrt(v_hat) + ADAM_EPS) + ADAM_WD * w)
    return delta, m, v


def _adamw_shards(gs, ws, ms, vs):
    n = len(gs)

    def body(*refs):
        g_refs, w_refs, m_refs, v_refs = (refs[k * n:(k + 1) * n] for k in range(4))
        d_out, m_out, v_out = (refs[(4 + k) * n:(5 + k) * n] for k in range(3))
        for i in range(n):
            chunk, steps = _row_chunks(gs[i].shape[0])

            def step(s, carry, i=i, chunk=chunk):
                r = pl.ds(pl.multiple_of(s * chunk, chunk), chunk)
                d, m, v = _adamw_math(w_refs[i][r, :], g_refs[i][r, :], m_refs[i][r, :], v_refs[i][r, :])
                d_out[i][r, :] = d
                m_out[i][r, :] = m
                v_out[i][r, :] = v
                return carry

            lax.fori_loop(0, steps, step, 0)

    shapes = [jax.ShapeDtypeStruct(g.shape, F32) for g in gs]
    outs = pl.pallas_call(
        body, name="adamw_shards", out_shape=shapes * 3,
        in_specs=[_vmem()] * (4 * n), out_specs=[_vmem()] * (3 * n),
        compiler_params=_params(vmem=VMEM_LIMIT_V7X),
    )(*gs, *ws, *ms, *vs)
    return outs[:n], outs[n:2 * n], outs[2 * n:]


def _small_sum(gathered):
    rows = gathered.shape[1]
    chunk, steps = _row_chunks(rows)

    def body(ga_ref, g_out):
        def step(s, carry):
            r = pl.ds(pl.multiple_of(s * chunk, chunk), chunk)
            g = ga_ref[0, r, :]
            for j in range(1, N_DEV):
                g = g + ga_ref[j, r, :]
            g_out[r, :] = g
            return carry

        lax.fori_loop(0, steps, step, 0)

    return pl.pallas_call(
        body, name="small_sum", out_shape=jax.ShapeDtypeStruct(gathered.shape[1:], F32),
        in_specs=[_vmem()], out_specs=_vmem(),
    )(gathered)


def _adamw_small(gs, ws, ms, vs):
    n = len(gs)

    def body(*refs):
        g_refs, w_refs, m_refs, v_refs = (refs[k * n:(k + 1) * n] for k in range(4))
        d_out, m_out, v_out = (refs[(4 + k) * n:(5 + k) * n] for k in range(3))
        for i in range(n):
            def update(idx, i=i):
                d, mm, vv = _adamw_math(w_refs[i][idx], g_refs[i][idx], m_refs[i][idx], v_refs[i][idx])
                d_out[i][idx] = d
                m_out[i][idx] = mm
                v_out[i][idx] = vv

            if len(gs[i].shape) == 3:
                def step(s, carry, update=update):
                    update(s)
                    return carry

                lax.fori_loop(0, gs[i].shape[0], step, 0)
            else:
                update(Ellipsis)

    shapes = [jax.ShapeDtypeStruct(g.shape, F32) for g in gs]
    outs = pl.pallas_call(
        body, name="adamw_small", out_shape=shapes * 3,
        in_specs=[_vmem()] * (4 * n), out_specs=[_vmem()] * (3 * n),
        compiler_params=_params(vmem=VMEM_LIMIT_V7X),
    )(*gs, *ws, *ms, *vs)
    return outs[:n], outs[n:2 * n], outs[2 * n:]


def _zoh(lr, li, logdt, btr, bti):
    dt = jnp.exp(logdt)
    mag = jnp.exp(lr * dt)
    th = li * dt
    ar = mag * jnp.cos(th)
    ai = mag * jnp.sin(th)
    den = lr * lr + li * li
    nr = ar - 1.0
    cr = (nr * lr + ai * li) / den
    ci = (ai * lr - nr * li) / den
    return ar, ai, cr * btr - ci * bti, cr * bti + ci * btr


BD_GROUPS = 8
BD_ROWS = BD_GROUPS * SSM_GROUP
BD_COLS = BD_GROUPS * SSM_STATE
N_BD = SSM_GROUPS // BD_GROUPS


def _bd_mask():
    r = lax.broadcasted_iota(jnp.int32, (BD_ROWS, BD_COLS), 0) // SSM_GROUP
    c = lax.broadcasted_iota(jnp.int32, (BD_ROWS, BD_COLS), 1) // SSM_STATE
    return r == c


def _blockdiag_store(out_ref, t):
    mask = _bd_mask()
    for j in range(N_BD):
        rows = t[j * BD_GROUPS:(j + 1) * BD_GROUPS].reshape(BD_ROWS, SSM_STATE)
        out_ref[j] = jnp.where(mask, jnp.tile(rows, (1, BD_GROUPS)), 0.0).astype(out_ref.dtype)


def _blockdiag_load(m_ref, fold):
    mask = _bd_mask()
    parts = [_hdot(jnp.where(mask, m_ref[j], 0.0), fold).reshape(BD_GROUPS, SSM_GROUP, SSM_STATE)
             for j in range(N_BD)]
    return jnp.concatenate(parts, axis=0)


def _zoh_fwd(lr, li, logdt, btr, bti, c_re, c_im):
    def body(lr_ref, li_ref, dt_ref, br_ref, bi_ref, cr_ref, ci_ref, ar_ref, ai_ref, bbr_ref, bbi_ref, ccr_ref,
             cci_ref):
        ar, ai, bbr, bbi = _zoh(lr_ref[...], li_ref[...], dt_ref[...], br_ref[...], bi_ref[...])
        ar_ref[...] = ar
        ai_ref[...] = ai
        _blockdiag_store(bbr_ref, bbr)
        _blockdiag_store(bbi_ref, bbi)
        _blockdiag_store(ccr_ref, cr_ref[...])
        _blockdiag_store(cci_ref, ci_ref[...])

    s = jax.ShapeDtypeStruct
    bd = s((N_BD, BD_ROWS, BD_COLS), BF16)
    return pl.pallas_call(
        body, name="zoh_fwd", out_shape=[s(lr.shape, F32), s(lr.shape, F32), bd, bd, bd, bd],
        in_specs=[_vmem()] * 7, out_specs=[_vmem()] * 6,
    )(lr, li, logdt, btr, bti, c_re, c_im)


def _zoh_bwd(lr, li, logdt, btr, bti, dar, dai, dbb_re, dbb_im, dcc_re, dcc_im, fold):
    def body(lr_ref, li_ref, dt_ref, br_ref, bi_ref, dar_ref, dai_ref, dbbr_ref, dbbi_ref, dccr_ref, dcci_ref,
             fold_ref, glr_ref, gli_ref, gdt_ref, gbr_ref, gbi_ref, gcr_ref, gci_ref):
        fold_m = fold_ref[...]
        _, vjp = jax.vjp(_zoh, lr_ref[...], li_ref[...], dt_ref[...], br_ref[...], bi_ref[...])
        glr, gli, gdt, gbr, gbi = vjp((dar_ref[...], dai_ref[...], _blockdiag_load(dbbr_ref, fold_m),
                                       _blockdiag_load(dbbi_ref, fold_m)))
        glr_ref[...] = glr
        gli_ref[...] = gli
        gdt_ref[...] = gdt
        gbr_ref[...] = gbr
        gbi_ref[...] = gbi
        gcr_ref[...] = _blockdiag_load(dccr_ref, fold_m)
        gci_ref[...] = _blockdiag_load(dcci_ref, fold_m)

    s = jax.ShapeDtypeStruct
    return pl.pallas_call(
        body, name="zoh_bwd",
        out_shape=[s(lr.shape, F32), s(lr.shape, F32), s(logdt.shape, F32)] + [s(btr.shape, F32)] * 4,
        in_specs=[_vmem()] * 12, out_specs=[_vmem()] * 7,
    )(lr, li, logdt, btr, bti, dar, dai, dbb_re, dbb_im, dcc_re, dcc_im, fold)


def _head_ones():
    r = np.arange(ATTN_W) // HEAD_DIM
    return jnp.asarray(r[:, None] == r[None, :], dtype=BF16)


def _head_fold():
    return jnp.asarray(np.tile(np.eye(HEAD_DIM), (ATTN_W // HEAD_DIM, 1)), dtype=BF16)


def _in_proj(x2, g_mix, w_in_sh):
    t_tok = x2.shape[0]
    tm = min(1024, t_tok)
    nt = t_tok // tm
    ag_w = _AllGather(1, cast=True)
    n_sem = len(ag_w.scratch())

    def owner(i):
        x, y, c = lax.axis_index("x"), lax.axis_index("y"), lax.axis_index("c")
        rel = i // 2
        px = jnp.where((rel == 1) | (rel == 3), 1 - x, x)
        py = jnp.where((rel == 2) | (rel == 3), 1 - y, y)
        pc = jnp.where(i % 2 == 1, 1 - c, c)
        return 4 * px + 2 * py + pc

    def body(*refs):
        x_ref, g_ref, w_ref, z_ref, xn_ref, wg_ref, xn_scr, w_land = refs[:8]
        sems_w, out_sem = refs[8:8 + n_sem], refs[8 + n_sem]
        i, t = pl.program_id(0), pl.program_id(1)
        _, first, passed, arrive_ici, arrive_d2d, _ = ag_w._plan([w_ref], [w_land], sems_w)

        @pl.when((i == 0) & (t == 0))
        def _():
            ag_w.start([w_ref], [w_land], sems_w)

        @pl.when((i == 1) & (t == 0))
        def _():
            arrive_d2d[0].wait_recv()

        for n in range(3):
            @pl.when((i == 2 + 2 * n) & (t == 0))
            def _(n=n):
                arrive_ici[n].wait_recv()
                passed[n].start()

            @pl.when((i == 3 + 2 * n) & (t == 0))
            def _(n=n):
                arrive_d2d[1 + n].wait_recv()

        @pl.when(i == 0)
        def _():
            x = x_ref[...]
            r = lax.rsqrt(jnp.mean(x * x, axis=-1, keepdims=True) + EPS)
            xn = (x * r * g_ref[...]).astype(BF16)
            xn_ref[...] = xn
            xn_scr[t] = xn

        z_ref[...] = _dot(xn_scr[t], w_land[owner(i)])

        @pl.when((i == N_DEV - 1) & (t == nt - 1))
        def _():
            for cp in first + passed:
                cp.wait_send()
            out = pltpu.make_async_copy(w_land, wg_ref, out_sem)
            out.start()
            out.wait()

    s = jax.ShapeDtypeStruct
    xmap = lambda i, t: (jnp.where(i == 0, t, nt - 1), 0)
    gathered = s((N_DEV,) + w_in_sh.shape, BF16)
    return pl.pallas_call(
        body, name="in_proj", grid=(N_DEV, nt),
        out_shape=[s((t_tok, IN_W), F32), s((t_tok, D_MODEL), BF16), gathered],
        in_specs=[pl.BlockSpec((tm, D_MODEL), xmap), _full(g_mix.shape), _full(w_in_sh.shape)],
        out_specs=[pl.BlockSpec((tm, COL_W), lambda i, t: (t, owner(i))), pl.BlockSpec((tm, D_MODEL), xmap),
                   pl.BlockSpec(memory_space=pl.ANY)],
        scratch_shapes=[pltpu.VMEM((nt, tm, D_MODEL), BF16), pltpu.VMEM(gathered.shape, BF16)] + ag_w.scratch()
        + [pltpu.SemaphoreType.DMA],
        compiler_params=_params(2, VMEM_LIMIT_V7X),
    )(x2, g_mix, w_in_sh)


TQ = 128
NEG = -1e30


def _head_col(t, lm):
    return jnp.max(jnp.where(lm, t, NEG), axis=-1, keepdims=True)


def _head_masks():
    lane = lax.broadcasted_iota(jnp.int32, (1, 1, LANES), 2)
    return [(lane // HEAD_DIM) == h for h in range(LANES // HEAD_DIM)]


def _stack_heads(t3, lms):
    return jnp.concatenate([jnp.where(lm, t3, jnp.zeros_like(t3)) for lm in lms], axis=1)


def _unstack_heads(t2, lms, tq):
    out = t2[:, :tq]
    for h in range(1, len(lms)):
        out = jnp.where(lms[h], t2[:, h * tq:(h + 1) * tq], out)
    return out


def _gather_classes(ref, dil, nt, tq, dtype):
    length = nt * tq
    if dil == 1:
        return ref[...].astype(dtype).reshape(nt, tq, LANES)
    parts = [ref[pl.ds(r, length, stride=dil), :].astype(dtype).reshape(nt, tq, LANES) for r in range(dil)]
    return jnp.concatenate(parts, axis=0)


def _scatter_classes(ref, val, dil, nt, tq, add):
    length = nt * tq
    for r in range(dil):
        rows = pl.ds(r, length, stride=dil) if dil > 1 else slice(None)
        part = val[r * nt:(r + 1) * nt].reshape(length, LANES)
        ref[rows, :] = ref[rows, :] + part if add else part


def _with_prev_tile(t3, dil, nt):
    parts = []
    for r in range(dil):
        t = t3[r * nt:(r + 1) * nt]
        parts.append(jnp.concatenate([t[:1], t[:-1]], axis=0))
    prev = parts[0] if dil == 1 else jnp.concatenate(parts, axis=0)
    return jnp.concatenate([prev, t3], axis=1)


def _band_valid(dil, nt, tq):
    if nt == 1:
        shape = (dil, tq, tq)
        return lax.broadcasted_iota(jnp.int32, shape, 1) >= lax.broadcasted_iota(jnp.int32, shape, 2)
    shape = (dil * nt, tq, 2 * tq)
    b = lax.broadcasted_iota(jnp.int32, shape, 0)
    c = lax.broadcasted_iota(jnp.int32, shape, 2)
    d = tq + lax.broadcasted_iota(jnp.int32, shape, 1) - c
    return (d >= 0) & (d <= tq) & (((b & (nt - 1)) != 0) | (c >= tq))


def _window_tiling(seq, window, dil):
    length = seq // dil
    tq = min(TQ, length)
    nt = length // tq
    assert length % tq == 0 and nt & (nt - 1) == 0 and (nt == 1 or window == tq * dil)
    return nt, tq


def _bqk(a, b):
    return jnp.einsum("bqd,bkd->bqk", a, b, preferred_element_type=F32)


def _bqd(a, b):
    return jnp.einsum("bqk,bkd->bqd", a, b, preferred_element_type=F32)


def _bkd(a, b):
    return jnp.einsum("bqk,bqd->bkd", a, b, preferred_element_type=F32)


def _qk_hat(q_ref, k_ref, gq_ref, gk_ref):
    lane = lax.broadcasted_iota(jnp.int32, (1, LANES), 1)

    def norm(raw, gain, scale):
        sq = raw * raw
        r = jnp.zeros_like(raw)
        for h in range(LANES // HEAD_DIM):
            lm = (lane // HEAD_DIM) == h
            ms = jnp.sum(jnp.where(lm, sq, 0.0), axis=-1, keepdims=True) * (1.0 / HEAD_DIM)
            r = jnp.where(lm, lax.rsqrt(ms + EPS), r)
        return raw * r * gain * scale

    return norm(q_ref[...], gq_ref[...], HEAD_DIM ** -0.5), norm(k_ref[...], gk_ref[...], 1.0)


def _zblock(seq, group):
    return pl.BlockSpec((seq, LANES), lambda b, hp: (b, group * (ATTN_W // LANES) + hp))


def _attn_fwd(z, gq2, gk2, nb, seq, late_sh):
    t_tok = nb * seq
    n_win = len(DILATED)
    host = _HostedGather(late_sh)
    n_late = host.n
    n_steps = (nb, ATTN_W // LANES)

    def body(*refs):
        (q_ref, k_ref, v_ref, ga_ref, gq_ref, gk_ref), refs = refs[:6], refs[6:]
        late_refs, refs = refs[:n_late], refs[n_late:]
        (o_ref, l_ref, ag_ref), refs = refs[:3], refs[3:]
        lateg_refs, refs = refs[:n_late], refs[n_late:]
        (qf, kf, oc, lc), host_scratch = refs[:4], refs[4:]
        step = pl.program_id(0) * n_steps[1] + pl.program_id(1)
        total = n_steps[0] * n_steps[1]

        @pl.when(step == 0)
        def _():
            host.start(late_refs, host_scratch)

        @pl.when(step == total // 2)
        def _():
            host.forward(late_refs, host_scratch)

        qf[...], kf[...] = _qk_hat(q_ref, k_ref, gq_ref, gk_ref)
        lms = _head_masks()
        for w, (window, dil) in enumerate(DILATED):
            nt, tq = _window_tiling(seq, window, dil)
            q3 = _gather_classes(qf, dil, nt, tq, BF16)
            k3 = _gather_classes(kf, dil, nt, tq, BF16)
            v3 = _gather_classes(v_ref, dil, nt, tq, BF16)
            if nt > 1:
                k3, v3 = _with_prev_tile(k3, dil, nt), _with_prev_tile(v3, dil, nt)
            valid = _band_valid(dil, nt, tq)
            valid = jnp.concatenate([valid] * len(lms), axis=1)
            s = _bqk(_stack_heads(q3, lms), k3)
            m = jnp.max(jnp.where(valid, s, NEG), axis=-1, keepdims=True)
            p = jnp.where(valid, jnp.exp(s - m), 0.0)
            den = jnp.sum(p, axis=-1, keepdims=True)
            o = _unstack_heads(_bqd(p.astype(BF16), v3) / den, lms, tq)
            lse = _unstack_heads(jnp.broadcast_to(m + jnp.log(den), s.shape[:2] + (LANES,)), lms, tq)
            _scatter_classes(oc.at[w], o, dil, nt, tq, add=False)
            _scatter_classes(lc.at[w], lse, dil, nt, tq, add=False)
        mx = lc[0]
        for w in range(1, n_win):
            mx = jnp.maximum(mx, lc[w])
        tot = jnp.zeros_like(mx)
        o = jnp.zeros_like(mx)
        for w in range(n_win):
            e = jnp.exp(lc[w] - mx)
            tot = tot + e
            o = o + e * oc[w]
        o = o / tot
        o_ref[...] = o
        l_ref[...] = mx + jnp.log(tot)
        ga = ga_ref[...]
        ag_ref[...] = (o * ga * _sig(ga)).astype(BF16)

        @pl.when(step == total - 1)
        def _():
            host.finish(late_refs, host_scratch, lateg_refs)

    blk = pl.BlockSpec((seq, LANES), lambda b, hp: (b, hp))
    s = jax.ShapeDtypeStruct
    outs = pl.pallas_call(
        body, name="attn_fwd", grid=n_steps,
        out_shape=[s((t_tok, ATTN_W), F32), s((t_tok, ATTN_W), F32), s((t_tok, ATTN_W), BF16)] + host.out_shape(),
        in_specs=[_zblock(seq, 0), _zblock(seq, 1), _zblock(seq, 2), _zblock(seq, 3), _full(gq2.shape),
                  _full(gk2.shape)] + [_full(a.shape) for a in late_sh],
        out_specs=[blk, blk, blk] + [pl.BlockSpec(memory_space=pl.ANY)] * n_late,
        scratch_shapes=[pltpu.VMEM((seq, LANES), F32)] * 2 + [pltpu.VMEM((n_win, seq, LANES), F32)] * 2
        + host.scratch(),
        compiler_params=_params(2, VMEM_LIMIT_V7X),
    )(z, z, z, z, gq2, gk2, *late_sh)
    return outs[:3], outs[3:]


SCAN_COLS = 512


def _to_segments(dst_ref, val):
    seg = val.shape[0] // SUBLANES
    for n in range(dst_ref.shape[0]):
        for s in range(SUBLANES):
            dst_ref[n, pl.ds(s, seg, stride=SUBLANES), :] = val[s * seg:(s + 1) * seg, n * LANES:(n + 1) * LANES]


def _from_segments(src_ref):
    seg = src_ref.shape[1] // SUBLANES
    return jnp.concatenate(
        [jnp.concatenate([src_ref[n, pl.ds(s, seg, stride=SUBLANES), :] for s in range(SUBLANES)], axis=0)
         for n in range(src_ref.shape[0])], axis=1)


def _scan_chunk(re_ref, im_ref, a_re_ref, a_im_ref, carry_re, carry_im, rows, reverse, visit=None):
    seg = rows // SUBLANES
    assert seg & (seg - 1) == 0
    rowi = lax.broadcasted_iota(jnp.int32, (SUBLANES, SCAN_COLS), 0)
    edge = (SUBLANES - 1) if reverse else 0
    last = 0 if reverse else SUBLANES - 1
    at_edge = rowi == edge

    def cmul(ar, ai, br, bi):
        return ar * br - ai * bi, ar * bi + ai * br

    for c0 in range(0, N_STATE, SCAN_COLS):
        cols = slice(c0, c0 + SCAN_COLS)
        a1r = jnp.broadcast_to(a_re_ref[:, cols], (SUBLANES, SCAN_COLS))
        a1i = jnp.broadcast_to(a_im_ref[:, cols], (SUBLANES, SCAN_COLS))
        if reverse:
            a1i = -a1i

        def block_of(i):
            j = (seg - 1 - i) if reverse else i
            return j, pl.ds(pl.multiple_of(j * SUBLANES, SUBLANES), SUBLANES)

        def local(i, carry, cols=cols, a1r=a1r, a1i=a1i):
            xr, xi = carry
            _, blk = block_of(i)
            nr, ni = cmul(a1r, a1i, xr, xi)
            xr, xi = nr + re_ref[blk, cols], ni + im_ref[blk, cols]
            re_ref[blk, cols] = xr
            im_ref[blk, cols] = xi
            return xr, xi

        zero = jnp.zeros((SUBLANES, SCAN_COLS), F32)
        er, ei = lax.fori_loop(0, seg, local, (zero, zero))

        pr, pi = a1r, a1i
        for _ in range(seg.bit_length() - 1):
            pr, pi = cmul(pr, pi, pr, pi)
        cr, ci = carry_re[:, cols], carry_im[:, cols]
        inr, ini = cmul(pr, pi, cr, ci)
        er = er + jnp.where(at_edge, inr, 0.0)
        ei = ei + jnp.where(at_edge, ini, 0.0)
        for sft in (1, 2, 4):
            shift, keep = (SUBLANES - sft, rowi < SUBLANES - sft) if reverse else (sft, rowi >= sft)
            rs = jnp.where(keep, pltpu.roll(er, shift, 0), 0.0)
            ims = jnp.where(keep, pltpu.roll(ei, shift, 0), 0.0)
            dr, di = cmul(pr, pi, rs, ims)
            er, ei = er + dr, ei + di
            pr, pi = cmul(pr, pi, pr, pi)
        carry_re[:, cols] = jnp.broadcast_to(er[last:last + 1, :], (SUBLANES, SCAN_COLS))
        carry_im[:, cols] = jnp.broadcast_to(ei[last:last + 1, :], (SUBLANES, SCAN_COLS))
        one = (SUBLANES - 1) if reverse else 1
        kr = jnp.where(at_edge, cr, pltpu.roll(er, one, 0))
        ki = jnp.where(at_edge, ci, pltpu.roll(ei, one, 0))

        def fix(i, carry, cols=cols, a1r=a1r, a1i=a1i):
            kr, ki, acc = carry
            j, blk = block_of(i)
            kr, ki = cmul(a1r, a1i, kr, ki)
            xr, xi = re_ref[blk, cols] + kr, im_ref[blk, cols] + ki
            re_ref[blk, cols] = xr
            im_ref[blk, cols] = xi
            if visit is not None:
                acc = visit(cols, j, xr, xi, acc)
            return kr, ki, acc

        _, _, acc = lax.fori_loop(0, seg, fix, (kr, ki, (zero, zero)))
        if visit is not None:
            visit(cols, None, None, None, acc)


SSM_CHUNK = 512


def _ssm_fwd(z, a_re, a_im, bb_re, bb_im, cc_re, cc_im, d_skip, w_glu, b_glu, nb, seq):
    t_tok = nb * seq
    tc = min(SSM_CHUNK, seq)
    nch = seq // tc
    grp = N_STATE // 4

    def body(u_ref, gs_ref, ar_ref, ai_ref, bbr_ref, bbi_ref, ccr_ref, cci_ref, d_ref, wg_ref, bg_ref,
             xr_ref, xi_ref, y_ref, sg_ref, car_re, car_im, seg_u, seg_y):
        @pl.when(pl.program_id(1) == 0)
        def _():
            car_re[...] = jnp.zeros_like(car_re)
            car_im[...] = jnp.zeros_like(car_im)

        u = u_ref[...]
        _to_segments(seg_u, u)
        for j in range(4):
            uj = seg_u[j].astype(BF16)
            xr_ref[:, j * grp:(j + 1) * grp] = _dot(uj, bbr_ref[j])
            xi_ref[:, j * grp:(j + 1) * grp] = _dot(uj, bbi_ref[j])
        _scan_chunk(xr_ref, xi_ref, ar_ref, ai_ref, car_re, car_im, tc, reverse=False)
        for j in range(4):
            xr = xr_ref[:, j * grp:(j + 1) * grp].astype(BF16)
            xi = xi_ref[:, j * grp:(j + 1) * grp].astype(BF16)
            seg_y[j] = _dot_nt(xr, ccr_ref[j]) - _dot_nt(xi, cci_ref[j])
        y = _from_segments(seg_y) + d_ref[...] * u
        y_ref[...] = y
        yg, _ = _gelu_and_grad(y)
        gl = _dot(yg.astype(BF16), wg_ref[...]) + bg_ref[...]
        gs = gs_ref[...]
        sg_ref[...] = (yg * _sig(gl) * gs * _sig(gs)).astype(BF16)

    umap = lambda b, ch: (b * nch + ch, 4)
    gmap = lambda b, ch: (b * nch + ch, 5)
    row = lambda b, ch: (b * nch + ch, 0)
    s = jax.ShapeDtypeStruct
    consts = [a_re, a_im, bb_re, bb_im, cc_re, cc_im, d_skip, w_glu, b_glu]
    return pl.pallas_call(
        body, name="ssm_fwd", grid=(nb, nch),
        out_shape=[s((t_tok, N_STATE), F32), s((t_tok, N_STATE), F32), s((t_tok, SSM_W), F32),
                   s((t_tok, SSM_W), BF16)],
        in_specs=[pl.BlockSpec((tc, SSM_W), umap), pl.BlockSpec((tc, SSM_W), gmap)] + [_full(c.shape) for c in consts],
        out_specs=[pl.BlockSpec((tc, N_STATE), row), pl.BlockSpec((tc, N_STATE), row),
                   pl.BlockSpec((tc, SSM_W), row), pl.BlockSpec((tc, SSM_W), row)],
        scratch_shapes=[pltpu.VMEM((SUBLANES, N_STATE), F32), pltpu.VMEM((SUBLANES, N_STATE), F32),
                        pltpu.VMEM((4, tc, LANES), F32), pltpu.VMEM((4, tc, LANES), F32)],
        compiler_params=_params(2, VMEM_LIMIT_V7X),
    )(z, z, *consts)


def _tail(x2, tg2, ag, sg, p2, w_out, w_g, w_p, g_ple):
    t_tok = x2.shape[0]
    tm = min(512, t_tok)
    nt = t_tok // tm
    half = ATTN_W

    def body(x_ref, tg_ref, ag_ref, sg_ref, p_ref, wo_ref, wg_ref, wp_ref, gp_ref,
             dmix_ref, dh1_ref, loss_ref, dgp_ref, dwo_ref, dwg_ref, dwp_ref, acc_o, acc_g, acc_p):
        i = pl.program_id(0)

        @pl.when(i == 0)
        def _():
            loss_ref[...] = jnp.zeros_like(loss_ref)
            dgp_ref[...] = jnp.zeros_like(dgp_ref)
            acc_o[...] = jnp.zeros_like(acc_o)
            acc_g[...] = jnp.zeros_like(acc_g)
            acc_p[...] = jnp.zeros_like(acc_p)

        ag_t, sg_t = ag_ref[...], sg_ref[...]
        h1 = x_ref[...] + _dot(ag_t, wo_ref[0:half, :]) + _dot(sg_t, wo_ref[half:2 * half, :])
        r2 = lax.rsqrt(jnp.mean(h1 * h1, axis=-1, keepdims=True) + EPS)
        hnorm = h1 * r2
        gp = gp_ref[...]
        hn = (hnorm * gp).astype(BF16)
        gate = _sig(_dot(hn, wg_ref[...]))
        pb = p_ref[...].astype(BF16)
        pp = jnp.concatenate([_dot(pb, wp_ref[j]) for j in range(N_DEV)], axis=-1)
        h2 = h1 + gate * pp
        err = h2 - tg_ref[...]
        loss_ref[...] += 0.5 * jnp.sum(err * err) * (1.0 / D_MODEL)
        dh2 = err * (1.0 / D_MODEL)
        dpp = (dh2 * gate).astype(BF16)
        dgpre = (dh2 * pp * gate * (1.0 - gate)).astype(BF16)
        acc_p[...] += _dot_tn(pb, dpp)
        acc_g[...] += _dot_tn(hn, dgpre)
        dhn = _dot_nt(dgpre, wg_ref[...])
        dgp_ref[...] += jnp.sum(dhn * hnorm, axis=0, keepdims=True)
        a = dhn * gp
        dh1 = dh2 + r2 * (a - hnorm * jnp.mean(a * hnorm, axis=-1, keepdims=True))
        dh1_ref[...] = dh1
        dh1b = dh1.astype(BF16)
        acc_o[0:half, :] += _dot_tn(ag_t, dh1b)
        acc_o[half:2 * half, :] += _dot_tn(sg_t, dh1b)
        dmix_ref[...] = _dot_nt(dh1b, wo_ref[...])

        @pl.when(i == nt - 1)
        def _():
            dwo_ref[...] = acc_o[...].astype(BF16)
            dwg_ref[...] = acc_g[...].astype(BF16)
            for j in range(N_DEV):
                dwp_ref[j] = acc_p[:, j * LANES:(j + 1) * LANES].astype(BF16)

    row = lambda i: (i, 0)
    s = jax.ShapeDtypeStruct
    return pl.pallas_call(
        body, name="tail_fwd_bwd", grid=(nt,),
        out_shape=[s((t_tok, D_MODEL), F32), s((t_tok, D_MODEL), F32), s((SUBLANES, LANES), F32),
                   s((1, D_MODEL), F32), s((D_MODEL, D_MODEL), BF16), s((D_MODEL, D_MODEL), BF16),
                   s((N_DEV, PLE_DIM, LANES), BF16)],
        in_specs=[pl.BlockSpec((tm, D_MODEL), row), pl.BlockSpec((tm, D_MODEL), row),
                  pl.BlockSpec((tm, half), row), pl.BlockSpec((tm, half), row), pl.BlockSpec((tm, PLE_DIM), row),
                  _full(w_out.shape), _full(w_g.shape), _full(w_p.shape), _full(g_ple.shape)],
        out_specs=[pl.BlockSpec((tm, D_MODEL), row), pl.BlockSpec((tm, D_MODEL), row), _full((SUBLANES, LANES)),
                   _full((1, D_MODEL)), _full((D_MODEL, D_MODEL)), _full((D_MODEL, D_MODEL)),
                   _full((N_DEV, PLE_DIM, LANES))],
        scratch_shapes=[pltpu.VMEM((D_MODEL, D_MODEL), F32), pltpu.VMEM((D_MODEL, D_MODEL), F32),
                        pltpu.VMEM((PLE_DIM, D_MODEL), F32)],
        compiler_params=_params(1, VMEM_LIMIT_V7X),
    )(x2, tg2, ag, sg, p2, w_out, w_g, w_p, g_ple)


def _attn_bwd(z, gq2, gk2, o, lse, dmix, nb, seq, parts):
    t_tok = nb * seq
    n_rs = len(parts)
    rs = _ReduceScatter([p.shape for p in parts])
    n_steps = (nb, ATTN_W // LANES)

    def body(*refs):
        (q_ref, k_ref, v_ref, ga_ref, gq_ref, gk_ref, o_ref, l_ref, da_ref), refs = refs[:9], refs[9:]
        part_refs, refs = refs[:n_rs], refs[n_rs:]
        (dq_ref, dk_ref, dv_ref, dga_ref), refs = refs[:4], refs[4:]
        g_refs, refs = refs[:n_rs], refs[n_rs:]
        (qf, kf, dof, dlf), rs_scratch = refs[:4], refs[4:]
        b, hp = pl.program_id(0), pl.program_id(1)

        @pl.when((b == 0) & (hp == 0))
        def _():
            rs.start(part_refs, rs_scratch)

        ga, o_t, da = ga_ref[...], o_ref[...], da_ref[...]
        sga = _sig(ga)
        d_o = da * ga * sga
        dga_ref[...] = da * o_t * sga * (1.0 + ga * (1.0 - sga))
        lane = lax.broadcasted_iota(jnp.int32, (1, LANES), 1)
        d_oo = d_o * o_t
        delta = jnp.zeros_like(d_oo)
        for h in range(LANES // HEAD_DIM):
            lm2 = (lane // HEAD_DIM) == h
            delta = jnp.where(lm2, jnp.sum(jnp.where(lm2, d_oo, 0.0), axis=-1, keepdims=True), delta)
        qf[...], kf[...] = _qk_hat(q_ref, k_ref, gq_ref, gk_ref)
        dof[...] = d_o
        dlf[...] = delta
        dq_ref[...] = jnp.zeros_like(dq_ref)
        dk_ref[...] = jnp.zeros_like(dk_ref)
        dv_ref[...] = jnp.zeros_like(dv_ref)
        lms = _head_masks()
        for window, dil in DILATED:
            nt, tq = _window_tiling(seq, window, dil)
            q3 = _gather_classes(qf, dil, nt, tq, BF16)
            k3 = _gather_classes(kf, dil, nt, tq, BF16)
            v3 = _gather_classes(v_ref, dil, nt, tq, BF16)
            do3 = _gather_classes(dof, dil, nt, tq, BF16)
            lt3 = _gather_classes(l_ref, dil, nt, tq, F32)
            dl3 = _gather_classes(dlf, dil, nt, tq, F32)
            if nt > 1:
                k3, v3 = _with_prev_tile(k3, dil, nt), _with_prev_tile(v3, dil, nt)
            valid = _band_valid(dil, nt, tq)
            dq = jnp.zeros(q3.shape, F32)
            dk = jnp.zeros(k3.shape, F32)
            dv = jnp.zeros(k3.shape, F32)
            for lm in lms:
                qm = jnp.where(lm, q3, jnp.zeros_like(q3))
                dom = jnp.where(lm, do3, jnp.zeros_like(do3))
                p = jnp.where(valid, jnp.exp(_bqk(qm, k3) - _head_col(lt3, lm)), 0.0)
                dv = dv + _bkd(p.astype(BF16), dom)
                ds = (p * (_bqk(dom, v3) - _head_col(dl3, lm))).astype(BF16)
                dq = dq + jnp.where(lm, _bqd(ds, k3), 0.0)
                dk = dk + _bkd(ds, qm)
            _scatter_classes(dq_ref, dq, dil, nt, tq, add=True)
            for ref, g in ((dk_ref, dk), (dv_ref, dv)):
                if nt > 1:
                    own, prev = g[:, tq:, :], g[:, :tq, :]
                    shifted = []
                    for r in range(dil):
                        t = prev[r * nt:(r + 1) * nt]
                        shifted.append(jnp.concatenate([t[1:], jnp.zeros_like(t[:1])], axis=0))
                    g = own + (shifted[0] if dil == 1 else jnp.concatenate(shifted, axis=0))
                _scatter_classes(ref, g, dil, nt, tq, add=True)

        @pl.when((b == n_steps[0] - 1) & (hp == n_steps[1] - 1))
        def _():
            rs.finish(part_refs, rs_scratch, g_refs)

    blk = pl.BlockSpec((seq, LANES), lambda b, hp: (b, hp))
    s = jax.ShapeDtypeStruct
    outs = pl.pallas_call(
        body, name="attn_bwd", grid=n_steps,
        out_shape=[s((t_tok, ATTN_W), F32)] * 4 + [s(p.shape[1:], F32) for p in parts],
        in_specs=[_zblock(seq, 0), _zblock(seq, 1), _zblock(seq, 2), _zblock(seq, 3), _full(gq2.shape),
                  _full(gk2.shape), blk, blk, blk] + [pl.BlockSpec(memory_space=pl.ANY)] * n_rs,
        out_specs=[blk] * 4 + [_full(p.shape[1:]) for p in parts],
        scratch_shapes=[pltpu.VMEM((seq, LANES), F32)] * 4 + rs.scratch(parts[0].dtype),
        compiler_params=_params(2, VMEM_LIMIT_V7X),
    )(z, z, z, z, gq2, gk2, o, lse, dmix, *parts)
    return outs[:4], outs[4:]


def _ssm_bwd(z, dmix, y, x_re, x_im, a_re, a_im, bb_re, bb_im, cc_re, cc_im, d_skip, w_glu, b_glu, nb, seq):
    t_tok = nb * seq
    tc = min(SSM_CHUNK, seq)
    nch = seq // tc
    grp = N_STATE // 4

    def body(u_ref, gs_ref, ds_ref, y_ref, xr_ref, xi_ref, xpr_ref, xpi_ref,
             ar_ref, ai_ref, bbr_ref, bbi_ref, ccr_ref, cci_ref, d_ref, wg_ref, bg_ref,
             du_ref, dgs_ref, dwg_ref, dbg_ref, dd_ref, dar_ref, dai_ref, dbbr_ref, dbbi_ref, dccr_ref, dcci_ref,
             lam_re, lam_im, car_re, car_im, acc_wg, seg_a, seg_b, ent_re, ent_im):
        step = pl.program_id(1)
        first_chunk = step == nch - 1

        @pl.when((pl.program_id(0) == 0) & (step == 0))
        def _():
            acc_wg[...] = jnp.zeros_like(acc_wg)
            for ref in (dbg_ref, dd_ref, dar_ref, dai_ref, dbbr_ref, dbbi_ref, dccr_ref, dcci_ref):
                ref[...] = jnp.zeros_like(ref)

        @pl.when(step == 0)
        def _():
            car_re[...] = jnp.zeros_like(car_re)
            car_im[...] = jnp.zeros_like(car_im)

        u, gs, dssm, y = u_ref[...], gs_ref[...], ds_ref[...], y_ref[...]
        yg, dgelu = _gelu_and_grad(y)
        ygb = yg.astype(BF16)
        sgl = _sig(_dot(ygb, wg_ref[...]) + bg_ref[...])
        sgs = _sig(gs)
        dout = dssm * gs * sgs
        dgs_ref[...] = dssm * yg * sgl * sgs * (1.0 + gs * (1.0 - sgs))
        dgl = dout * yg * sgl * (1.0 - sgl)
        dglb = dgl.astype(BF16)
        dyg = dout * sgl + _dot_nt(dglb, wg_ref[...])
        acc_wg[...] += _dot_tn(ygb, dglb)
        dbg_ref[...] += jnp.sum(dgl, axis=0, keepdims=True)
        dy = dyg * dgelu
        dd_ref[...] += jnp.sum(dy * u, axis=0, keepdims=True)
        _to_segments(seg_a, dy)
        _to_segments(seg_b, u)
        for j in range(4):
            dyj = seg_a[j].astype(BF16)
            sl = slice(j * grp, (j + 1) * grp)
            lam_re[:, sl] = _dot(dyj, ccr_ref[j])
            lam_im[:, sl] = -_dot(dyj, cci_ref[j])
            dccr_ref[j] += _dot_tn(dyj, xr_ref[:, sl].astype(BF16))
            dcci_ref[j] -= _dot_tn(dyj, xi_ref[:, sl].astype(BF16))

        keep_prev = jnp.where(first_chunk, 0.0, 1.0)
        seg = tc // SUBLANES
        last_blk = pl.ds((seg - 1) * SUBLANES, SUBLANES)
        row0 = lax.broadcasted_iota(jnp.int32, (SUBLANES, N_STATE), 0) == 0
        for src, prev, dst in ((xr_ref, xpr_ref, ent_re), (xi_ref, xpi_ref, ent_im)):
            before = jnp.broadcast_to(prev[SUBLANES - 1:SUBLANES, :] * keep_prev, (SUBLANES, N_STATE))
            dst[...] = jnp.where(row0, before, pltpu.roll(src[last_blk, :], 1, 0))

        def visit(cols, j, lr, li, acc):
            if j is None:
                dar_ref[:, cols] += jnp.sum(acc[0], axis=0, keepdims=True)
                dai_ref[:, cols] += jnp.sum(acc[1], axis=0, keepdims=True)
                return None
            blk = pl.ds(pl.multiple_of(jnp.maximum(j - 1, 0) * SUBLANES, SUBLANES), SUBLANES)
            inside = j > 0
            xpr = jnp.where(inside, xr_ref[blk, cols], ent_re[:, cols])
            xpi = jnp.where(inside, xi_ref[blk, cols], ent_im[:, cols])
            return acc[0] + lr * xpr + li * xpi, acc[1] + li * xpr - lr * xpi

        _scan_chunk(lam_re, lam_im, ar_ref, ai_ref, car_re, car_im, tc, reverse=True, visit=visit)

        for j in range(4):
            sl = slice(j * grp, (j + 1) * grp)
            lr = lam_re[:, sl].astype(BF16)
            li = lam_im[:, sl].astype(BF16)
            uj = seg_b[j].astype(BF16)
            seg_a[j] = _dot_nt(lr, bbr_ref[j]) + _dot_nt(li, bbi_ref[j])
            dbbr_ref[j] += _dot_tn(uj, lr)
            dbbi_ref[j] += _dot_tn(uj, li)
        du_ref[...] = _from_segments(seg_a) + dy * d_ref[...]

        @pl.when((pl.program_id(0) == nb - 1) & (step == nch - 1))
        def _():
            dwg_ref[...] = acc_wg[...].astype(BF16)

    rev = lambda b, ch: b * nch + (nch - 1 - ch)
    umap = lambda b, ch: (rev(b, ch), 4)
    gmap = lambda b, ch: (rev(b, ch), 5)
    smap = lambda b, ch: (rev(b, ch), 1)
    row = lambda b, ch: (rev(b, ch), 0)
    prev = lambda b, ch: (jnp.maximum(rev(b, ch) * (tc // SUBLANES) - 1, 0), 0)
    s = jax.ShapeDtypeStruct
    consts = [a_re, a_im, bb_re, bb_im, cc_re, cc_im, d_skip, w_glu, b_glu]
    acc_shapes = [s((1, SSM_W), F32), s((1, SSM_W), F32), s((1, N_STATE), F32), s((1, N_STATE), F32),
                  s(bb_re.shape, F32), s(bb_re.shape, F32), s(cc_re.shape, F32), s(cc_re.shape, F32)]
    return pl.pallas_call(
        body, name="ssm_bwd", grid=(nb, nch),
        out_shape=[s((t_tok, SSM_W), F32), s((t_tok, SSM_W), F32), s((SSM_W, SSM_W), BF16)] + acc_shapes,
        in_specs=[pl.BlockSpec((tc, SSM_W), umap), pl.BlockSpec((tc, SSM_W), gmap), pl.BlockSpec((tc, SSM_W), smap),
                  pl.BlockSpec((tc, SSM_W), row), pl.BlockSpec((tc, N_STATE), row), pl.BlockSpec((tc, N_STATE), row),
                  pl.BlockSpec((SUBLANES, N_STATE), prev), pl.BlockSpec((SUBLANES, N_STATE), prev)]
        + [_full(c.shape) for c in consts],
        out_specs=[pl.BlockSpec((tc, SSM_W), row), pl.BlockSpec((tc, SSM_W), row), _full((SSM_W, SSM_W))]
        + [_full(a.shape) for a in acc_shapes],
        scratch_shapes=[pltpu.VMEM((tc, N_STATE), F32), pltpu.VMEM((tc, N_STATE), F32),
                        pltpu.VMEM((SUBLANES, N_STATE), F32), pltpu.VMEM((SUBLANES, N_STATE), F32),
                        pltpu.VMEM((SSM_W, SSM_W), F32), pltpu.VMEM((4, tc, LANES), F32),
                        pltpu.VMEM((4, tc, LANES), F32),
                        pltpu.VMEM((SUBLANES, N_STATE), F32), pltpu.VMEM((SUBLANES, N_STATE), F32)],
        compiler_params=_params(2, VMEM_LIMIT_V7X),
    )(z, z, dmix, y, x_re, x_im, x_re, x_im, *consts)


def _dz_and_dx(x2, z, dqh, dkh, dvb, dga, du, dgs, dh1, w_in_g, g_mix, gq_t, gk_t, ones_bd, fold):
    t_tok = x2.shape[0]
    tm = min(256, t_tok)
    nt = t_tok // tm
    a_w = ATTN_W

    def head_norm_bwd(raw, d_hat, gain, scale, ones):
        r = lax.rsqrt(_hdot(raw * raw, ones) * (1.0 / HEAD_DIM) + EPS)
        n = raw * r
        a = d_hat * gain * scale
        d_raw = r * (a - n * (_hdot(a * n, ones) * (1.0 / HEAD_DIM)))
        return d_raw, jnp.sum(d_hat * n * scale, axis=0, keepdims=True)

    def body(x_ref, q_ref, k_ref, dq_ref, dk_ref, dv_ref, dga_ref, du_ref, dgs_ref, dh1_ref, w_ref, g_ref,
             gq_ref, gk_ref, ones_ref, fold_ref, dz_ref, gx_ref, dgm_ref, dgq_ref, dgk_ref, acc_q, acc_k):
        i = pl.program_id(0)

        @pl.when(i == 0)
        def _():
            dgm_ref[...] = jnp.zeros_like(dgm_ref)
            acc_q[...] = jnp.zeros_like(acc_q)
            acc_k[...] = jnp.zeros_like(acc_k)

        ones = ones_ref[...]
        dq, sq = head_norm_bwd(q_ref[...], dq_ref[...], gq_ref[...], HEAD_DIM ** -0.5, ones)
        dk, sk = head_norm_bwd(k_ref[...], dk_ref[...], gk_ref[...], 1.0, ones)
        acc_q[...] += jnp.broadcast_to(sq, acc_q.shape)
        acc_k[...] += jnp.broadcast_to(sk, acc_k.shape)
        parts = (dq, dk, dv_ref[...], dga_ref[...], du_ref[...], dgs_ref[...])
        for n, part in enumerate(parts):
            dz_ref[:, n * a_w:(n + 1) * a_w] = part.astype(BF16)
        dxn = jnp.zeros((tm, D_MODEL), F32)
        for j in range(N_DEV):
            dxn = dxn + _dot_nt(dz_ref[:, j * COL_W:(j + 1) * COL_W], w_ref[j])
        x = x_ref[...]
        r1 = lax.rsqrt(jnp.mean(x * x, axis=-1, keepdims=True) + EPS)
        xnorm = x * r1
        dgm_ref[...] += jnp.sum(dxn * xnorm, axis=0, keepdims=True)
        a = dxn * g_ref[...]
        gx_ref[...] = dh1_ref[...] + r1 * (a - xnorm * jnp.mean(a * xnorm, axis=-1, keepdims=True))

        @pl.when(i == nt - 1)
        def _():
            dgq_ref[...] = _hdot(acc_q[...], fold_ref[...])
            dgk_ref[...] = _hdot(acc_k[...], fold_ref[...])

    row = lambda i: (i, 0)
    col = lambda n: (lambda i: (i, n))
    s = jax.ShapeDtypeStruct
    half = pl.BlockSpec((tm, a_w), row)
    return pl.pallas_call(
        body, name="dz_dx", grid=(nt,),
        out_shape=[s((t_tok, IN_W), BF16), s((t_tok, D_MODEL), F32), s((1, D_MODEL), F32),
                   s((SUBLANES, HEAD_DIM), F32), s((SUBLANES, HEAD_DIM), F32)],
        in_specs=[pl.BlockSpec((tm, D_MODEL), row), pl.BlockSpec((tm, a_w), col(0)), pl.BlockSpec((tm, a_w), col(1)),
                  half, half, half, half, half, half, pl.BlockSpec((tm, D_MODEL), row),
                  _full(w_in_g.shape), _full(g_mix.shape), _full(gq_t.shape), _full(gk_t.shape),
                  _full(ones_bd.shape), _full(fold.shape)],
        out_specs=[pl.BlockSpec((tm, IN_W), row), pl.BlockSpec((tm, D_MODEL), row), _full((1, D_MODEL)),
                   _full((SUBLANES, HEAD_DIM)), _full((SUBLANES, HEAD_DIM))],
        scratch_shapes=[pltpu.VMEM((SUBLANES, a_w), F32), pltpu.VMEM((SUBLANES, a_w), F32)],
        compiler_params=_params(1, VMEM_LIMIT_V7X),
    )(x2, z, z, dqh, dkh, dvb, dga, du, dgs, dh1, w_in_g, g_mix, gq_t, gk_t, ones_bd, fold)


def _dw_in(xn, dz, glu_parts, small):
    t_tok = xn.shape[0]
    tk = min(1024, t_tok)
    nk = t_tok // tk
    rs = _ReduceScatter([glu_parts.shape])
    ag = _AllGather(1, cast=False)
    n_rs = len(rs.scratch(BF16))

    def place():
        x, y, c = lax.axis_index("x"), lax.axis_index("y"), lax.axis_index("c")
        return x, y, c, [(1 - x, 1 - y), (1 - x, y), (x, 1 - y)]

    def target(i):
        x, y, c, _ = place()
        n = i // 2
        px = jnp.where((n == 0) | (n == 1), 1 - x, x)
        py = jnp.where((n == 0) | (n == 2), 1 - y, y)
        pc = jnp.where(i % 2 == 0, 1 - c, c)
        return 4 * px + 2 * py + pc

    chunk, chunks = _row_chunks(D_MODEL)

    def body(xn_ref, dz_ref, glu_ref, small_ref, gin_ref, gglu_ref, gath_ref, acc, stage, land, send_sems, recv_sems,
             *rest):
        rs_scratch, ag_sems = rest[:n_rs], rest[n_rs:]
        i, k = pl.program_id(0), pl.program_id(1)
        x, y, c, chips = place()

        def push(slot, to):
            return pltpu.make_async_remote_copy(
                src_ref=stage.at[slot], dst_ref=land.at[slot], send_sem=send_sems.at[slot],
                recv_sem=recv_sems.at[slot], device_id=to, device_id_type=MESH)

        pushes = [push(n, (x, y, 1 - c)) for n in range(4)] + [push(4 + n, (*chips[n], c)) for n in range(3)]

        def staged(slot, plus=None):
            def put(s, carry):
                r = pl.ds(pl.multiple_of(s * chunk, chunk), chunk)
                val = acc[r, :]
                if plus is not None:
                    val = val + land[plus, r, :].astype(F32)
                stage[slot, r, :] = val.astype(BF16)
                return carry

            lax.fori_loop(0, chunks, put, 0)

        @pl.when((i == 0) & (k == 0))
        def _():
            rs.start([glu_ref], rs_scratch)
            ag.start([small_ref], [gath_ref], ag_sems)

        @pl.when((i == N_DEV // 2) & (k == 0))
        def _():
            ag.forward([small_ref], [gath_ref], ag_sems)

        @pl.when(k == 0)
        def _():
            acc[...] = jnp.zeros_like(acc)

        acc[...] += _dot_tn(xn_ref[...], dz_ref[...])

        for n in range(4):
            @pl.when((k == nk - 1) & (i == 2 * n))
            def _(n=n):
                staged(n)
                pushes[n].start()

        for n in range(3):
            @pl.when((k == nk - 1) & (i == 2 * n + 1))
            def _(n=n):
                pushes[n].wait_recv()
                staged(4 + n, plus=n)
                pushes[4 + n].start()

        @pl.when((k == nk - 1) & (i == N_DEV - 1))
        def _():
            for slot in range(3, N_DEV - 1):
                pushes[slot].wait_recv()

            def add(s, carry):
                r = pl.ds(pl.multiple_of(s * chunk, chunk), chunk)
                total = acc[r, :]
                for slot in range(3, N_DEV - 1):
                    total = total + land[slot, r, :].astype(F32)
                gin_ref[r, :] = total
                return carry

            lax.fori_loop(0, chunks, add, 0)
            for cp in pushes:
                cp.wait_send()
            rs.finish([glu_ref], rs_scratch, [gglu_ref])
            ag.finish([small_ref], [gath_ref], ag_sems)

    any_spec = pl.BlockSpec(memory_space=pl.ANY)
    s = jax.ShapeDtypeStruct
    return pl.pallas_call(
        body, name="dw_in", grid=(N_DEV, nk),
        out_shape=[s((D_MODEL, COL_W), F32), s(glu_parts.shape[1:], F32), s((N_DEV,) + small.shape, F32)],
        in_specs=[pl.BlockSpec((tk, D_MODEL), lambda i, k: (k, 0)),
                  pl.BlockSpec((tk, COL_W), lambda i, k: (k, target(i))), any_spec, any_spec],
        out_specs=[_full((D_MODEL, COL_W)), _full(glu_parts.shape[1:]), any_spec],
        scratch_shapes=[pltpu.VMEM((D_MODEL, COL_W), F32), pltpu.VMEM((N_DEV - 1, D_MODEL, COL_W), BF16),
                        pltpu.VMEM((N_DEV - 1, D_MODEL, COL_W), BF16), pltpu.SemaphoreType.DMA((N_DEV - 1,)),
                        pltpu.SemaphoreType.DMA((N_DEV - 1,))] + rs.scratch(BF16) + ag.scratch(),
        compiler_params=_params(2, VMEM_LIMIT_V7X),
    )(xn, dz, glu_parts, small)


SMALL = ("mix_norm", "q_norm", "k_norm", "lambda_re", "lambda_im", "log_dt", "b_re", "b_im", "c_re", "c_im",
         "d_skip", "b_glu", "ple_norm")
BIG = ("w_in", "w_glu", "w_out", "w_ple_gate", "w_ple_proj")
WEIGHTS = ("mix_norm", "w_in", "q_norm", "k_norm", "lambda_re", "lambda_im", "log_dt", "b_re", "b_im", "c_re",
           "c_im", "d_skip", "w_glu", "b_glu", "w_out", "ple_norm", "w_ple_gate", "w_ple_proj")


def _pack(arrs):
    flat = jnp.concatenate([a.reshape(-1).astype(F32) for a in arrs])
    rows = -(-flat.shape[0] // (64 * LANES)) * 64
    return jnp.pad(flat, (0, rows * LANES - flat.shape[0])).reshape(rows, LANES)


def _unpack(packed, shapes):
    flat = packed.reshape(-1)
    out, off = [], 0
    for shp in shapes:
        size = math.prod(shp)
        out.append(flat[off:off + size].reshape(shp))
        off += size
    return out


def kernel(x, p, mix_norm, w_in, q_norm, k_norm, lambda_re, lambda_im, log_dt, b_re, b_im, c_re, c_im, d_skip, w_glu, b_glu, w_out, ple_norm, w_ple_gate, w_ple_proj, loss_target, m_mix_norm, m_w_in, m_q_norm, m_k_norm, m_lambda_re, m_lambda_im, m_log_dt, m_b_re, m_b_im, m_c_re, m_c_im, m_d_skip, m_w_glu, m_b_glu, m_w_out, m_ple_norm, m_w_ple_gate, m_w_ple_proj, v_mix_norm, v_w_in, v_q_norm, v_k_norm, v_lambda_re, v_lambda_im, v_log_dt, v_b_re, v_b_im, v_c_re, v_c_im, v_d_skip, v_w_glu, v_b_glu, v_w_out, v_ple_norm, v_w_ple_gate, v_w_ple_proj):
    env = dict(locals())
    w = {n: env[n] for n in WEIGHTS}
    m = {n: env["m_" + n] for n in WEIGHTS}
    v = {n: env["v_" + n] for n in WEIGHTS}
    nb, seq, _ = x.shape
    t_tok = nb * seq
    x2 = x.reshape(t_tok, D_MODEL)
    tg2 = loss_target.reshape(t_tok, D_MODEL)
    p2 = p.reshape(t_tok, PLE_DIM)

    shard2d = {"w_in": (D_MODEL, COL_W), "w_glu": (SSM_W // N_DEV, SSM_W), "w_out": (D_MODEL // N_DEV, D_MODEL),
               "w_ple_gate": (D_MODEL // N_DEV, D_MODEL), "w_ple_proj": (PLE_DIM, D_MODEL // N_DEV)}
    w_sh = [w[n].reshape(shard2d[n]) for n in BIG]

    g3 = (SSM_GROUPS, 1, SSM_STATE)
    lr3, li3 = lambda_re.reshape(g3), lambda_im.reshape(g3)
    dt3 = log_dt.reshape(SSM_GROUPS, 1, 1)
    btr = b_re[0].transpose(0, 2, 1)
    bti = b_im[0].transpose(0, 2, 1)
    a_re3, a_im3, bb_re, bb_im, cc_re, cc_im = _zoh_fwd(lr3, li3, dt3, btr, bti, c_re[0], c_im[0])
    a_re, a_im = a_re3.reshape(1, N_STATE), a_im3.reshape(1, N_STATE)

    ones_bd = _head_ones()
    fold = _head_fold()
    gq_t = jnp.tile(q_norm, (1, ATTN_W // HEAD_DIM))
    gk_t = jnp.tile(k_norm, (1, ATTN_W // HEAD_DIM))

    gq2 = jnp.tile(q_norm, (1, LANES // HEAD_DIM))
    gk2 = jnp.tile(k_norm, (1, LANES // HEAD_DIM))

    z, xn, w_in_g = _in_proj(x2, mix_norm, w_sh[0])
    (o, lse, ag), (w_glu_g, w_out_g, w_g_g, w_p_g) = _attn_fwd(z, gq2, gk2, nb, seq, w_sh[1:])
    w_glu_f = w_glu_g.reshape(SSM_W, SSM_W)
    w_out_f = w_out_g.reshape(D_MODEL, D_MODEL)
    w_g_f = w_g_g.reshape(D_MODEL, D_MODEL)
    x_re, x_im, y, sg = _ssm_fwd(z, a_re, a_im, bb_re, bb_im, cc_re, cc_im, d_skip, w_glu_f, b_glu, nb, seq)
    dmix, dh1, loss_t, d_ple, dw_out, dw_g, dw_p = _tail(x2, tg2, ag, sg, p2, w_out_f, w_g_f, w_p_g, ple_norm)

    early_parts = [dw_out.reshape(N_DEV, D_MODEL // N_DEV, D_MODEL), dw_g.reshape(N_DEV, D_MODEL // N_DEV, D_MODEL),
                   dw_p]
    (dqh, dkh, dvb, dga), (g_out, g_g, g_p) = _attn_bwd(z, gq2, gk2, o, lse, dmix, nb, seq, early_parts)
    (du, dgs, dw_glu, d_bglu, d_dskip, da_re, da_im, dbb_re, dbb_im, dcc_re, dcc_im) = _ssm_bwd(
        z, dmix, y, x_re, x_im, a_re, a_im, bb_re, bb_im, cc_re, cc_im, d_skip, w_glu_f, b_glu, nb, seq)
    dz, gx, d_mix, d_gq, d_gk = _dz_and_dx(x2, z, dqh, dkh, dvb, dga, du, dgs, dh1, w_in_g, mix_norm, gq_t, gk_t,
                                           ones_bd, fold)
    d_lr, d_li, d_dt, d_btr, d_bti, d_cr, d_ci = _zoh_bwd(
        lr3, li3, dt3, btr, bti, da_re.reshape(g3), da_im.reshape(g3), dbb_re, dbb_im, dcc_re, dcc_im, fold)
    small_g = {
        "mix_norm": d_mix, "q_norm": d_gq[0:1], "k_norm": d_gk[0:1], "lambda_re": d_lr, "lambda_im": d_li,
        "log_dt": d_dt, "b_re": d_btr, "b_im": d_bti, "c_re": d_cr, "c_im": d_ci,
        "d_skip": d_dskip, "b_glu": d_bglu, "ple_norm": d_ple}

    g_in, g_glu, gathered = _dw_in(xn, dz, dw_glu.reshape(N_DEV, SSM_W // N_DEV, SSM_W),
                                   _pack([small_g[n] for n in SMALL] + [loss_t[0:1, 0:1]]))
    g_sh = [g_in, g_glu, g_out, g_g, g_p]
    d_sh, m_sh, v_sh = _adamw_shards(g_sh, w_sh, [m[n].reshape(shard2d[n]) for n in BIG],
                                     [v[n].reshape(shard2d[n]) for n in BIG])

    g_pk = _small_sum(gathered)
    small_shapes = [w[n].shape for n in SMALL]
    swapped = ("b_re", "b_im")

    def to_own(n, a):
        a = a.reshape(a.shape[1:]) if a.ndim > 2 else a
        return a.transpose(0, 2, 1) if n in swapped else a

    def from_own(n, a):
        a = a.transpose(0, 2, 1) if n in swapped else a
        return a.reshape(w[n].shape)

    own = [to_own(n, w[n]).shape for n in SMALL]
    g_small = _unpack(g_pk, own)
    d_small, m_small, v_small = _adamw_small(
        g_small, *[[to_own(n, src[n]) for n in SMALL] for src in (w, m, v)])

    grads, deltas, new_m, new_v = {}, {}, {}, {}
    for dst, arrs in ((grads, g_small), (deltas, d_small), (new_m, m_small), (new_v, v_small)):
        for n, a in zip(SMALL, arrs):
            dst[n] = from_own(n, a)
    for i, n in enumerate(BIG):
        grads[n] = g_sh[i].reshape(w[n].shape)
        deltas[n] = d_sh[i].reshape(w[n].shape)
        new_m[n] = m_sh[i].reshape(w[n].shape)
        new_v[n] = v_sh[i].reshape(w[n].shape)

    loss = _unpack(g_pk, small_shapes + [()])[-1]
    return (loss, gx.reshape(x.shape), *[grads[n] for n in WEIGHTS], *[deltas[n] for n in WEIGHTS],
            *[new_m[n] for n in WEIGHTS], *[new_v[n] for n in WEIGHTS])
```

```python
import math

import numpy as np
import jax
import jax.numpy as jnp
from jax import lax
from jax.experimental import pallas as pl
from jax.experimental.pallas import tpu as pltpu

F32 = jnp.float32
BF16 = jnp.bfloat16
MESH = pl.DeviceIdType.MESH
AXES = ("x", "y", "c")
N_DEV = 8

D_MODEL = 1024
HEAD_DIM = 64
ATTN_W = 512
SSM_W = 512
SSM_GROUPS = 32
SSM_GROUP = 16
SSM_STATE = 64
N_STATE = SSM_GROUPS * SSM_STATE
PLE_DIM = 256
IN_W = 3072
COL_W = IN_W // N_DEV
DILATED = ((128, 1), (512, 4), (2048, 16))
EPS = 1e-6
INV_SQRT2 = 1.0 / math.sqrt(2.0)
INV_SQRT_2PI = 1.0 / math.sqrt(2.0 * math.pi)

ADAM_LR, ADAM_B1, ADAM_B2, ADAM_EPS, ADAM_WD, ADAM_STEP = 0.001, 0.9, 0.999, 1e-08, 0.01, 10

VMEM_LIMIT_V7X = 56 * 1024 * 1024
SUBLANES = 8
LANES = 128


def _params(n_axes=None, vmem=None):
    kw = {}
    if n_axes:
        kw["dimension_semantics"] = ("arbitrary",) * n_axes
    if vmem:
        kw["vmem_limit_bytes"] = vmem
    return pltpu.CompilerParams(**kw)


def _dot(a, b):
    return jnp.dot(a, b, preferred_element_type=F32)


def _dot_nt(a, b):
    return lax.dot_general(a, b, (((1,), (1,)), ((), ())), preferred_element_type=F32)


def _dot_tn(a, b):
    return lax.dot_general(a, b, (((0,), (0,)), ((), ())), preferred_element_type=F32)


def _hdot(a, ones):
    hi = a.astype(BF16)
    lo = (a - hi.astype(F32)).astype(BF16)
    return _dot(hi, ones) + _dot(lo, ones)


def _sig(x):
    return 1.0 / (1.0 + jnp.exp(-x))


def _gelu_and_grad(y):
    cdf = 0.5 * (1.0 + lax.erf(y * INV_SQRT2))
    pdf = jnp.exp(-0.5 * y * y) * INV_SQRT_2PI
    return y * cdf, cdf + y * pdf


def _vmem():
    return pl.BlockSpec(memory_space=pltpu.VMEM)


def _full(shape):
    nd = len(shape)
    return pl.BlockSpec(shape, lambda *_: (0,) * nd)


class _AllGather:
    def __init__(self, n, cast):
        self.n, self.cast = n, cast

    def scratch(self):
        n = self.n
        return [pltpu.SemaphoreType.DMA((7 * n,)), pltpu.SemaphoreType.DMA((7 * n,)), pltpu.SemaphoreType.DMA((n,))]

    def _plan(self, src_refs, out_refs, sems):
        send_sems, recv_sems, own_sems = sems
        x, y, c = lax.axis_index("x"), lax.axis_index("y"), lax.axis_index("c")
        me, sibling = (x, y, c), (x, y, 1 - c)
        chips = [(1 - x, y), (x, 1 - y), (1 - x, 1 - y)]

        def idx(px, py, pc):
            return 4 * px + 2 * py + pc

        def copy(i, k, block, to, own_src=False):
            ref = out_refs[i].at[idx(*block)]
            return pltpu.make_async_remote_copy(
                src_ref=src_refs[i] if own_src and not self.cast else ref, dst_ref=ref,
                send_sem=send_sems.at[7 * i + k], recv_sem=recv_sems.at[7 * i + k],
                device_id=to, device_id_type=MESH)

        first, passed, arrive_ici, arrive_d2d, own = [], [], [], [], []
        for i in range(self.n):
            first.append(copy(i, 0, me, sibling, own_src=True))
            first += [copy(i, 1 + j, me, (*chip, c), own_src=True) for j, chip in enumerate(chips)]
            arrive_ici += [copy(i, 1 + j, (*chip, c), me) for j, chip in enumerate(chips)]
            passed += [copy(i, 4 + j, (*chip, c), sibling) for j, chip in enumerate(chips)]
            arrive_d2d.append(copy(i, 0, sibling, me))
            arrive_d2d += [copy(i, 4 + j, (*chip, 1 - c), me) for j, chip in enumerate(chips)]
            if not self.cast:
                own.append(pltpu.make_async_copy(src_refs[i], out_refs[i].at[idx(*me)], own_sems.at[i]))
        return idx(*me), first, passed, arrive_ici, arrive_d2d, own

    def start(self, src_refs, out_refs, sems):
        my, first, _, _, _, own = self._plan(src_refs, out_refs, sems)
        if self.cast:
            for i in range(self.n):
                out_refs[i][my] = src_refs[i][...].astype(out_refs[i].dtype)
        for cp in own + first:
            cp.start()

    def forward(self, src_refs, out_refs, sems):
        _, _, passed, arrive_ici, _, _ = self._plan(src_refs, out_refs, sems)
        for cp in arrive_ici:
            cp.wait_recv()
        for cp in passed:
            cp.start()

    def finish(self, src_refs, out_refs, sems):
        _, first, passed, _, arrive_d2d, own = self._plan(src_refs, out_refs, sems)
        for cp in own:
            cp.wait()
        for cp in arrive_d2d:
            cp.wait_recv()
        for cp in first + passed:
            cp.wait_send()


class _HostedGather:
    def __init__(self, shards):
        self.shapes = [(N_DEV,) + a.shape for a in shards]
        self.n = len(shards)
        self.ag = _AllGather(self.n, cast=True)

    def out_shape(self):
        return [jax.ShapeDtypeStruct(s, BF16) for s in self.shapes]

    def scratch(self):
        return [pltpu.VMEM(s, BF16) for s in self.shapes] + self.ag.scratch() + [pltpu.SemaphoreType.DMA((self.n,))]

    def _split(self, scratch):
        return scratch[:self.n], scratch[self.n:-1], scratch[-1]

    def start(self, src_refs, scratch):
        land, sems, _ = self._split(scratch)
        self.ag.start(src_refs, land, sems)

    def forward(self, src_refs, scratch):
        land, sems, _ = self._split(scratch)
        self.ag.forward(src_refs, land, sems)

    def finish(self, src_refs, scratch, out_refs):
        land, sems, out_sems = self._split(scratch)
        self.ag.finish(src_refs, land, sems)
        outs = [pltpu.make_async_copy(land[n], out_refs[n], out_sems.at[n]) for n in range(self.n)]
        for cp in outs:
            cp.start()
        for cp in outs:
            cp.wait()


def _all_gather(shards, out_dtypes, name):
    n = len(shards)
    ag = _AllGather(n, cast=True)

    def body(*refs):
        in_refs, out_refs, sems = refs[:n], refs[n:2 * n], refs[2 * n:]
        ag.start(in_refs, out_refs, sems)
        ag.forward(in_refs, out_refs, sems)
        ag.finish(in_refs, out_refs, sems)

    return pl.pallas_call(
        body, name=name,
        out_shape=[jax.ShapeDtypeStruct((N_DEV,) + s.shape, dt) for s, dt in zip(shards, out_dtypes)],
        in_specs=[_vmem()] * n, out_specs=[_vmem()] * n,
        scratch_shapes=ag.scratch(),
        compiler_params=_params(vmem=VMEM_LIMIT_V7X),
    )(*shards)


def _row_chunks(rows):
    chunk = 64 if rows % 64 == 0 else rows
    return chunk, rows // chunk


class _ReduceScatter:
    def __init__(self, shapes):
        self.shapes = shapes
        self.n = len(shapes)

    def scratch(self, dtype):
        return ([pltpu.VMEM(s, dtype) for s in self.shapes]
                + [pltpu.SemaphoreType.DMA((7 * self.n,)), pltpu.SemaphoreType.DMA((7 * self.n,)),
                   pltpu.SemaphoreType.DMA((self.n,))])

    def _copies(self, in_refs, land_refs, send_sems, recv_sems, own_sems):
        x, y, c = lax.axis_index("x"), lax.axis_index("y"), lax.axis_index("c")
        remote, own = [], []
        for i in range(self.n):
            for m in range(1, N_DEV):
                px = 1 - x if m & 4 else x
                py = 1 - y if m & 2 else y
                pc = 1 - c if m & 1 else c
                remote.append(pltpu.make_async_remote_copy(
                    src_ref=in_refs[i].at[4 * px + 2 * py + pc], dst_ref=land_refs[i].at[m - 1],
                    send_sem=send_sems.at[7 * i + m - 1], recv_sem=recv_sems.at[7 * i + m - 1],
                    device_id=(px, py, pc), device_id_type=MESH))
            own.append(pltpu.make_async_copy(in_refs[i].at[4 * x + 2 * y + c], land_refs[i].at[N_DEV - 1],
                                             own_sems.at[i]))
        return remote, own

    def start(self, in_refs, scratch):
        remote, own = self._copies(in_refs, scratch[:self.n], *scratch[self.n:])
        for cp in remote + own:
            cp.start()

    def finish(self, in_refs, scratch, out_refs):
        land_refs = scratch[:self.n]
        remote, own = self._copies(in_refs, land_refs, *scratch[self.n:])
        for cp in own:
            cp.wait()
        for cp in remote:
            cp.wait_recv()
        for i in range(self.n):
            chunk, steps = _row_chunks(self.shapes[i][1])

            def step(s, carry, i=i, chunk=chunk):
                r = pl.ds(pl.multiple_of(s * chunk, chunk), chunk)
                acc = land_refs[i][N_DEV - 1, r, :].astype(F32)
                for m in range(1, N_DEV):
                    acc = acc + land_refs[i][m - 1, r, :].astype(F32)
                out_refs[i][r, :] = acc
                return carry

            lax.fori_loop(0, steps, step, 0)
        for cp in remote:
            cp.wait_send()


def _reduce_scatter(parts, name):
    n = len(parts)
    rs = _ReduceScatter([p.shape for p in parts])

    def body(*refs):
        in_refs, out_refs, scratch = refs[:n], refs[n:2 * n], refs[2 * n:]
        rs.start(in_refs, scratch)
        rs.finish(in_refs, scratch, out_refs)

    return pl.pallas_call(
        body, name=name,
        out_shape=[jax.ShapeDtypeStruct(p.shape[1:], F32) for p in parts],
        in_specs=[_vmem()] * n, out_specs=[_vmem()] * n,
        scratch_shapes=rs.scratch(parts[0].dtype),
        compiler_params=_params(vmem=VMEM_LIMIT_V7X),
    )(*parts)


def _adamw_math(w, g, m, v):
    m = ADAM_B1 * m + (1.0 - ADAM_B1) * g
    v = ADAM_B2 * v + (1.0 - ADAM_B2) * (g * g)
    m_hat = m / (1.0 - ADAM_B1 ** ADAM_STEP)
    v_hat = v / (1.0 - ADAM_B2 ** ADAM_STEP)
    delta = -ADAM_LR * (m_hat / (jnp.sqrt(v_hat) + ADAM_EPS) + ADAM_WD * w)
    return delta, m, v


def _adamw_shards(gs, ws, ms, vs):
    n = len(gs)

    def body(*refs):
        g_refs, w_refs, m_refs, v_refs = (refs[k * n:(k + 1) * n] for k in range(4))
        d_out, m_out, v_out = (refs[(4 + k) * n:(5 + k) * n] for k in range(3))
        for i in range(n):
            chunk, steps = _row_chunks(gs[i].shape[0])

            def step(s, carry, i=i, chunk=chunk):
                r = pl.ds(pl.multiple_of(s * chunk, chunk), chunk)
                d, m, v = _adamw_math(w_refs[i][r, :], g_refs[i][r, :], m_refs[i][r, :], v_refs[i][r, :])
                d_out[i][r, :] = d
                m_out[i][r, :] = m
                v_out[i][r, :] = v
                return carry

            lax.fori_loop(0, steps, step, 0)

    shapes = [jax.ShapeDtypeStruct(g.shape, F32) for g in gs]
    outs = pl.pallas_call(
        body, name="adamw_shards", out_shape=shapes * 3,
        in_specs=[_vmem()] * (4 * n), out_specs=[_vmem()] * (3 * n),
        compiler_params=_params(vmem=VMEM_LIMIT_V7X),
    )(*gs, *ws, *ms, *vs)
    return outs[:n], outs[n:2 * n], outs[2 * n:]


def _small_sum(gathered):
    n = len(gathered)

    def body(*refs):
        ga_refs, out_refs = refs[:n], refs[n:]
        for i in range(n):
            def total(idx, i=i):
                g = ga_refs[i][(0,) + idx]
                for j in range(1, N_DEV):
                    g = g + ga_refs[i][(j,) + idx]
                out_refs[i][idx] = g

            if len(gathered[i].shape) == 4:
                def step(s, carry, total=total):
                    total((s,))
                    return carry

                lax.fori_loop(0, gathered[i].shape[1], step, 0)
            else:
                total((Ellipsis,))

    return pl.pallas_call(
        body, name="small_sum", out_shape=[jax.ShapeDtypeStruct(g.shape[1:], F32) for g in gathered],
        in_specs=[_vmem()] * n, out_specs=[_vmem()] * n,
        compiler_params=_params(vmem=VMEM_LIMIT_V7X),
    )(*gathered)


def _adamw_small(gs, ws, ms, vs):
    n = len(gs)

    def body(*refs):
        g_refs, w_refs, m_refs, v_refs = (refs[k * n:(k + 1) * n] for k in range(4))
        d_out, m_out, v_out = (refs[(4 + k) * n:(5 + k) * n] for k in range(3))
        for i in range(n):
            def update(idx, i=i):
                d, mm, vv = _adamw_math(w_refs[i][idx], g_refs[i][idx], m_refs[i][idx], v_refs[i][idx])
                d_out[i][idx] = d
                m_out[i][idx] = mm
                v_out[i][idx] = vv

            if len(gs[i].shape) == 3:
                def step(s, carry, update=update):
                    update(s)
                    return carry

                lax.fori_loop(0, gs[i].shape[0], step, 0)
            else:
                update(Ellipsis)

    shapes = [jax.ShapeDtypeStruct(g.shape, F32) for g in gs]
    outs = pl.pallas_call(
        body, name="adamw_small", out_shape=shapes * 3,
        in_specs=[_vmem()] * (4 * n), out_specs=[_vmem()] * (3 * n),
        compiler_params=_params(vmem=VMEM_LIMIT_V7X),
    )(*gs, *ws, *ms, *vs)
    return outs[:n], outs[n:2 * n], outs[2 * n:]


def _zoh(lr, li, logdt, btr, bti):
    dt = jnp.exp(logdt)
    mag = jnp.exp(lr * dt)
    th = li * dt
    ar = mag * jnp.cos(th)
    ai = mag * jnp.sin(th)
    den = lr * lr + li * li
    nr = ar - 1.0
    cr = (nr * lr + ai * li) / den
    ci = (ai * lr - nr * li) / den
    return ar, ai, cr * btr - ci * bti, cr * bti + ci * btr


BD_GROUPS = 8
BD_ROWS = BD_GROUPS * SSM_GROUP
BD_COLS = BD_GROUPS * SSM_STATE
N_BD = SSM_GROUPS // BD_GROUPS


def _bd_mask():
    r = lax.broadcasted_iota(jnp.int32, (BD_ROWS, BD_COLS), 0) // SSM_GROUP
    c = lax.broadcasted_iota(jnp.int32, (BD_ROWS, BD_COLS), 1) // SSM_STATE
    return r == c


def _blockdiag_store(out_ref, t):
    mask = _bd_mask()
    for j in range(N_BD):
        rows = t[j * BD_GROUPS:(j + 1) * BD_GROUPS].reshape(BD_ROWS, SSM_STATE)
        out_ref[j] = jnp.where(mask, jnp.tile(rows, (1, BD_GROUPS)), 0.0).astype(out_ref.dtype)


def _blockdiag_load(m_ref, fold):
    mask = _bd_mask()
    parts = [_hdot(jnp.where(mask, m_ref[j], 0.0), fold).reshape(BD_GROUPS, SSM_GROUP, SSM_STATE)
             for j in range(N_BD)]
    return jnp.concatenate(parts, axis=0)


def _zoh_fwd(lr, li, logdt, btr, bti, c_re, c_im):
    def body(lr_ref, li_ref, dt_ref, br_ref, bi_ref, cr_ref, ci_ref, ar_ref, ai_ref, bbr_ref, bbi_ref, ccr_ref,
             cci_ref):
        ar, ai, bbr, bbi = _zoh(lr_ref[...], li_ref[...], dt_ref[...], br_ref[...], bi_ref[...])
        ar_ref[...] = ar
        ai_ref[...] = ai
        _blockdiag_store(bbr_ref, bbr)
        _blockdiag_store(bbi_ref, bbi)
        _blockdiag_store(ccr_ref, cr_ref[...])
        _blockdiag_store(cci_ref, ci_ref[...])

    s = jax.ShapeDtypeStruct
    bd = s((N_BD, BD_ROWS, BD_COLS), BF16)
    return pl.pallas_call(
        body, name="zoh_fwd", out_shape=[s(lr.shape, F32), s(lr.shape, F32), bd, bd, bd, bd],
        in_specs=[_vmem()] * 7, out_specs=[_vmem()] * 6,
    )(lr, li, logdt, btr, bti, c_re, c_im)


def _zoh_bwd(lr, li, logdt, btr, bti, dar, dai, dbb_re, dbb_im, dcc_re, dcc_im, fold):
    def body(lr_ref, li_ref, dt_ref, br_ref, bi_ref, dar_ref, dai_ref, dbbr_ref, dbbi_ref, dccr_ref, dcci_ref,
             fold_ref, glr_ref, gli_ref, gdt_ref, gbr_ref, gbi_ref, gcr_ref, gci_ref):
        fold_m = fold_ref[...]
        _, vjp = jax.vjp(_zoh, lr_ref[...], li_ref[...], dt_ref[...], br_ref[...], bi_ref[...])
        glr, gli, gdt, gbr, gbi = vjp((dar_ref[...], dai_ref[...], _blockdiag_load(dbbr_ref, fold_m),
                                       _blockdiag_load(dbbi_ref, fold_m)))
        glr_ref[...] = glr
        gli_ref[...] = gli
        gdt_ref[...] = gdt
        gbr_ref[...] = gbr
        gbi_ref[...] = gbi
        gcr_ref[...] = _blockdiag_load(dccr_ref, fold_m)
        gci_ref[...] = _blockdiag_load(dcci_ref, fold_m)

    s = jax.ShapeDtypeStruct
    return pl.pallas_call(
        body, name="zoh_bwd",
        out_shape=[s(lr.shape, F32), s(lr.shape, F32), s(logdt.shape, F32)] + [s(btr.shape, F32)] * 4,
        in_specs=[_vmem()] * 12, out_specs=[_vmem()] * 7,
    )(lr, li, logdt, btr, bti, dar, dai, dbb_re, dbb_im, dcc_re, dcc_im, fold)


def _head_ones():
    r = np.arange(ATTN_W) // HEAD_DIM
    return jnp.asarray(r[:, None] == r[None, :], dtype=BF16)


def _head_fold():
    return jnp.asarray(np.tile(np.eye(HEAD_DIM), (ATTN_W // HEAD_DIM, 1)), dtype=BF16)


def _in_proj(x2, g_mix, w_in_sh):
    t_tok = x2.shape[0]
    tm = min(1024, t_tok)
    nt = t_tok // tm
    ag_w = _AllGather(1, cast=True)
    n_sem = len(ag_w.scratch())

    def owner(i):
        x, y, c = lax.axis_index("x"), lax.axis_index("y"), lax.axis_index("c")
        rel = i // 2
        px = jnp.where((rel == 1) | (rel == 3), 1 - x, x)
        py = jnp.where((rel == 2) | (rel == 3), 1 - y, y)
        pc = jnp.where(i % 2 == 1, 1 - c, c)
        return 4 * px + 2 * py + pc

    def body(*refs):
        x_ref, g_ref, w_ref, z_ref, xn_ref, wg_ref, xn_scr, w_land = refs[:8]
        sems_w, out_sem = refs[8:8 + n_sem], refs[8 + n_sem]
        i, t = pl.program_id(0), pl.program_id(1)
        _, first, passed, arrive_ici, arrive_d2d, _ = ag_w._plan([w_ref], [w_land], sems_w)

        @pl.when((i == 0) & (t == 0))
        def _():
            ag_w.start([w_ref], [w_land], sems_w)

        @pl.when((i == 1) & (t == 0))
        def _():
            arrive_d2d[0].wait_recv()

        for n in range(3):
            @pl.when((i == 2 + 2 * n) & (t == 0))
            def _(n=n):
                arrive_ici[n].wait_recv()
                passed[n].start()

            @pl.when((i == 3 + 2 * n) & (t == 0))
            def _(n=n):
                arrive_d2d[1 + n].wait_recv()

        @pl.when(i == 0)
        def _():
            x = x_ref[...]
            r = lax.rsqrt(jnp.mean(x * x, axis=-1, keepdims=True) + EPS)
            xn = (x * r * g_ref[...]).astype(BF16)
            xn_ref[...] = xn
            xn_scr[t] = xn

        z_ref[...] = _dot(xn_scr[t], w_land[owner(i)])

        @pl.when((i == N_DEV - 1) & (t == nt - 1))
        def _():
            for cp in first + passed:
                cp.wait_send()
            out = pltpu.make_async_copy(w_land, wg_ref, out_sem)
            out.start()
            out.wait()

    s = jax.ShapeDtypeStruct
    xmap = lambda i, t: (jnp.where(i == 0, t, nt - 1), 0)
    gathered = s((N_DEV,) + w_in_sh.shape, BF16)
    return pl.pallas_call(
        body, name="in_proj", grid=(N_DEV, nt),
        out_shape=[s((t_tok, IN_W), F32), s((t_tok, D_MODEL), BF16), gathered],
        in_specs=[pl.BlockSpec((tm, D_MODEL), xmap), _full(g_mix.shape), _full(w_in_sh.shape)],
        out_specs=[pl.BlockSpec((tm, COL_W), lambda i, t: (t, owner(i))), pl.BlockSpec((tm, D_MODEL), xmap),
                   pl.BlockSpec(memory_space=pl.ANY)],
        scratch_shapes=[pltpu.VMEM((nt, tm, D_MODEL), BF16), pltpu.VMEM(gathered.shape, BF16)] + ag_w.scratch()
        + [pltpu.SemaphoreType.DMA],
        compiler_params=_params(2, VMEM_LIMIT_V7X),
    )(x2, g_mix, w_in_sh)


TQ = 128
NEG = -1e30


def _head_col(t, lm):
    return jnp.max(jnp.where(lm, t, NEG), axis=-1, keepdims=True)


def _head_masks():
    lane = lax.broadcasted_iota(jnp.int32, (1, 1, LANES), 2)
    return [(lane // HEAD_DIM) == h for h in range(LANES // HEAD_DIM)]


def _stack_heads(t3, lms):
    return jnp.concatenate([jnp.where(lm, t3, jnp.zeros_like(t3)) for lm in lms], axis=1)


def _unstack_heads(t2, lms, tq):
    out = t2[:, :tq]
    for h in range(1, len(lms)):
        out = jnp.where(lms[h], t2[:, h * tq:(h + 1) * tq], out)
    return out


def _gather_classes(ref, dil, nt, tq, dtype):
    length = nt * tq
    if dil == 1:
        return ref[...].astype(dtype).reshape(nt, tq, LANES)
    parts = [ref[pl.ds(r, length, stride=dil), :].astype(dtype).reshape(nt, tq, LANES) for r in range(dil)]
    return jnp.concatenate(parts, axis=0)


def _scatter_classes(ref, val, dil, nt, tq, add):
    length = nt * tq
    for r in range(dil):
        rows = pl.ds(r, length, stride=dil) if dil > 1 else slice(None)
        part = val[r * nt:(r + 1) * nt].reshape(length, LANES)
        ref[rows, :] = ref[rows, :] + part if add else part


def _with_prev_tile(t3, dil, nt):
    parts = []
    for r in range(dil):
        t = t3[r * nt:(r + 1) * nt]
        parts.append(jnp.concatenate([t[:1], t[:-1]], axis=0))
    prev = parts[0] if dil == 1 else jnp.concatenate(parts, axis=0)
    return jnp.concatenate([prev, t3], axis=1)


def _band_valid(dil, nt, tq):
    if nt == 1:
        shape = (dil, tq, tq)
        return lax.broadcasted_iota(jnp.int32, shape, 1) >= lax.broadcasted_iota(jnp.int32, shape, 2)
    shape = (dil * nt, tq, 2 * tq)
    b = lax.broadcasted_iota(jnp.int32, shape, 0)
    c = lax.broadcasted_iota(jnp.int32, shape, 2)
    d = tq + lax.broadcasted_iota(jnp.int32, shape, 1) - c
    return (d >= 0) & (d <= tq) & (((b & (nt - 1)) != 0) | (c >= tq))


def _window_tiling(seq, window, dil):
    length = seq // dil
    tq = min(TQ, length)
    nt = length // tq
    assert length % tq == 0 and nt & (nt - 1) == 0 and (nt == 1 or window == tq * dil)
    return nt, tq


def _bqk(a, b):
    return jnp.einsum("bqd,bkd->bqk", a, b, preferred_element_type=F32)


def _bqd(a, b):
    return jnp.einsum("bqk,bkd->bqd", a, b, preferred_element_type=F32)


def _bkd(a, b):
    return jnp.einsum("bqk,bqd->bkd", a, b, preferred_element_type=F32)


def _qk_hat(q_ref, k_ref, gq_ref, gk_ref):
    lane = lax.broadcasted_iota(jnp.int32, (1, LANES), 1)

    def norm(raw, gain, scale):
        sq = raw * raw
        r = jnp.zeros_like(raw)
        for h in range(LANES // HEAD_DIM):
            lm = (lane // HEAD_DIM) == h
            ms = jnp.sum(jnp.where(lm, sq, 0.0), axis=-1, keepdims=True) * (1.0 / HEAD_DIM)
            r = jnp.where(lm, lax.rsqrt(ms + EPS), r)
        return raw * r * gain * scale

    return norm(q_ref[...], gq_ref[...], HEAD_DIM ** -0.5), norm(k_ref[...], gk_ref[...], 1.0)


def _zblock(seq, group):
    return pl.BlockSpec((seq, LANES), lambda b, hp: (b, group * (ATTN_W // LANES) + hp))


def _attn_fwd(z, gq2, gk2, nb, seq, late_sh):
    t_tok = nb * seq
    n_win = len(DILATED)
    host = _HostedGather(late_sh)
    n_late = host.n
    n_steps = (nb, ATTN_W // LANES)

    def body(*refs):
        (q_ref, k_ref, v_ref, ga_ref, gq_ref, gk_ref), refs = refs[:6], refs[6:]
        late_refs, refs = refs[:n_late], refs[n_late:]
        (o_ref, l_ref, ag_ref), refs = refs[:3], refs[3:]
        lateg_refs, refs = refs[:n_late], refs[n_late:]
        (qf, kf, oc, lc), host_scratch = refs[:4], refs[4:]
        step = pl.program_id(0) * n_steps[1] + pl.program_id(1)
        total = n_steps[0] * n_steps[1]

        @pl.when(step == 0)
        def _():
            host.start(late_refs, host_scratch)

        @pl.when(step == total // 2)
        def _():
            host.forward(late_refs, host_scratch)

        qf[...], kf[...] = _qk_hat(q_ref, k_ref, gq_ref, gk_ref)
        lms = _head_masks()
        for w, (window, dil) in enumerate(DILATED):
            nt, tq = _window_tiling(seq, window, dil)
            q3 = _gather_classes(qf, dil, nt, tq, BF16)
            k3 = _gather_classes(kf, dil, nt, tq, BF16)
            v3 = _gather_classes(v_ref, dil, nt, tq, BF16)
            if nt > 1:
                k3, v3 = _with_prev_tile(k3, dil, nt), _with_prev_tile(v3, dil, nt)
            valid = _band_valid(dil, nt, tq)
            valid = jnp.concatenate([valid] * len(lms), axis=1)
            s = _bqk(_stack_heads(q3, lms), k3)
            m = jnp.max(jnp.where(valid, s, NEG), axis=-1, keepdims=True)
            p = jnp.where(valid, jnp.exp(s - m), 0.0)
            den = jnp.sum(p, axis=-1, keepdims=True)
            o = _unstack_heads(_bqd(p.astype(BF16), v3) / den, lms, tq)
            lse = _unstack_heads(jnp.broadcast_to(m + jnp.log(den), s.shape[:2] + (LANES,)), lms, tq)
            _scatter_classes(oc.at[w], o, dil, nt, tq, add=False)
            _scatter_classes(lc.at[w], lse, dil, nt, tq, add=False)
        mx = lc[0]
        for w in range(1, n_win):
            mx = jnp.maximum(mx, lc[w])
        tot = jnp.zeros_like(mx)
        o = jnp.zeros_like(mx)
        for w in range(n_win):
            e = jnp.exp(lc[w] - mx)
            tot = tot + e
            o = o + e * oc[w]
        o = o / tot
        o_ref[...] = o
        l_ref[...] = mx + jnp.log(tot)
        ga = ga_ref[...]
        ag_ref[...] = (o * ga * _sig(ga)).astype(BF16)

        @pl.when(step == total - 1)
        def _():
            host.finish(late_refs, host_scratch, lateg_refs)

    blk = pl.BlockSpec((seq, LANES), lambda b, hp: (b, hp))
    s = jax.ShapeDtypeStruct
    outs = pl.pallas_call(
        body, name="attn_fwd", grid=n_steps,
        out_shape=[s((t_tok, ATTN_W), F32), s((t_tok, ATTN_W), F32), s((t_tok, ATTN_W), BF16)] + host.out_shape(),
        in_specs=[_zblock(seq, 0), _zblock(seq, 1), _zblock(seq, 2), _zblock(seq, 3), _full(gq2.shape),
                  _full(gk2.shape)] + [_full(a.shape) for a in late_sh],
        out_specs=[blk, blk, blk] + [pl.BlockSpec(memory_space=pl.ANY)] * n_late,
        scratch_shapes=[pltpu.VMEM((seq, LANES), F32)] * 2 + [pltpu.VMEM((n_win, seq, LANES), F32)] * 2
        + host.scratch(),
        compiler_params=_params(2, VMEM_LIMIT_V7X),
    )(z, z, z, z, gq2, gk2, *late_sh)
    return outs[:3], outs[3:]


SCAN_COLS = 512


def _to_segments(dst_ref, val):
    seg = val.shape[0] // SUBLANES
    for n in range(dst_ref.shape[0]):
        for s in range(SUBLANES):
            dst_ref[n, pl.ds(s, seg, stride=SUBLANES), :] = val[s * seg:(s + 1) * seg, n * LANES:(n + 1) * LANES]


def _from_segments(src_ref):
    seg = src_ref.shape[1] // SUBLANES
    return jnp.concatenate(
        [jnp.concatenate([src_ref[n, pl.ds(s, seg, stride=SUBLANES), :] for s in range(SUBLANES)], axis=0)
         for n in range(src_ref.shape[0])], axis=1)


def _scan_chunk(re_ref, im_ref, a_re_ref, a_im_ref, carry_re, carry_im, rows, reverse, visit=None):
    seg = rows // SUBLANES
    assert seg & (seg - 1) == 0
    rowi = lax.broadcasted_iota(jnp.int32, (SUBLANES, SCAN_COLS), 0)
    edge = (SUBLANES - 1) if reverse else 0
    last = 0 if reverse else SUBLANES - 1
    at_edge = rowi == edge

    def cmul(ar, ai, br, bi):
        return ar * br - ai * bi, ar * bi + ai * br

    for c0 in range(0, N_STATE, SCAN_COLS):
        cols = slice(c0, c0 + SCAN_COLS)
        a1r = jnp.broadcast_to(a_re_ref[:, cols], (SUBLANES, SCAN_COLS))
        a1i = jnp.broadcast_to(a_im_ref[:, cols], (SUBLANES, SCAN_COLS))
        if reverse:
            a1i = -a1i

        def block_of(i):
            j = (seg - 1 - i) if reverse else i
            return j, pl.ds(pl.multiple_of(j * SUBLANES, SUBLANES), SUBLANES)

        def local(i, carry, cols=cols, a1r=a1r, a1i=a1i):
            xr, xi = carry
            _, blk = block_of(i)
            nr, ni = cmul(a1r, a1i, xr, xi)
            xr, xi = nr + re_ref[blk, cols], ni + im_ref[blk, cols]
            re_ref[blk, cols] = xr
            im_ref[blk, cols] = xi
            return xr, xi

        zero = jnp.zeros((SUBLANES, SCAN_COLS), F32)
        er, ei = lax.fori_loop(0, seg, local, (zero, zero))

        pr, pi = a1r, a1i
        for _ in range(seg.bit_length() - 1):
            pr, pi = cmul(pr, pi, pr, pi)
        cr, ci = carry_re[:, cols], carry_im[:, cols]
        inr, ini = cmul(pr, pi, cr, ci)
        er = er + jnp.where(at_edge, inr, 0.0)
        ei = ei + jnp.where(at_edge, ini, 0.0)
        for sft in (1, 2, 4):
            shift, keep = (SUBLANES - sft, rowi < SUBLANES - sft) if reverse else (sft, rowi >= sft)
            rs = jnp.where(keep, pltpu.roll(er, shift, 0), 0.0)
            ims = jnp.where(keep, pltpu.roll(ei, shift, 0), 0.0)
            dr, di = cmul(pr, pi, rs, ims)
            er, ei = er + dr, ei + di
            pr, pi = cmul(pr, pi, pr, pi)
        carry_re[:, cols] = jnp.broadcast_to(er[last:last + 1, :], (SUBLANES, SCAN_COLS))
        carry_im[:, cols] = jnp.broadcast_to(ei[last:last + 1, :], (SUBLANES, SCAN_COLS))
        one = (SUBLANES - 1) if reverse else 1
        kr = jnp.where(at_edge, cr, pltpu.roll(er, one, 0))
        ki = jnp.where(at_edge, ci, pltpu.roll(ei, one, 0))

        def fix(i, carry, cols=cols, a1r=a1r, a1i=a1i):
            kr, ki, acc = carry
            j, blk = block_of(i)
            kr, ki = cmul(a1r, a1i, kr, ki)
            xr, xi = re_ref[blk, cols] + kr, im_ref[blk, cols] + ki
            re_ref[blk, cols] = xr
            im_ref[blk, cols] = xi
            if visit is not None:
                acc = visit(cols, j, xr, xi, acc)
            return kr, ki, acc

        _, _, acc = lax.fori_loop(0, seg, fix, (kr, ki, (zero, zero)))
        if visit is not None:
            visit(cols, None, None, None, acc)


SSM_CHUNK = 512


def _ssm_fwd(z, a_re, a_im, bb_re, bb_im, cc_re, cc_im, d_skip, w_glu, b_glu, nb, seq):
    t_tok = nb * seq
    tc = min(SSM_CHUNK, seq)
    nch = seq // tc
    grp = N_STATE // 4

    def body(u_ref, gs_ref, ar_ref, ai_ref, bbr_ref, bbi_ref, ccr_ref, cci_ref, d_ref, wg_ref, bg_ref,
             xr_ref, xi_ref, y_ref, sg_ref, car_re, car_im, seg_u, seg_y):
        @pl.when(pl.program_id(1) == 0)
        def _():
            car_re[...] = jnp.zeros_like(car_re)
            car_im[...] = jnp.zeros_like(car_im)

        u = u_ref[...]
        _to_segments(seg_u, u)
        for j in range(4):
            uj = seg_u[j].astype(BF16)
            xr_ref[:, j * grp:(j + 1) * grp] = _dot(uj, bbr_ref[j])
            xi_ref[:, j * grp:(j + 1) * grp] = _dot(uj, bbi_ref[j])
        _scan_chunk(xr_ref, xi_ref, ar_ref, ai_ref, car_re, car_im, tc, reverse=False)
        for j in range(4):
            xr = xr_ref[:, j * grp:(j + 1) * grp].astype(BF16)
            xi = xi_ref[:, j * grp:(j + 1) * grp].astype(BF16)
            seg_y[j] = _dot_nt(xr, ccr_ref[j]) - _dot_nt(xi, cci_ref[j])
        y = _from_segments(seg_y) + d_ref[...] * u
        y_ref[...] = y
        yg, _ = _gelu_and_grad(y)
        gl = _dot(yg.astype(BF16), wg_ref[...]) + bg_ref[...]
        gs = gs_ref[...]
        sg_ref[...] = (yg * _sig(gl) * gs * _sig(gs)).astype(BF16)

    umap = lambda b, ch: (b * nch + ch, 4)
    gmap = lambda b, ch: (b * nch + ch, 5)
    row = lambda b, ch: (b * nch + ch, 0)
    s = jax.ShapeDtypeStruct
    consts = [a_re, a_im, bb_re, bb_im, cc_re, cc_im, d_skip, w_glu, b_glu]
    return pl.pallas_call(
        body, name="ssm_fwd", grid=(nb, nch),
        out_shape=[s((t_tok, N_STATE), F32), s((t_tok, N_STATE), F32), s((t_tok, SSM_W), F32),
                   s((t_tok, SSM_W), BF16)],
        in_specs=[pl.BlockSpec((tc, SSM_W), umap), pl.BlockSpec((tc, SSM_W), gmap)] + [_full(c.shape) for c in consts],
        out_specs=[pl.BlockSpec((tc, N_STATE), row), pl.BlockSpec((tc, N_STATE), row),
                   pl.BlockSpec((tc, SSM_W), row), pl.BlockSpec((tc, SSM_W), row)],
        scratch_shapes=[pltpu.VMEM((SUBLANES, N_STATE), F32), pltpu.VMEM((SUBLANES, N_STATE), F32),
                        pltpu.VMEM((4, tc, LANES), F32), pltpu.VMEM((4, tc, LANES), F32)],
        compiler_params=_params(2, VMEM_LIMIT_V7X),
    )(z, z, *consts)


def _tail(x2, tg2, ag, sg, p2, w_out, w_g, w_p, g_ple):
    t_tok = x2.shape[0]
    tm = min(512, t_tok)
    nt = t_tok // tm
    half = ATTN_W

    def body(x_ref, tg_ref, ag_ref, sg_ref, p_ref, wo_ref, wg_ref, wp_ref, gp_ref,
             dmix_ref, dh1_ref, loss_ref, dgp_ref, dwo_ref, dwg_ref, dwp_ref, acc_o, acc_g, acc_p):
        i = pl.program_id(0)

        @pl.when(i == 0)
        def _():
            loss_ref[...] = jnp.zeros_like(loss_ref)
            dgp_ref[...] = jnp.zeros_like(dgp_ref)
            acc_o[...] = jnp.zeros_like(acc_o)
            acc_g[...] = jnp.zeros_like(acc_g)
            acc_p[...] = jnp.zeros_like(acc_p)

        ag_t, sg_t = ag_ref[...], sg_ref[...]
        h1 = x_ref[...] + _dot(ag_t, wo_ref[0:half, :]) + _dot(sg_t, wo_ref[half:2 * half, :])
        r2 = lax.rsqrt(jnp.mean(h1 * h1, axis=-1, keepdims=True) + EPS)
        hnorm = h1 * r2
        gp = gp_ref[...]
        hn = (hnorm * gp).astype(BF16)
        gate = _sig(_dot(hn, wg_ref[...]))
        pb = p_ref[...].astype(BF16)
        pp = jnp.concatenate([_dot(pb, wp_ref[j]) for j in range(N_DEV)], axis=-1)
        h2 = h1 + gate * pp
        err = h2 - tg_ref[...]
        loss_ref[...] += 0.5 * jnp.sum(err * err) * (1.0 / D_MODEL)
        dh2 = err * (1.0 / D_MODEL)
        dpp = (dh2 * gate).astype(BF16)
        dgpre = (dh2 * pp * gate * (1.0 - gate)).astype(BF16)
        acc_p[...] += _dot_tn(pb, dpp)
        acc_g[...] += _dot_tn(hn, dgpre)
        dhn = _dot_nt(dgpre, wg_ref[...])
        dgp_ref[...] += jnp.sum(dhn * hnorm, axis=0, keepdims=True)
        a = dhn * gp
        dh1 = dh2 + r2 * (a - hnorm * jnp.mean(a * hnorm, axis=-1, keepdims=True))
        dh1_ref[...] = dh1
        dh1b = dh1.astype(BF16)
        acc_o[0:half, :] += _dot_tn(ag_t, dh1b)
        acc_o[half:2 * half, :] += _dot_tn(sg_t, dh1b)
        dmix_ref[...] = _dot_nt(dh1b, wo_ref[...])

        @pl.when(i == nt - 1)
        def _():
            dwo_ref[...] = acc_o[...].astype(BF16)
            dwg_ref[...] = acc_g[...].astype(BF16)
            for j in range(N_DEV):
                dwp_ref[j] = acc_p[:, j * LANES:(j + 1) * LANES].astype(BF16)

    row = lambda i: (i, 0)
    s = jax.ShapeDtypeStruct
    return pl.pallas_call(
        body, name="tail_fwd_bwd", grid=(nt,),
        out_shape=[s((t_tok, D_MODEL), F32), s((t_tok, D_MODEL), F32), s((SUBLANES, LANES), F32),
                   s((1, D_MODEL), F32), s((D_MODEL, D_MODEL), BF16), s((D_MODEL, D_MODEL), BF16),
                   s((N_DEV, PLE_DIM, LANES), BF16)],
        in_specs=[pl.BlockSpec((tm, D_MODEL), row), pl.BlockSpec((tm, D_MODEL), row),
                  pl.BlockSpec((tm, half), row), pl.BlockSpec((tm, half), row), pl.BlockSpec((tm, PLE_DIM), row),
                  _full(w_out.shape), _full(w_g.shape), _full(w_p.shape), _full(g_ple.shape)],
        out_specs=[pl.BlockSpec((tm, D_MODEL), row), pl.BlockSpec((tm, D_MODEL), row), _full((SUBLANES, LANES)),
                   _full((1, D_MODEL)), _full((D_MODEL, D_MODEL)), _full((D_MODEL, D_MODEL)),
                   _full((N_DEV, PLE_DIM, LANES))],
        scratch_shapes=[pltpu.VMEM((D_MODEL, D_MODEL), F32), pltpu.VMEM((D_MODEL, D_MODEL), F32),
                        pltpu.VMEM((PLE_DIM, D_MODEL), F32)],
        compiler_params=_params(1, VMEM_LIMIT_V7X),
    )(x2, tg2, ag, sg, p2, w_out, w_g, w_p, g_ple)


def _attn_bwd(z, gq2, gk2, o, lse, dmix, nb, seq, parts):
    t_tok = nb * seq
    n_rs = len(parts)
    rs = _ReduceScatter([p.shape for p in parts])
    n_steps = (nb, ATTN_W // LANES)

    def body(*refs):
        (q_ref, k_ref, v_ref, ga_ref, gq_ref, gk_ref, o_ref, l_ref, da_ref), refs = refs[:9], refs[9:]
        part_refs, refs = refs[:n_rs], refs[n_rs:]
        (dq_ref, dk_ref, dv_ref, dga_ref), refs = refs[:4], refs[4:]
        g_refs, refs = refs[:n_rs], refs[n_rs:]
        (qf, kf, dof, dlf), rs_scratch = refs[:4], refs[4:]
        b, hp = pl.program_id(0), pl.program_id(1)

        @pl.when((b == 0) & (hp == 0))
        def _():
            rs.start(part_refs, rs_scratch)

        ga, o_t, da = ga_ref[...], o_ref[...], da_ref[...]
        sga = _sig(ga)
        d_o = da * ga * sga
        dga_ref[...] = da * o_t * sga * (1.0 + ga * (1.0 - sga))
        lane = lax.broadcasted_iota(jnp.int32, (1, LANES), 1)
        d_oo = d_o * o_t
        delta = jnp.zeros_like(d_oo)
        for h in range(LANES // HEAD_DIM):
            lm2 = (lane // HEAD_DIM) == h
            delta = jnp.where(lm2, jnp.sum(jnp.where(lm2, d_oo, 0.0), axis=-1, keepdims=True), delta)
        qf[...], kf[...] = _qk_hat(q_ref, k_ref, gq_ref, gk_ref)
        dof[...] = d_o
        dlf[...] = delta
        dq_ref[...] = jnp.zeros_like(dq_ref)
        dk_ref[...] = jnp.zeros_like(dk_ref)
        dv_ref[...] = jnp.zeros_like(dv_ref)
        lms = _head_masks()
        for window, dil in DILATED:
            nt, tq = _window_tiling(seq, window, dil)
            q3 = _gather_classes(qf, dil, nt, tq, BF16)
            k3 = _gather_classes(kf, dil, nt, tq, BF16)
            v3 = _gather_classes(v_ref, dil, nt, tq, BF16)
            do3 = _gather_classes(dof, dil, nt, tq, BF16)
            lt3 = _gather_classes(l_ref, dil, nt, tq, F32)
            dl3 = _gather_classes(dlf, dil, nt, tq, F32)
            if nt > 1:
                k3, v3 = _with_prev_tile(k3, dil, nt), _with_prev_tile(v3, dil, nt)
            valid = _band_valid(dil, nt, tq)
            dq = jnp.zeros(q3.shape, F32)
            dk = jnp.zeros(k3.shape, F32)
            dv = jnp.zeros(k3.shape, F32)
            for lm in lms:
                qm = jnp.where(lm, q3, jnp.zeros_like(q3))
                dom = jnp.where(lm, do3, jnp.zeros_like(do3))
                p = jnp.where(valid, jnp.exp(_bqk(qm, k3) - _head_col(lt3, lm)), 0.0)
                dv = dv + _bkd(p.astype(BF16), dom)
                ds = (p * (_bqk(dom, v3) - _head_col(dl3, lm))).astype(BF16)
                dq = dq + jnp.where(lm, _bqd(ds, k3), 0.0)
                dk = dk + _bkd(ds, qm)
            _scatter_classes(dq_ref, dq, dil, nt, tq, add=True)
            for ref, g in ((dk_ref, dk), (dv_ref, dv)):
                if nt > 1:
                    own, prev = g[:, tq:, :], g[:, :tq, :]
                    shifted = []
                    for r in range(dil):
                        t = prev[r * nt:(r + 1) * nt]
                        shifted.append(jnp.concatenate([t[1:], jnp.zeros_like(t[:1])], axis=0))
                    g = own + (shifted[0] if dil == 1 else jnp.concatenate(shifted, axis=0))
                _scatter_classes(ref, g, dil, nt, tq, add=True)

        @pl.when((b == n_steps[0] - 1) & (hp == n_steps[1] - 1))
        def _():
            rs.finish(part_refs, rs_scratch, g_refs)

    blk = pl.BlockSpec((seq, LANES), lambda b, hp: (b, hp))
    s = jax.ShapeDtypeStruct
    outs = pl.pallas_call(
        body, name="attn_bwd", grid=n_steps,
        out_shape=[s((t_tok, ATTN_W), F32)] * 4 + [s(p.shape[1:], F32) for p in parts],
        in_specs=[_zblock(seq, 0), _zblock(seq, 1), _zblock(seq, 2), _zblock(seq, 3), _full(gq2.shape),
                  _full(gk2.shape), blk, blk, blk] + [pl.BlockSpec(memory_space=pl.ANY)] * n_rs,
        out_specs=[blk] * 4 + [_full(p.shape[1:]) for p in parts],
        scratch_shapes=[pltpu.VMEM((seq, LANES), F32)] * 4 + rs.scratch(parts[0].dtype),
        compiler_params=_params(2, VMEM_LIMIT_V7X),
    )(z, z, z, z, gq2, gk2, o, lse, dmix, *parts)
    return outs[:4], outs[4:]


def _ssm_bwd(z, dmix, y, x_re, x_im, a_re, a_im, bb_re, bb_im, cc_re, cc_im, d_skip, w_glu, b_glu, nb, seq):
    t_tok = nb * seq
    tc = min(SSM_CHUNK, seq)
    nch = seq // tc
    grp = N_STATE // 4

    def body(u_ref, gs_ref, ds_ref, y_ref, xr_ref, xi_ref, xpr_ref, xpi_ref,
             ar_ref, ai_ref, bbr_ref, bbi_ref, ccr_ref, cci_ref, d_ref, wg_ref, bg_ref,
             du_ref, dgs_ref, dwg_ref, dbg_ref, dd_ref, dar_ref, dai_ref, dbbr_ref, dbbi_ref, dccr_ref, dcci_ref,
             lam_re, lam_im, car_re, car_im, acc_wg, seg_a, seg_b, ent_re, ent_im):
        step = pl.program_id(1)
        first_chunk = step == nch - 1

        @pl.when((pl.program_id(0) == 0) & (step == 0))
        def _():
            acc_wg[...] = jnp.zeros_like(acc_wg)
            for ref in (dbg_ref, dd_ref, dar_ref, dai_ref, dbbr_ref, dbbi_ref, dccr_ref, dcci_ref):
                ref[...] = jnp.zeros_like(ref)

        @pl.when(step == 0)
        def _():
            car_re[...] = jnp.zeros_like(car_re)
            car_im[...] = jnp.zeros_like(car_im)

        u, gs, dssm, y = u_ref[...], gs_ref[...], ds_ref[...], y_ref[...]
        yg, dgelu = _gelu_and_grad(y)
        ygb = yg.astype(BF16)
        sgl = _sig(_dot(ygb, wg_ref[...]) + bg_ref[...])
        sgs = _sig(gs)
        dout = dssm * gs * sgs
        dgs_ref[...] = dssm * yg * sgl * sgs * (1.0 + gs * (1.0 - sgs))
        dgl = dout * yg * sgl * (1.0 - sgl)
        dglb = dgl.astype(BF16)
        dyg = dout * sgl + _dot_nt(dglb, wg_ref[...])
        acc_wg[...] += _dot_tn(ygb, dglb)
        dbg_ref[...] += jnp.sum(dgl, axis=0, keepdims=True)
        dy = dyg * dgelu
        dd_ref[...] += jnp.sum(dy * u, axis=0, keepdims=True)
        _to_segments(seg_a, dy)
        _to_segments(seg_b, u)
        for j in range(4):
            dyj = seg_a[j].astype(BF16)
            sl = slice(j * grp, (j + 1) * grp)
            lam_re[:, sl] = _dot(dyj, ccr_ref[j])
            lam_im[:, sl] = -_dot(dyj, cci_ref[j])
            dccr_ref[j] += _dot_tn(dyj, xr_ref[:, sl].astype(BF16))
            dcci_ref[j] -= _dot_tn(dyj, xi_ref[:, sl].astype(BF16))

        keep_prev = jnp.where(first_chunk, 0.0, 1.0)
        seg = tc // SUBLANES
        last_blk = pl.ds((seg - 1) * SUBLANES, SUBLANES)
        row0 = lax.broadcasted_iota(jnp.int32, (SUBLANES, N_STATE), 0) == 0
        for src, prev, dst in ((xr_ref, xpr_ref, ent_re), (xi_ref, xpi_ref, ent_im)):
            before = jnp.broadcast_to(prev[SUBLANES - 1:SUBLANES, :] * keep_prev, (SUBLANES, N_STATE))
            dst[...] = jnp.where(row0, before, pltpu.roll(src[last_blk, :], 1, 0))

        def visit(cols, j, lr, li, acc):
            if j is None:
                dar_ref[:, cols] += jnp.sum(acc[0], axis=0, keepdims=True)
                dai_ref[:, cols] += jnp.sum(acc[1], axis=0, keepdims=True)
                return None
            blk = pl.ds(pl.multiple_of(jnp.maximum(j - 1, 0) * SUBLANES, SUBLANES), SUBLANES)
            inside = j > 0
            xpr = jnp.where(inside, xr_ref[blk, cols], ent_re[:, cols])
            xpi = jnp.where(inside, xi_ref[blk, cols], ent_im[:, cols])
            return acc[0] + lr * xpr + li * xpi, acc[1] + li * xpr - lr * xpi

        _scan_chunk(lam_re, lam_im, ar_ref, ai_ref, car_re, car_im, tc, reverse=True, visit=visit)

        for j in range(4):
            sl = slice(j * grp, (j + 1) * grp)
            lr = lam_re[:, sl].astype(BF16)
            li = lam_im[:, sl].astype(BF16)
            uj = seg_b[j].astype(BF16)
            seg_a[j] = _dot_nt(lr, bbr_ref[j]) + _dot_nt(li, bbi_ref[j])
            dbbr_ref[j] += _dot_tn(uj, lr)
            dbbi_ref[j] += _dot_tn(uj, li)
        du_ref[...] = _from_segments(seg_a) + dy * d_ref[...]

        @pl.when((pl.program_id(0) == nb - 1) & (step == nch - 1))
        def _():
            dwg_ref[...] = acc_wg[...].astype(BF16)

    rev = lambda b, ch: b * nch + (nch - 1 - ch)
    umap = lambda b, ch: (rev(b, ch), 4)
    gmap = lambda b, ch: (rev(b, ch), 5)
    smap = lambda b, ch: (rev(b, ch), 1)
    row = lambda b, ch: (rev(b, ch), 0)
    prev = lambda b, ch: (jnp.maximum(rev(b, ch) * (tc // SUBLANES) - 1, 0), 0)
    s = jax.ShapeDtypeStruct
    consts = [a_re, a_im, bb_re, bb_im, cc_re, cc_im, d_skip, w_glu, b_glu]
    acc_shapes = [s((1, SSM_W), F32), s((1, SSM_W), F32), s((1, N_STATE), F32), s((1, N_STATE), F32),
                  s(bb_re.shape, F32), s(bb_re.shape, F32), s(cc_re.shape, F32), s(cc_re.shape, F32)]
    return pl.pallas_call(
        body, name="ssm_bwd", grid=(nb, nch),
        out_shape=[s((t_tok, SSM_W), F32), s((t_tok, SSM_W), F32), s((SSM_W, SSM_W), BF16)] + acc_shapes,
        in_specs=[pl.BlockSpec((tc, SSM_W), umap), pl.BlockSpec((tc, SSM_W), gmap), pl.BlockSpec((tc, SSM_W), smap),
                  pl.BlockSpec((tc, SSM_W), row), pl.BlockSpec((tc, N_STATE), row), pl.BlockSpec((tc, N_STATE), row),
                  pl.BlockSpec((SUBLANES, N_STATE), prev), pl.BlockSpec((SUBLANES, N_STATE), prev)]
        + [_full(c.shape) for c in consts],
        out_specs=[pl.BlockSpec((tc, SSM_W), row), pl.BlockSpec((tc, SSM_W), row), _full((SSM_W, SSM_W))]
        + [_full(a.shape) for a in acc_shapes],
        scratch_shapes=[pltpu.VMEM((tc, N_STATE), F32), pltpu.VMEM((tc, N_STATE), F32),
                        pltpu.VMEM((SUBLANES, N_STATE), F32), pltpu.VMEM((SUBLANES, N_STATE), F32),
                        pltpu.VMEM((SSM_W, SSM_W), F32), pltpu.VMEM((4, tc, LANES), F32),
                        pltpu.VMEM((4, tc, LANES), F32),
                        pltpu.VMEM((SUBLANES, N_STATE), F32), pltpu.VMEM((SUBLANES, N_STATE), F32)],
        compiler_params=_params(2, VMEM_LIMIT_V7X),
    )(z, z, dmix, y, x_re, x_im, x_re, x_im, *consts)


def _dz_and_dx(x2, z, dqh, dkh, dvb, dga, du, dgs, dh1, w_in_g, g_mix, gq_t, gk_t, ones_bd, fold):
    t_tok = x2.shape[0]
    tm = min(256, t_tok)
    nt = t_tok // tm
    a_w = ATTN_W

    def head_norm_bwd(raw, d_hat, gain, scale, ones):
        r = lax.rsqrt(_hdot(raw * raw, ones) * (1.0 / HEAD_DIM) + EPS)
        n = raw * r
        a = d_hat * gain * scale
        d_raw = r * (a - n * (_hdot(a * n, ones) * (1.0 / HEAD_DIM)))
        return d_raw, jnp.sum(d_hat * n * scale, axis=0, keepdims=True)

    def body(x_ref, q_ref, k_ref, dq_ref, dk_ref, dv_ref, dga_ref, du_ref, dgs_ref, dh1_ref, w_ref, g_ref,
             gq_ref, gk_ref, ones_ref, fold_ref, dz_ref, gx_ref, dgm_ref, dgq_ref, dgk_ref, acc_q, acc_k):
        i = pl.program_id(0)

        @pl.when(i == 0)
        def _():
            dgm_ref[...] = jnp.zeros_like(dgm_ref)
            acc_q[...] = jnp.zeros_like(acc_q)
            acc_k[...] = jnp.zeros_like(acc_k)

        ones = ones_ref[...]
        dq, sq = head_norm_bwd(q_ref[...], dq_ref[...], gq_ref[...], HEAD_DIM ** -0.5, ones)
        dk, sk = head_norm_bwd(k_ref[...], dk_ref[...], gk_ref[...], 1.0, ones)
        acc_q[...] += jnp.broadcast_to(sq, acc_q.shape)
        acc_k[...] += jnp.broadcast_to(sk, acc_k.shape)
        parts = (dq, dk, dv_ref[...], dga_ref[...], du_ref[...], dgs_ref[...])
        for n, part in enumerate(parts):
            dz_ref[:, n * a_w:(n + 1) * a_w] = part.astype(BF16)
        dxn = jnp.zeros((tm, D_MODEL), F32)
        for j in range(N_DEV):
            dxn = dxn + _dot_nt(dz_ref[:, j * COL_W:(j + 1) * COL_W], w_ref[j])
        x = x_ref[...]
        r1 = lax.rsqrt(jnp.mean(x * x, axis=-1, keepdims=True) + EPS)
        xnorm = x * r1
        dgm_ref[...] += jnp.sum(dxn * xnorm, axis=0, keepdims=True)
        a = dxn * g_ref[...]
        gx_ref[...] = dh1_ref[...] + r1 * (a - xnorm * jnp.mean(a * xnorm, axis=-1, keepdims=True))

        @pl.when(i == nt - 1)
        def _():
            dgq_ref[...] = _hdot(acc_q[...], fold_ref[...])
            dgk_ref[...] = _hdot(acc_k[...], fold_ref[...])

    row = lambda i: (i, 0)
    col = lambda n: (lambda i: (i, n))
    s = jax.ShapeDtypeStruct
    half = pl.BlockSpec((tm, a_w), row)
    return pl.pallas_call(
        body, name="dz_dx", grid=(nt,),
        out_shape=[s((t_tok, IN_W), BF16), s((t_tok, D_MODEL), F32), s((1, D_MODEL), F32),
                   s((SUBLANES, HEAD_DIM), F32), s((SUBLANES, HEAD_DIM), F32)],
        in_specs=[pl.BlockSpec((tm, D_MODEL), row), pl.BlockSpec((tm, a_w), col(0)), pl.BlockSpec((tm, a_w), col(1)),
                  half, half, half, half, half, half, pl.BlockSpec((tm, D_MODEL), row),
                  _full(w_in_g.shape), _full(g_mix.shape), _full(gq_t.shape), _full(gk_t.shape),
                  _full(ones_bd.shape), _full(fold.shape)],
        out_specs=[pl.BlockSpec((tm, IN_W), row), pl.BlockSpec((tm, D_MODEL), row), _full((1, D_MODEL)),
                   _full((SUBLANES, HEAD_DIM)), _full((SUBLANES, HEAD_DIM))],
        scratch_shapes=[pltpu.VMEM((SUBLANES, a_w), F32), pltpu.VMEM((SUBLANES, a_w), F32)],
        compiler_params=_params(1, VMEM_LIMIT_V7X),
    )(x2, z, z, dqh, dkh, dvb, dga, du, dgs, dh1, w_in_g, g_mix, gq_t, gk_t, ones_bd, fold)


def _dw_in(xn, dz, glu_parts, smalls):
    t_tok = xn.shape[0]
    tk = min(1024, t_tok)
    nk = t_tok // tk
    rs = _ReduceScatter([glu_parts.shape])
    n_small = len(smalls)
    ag = _AllGather(n_small, cast=False)
    n_rs = len(rs.scratch(BF16))

    def place():
        x, y, c = lax.axis_index("x"), lax.axis_index("y"), lax.axis_index("c")
        return x, y, c, [(1 - x, y), (x, 1 - y), (1 - x, 1 - y)]

    def target(i):
        x, y, c, _ = place()
        n = i // 2
        px = jnp.where((n == 0) | (n == 2), 1 - x, x)
        py = jnp.where((n == 1) | (n == 2), 1 - y, y)
        pc = jnp.where(i % 2 == 0, 1 - c, c)
        return 4 * px + 2 * py + pc

    chunk, chunks = _row_chunks(D_MODEL)

    def body(*refs):
        (xn_ref, dz_ref, glu_ref), refs = refs[:3], refs[3:]
        small_refs, refs = list(refs[:n_small]), refs[n_small:]
        (gin_ref, gglu_ref), refs = refs[:2], refs[2:]
        gath_refs, refs = list(refs[:n_small]), refs[n_small:]
        (acc, stage, land, send_sems, recv_sems), rest = refs[:5], refs[5:]
        rs_scratch, ag_sems = rest[:n_rs], rest[n_rs:]
        i, k = pl.program_id(0), pl.program_id(1)
        x, y, c, chips = place()

        def push(slot, to):
            return pltpu.make_async_remote_copy(
                src_ref=stage.at[slot], dst_ref=land.at[slot], send_sem=send_sems.at[slot],
                recv_sem=recv_sems.at[slot], device_id=to, device_id_type=MESH)

        pushes = [push(n, (x, y, 1 - c)) for n in range(4)] + [push(4 + n, (*chips[n], c)) for n in range(3)]

        def staged(slot, plus=None):
            def put(s, carry):
                r = pl.ds(pl.multiple_of(s * chunk, chunk), chunk)
                val = acc[r, :]
                if plus is not None:
                    val = val + land[plus, r, :].astype(F32)
                stage[slot, r, :] = val.astype(BF16)
                return carry

            lax.fori_loop(0, chunks, put, 0)

        @pl.when((i == 0) & (k == 0))
        def _():
            rs.start([glu_ref], rs_scratch)
            ag.start(small_refs, gath_refs, ag_sems)

        @pl.when((i == N_DEV // 2) & (k == 0))
        def _():
            ag.forward(small_refs, gath_refs, ag_sems)

        @pl.when(k == 0)
        def _():
            acc[...] = jnp.zeros_like(acc)

        acc[...] += _dot_tn(xn_ref[...], dz_ref[...])

        for n in range(4):
            @pl.when((k == nk - 1) & (i == 2 * n))
            def _(n=n):
                staged(n)
                pushes[n].start()

        for n in range(3):
            @pl.when((k == nk - 1) & (i == 2 * n + 1))
            def _(n=n):
                pushes[n].wait_recv()
                staged(4 + n, plus=n)
                pushes[4 + n].start()

        @pl.when((k == nk - 1) & (i == N_DEV - 1))
        def _():
            for slot in range(3, N_DEV - 1):
                pushes[slot].wait_recv()

            def add(s, carry):
                r = pl.ds(pl.multiple_of(s * chunk, chunk), chunk)
                total = acc[r, :]
                for slot in range(3, N_DEV - 1):
                    total = total + land[slot, r, :].astype(F32)
                gin_ref[r, :] = total
                return carry

            lax.fori_loop(0, chunks, add, 0)
            for cp in pushes:
                cp.wait_send()
            rs.finish([glu_ref], rs_scratch, [gglu_ref])
            ag.finish(small_refs, gath_refs, ag_sems)

    any_spec = pl.BlockSpec(memory_space=pl.ANY)
    s = jax.ShapeDtypeStruct
    outs = pl.pallas_call(
        body, name="dw_in", grid=(N_DEV, nk),
        out_shape=[s((D_MODEL, COL_W), F32), s(glu_parts.shape[1:], F32)]
        + [s((N_DEV,) + a.shape, F32) for a in smalls],
        in_specs=[pl.BlockSpec((tk, D_MODEL), lambda i, k: (k, 0)),
                  pl.BlockSpec((tk, COL_W), lambda i, k: (k, target(i))), any_spec] + [any_spec] * n_small,
        out_specs=[_full((D_MODEL, COL_W)), _full(glu_parts.shape[1:])] + [any_spec] * n_small,
        scratch_shapes=[pltpu.VMEM((D_MODEL, COL_W), F32), pltpu.VMEM((N_DEV - 1, D_MODEL, COL_W), BF16),
                        pltpu.VMEM((N_DEV - 1, D_MODEL, COL_W), BF16), pltpu.SemaphoreType.DMA((N_DEV - 1,)),
                        pltpu.SemaphoreType.DMA((N_DEV - 1,))] + rs.scratch(BF16) + ag.scratch(),
        compiler_params=_params(2, VMEM_LIMIT_V7X),
    )(xn, dz, glu_parts, *smalls)
    return outs[0], outs[1], outs[2:]


SMALL = ("mix_norm", "q_norm", "k_norm", "lambda_re", "lambda_im", "log_dt", "b_re", "b_im", "c_re", "c_im",
         "d_skip", "b_glu", "ple_norm")
BIG = ("w_in", "w_glu", "w_out", "w_ple_gate", "w_ple_proj")
WEIGHTS = ("mix_norm", "w_in", "q_norm", "k_norm", "lambda_re", "lambda_im", "log_dt", "b_re", "b_im", "c_re",
           "c_im", "d_skip", "w_glu", "b_glu", "w_out", "ple_norm", "w_ple_gate", "w_ple_proj")


def kernel(x, p, mix_norm, w_in, q_norm, k_norm, lambda_re, lambda_im, log_dt, b_re, b_im, c_re, c_im, d_skip, w_glu, b_glu, w_out, ple_norm, w_ple_gate, w_ple_proj, loss_target, m_mix_norm, m_w_in, m_q_norm, m_k_norm, m_lambda_re, m_lambda_im, m_log_dt, m_b_re, m_b_im, m_c_re, m_c_im, m_d_skip, m_w_glu, m_b_glu, m_w_out, m_ple_norm, m_w_ple_gate, m_w_ple_proj, v_mix_norm, v_w_in, v_q_norm, v_k_norm, v_lambda_re, v_lambda_im, v_log_dt, v_b_re, v_b_im, v_c_re, v_c_im, v_d_skip, v_w_glu, v_b_glu, v_w_out, v_ple_norm, v_w_ple_gate, v_w_ple_proj):
    env = dict(locals())
    w = {n: env[n] for n in WEIGHTS}
    m = {n: env["m_" + n] for n in WEIGHTS}
    v = {n: env["v_" + n] for n in WEIGHTS}
    nb, seq, _ = x.shape
    t_tok = nb * seq
    x2 = x.reshape(t_tok, D_MODEL)
    tg2 = loss_target.reshape(t_tok, D_MODEL)
    p2 = p.reshape(t_tok, PLE_DIM)

    shard2d = {"w_in": (D_MODEL, COL_W), "w_glu": (SSM_W // N_DEV, SSM_W), "w_out": (D_MODEL // N_DEV, D_MODEL),
               "w_ple_gate": (D_MODEL // N_DEV, D_MODEL), "w_ple_proj": (PLE_DIM, D_MODEL // N_DEV)}
    w_sh = [w[n].reshape(shard2d[n]) for n in BIG]

    g3 = (SSM_GROUPS, 1, SSM_STATE)
    lr3, li3 = lambda_re.reshape(g3), lambda_im.reshape(g3)
    dt3 = log_dt.reshape(SSM_GROUPS, 1, 1)
    btr = b_re[0].transpose(0, 2, 1)
    bti = b_im[0].transpose(0, 2, 1)
    a_re3, a_im3, bb_re, bb_im, cc_re, cc_im = _zoh_fwd(lr3, li3, dt3, btr, bti, c_re[0], c_im[0])
    a_re, a_im = a_re3.reshape(1, N_STATE), a_im3.reshape(1, N_STATE)

    ones_bd = _head_ones()
    fold = _head_fold()
    gq_t = jnp.tile(q_norm, (1, ATTN_W // HEAD_DIM))
    gk_t = jnp.tile(k_norm, (1, ATTN_W // HEAD_DIM))

    gq2 = jnp.tile(q_norm, (1, LANES // HEAD_DIM))
    gk2 = jnp.tile(k_norm, (1, LANES // HEAD_DIM))

    z, xn, w_in_g = _in_proj(x2, mix_norm, w_sh[0])
    (o, lse, ag), (w_glu_g, w_out_g, w_g_g, w_p_g) = _attn_fwd(z, gq2, gk2, nb, seq, w_sh[1:])
    w_glu_f = w_glu_g.reshape(SSM_W, SSM_W)
    w_out_f = w_out_g.reshape(D_MODEL, D_MODEL)
    w_g_f = w_g_g.reshape(D_MODEL, D_MODEL)
    x_re, x_im, y, sg = _ssm_fwd(z, a_re, a_im, bb_re, bb_im, cc_re, cc_im, d_skip, w_glu_f, b_glu, nb, seq)
    dmix, dh1, loss_t, d_ple, dw_out, dw_g, dw_p = _tail(x2, tg2, ag, sg, p2, w_out_f, w_g_f, w_p_g, ple_norm)

    early_parts = [dw_out.reshape(N_DEV, D_MODEL // N_DEV, D_MODEL), dw_g.reshape(N_DEV, D_MODEL // N_DEV, D_MODEL),
                   dw_p]
    (dqh, dkh, dvb, dga), (g_out, g_g, g_p) = _attn_bwd(z, gq2, gk2, o, lse, dmix, nb, seq, early_parts)
    (du, dgs, dw_glu, d_bglu, d_dskip, da_re, da_im, dbb_re, dbb_im, dcc_re, dcc_im) = _ssm_bwd(
        z, dmix, y, x_re, x_im, a_re, a_im, bb_re, bb_im, cc_re, cc_im, d_skip, w_glu_f, b_glu, nb, seq)
    dz, gx, d_mix, d_gq, d_gk = _dz_and_dx(x2, z, dqh, dkh, dvb, dga, du, dgs, dh1, w_in_g, mix_norm, gq_t, gk_t,
                                           ones_bd, fold)
    d_lr, d_li, d_dt, d_btr, d_bti, d_cr, d_ci = _zoh_bwd(
        lr3, li3, dt3, btr, bti, da_re.reshape(g3), da_im.reshape(g3), dbb_re, dbb_im, dcc_re, dcc_im, fold)
    small_g = {
        "mix_norm": d_mix, "q_norm": d_gq[0:1], "k_norm": d_gk[0:1], "lambda_re": d_lr, "lambda_im": d_li,
        "log_dt": d_dt, "b_re": d_btr, "b_im": d_bti, "c_re": d_cr, "c_im": d_ci,
        "d_skip": d_dskip, "b_glu": d_bglu, "ple_norm": d_ple}

    swapped = ("b_re", "b_im")

    def to_own(n, a):
        a = a.reshape(a.shape[1:]) if a.ndim > 2 else a
        return a.transpose(0, 2, 1) if n in swapped else a

    def from_own(n, a):
        a = a.transpose(0, 2, 1) if n in swapped else a
        return a.reshape(w[n].shape)

    own = [to_own(n, w[n]).shape for n in SMALL]
    g_in, g_glu, gathered = _dw_in(xn, dz, dw_glu.reshape(N_DEV, SSM_W // N_DEV, SSM_W),
                                   [small_g[n].reshape(s) for n, s in zip(SMALL, own)] + [loss_t])
    g_sh = [g_in, g_glu, g_out, g_g, g_p]
    d_sh, m_sh, v_sh = _adamw_shards(g_sh, w_sh, [m[n].reshape(shard2d[n]) for n in BIG],
                                     [v[n].reshape(shard2d[n]) for n in BIG])

    *g_small, loss_sum = _small_sum(list(gathered))
    d_small, m_small, v_small = _adamw_small(
        g_small, *[[to_own(n, src[n]) for n in SMALL] for src in (w, m, v)])

    grads, deltas, new_m, new_v = {}, {}, {}, {}
    for dst, arrs in ((grads, g_small), (deltas, d_small), (new_m, m_small), (new_v, v_small)):
        for n, a in zip(SMALL, arrs):
            dst[n] = from_own(n, a)
    for i, n in enumerate(BIG):
        grads[n] = g_sh[i].reshape(w[n].shape)
        deltas[n] = d_sh[i].reshape(w[n].shape)
        new_m[n] = m_sh[i].reshape(w[n].shape)
        new_v[n] = v_sh[i].reshape(w[n].shape)

    loss = loss_sum[0, 0]
    return (loss, gx.reshape(x.shape), *[grads[n] for n in WEIGHTS], *[deltas[n] for n in WEIGHTS],
            *[new_m[n] for n in WEIGHTS], *[new_v[n] for n in WEIGHTS])
```

```python
import math

import numpy as np
import jax
import jax.numpy as jnp
from jax import lax
from jax.experimental import pallas as pl
from jax.experimental.pallas import tpu as pltpu

F32 = jnp.float32
BF16 = jnp.bfloat16
MESH = pl.DeviceIdType.MESH
AXES = ("x", "y", "c")
N_DEV = 8

D_MODEL = 1024
HEAD_DIM = 64
ATTN_W = 512
SSM_W = 512
SSM_GROUPS = 32
SSM_GROUP = 16
SSM_STATE = 64
N_STATE = SSM_GROUPS * SSM_STATE
PLE_DIM = 256
IN_W = 3072
COL_W = IN_W // N_DEV
DILATED = ((128, 1), (512, 4), (2048, 16))
EPS = 1e-6
INV_SQRT2 = 1.0 / math.sqrt(2.0)
INV_SQRT_2PI = 1.0 / math.sqrt(2.0 * math.pi)

ADAM_LR, ADAM_B1, ADAM_B2, ADAM_EPS, ADAM_WD, ADAM_STEP = 0.001, 0.9, 0.999, 1e-08, 0.01, 10

VMEM_LIMIT_V7X = 56 * 1024 * 1024
SUBLANES = 8
LANES = 128


def _params(n_axes=None, vmem=None):
    kw = {}
    if n_axes:
        kw["dimension_semantics"] = ("arbitrary",) * n_axes
    if vmem:
        kw["vmem_limit_bytes"] = vmem
    return pltpu.CompilerParams(**kw)


def _dot(a, b):
    return jnp.dot(a, b, preferred_element_type=F32)


def _dot_nt(a, b):
    return lax.dot_general(a, b, (((1,), (1,)), ((), ())), preferred_element_type=F32)


def _dot_tn(a, b):
    return lax.dot_general(a, b, (((0,), (0,)), ((), ())), preferred_element_type=F32)


def _hdot(a, ones):
    hi = a.astype(BF16)
    lo = (a - hi.astype(F32)).astype(BF16)
    return _dot(hi, ones) + _dot(lo, ones)


def _sig(x):
    return 1.0 / (1.0 + jnp.exp(-x))


def _gelu_and_grad(y):
    cdf = 0.5 * (1.0 + lax.erf(y * INV_SQRT2))
    pdf = jnp.exp(-0.5 * y * y) * INV_SQRT_2PI
    return y * cdf, cdf + y * pdf


def _vmem():
    return pl.BlockSpec(memory_space=pltpu.VMEM)


def _full(shape):
    nd = len(shape)
    return pl.BlockSpec(shape, lambda *_: (0,) * nd)


class _AllGather:
    def __init__(self, n, cast):
        self.n, self.cast = n, cast

    def scratch(self):
        n = self.n
        return [pltpu.SemaphoreType.DMA((7 * n,)), pltpu.SemaphoreType.DMA((7 * n,)), pltpu.SemaphoreType.DMA((n,))]

    def _plan(self, src_refs, out_refs, sems):
        send_sems, recv_sems, own_sems = sems
        x, y, c = lax.axis_index("x"), lax.axis_index("y"), lax.axis_index("c")
        me, sibling = (x, y, c), (x, y, 1 - c)
        chips = [(1 - x, y), (x, 1 - y), (1 - x, 1 - y)]

        def idx(px, py, pc):
            return 4 * px + 2 * py + pc

        def copy(i, k, block, to, own_src=False):
            ref = out_refs[i].at[idx(*block)]
            return pltpu.make_async_remote_copy(
                src_ref=src_refs[i] if own_src and not self.cast else ref, dst_ref=ref,
                send_sem=send_sems.at[7 * i + k], recv_sem=recv_sems.at[7 * i + k],
                device_id=to, device_id_type=MESH)

        first, passed, arrive_ici, arrive_d2d, own = [], [], [], [], []
        for i in range(self.n):
            first.append(copy(i, 0, me, sibling, own_src=True))
            first += [copy(i, 1 + j, me, (*chip, c), own_src=True) for j, chip in enumerate(chips)]
            arrive_ici += [copy(i, 1 + j, (*chip, c), me) for j, chip in enumerate(chips)]
            passed += [copy(i, 4 + j, (*chip, c), sibling) for j, chip in enumerate(chips)]
            arrive_d2d.append(copy(i, 0, sibling, me))
            arrive_d2d += [copy(i, 4 + j, (*chip, 1 - c), me) for j, chip in enumerate(chips)]
            if not self.cast:
                own.append(pltpu.make_async_copy(src_refs[i], out_refs[i].at[idx(*me)], own_sems.at[i]))
        return idx(*me), first, passed, arrive_ici, arrive_d2d, own

    def start(self, src_refs, out_refs, sems):
        my, first, _, _, _, own = self._plan(src_refs, out_refs, sems)
        if self.cast:
            for i in range(self.n):
                out_refs[i][my] = src_refs[i][...].astype(out_refs[i].dtype)
        for cp in own + first:
            cp.start()

    def forward(self, src_refs, out_refs, sems):
        _, _, passed, arrive_ici, _, _ = self._plan(src_refs, out_refs, sems)
        for cp in arrive_ici:
            cp.wait_recv()
        for cp in passed:
            cp.start()

    def finish(self, src_refs, out_refs, sems):
        _, first, passed, _, arrive_d2d, own = self._plan(src_refs, out_refs, sems)
        for cp in own:
            cp.wait()
        for cp in arrive_d2d:
            cp.wait_recv()
        for cp in first + passed:
            cp.wait_send()


class _HostedGather:
    def __init__(self, shards):
        self.shapes = [(N_DEV,) + a.shape for a in shards]
        self.n = len(shards)
        self.ag = _AllGather(self.n, cast=True)

    def out_shape(self):
        return [jax.ShapeDtypeStruct(s, BF16) for s in self.shapes]

    def scratch(self):
        return [pltpu.VMEM(s, BF16) for s in self.shapes] + self.ag.scratch() + [pltpu.SemaphoreType.DMA((self.n,))]

    def _split(self, scratch):
        return scratch[:self.n], scratch[self.n:-1], scratch[-1]

    def start(self, src_refs, scratch):
        land, sems, _ = self._split(scratch)
        self.ag.start(src_refs, land, sems)

    def forward(self, src_refs, scratch):
        land, sems, _ = self._split(scratch)
        self.ag.forward(src_refs, land, sems)

    def finish(self, src_refs, scratch, out_refs):
        land, sems, out_sems = self._split(scratch)
        self.ag.finish(src_refs, land, sems)
        outs = [pltpu.make_async_copy(land[n], out_refs[n], out_sems.at[n]) for n in range(self.n)]
        for cp in outs:
            cp.start()
        for cp in outs:
            cp.wait()


def _all_gather(shards, out_dtypes, name):
    n = len(shards)
    ag = _AllGather(n, cast=True)

    def body(*refs):
        in_refs, out_refs, sems = refs[:n], refs[n:2 * n], refs[2 * n:]
        ag.start(in_refs, out_refs, sems)
        ag.forward(in_refs, out_refs, sems)
        ag.finish(in_refs, out_refs, sems)

    return pl.pallas_call(
        body, name=name,
        out_shape=[jax.ShapeDtypeStruct((N_DEV,) + s.shape, dt) for s, dt in zip(shards, out_dtypes)],
        in_specs=[_vmem()] * n, out_specs=[_vmem()] * n,
        scratch_shapes=ag.scratch(),
        compiler_params=_params(vmem=VMEM_LIMIT_V7X),
    )(*shards)


def _row_chunks(rows):
    chunk = 64 if rows % 64 == 0 else rows
    return chunk, rows // chunk


class _ReduceScatter:
    def __init__(self, shapes):
        self.shapes = shapes
        self.n = len(shapes)

    def scratch(self, dtype):
        return ([pltpu.VMEM(s, dtype) for s in self.shapes]
                + [pltpu.SemaphoreType.DMA((7 * self.n,)), pltpu.SemaphoreType.DMA((7 * self.n,)),
                   pltpu.SemaphoreType.DMA((self.n,))])

    def _copies(self, in_refs, land_refs, send_sems, recv_sems, own_sems):
        x, y, c = lax.axis_index("x"), lax.axis_index("y"), lax.axis_index("c")
        remote, own = [], []
        for i in range(self.n):
            for m in range(1, N_DEV):
                px = 1 - x if m & 4 else x
                py = 1 - y if m & 2 else y
                pc = 1 - c if m & 1 else c
                remote.append(pltpu.make_async_remote_copy(
                    src_ref=in_refs[i].at[4 * px + 2 * py + pc], dst_ref=land_refs[i].at[m - 1],
                    send_sem=send_sems.at[7 * i + m - 1], recv_sem=recv_sems.at[7 * i + m - 1],
                    device_id=(px, py, pc), device_id_type=MESH))
            own.append(pltpu.make_async_copy(in_refs[i].at[4 * x + 2 * y + c], land_refs[i].at[N_DEV - 1],
                                             own_sems.at[i]))
        return remote, own

    def start(self, in_refs, scratch):
        remote, own = self._copies(in_refs, scratch[:self.n], *scratch[self.n:])
        for cp in remote + own:
            cp.start()

    def finish(self, in_refs, scratch, out_refs):
        land_refs = scratch[:self.n]
        remote, own = self._copies(in_refs, land_refs, *scratch[self.n:])
        for cp in own:
            cp.wait()
        for cp in remote:
            cp.wait_recv()
        for i in range(self.n):
            chunk, steps = _row_chunks(self.shapes[i][1])

            def step(s, carry, i=i, chunk=chunk):
                r = pl.ds(pl.multiple_of(s * chunk, chunk), chunk)
                acc = land_refs[i][N_DEV - 1, r, :].astype(F32)
                for m in range(1, N_DEV):
                    acc = acc + land_refs[i][m - 1, r, :].astype(F32)
                out_refs[i][r, :] = acc
                return carry

            lax.fori_loop(0, steps, step, 0)
        for cp in remote:
            cp.wait_send()


def _reduce_scatter(parts, name):
    n = len(parts)
    rs = _ReduceScatter([p.shape for p in parts])

    def body(*refs):
        in_refs, out_refs, scratch = refs[:n], refs[n:2 * n], refs[2 * n:]
        rs.start(in_refs, scratch)
        rs.finish(in_refs, scratch, out_refs)

    return pl.pallas_call(
        body, name=name,
        out_shape=[jax.ShapeDtypeStruct(p.shape[1:], F32) for p in parts],
        in_specs=[_vmem()] * n, out_specs=[_vmem()] * n,
        scratch_shapes=rs.scratch(parts[0].dtype),
        compiler_params=_params(vmem=VMEM_LIMIT_V7X),
    )(*parts)


def _adamw_math(w, g, m, v):
    m = ADAM_B1 * m + (1.0 - ADAM_B1) * g
    v = ADAM_B2 * v + (1.0 - ADAM_B2) * (g * g)
    m_hat = m / (1.0 - ADAM_B1 ** ADAM_STEP)
    v_hat = v / (1.0 - ADAM_B2 ** ADAM_STEP)
    delta = -ADAM_LR * (m_hat / (jnp.sqrt(v_hat) + ADAM_EPS) + ADAM_WD * w)
    return delta, m, v


def _adamw_shards(gs, ws, ms, vs):
    n = len(gs)

    def body(*refs):
        g_refs, w_refs, m_refs, v_refs = (refs[k * n:(k + 1) * n] for k in range(4))
        d_out, m_out, v_out = (refs[(4 + k) * n:(5 + k) * n] for k in range(3))
        for i in range(n):
            chunk, steps = _row_chunks(gs[i].shape[0])

            def step(s, carry, i=i, chunk=chunk):
                r = pl.ds(pl.multiple_of(s * chunk, chunk), chunk)
                d, m, v = _adamw_math(w_refs[i][r, :], g_refs[i][r, :], m_refs[i][r, :], v_refs[i][r, :])
                d_out[i][r, :] = d
                m_out[i][r, :] = m
                v_out[i][r, :] = v
                return carry

            lax.fori_loop(0, steps, step, 0)

    shapes = [jax.ShapeDtypeStruct(g.shape, F32) for g in gs]
    outs = pl.pallas_call(
        body, name="adamw_shards", out_shape=shapes * 3,
        in_specs=[_vmem()] * (4 * n), out_specs=[_vmem()] * (3 * n),
        compiler_params=_params(vmem=VMEM_LIMIT_V7X),
    )(*gs, *ws, *ms, *vs)
    return outs[:n], outs[n:2 * n], outs[2 * n:]


def _small_sum(gathered):
    n = len(gathered)

    def body(*refs):
        ga_refs, out_refs = refs[:n], refs[n:]
        for i in range(n):
            def total(idx, i=i):
                g = ga_refs[i][(0,) + idx]
                for j in range(1, N_DEV):
                    g = g + ga_refs[i][(j,) + idx]
                out_refs[i][idx] = g

            if len(gathered[i].shape) == 4:
                def step(s, carry, total=total):
                    total((s,))
                    return carry

                lax.fori_loop(0, gathered[i].shape[1], step, 0)
            else:
                total((Ellipsis,))

    return pl.pallas_call(
        body, name="small_sum", out_shape=[jax.ShapeDtypeStruct(g.shape[1:], F32) for g in gathered],
        in_specs=[_vmem()] * n, out_specs=[_vmem()] * n,
        compiler_params=_params(vmem=VMEM_LIMIT_V7X),
    )(*gathered)


def _adamw_small(gs, ws, ms, vs):
    n = len(gs)

    def body(*refs):
        g_refs, w_refs, m_refs, v_refs = (refs[k * n:(k + 1) * n] for k in range(4))
        d_out, m_out, v_out = (refs[(4 + k) * n:(5 + k) * n] for k in range(3))
        for i in range(n):
            def update(idx, i=i):
                d, mm, vv = _adamw_math(w_refs[i][idx], g_refs[i][idx], m_refs[i][idx], v_refs[i][idx])
                d_out[i][idx] = d
                m_out[i][idx] = mm
                v_out[i][idx] = vv

            if len(gs[i].shape) == 3:
                def step(s, carry, update=update):
                    update(s)
                    return carry

                lax.fori_loop(0, gs[i].shape[0], step, 0)
            else:
                update(Ellipsis)

    shapes = [jax.ShapeDtypeStruct(g.shape, F32) for g in gs]
    outs = pl.pallas_call(
        body, name="adamw_small", out_shape=shapes * 3,
        in_specs=[_vmem()] * (4 * n), out_specs=[_vmem()] * (3 * n),
        compiler_params=_params(vmem=VMEM_LIMIT_V7X),
    )(*gs, *ws, *ms, *vs)
    return outs[:n], outs[n:2 * n], outs[2 * n:]


def _zoh(lr, li, logdt, btr, bti):
    dt = jnp.exp(logdt)
    mag = jnp.exp(lr * dt)
    th = li * dt
    ar = mag * jnp.cos(th)
    ai = mag * jnp.sin(th)
    den = lr * lr + li * li
    nr = ar - 1.0
    cr = (nr * lr + ai * li) / den
    ci = (ai * lr - nr * li) / den
    return ar, ai, cr * btr - ci * bti, cr * bti + ci * btr


BD_GROUPS = 8
BD_ROWS = BD_GROUPS * SSM_GROUP
BD_COLS = BD_GROUPS * SSM_STATE
N_BD = SSM_GROUPS // BD_GROUPS


def _bd_mask():
    r = lax.broadcasted_iota(jnp.int32, (BD_ROWS, BD_COLS), 0) // SSM_GROUP
    c = lax.broadcasted_iota(jnp.int32, (BD_ROWS, BD_COLS), 1) // SSM_STATE
    return r == c


def _blockdiag_store(out_ref, t):
    mask = _bd_mask()
    for j in range(N_BD):
        rows = t[j * BD_GROUPS:(j + 1) * BD_GROUPS].reshape(BD_ROWS, SSM_STATE)
        out_ref[j] = jnp.where(mask, jnp.tile(rows, (1, BD_GROUPS)), 0.0).astype(out_ref.dtype)


def _blockdiag_load(m_ref, fold):
    mask = _bd_mask()
    parts = [_hdot(jnp.where(mask, m_ref[j], 0.0), fold).reshape(BD_GROUPS, SSM_GROUP, SSM_STATE)
             for j in range(N_BD)]
    return jnp.concatenate(parts, axis=0)


def _zoh_fwd(lr, li, logdt, btr, bti, c_re, c_im):
    def body(lr_ref, li_ref, dt_ref, br_ref, bi_ref, cr_ref, ci_ref, ar_ref, ai_ref, bbr_ref, bbi_ref, ccr_ref,
             cci_ref):
        ar, ai, bbr, bbi = _zoh(lr_ref[...], li_ref[...], dt_ref[...], br_ref[...], bi_ref[...])
        ar_ref[...] = ar
        ai_ref[...] = ai
        _blockdiag_store(bbr_ref, bbr)
        _blockdiag_store(bbi_ref, bbi)
        _blockdiag_store(ccr_ref, cr_ref[...])
        _blockdiag_store(cci_ref, ci_ref[...])

    s = jax.ShapeDtypeStruct
    bd = s((N_BD, BD_ROWS, BD_COLS), BF16)
    return pl.pallas_call(
        body, name="zoh_fwd", out_shape=[s(lr.shape, F32), s(lr.shape, F32), bd, bd, bd, bd],
        in_specs=[_vmem()] * 7, out_specs=[_vmem()] * 6,
    )(lr, li, logdt, btr, bti, c_re, c_im)


def _zoh_bwd(lr, li, logdt, btr, bti, dar, dai, dbb_re, dbb_im, dcc_re, dcc_im, fold):
    def body(lr_ref, li_ref, dt_ref, br_ref, bi_ref, dar_ref, dai_ref, dbbr_ref, dbbi_ref, dccr_ref, dcci_ref,
             fold_ref, glr_ref, gli_ref, gdt_ref, gbr_ref, gbi_ref, gcr_ref, gci_ref):
        fold_m = fold_ref[...]
        _, vjp = jax.vjp(_zoh, lr_ref[...], li_ref[...], dt_ref[...], br_ref[...], bi_ref[...])
        glr, gli, gdt, gbr, gbi = vjp((dar_ref[...], dai_ref[...], _blockdiag_load(dbbr_ref, fold_m),
                                       _blockdiag_load(dbbi_ref, fold_m)))
        glr_ref[...] = glr
        gli_ref[...] = gli
        gdt_ref[...] = gdt
        gbr_ref[...] = gbr
        gbi_ref[...] = gbi
        gcr_ref[...] = _blockdiag_load(dccr_ref, fold_m)
        gci_ref[...] = _blockdiag_load(dcci_ref, fold_m)

    s = jax.ShapeDtypeStruct
    return pl.pallas_call(
        body, name="zoh_bwd",
        out_shape=[s(lr.shape, F32), s(lr.shape, F32), s(logdt.shape, F32)] + [s(btr.shape, F32)] * 4,
        in_specs=[_vmem()] * 12, out_specs=[_vmem()] * 7,
    )(lr, li, logdt, btr, bti, dar, dai, dbb_re, dbb_im, dcc_re, dcc_im, fold)


def _head_ones():
    r = np.arange(ATTN_W) // HEAD_DIM
    return jnp.asarray(r[:, None] == r[None, :], dtype=BF16)


def _head_fold():
    return jnp.asarray(np.tile(np.eye(HEAD_DIM), (ATTN_W // HEAD_DIM, 1)), dtype=BF16)


def _in_proj(x2, g_mix, w_in_sh):
    t_tok = x2.shape[0]
    tm = min(1024, t_tok)
    nt = t_tok // tm
    ag_w = _AllGather(1, cast=True)
    n_sem = len(ag_w.scratch())

    def owner(i):
        x, y, c = lax.axis_index("x"), lax.axis_index("y"), lax.axis_index("c")
        rel = i // 2
        px = jnp.where((rel == 1) | (rel == 3), 1 - x, x)
        py = jnp.where((rel == 2) | (rel == 3), 1 - y, y)
        pc = jnp.where(i % 2 == 1, 1 - c, c)
        return 4 * px + 2 * py + pc

    def body(*refs):
        x_ref, g_ref, w_ref, z_ref, xn_ref, wg_ref, xn_scr, w_land = refs[:8]
        sems_w, out_sem = refs[8:8 + n_sem], refs[8 + n_sem]
        i, t = pl.program_id(0), pl.program_id(1)
        _, first, passed, arrive_ici, arrive_d2d, _ = ag_w._plan([w_ref], [w_land], sems_w)

        @pl.when((i == 0) & (t == 0))
        def _():
            ag_w.start([w_ref], [w_land], sems_w)

        @pl.when((i == 1) & (t == 0))
        def _():
            arrive_d2d[0].wait_recv()

        for n in range(3):
            @pl.when((i == 2 + 2 * n) & (t == 0))
            def _(n=n):
                arrive_ici[n].wait_recv()
                passed[n].start()

            @pl.when((i == 3 + 2 * n) & (t == 0))
            def _(n=n):
                arrive_d2d[1 + n].wait_recv()

        @pl.when(i == 0)
        def _():
            x = x_ref[...]
            r = lax.rsqrt(jnp.mean(x * x, axis=-1, keepdims=True) + EPS)
            xn = (x * r * g_ref[...]).astype(BF16)
            xn_ref[...] = xn
            xn_scr[t] = xn

        z_ref[...] = _dot(xn_scr[t], w_land[owner(i)])

        @pl.when((i == N_DEV - 1) & (t == nt - 1))
        def _():
            for cp in first + passed:
                cp.wait_send()
            out = pltpu.make_async_copy(w_land, wg_ref, out_sem)
            out.start()
            out.wait()

    s = jax.ShapeDtypeStruct
    xmap = lambda i, t: (jnp.where(i == 0, t, nt - 1), 0)
    gathered = s((N_DEV,) + w_in_sh.shape, BF16)
    return pl.pallas_call(
        body, name="in_proj", grid=(N_DEV, nt),
        out_shape=[s((t_tok, IN_W), F32), s((t_tok, D_MODEL), BF16), gathered],
        in_specs=[pl.BlockSpec((tm, D_MODEL), xmap), _full(g_mix.shape), _full(w_in_sh.shape)],
        out_specs=[pl.BlockSpec((tm, COL_W), lambda i, t: (t, owner(i))), pl.BlockSpec((tm, D_MODEL), xmap),
                   pl.BlockSpec(memory_space=pl.ANY)],
        scratch_shapes=[pltpu.VMEM((nt, tm, D_MODEL), BF16), pltpu.VMEM(gathered.shape, BF16)] + ag_w.scratch()
        + [pltpu.SemaphoreType.DMA],
        compiler_params=_params(2, VMEM_LIMIT_V7X),
    )(x2, g_mix, w_in_sh)


TQ = 128
NEG = -1e30


def _head_col(t, lm):
    return jnp.max(jnp.where(lm, t, NEG), axis=-1, keepdims=True)


def _head_masks():
    lane = lax.broadcasted_iota(jnp.int32, (1, 1, LANES), 2)
    return [(lane // HEAD_DIM) == h for h in range(LANES // HEAD_DIM)]


def _stack_heads(t3, lms):
    return jnp.concatenate([jnp.where(lm, t3, jnp.zeros_like(t3)) for lm in lms], axis=1)


def _unstack_heads(t2, lms, tq):
    out = t2[:, :tq]
    for h in range(1, len(lms)):
        out = jnp.where(lms[h], t2[:, h * tq:(h + 1) * tq], out)
    return out


def _gather_classes(ref, dil, nt, tq, dtype):
    length = nt * tq
    if dil == 1:
        return ref[...].astype(dtype).reshape(nt, tq, LANES)
    parts = [ref[pl.ds(r, length, stride=dil), :].astype(dtype).reshape(nt, tq, LANES) for r in range(dil)]
    return jnp.concatenate(parts, axis=0)


def _scatter_classes(ref, val, dil, nt, tq, add):
    length = nt * tq
    for r in range(dil):
        rows = pl.ds(r, length, stride=dil) if dil > 1 else slice(None)
        part = val[r * nt:(r + 1) * nt].reshape(length, LANES)
        ref[rows, :] = ref[rows, :] + part if add else part


def _with_prev_tile(t3, dil, nt):
    parts = []
    for r in range(dil):
        t = t3[r * nt:(r + 1) * nt]
        parts.append(jnp.concatenate([t[:1], t[:-1]], axis=0))
    prev = parts[0] if dil == 1 else jnp.concatenate(parts, axis=0)
    return jnp.concatenate([prev, t3], axis=1)


def _band_valid(dil, nt, tq):
    if nt == 1:
        shape = (dil, tq, tq)
        return lax.broadcasted_iota(jnp.int32, shape, 1) >= lax.broadcasted_iota(jnp.int32, shape, 2)
    shape = (dil * nt, tq, 2 * tq)
    b = lax.broadcasted_iota(jnp.int32, shape, 0)
    c = lax.broadcasted_iota(jnp.int32, shape, 2)
    d = tq + lax.broadcasted_iota(jnp.int32, shape, 1) - c
    return (d >= 0) & (d <= tq) & (((b & (nt - 1)) != 0) | (c >= tq))


def _window_tiling(seq, window, dil):
    length = seq // dil
    tq = min(TQ, length)
    nt = length // tq
    assert length % tq == 0 and nt & (nt - 1) == 0 and (nt == 1 or window == tq * dil)
    return nt, tq


def _bqk(a, b):
    return jnp.einsum("bqd,bkd->bqk", a, b, preferred_element_type=F32)


def _bqd(a, b):
    return jnp.einsum("bqk,bkd->bqd", a, b, preferred_element_type=F32)


def _bkd(a, b):
    return jnp.einsum("bqk,bqd->bkd", a, b, preferred_element_type=F32)


def _qk_hat(q_ref, k_ref, gq_ref, gk_ref):
    lane = lax.broadcasted_iota(jnp.int32, (1, LANES), 1)

    def norm(raw, gain, scale):
        sq = raw * raw
        r = jnp.zeros_like(raw)
        for h in range(LANES // HEAD_DIM):
            lm = (lane // HEAD_DIM) == h
            ms = jnp.sum(jnp.where(lm, sq, 0.0), axis=-1, keepdims=True) * (1.0 / HEAD_DIM)
            r = jnp.where(lm, lax.rsqrt(ms + EPS), r)
        return raw * r * gain * scale

    return norm(q_ref[...], gq_ref[...], HEAD_DIM ** -0.5), norm(k_ref[...], gk_ref[...], 1.0)


def _zblock(seq, group):
    return pl.BlockSpec((seq, LANES), lambda b, hp: (b, group * (ATTN_W // LANES) + hp))


def _attn_fwd(z, gq2, gk2, nb, seq, late_sh):
    t_tok = nb * seq
    n_win = len(DILATED)
    host = _HostedGather(late_sh)
    n_late = host.n
    n_steps = (nb, ATTN_W // LANES)

    def body(*refs):
        (q_ref, k_ref, v_ref, ga_ref, gq_ref, gk_ref), refs = refs[:6], refs[6:]
        late_refs, refs = refs[:n_late], refs[n_late:]
        (o_ref, l_ref, ag_ref), refs = refs[:3], refs[3:]
        lateg_refs, refs = refs[:n_late], refs[n_late:]
        (qf, kf, oc, lc), host_scratch = refs[:4], refs[4:]
        step = pl.program_id(0) * n_steps[1] + pl.program_id(1)
        total = n_steps[0] * n_steps[1]

        @pl.when(step == 0)
        def _():
            host.start(late_refs, host_scratch)

        @pl.when(step == total // 2)
        def _():
            host.forward(late_refs, host_scratch)

        qf[...], kf[...] = _qk_hat(q_ref, k_ref, gq_ref, gk_ref)
        lms = _head_masks()
        for w, (window, dil) in enumerate(DILATED):
            nt, tq = _window_tiling(seq, window, dil)
            q3 = _gather_classes(qf, dil, nt, tq, BF16)
            k3 = _gather_classes(kf, dil, nt, tq, BF16)
            v3 = _gather_classes(v_ref, dil, nt, tq, BF16)
            if nt > 1:
                k3, v3 = _with_prev_tile(k3, dil, nt), _with_prev_tile(v3, dil, nt)
            valid = _band_valid(dil, nt, tq)
            valid = jnp.concatenate([valid] * len(lms), axis=1)
            s = _bqk(_stack_heads(q3, lms), k3)
            m = jnp.max(jnp.where(valid, s, NEG), axis=-1, keepdims=True)
            p = jnp.where(valid, jnp.exp(s - m), 0.0)
            den = jnp.sum(p, axis=-1, keepdims=True)
            o = _unstack_heads(_bqd(p.astype(BF16), v3) / den, lms, tq)
            lse = _unstack_heads(jnp.broadcast_to(m + jnp.log(den), s.shape[:2] + (LANES,)), lms, tq)
            _scatter_classes(oc.at[w], o, dil, nt, tq, add=False)
            _scatter_classes(lc.at[w], lse, dil, nt, tq, add=False)
        mx = lc[0]
        for w in range(1, n_win):
            mx = jnp.maximum(mx, lc[w])
        tot = jnp.zeros_like(mx)
        o = jnp.zeros_like(mx)
        for w in range(n_win):
            e = jnp.exp(lc[w] - mx)
            tot = tot + e
            o = o + e * oc[w]
        o = o / tot
        o_ref[...] = o
        l_ref[...] = mx + jnp.log(tot)
        ga = ga_ref[...]
        ag_ref[...] = (o * ga * _sig(ga)).astype(BF16)

        @pl.when(step == total - 1)
        def _():
            host.finish(late_refs, host_scratch, lateg_refs)

    blk = pl.BlockSpec((seq, LANES), lambda b, hp: (b, hp))
    s = jax.ShapeDtypeStruct
    outs = pl.pallas_call(
        body, name="attn_fwd", grid=n_steps,
        out_shape=[s((t_tok, ATTN_W), F32), s((t_tok, ATTN_W), F32), s((t_tok, ATTN_W), BF16)] + host.out_shape(),
        in_specs=[_zblock(seq, 0), _zblock(seq, 1), _zblock(seq, 2), _zblock(seq, 3), _full(gq2.shape),
                  _full(gk2.shape)] + [_full(a.shape) for a in late_sh],
        out_specs=[blk, blk, blk] + [pl.BlockSpec(memory_space=pl.ANY)] * n_late,
        scratch_shapes=[pltpu.VMEM((seq, LANES), F32)] * 2 + [pltpu.VMEM((n_win, seq, LANES), F32)] * 2
        + host.scratch(),
        compiler_params=_params(2, VMEM_LIMIT_V7X),
    )(z, z, z, z, gq2, gk2, *late_sh)
    return outs[:3], outs[3:]


SCAN_COLS = 512


def _to_segments(dst_ref, val):
    seg = val.shape[0] // SUBLANES
    for n in range(dst_ref.shape[0]):
        for s in range(SUBLANES):
            dst_ref[n, pl.ds(s, seg, stride=SUBLANES), :] = val[s * seg:(s + 1) * seg, n * LANES:(n + 1) * LANES]


def _from_segments(src_ref):
    seg = src_ref.shape[1] // SUBLANES
    return jnp.concatenate(
        [jnp.concatenate([src_ref[n, pl.ds(s, seg, stride=SUBLANES), :] for s in range(SUBLANES)], axis=0)
         for n in range(src_ref.shape[0])], axis=1)


def _scan_chunk(re_ref, im_ref, a_re_ref, a_im_ref, carry_re, carry_im, rows, reverse, visit=None):
    seg = rows // SUBLANES
    assert seg & (seg - 1) == 0
    rowi = lax.broadcasted_iota(jnp.int32, (SUBLANES, SCAN_COLS), 0)
    edge = (SUBLANES - 1) if reverse else 0
    last = 0 if reverse else SUBLANES - 1
    at_edge = rowi == edge

    def cmul(ar, ai, br, bi):
        return ar * br - ai * bi, ar * bi + ai * br

    for c0 in range(0, N_STATE, SCAN_COLS):
        cols = slice(c0, c0 + SCAN_COLS)
        a1r = jnp.broadcast_to(a_re_ref[:, cols], (SUBLANES, SCAN_COLS))
        a1i = jnp.broadcast_to(a_im_ref[:, cols], (SUBLANES, SCAN_COLS))
        if reverse:
            a1i = -a1i

        def block_of(i):
            j = (seg - 1 - i) if reverse else i
            return j, pl.ds(pl.multiple_of(j * SUBLANES, SUBLANES), SUBLANES)

        def local(i, carry, cols=cols, a1r=a1r, a1i=a1i):
            xr, xi = carry
            _, blk = block_of(i)
            nr, ni = cmul(a1r, a1i, xr, xi)
            xr, xi = nr + re_ref[blk, cols], ni + im_ref[blk, cols]
            re_ref[blk, cols] = xr
            im_ref[blk, cols] = xi
            return xr, xi

        zero = jnp.zeros((SUBLANES, SCAN_COLS), F32)
        er, ei = lax.fori_loop(0, seg, local, (zero, zero))

        pr, pi = a1r, a1i
        for _ in range(seg.bit_length() - 1):
            pr, pi = cmul(pr, pi, pr, pi)
        cr, ci = carry_re[:, cols], carry_im[:, cols]
        inr, ini = cmul(pr, pi, cr, ci)
        er = er + jnp.where(at_edge, inr, 0.0)
        ei = ei + jnp.where(at_edge, ini, 0.0)
        for sft in (1, 2, 4):
            shift, keep = (SUBLANES - sft, rowi < SUBLANES - sft) if reverse else (sft, rowi >= sft)
            rs = jnp.where(keep, pltpu.roll(er, shift, 0), 0.0)
            ims = jnp.where(keep, pltpu.roll(ei, shift, 0), 0.0)
            dr, di = cmul(pr, pi, rs, ims)
            er, ei = er + dr, ei + di
            pr, pi = cmul(pr, pi, pr, pi)
        carry_re[:, cols] = jnp.broadcast_to(er[last:last + 1, :], (SUBLANES, SCAN_COLS))
        carry_im[:, cols] = jnp.broadcast_to(ei[last:last + 1, :], (SUBLANES, SCAN_COLS))
        one = (SUBLANES - 1) if reverse else 1
        kr = jnp.where(at_edge, cr, pltpu.roll(er, one, 0))
        ki = jnp.where(at_edge, ci, pltpu.roll(ei, one, 0))

        def fix(i, carry, cols=cols, a1r=a1r, a1i=a1i):
            kr, ki, acc = carry
            j, blk = block_of(i)
            kr, ki = cmul(a1r, a1i, kr, ki)
            xr, xi = re_ref[blk, cols] + kr, im_ref[blk, cols] + ki
            re_ref[blk, cols] = xr
            im_ref[blk, cols] = xi
            if visit is not None:
                acc = visit(cols, j, xr, xi, acc)
            return kr, ki, acc

        _, _, acc = lax.fori_loop(0, seg, fix, (kr, ki, (zero, zero)))
        if visit is not None:
            visit(cols, None, None, None, acc)


SSM_CHUNK = 512


def _ssm_fwd(z, a_re, a_im, bb_re, bb_im, cc_re, cc_im, d_skip, w_glu, b_glu, nb, seq):
    t_tok = nb * seq
    tc = min(SSM_CHUNK, seq)
    nch = seq // tc
    grp = N_STATE // 4

    def body(u_ref, gs_ref, ar_ref, ai_ref, bbr_ref, bbi_ref, ccr_ref, cci_ref, d_ref, wg_ref, bg_ref,
             xr_ref, xi_ref, y_ref, sg_ref, car_re, car_im, seg_u, seg_y):
        @pl.when(pl.program_id(1) == 0)
        def _():
            car_re[...] = jnp.zeros_like(car_re)
            car_im[...] = jnp.zeros_like(car_im)

        u = u_ref[...]
        _to_segments(seg_u, u)
        for j in range(4):
            uj = seg_u[j].astype(BF16)
            xr_ref[:, j * grp:(j + 1) * grp] = _dot(uj, bbr_ref[j])
            xi_ref[:, j * grp:(j + 1) * grp] = _dot(uj, bbi_ref[j])
        _scan_chunk(xr_ref, xi_ref, ar_ref, ai_ref, car_re, car_im, tc, reverse=False)
        for j in range(4):
            xr = xr_ref[:, j * grp:(j + 1) * grp].astype(BF16)
            xi = xi_ref[:, j * grp:(j + 1) * grp].astype(BF16)
            seg_y[j] = _dot_nt(xr, ccr_ref[j]) - _dot_nt(xi, cci_ref[j])
        y = _from_segments(seg_y) + d_ref[...] * u
        y_ref[...] = y
        yg, _ = _gelu_and_grad(y)
        gl = _dot(yg.astype(BF16), wg_ref[...]) + bg_ref[...]
        gs = gs_ref[...]
        sg_ref[...] = (yg * _sig(gl) * gs * _sig(gs)).astype(BF16)

    umap = lambda b, ch: (b * nch + ch, 4)
    gmap = lambda b, ch: (b * nch + ch, 5)
    row = lambda b, ch: (b * nch + ch, 0)
    s = jax.ShapeDtypeStruct
    consts = [a_re, a_im, bb_re, bb_im, cc_re, cc_im, d_skip, w_glu, b_glu]
    return pl.pallas_call(
        body, name="ssm_fwd", grid=(nb, nch),
        out_shape=[s((t_tok, N_STATE), F32), s((t_tok, N_STATE), F32), s((t_tok, SSM_W), F32),
                   s((t_tok, SSM_W), BF16)],
        in_specs=[pl.BlockSpec((tc, SSM_W), umap), pl.BlockSpec((tc, SSM_W), gmap)] + [_full(c.shape) for c in consts],
        out_specs=[pl.BlockSpec((tc, N_STATE), row), pl.BlockSpec((tc, N_STATE), row),
                   pl.BlockSpec((tc, SSM_W), row), pl.BlockSpec((tc, SSM_W), row)],
        scratch_shapes=[pltpu.VMEM((SUBLANES, N_STATE), F32), pltpu.VMEM((SUBLANES, N_STATE), F32),
                        pltpu.VMEM((4, tc, LANES), F32), pltpu.VMEM((4, tc, LANES), F32)],
        compiler_params=_params(2, VMEM_LIMIT_V7X),
    )(z, z, *consts)


def _tail(x2, tg2, ag, sg, p2, w_out, w_g, w_p, g_ple):
    t_tok = x2.shape[0]
    tm = min(512, t_tok)
    nt = t_tok // tm
    half = ATTN_W

    def body(x_ref, tg_ref, ag_ref, sg_ref, p_ref, wo_ref, wg_ref, wp_ref, gp_ref,
             dmix_ref, dh1_ref, loss_ref, dgp_ref, dwo_ref, dwg_ref, dwp_ref, acc_o, acc_g, acc_p):
        i = pl.program_id(0)

        @pl.when(i == 0)
        def _():
            loss_ref[...] = jnp.zeros_like(loss_ref)
            dgp_ref[...] = jnp.zeros_like(dgp_ref)
            acc_o[...] = jnp.zeros_like(acc_o)
            acc_g[...] = jnp.zeros_like(acc_g)
            acc_p[...] = jnp.zeros_like(acc_p)

        ag_t, sg_t = ag_ref[...], sg_ref[...]
        h1 = x_ref[...] + _dot(ag_t, wo_ref[0:half, :]) + _dot(sg_t, wo_ref[half:2 * half, :])
        r2 = lax.rsqrt(jnp.mean(h1 * h1, axis=-1, keepdims=True) + EPS)
        hnorm = h1 * r2
        gp = gp_ref[...]
        hn = (hnorm * gp).astype(BF16)
        gate = _sig(_dot(hn, wg_ref[...]))
        pb = p_ref[...].astype(BF16)
        pp = jnp.concatenate([_dot(pb, wp_ref[j]) for j in range(N_DEV)], axis=-1)
        h2 = h1 + gate * pp
        err = h2 - tg_ref[...]
        loss_ref[...] += 0.5 * jnp.sum(err * err) * (1.0 / D_MODEL)
        dh2 = err * (1.0 / D_MODEL)
        dpp = (dh2 * gate).astype(BF16)
        dgpre = (dh2 * pp * gate * (1.0 - gate)).astype(BF16)
        acc_p[...] += _dot_tn(pb, dpp)
        acc_g[...] += _dot_tn(hn, dgpre)
        dhn = _dot_nt(dgpre, wg_ref[...])
        dgp_ref[...] += jnp.sum(dhn * hnorm, axis=0, keepdims=True)
        a = dhn * gp
        dh1 = dh2 + r2 * (a - hnorm * jnp.mean(a * hnorm, axis=-1, keepdims=True))
        dh1_ref[...] = dh1
        dh1b = dh1.astype(BF16)
        acc_o[0:half, :] += _dot_tn(ag_t, dh1b)
        acc_o[half:2 * half, :] += _dot_tn(sg_t, dh1b)
        dmix_ref[...] = _dot_nt(dh1b, wo_ref[...])

        @pl.when(i == nt - 1)
        def _():
            dwo_ref[...] = acc_o[...].astype(BF16)
            dwg_ref[...] = acc_g[...].astype(BF16)
            for j in range(N_DEV):
                dwp_ref[j] = acc_p[:, j * LANES:(j + 1) * LANES].astype(BF16)

    row = lambda i: (i, 0)
    s = jax.ShapeDtypeStruct
    return pl.pallas_call(
        body, name="tail_fwd_bwd", grid=(nt,),
        out_shape=[s((t_tok, D_MODEL), F32), s((t_tok, D_MODEL), F32), s((SUBLANES, LANES), F32),
                   s((1, D_MODEL), F32), s((D_MODEL, D_MODEL), BF16), s((D_MODEL, D_MODEL), BF16),
                   s((N_DEV, PLE_DIM, LANES), BF16)],
        in_specs=[pl.BlockSpec((tm, D_MODEL), row), pl.BlockSpec((tm, D_MODEL), row),
                  pl.BlockSpec((tm, half), row), pl.BlockSpec((tm, half), row), pl.BlockSpec((tm, PLE_DIM), row),
                  _full(w_out.shape), _full(w_g.shape), _full(w_p.shape), _full(g_ple.shape)],
        out_specs=[pl.BlockSpec((tm, D_MODEL), row), pl.BlockSpec((tm, D_MODEL), row), _full((SUBLANES, LANES)),
                   _full((1, D_MODEL)), _full((D_MODEL, D_MODEL)), _full((D_MODEL, D_MODEL)),
                   _full((N_DEV, PLE_DIM, LANES))],
        scratch_shapes=[pltpu.VMEM((D_MODEL, D_MODEL), F32), pltpu.VMEM((D_MODEL, D_MODEL), F32),
                        pltpu.VMEM((PLE_DIM, D_MODEL), F32)],
        compiler_params=_params(1, VMEM_LIMIT_V7X),
    )(x2, tg2, ag, sg, p2, w_out, w_g, w_p, g_ple)


def _attn_bwd(z, gq2, gk2, o, lse, dmix, nb, seq, parts):
    t_tok = nb * seq
    n_rs = len(parts)
    rs = _ReduceScatter([p.shape for p in parts])
    n_steps = (nb, ATTN_W // LANES)

    def body(*refs):
        (q_ref, k_ref, v_ref, ga_ref, gq_ref, gk_ref, o_ref, l_ref, da_ref), refs = refs[:9], refs[9:]
        part_refs, refs = refs[:n_rs], refs[n_rs:]
        (dq_ref, dk_ref, dv_ref, dga_ref), refs = refs[:4], refs[4:]
        g_refs, refs = refs[:n_rs], refs[n_rs:]
        (qf, kf, dof, dlf), rs_scratch = refs[:4], refs[4:]
        b, hp = pl.program_id(0), pl.program_id(1)

        @pl.when((b == 0) & (hp == 0))
        def _():
            rs.start(part_refs, rs_scratch)

        ga, o_t, da = ga_ref[...], o_ref[...], da_ref[...]
        sga = _sig(ga)
        d_o = da * ga * sga
        dga_ref[...] = da * o_t * sga * (1.0 + ga * (1.0 - sga))
        lane = lax.broadcasted_iota(jnp.int32, (1, LANES), 1)
        d_oo = d_o * o_t
        delta = jnp.zeros_like(d_oo)
        for h in range(LANES // HEAD_DIM):
            lm2 = (lane // HEAD_DIM) == h
            delta = jnp.where(lm2, jnp.sum(jnp.where(lm2, d_oo, 0.0), axis=-1, keepdims=True), delta)
        qf[...], kf[...] = _qk_hat(q_ref, k_ref, gq_ref, gk_ref)
        dof[...] = d_o
        dlf[...] = delta
        dq_ref[...] = jnp.zeros_like(dq_ref)
        dk_ref[...] = jnp.zeros_like(dk_ref)
        dv_ref[...] = jnp.zeros_like(dv_ref)
        lms = _head_masks()
        for window, dil in DILATED:
            nt, tq = _window_tiling(seq, window, dil)
            q3 = _gather_classes(qf, dil, nt, tq, BF16)
            k3 = _gather_classes(kf, dil, nt, tq, BF16)
            v3 = _gather_classes(v_ref, dil, nt, tq, BF16)
            do3 = _gather_classes(dof, dil, nt, tq, BF16)
            lt3 = _gather_classes(l_ref, dil, nt, tq, F32)
            dl3 = _gather_classes(dlf, dil, nt, tq, F32)
            if nt > 1:
                k3, v3 = _with_prev_tile(k3, dil, nt), _with_prev_tile(v3, dil, nt)
            valid = _band_valid(dil, nt, tq)
            dq = jnp.zeros(q3.shape, F32)
            dk = jnp.zeros(k3.shape, F32)
            dv = jnp.zeros(k3.shape, F32)
            for lm in lms:
                qm = jnp.where(lm, q3, jnp.zeros_like(q3))
                dom = jnp.where(lm, do3, jnp.zeros_like(do3))
                p = jnp.where(valid, jnp.exp(_bqk(qm, k3) - _head_col(lt3, lm)), 0.0)
                dv = dv + _bkd(p.astype(BF16), dom)
                ds = (p * (_bqk(dom, v3) - _head_col(dl3, lm))).astype(BF16)
                dq = dq + jnp.where(lm, _bqd(ds, k3), 0.0)
                dk = dk + _bkd(ds, qm)
            _scatter_classes(dq_ref, dq, dil, nt, tq, add=True)
            for ref, g in ((dk_ref, dk), (dv_ref, dv)):
                if nt > 1:
                    own, prev = g[:, tq:, :], g[:, :tq, :]
                    shifted = []
                    for r in range(dil):
                        t = prev[r * nt:(r + 1) * nt]
                        shifted.append(jnp.concatenate([t[1:], jnp.zeros_like(t[:1])], axis=0))
                    g = own + (shifted[0] if dil == 1 else jnp.concatenate(shifted, axis=0))
                _scatter_classes(ref, g, dil, nt, tq, add=True)

        @pl.when((b == n_steps[0] - 1) & (hp == n_steps[1] - 1))
        def _():
            rs.finish(part_refs, rs_scratch, g_refs)

    blk = pl.BlockSpec((seq, LANES), lambda b, hp: (b, hp))
    s = jax.ShapeDtypeStruct
    outs = pl.pallas_call(
        body, name="attn_bwd", grid=n_steps,
        out_shape=[s((t_tok, ATTN_W), F32)] * 4 + [s(p.shape[1:], F32) for p in parts],
        in_specs=[_zblock(seq, 0), _zblock(seq, 1), _zblock(seq, 2), _zblock(seq, 3), _full(gq2.shape),
                  _full(gk2.shape), blk, blk, blk] + [pl.BlockSpec(memory_space=pl.ANY)] * n_rs,
        out_specs=[blk] * 4 + [_full(p.shape[1:]) for p in parts],
        scratch_shapes=[pltpu.VMEM((seq, LANES), F32)] * 4 + rs.scratch(parts[0].dtype),
        compiler_params=_params(2, VMEM_LIMIT_V7X),
    )(z, z, z, z, gq2, gk2, o, lse, dmix, *parts)
    return outs[:4], outs[4:]


def _ssm_bwd(z, dmix, y, x_re, x_im, a_re, a_im, bb_re, bb_im, cc_re, cc_im, d_skip, w_glu, b_glu, nb, seq):
    t_tok = nb * seq
    tc = min(SSM_CHUNK, seq)
    nch = seq // tc
    grp = N_STATE // 4

    def body(u_ref, gs_ref, ds_ref, y_ref, xr_ref, xi_ref, xpr_ref, xpi_ref,
             ar_ref, ai_ref, bbr_ref, bbi_ref, ccr_ref, cci_ref, d_ref, wg_ref, bg_ref,
             du_ref, dgs_ref, dwg_ref, dbg_ref, dd_ref, dar_ref, dai_ref, dbbr_ref, dbbi_ref, dccr_ref, dcci_ref,
             lam_re, lam_im, car_re, car_im, acc_wg, seg_a, seg_b, ent_re, ent_im):
        step = pl.program_id(1)
        first_chunk = step == nch - 1

        @pl.when((pl.program_id(0) == 0) & (step == 0))
        def _():
            acc_wg[...] = jnp.zeros_like(acc_wg)
            for ref in (dbg_ref, dd_ref, dar_ref, dai_ref, dbbr_ref, dbbi_ref, dccr_ref, dcci_ref):
                ref[...] = jnp.zeros_like(ref)

        @pl.when(step == 0)
        def _():
            car_re[...] = jnp.zeros_like(car_re)
            car_im[...] = jnp.zeros_like(car_im)

        u, gs, dssm, y = u_ref[...], gs_ref[...], ds_ref[...], y_ref[...]
        yg, dgelu = _gelu_and_grad(y)
        ygb = yg.astype(BF16)
        sgl = _sig(_dot(ygb, wg_ref[...]) + bg_ref[...])
        sgs = _sig(gs)
        dout = dssm * gs * sgs
        dgs_ref[...] = dssm * yg * sgl * sgs * (1.0 + gs * (1.0 - sgs))
        dgl = dout * yg * sgl * (1.0 - sgl)
        dglb = dgl.astype(BF16)
        dyg = dout * sgl + _dot_nt(dglb, wg_ref[...])
        acc_wg[...] += _dot_tn(ygb, dglb)
        dbg_ref[...] += jnp.sum(dgl, axis=0, keepdims=True)
        dy = dyg * dgelu
        dd_ref[...] += jnp.sum(dy * u, axis=0, keepdims=True)
        _to_segments(seg_a, dy)
        _to_segments(seg_b, u)
        for j in range(4):
            dyj = seg_a[j].astype(BF16)
            sl = slice(j * grp, (j + 1) * grp)
            lam_re[:, sl] = _dot(dyj, ccr_ref[j])
            lam_im[:, sl] = -_dot(dyj, cci_ref[j])
            dccr_ref[j] += _dot_tn(dyj, xr_ref[:, sl].astype(BF16))
            dcci_ref[j] -= _dot_tn(dyj, xi_ref[:, sl].astype(BF16))

        keep_prev = jnp.where(first_chunk, 0.0, 1.0)
        seg = tc // SUBLANES
        last_blk = pl.ds((seg - 1) * SUBLANES, SUBLANES)
        row0 = lax.broadcasted_iota(jnp.int32, (SUBLANES, N_STATE), 0) == 0
        for src, prev, dst in ((xr_ref, xpr_ref, ent_re), (xi_ref, xpi_ref, ent_im)):
            before = jnp.broadcast_to(prev[SUBLANES - 1:SUBLANES, :] * keep_prev, (SUBLANES, N_STATE))
            dst[...] = jnp.where(row0, before, pltpu.roll(src[last_blk, :], 1, 0))

        def visit(cols, j, lr, li, acc):
            if j is None:
                dar_ref[:, cols] += jnp.sum(acc[0], axis=0, keepdims=True)
                dai_ref[:, cols] += jnp.sum(acc[1], axis=0, keepdims=True)
                return None
            blk = pl.ds(pl.multiple_of(jnp.maximum(j - 1, 0) * SUBLANES, SUBLANES), SUBLANES)
            inside = j > 0
            xpr = jnp.where(inside, xr_ref[blk, cols], ent_re[:, cols])
            xpi = jnp.where(inside, xi_ref[blk, cols], ent_im[:, cols])
            return acc[0] + lr * xpr + li * xpi, acc[1] + li * xpr - lr * xpi

        _scan_chunk(lam_re, lam_im, ar_ref, ai_ref, car_re, car_im, tc, reverse=True, visit=visit)

        for j in range(4):
            sl = slice(j * grp, (j + 1) * grp)
            lr = lam_re[:, sl].astype(BF16)
            li = lam_im[:, sl].astype(BF16)
            uj = seg_b[j].astype(BF16)
            seg_a[j] = _dot_nt(lr, bbr_ref[j]) + _dot_nt(li, bbi_ref[j])
            dbbr_ref[j] += _dot_tn(uj, lr)
            dbbi_ref[j] += _dot_tn(uj, li)
        du_ref[...] = _from_segments(seg_a) + dy * d_ref[...]

        @pl.when((pl.program_id(0) == nb - 1) & (step == nch - 1))
        def _():
            dwg_ref[...] = acc_wg[...].astype(BF16)

    rev = lambda b, ch: b * nch + (nch - 1 - ch)
    umap = lambda b, ch: (rev(b, ch), 4)
    gmap = lambda b, ch: (rev(b, ch), 5)
    smap = lambda b, ch: (rev(b, ch), 1)
    row = lambda b, ch: (rev(b, ch), 0)
    prev = lambda b, ch: (jnp.maximum(rev(b, ch) * (tc // SUBLANES) - 1, 0), 0)
    s = jax.ShapeDtypeStruct
    consts = [a_re, a_im, bb_re, bb_im, cc_re, cc_im, d_skip, w_glu, b_glu]
    acc_shapes = [s((1, SSM_W), F32), s((1, SSM_W), F32), s((1, N_STATE), F32), s((1, N_STATE), F32),
                  s(bb_re.shape, F32), s(bb_re.shape, F32), s(cc_re.shape, F32), s(cc_re.shape, F32)]
    return pl.pallas_call(
        body, name="ssm_bwd", grid=(nb, nch),
        out_shape=[s((t_tok, SSM_W), F32), s((t_tok, SSM_W), F32), s((SSM_W, SSM_W), BF16)] + acc_shapes,
        in_specs=[pl.BlockSpec((tc, SSM_W), umap), pl.BlockSpec((tc, SSM_W), gmap), pl.BlockSpec((tc, SSM_W), smap),
                  pl.BlockSpec((tc, SSM_W), row), pl.BlockSpec((tc, N_STATE), row), pl.BlockSpec((tc, N_STATE), row),
                  pl.BlockSpec((SUBLANES, N_STATE), prev), pl.BlockSpec((SUBLANES, N_STATE), prev)]
        + [_full(c.shape) for c in consts],
        out_specs=[pl.BlockSpec((tc, SSM_W), row), pl.BlockSpec((tc, SSM_W), row), _full((SSM_W, SSM_W))]
        + [_full(a.shape) for a in acc_shapes],
        scratch_shapes=[pltpu.VMEM((tc, N_STATE), F32), pltpu.VMEM((tc, N_STATE), F32),
                        pltpu.VMEM((SUBLANES, N_STATE), F32), pltpu.VMEM((SUBLANES, N_STATE), F32),
                        pltpu.VMEM((SSM_W, SSM_W), F32), pltpu.VMEM((4, tc, LANES), F32),
                        pltpu.VMEM((4, tc, LANES), F32),
                        pltpu.VMEM((SUBLANES, N_STATE), F32), pltpu.VMEM((SUBLANES, N_STATE), F32)],
        compiler_params=_params(2, VMEM_LIMIT_V7X),
    )(z, z, dmix, y, x_re, x_im, x_re, x_im, *consts)


def _dz_and_dx(x2, z, dqh, dkh, dvb, dga, du, dgs, dh1, w_in_g, g_mix, gq_t, gk_t, ones_bd, fold, smalls):
    t_tok = x2.shape[0]
    tm = min(256, t_tok)
    nt = t_tok // tm
    a_w = ATTN_W

    def head_norm_bwd(raw, d_hat, gain, scale, ones):
        r = lax.rsqrt(_hdot(raw * raw, ones) * (1.0 / HEAD_DIM) + EPS)
        n = raw * r
        a = d_hat * gain * scale
        d_raw = r * (a - n * (_hdot(a * n, ones) * (1.0 / HEAD_DIM)))
        return d_raw, jnp.sum(d_hat * n * scale, axis=0, keepdims=True)

    n_small = len(smalls)
    ag = _AllGather(n_small, cast=False)

    def body(*refs):
        (x_ref, q_ref, k_ref, dq_ref, dk_ref, dv_ref, dga_ref, du_ref, dgs_ref, dh1_ref, w_ref, g_ref,
         gq_ref, gk_ref, ones_ref, fold_ref), refs = refs[:16], refs[16:]
        small_refs, refs = list(refs[:n_small]), refs[n_small:]
        (dz_ref, gx_ref, dgm_ref, dgq_ref, dgk_ref), refs = refs[:5], refs[5:]
        gath_refs, refs = list(refs[:n_small]), refs[n_small:]
        (acc_q, acc_k), ag_sems = refs[:2], refs[2:]
        i = pl.program_id(0)

        @pl.when(i == 0)
        def _():
            dgm_ref[...] = jnp.zeros_like(dgm_ref)
            acc_q[...] = jnp.zeros_like(acc_q)
            acc_k[...] = jnp.zeros_like(acc_k)
            ag.start(small_refs, gath_refs, ag_sems)

        @pl.when(i == nt // 2)
        def _():
            ag.forward(small_refs, gath_refs, ag_sems)

        ones = ones_ref[...]
        dq, sq = head_norm_bwd(q_ref[...], dq_ref[...], gq_ref[...], HEAD_DIM ** -0.5, ones)
        dk, sk = head_norm_bwd(k_ref[...], dk_ref[...], gk_ref[...], 1.0, ones)
        acc_q[...] += jnp.broadcast_to(sq, acc_q.shape)
        acc_k[...] += jnp.broadcast_to(sk, acc_k.shape)
        parts = (dq, dk, dv_ref[...], dga_ref[...], du_ref[...], dgs_ref[...])
        for n, part in enumerate(parts):
            dz_ref[:, n * a_w:(n + 1) * a_w] = part.astype(BF16)
        dxn = jnp.zeros((tm, D_MODEL), F32)
        for j in range(N_DEV):
            dxn = dxn + _dot_nt(dz_ref[:, j * COL_W:(j + 1) * COL_W], w_ref[j])
        x = x_ref[...]
        r1 = lax.rsqrt(jnp.mean(x * x, axis=-1, keepdims=True) + EPS)
        xnorm = x * r1
        dgm_ref[...] += jnp.sum(dxn * xnorm, axis=0, keepdims=True)
        a = dxn * g_ref[...]
        gx_ref[...] = dh1_ref[...] + r1 * (a - xnorm * jnp.mean(a * xnorm, axis=-1, keepdims=True))

        @pl.when(i == nt - 1)
        def _():
            dgq_ref[...] = _hdot(acc_q[...], fold_ref[...])
            dgk_ref[...] = _hdot(acc_k[...], fold_ref[...])
            ag.finish(small_refs, gath_refs, ag_sems)

    row = lambda i: (i, 0)
    col = lambda n: (lambda i: (i, n))
    s = jax.ShapeDtypeStruct
    half = pl.BlockSpec((tm, a_w), row)
    any_spec = pl.BlockSpec(memory_space=pl.ANY)
    outs = pl.pallas_call(
        body, name="dz_dx", grid=(nt,),
        out_shape=[s((t_tok, IN_W), BF16), s((t_tok, D_MODEL), F32), s((1, D_MODEL), F32),
                   s((SUBLANES, HEAD_DIM), F32), s((SUBLANES, HEAD_DIM), F32)]
        + [s((N_DEV,) + a.shape, F32) for a in smalls],
        in_specs=[pl.BlockSpec((tm, D_MODEL), row), pl.BlockSpec((tm, a_w), col(0)), pl.BlockSpec((tm, a_w), col(1)),
                  half, half, half, half, half, half, pl.BlockSpec((tm, D_MODEL), row),
                  _full(w_in_g.shape), _full(g_mix.shape), _full(gq_t.shape), _full(gk_t.shape),
                  _full(ones_bd.shape), _full(fold.shape)] + [any_spec] * n_small,
        out_specs=[pl.BlockSpec((tm, IN_W), row), pl.BlockSpec((tm, D_MODEL), row), _full((1, D_MODEL)),
                   _full((SUBLANES, HEAD_DIM)), _full((SUBLANES, HEAD_DIM))] + [any_spec] * n_small,
        scratch_shapes=[pltpu.VMEM((SUBLANES, a_w), F32), pltpu.VMEM((SUBLANES, a_w), F32)] + ag.scratch(),
        compiler_params=_params(1, VMEM_LIMIT_V7X),
    )(x2, z, z, dqh, dkh, dvb, dga, du, dgs, dh1, w_in_g, g_mix, gq_t, gk_t, ones_bd, fold, *smalls)
    return outs[:5], outs[5:]


def _dw_in(xn, dz, glu_parts, smalls):
    t_tok = xn.shape[0]
    tk = min(1024, t_tok)
    nk = t_tok // tk
    rs = _ReduceScatter([glu_parts.shape])
    n_small = len(smalls)
    ag = _AllGather(n_small, cast=False)
    n_rs = len(rs.scratch(BF16))

    def place():
        x, y, c = lax.axis_index("x"), lax.axis_index("y"), lax.axis_index("c")
        return x, y, c, [(1 - x, y), (x, 1 - y), (1 - x, 1 - y)]

    def target(i):
        x, y, c, _ = place()
        n = i // 2
        px = jnp.where((n == 0) | (n == 2), 1 - x, x)
        py = jnp.where((n == 1) | (n == 2), 1 - y, y)
        pc = jnp.where(i % 2 == 0, 1 - c, c)
        return 4 * px + 2 * py + pc

    chunk, chunks = _row_chunks(D_MODEL)

    def body(*refs):
        (xn_ref, dz_ref, glu_ref), refs = refs[:3], refs[3:]
        small_refs, refs = list(refs[:n_small]), refs[n_small:]
        (gin_ref, gglu_ref), refs = refs[:2], refs[2:]
        gath_refs, refs = list(refs[:n_small]), refs[n_small:]
        (acc, stage, land, send_sems, recv_sems), rest = refs[:5], refs[5:]
        rs_scratch, ag_sems = rest[:n_rs], rest[n_rs:]
        i, k = pl.program_id(0), pl.program_id(1)
        x, y, c, chips = place()

        def push(slot, to):
            return pltpu.make_async_remote_copy(
                src_ref=stage.at[slot], dst_ref=land.at[slot], send_sem=send_sems.at[slot],
                recv_sem=recv_sems.at[slot], device_id=to, device_id_type=MESH)

        pushes = [push(n, (x, y, 1 - c)) for n in range(4)] + [push(4 + n, (*chips[n], c)) for n in range(3)]

        def staged(slot, plus=None):
            def put(s, carry):
                r = pl.ds(pl.multiple_of(s * chunk, chunk), chunk)
                val = acc[r, :]
                if plus is not None:
                    val = val + land[plus, r, :].astype(F32)
                stage[slot, r, :] = val.astype(BF16)
                return carry

            lax.fori_loop(0, chunks, put, 0)

        @pl.when((i == 0) & (k == 0))
        def _():
            rs.start([glu_ref], rs_scratch)
            ag.start(small_refs, gath_refs, ag_sems)

        @pl.when((i == N_DEV // 2) & (k == 0))
        def _():
            ag.forward(small_refs, gath_refs, ag_sems)

        @pl.when(k == 0)
        def _():
            acc[...] = jnp.zeros_like(acc)

        acc[...] += _dot_tn(xn_ref[...], dz_ref[...])

        for n in range(4):
            @pl.when((k == nk - 1) & (i == 2 * n))
            def _(n=n):
                staged(n)
                pushes[n].start()

        for n in range(3):
            @pl.when((k == nk - 1) & (i == 2 * n + 1))
            def _(n=n):
                pushes[n].wait_recv()
                staged(4 + n, plus=n)
                pushes[4 + n].start()

        @pl.when((k == nk - 1) & (i == N_DEV - 1))
        def _():
            for slot in range(3, N_DEV - 1):
                pushes[slot].wait_recv()

            def add(s, carry):
                r = pl.ds(pl.multiple_of(s * chunk, chunk), chunk)
                total = acc[r, :]
                for slot in range(3, N_DEV - 1):
                    total = total + land[slot, r, :].astype(F32)
                gin_ref[r, :] = total
                return carry

            lax.fori_loop(0, chunks, add, 0)
            for cp in pushes:
                cp.wait_send()
            rs.finish([glu_ref], rs_scratch, [gglu_ref])
            ag.finish(small_refs, gath_refs, ag_sems)

    any_spec = pl.BlockSpec(memory_space=pl.ANY)
    s = jax.ShapeDtypeStruct
    outs = pl.pallas_call(
        body, name="dw_in", grid=(N_DEV, nk),
        out_shape=[s((D_MODEL, COL_W), F32), s(glu_parts.shape[1:], F32)]
        + [s((N_DEV,) + a.shape, F32) for a in smalls],
        in_specs=[pl.BlockSpec((tk, D_MODEL), lambda i, k: (k, 0)),
                  pl.BlockSpec((tk, COL_W), lambda i, k: (k, target(i))), any_spec] + [any_spec] * n_small,
        out_specs=[_full((D_MODEL, COL_W)), _full(glu_parts.shape[1:])] + [any_spec] * n_small,
        scratch_shapes=[pltpu.VMEM((D_MODEL, COL_W), F32), pltpu.VMEM((N_DEV - 1, D_MODEL, COL_W), BF16),
                        pltpu.VMEM((N_DEV - 1, D_MODEL, COL_W), BF16), pltpu.SemaphoreType.DMA((N_DEV - 1,)),
                        pltpu.SemaphoreType.DMA((N_DEV - 1,))] + rs.scratch(BF16) + ag.scratch(),
        compiler_params=_params(2, VMEM_LIMIT_V7X),
    )(xn, dz, glu_parts, *smalls)
    return outs[0], outs[1], outs[2:]


SMALL = ("mix_norm", "q_norm", "k_norm", "lambda_re", "lambda_im", "log_dt", "b_re", "b_im", "c_re", "c_im",
         "d_skip", "b_glu", "ple_norm")
BIG = ("w_in", "w_glu", "w_out", "w_ple_gate", "w_ple_proj")
WEIGHTS = ("mix_norm", "w_in", "q_norm", "k_norm", "lambda_re", "lambda_im", "log_dt", "b_re", "b_im", "c_re",
           "c_im", "d_skip", "w_glu", "b_glu", "w_out", "ple_norm", "w_ple_gate", "w_ple_proj")


def kernel(x, p, mix_norm, w_in, q_norm, k_norm, lambda_re, lambda_im, log_dt, b_re, b_im, c_re, c_im, d_skip, w_glu, b_glu, w_out, ple_norm, w_ple_gate, w_ple_proj, loss_target, m_mix_norm, m_w_in, m_q_norm, m_k_norm, m_lambda_re, m_lambda_im, m_log_dt, m_b_re, m_b_im, m_c_re, m_c_im, m_d_skip, m_w_glu, m_b_glu, m_w_out, m_ple_norm, m_w_ple_gate, m_w_ple_proj, v_mix_norm, v_w_in, v_q_norm, v_k_norm, v_lambda_re, v_lambda_im, v_log_dt, v_b_re, v_b_im, v_c_re, v_c_im, v_d_skip, v_w_glu, v_b_glu, v_w_out, v_ple_norm, v_w_ple_gate, v_w_ple_proj):
    env = dict(locals())
    w = {n: env[n] for n in WEIGHTS}
    m = {n: env["m_" + n] for n in WEIGHTS}
    v = {n: env["v_" + n] for n in WEIGHTS}
    nb, seq, _ = x.shape
    t_tok = nb * seq
    x2 = x.reshape(t_tok, D_MODEL)
    tg2 = loss_target.reshape(t_tok, D_MODEL)
    p2 = p.reshape(t_tok, PLE_DIM)

    shard2d = {"w_in": (D_MODEL, COL_W), "w_glu": (SSM_W // N_DEV, SSM_W), "w_out": (D_MODEL // N_DEV, D_MODEL),
               "w_ple_gate": (D_MODEL // N_DEV, D_MODEL), "w_ple_proj": (PLE_DIM, D_MODEL // N_DEV)}
    w_sh = [w[n].reshape(shard2d[n]) for n in BIG]

    g3 = (SSM_GROUPS, 1, SSM_STATE)
    lr3, li3 = lambda_re.reshape(g3), lambda_im.reshape(g3)
    dt3 = log_dt.reshape(SSM_GROUPS, 1, 1)
    btr = b_re[0].transpose(0, 2, 1)
    bti = b_im[0].transpose(0, 2, 1)
    a_re3, a_im3, bb_re, bb_im, cc_re, cc_im = _zoh_fwd(lr3, li3, dt3, btr, bti, c_re[0], c_im[0])
    a_re, a_im = a_re3.reshape(1, N_STATE), a_im3.reshape(1, N_STATE)

    ones_bd = _head_ones()
    fold = _head_fold()
    gq_t = jnp.tile(q_norm, (1, ATTN_W // HEAD_DIM))
    gk_t = jnp.tile(k_norm, (1, ATTN_W // HEAD_DIM))

    gq2 = jnp.tile(q_norm, (1, LANES // HEAD_DIM))
    gk2 = jnp.tile(k_norm, (1, LANES // HEAD_DIM))

    z, xn, w_in_g = _in_proj(x2, mix_norm, w_sh[0])
    (o, lse, ag), (w_glu_g, w_out_g, w_g_g, w_p_g) = _attn_fwd(z, gq2, gk2, nb, seq, w_sh[1:])
    w_glu_f = w_glu_g.reshape(SSM_W, SSM_W)
    w_out_f = w_out_g.reshape(D_MODEL, D_MODEL)
    w_g_f = w_g_g.reshape(D_MODEL, D_MODEL)
    x_re, x_im, y, sg = _ssm_fwd(z, a_re, a_im, bb_re, bb_im, cc_re, cc_im, d_skip, w_glu_f, b_glu, nb, seq)
    dmix, dh1, loss_t, d_ple, dw_out, dw_g, dw_p = _tail(x2, tg2, ag, sg, p2, w_out_f, w_g_f, w_p_g, ple_norm)

    early_parts = [dw_out.reshape(N_DEV, D_MODEL // N_DEV, D_MODEL), dw_g.reshape(N_DEV, D_MODEL // N_DEV, D_MODEL),
                   dw_p]
    (dqh, dkh, dvb, dga), (g_out, g_g, g_p) = _attn_bwd(z, gq2, gk2, o, lse, dmix, nb, seq, early_parts)
    (du, dgs, dw_glu, d_bglu, d_dskip, da_re, da_im, dbb_re, dbb_im, dcc_re, dcc_im) = _ssm_bwd(
        z, dmix, y, x_re, x_im, a_re, a_im, bb_re, bb_im, cc_re, cc_im, d_skip, w_glu_f, b_glu, nb, seq)
    d_lr, d_li, d_dt, d_btr, d_bti, d_cr, d_ci = _zoh_bwd(
        lr3, li3, dt3, btr, bti, da_re.reshape(g3), da_im.reshape(g3), dbb_re, dbb_im, dcc_re, dcc_im, fold)

    swapped = ("b_re", "b_im")

    def to_own(n, a):
        a = a.reshape(a.shape[1:]) if a.ndim > 2 else a
        return a.transpose(0, 2, 1) if n in swapped else a

    def from_own(n, a):
        a = a.transpose(0, 2, 1) if n in swapped else a
        return a.reshape(w[n].shape)

    own = {n: to_own(n, w[n]).shape for n in SMALL}
    early_g = {"lambda_re": d_lr, "lambda_im": d_li, "log_dt": d_dt, "b_re": d_btr, "b_im": d_bti, "c_re": d_cr,
               "c_im": d_ci, "d_skip": d_dskip, "b_glu": d_bglu, "ple_norm": d_ple}
    early = [n for n in SMALL if n in early_g]
    (dz, gx, d_mix, d_gq, d_gk), early_gathered = _dz_and_dx(
        x2, z, dqh, dkh, dvb, dga, du, dgs, dh1, w_in_g, mix_norm, gq_t, gk_t, ones_bd, fold,
        [early_g[n].reshape(own[n]) for n in early] + [loss_t])
    late_g = {"mix_norm": d_mix, "q_norm": d_gq[0:1], "k_norm": d_gk[0:1]}
    late = [n for n in SMALL if n in late_g]
    g_in, g_glu, late_gathered = _dw_in(xn, dz, dw_glu.reshape(N_DEV, SSM_W // N_DEV, SSM_W),
                                        [late_g[n].reshape(own[n]) for n in late])
    g_sh = [g_in, g_glu, g_out, g_g, g_p]
    d_sh, m_sh, v_sh = _adamw_shards(g_sh, w_sh, [m[n].reshape(shard2d[n]) for n in BIG],
                                     [v[n].reshape(shard2d[n]) for n in BIG])

    gathered = dict(zip(early + ["loss"] + late, list(early_gathered) + list(late_gathered)))
    *g_small, loss_sum = _small_sum([gathered[n] for n in SMALL] + [gathered["loss"]])
    d_small, m_small, v_small = _adamw_small(
        g_small, *[[to_own(n, src[n]) for n in SMALL] for src in (w, m, v)])

    grads, deltas, new_m, new_v = {}, {}, {}, {}
    for dst, arrs in ((grads, g_small), (deltas, d_small), (new_m, m_small), (new_v, v_small)):
        for n, a in zip(SMALL, arrs):
            dst[n] = from_own(n, a)
    for i, n in enumerate(BIG):
        grads[n] = g_sh[i].reshape(w[n].shape)
        deltas[n] = d_sh[i].reshape(w[n].shape)
        new_m[n] = m_sh[i].reshape(w[n].shape)
        new_v[n] = v_sh[i].reshape(w[n].shape)

    loss = loss_sum[0, 0]
    return (loss, gx.reshape(x.shape), *[grads[n] for n in WEIGHTS], *[deltas[n] for n in WEIGHTS],
            *[new_m[n] for n in WEIGHTS], *[new_v[n] for n in WEIGHTS])
```

```python
import math

import numpy as np
import jax
import jax.numpy as jnp
from jax import lax
from jax.experimental import pallas as pl
from jax.experimental.pallas import tpu as pltpu

F32 = jnp.float32
BF16 = jnp.bfloat16
MESH = pl.DeviceIdType.MESH
AXES = ("x", "y", "c")
N_DEV = 8

D_MODEL = 1024
HEAD_DIM = 64
ATTN_W = 512
SSM_W = 512
SSM_GROUPS = 32
SSM_GROUP = 16
SSM_STATE = 64
N_STATE = SSM_GROUPS * SSM_STATE
PLE_DIM = 256
IN_W = 3072
COL_W = IN_W // N_DEV
DILATED = ((128, 1), (512, 4), (2048, 16))
EPS = 1e-6
INV_SQRT2 = 1.0 / math.sqrt(2.0)
INV_SQRT_2PI = 1.0 / math.sqrt(2.0 * math.pi)

ADAM_LR, ADAM_B1, ADAM_B2, ADAM_EPS, ADAM_WD, ADAM_STEP = 0.001, 0.9, 0.999, 1e-08, 0.01, 10

VMEM_LIMIT_V7X = 56 * 1024 * 1024
SUBLANES = 8
LANES = 128


def _params(n_axes=None, vmem=None):
    kw = {}
    if n_axes:
        kw["dimension_semantics"] = ("arbitrary",) * n_axes
    if vmem:
        kw["vmem_limit_bytes"] = vmem
    return pltpu.CompilerParams(**kw)


def _dot(a, b):
    return jnp.dot(a, b, preferred_element_type=F32)


def _dot_nt(a, b):
    return lax.dot_general(a, b, (((1,), (1,)), ((), ())), preferred_element_type=F32)


def _dot_tn(a, b):
    return lax.dot_general(a, b, (((0,), (0,)), ((), ())), preferred_element_type=F32)


def _hdot(a, ones):
    hi = a.astype(BF16)
    lo = (a - hi.astype(F32)).astype(BF16)
    return _dot(hi, ones) + _dot(lo, ones)


def _sig(x):
    return 1.0 / (1.0 + jnp.exp(-x))


def _gelu_and_grad(y):
    cdf = 0.5 * (1.0 + lax.erf(y * INV_SQRT2))
    pdf = jnp.exp(-0.5 * y * y) * INV_SQRT_2PI
    return y * cdf, cdf + y * pdf


def _vmem():
    return pl.BlockSpec(memory_space=pltpu.VMEM)


def _full(shape):
    nd = len(shape)
    return pl.BlockSpec(shape, lambda *_: (0,) * nd)


class _AllGather:
    def __init__(self, n, cast):
        self.n, self.cast = n, cast

    def scratch(self):
        n = self.n
        return [pltpu.SemaphoreType.DMA((7 * n,)), pltpu.SemaphoreType.DMA((7 * n,)), pltpu.SemaphoreType.DMA((n,))]

    def _plan(self, src_refs, out_refs, sems):
        send_sems, recv_sems, own_sems = sems
        x, y, c = lax.axis_index("x"), lax.axis_index("y"), lax.axis_index("c")
        me, sibling = (x, y, c), (x, y, 1 - c)
        chips = [(1 - x, y), (x, 1 - y), (1 - x, 1 - y)]

        def idx(px, py, pc):
            return 4 * px + 2 * py + pc

        def copy(i, k, block, to, own_src=False):
            ref = out_refs[i].at[idx(*block)]
            return pltpu.make_async_remote_copy(
                src_ref=src_refs[i] if own_src and not self.cast else ref, dst_ref=ref,
                send_sem=send_sems.at[7 * i + k], recv_sem=recv_sems.at[7 * i + k],
                device_id=to, device_id_type=MESH)

        first, passed, arrive_ici, arrive_d2d, own = [], [], [], [], []
        for i in range(self.n):
            first.append(copy(i, 0, me, sibling, own_src=True))
            first += [copy(i, 1 + j, me, (*chip, c), own_src=True) for j, chip in enumerate(chips)]
            arrive_ici += [copy(i, 1 + j, (*chip, c), me) for j, chip in enumerate(chips)]
            passed += [copy(i, 4 + j, (*chip, c), sibling) for j, chip in enumerate(chips)]
            arrive_d2d.append(copy(i, 0, sibling, me))
            arrive_d2d += [copy(i, 4 + j, (*chip, 1 - c), me) for j, chip in enumerate(chips)]
            if not self.cast:
                own.append(pltpu.make_async_copy(src_refs[i], out_refs[i].at[idx(*me)], own_sems.at[i]))
        return idx(*me), first, passed, arrive_ici, arrive_d2d, own

    def start(self, src_refs, out_refs, sems):
        my, first, _, _, _, own = self._plan(src_refs, out_refs, sems)
        if self.cast:
            for i in range(self.n):
                out_refs[i][my] = src_refs[i][...].astype(out_refs[i].dtype)
        for cp in own + first:
            cp.start()

    def forward(self, src_refs, out_refs, sems):
        _, _, passed, arrive_ici, _, _ = self._plan(src_refs, out_refs, sems)
        for cp in arrive_ici:
            cp.wait_recv()
        for cp in passed:
            cp.start()

    def finish(self, src_refs, out_refs, sems):
        _, first, passed, _, arrive_d2d, own = self._plan(src_refs, out_refs, sems)
        for cp in own:
            cp.wait()
        for cp in arrive_d2d:
            cp.wait_recv()
        for cp in first + passed:
            cp.wait_send()


class _HostedGather:
    def __init__(self, shards):
        self.shapes = [(N_DEV,) + a.shape for a in shards]
        self.n = len(shards)
        self.ag = _AllGather(self.n, cast=True)

    def out_shape(self):
        return [jax.ShapeDtypeStruct(s, BF16) for s in self.shapes]

    def scratch(self):
        return [pltpu.VMEM(s, BF16) for s in self.shapes] + self.ag.scratch() + [pltpu.SemaphoreType.DMA((self.n,))]

    def _split(self, scratch):
        return scratch[:self.n], scratch[self.n:-1], scratch[-1]

    def start(self, src_refs, scratch):
        land, sems, _ = self._split(scratch)
        self.ag.start(src_refs, land, sems)

    def forward(self, src_refs, scratch):
        land, sems, _ = self._split(scratch)
        self.ag.forward(src_refs, land, sems)

    def finish(self, src_refs, scratch, out_refs):
        land, sems, out_sems = self._split(scratch)
        self.ag.finish(src_refs, land, sems)
        outs = [pltpu.make_async_copy(land[n], out_refs[n], out_sems.at[n]) for n in range(self.n)]
        for cp in outs:
            cp.start()
        for cp in outs:
            cp.wait()


def _all_gather(shards, out_dtypes, name):
    n = len(shards)
    ag = _AllGather(n, cast=True)

    def body(*refs):
        in_refs, out_refs, sems = refs[:n], refs[n:2 * n], refs[2 * n:]
        ag.start(in_refs, out_refs, sems)
        ag.forward(in_refs, out_refs, sems)
        ag.finish(in_refs, out_refs, sems)

    return pl.pallas_call(
        body, name=name,
        out_shape=[jax.ShapeDtypeStruct((N_DEV,) + s.shape, dt) for s, dt in zip(shards, out_dtypes)],
        in_specs=[_vmem()] * n, out_specs=[_vmem()] * n,
        scratch_shapes=ag.scratch(),
        compiler_params=_params(vmem=VMEM_LIMIT_V7X),
    )(*shards)


def _row_chunks(rows):
    chunk = 64 if rows % 64 == 0 else rows
    return chunk, rows // chunk


class _ReduceScatter:
    def __init__(self, shapes):
        self.shapes = shapes
        self.n = len(shapes)

    def scratch(self, dtype):
        return ([pltpu.VMEM(s, dtype) for s in self.shapes]
                + [pltpu.SemaphoreType.DMA((7 * self.n,)), pltpu.SemaphoreType.DMA((7 * self.n,)),
                   pltpu.SemaphoreType.DMA((self.n,))])

    def _copies(self, in_refs, land_refs, send_sems, recv_sems, own_sems):
        x, y, c = lax.axis_index("x"), lax.axis_index("y"), lax.axis_index("c")
        remote, own = [], []
        for i in range(self.n):
            for m in range(1, N_DEV):
                px = 1 - x if m & 4 else x
                py = 1 - y if m & 2 else y
                pc = 1 - c if m & 1 else c
                remote.append(pltpu.make_async_remote_copy(
                    src_ref=in_refs[i].at[4 * px + 2 * py + pc], dst_ref=land_refs[i].at[m - 1],
                    send_sem=send_sems.at[7 * i + m - 1], recv_sem=recv_sems.at[7 * i + m - 1],
                    device_id=(px, py, pc), device_id_type=MESH))
            own.append(pltpu.make_async_copy(in_refs[i].at[4 * x + 2 * y + c], land_refs[i].at[N_DEV - 1],
                                             own_sems.at[i]))
        return remote, own

    def start(self, in_refs, scratch):
        remote, own = self._copies(in_refs, scratch[:self.n], *scratch[self.n:])
        for cp in remote + own:
            cp.start()

    def finish(self, in_refs, scratch, out_refs):
        land_refs = scratch[:self.n]
        remote, own = self._copies(in_refs, land_refs, *scratch[self.n:])
        for cp in own:
            cp.wait()
        for cp in remote:
            cp.wait_recv()
        for i in range(self.n):
            chunk, steps = _row_chunks(self.shapes[i][1])

            def step(s, carry, i=i, chunk=chunk):
                r = pl.ds(pl.multiple_of(s * chunk, chunk), chunk)
                acc = land_refs[i][N_DEV - 1, r, :].astype(F32)
                for m in range(1, N_DEV):
                    acc = acc + land_refs[i][m - 1, r, :].astype(F32)
                out_refs[i][r, :] = acc
                return carry

            lax.fori_loop(0, steps, step, 0)
        for cp in remote:
            cp.wait_send()


def _reduce_scatter(parts, name):
    n = len(parts)
    rs = _ReduceScatter([p.shape for p in parts])

    def body(*refs):
        in_refs, out_refs, scratch = refs[:n], refs[n:2 * n], refs[2 * n:]
        rs.start(in_refs, scratch)
        rs.finish(in_refs, scratch, out_refs)

    return pl.pallas_call(
        body, name=name,
        out_shape=[jax.ShapeDtypeStruct(p.shape[1:], F32) for p in parts],
        in_specs=[_vmem()] * n, out_specs=[_vmem()] * n,
        scratch_shapes=rs.scratch(parts[0].dtype),
        compiler_params=_params(vmem=VMEM_LIMIT_V7X),
    )(*parts)


def _adamw_math(w, g, m, v):
    m = ADAM_B1 * m + (1.0 - ADAM_B1) * g
    v = ADAM_B2 * v + (1.0 - ADAM_B2) * (g * g)
    m_hat = m / (1.0 - ADAM_B1 ** ADAM_STEP)
    v_hat = v / (1.0 - ADAM_B2 ** ADAM_STEP)
    delta = -ADAM_LR * (m_hat / (jnp.sqrt(v_hat) + ADAM_EPS) + ADAM_WD * w)
    return delta, m, v


def _adamw_shards(gs, ws, ms, vs):
    n = len(gs)

    def body(*refs):
        g_refs, w_refs, m_refs, v_refs = (refs[k * n:(k + 1) * n] for k in range(4))
        d_out, m_out, v_out = (refs[(4 + k) * n:(5 + k) * n] for k in range(3))
        for i in range(n):
            chunk, steps = _row_chunks(gs[i].shape[0])

            def step(s, carry, i=i, chunk=chunk):
                r = pl.ds(pl.multiple_of(s * chunk, chunk), chunk)
                d, m, v = _adamw_math(w_refs[i][r, :], g_refs[i][r, :], m_refs[i][r, :], v_refs[i][r, :])
                d_out[i][r, :] = d
                m_out[i][r, :] = m
                v_out[i][r, :] = v
                return carry

            lax.fori_loop(0, steps, step, 0)

    shapes = [jax.ShapeDtypeStruct(g.shape, F32) for g in gs]
    outs = pl.pallas_call(
        body, name="adamw_shards", out_shape=shapes * 3,
        in_specs=[_vmem()] * (4 * n), out_specs=[_vmem()] * (3 * n),
        compiler_params=_params(vmem=VMEM_LIMIT_V7X),
    )(*gs, *ws, *ms, *vs)
    return outs[:n], outs[n:2 * n], outs[2 * n:]


def _small_sum(gathered):
    n = len(gathered)

    def body(*refs):
        ga_refs, out_refs = refs[:n], refs[n:]
        for i in range(n):
            def total(idx, i=i):
                g = ga_refs[i][(0,) + idx].astype(F32)
                for j in range(1, N_DEV):
                    g = g + ga_refs[i][(j,) + idx].astype(F32)
                out_refs[i][idx] = g

            if len(gathered[i].shape) == 4:
                def step(s, carry, total=total):
                    total((s,))
                    return carry

                lax.fori_loop(0, gathered[i].shape[1], step, 0)
            else:
                total((Ellipsis,))

    return pl.pallas_call(
        body, name="small_sum", out_shape=[jax.ShapeDtypeStruct(g.shape[1:], F32) for g in gathered],
        in_specs=[_vmem()] * n, out_specs=[_vmem()] * n,
        compiler_params=_params(vmem=VMEM_LIMIT_V7X),
    )(*gathered)


def _adamw_small(gs, ws, ms, vs):
    n = len(gs)

    def body(*refs):
        g_refs, w_refs, m_refs, v_refs = (refs[k * n:(k + 1) * n] for k in range(4))
        d_out, m_out, v_out = (refs[(4 + k) * n:(5 + k) * n] for k in range(3))
        for i in range(n):
            def update(idx, i=i):
                d, mm, vv = _adamw_math(w_refs[i][idx], g_refs[i][idx], m_refs[i][idx], v_refs[i][idx])
                d_out[i][idx] = d
                m_out[i][idx] = mm
                v_out[i][idx] = vv

            if len(gs[i].shape) == 3:
                def step(s, carry, update=update):
                    update(s)
                    return carry

                lax.fori_loop(0, gs[i].shape[0], step, 0)
            else:
                update(Ellipsis)

    shapes = [jax.ShapeDtypeStruct(g.shape, F32) for g in gs]
    outs = pl.pallas_call(
        body, name="adamw_small", out_shape=shapes * 3,
        in_specs=[_vmem()] * (4 * n), out_specs=[_vmem()] * (3 * n),
        compiler_params=_params(vmem=VMEM_LIMIT_V7X),
    )(*gs, *ws, *ms, *vs)
    return outs[:n], outs[n:2 * n], outs[2 * n:]


def _zoh(lr, li, logdt, btr, bti):
    dt = jnp.exp(logdt)
    mag = jnp.exp(lr * dt)
    th = li * dt
    ar = mag * jnp.cos(th)
    ai = mag * jnp.sin(th)
    den = lr * lr + li * li
    nr = ar - 1.0
    cr = (nr * lr + ai * li) / den
    ci = (ai * lr - nr * li) / den
    return ar, ai, cr * btr - ci * bti, cr * bti + ci * btr


BD_GROUPS = 8
BD_ROWS = BD_GROUPS * SSM_GROUP
BD_COLS = BD_GROUPS * SSM_STATE
N_BD = SSM_GROUPS // BD_GROUPS


def _bd_mask():
    r = lax.broadcasted_iota(jnp.int32, (BD_ROWS, BD_COLS), 0) // SSM_GROUP
    c = lax.broadcasted_iota(jnp.int32, (BD_ROWS, BD_COLS), 1) // SSM_STATE
    return r == c


def _blockdiag_store(out_ref, t):
    mask = _bd_mask()
    for j in range(N_BD):
        rows = t[j * BD_GROUPS:(j + 1) * BD_GROUPS].reshape(BD_ROWS, SSM_STATE)
        out_ref[j] = jnp.where(mask, jnp.tile(rows, (1, BD_GROUPS)), 0.0).astype(out_ref.dtype)


def _blockdiag_load(m_ref, fold):
    mask = _bd_mask()
    parts = [_hdot(jnp.where(mask, m_ref[j], 0.0), fold).reshape(BD_GROUPS, SSM_GROUP, SSM_STATE)
             for j in range(N_BD)]
    return jnp.concatenate(parts, axis=0)


def _zoh_fwd(lr, li, logdt, btr, bti, c_re, c_im):
    def body(lr_ref, li_ref, dt_ref, br_ref, bi_ref, cr_ref, ci_ref, ar_ref, ai_ref, bbr_ref, bbi_ref, ccr_ref,
             cci_ref):
        ar, ai, bbr, bbi = _zoh(lr_ref[...], li_ref[...], dt_ref[...], br_ref[...], bi_ref[...])
        ar_ref[...] = ar
        ai_ref[...] = ai
        _blockdiag_store(bbr_ref, bbr)
        _blockdiag_store(bbi_ref, bbi)
        _blockdiag_store(ccr_ref, cr_ref[...])
        _blockdiag_store(cci_ref, ci_ref[...])

    s = jax.ShapeDtypeStruct
    bd = s((N_BD, BD_ROWS, BD_COLS), BF16)
    return pl.pallas_call(
        body, name="zoh_fwd", out_shape=[s(lr.shape, F32), s(lr.shape, F32), bd, bd, bd, bd],
        in_specs=[_vmem()] * 7, out_specs=[_vmem()] * 6,
    )(lr, li, logdt, btr, bti, c_re, c_im)


def _zoh_bwd(lr, li, logdt, btr, bti, dar, dai, dbb_re, dbb_im, dcc_re, dcc_im, fold):
    def body(lr_ref, li_ref, dt_ref, br_ref, bi_ref, dar_ref, dai_ref, dbbr_ref, dbbi_ref, dccr_ref, dcci_ref,
             fold_ref, glr_ref, gli_ref, gdt_ref, gbr_ref, gbi_ref, gcr_ref, gci_ref):
        fold_m = fold_ref[...]
        _, vjp = jax.vjp(_zoh, lr_ref[...], li_ref[...], dt_ref[...], br_ref[...], bi_ref[...])
        glr, gli, gdt, gbr, gbi = vjp((dar_ref[...], dai_ref[...], _blockdiag_load(dbbr_ref, fold_m),
                                       _blockdiag_load(dbbi_ref, fold_m)))
        glr_ref[...] = glr
        gli_ref[...] = gli
        gdt_ref[...] = gdt
        gbr_ref[...] = gbr.astype(BF16)
        gbi_ref[...] = gbi.astype(BF16)
        gcr_ref[...] = _blockdiag_load(dccr_ref, fold_m).astype(BF16)
        gci_ref[...] = _blockdiag_load(dcci_ref, fold_m).astype(BF16)

    s = jax.ShapeDtypeStruct
    return pl.pallas_call(
        body, name="zoh_bwd",
        out_shape=[s(lr.shape, F32), s(lr.shape, F32), s(logdt.shape, F32)] + [s(btr.shape, BF16)] * 4,
        in_specs=[_vmem()] * 12, out_specs=[_vmem()] * 7,
    )(lr, li, logdt, btr, bti, dar, dai, dbb_re, dbb_im, dcc_re, dcc_im, fold)


def _head_ones():
    r = np.arange(ATTN_W) // HEAD_DIM
    return jnp.asarray(r[:, None] == r[None, :], dtype=BF16)


def _head_fold():
    return jnp.asarray(np.tile(np.eye(HEAD_DIM), (ATTN_W // HEAD_DIM, 1)), dtype=BF16)


def _in_proj(x2, g_mix, w_in_sh):
    t_tok = x2.shape[0]
    tm = min(1024, t_tok)
    nt = t_tok // tm
    ag_w = _AllGather(1, cast=True)
    n_sem = len(ag_w.scratch())

    def owner(i):
        x, y, c = lax.axis_index("x"), lax.axis_index("y"), lax.axis_index("c")
        rel = i // 2
        px = jnp.where((rel == 1) | (rel == 3), 1 - x, x)
        py = jnp.where((rel == 2) | (rel == 3), 1 - y, y)
        pc = jnp.where(i % 2 == 1, 1 - c, c)
        return 4 * px + 2 * py + pc

    def body(*refs):
        x_ref, g_ref, w_ref, z_ref, xn_ref, wg_ref, xn_scr, w_land = refs[:8]
        sems_w, out_sem = refs[8:8 + n_sem], refs[8 + n_sem]
        i, t = pl.program_id(0), pl.program_id(1)
        _, first, passed, arrive_ici, arrive_d2d, _ = ag_w._plan([w_ref], [w_land], sems_w)

        @pl.when((i == 0) & (t == 0))
        def _():
            ag_w.start([w_ref], [w_land], sems_w)

        @pl.when((i == 1) & (t == 0))
        def _():
            arrive_d2d[0].wait_recv()

        for n in range(3):
            @pl.when((i == 2 + 2 * n) & (t == 0))
            def _(n=n):
                arrive_ici[n].wait_recv()
                passed[n].start()

            @pl.when((i == 3 + 2 * n) & (t == 0))
            def _(n=n):
                arrive_d2d[1 + n].wait_recv()

        @pl.when(i == 0)
        def _():
            x = x_ref[...]
            r = lax.rsqrt(jnp.mean(x * x, axis=-1, keepdims=True) + EPS)
            xn = (x * r * g_ref[...]).astype(BF16)
            xn_ref[...] = xn
            xn_scr[t] = xn

        z_ref[...] = _dot(xn_scr[t], w_land[owner(i)])

        @pl.when((i == N_DEV - 1) & (t == nt - 1))
        def _():
            for cp in first + passed:
                cp.wait_send()
            out = pltpu.make_async_copy(w_land, wg_ref, out_sem)
            out.start()
            out.wait()

    s = jax.ShapeDtypeStruct
    xmap = lambda i, t: (jnp.where(i == 0, t, nt - 1), 0)
    gathered = s((N_DEV,) + w_in_sh.shape, BF16)
    return pl.pallas_call(
        body, name="in_proj", grid=(N_DEV, nt),
        out_shape=[s((t_tok, IN_W), F32), s((t_tok, D_MODEL), BF16), gathered],
        in_specs=[pl.BlockSpec((tm, D_MODEL), xmap), _full(g_mix.shape), _full(w_in_sh.shape)],
        out_specs=[pl.BlockSpec((tm, COL_W), lambda i, t: (t, owner(i))), pl.BlockSpec((tm, D_MODEL), xmap),
                   pl.BlockSpec(memory_space=pl.ANY)],
        scratch_shapes=[pltpu.VMEM((nt, tm, D_MODEL), BF16), pltpu.VMEM(gathered.shape, BF16)] + ag_w.scratch()
        + [pltpu.SemaphoreType.DMA],
        compiler_params=_params(2, VMEM_LIMIT_V7X),
    )(x2, g_mix, w_in_sh)


TQ = 128
NEG = -1e30


def _head_col(t, lm):
    return jnp.max(jnp.where(lm, t, NEG), axis=-1, keepdims=True)


def _head_masks():
    lane = lax.broadcasted_iota(jnp.int32, (1, 1, LANES), 2)
    return [(lane // HEAD_DIM) == h for h in range(LANES // HEAD_DIM)]


def _stack_heads(t3, lms):
    return jnp.concatenate([jnp.where(lm, t3, jnp.zeros_like(t3)) for lm in lms], axis=1)


def _unstack_heads(t2, lms, tq):
    out = t2[:, :tq]
    for h in range(1, len(lms)):
        out = jnp.where(lms[h], t2[:, h * tq:(h + 1) * tq], out)
    return out


def _gather_classes(ref, dil, nt, tq, dtype):
    length = nt * tq
    if dil == 1:
        return ref[...].astype(dtype).reshape(nt, tq, LANES)
    parts = [ref[pl.ds(r, length, stride=dil), :].astype(dtype).reshape(nt, tq, LANES) for r in range(dil)]
    return jnp.concatenate(parts, axis=0)


def _scatter_classes(ref, val, dil, nt, tq, add):
    length = nt * tq
    for r in range(dil):
        rows = pl.ds(r, length, stride=dil) if dil > 1 else slice(None)
        part = val[r * nt:(r + 1) * nt].reshape(length, LANES)
        ref[rows, :] = ref[rows, :] + part if add else part


def _with_prev_tile(t3, dil, nt):
    parts = []
    for r in range(dil):
        t = t3[r * nt:(r + 1) * nt]
        parts.append(jnp.concatenate([t[:1], t[:-1]], axis=0))
    prev = parts[0] if dil == 1 else jnp.concatenate(parts, axis=0)
    return jnp.concatenate([prev, t3], axis=1)


def _band_valid(dil, nt, tq):
    if nt == 1:
        shape = (dil, tq, tq)
        return lax.broadcasted_iota(jnp.int32, shape, 1) >= lax.broadcasted_iota(jnp.int32, shape, 2)
    shape = (dil * nt, tq, 2 * tq)
    b = lax.broadcasted_iota(jnp.int32, shape, 0)
    c = lax.broadcasted_iota(jnp.int32, shape, 2)
    d = tq + lax.broadcasted_iota(jnp.int32, shape, 1) - c
    return (d >= 0) & (d <= tq) & (((b & (nt - 1)) != 0) | (c >= tq))


def _window_tiling(seq, window, dil):
    length = seq // dil
    tq = min(TQ, length)
    nt = length // tq
    assert length % tq == 0 and nt & (nt - 1) == 0 and (nt == 1 or window == tq * dil)
    return nt, tq


def _bqk(a, b):
    return jnp.einsum("bqd,bkd->bqk", a, b, preferred_element_type=F32)


def _bqd(a, b):
    return jnp.einsum("bqk,bkd->bqd", a, b, preferred_element_type=F32)


def _bkd(a, b):
    return jnp.einsum("bqk,bqd->bkd", a, b, preferred_element_type=F32)


def _qk_hat(q_ref, k_ref, gq_ref, gk_ref):
    lane = lax.broadcasted_iota(jnp.int32, (1, LANES), 1)

    def norm(raw, gain, scale):
        sq = raw * raw
        r = jnp.zeros_like(raw)
        for h in range(LANES // HEAD_DIM):
            lm = (lane // HEAD_DIM) == h
            ms = jnp.sum(jnp.where(lm, sq, 0.0), axis=-1, keepdims=True) * (1.0 / HEAD_DIM)
            r = jnp.where(lm, lax.rsqrt(ms + EPS), r)
        return raw * r * gain * scale

    return norm(q_ref[...], gq_ref[...], HEAD_DIM ** -0.5), norm(k_ref[...], gk_ref[...], 1.0)


def _zblock(seq, group):
    return pl.BlockSpec((seq, LANES), lambda b, hp: (b, group * (ATTN_W // LANES) + hp))


def _attn_fwd(z, gq2, gk2, nb, seq, late_sh):
    t_tok = nb * seq
    n_win = len(DILATED)
    host = _HostedGather(late_sh)
    n_late = host.n
    n_steps = (nb, ATTN_W // LANES)

    def body(*refs):
        (q_ref, k_ref, v_ref, ga_ref, gq_ref, gk_ref), refs = refs[:6], refs[6:]
        late_refs, refs = refs[:n_late], refs[n_late:]
        (o_ref, l_ref, ag_ref), refs = refs[:3], refs[3:]
        lateg_refs, refs = refs[:n_late], refs[n_late:]
        (qf, kf, oc, lc), host_scratch = refs[:4], refs[4:]
        step = pl.program_id(0) * n_steps[1] + pl.program_id(1)
        total = n_steps[0] * n_steps[1]

        @pl.when(step == 0)
        def _():
            host.start(late_refs, host_scratch)

        @pl.when(step == total // 2)
        def _():
            host.forward(late_refs, host_scratch)

        qf[...], kf[...] = _qk_hat(q_ref, k_ref, gq_ref, gk_ref)
        lms = _head_masks()
        for w, (window, dil) in enumerate(DILATED):
            nt, tq = _window_tiling(seq, window, dil)
            q3 = _gather_classes(qf, dil, nt, tq, BF16)
            k3 = _gather_classes(kf, dil, nt, tq, BF16)
            v3 = _gather_classes(v_ref, dil, nt, tq, BF16)
            if nt > 1:
                k3, v3 = _with_prev_tile(k3, dil, nt), _with_prev_tile(v3, dil, nt)
            valid = _band_valid(dil, nt, tq)
            valid = jnp.concatenate([valid] * len(lms), axis=1)
            s = _bqk(_stack_heads(q3, lms), k3)
            m = jnp.max(jnp.where(valid, s, NEG), axis=-1, keepdims=True)
            p = jnp.where(valid, jnp.exp(s - m), 0.0)
            den = jnp.sum(p, axis=-1, keepdims=True)
            o = _unstack_heads(_bqd(p.astype(BF16), v3) / den, lms, tq)
            lse = _unstack_heads(jnp.broadcast_to(m + jnp.log(den), s.shape[:2] + (LANES,)), lms, tq)
            _scatter_classes(oc.at[w], o, dil, nt, tq, add=False)
            _scatter_classes(lc.at[w], lse, dil, nt, tq, add=False)
        mx = lc[0]
        for w in range(1, n_win):
            mx = jnp.maximum(mx, lc[w])
        tot = jnp.zeros_like(mx)
        o = jnp.zeros_like(mx)
        for w in range(n_win):
            e = jnp.exp(lc[w] - mx)
            tot = tot + e
            o = o + e * oc[w]
        o = o / tot
        o_ref[...] = o
        l_ref[...] = mx + jnp.log(tot)
        ga = ga_ref[...]
        ag_ref[...] = (o * ga * _sig(ga)).astype(BF16)

        @pl.when(step == total - 1)
        def _():
            host.finish(late_refs, host_scratch, lateg_refs)

    blk = pl.BlockSpec((seq, LANES), lambda b, hp: (b, hp))
    s = jax.ShapeDtypeStruct
    outs = pl.pallas_call(
        body, name="attn_fwd", grid=n_steps,
        out_shape=[s((t_tok, ATTN_W), F32), s((t_tok, ATTN_W), F32), s((t_tok, ATTN_W), BF16)] + host.out_shape(),
        in_specs=[_zblock(seq, 0), _zblock(seq, 1), _zblock(seq, 2), _zblock(seq, 3), _full(gq2.shape),
                  _full(gk2.shape)] + [_full(a.shape) for a in late_sh],
        out_specs=[blk, blk, blk] + [pl.BlockSpec(memory_space=pl.ANY)] * n_late,
        scratch_shapes=[pltpu.VMEM((seq, LANES), F32)] * 2 + [pltpu.VMEM((n_win, seq, LANES), F32)] * 2
        + host.scratch(),
        compiler_params=_params(2, VMEM_LIMIT_V7X),
    )(z, z, z, z, gq2, gk2, *late_sh)
    return outs[:3], outs[3:]


SCAN_COLS = 512


def _to_segments(dst_ref, val):
    seg = val.shape[0] // SUBLANES
    for n in range(dst_ref.shape[0]):
        for s in range(SUBLANES):
            dst_ref[n, pl.ds(s, seg, stride=SUBLANES), :] = val[s * seg:(s + 1) * seg, n * LANES:(n + 1) * LANES]


def _from_segments(src_ref):
    seg = src_ref.shape[1] // SUBLANES
    return jnp.concatenate(
        [jnp.concatenate([src_ref[n, pl.ds(s, seg, stride=SUBLANES), :] for s in range(SUBLANES)], axis=0)
         for n in range(src_ref.shape[0])], axis=1)


def _scan_chunk(re_ref, im_ref, a_re_ref, a_im_ref, carry_re, carry_im, rows, reverse, visit=None):
    seg = rows // SUBLANES
    assert seg & (seg - 1) == 0
    rowi = lax.broadcasted_iota(jnp.int32, (SUBLANES, SCAN_COLS), 0)
    edge = (SUBLANES - 1) if reverse else 0
    last = 0 if reverse else SUBLANES - 1
    at_edge = rowi == edge

    def cmul(ar, ai, br, bi):
        return ar * br - ai * bi, ar * bi + ai * br

    for c0 in range(0, N_STATE, SCAN_COLS):
        cols = slice(c0, c0 + SCAN_COLS)
        a1r = jnp.broadcast_to(a_re_ref[:, cols], (SUBLANES, SCAN_COLS))
        a1i = jnp.broadcast_to(a_im_ref[:, cols], (SUBLANES, SCAN_COLS))
        if reverse:
            a1i = -a1i

        def block_of(i):
            j = (seg - 1 - i) if reverse else i
            return j, pl.ds(pl.multiple_of(j * SUBLANES, SUBLANES), SUBLANES)

        def local(i, carry, cols=cols, a1r=a1r, a1i=a1i):
            xr, xi = carry
            _, blk = block_of(i)
            nr, ni = cmul(a1r, a1i, xr, xi)
            xr, xi = nr + re_ref[blk, cols], ni + im_ref[blk, cols]
            re_ref[blk, cols] = xr
            im_ref[blk, cols] = xi
            return xr, xi

        zero = jnp.zeros((SUBLANES, SCAN_COLS), F32)
        er, ei = lax.fori_loop(0, seg, local, (zero, zero))

        pr, pi = a1r, a1i
        for _ in range(seg.bit_length() - 1):
            pr, pi = cmul(pr, pi, pr, pi)
        cr, ci = carry_re[:, cols], carry_im[:, cols]
        inr, ini = cmul(pr, pi, cr, ci)
        er = er + jnp.where(at_edge, inr, 0.0)
        ei = ei + jnp.where(at_edge, ini, 0.0)
        for sft in (1, 2, 4):
            shift, keep = (SUBLANES - sft, rowi < SUBLANES - sft) if reverse else (sft, rowi >= sft)
            rs = jnp.where(keep, pltpu.roll(er, shift, 0), 0.0)
            ims = jnp.where(keep, pltpu.roll(ei, shift, 0), 0.0)
            dr, di = cmul(pr, pi, rs, ims)
            er, ei = er + dr, ei + di
            pr, pi = cmul(pr, pi, pr, pi)
        carry_re[:, cols] = jnp.broadcast_to(er[last:last + 1, :], (SUBLANES, SCAN_COLS))
        carry_im[:, cols] = jnp.broadcast_to(ei[last:last + 1, :], (SUBLANES, SCAN_COLS))
        one = (SUBLANES - 1) if reverse else 1
        kr = jnp.where(at_edge, cr, pltpu.roll(er, one, 0))
        ki = jnp.where(at_edge, ci, pltpu.roll(ei, one, 0))

        def fix(i, carry, cols=cols, a1r=a1r, a1i=a1i):
            kr, ki, acc = carry
            j, blk = block_of(i)
            kr, ki = cmul(a1r, a1i, kr, ki)
            xr, xi = re_ref[blk, cols] + kr, im_ref[blk, cols] + ki
            re_ref[blk, cols] = xr
            im_ref[blk, cols] = xi
            if visit is not None:
                acc = visit(cols, j, xr, xi, acc)
            return kr, ki, acc

        _, _, acc = lax.fori_loop(0, seg, fix, (kr, ki, (zero, zero)))
        if visit is not None:
            visit(cols, None, None, None, acc)


SSM_CHUNK = 512


def _ssm_fwd(z, a_re, a_im, bb_re, bb_im, cc_re, cc_im, d_skip, w_glu, b_glu, nb, seq):
    t_tok = nb * seq
    tc = min(SSM_CHUNK, seq)
    nch = seq // tc
    grp = N_STATE // 4

    def body(u_ref, gs_ref, ar_ref, ai_ref, bbr_ref, bbi_ref, ccr_ref, cci_ref, d_ref, wg_ref, bg_ref,
             xr_ref, xi_ref, y_ref, sg_ref, car_re, car_im, seg_u, seg_y):
        @pl.when(pl.program_id(1) == 0)
        def _():
            car_re[...] = jnp.zeros_like(car_re)
            car_im[...] = jnp.zeros_like(car_im)

        u = u_ref[...]
        _to_segments(seg_u, u)
        for j in range(4):
            uj = seg_u[j].astype(BF16)
            xr_ref[:, j * grp:(j + 1) * grp] = _dot(uj, bbr_ref[j])
            xi_ref[:, j * grp:(j + 1) * grp] = _dot(uj, bbi_ref[j])
        _scan_chunk(xr_ref, xi_ref, ar_ref, ai_ref, car_re, car_im, tc, reverse=False)
        for j in range(4):
            xr = xr_ref[:, j * grp:(j + 1) * grp].astype(BF16)
            xi = xi_ref[:, j * grp:(j + 1) * grp].astype(BF16)
            seg_y[j] = _dot_nt(xr, ccr_ref[j]) - _dot_nt(xi, cci_ref[j])
        y = _from_segments(seg_y) + d_ref[...] * u
        y_ref[...] = y
        yg, _ = _gelu_and_grad(y)
        gl = _dot(yg.astype(BF16), wg_ref[...]) + bg_ref[...]
        gs = gs_ref[...]
        sg_ref[...] = (yg * _sig(gl) * gs * _sig(gs)).astype(BF16)

    umap = lambda b, ch: (b * nch + ch, 4)
    gmap = lambda b, ch: (b * nch + ch, 5)
    row = lambda b, ch: (b * nch + ch, 0)
    s = jax.ShapeDtypeStruct
    consts = [a_re, a_im, bb_re, bb_im, cc_re, cc_im, d_skip, w_glu, b_glu]
    return pl.pallas_call(
        body, name="ssm_fwd", grid=(nb, nch),
        out_shape=[s((t_tok, N_STATE), F32), s((t_tok, N_STATE), F32), s((t_tok, SSM_W), F32),
                   s((t_tok, SSM_W), BF16)],
        in_specs=[pl.BlockSpec((tc, SSM_W), umap), pl.BlockSpec((tc, SSM_W), gmap)] + [_full(c.shape) for c in consts],
        out_specs=[pl.BlockSpec((tc, N_STATE), row), pl.BlockSpec((tc, N_STATE), row),
                   pl.BlockSpec((tc, SSM_W), row), pl.BlockSpec((tc, SSM_W), row)],
        scratch_shapes=[pltpu.VMEM((SUBLANES, N_STATE), F32), pltpu.VMEM((SUBLANES, N_STATE), F32),
                        pltpu.VMEM((4, tc, LANES), F32), pltpu.VMEM((4, tc, LANES), F32)],
        compiler_params=_params(2, VMEM_LIMIT_V7X),
    )(z, z, *consts)


def _tail(x2, tg2, ag, sg, p2, w_out, w_g, w_p, g_ple):
    t_tok = x2.shape[0]
    tm = min(512, t_tok)
    nt = t_tok // tm
    half = ATTN_W

    def body(x_ref, tg_ref, ag_ref, sg_ref, p_ref, wo_ref, wg_ref, wp_ref, gp_ref,
             dmix_ref, dh1_ref, loss_ref, dgp_ref, dwo_ref, dwg_ref, dwp_ref, acc_o, acc_g, acc_p):
        i = pl.program_id(0)

        @pl.when(i == 0)
        def _():
            loss_ref[...] = jnp.zeros_like(loss_ref)
            dgp_ref[...] = jnp.zeros_like(dgp_ref)
            acc_o[...] = jnp.zeros_like(acc_o)
            acc_g[...] = jnp.zeros_like(acc_g)
            acc_p[...] = jnp.zeros_like(acc_p)

        ag_t, sg_t = ag_ref[...], sg_ref[...]
        h1 = x_ref[...] + _dot(ag_t, wo_ref[0:half, :]) + _dot(sg_t, wo_ref[half:2 * half, :])
        r2 = lax.rsqrt(jnp.mean(h1 * h1, axis=-1, keepdims=True) + EPS)
        hnorm = h1 * r2
        gp = gp_ref[...]
        hn = (hnorm * gp).astype(BF16)
        gate = _sig(_dot(hn, wg_ref[...]))
        pb = p_ref[...].astype(BF16)
        pp = jnp.concatenate([_dot(pb, wp_ref[j]) for j in range(N_DEV)], axis=-1)
        h2 = h1 + gate * pp
        err = h2 - tg_ref[...]
        loss_ref[...] += 0.5 * jnp.sum(err * err) * (1.0 / D_MODEL)
        dh2 = err * (1.0 / D_MODEL)
        dpp = (dh2 * gate).astype(BF16)
        dgpre = (dh2 * pp * gate * (1.0 - gate)).astype(BF16)
        acc_p[...] += _dot_tn(pb, dpp)
        acc_g[...] += _dot_tn(hn, dgpre)
        dhn = _dot_nt(dgpre, wg_ref[...])
        dgp_ref[...] += jnp.sum(dhn * hnorm, axis=0, keepdims=True)
        a = dhn * gp
        dh1 = dh2 + r2 * (a - hnorm * jnp.mean(a * hnorm, axis=-1, keepdims=True))
        dh1_ref[...] = dh1
        dh1b = dh1.astype(BF16)
        acc_o[0:half, :] += _dot_tn(ag_t, dh1b)
        acc_o[half:2 * half, :] += _dot_tn(sg_t, dh1b)
        dmix_ref[...] = _dot_nt(dh1b, wo_ref[...])

        @pl.when(i == nt - 1)
        def _():
            dwo_ref[...] = acc_o[...].astype(BF16)
            dwg_ref[...] = acc_g[...].astype(BF16)
            for j in range(N_DEV):
                dwp_ref[j] = acc_p[:, j * LANES:(j + 1) * LANES].astype(BF16)

    row = lambda i: (i, 0)
    s = jax.ShapeDtypeStruct
    return pl.pallas_call(
        body, name="tail_fwd_bwd", grid=(nt,),
        out_shape=[s((t_tok, D_MODEL), F32), s((t_tok, D_MODEL), F32), s((SUBLANES, LANES), F32),
                   s((1, D_MODEL), F32), s((D_MODEL, D_MODEL), BF16), s((D_MODEL, D_MODEL), BF16),
                   s((N_DEV, PLE_DIM, LANES), BF16)],
        in_specs=[pl.BlockSpec((tm, D_MODEL), row), pl.BlockSpec((tm, D_MODEL), row),
                  pl.BlockSpec((tm, half), row), pl.BlockSpec((tm, half), row), pl.BlockSpec((tm, PLE_DIM), row),
                  _full(w_out.shape), _full(w_g.shape), _full(w_p.shape), _full(g_ple.shape)],
        out_specs=[pl.BlockSpec((tm, D_MODEL), row), pl.BlockSpec((tm, D_MODEL), row), _full((SUBLANES, LANES)),
                   _full((1, D_MODEL)), _full((D_MODEL, D_MODEL)), _full((D_MODEL, D_MODEL)),
                   _full((N_DEV, PLE_DIM, LANES))],
        scratch_shapes=[pltpu.VMEM((D_MODEL, D_MODEL), F32), pltpu.VMEM((D_MODEL, D_MODEL), F32),
                        pltpu.VMEM((PLE_DIM, D_MODEL), F32)],
        compiler_params=_params(1, VMEM_LIMIT_V7X),
    )(x2, tg2, ag, sg, p2, w_out, w_g, w_p, g_ple)


def _attn_bwd(z, gq2, gk2, o, lse, dmix, nb, seq, parts):
    t_tok = nb * seq
    n_rs = len(parts)
    rs = _ReduceScatter([p.shape for p in parts])
    n_steps = (nb, ATTN_W // LANES)

    def body(*refs):
        (q_ref, k_ref, v_ref, ga_ref, gq_ref, gk_ref, o_ref, l_ref, da_ref), refs = refs[:9], refs[9:]
        part_refs, refs = refs[:n_rs], refs[n_rs:]
        (dq_ref, dk_ref, dv_ref, dga_ref), refs = refs[:4], refs[4:]
        g_refs, refs = refs[:n_rs], refs[n_rs:]
        (qf, kf, dof, dlf), rs_scratch = refs[:4], refs[4:]
        b, hp = pl.program_id(0), pl.program_id(1)

        @pl.when((b == 0) & (hp == 0))
        def _():
            rs.start(part_refs, rs_scratch)

        ga, o_t, da = ga_ref[...], o_ref[...], da_ref[...]
        sga = _sig(ga)
        d_o = da * ga * sga
        dga_ref[...] = da * o_t * sga * (1.0 + ga * (1.0 - sga))
        lane = lax.broadcasted_iota(jnp.int32, (1, LANES), 1)
        d_oo = d_o * o_t
        delta = jnp.zeros_like(d_oo)
        for h in range(LANES // HEAD_DIM):
            lm2 = (lane // HEAD_DIM) == h
            delta = jnp.where(lm2, jnp.sum(jnp.where(lm2, d_oo, 0.0), axis=-1, keepdims=True), delta)
        qf[...], kf[...] = _qk_hat(q_ref, k_ref, gq_ref, gk_ref)
        dof[...] = d_o
        dlf[...] = delta
        dq_ref[...] = jnp.zeros_like(dq_ref)
        dk_ref[...] = jnp.zeros_like(dk_ref)
        dv_ref[...] = jnp.zeros_like(dv_ref)
        lms = _head_masks()
        for window, dil in DILATED:
            nt, tq = _window_tiling(seq, window, dil)
            q3 = _gather_classes(qf, dil, nt, tq, BF16)
            k3 = _gather_classes(kf, dil, nt, tq, BF16)
            v3 = _gather_classes(v_ref, dil, nt, tq, BF16)
            do3 = _gather_classes(dof, dil, nt, tq, BF16)
            lt3 = _gather_classes(l_ref, dil, nt, tq, F32)
            dl3 = _gather_classes(dlf, dil, nt, tq, F32)
            if nt > 1:
                k3, v3 = _with_prev_tile(k3, dil, nt), _with_prev_tile(v3, dil, nt)
            valid = _band_valid(dil, nt, tq)
            dq = jnp.zeros(q3.shape, F32)
            dk = jnp.zeros(k3.shape, F32)
            dv = jnp.zeros(k3.shape, F32)
            for lm in lms:
                qm = jnp.where(lm, q3, jnp.zeros_like(q3))
                dom = jnp.where(lm, do3, jnp.zeros_like(do3))
                p = jnp.where(valid, jnp.exp(_bqk(qm, k3) - _head_col(lt3, lm)), 0.0)
                dv = dv + _bkd(p.astype(BF16), dom)
                ds = (p * (_bqk(dom, v3) - _head_col(dl3, lm))).astype(BF16)
                dq = dq + jnp.where(lm, _bqd(ds, k3), 0.0)
                dk = dk + _bkd(ds, qm)
            _scatter_classes(dq_ref, dq, dil, nt, tq, add=True)
            for ref, g in ((dk_ref, dk), (dv_ref, dv)):
                if nt > 1:
                    own, prev = g[:, tq:, :], g[:, :tq, :]
                    shifted = []
                    for r in range(dil):
                        t = prev[r * nt:(r + 1) * nt]
                        shifted.append(jnp.concatenate([t[1:], jnp.zeros_like(t[:1])], axis=0))
                    g = own + (shifted[0] if dil == 1 else jnp.concatenate(shifted, axis=0))
                _scatter_classes(ref, g, dil, nt, tq, add=True)

        @pl.when((b == n_steps[0] - 1) & (hp == n_steps[1] - 1))
        def _():
            rs.finish(part_refs, rs_scratch, g_refs)

    blk = pl.BlockSpec((seq, LANES), lambda b, hp: (b, hp))
    s = jax.ShapeDtypeStruct
    outs = pl.pallas_call(
        body, name="attn_bwd", grid=n_steps,
        out_shape=[s((t_tok, ATTN_W), F32)] * 4 + [s(p.shape[1:], F32) for p in parts],
        in_specs=[_zblock(seq, 0), _zblock(seq, 1), _zblock(seq, 2), _zblock(seq, 3), _full(gq2.shape),
                  _full(gk2.shape), blk, blk, blk] + [pl.BlockSpec(memory_space=pl.ANY)] * n_rs,
        out_specs=[blk] * 4 + [_full(p.shape[1:]) for p in parts],
        scratch_shapes=[pltpu.VMEM((seq, LANES), F32)] * 4 + rs.scratch(parts[0].dtype),
        compiler_params=_params(2, VMEM_LIMIT_V7X),
    )(z, z, z, z, gq2, gk2, o, lse, dmix, *parts)
    return outs[:4], outs[4:]


def _ssm_bwd(z, dmix, y, x_re, x_im, a_re, a_im, bb_re, bb_im, cc_re, cc_im, d_skip, w_glu, b_glu, nb, seq):
    t_tok = nb * seq
    tc = min(SSM_CHUNK, seq)
    nch = seq // tc
    grp = N_STATE // 4

    def body(u_ref, gs_ref, ds_ref, y_ref, xr_ref, xi_ref, xpr_ref, xpi_ref,
             ar_ref, ai_ref, bbr_ref, bbi_ref, ccr_ref, cci_ref, d_ref, wg_ref, bg_ref,
             du_ref, dgs_ref, dwg_ref, dbg_ref, dd_ref, dar_ref, dai_ref, dbbr_ref, dbbi_ref, dccr_ref, dcci_ref,
             lam_re, lam_im, car_re, car_im, acc_wg, seg_a, seg_b, ent_re, ent_im):
        step = pl.program_id(1)
        first_chunk = step == nch - 1

        @pl.when((pl.program_id(0) == 0) & (step == 0))
        def _():
            acc_wg[...] = jnp.zeros_like(acc_wg)
            for ref in (dbg_ref, dd_ref, dar_ref, dai_ref, dbbr_ref, dbbi_ref, dccr_ref, dcci_ref):
                ref[...] = jnp.zeros_like(ref)

        @pl.when(step == 0)
        def _():
            car_re[...] = jnp.zeros_like(car_re)
            car_im[...] = jnp.zeros_like(car_im)

        u, gs, dssm, y = u_ref[...], gs_ref[...], ds_ref[...], y_ref[...]
        yg, dgelu = _gelu_and_grad(y)
        ygb = yg.astype(BF16)
        sgl = _sig(_dot(ygb, wg_ref[...]) + bg_ref[...])
        sgs = _sig(gs)
        dout = dssm * gs * sgs
        dgs_ref[...] = dssm * yg * sgl * sgs * (1.0 + gs * (1.0 - sgs))
        dgl = dout * yg * sgl * (1.0 - sgl)
        dglb = dgl.astype(BF16)
        dyg = dout * sgl + _dot_nt(dglb, wg_ref[...])
        acc_wg[...] += _dot_tn(ygb, dglb)
        dbg_ref[...] += jnp.sum(dgl, axis=0, keepdims=True)
        dy = dyg * dgelu
        dd_ref[...] += jnp.sum(dy * u, axis=0, keepdims=True)
        _to_segments(seg_a, dy)
        _to_segments(seg_b, u)
        for j in range(4):
            dyj = seg_a[j].astype(BF16)
            sl = slice(j * grp, (j + 1) * grp)
            lam_re[:, sl] = _dot(dyj, ccr_ref[j])
            lam_im[:, sl] = -_dot(dyj, cci_ref[j])
            dccr_ref[j] += _dot_tn(dyj, xr_ref[:, sl].astype(BF16))
            dcci_ref[j] -= _dot_tn(dyj, xi_ref[:, sl].astype(BF16))

        keep_prev = jnp.where(first_chunk, 0.0, 1.0)
        seg = tc // SUBLANES
        last_blk = pl.ds((seg - 1) * SUBLANES, SUBLANES)
        row0 = lax.broadcasted_iota(jnp.int32, (SUBLANES, N_STATE), 0) == 0
        for src, prev, dst in ((xr_ref, xpr_ref, ent_re), (xi_ref, xpi_ref, ent_im)):
            before = jnp.broadcast_to(prev[SUBLANES - 1:SUBLANES, :] * keep_prev, (SUBLANES, N_STATE))
            dst[...] = jnp.where(row0, before, pltpu.roll(src[last_blk, :], 1, 0))

        def visit(cols, j, lr, li, acc):
            if j is None:
                dar_ref[:, cols] += jnp.sum(acc[0], axis=0, keepdims=True)
                dai_ref[:, cols] += jnp.sum(acc[1], axis=0, keepdims=True)
                return None
            blk = pl.ds(pl.multiple_of(jnp.maximum(j - 1, 0) * SUBLANES, SUBLANES), SUBLANES)
            inside = j > 0
            xpr = jnp.where(inside, xr_ref[blk, cols], ent_re[:, cols])
            xpi = jnp.where(inside, xi_ref[blk, cols], ent_im[:, cols])
            return acc[0] + lr * xpr + li * xpi, acc[1] + li * xpr - lr * xpi

        _scan_chunk(lam_re, lam_im, ar_ref, ai_ref, car_re, car_im, tc, reverse=True, visit=visit)

        for j in range(4):
            sl = slice(j * grp, (j + 1) * grp)
            lr = lam_re[:, sl].astype(BF16)
            li = lam_im[:, sl].astype(BF16)
            uj = seg_b[j].astype(BF16)
            seg_a[j] = _dot_nt(lr, bbr_ref[j]) + _dot_nt(li, bbi_ref[j])
            dbbr_ref[j] += _dot_tn(uj, lr)
            dbbi_ref[j] += _dot_tn(uj, li)
        du_ref[...] = _from_segments(seg_a) + dy * d_ref[...]

        @pl.when((pl.program_id(0) == nb - 1) & (step == nch - 1))
        def _():
            dwg_ref[...] = acc_wg[...].astype(BF16)

    rev = lambda b, ch: b * nch + (nch - 1 - ch)
    umap = lambda b, ch: (rev(b, ch), 4)
    gmap = lambda b, ch: (rev(b, ch), 5)
    smap = lambda b, ch: (rev(b, ch), 1)
    row = lambda b, ch: (rev(b, ch), 0)
    prev = lambda b, ch: (jnp.maximum(rev(b, ch) * (tc // SUBLANES) - 1, 0), 0)
    s = jax.ShapeDtypeStruct
    consts = [a_re, a_im, bb_re, bb_im, cc_re, cc_im, d_skip, w_glu, b_glu]
    acc_shapes = [s((1, SSM_W), F32), s((1, SSM_W), F32), s((1, N_STATE), F32), s((1, N_STATE), F32),
                  s(bb_re.shape, F32), s(bb_re.shape, F32), s(cc_re.shape, F32), s(cc_re.shape, F32)]
    return pl.pallas_call(
        body, name="ssm_bwd", grid=(nb, nch),
        out_shape=[s((t_tok, SSM_W), F32), s((t_tok, SSM_W), F32), s((SSM_W, SSM_W), BF16)] + acc_shapes,
        in_specs=[pl.BlockSpec((tc, SSM_W), umap), pl.BlockSpec((tc, SSM_W), gmap), pl.BlockSpec((tc, SSM_W), smap),
                  pl.BlockSpec((tc, SSM_W), row), pl.BlockSpec((tc, N_STATE), row), pl.BlockSpec((tc, N_STATE), row),
                  pl.BlockSpec((SUBLANES, N_STATE), prev), pl.BlockSpec((SUBLANES, N_STATE), prev)]
        + [_full(c.shape) for c in consts],
        out_specs=[pl.BlockSpec((tc, SSM_W), row), pl.BlockSpec((tc, SSM_W), row), _full((SSM_W, SSM_W))]
        + [_full(a.shape) for a in acc_shapes],
        scratch_shapes=[pltpu.VMEM((tc, N_STATE), F32), pltpu.VMEM((tc, N_STATE), F32),
                        pltpu.VMEM((SUBLANES, N_STATE), F32), pltpu.VMEM((SUBLANES, N_STATE), F32),
                        pltpu.VMEM((SSM_W, SSM_W), F32), pltpu.VMEM((4, tc, LANES), F32),
                        pltpu.VMEM((4, tc, LANES), F32),
                        pltpu.VMEM((SUBLANES, N_STATE), F32), pltpu.VMEM((SUBLANES, N_STATE), F32)],
        compiler_params=_params(2, VMEM_LIMIT_V7X),
    )(z, z, dmix, y, x_re, x_im, x_re, x_im, *consts)


def _dz_and_dx(x2, z, dqh, dkh, dvb, dga, du, dgs, dh1, w_in_g, g_mix, gq_t, gk_t, ones_bd, fold):
    t_tok = x2.shape[0]
    tm = min(256, t_tok)
    nt = t_tok // tm
    a_w = ATTN_W

    def head_norm_bwd(raw, d_hat, gain, scale, ones):
        r = lax.rsqrt(_hdot(raw * raw, ones) * (1.0 / HEAD_DIM) + EPS)
        n = raw * r
        a = d_hat * gain * scale
        d_raw = r * (a - n * (_hdot(a * n, ones) * (1.0 / HEAD_DIM)))
        return d_raw, jnp.sum(d_hat * n * scale, axis=0, keepdims=True)

    def body(x_ref, q_ref, k_ref, dq_ref, dk_ref, dv_ref, dga_ref, du_ref, dgs_ref, dh1_ref, w_ref, g_ref,
             gq_ref, gk_ref, ones_ref, fold_ref, dz_ref, gx_ref, dgm_ref, dgq_ref, dgk_ref, acc_q, acc_k):
        i = pl.program_id(0)

        @pl.when(i == 0)
        def _():
            dgm_ref[...] = jnp.zeros_like(dgm_ref)
            acc_q[...] = jnp.zeros_like(acc_q)
            acc_k[...] = jnp.zeros_like(acc_k)

        ones = ones_ref[...]
        dq, sq = head_norm_bwd(q_ref[...], dq_ref[...], gq_ref[...], HEAD_DIM ** -0.5, ones)
        dk, sk = head_norm_bwd(k_ref[...], dk_ref[...], gk_ref[...], 1.0, ones)
        acc_q[...] += jnp.broadcast_to(sq, acc_q.shape)
        acc_k[...] += jnp.broadcast_to(sk, acc_k.shape)
        parts = (dq, dk, dv_ref[...], dga_ref[...], du_ref[...], dgs_ref[...])
        for n, part in enumerate(parts):
            dz_ref[:, n * a_w:(n + 1) * a_w] = part.astype(BF16)
        dxn = jnp.zeros((tm, D_MODEL), F32)
        for j in range(N_DEV):
            dxn = dxn + _dot_nt(dz_ref[:, j * COL_W:(j + 1) * COL_W], w_ref[j])
        x = x_ref[...]
        r1 = lax.rsqrt(jnp.mean(x * x, axis=-1, keepdims=True) + EPS)
        xnorm = x * r1
        dgm_ref[...] += jnp.sum(dxn * xnorm, axis=0, keepdims=True)
        a = dxn * g_ref[...]
        gx_ref[...] = dh1_ref[...] + r1 * (a - xnorm * jnp.mean(a * xnorm, axis=-1, keepdims=True))

        @pl.when(i == nt - 1)
        def _():
            dgq_ref[...] = _hdot(acc_q[...], fold_ref[...])
            dgk_ref[...] = _hdot(acc_k[...], fold_ref[...])

    row = lambda i: (i, 0)
    col = lambda n: (lambda i: (i, n))
    s = jax.ShapeDtypeStruct
    half = pl.BlockSpec((tm, a_w), row)
    return pl.pallas_call(
        body, name="dz_dx", grid=(nt,),
        out_shape=[s((t_tok, IN_W), BF16), s((t_tok, D_MODEL), F32), s((1, D_MODEL), F32),
                   s((SUBLANES, HEAD_DIM), F32), s((SUBLANES, HEAD_DIM), F32)],
        in_specs=[pl.BlockSpec((tm, D_MODEL), row), pl.BlockSpec((tm, a_w), col(0)), pl.BlockSpec((tm, a_w), col(1)),
                  half, half, half, half, half, half, pl.BlockSpec((tm, D_MODEL), row),
                  _full(w_in_g.shape), _full(g_mix.shape), _full(gq_t.shape), _full(gk_t.shape),
                  _full(ones_bd.shape), _full(fold.shape)],
        out_specs=[pl.BlockSpec((tm, IN_W), row), pl.BlockSpec((tm, D_MODEL), row), _full((1, D_MODEL)),
                   _full((SUBLANES, HEAD_DIM)), _full((SUBLANES, HEAD_DIM))],
        scratch_shapes=[pltpu.VMEM((SUBLANES, a_w), F32), pltpu.VMEM((SUBLANES, a_w), F32)],
        compiler_params=_params(1, VMEM_LIMIT_V7X),
    )(x2, z, z, dqh, dkh, dvb, dga, du, dgs, dh1, w_in_g, g_mix, gq_t, gk_t, ones_bd, fold)


def _dw_in(xn, dz, glu_parts, smalls):
    t_tok = xn.shape[0]
    tk = min(1024, t_tok)
    nk = t_tok // tk
    rs = _ReduceScatter([glu_parts.shape])
    n_small = len(smalls)
    ag = _AllGather(n_small, cast=False)
    n_rs = len(rs.scratch(BF16))

    def place():
        x, y, c = lax.axis_index("x"), lax.axis_index("y"), lax.axis_index("c")
        return x, y, c, [(1 - x, y), (x, 1 - y), (1 - x, 1 - y)]

    def target(i):
        x, y, c, _ = place()
        n = i // 2
        px = jnp.where((n == 0) | (n == 2), 1 - x, x)
        py = jnp.where((n == 1) | (n == 2), 1 - y, y)
        pc = jnp.where(i % 2 == 0, 1 - c, c)
        return 4 * px + 2 * py + pc

    chunk, chunks = _row_chunks(D_MODEL)

    def body(*refs):
        (xn_ref, dz_ref, glu_ref), refs = refs[:3], refs[3:]
        small_refs, refs = list(refs[:n_small]), refs[n_small:]
        (gin_ref, gglu_ref), refs = refs[:2], refs[2:]
        gath_refs, refs = list(refs[:n_small]), refs[n_small:]
        (acc, stage, land, send_sems, recv_sems), rest = refs[:5], refs[5:]
        rs_scratch, ag_sems = rest[:n_rs], rest[n_rs:]
        i, k = pl.program_id(0), pl.program_id(1)
        x, y, c, chips = place()

        def push(slot, to):
            return pltpu.make_async_remote_copy(
                src_ref=stage.at[slot], dst_ref=land.at[slot], send_sem=send_sems.at[slot],
                recv_sem=recv_sems.at[slot], device_id=to, device_id_type=MESH)

        pushes = [push(n, (x, y, 1 - c)) for n in range(4)] + [push(4 + n, (*chips[n], c)) for n in range(3)]

        def staged(slot, plus=None):
            def put(s, carry):
                r = pl.ds(pl.multiple_of(s * chunk, chunk), chunk)
                val = acc[r, :]
                if plus is not None:
                    val = val + land[plus, r, :].astype(F32)
                stage[slot, r, :] = val.astype(BF16)
                return carry

            lax.fori_loop(0, chunks, put, 0)

        @pl.when((i == 0) & (k == 0))
        def _():
            rs.start([glu_ref], rs_scratch)
            ag.start(small_refs, gath_refs, ag_sems)

        @pl.when((i == N_DEV // 2) & (k == 0))
        def _():
            ag.forward(small_refs, gath_refs, ag_sems)

        @pl.when(k == 0)
        def _():
            acc[...] = jnp.zeros_like(acc)

        acc[...] += _dot_tn(xn_ref[...], dz_ref[...])

        for n in range(4):
            @pl.when((k == nk - 1) & (i == 2 * n))
            def _(n=n):
                staged(n)
                pushes[n].start()

        for n in range(3):
            @pl.when((k == nk - 1) & (i == 2 * n + 1))
            def _(n=n):
                pushes[n].wait_recv()
                staged(4 + n, plus=n)
                pushes[4 + n].start()

        @pl.when((k == nk - 1) & (i == N_DEV - 1))
        def _():
            for slot in range(3, N_DEV - 1):
                pushes[slot].wait_recv()

            def add(s, carry):
                r = pl.ds(pl.multiple_of(s * chunk, chunk), chunk)
                total = acc[r, :]
                for slot in range(3, N_DEV - 1):
                    total = total + land[slot, r, :].astype(F32)
                gin_ref[r, :] = total
                return carry

            lax.fori_loop(0, chunks, add, 0)
            for cp in pushes:
                cp.wait_send()
            rs.finish([glu_ref], rs_scratch, [gglu_ref])
            ag.finish(small_refs, gath_refs, ag_sems)

    any_spec = pl.BlockSpec(memory_space=pl.ANY)
    s = jax.ShapeDtypeStruct
    outs = pl.pallas_call(
        body, name="dw_in", grid=(N_DEV, nk),
        out_shape=[s((D_MODEL, COL_W), F32), s(glu_parts.shape[1:], F32)]
        + [s((N_DEV,) + a.shape, a.dtype) for a in smalls],
        in_specs=[pl.BlockSpec((tk, D_MODEL), lambda i, k: (k, 0)),
                  pl.BlockSpec((tk, COL_W), lambda i, k: (k, target(i))), any_spec] + [any_spec] * n_small,
        out_specs=[_full((D_MODEL, COL_W)), _full(glu_parts.shape[1:])] + [any_spec] * n_small,
        scratch_shapes=[pltpu.VMEM((D_MODEL, COL_W), F32), pltpu.VMEM((N_DEV - 1, D_MODEL, COL_W), BF16),
                        pltpu.VMEM((N_DEV - 1, D_MODEL, COL_W), BF16), pltpu.SemaphoreType.DMA((N_DEV - 1,)),
                        pltpu.SemaphoreType.DMA((N_DEV - 1,))] + rs.scratch(BF16) + ag.scratch(),
        compiler_params=_params(2, VMEM_LIMIT_V7X),
    )(xn, dz, glu_parts, *smalls)
    return outs[0], outs[1], outs[2:]


SMALL = ("mix_norm", "q_norm", "k_norm", "lambda_re", "lambda_im", "log_dt", "b_re", "b_im", "c_re", "c_im",
         "d_skip", "b_glu", "ple_norm")
BIG = ("w_in", "w_glu", "w_out", "w_ple_gate", "w_ple_proj")
WEIGHTS = ("mix_norm", "w_in", "q_norm", "k_norm", "lambda_re", "lambda_im", "log_dt", "b_re", "b_im", "c_re",
           "c_im", "d_skip", "w_glu", "b_glu", "w_out", "ple_norm", "w_ple_gate", "w_ple_proj")


def kernel(x, p, mix_norm, w_in, q_norm, k_norm, lambda_re, lambda_im, log_dt, b_re, b_im, c_re, c_im, d_skip, w_glu, b_glu, w_out, ple_norm, w_ple_gate, w_ple_proj, loss_target, m_mix_norm, m_w_in, m_q_norm, m_k_norm, m_lambda_re, m_lambda_im, m_log_dt, m_b_re, m_b_im, m_c_re, m_c_im, m_d_skip, m_w_glu, m_b_glu, m_w_out, m_ple_norm, m_w_ple_gate, m_w_ple_proj, v_mix_norm, v_w_in, v_q_norm, v_k_norm, v_lambda_re, v_lambda_im, v_log_dt, v_b_re, v_b_im, v_c_re, v_c_im, v_d_skip, v_w_glu, v_b_glu, v_w_out, v_ple_norm, v_w_ple_gate, v_w_ple_proj):
    env = dict(locals())
    w = {n: env[n] for n in WEIGHTS}
    m = {n: env["m_" + n] for n in WEIGHTS}
    v = {n: env["v_" + n] for n in WEIGHTS}
    nb, seq, _ = x.shape
    t_tok = nb * seq
    x2 = x.reshape(t_tok, D_MODEL)
    tg2 = loss_target.reshape(t_tok, D_MODEL)
    p2 = p.reshape(t_tok, PLE_DIM)

    shard2d = {"w_in": (D_MODEL, COL_W), "w_glu": (SSM_W // N_DEV, SSM_W), "w_out": (D_MODEL // N_DEV, D_MODEL),
               "w_ple_gate": (D_MODEL // N_DEV, D_MODEL), "w_ple_proj": (PLE_DIM, D_MODEL // N_DEV)}
    w_sh = [w[n].reshape(shard2d[n]) for n in BIG]

    g3 = (SSM_GROUPS, 1, SSM_STATE)
    lr3, li3 = lambda_re.reshape(g3), lambda_im.reshape(g3)
    dt3 = log_dt.reshape(SSM_GROUPS, 1, 1)
    btr = b_re[0].transpose(0, 2, 1)
    bti = b_im[0].transpose(0, 2, 1)
    a_re3, a_im3, bb_re, bb_im, cc_re, cc_im = _zoh_fwd(lr3, li3, dt3, btr, bti, c_re[0], c_im[0])
    a_re, a_im = a_re3.reshape(1, N_STATE), a_im3.reshape(1, N_STATE)

    ones_bd = _head_ones()
    fold = _head_fold()
    gq_t = jnp.tile(q_norm, (1, ATTN_W // HEAD_DIM))
    gk_t = jnp.tile(k_norm, (1, ATTN_W // HEAD_DIM))

    gq2 = jnp.tile(q_norm, (1, LANES // HEAD_DIM))
    gk2 = jnp.tile(k_norm, (1, LANES // HEAD_DIM))

    z, xn, w_in_g = _in_proj(x2, mix_norm, w_sh[0])
    (o, lse, ag), (w_glu_g, w_out_g, w_g_g, w_p_g) = _attn_fwd(z, gq2, gk2, nb, seq, w_sh[1:])
    w_glu_f = w_glu_g.reshape(SSM_W, SSM_W)
    w_out_f = w_out_g.reshape(D_MODEL, D_MODEL)
    w_g_f = w_g_g.reshape(D_MODEL, D_MODEL)
    x_re, x_im, y, sg = _ssm_fwd(z, a_re, a_im, bb_re, bb_im, cc_re, cc_im, d_skip, w_glu_f, b_glu, nb, seq)
    dmix, dh1, loss_t, d_ple, dw_out, dw_g, dw_p = _tail(x2, tg2, ag, sg, p2, w_out_f, w_g_f, w_p_g, ple_norm)

    early_parts = [dw_out.reshape(N_DEV, D_MODEL // N_DEV, D_MODEL), dw_g.reshape(N_DEV, D_MODEL // N_DEV, D_MODEL),
                   dw_p]
    (dqh, dkh, dvb, dga), (g_out, g_g, g_p) = _attn_bwd(z, gq2, gk2, o, lse, dmix, nb, seq, early_parts)
    (du, dgs, dw_glu, d_bglu, d_dskip, da_re, da_im, dbb_re, dbb_im, dcc_re, dcc_im) = _ssm_bwd(
        z, dmix, y, x_re, x_im, a_re, a_im, bb_re, bb_im, cc_re, cc_im, d_skip, w_glu_f, b_glu, nb, seq)
    d_lr, d_li, d_dt, d_btr, d_bti, d_cr, d_ci = _zoh_bwd(
        lr3, li3, dt3, btr, bti, da_re.reshape(g3), da_im.reshape(g3), dbb_re, dbb_im, dcc_re, dcc_im, fold)

    swapped = ("b_re", "b_im")

    def to_own(n, a):
        a = a.reshape(a.shape[1:]) if a.ndim > 2 else a
        return a.transpose(0, 2, 1) if n in swapped else a

    def from_own(n, a):
        a = a.transpose(0, 2, 1) if n in swapped else a
        return a.reshape(w[n].shape)

    own = {n: to_own(n, w[n]).shape for n in SMALL}
    dz, gx, d_mix, d_gq, d_gk = _dz_and_dx(x2, z, dqh, dkh, dvb, dga, du, dgs, dh1, w_in_g, mix_norm, gq_t, gk_t,
                                           ones_bd, fold)
    small_g = {"mix_norm": d_mix, "q_norm": d_gq[0:1], "k_norm": d_gk[0:1], "lambda_re": d_lr, "lambda_im": d_li,
               "log_dt": d_dt, "b_re": d_btr, "b_im": d_bti, "c_re": d_cr, "c_im": d_ci,
               "d_skip": d_dskip, "b_glu": d_bglu, "ple_norm": d_ple}
    g_in, g_glu, gathered = _dw_in(xn, dz, dw_glu.reshape(N_DEV, SSM_W // N_DEV, SSM_W),
                                   [small_g[n].reshape(own[n]) for n in SMALL] + [loss_t])
    g_sh = [g_in, g_glu, g_out, g_g, g_p]
    d_sh, m_sh, v_sh = _adamw_shards(g_sh, w_sh, [m[n].reshape(shard2d[n]) for n in BIG],
                                     [v[n].reshape(shard2d[n]) for n in BIG])

    *g_small, loss_sum = _small_sum(list(gathered))
    d_small, m_small, v_small = _adamw_small(
        g_small, *[[to_own(n, src[n]) for n in SMALL] for src in (w, m, v)])

    grads, deltas, new_m, new_v = {}, {}, {}, {}
    for dst, arrs in ((grads, g_small), (deltas, d_small), (new_m, m_small), (new_v, v_small)):
        for n, a in zip(SMALL, arrs):
            dst[n] = from_own(n, a)
    for i, n in enumerate(BIG):
        grads[n] = g_sh[i].reshape(w[n].shape)
        deltas[n] = d_sh[i].reshape(w[n].shape)
        new_m[n] = m_sh[i].reshape(w[n].shape)
        new_v[n] = v_sh[i].reshape(w[n].shape)

    loss = loss_sum[0, 0]
    return (loss, gx.reshape(x.shape), *[grads[n] for n in WEIGHTS], *[deltas[n] for n in WEIGHTS],
            *[new_m[n] for n in WEIGHTS], *[new_v[n] for n in WEIGHTS])
```

```python
import math

import numpy as np
import jax
import jax.numpy as jnp
from jax import lax
from jax.experimental import pallas as pl
from jax.experimental.pallas import tpu as pltpu

F32 = jnp.float32
BF16 = jnp.bfloat16
MESH = pl.DeviceIdType.MESH
AXES = ("x", "y", "c")
N_DEV = 8

D_MODEL = 1024
HEAD_DIM = 64
ATTN_W = 512
SSM_W = 512
SSM_GROUPS = 32
SSM_GROUP = 16
SSM_STATE = 64
N_STATE = SSM_GROUPS * SSM_STATE
PLE_DIM = 256
IN_W = 3072
COL_W = IN_W // N_DEV
DILATED = ((128, 1), (512, 4), (2048, 16))
EPS = 1e-6
INV_SQRT2 = 1.0 / math.sqrt(2.0)
INV_SQRT_2PI = 1.0 / math.sqrt(2.0 * math.pi)

ADAM_LR, ADAM_B1, ADAM_B2, ADAM_EPS, ADAM_WD, ADAM_STEP = 0.001, 0.9, 0.999, 1e-08, 0.01, 10

VMEM_LIMIT_V7X = 56 * 1024 * 1024
SUBLANES = 8
LANES = 128


def _params(n_axes=None, vmem=None):
    kw = {}
    if n_axes:
        kw["dimension_semantics"] = ("arbitrary",) * n_axes
    if vmem:
        kw["vmem_limit_bytes"] = vmem
    return pltpu.CompilerParams(**kw)


def _dot(a, b):
    return jnp.dot(a, b, preferred_element_type=F32)


def _dot_nt(a, b):
    return lax.dot_general(a, b, (((1,), (1,)), ((), ())), preferred_element_type=F32)


def _dot_tn(a, b):
    return lax.dot_general(a, b, (((0,), (0,)), ((), ())), preferred_element_type=F32)


def _hdot(a, ones):
    hi = a.astype(BF16)
    lo = (a - hi.astype(F32)).astype(BF16)
    return _dot(hi, ones) + _dot(lo, ones)


def _sig(x):
    return 1.0 / (1.0 + jnp.exp(-x))


def _gelu_and_grad(y):
    cdf = 0.5 * (1.0 + lax.erf(y * INV_SQRT2))
    pdf = jnp.exp(-0.5 * y * y) * INV_SQRT_2PI
    return y * cdf, cdf + y * pdf


def _vmem():
    return pl.BlockSpec(memory_space=pltpu.VMEM)


def _full(shape):
    nd = len(shape)
    return pl.BlockSpec(shape, lambda *_: (0,) * nd)


class _AllGather:
    def __init__(self, n, cast):
        self.n, self.cast = n, cast

    def scratch(self):
        n = self.n
        return [pltpu.SemaphoreType.DMA((7 * n,)), pltpu.SemaphoreType.DMA((7 * n,)), pltpu.SemaphoreType.DMA((n,))]

    def _plan(self, src_refs, out_refs, sems):
        send_sems, recv_sems, own_sems = sems
        x, y, c = lax.axis_index("x"), lax.axis_index("y"), lax.axis_index("c")
        me, sibling = (x, y, c), (x, y, 1 - c)
        chips = [(1 - x, y), (x, 1 - y), (1 - x, 1 - y)]

        def idx(px, py, pc):
            return 4 * px + 2 * py + pc

        def copy(i, k, block, to, own_src=False):
            ref = out_refs[i].at[idx(*block)]
            return pltpu.make_async_remote_copy(
                src_ref=src_refs[i] if own_src and not self.cast else ref, dst_ref=ref,
                send_sem=send_sems.at[7 * i + k], recv_sem=recv_sems.at[7 * i + k],
                device_id=to, device_id_type=MESH)

        first, passed, arrive_ici, arrive_d2d, own = [], [], [], [], []
        for i in range(self.n):
            first.append(copy(i, 0, me, sibling, own_src=True))
            first += [copy(i, 1 + j, me, (*chip, c), own_src=True) for j, chip in enumerate(chips)]
            arrive_ici += [copy(i, 1 + j, (*chip, c), me) for j, chip in enumerate(chips)]
            passed += [copy(i, 4 + j, (*chip, c), sibling) for j, chip in enumerate(chips)]
            arrive_d2d.append(copy(i, 0, sibling, me))
            arrive_d2d += [copy(i, 4 + j, (*chip, 1 - c), me) for j, chip in enumerate(chips)]
            if not self.cast:
                own.append(pltpu.make_async_copy(src_refs[i], out_refs[i].at[idx(*me)], own_sems.at[i]))
        return idx(*me), first, passed, arrive_ici, arrive_d2d, own

    def start(self, src_refs, out_refs, sems):
        my, first, _, _, _, own = self._plan(src_refs, out_refs, sems)
        if self.cast:
            for i in range(self.n):
                out_refs[i][my] = src_refs[i][...].astype(out_refs[i].dtype)
        for cp in own + first:
            cp.start()

    def forward(self, src_refs, out_refs, sems):
        _, _, passed, arrive_ici, _, _ = self._plan(src_refs, out_refs, sems)
        for cp in arrive_ici:
            cp.wait_recv()
        for cp in passed:
            cp.start()

    def finish(self, src_refs, out_refs, sems):
        _, first, passed, _, arrive_d2d, own = self._plan(src_refs, out_refs, sems)
        for cp in own:
            cp.wait()
        for cp in arrive_d2d:
            cp.wait_recv()
        for cp in first + passed:
            cp.wait_send()


class _HostedGather:
    def __init__(self, shards):
        self.shapes = [(N_DEV,) + a.shape for a in shards]
        self.n = len(shards)
        self.ag = _AllGather(self.n, cast=True)

    def out_shape(self):
        return [jax.ShapeDtypeStruct(s, BF16) for s in self.shapes]

    def scratch(self):
        return [pltpu.VMEM(s, BF16) for s in self.shapes] + self.ag.scratch() + [pltpu.SemaphoreType.DMA((self.n,))]

    def _split(self, scratch):
        return scratch[:self.n], scratch[self.n:-1], scratch[-1]

    def start(self, src_refs, scratch):
        land, sems, _ = self._split(scratch)
        self.ag.start(src_refs, land, sems)

    def forward(self, src_refs, scratch):
        land, sems, _ = self._split(scratch)
        self.ag.forward(src_refs, land, sems)

    def finish(self, src_refs, scratch, out_refs):
        land, sems, out_sems = self._split(scratch)
        self.ag.finish(src_refs, land, sems)
        outs = [pltpu.make_async_copy(land[n], out_refs[n], out_sems.at[n]) for n in range(self.n)]
        for cp in outs:
            cp.start()
        for cp in outs:
            cp.wait()


def _all_gather(shards, out_dtypes, name):
    n = len(shards)
    ag = _AllGather(n, cast=True)

    def body(*refs):
        in_refs, out_refs, sems = refs[:n], refs[n:2 * n], refs[2 * n:]
        ag.start(in_refs, out_refs, sems)
        ag.forward(in_refs, out_refs, sems)
        ag.finish(in_refs, out_refs, sems)

    return pl.pallas_call(
        body, name=name,
        out_shape=[jax.ShapeDtypeStruct((N_DEV,) + s.shape, dt) for s, dt in zip(shards, out_dtypes)],
        in_specs=[_vmem()] * n, out_specs=[_vmem()] * n,
        scratch_shapes=ag.scratch(),
        compiler_params=_params(vmem=VMEM_LIMIT_V7X),
    )(*shards)


def _row_chunks(rows):
    chunk = 64 if rows % 64 == 0 else rows
    return chunk, rows // chunk


class _ReduceScatter:
    def __init__(self, shapes):
        self.shapes = shapes
        self.n = len(shapes)

    def scratch(self, dtype):
        return ([pltpu.VMEM(s, dtype) for s in self.shapes]
                + [pltpu.SemaphoreType.DMA((7 * self.n,)), pltpu.SemaphoreType.DMA((7 * self.n,)),
                   pltpu.SemaphoreType.DMA((self.n,))])

    def _copies(self, in_refs, land_refs, send_sems, recv_sems, own_sems):
        x, y, c = lax.axis_index("x"), lax.axis_index("y"), lax.axis_index("c")
        remote, own = [], []
        for i in range(self.n):
            for m in range(1, N_DEV):
                px = 1 - x if m & 4 else x
                py = 1 - y if m & 2 else y
                pc = 1 - c if m & 1 else c
                remote.append(pltpu.make_async_remote_copy(
                    src_ref=in_refs[i].at[4 * px + 2 * py + pc], dst_ref=land_refs[i].at[m - 1],
                    send_sem=send_sems.at[7 * i + m - 1], recv_sem=recv_sems.at[7 * i + m - 1],
                    device_id=(px, py, pc), device_id_type=MESH))
            own.append(pltpu.make_async_copy(in_refs[i].at[4 * x + 2 * y + c], land_refs[i].at[N_DEV - 1],
                                             own_sems.at[i]))
        return remote, own

    def start(self, in_refs, scratch):
        remote, own = self._copies(in_refs, scratch[:self.n], *scratch[self.n:])
        for cp in remote + own:
            cp.start()

    def finish(self, in_refs, scratch, out_refs):
        land_refs = scratch[:self.n]
        remote, own = self._copies(in_refs, land_refs, *scratch[self.n:])
        for cp in own:
            cp.wait()
        for cp in remote:
            cp.wait_recv()
        for i in range(self.n):
            chunk, steps = _row_chunks(self.shapes[i][1])

            def step(s, carry, i=i, chunk=chunk):
                r = pl.ds(pl.multiple_of(s * chunk, chunk), chunk)
                acc = land_refs[i][N_DEV - 1, r, :].astype(F32)
                for m in range(1, N_DEV):
                    acc = acc + land_refs[i][m - 1, r, :].astype(F32)
                out_refs[i][r, :] = acc
                return carry

            lax.fori_loop(0, steps, step, 0)
        for cp in remote:
            cp.wait_send()


def _reduce_scatter(parts, name):
    n = len(parts)
    rs = _ReduceScatter([p.shape for p in parts])

    def body(*refs):
        in_refs, out_refs, scratch = refs[:n], refs[n:2 * n], refs[2 * n:]
        rs.start(in_refs, scratch)
        rs.finish(in_refs, scratch, out_refs)

    return pl.pallas_call(
        body, name=name,
        out_shape=[jax.ShapeDtypeStruct(p.shape[1:], F32) for p in parts],
        in_specs=[_vmem()] * n, out_specs=[_vmem()] * n,
        scratch_shapes=rs.scratch(parts[0].dtype),
        compiler_params=_params(vmem=VMEM_LIMIT_V7X),
    )(*parts)


def _adamw_math(w, g, m, v):
    m = ADAM_B1 * m + (1.0 - ADAM_B1) * g
    v = ADAM_B2 * v + (1.0 - ADAM_B2) * (g * g)
    m_hat = m / (1.0 - ADAM_B1 ** ADAM_STEP)
    v_hat = v / (1.0 - ADAM_B2 ** ADAM_STEP)
    delta = -ADAM_LR * (m_hat / (jnp.sqrt(v_hat) + ADAM_EPS) + ADAM_WD * w)
    return delta, m, v


def _adamw_shards(gs, ws, ms, vs):
    n = len(gs)

    def body(*refs):
        g_refs, w_refs, m_refs, v_refs = (refs[k * n:(k + 1) * n] for k in range(4))
        d_out, m_out, v_out = (refs[(4 + k) * n:(5 + k) * n] for k in range(3))
        for i in range(n):
            chunk, steps = _row_chunks(gs[i].shape[0])

            def step(s, carry, i=i, chunk=chunk):
                r = pl.ds(pl.multiple_of(s * chunk, chunk), chunk)
                d, m, v = _adamw_math(w_refs[i][r, :], g_refs[i][r, :], m_refs[i][r, :], v_refs[i][r, :])
                d_out[i][r, :] = d
                m_out[i][r, :] = m
                v_out[i][r, :] = v
                return carry

            lax.fori_loop(0, steps, step, 0)

    shapes = [jax.ShapeDtypeStruct(g.shape, F32) for g in gs]
    outs = pl.pallas_call(
        body, name="adamw_shards", out_shape=shapes * 3,
        in_specs=[_vmem()] * (4 * n), out_specs=[_vmem()] * (3 * n),
        compiler_params=_params(vmem=VMEM_LIMIT_V7X),
    )(*gs, *ws, *ms, *vs)
    return outs[:n], outs[n:2 * n], outs[2 * n:]


def _small_sum(gathered):
    n = len(gathered)

    def body(*refs):
        ga_refs, out_refs = refs[:n], refs[n:]
        for i in range(n):
            def total(idx, i=i):
                g = ga_refs[i][(0,) + idx].astype(F32)
                for j in range(1, N_DEV):
                    g = g + ga_refs[i][(j,) + idx].astype(F32)
                out_refs[i][idx] = g

            if len(gathered[i].shape) == 4:
                def step(s, carry, total=total):
                    total((s,))
                    return carry

                lax.fori_loop(0, gathered[i].shape[1], step, 0)
            else:
                total((Ellipsis,))

    return pl.pallas_call(
        body, name="small_sum", out_shape=[jax.ShapeDtypeStruct(g.shape[1:], F32) for g in gathered],
        in_specs=[_vmem()] * n, out_specs=[_vmem()] * n,
        compiler_params=_params(vmem=VMEM_LIMIT_V7X),
    )(*gathered)


def _adamw_small(gs, ws, ms, vs):
    n = len(gs)

    def body(*refs):
        g_refs, w_refs, m_refs, v_refs = (refs[k * n:(k + 1) * n] for k in range(4))
        d_out, m_out, v_out = (refs[(4 + k) * n:(5 + k) * n] for k in range(3))
        for i in range(n):
            def update(idx, i=i):
                d, mm, vv = _adamw_math(w_refs[i][idx], g_refs[i][idx], m_refs[i][idx], v_refs[i][idx])
                d_out[i][idx] = d
                m_out[i][idx] = mm
                v_out[i][idx] = vv

            if len(gs[i].shape) == 3:
                def step(s, carry, update=update):
                    update(s)
                    return carry

                lax.fori_loop(0, gs[i].shape[0], step, 0)
            else:
                update(Ellipsis)

    shapes = [jax.ShapeDtypeStruct(g.shape, F32) for g in gs]
    outs = pl.pallas_call(
        body, name="adamw_small", out_shape=shapes * 3,
        in_specs=[_vmem()] * (4 * n), out_specs=[_vmem()] * (3 * n),
        compiler_params=_params(vmem=VMEM_LIMIT_V7X),
    )(*gs, *ws, *ms, *vs)
    return outs[:n], outs[n:2 * n], outs[2 * n:]


def _zoh(lr, li, logdt, btr, bti):
    dt = jnp.exp(logdt)
    mag = jnp.exp(lr * dt)
    th = li * dt
    ar = mag * jnp.cos(th)
    ai = mag * jnp.sin(th)
    den = lr * lr + li * li
    nr = ar - 1.0
    cr = (nr * lr + ai * li) / den
    ci = (ai * lr - nr * li) / den
    return ar, ai, cr * btr - ci * bti, cr * bti + ci * btr


BD_GROUPS = 8
BD_ROWS = BD_GROUPS * SSM_GROUP
BD_COLS = BD_GROUPS * SSM_STATE
N_BD = SSM_GROUPS // BD_GROUPS


def _bd_mask():
    r = lax.broadcasted_iota(jnp.int32, (BD_ROWS, BD_COLS), 0) // SSM_GROUP
    c = lax.broadcasted_iota(jnp.int32, (BD_ROWS, BD_COLS), 1) // SSM_STATE
    return r == c


def _blockdiag_store(out_ref, t):
    mask = _bd_mask()
    for j in range(N_BD):
        rows = t[j * BD_GROUPS:(j + 1) * BD_GROUPS].reshape(BD_ROWS, SSM_STATE)
        out_ref[j] = jnp.where(mask, jnp.tile(rows, (1, BD_GROUPS)), 0.0).astype(out_ref.dtype)


def _blockdiag_load(m_ref, fold):
    mask = _bd_mask()
    parts = [_hdot(jnp.where(mask, m_ref[j], 0.0), fold).reshape(BD_GROUPS, SSM_GROUP, SSM_STATE)
             for j in range(N_BD)]
    return jnp.concatenate(parts, axis=0)


def _zoh_fwd(lr, li, logdt, btr, bti, c_re, c_im):
    def body(lr_ref, li_ref, dt_ref, br_ref, bi_ref, cr_ref, ci_ref, ar_ref, ai_ref, bbr_ref, bbi_ref, ccr_ref,
             cci_ref):
        ar, ai, bbr, bbi = _zoh(lr_ref[...], li_ref[...], dt_ref[...], br_ref[...], bi_ref[...])
        ar_ref[...] = ar
        ai_ref[...] = ai
        _blockdiag_store(bbr_ref, bbr)
        _blockdiag_store(bbi_ref, bbi)
        _blockdiag_store(ccr_ref, cr_ref[...])
        _blockdiag_store(cci_ref, ci_ref[...])

    s = jax.ShapeDtypeStruct
    bd = s((N_BD, BD_ROWS, BD_COLS), BF16)
    return pl.pallas_call(
        body, name="zoh_fwd", out_shape=[s(lr.shape, F32), s(lr.shape, F32), bd, bd, bd, bd],
        in_specs=[_vmem()] * 7, out_specs=[_vmem()] * 6,
    )(lr, li, logdt, btr, bti, c_re, c_im)


def _zoh_bwd(lr, li, logdt, btr, bti, dar, dai, dbb_re, dbb_im, dcc_re, dcc_im, fold):
    def body(lr_ref, li_ref, dt_ref, br_ref, bi_ref, dar_ref, dai_ref, dbbr_ref, dbbi_ref, dccr_ref, dcci_ref,
             fold_ref, glr_ref, gli_ref, gdt_ref, gbr_ref, gbi_ref, gcr_ref, gci_ref):
        fold_m = fold_ref[...]
        _, vjp = jax.vjp(_zoh, lr_ref[...], li_ref[...], dt_ref[...], br_ref[...], bi_ref[...])
        glr, gli, gdt, gbr, gbi = vjp((dar_ref[...], dai_ref[...], _blockdiag_load(dbbr_ref, fold_m),
                                       _blockdiag_load(dbbi_ref, fold_m)))
        glr_ref[...] = glr
        gli_ref[...] = gli
        gdt_ref[...] = gdt
        gbr_ref[...] = gbr.astype(BF16)
        gbi_ref[...] = gbi.astype(BF16)
        gcr_ref[...] = _blockdiag_load(dccr_ref, fold_m).astype(BF16)
        gci_ref[...] = _blockdiag_load(dcci_ref, fold_m).astype(BF16)

    s = jax.ShapeDtypeStruct
    return pl.pallas_call(
        body, name="zoh_bwd",
        out_shape=[s(lr.shape, F32), s(lr.shape, F32), s(logdt.shape, F32)] + [s(btr.shape, BF16)] * 4,
        in_specs=[_vmem()] * 12, out_specs=[_vmem()] * 7,
    )(lr, li, logdt, btr, bti, dar, dai, dbb_re, dbb_im, dcc_re, dcc_im, fold)


def _head_ones():
    r = np.arange(ATTN_W) // HEAD_DIM
    return jnp.asarray(r[:, None] == np.arange(LANES)[None, :], dtype=BF16)


def _head_fold():
    return jnp.asarray(np.tile(np.eye(HEAD_DIM), (ATTN_W // HEAD_DIM, 1)), dtype=BF16)


def _in_proj(x2, g_mix, w_in_sh):
    t_tok = x2.shape[0]
    tm = min(1024, t_tok)
    nt = t_tok // tm
    ag_w = _AllGather(1, cast=True)
    n_sem = len(ag_w.scratch())

    def owner(i):
        x, y, c = lax.axis_index("x"), lax.axis_index("y"), lax.axis_index("c")
        rel = i // 2
        px = jnp.where((rel == 1) | (rel == 3), 1 - x, x)
        py = jnp.where((rel == 2) | (rel == 3), 1 - y, y)
        pc = jnp.where(i % 2 == 1, 1 - c, c)
        return 4 * px + 2 * py + pc

    def body(*refs):
        x_ref, g_ref, w_ref, z_ref, xn_ref, wg_ref, xn_scr, w_land = refs[:8]
        sems_w, out_sem = refs[8:8 + n_sem], refs[8 + n_sem]
        i, t = pl.program_id(0), pl.program_id(1)
        _, first, passed, arrive_ici, arrive_d2d, _ = ag_w._plan([w_ref], [w_land], sems_w)

        @pl.when((i == 0) & (t == 0))
        def _():
            ag_w.start([w_ref], [w_land], sems_w)

        @pl.when((i == 1) & (t == 0))
        def _():
            arrive_d2d[0].wait_recv()

        for n in range(3):
            @pl.when((i == 2 + 2 * n) & (t == 0))
            def _(n=n):
                arrive_ici[n].wait_recv()
                passed[n].start()

            @pl.when((i == 3 + 2 * n) & (t == 0))
            def _(n=n):
                arrive_d2d[1 + n].wait_recv()

        @pl.when(i == 0)
        def _():
            x = x_ref[...]
            r = lax.rsqrt(jnp.mean(x * x, axis=-1, keepdims=True) + EPS)
            xn = (x * r * g_ref[...]).astype(BF16)
            xn_ref[...] = xn
            xn_scr[t] = xn

        z_ref[...] = _dot(xn_scr[t], w_land[owner(i)])

        @pl.when((i == N_DEV - 1) & (t == nt - 1))
        def _():
            for cp in first + passed:
                cp.wait_send()
            out = pltpu.make_async_copy(w_land, wg_ref, out_sem)
            out.start()
            out.wait()

    s = jax.ShapeDtypeStruct
    xmap = lambda i, t: (jnp.where(i == 0, t, nt - 1), 0)
    gathered = s((N_DEV,) + w_in_sh.shape, BF16)
    return pl.pallas_call(
        body, name="in_proj", grid=(N_DEV, nt),
        out_shape=[s((t_tok, IN_W), F32), s((t_tok, D_MODEL), BF16), gathered],
        in_specs=[pl.BlockSpec((tm, D_MODEL), xmap), _full(g_mix.shape), _full(w_in_sh.shape)],
        out_specs=[pl.BlockSpec((tm, COL_W), lambda i, t: (t, owner(i))), pl.BlockSpec((tm, D_MODEL), xmap),
                   pl.BlockSpec(memory_space=pl.ANY)],
        scratch_shapes=[pltpu.VMEM((nt, tm, D_MODEL), BF16), pltpu.VMEM(gathered.shape, BF16)] + ag_w.scratch()
        + [pltpu.SemaphoreType.DMA],
        compiler_params=_params(2, VMEM_LIMIT_V7X),
    )(x2, g_mix, w_in_sh)


TQ = 128
NEG = -1e30


def _head_col(t, lm):
    return jnp.max(jnp.where(lm, t, NEG), axis=-1, keepdims=True)


def _head_masks():
    lane = lax.broadcasted_iota(jnp.int32, (1, 1, LANES), 2)
    return [(lane // HEAD_DIM) == h for h in range(LANES // HEAD_DIM)]


def _stack_heads(t3, lms):
    return jnp.concatenate([jnp.where(lm, t3, jnp.zeros_like(t3)) for lm in lms], axis=1)


def _unstack_heads(t2, lms, tq):
    out = t2[:, :tq]
    for h in range(1, len(lms)):
        out = jnp.where(lms[h], t2[:, h * tq:(h + 1) * tq], out)
    return out


def _gather_classes(ref, dil, nt, tq, dtype):
    length = nt * tq
    if dil == 1:
        return ref[...].astype(dtype).reshape(nt, tq, LANES)
    parts = [ref[pl.ds(r, length, stride=dil), :].astype(dtype).reshape(nt, tq, LANES) for r in range(dil)]
    return jnp.concatenate(parts, axis=0)


def _scatter_classes(ref, val, dil, nt, tq, add):
    length = nt * tq
    for r in range(dil):
        rows = pl.ds(r, length, stride=dil) if dil > 1 else slice(None)
        part = val[r * nt:(r + 1) * nt].reshape(length, LANES)
        ref[rows, :] = ref[rows, :] + part if add else part


def _with_prev_tile(t3, dil, nt):
    parts = []
    for r in range(dil):
        t = t3[r * nt:(r + 1) * nt]
        parts.append(jnp.concatenate([t[:1], t[:-1]], axis=0))
    prev = parts[0] if dil == 1 else jnp.concatenate(parts, axis=0)
    return jnp.concatenate([prev, t3], axis=1)


def _band_valid(dil, nt, tq):
    if nt == 1:
        shape = (dil, tq, tq)
        return lax.broadcasted_iota(jnp.int32, shape, 1) >= lax.broadcasted_iota(jnp.int32, shape, 2)
    shape = (dil * nt, tq, 2 * tq)
    b = lax.broadcasted_iota(jnp.int32, shape, 0)
    c = lax.broadcasted_iota(jnp.int32, shape, 2)
    d = tq + lax.broadcasted_iota(jnp.int32, shape, 1) - c
    return (d >= 0) & (d <= tq) & (((b & (nt - 1)) != 0) | (c >= tq))


def _window_tiling(seq, window, dil):
    length = seq // dil
    tq = min(TQ, length)
    nt = length // tq
    assert length % tq == 0 and nt & (nt - 1) == 0 and (nt == 1 or window == tq * dil)
    return nt, tq


def _bqk(a, b):
    return jnp.einsum("bqd,bkd->bqk", a, b, preferred_element_type=F32)


def _bqd(a, b):
    return jnp.einsum("bqk,bkd->bqd", a, b, preferred_element_type=F32)


def _bkd(a, b):
    return jnp.einsum("bqk,bqd->bkd", a, b, preferred_element_type=F32)


def _qk_hat(q_ref, k_ref, gq_ref, gk_ref):
    lane = lax.broadcasted_iota(jnp.int32, (1, LANES), 1)

    def norm(raw, gain, scale):
        sq = raw * raw
        r = jnp.zeros_like(raw)
        for h in range(LANES // HEAD_DIM):
            lm = (lane // HEAD_DIM) == h
            ms = jnp.sum(jnp.where(lm, sq, 0.0), axis=-1, keepdims=True) * (1.0 / HEAD_DIM)
            r = jnp.where(lm, lax.rsqrt(ms + EPS), r)
        return raw * r * gain * scale

    return norm(q_ref[...], gq_ref[...], HEAD_DIM ** -0.5), norm(k_ref[...], gk_ref[...], 1.0)


def _zblock(seq, group):
    return pl.BlockSpec((seq, LANES), lambda b, hp: (b, group * (ATTN_W // LANES) + hp))


def _attn_fwd(z, gq2, gk2, nb, seq, late_sh):
    t_tok = nb * seq
    n_win = len(DILATED)
    host = _HostedGather(late_sh)
    n_late = host.n
    n_steps = (nb, ATTN_W // LANES)

    def body(*refs):
        (q_ref, k_ref, v_ref, ga_ref, gq_ref, gk_ref), refs = refs[:6], refs[6:]
        late_refs, refs = refs[:n_late], refs[n_late:]
        (o_ref, l_ref, ag_ref, qh_ref, kh_ref), refs = refs[:5], refs[5:]
        lateg_refs, refs = refs[:n_late], refs[n_late:]
        (qf, kf, oc, lc), host_scratch = refs[:4], refs[4:]
        step = pl.program_id(0) * n_steps[1] + pl.program_id(1)
        total = n_steps[0] * n_steps[1]

        @pl.when(step == 0)
        def _():
            host.start(late_refs, host_scratch)

        @pl.when(step == total // 2)
        def _():
            host.forward(late_refs, host_scratch)

        q_hat, k_hat = _qk_hat(q_ref, k_ref, gq_ref, gk_ref)
        qh_ref[...] = q_hat.astype(BF16)
        kh_ref[...] = k_hat.astype(BF16)
        qf[...] = q_hat
        kf[...] = k_hat
        lms = _head_masks()
        for w, (window, dil) in enumerate(DILATED):
            nt, tq = _window_tiling(seq, window, dil)
            q3 = _gather_classes(qf, dil, nt, tq, BF16)
            k3 = _gather_classes(kf, dil, nt, tq, BF16)
            v3 = _gather_classes(v_ref, dil, nt, tq, BF16)
            if nt > 1:
                k3, v3 = _with_prev_tile(k3, dil, nt), _with_prev_tile(v3, dil, nt)
            valid = _band_valid(dil, nt, tq)
            valid = jnp.concatenate([valid] * len(lms), axis=1)
            s = _bqk(_stack_heads(q3, lms), k3)
            m = jnp.max(jnp.where(valid, s, NEG), axis=-1, keepdims=True)
            p = jnp.where(valid, jnp.exp(s - m), 0.0)
            den = jnp.sum(p, axis=-1, keepdims=True)
            o = _unstack_heads(_bqd(p.astype(BF16), v3) / den, lms, tq)
            lse = _unstack_heads(jnp.broadcast_to(m + jnp.log(den), s.shape[:2] + (LANES,)), lms, tq)
            _scatter_classes(oc.at[w], o, dil, nt, tq, add=False)
            _scatter_classes(lc.at[w], lse, dil, nt, tq, add=False)
        mx = lc[0]
        for w in range(1, n_win):
            mx = jnp.maximum(mx, lc[w])
        tot = jnp.zeros_like(mx)
        o = jnp.zeros_like(mx)
        for w in range(n_win):
            e = jnp.exp(lc[w] - mx)
            tot = tot + e
            o = o + e * oc[w]
        o = o / tot
        o_ref[...] = o
        l_ref[...] = mx + jnp.log(tot)
        ga = ga_ref[...]
        ag_ref[...] = (o * ga * _sig(ga)).astype(BF16)

        @pl.when(step == total - 1)
        def _():
            host.finish(late_refs, host_scratch, lateg_refs)

    blk = pl.BlockSpec((seq, LANES), lambda b, hp: (b, hp))
    s = jax.ShapeDtypeStruct
    outs = pl.pallas_call(
        body, name="attn_fwd", grid=n_steps,
        out_shape=[s((t_tok, ATTN_W), F32), s((t_tok, ATTN_W), F32)] + [s((t_tok, ATTN_W), BF16)] * 3
        + host.out_shape(),
        in_specs=[_zblock(seq, 0), _zblock(seq, 1), _zblock(seq, 2), _zblock(seq, 3), _full(gq2.shape),
                  _full(gk2.shape)] + [_full(a.shape) for a in late_sh],
        out_specs=[blk] * 5 + [pl.BlockSpec(memory_space=pl.ANY)] * n_late,
        scratch_shapes=[pltpu.VMEM((seq, LANES), F32)] * 2 + [pltpu.VMEM((n_win, seq, LANES), F32)] * 2
        + host.scratch(),
        compiler_params=_params(2, VMEM_LIMIT_V7X),
    )(z, z, z, z, gq2, gk2, *late_sh)
    return outs[:5], outs[5:]


SCAN_COLS = 512


def _to_segments(dst_ref, val):
    seg = val.shape[0] // SUBLANES
    for n in range(dst_ref.shape[0]):
        for s in range(SUBLANES):
            dst_ref[n, pl.ds(s, seg, stride=SUBLANES), :] = val[s * seg:(s + 1) * seg, n * LANES:(n + 1) * LANES]


def _from_segments(src_ref):
    seg = src_ref.shape[1] // SUBLANES
    return jnp.concatenate(
        [jnp.concatenate([src_ref[n, pl.ds(s, seg, stride=SUBLANES), :] for s in range(SUBLANES)], axis=0)
         for n in range(src_ref.shape[0])], axis=1)


def _scan_chunk(re_ref, im_ref, a_re_ref, a_im_ref, carry_re, carry_im, rows, reverse, visit=None):
    seg = rows // SUBLANES
    assert seg & (seg - 1) == 0
    rowi = lax.broadcasted_iota(jnp.int32, (SUBLANES, SCAN_COLS), 0)
    edge = (SUBLANES - 1) if reverse else 0
    last = 0 if reverse else SUBLANES - 1
    at_edge = rowi == edge

    def cmul(ar, ai, br, bi):
        return ar * br - ai * bi, ar * bi + ai * br

    for c0 in range(0, N_STATE, SCAN_COLS):
        cols = slice(c0, c0 + SCAN_COLS)
        a1r = jnp.broadcast_to(a_re_ref[:, cols], (SUBLANES, SCAN_COLS))
        a1i = jnp.broadcast_to(a_im_ref[:, cols], (SUBLANES, SCAN_COLS))
        if reverse:
            a1i = -a1i

        def block_of(i):
            j = (seg - 1 - i) if reverse else i
            return j, pl.ds(pl.multiple_of(j * SUBLANES, SUBLANES), SUBLANES)

        def local(i, carry, cols=cols, a1r=a1r, a1i=a1i):
            xr, xi = carry
            _, blk = block_of(i)
            nr, ni = cmul(a1r, a1i, xr, xi)
            xr, xi = nr + re_ref[blk, cols], ni + im_ref[blk, cols]
            re_ref[blk, cols] = xr
            im_ref[blk, cols] = xi
            return xr, xi

        zero = jnp.zeros((SUBLANES, SCAN_COLS), F32)
        er, ei = lax.fori_loop(0, seg, local, (zero, zero))

        pr, pi = a1r, a1i
        for _ in range(seg.bit_length() - 1):
            pr, pi = cmul(pr, pi, pr, pi)
        cr, ci = carry_re[:, cols], carry_im[:, cols]
        inr, ini = cmul(pr, pi, cr, ci)
        er = er + jnp.where(at_edge, inr, 0.0)
        ei = ei + jnp.where(at_edge, ini, 0.0)
        for sft in (1, 2, 4):
            shift, keep = (SUBLANES - sft, rowi < SUBLANES - sft) if reverse else (sft, rowi >= sft)
            rs = jnp.where(keep, pltpu.roll(er, shift, 0), 0.0)
            ims = jnp.where(keep, pltpu.roll(ei, shift, 0), 0.0)
            dr, di = cmul(pr, pi, rs, ims)
            er, ei = er + dr, ei + di
            pr, pi = cmul(pr, pi, pr, pi)
        carry_re[:, cols] = jnp.broadcast_to(er[last:last + 1, :], (SUBLANES, SCAN_COLS))
        carry_im[:, cols] = jnp.broadcast_to(ei[last:last + 1, :], (SUBLANES, SCAN_COLS))
        one = (SUBLANES - 1) if reverse else 1
        kr = jnp.where(at_edge, cr, pltpu.roll(er, one, 0))
        ki = jnp.where(at_edge, ci, pltpu.roll(ei, one, 0))

        def fix(i, carry, cols=cols, a1r=a1r, a1i=a1i):
            kr, ki, acc = carry
            j, blk = block_of(i)
            kr, ki = cmul(a1r, a1i, kr, ki)
            xr, xi = re_ref[blk, cols] + kr, im_ref[blk, cols] + ki
            re_ref[blk, cols] = xr
            im_ref[blk, cols] = xi
            if visit is not None:
                acc = visit(cols, j, xr, xi, acc)
            return kr, ki, acc

        _, _, acc = lax.fori_loop(0, seg, fix, (kr, ki, (zero, zero)))
        if visit is not None:
            visit(cols, None, None, None, acc)


SSM_CHUNK = 512


def _ssm_fwd(z, a_re, a_im, bb_re, bb_im, cc_re, cc_im, d_skip, w_glu, b_glu, nb, seq):
    t_tok = nb * seq
    tc = min(SSM_CHUNK, seq)
    nch = seq // tc
    grp = N_STATE // 4

    def body(u_ref, gs_ref, ar_ref, ai_ref, bbr_ref, bbi_ref, ccr_ref, cci_ref, d_ref, wg_ref, bg_ref,
             xr_ref, xi_ref, y_ref, sg_ref, car_re, car_im, seg_u, seg_y):
        @pl.when(pl.program_id(1) == 0)
        def _():
            car_re[...] = jnp.zeros_like(car_re)
            car_im[...] = jnp.zeros_like(car_im)

        u = u_ref[...]
        _to_segments(seg_u, u)
        for j in range(4):
            uj = seg_u[j].astype(BF16)
            xr_ref[:, j * grp:(j + 1) * grp] = _dot(uj, bbr_ref[j])
            xi_ref[:, j * grp:(j + 1) * grp] = _dot(uj, bbi_ref[j])
        _scan_chunk(xr_ref, xi_ref, ar_ref, ai_ref, car_re, car_im, tc, reverse=False)
        for j in range(4):
            xr = xr_ref[:, j * grp:(j + 1) * grp].astype(BF16)
            xi = xi_ref[:, j * grp:(j + 1) * grp].astype(BF16)
            seg_y[j] = _dot_nt(xr, ccr_ref[j]) - _dot_nt(xi, cci_ref[j])
        y = _from_segments(seg_y) + d_ref[...] * u
        y_ref[...] = y
        yg, _ = _gelu_and_grad(y)
        gl = _dot(yg.astype(BF16), wg_ref[...]) + bg_ref[...]
        gs = gs_ref[...]
        sg_ref[...] = (yg * _sig(gl) * gs * _sig(gs)).astype(BF16)

    umap = lambda b, ch: (b * nch + ch, 4)
    gmap = lambda b, ch: (b * nch + ch, 5)
    row = lambda b, ch: (b * nch + ch, 0)
    s = jax.ShapeDtypeStruct
    consts = [a_re, a_im, bb_re, bb_im, cc_re, cc_im, d_skip, w_glu, b_glu]
    return pl.pallas_call(
        body, name="ssm_fwd", grid=(nb, nch),
        out_shape=[s((t_tok, N_STATE), F32), s((t_tok, N_STATE), F32), s((t_tok, SSM_W), F32),
                   s((t_tok, SSM_W), BF16)],
        in_specs=[pl.BlockSpec((tc, SSM_W), umap), pl.BlockSpec((tc, SSM_W), gmap)] + [_full(c.shape) for c in consts],
        out_specs=[pl.BlockSpec((tc, N_STATE), row), pl.BlockSpec((tc, N_STATE), row),
                   pl.BlockSpec((tc, SSM_W), row), pl.BlockSpec((tc, SSM_W), row)],
        scratch_shapes=[pltpu.VMEM((SUBLANES, N_STATE), F32), pltpu.VMEM((SUBLANES, N_STATE), F32),
                        pltpu.VMEM((4, tc, LANES), F32), pltpu.VMEM((4, tc, LANES), F32)],
        compiler_params=_params(2, VMEM_LIMIT_V7X),
    )(z, z, *consts)


def _tail(x2, tg2, ag, sg, p2, w_out, w_g, w_p, g_ple):
    t_tok = x2.shape[0]
    tm = min(512, t_tok)
    nt = t_tok // tm
    half = ATTN_W

    def body(x_ref, tg_ref, ag_ref, sg_ref, p_ref, wo_ref, wg_ref, wp_ref, gp_ref,
             dmix_ref, dh1_ref, loss_ref, dgp_ref, dwo_ref, dwg_ref, dwp_ref, acc_o, acc_g, acc_p):
        i = pl.program_id(0)

        @pl.when(i == 0)
        def _():
            loss_ref[...] = jnp.zeros_like(loss_ref)
            dgp_ref[...] = jnp.zeros_like(dgp_ref)
            acc_o[...] = jnp.zeros_like(acc_o)
            acc_g[...] = jnp.zeros_like(acc_g)
            acc_p[...] = jnp.zeros_like(acc_p)

        ag_t, sg_t = ag_ref[...], sg_ref[...]
        h1 = x_ref[...] + _dot(ag_t, wo_ref[0:half, :]) + _dot(sg_t, wo_ref[half:2 * half, :])
        r2 = lax.rsqrt(jnp.mean(h1 * h1, axis=-1, keepdims=True) + EPS)
        hnorm = h1 * r2
        gp = gp_ref[...]
        hn = (hnorm * gp).astype(BF16)
        gate = _sig(_dot(hn, wg_ref[...]))
        pb = p_ref[...].astype(BF16)
        pp = jnp.concatenate([_dot(pb, wp_ref[j]) for j in range(N_DEV)], axis=-1)
        h2 = h1 + gate * pp
        err = h2 - tg_ref[...]
        loss_ref[...] += 0.5 * jnp.sum(err * err) * (1.0 / D_MODEL)
        dh2 = err * (1.0 / D_MODEL)
        dpp = (dh2 * gate).astype(BF16)
        dgpre = (dh2 * pp * gate * (1.0 - gate)).astype(BF16)
        acc_p[...] += _dot_tn(pb, dpp)
        acc_g[...] += _dot_tn(hn, dgpre)
        dhn = _dot_nt(dgpre, wg_ref[...])
        dgp_ref[...] += jnp.sum(dhn * hnorm, axis=0, keepdims=True)
        a = dhn * gp
        dh1 = dh2 + r2 * (a - hnorm * jnp.mean(a * hnorm, axis=-1, keepdims=True))
        dh1_ref[...] = dh1
        dh1b = dh1.astype(BF16)
        acc_o[0:half, :] += _dot_tn(ag_t, dh1b)
        acc_o[half:2 * half, :] += _dot_tn(sg_t, dh1b)
        dmix_ref[...] = _dot_nt(dh1b, wo_ref[...])

        @pl.when(i == nt - 1)
        def _():
            dwo_ref[...] = acc_o[...].astype(BF16)
            dwg_ref[...] = acc_g[...].astype(BF16)
            for j in range(N_DEV):
                dwp_ref[j] = acc_p[:, j * LANES:(j + 1) * LANES].astype(BF16)

    row = lambda i: (i, 0)
    s = jax.ShapeDtypeStruct
    return pl.pallas_call(
        body, name="tail_fwd_bwd", grid=(nt,),
        out_shape=[s((t_tok, D_MODEL), F32), s((t_tok, D_MODEL), F32), s((SUBLANES, LANES), F32),
                   s((1, D_MODEL), F32), s((D_MODEL, D_MODEL), BF16), s((D_MODEL, D_MODEL), BF16),
                   s((N_DEV, PLE_DIM, LANES), BF16)],
        in_specs=[pl.BlockSpec((tm, D_MODEL), row), pl.BlockSpec((tm, D_MODEL), row),
                  pl.BlockSpec((tm, half), row), pl.BlockSpec((tm, half), row), pl.BlockSpec((tm, PLE_DIM), row),
                  _full(w_out.shape), _full(w_g.shape), _full(w_p.shape), _full(g_ple.shape)],
        out_specs=[pl.BlockSpec((tm, D_MODEL), row), pl.BlockSpec((tm, D_MODEL), row), _full((SUBLANES, LANES)),
                   _full((1, D_MODEL)), _full((D_MODEL, D_MODEL)), _full((D_MODEL, D_MODEL)),
                   _full((N_DEV, PLE_DIM, LANES))],
        scratch_shapes=[pltpu.VMEM((D_MODEL, D_MODEL), F32), pltpu.VMEM((D_MODEL, D_MODEL), F32),
                        pltpu.VMEM((PLE_DIM, D_MODEL), F32)],
        compiler_params=_params(1, VMEM_LIMIT_V7X),
    )(x2, tg2, ag, sg, p2, w_out, w_g, w_p, g_ple)


def _attn_bwd(qh, kh, z, o, lse, dmix, nb, seq, parts):
    t_tok = nb * seq
    n_rs = len(parts)
    rs = _ReduceScatter([p.shape for p in parts])
    n_steps = (nb, ATTN_W // LANES)

    def body(*refs):
        (q_ref, k_ref, v_ref, ga_ref, o_ref, l_ref, da_ref), refs = refs[:7], refs[7:]
        part_refs, refs = refs[:n_rs], refs[n_rs:]
        (dq_ref, dk_ref, dv_ref, dga_ref), refs = refs[:4], refs[4:]
        g_refs, refs = refs[:n_rs], refs[n_rs:]
        (qf, kf, dof, dlf), rs_scratch = refs[:4], refs[4:]
        b, hp = pl.program_id(0), pl.program_id(1)

        @pl.when((b == 0) & (hp == 0))
        def _():
            rs.start(part_refs, rs_scratch)

        ga, o_t, da = ga_ref[...], o_ref[...], da_ref[...]
        sga = _sig(ga)
        d_o = da * ga * sga
        dga_ref[...] = da * o_t * sga * (1.0 + ga * (1.0 - sga))
        lane = lax.broadcasted_iota(jnp.int32, (1, LANES), 1)
        d_oo = d_o * o_t
        delta = jnp.zeros_like(d_oo)
        for h in range(LANES // HEAD_DIM):
            lm2 = (lane // HEAD_DIM) == h
            delta = jnp.where(lm2, jnp.sum(jnp.where(lm2, d_oo, 0.0), axis=-1, keepdims=True), delta)
        qf[...] = q_ref[...].astype(F32)
        kf[...] = k_ref[...].astype(F32)
        dof[...] = d_o
        dlf[...] = delta
        dq_ref[...] = jnp.zeros_like(dq_ref)
        dk_ref[...] = jnp.zeros_like(dk_ref)
        dv_ref[...] = jnp.zeros_like(dv_ref)
        lms = _head_masks()
        for window, dil in DILATED:
            nt, tq = _window_tiling(seq, window, dil)
            q3 = _gather_classes(qf, dil, nt, tq, BF16)
            k3 = _gather_classes(kf, dil, nt, tq, BF16)
            v3 = _gather_classes(v_ref, dil, nt, tq, BF16)
            do3 = _gather_classes(dof, dil, nt, tq, BF16)
            lt3 = _gather_classes(l_ref, dil, nt, tq, F32)
            dl3 = _gather_classes(dlf, dil, nt, tq, F32)
            if nt > 1:
                k3, v3 = _with_prev_tile(k3, dil, nt), _with_prev_tile(v3, dil, nt)
            valid = _band_valid(dil, nt, tq)
            dq = jnp.zeros(q3.shape, F32)
            dk = jnp.zeros(k3.shape, F32)
            dv = jnp.zeros(k3.shape, F32)
            for lm in lms:
                qm = jnp.where(lm, q3, jnp.zeros_like(q3))
                dom = jnp.where(lm, do3, jnp.zeros_like(do3))
                p = jnp.where(valid, jnp.exp(_bqk(qm, k3) - _head_col(lt3, lm)), 0.0)
                dv = dv + _bkd(p.astype(BF16), dom)
                ds = (p * (_bqk(dom, v3) - _head_col(dl3, lm))).astype(BF16)
                dq = dq + jnp.where(lm, _bqd(ds, k3), 0.0)
                dk = dk + _bkd(ds, qm)
            _scatter_classes(dq_ref, dq, dil, nt, tq, add=True)
            for ref, g in ((dk_ref, dk), (dv_ref, dv)):
                if nt > 1:
                    own, prev = g[:, tq:, :], g[:, :tq, :]
                    shifted = []
                    for r in range(dil):
                        t = prev[r * nt:(r + 1) * nt]
                        shifted.append(jnp.concatenate([t[1:], jnp.zeros_like(t[:1])], axis=0))
                    g = own + (shifted[0] if dil == 1 else jnp.concatenate(shifted, axis=0))
                _scatter_classes(ref, g, dil, nt, tq, add=True)

        @pl.when((b == n_steps[0] - 1) & (hp == n_steps[1] - 1))
        def _():
            rs.finish(part_refs, rs_scratch, g_refs)

    blk = pl.BlockSpec((seq, LANES), lambda b, hp: (b, hp))
    s = jax.ShapeDtypeStruct
    outs = pl.pallas_call(
        body, name="attn_bwd", grid=n_steps,
        out_shape=[s((t_tok, ATTN_W), F32)] * 4 + [s(p.shape[1:], F32) for p in parts],
        in_specs=[blk, blk, _zblock(seq, 2), _zblock(seq, 3), blk, blk, blk]
        + [pl.BlockSpec(memory_space=pl.ANY)] * n_rs,
        out_specs=[blk] * 4 + [_full(p.shape[1:]) for p in parts],
        scratch_shapes=[pltpu.VMEM((seq, LANES), F32)] * 4 + rs.scratch(parts[0].dtype),
        compiler_params=_params(2, VMEM_LIMIT_V7X),
    )(qh, kh, z, z, o, lse, dmix, *parts)
    return outs[:4], outs[4:]


def _ssm_bwd(z, dmix, y, x_re, x_im, a_re, a_im, bb_re, bb_im, cc_re, cc_im, d_skip, w_glu, b_glu, nb, seq):
    t_tok = nb * seq
    tc = min(SSM_CHUNK, seq)
    nch = seq // tc
    grp = N_STATE // 4

    def body(u_ref, gs_ref, ds_ref, y_ref, xr_ref, xi_ref, xpr_ref, xpi_ref,
             ar_ref, ai_ref, bbr_ref, bbi_ref, ccr_ref, cci_ref, d_ref, wg_ref, bg_ref,
             du_ref, dgs_ref, dwg_ref, dbg_ref, dd_ref, dar_ref, dai_ref, dbbr_ref, dbbi_ref, dccr_ref, dcci_ref,
             lam_re, lam_im, car_re, car_im, acc_wg, seg_a, seg_b, ent_re, ent_im):
        step = pl.program_id(1)
        first_chunk = step == nch - 1

        @pl.when((pl.program_id(0) == 0) & (step == 0))
        def _():
            acc_wg[...] = jnp.zeros_like(acc_wg)
            for ref in (dbg_ref, dd_ref, dar_ref, dai_ref, dbbr_ref, dbbi_ref, dccr_ref, dcci_ref):
                ref[...] = jnp.zeros_like(ref)

        @pl.when(step == 0)
        def _():
            car_re[...] = jnp.zeros_like(car_re)
            car_im[...] = jnp.zeros_like(car_im)

        u, gs, dssm, y = u_ref[...], gs_ref[...], ds_ref[...], y_ref[...]
        yg, dgelu = _gelu_and_grad(y)
        ygb = yg.astype(BF16)
        sgl = _sig(_dot(ygb, wg_ref[...]) + bg_ref[...])
        sgs = _sig(gs)
        dout = dssm * gs * sgs
        dgs_ref[...] = dssm * yg * sgl * sgs * (1.0 + gs * (1.0 - sgs))
        dgl = dout * yg * sgl * (1.0 - sgl)
        dglb = dgl.astype(BF16)
        dyg = dout * sgl + _dot_nt(dglb, wg_ref[...])
        acc_wg[...] += _dot_tn(ygb, dglb)
        dbg_ref[...] += jnp.sum(dgl, axis=0, keepdims=True)
        dy = dyg * dgelu
        dd_ref[...] += jnp.sum(dy * u, axis=0, keepdims=True)
        _to_segments(seg_a, dy)
        _to_segments(seg_b, u)
        for j in range(4):
            dyj = seg_a[j].astype(BF16)
            sl = slice(j * grp, (j + 1) * grp)
            lam_re[:, sl] = _dot(dyj, ccr_ref[j])
            lam_im[:, sl] = -_dot(dyj, cci_ref[j])
            dccr_ref[j] += _dot_tn(dyj, xr_ref[:, sl].astype(BF16))
            dcci_ref[j] -= _dot_tn(dyj, xi_ref[:, sl].astype(BF16))

        keep_prev = jnp.where(first_chunk, 0.0, 1.0)
        seg = tc // SUBLANES
        last_blk = pl.ds((seg - 1) * SUBLANES, SUBLANES)
        row0 = lax.broadcasted_iota(jnp.int32, (SUBLANES, N_STATE), 0) == 0
        for src, prev, dst in ((xr_ref, xpr_ref, ent_re), (xi_ref, xpi_ref, ent_im)):
            before = jnp.broadcast_to(prev[SUBLANES - 1:SUBLANES, :] * keep_prev, (SUBLANES, N_STATE))
            dst[...] = jnp.where(row0, before, pltpu.roll(src[last_blk, :], 1, 0))

        def visit(cols, j, lr, li, acc):
            if j is None:
                dar_ref[:, cols] += jnp.sum(acc[0], axis=0, keepdims=True)
                dai_ref[:, cols] += jnp.sum(acc[1], axis=0, keepdims=True)
                return None
            blk = pl.ds(pl.multiple_of(jnp.maximum(j - 1, 0) * SUBLANES, SUBLANES), SUBLANES)
            inside = j > 0
            xpr = jnp.where(inside, xr_ref[blk, cols], ent_re[:, cols])
            xpi = jnp.where(inside, xi_ref[blk, cols], ent_im[:, cols])
            return acc[0] + lr * xpr + li * xpi, acc[1] + li * xpr - lr * xpi

        _scan_chunk(lam_re, lam_im, ar_ref, ai_ref, car_re, car_im, tc, reverse=True, visit=visit)

        for j in range(4):
            sl = slice(j * grp, (j + 1) * grp)
            lr = lam_re[:, sl].astype(BF16)
            li = lam_im[:, sl].astype(BF16)
            uj = seg_b[j].astype(BF16)
            seg_a[j] = _dot_nt(lr, bbr_ref[j]) + _dot_nt(li, bbi_ref[j])
            dbbr_ref[j] += _dot_tn(uj, lr)
            dbbi_ref[j] += _dot_tn(uj, li)
        du_ref[...] = _from_segments(seg_a) + dy * d_ref[...]

        @pl.when((pl.program_id(0) == nb - 1) & (step == nch - 1))
        def _():
            dwg_ref[...] = acc_wg[...].astype(BF16)

    rev = lambda b, ch: b * nch + (nch - 1 - ch)
    umap = lambda b, ch: (rev(b, ch), 4)
    gmap = lambda b, ch: (rev(b, ch), 5)
    smap = lambda b, ch: (rev(b, ch), 1)
    row = lambda b, ch: (rev(b, ch), 0)
    prev = lambda b, ch: (jnp.maximum(rev(b, ch) * (tc // SUBLANES) - 1, 0), 0)
    s = jax.ShapeDtypeStruct
    consts = [a_re, a_im, bb_re, bb_im, cc_re, cc_im, d_skip, w_glu, b_glu]
    acc_shapes = [s((1, SSM_W), F32), s((1, SSM_W), F32), s((1, N_STATE), F32), s((1, N_STATE), F32),
                  s(bb_re.shape, F32), s(bb_re.shape, F32), s(cc_re.shape, F32), s(cc_re.shape, F32)]
    return pl.pallas_call(
        body, name="ssm_bwd", grid=(nb, nch),
        out_shape=[s((t_tok, SSM_W), F32), s((t_tok, SSM_W), F32), s((SSM_W, SSM_W), BF16)] + acc_shapes,
        in_specs=[pl.BlockSpec((tc, SSM_W), umap), pl.BlockSpec((tc, SSM_W), gmap), pl.BlockSpec((tc, SSM_W), smap),
                  pl.BlockSpec((tc, SSM_W), row), pl.BlockSpec((tc, N_STATE), row), pl.BlockSpec((tc, N_STATE), row),
                  pl.BlockSpec((SUBLANES, N_STATE), prev), pl.BlockSpec((SUBLANES, N_STATE), prev)]
        + [_full(c.shape) for c in consts],
        out_specs=[pl.BlockSpec((tc, SSM_W), row), pl.BlockSpec((tc, SSM_W), row), _full((SSM_W, SSM_W))]
        + [_full(a.shape) for a in acc_shapes],
        scratch_shapes=[pltpu.VMEM((tc, N_STATE), F32), pltpu.VMEM((tc, N_STATE), F32),
                        pltpu.VMEM((SUBLANES, N_STATE), F32), pltpu.VMEM((SUBLANES, N_STATE), F32),
                        pltpu.VMEM((SSM_W, SSM_W), F32), pltpu.VMEM((4, tc, LANES), F32),
                        pltpu.VMEM((4, tc, LANES), F32),
                        pltpu.VMEM((SUBLANES, N_STATE), F32), pltpu.VMEM((SUBLANES, N_STATE), F32)],
        compiler_params=_params(2, VMEM_LIMIT_V7X),
    )(z, z, dmix, y, x_re, x_im, x_re, x_im, *consts)


def _dz_and_dx(x2, z, dqh, dkh, dvb, dga, du, dgs, dh1, w_in_g, g_mix, gq_t, gk_t, ones_bd, fold):
    t_tok = x2.shape[0]
    tm = min(256, t_tok)
    nt = t_tok // tm
    a_w = ATTN_W

    def head_norm_bwd(raw, d_hat, gain, scale, ones):
        def expand(narrow):
            hi = narrow.astype(BF16)
            lo = (narrow - hi.astype(F32)).astype(BF16)
            return _dot_nt(hi, ones) + _dot_nt(lo, ones)

        r = expand(lax.rsqrt(_hdot(raw * raw, ones) * (1.0 / HEAD_DIM) + EPS))
        n = raw * r
        a = d_hat * gain * scale
        d_raw = r * (a - n * expand(_hdot(a * n, ones) * (1.0 / HEAD_DIM)))
        return d_raw, jnp.sum(d_hat * n * scale, axis=0, keepdims=True)

    def body(x_ref, q_ref, k_ref, dq_ref, dk_ref, dv_ref, dga_ref, du_ref, dgs_ref, dh1_ref, w_ref, g_ref,
             gq_ref, gk_ref, ones_ref, fold_ref, dz_ref, gx_ref, dgm_ref, dgq_ref, dgk_ref, acc_q, acc_k):
        i = pl.program_id(0)

        @pl.when(i == 0)
        def _():
            dgm_ref[...] = jnp.zeros_like(dgm_ref)
            acc_q[...] = jnp.zeros_like(acc_q)
            acc_k[...] = jnp.zeros_like(acc_k)

        ones = ones_ref[...]
        dq, sq = head_norm_bwd(q_ref[...], dq_ref[...], gq_ref[...], HEAD_DIM ** -0.5, ones)
        dk, sk = head_norm_bwd(k_ref[...], dk_ref[...], gk_ref[...], 1.0, ones)
        acc_q[...] += jnp.broadcast_to(sq, acc_q.shape)
        acc_k[...] += jnp.broadcast_to(sk, acc_k.shape)
        parts = (dq, dk, dv_ref[...], dga_ref[...], du_ref[...], dgs_ref[...])
        for n, part in enumerate(parts):
            dz_ref[:, n * a_w:(n + 1) * a_w] = part.astype(BF16)
        dxn = jnp.zeros((tm, D_MODEL), F32)
        for j in range(N_DEV):
            dxn = dxn + _dot_nt(dz_ref[:, j * COL_W:(j + 1) * COL_W], w_ref[j])
        x = x_ref[...]
        r1 = lax.rsqrt(jnp.mean(x * x, axis=-1, keepdims=True) + EPS)
        xnorm = x * r1
        dgm_ref[...] += jnp.sum(dxn * xnorm, axis=0, keepdims=True)
        a = dxn * g_ref[...]
        gx_ref[...] = dh1_ref[...] + r1 * (a - xnorm * jnp.mean(a * xnorm, axis=-1, keepdims=True))

        @pl.when(i == nt - 1)
        def _():
            dgq_ref[...] = _hdot(acc_q[...], fold_ref[...])
            dgk_ref[...] = _hdot(acc_k[...], fold_ref[...])

    row = lambda i: (i, 0)
    col = lambda n: (lambda i: (i, n))
    s = jax.ShapeDtypeStruct
    half = pl.BlockSpec((tm, a_w), row)
    return pl.pallas_call(
        body, name="dz_dx", grid=(nt,),
        out_shape=[s((t_tok, IN_W), BF16), s((t_tok, D_MODEL), F32), s((1, D_MODEL), F32),
                   s((SUBLANES, HEAD_DIM), F32), s((SUBLANES, HEAD_DIM), F32)],
        in_specs=[pl.BlockSpec((tm, D_MODEL), row), pl.BlockSpec((tm, a_w), col(0)), pl.BlockSpec((tm, a_w), col(1)),
                  half, half, half, half, half, half, pl.BlockSpec((tm, D_MODEL), row),
                  _full(w_in_g.shape), _full(g_mix.shape), _full(gq_t.shape), _full(gk_t.shape),
                  _full(ones_bd.shape), _full(fold.shape)],
        out_specs=[pl.BlockSpec((tm, IN_W), row), pl.BlockSpec((tm, D_MODEL), row), _full((1, D_MODEL)),
                   _full((SUBLANES, HEAD_DIM)), _full((SUBLANES, HEAD_DIM))],
        scratch_shapes=[pltpu.VMEM((SUBLANES, a_w), F32), pltpu.VMEM((SUBLANES, a_w), F32)],
        compiler_params=_params(1, VMEM_LIMIT_V7X),
    )(x2, z, z, dqh, dkh, dvb, dga, du, dgs, dh1, w_in_g, g_mix, gq_t, gk_t, ones_bd, fold)


def _dw_in(xn, dz, glu_parts, smalls):
    t_tok = xn.shape[0]
    tk = min(1024, t_tok)
    nk = t_tok // tk
    rs = _ReduceScatter([glu_parts.shape])
    n_small = len(smalls)
    ag = _AllGather(n_small, cast=False)
    n_rs = len(rs.scratch(BF16))

    def place():
        x, y, c = lax.axis_index("x"), lax.axis_index("y"), lax.axis_index("c")
        return x, y, c, [(1 - x, y), (x, 1 - y), (1 - x, 1 - y)]

    def target(i):
        x, y, c, _ = place()
        n = i // 2
        px = jnp.where((n == 0) | (n == 2), 1 - x, x)
        py = jnp.where((n == 1) | (n == 2), 1 - y, y)
        pc = jnp.where(i % 2 == 0, 1 - c, c)
        return 4 * px + 2 * py + pc

    chunk, chunks = _row_chunks(D_MODEL)

    def body(*refs):
        (xn_ref, dz_ref, glu_ref), refs = refs[:3], refs[3:]
        small_refs, refs = list(refs[:n_small]), refs[n_small:]
        (gin_ref, gglu_ref), refs = refs[:2], refs[2:]
        gath_refs, refs = list(refs[:n_small]), refs[n_small:]
        (acc, stage, land, send_sems, recv_sems), rest = refs[:5], refs[5:]
        rs_scratch, ag_sems = rest[:n_rs], rest[n_rs:]
        i, k = pl.program_id(0), pl.program_id(1)
        x, y, c, chips = place()

        def push(slot, to):
            return pltpu.make_async_remote_copy(
                src_ref=stage.at[slot], dst_ref=land.at[slot], send_sem=send_sems.at[slot],
                recv_sem=recv_sems.at[slot], device_id=to, device_id_type=MESH)

        pushes = [push(n, (x, y, 1 - c)) for n in range(4)] + [push(4 + n, (*chips[n], c)) for n in range(3)]

        def staged(slot, plus=None):
            def put(s, carry):
                r = pl.ds(pl.multiple_of(s * chunk, chunk), chunk)
                val = acc[r, :]
                if plus is not None:
                    val = val + land[plus, r, :].astype(F32)
                stage[slot, r, :] = val.astype(BF16)
                return carry

            lax.fori_loop(0, chunks, put, 0)

        @pl.when((i == 0) & (k == 0))
        def _():
            rs.start([glu_ref], rs_scratch)
            ag.start(small_refs, gath_refs, ag_sems)

        @pl.when((i == N_DEV // 2) & (k == 0))
        def _():
            ag.forward(small_refs, gath_refs, ag_sems)

        @pl.when(k == 0)
        def _():
            acc[...] = jnp.zeros_like(acc)

        acc[...] += _dot_tn(xn_ref[...], dz_ref[...])

        for n in range(4):
            @pl.when((k == nk - 1) & (i == 2 * n))
            def _(n=n):
                staged(n)
                pushes[n].start()

        for n in range(3):
            @pl.when((k == nk - 1) & (i == 2 * n + 1))
            def _(n=n):
                pushes[n].wait_recv()
                staged(4 + n, plus=n)
                pushes[4 + n].start()

        @pl.when((k == nk - 1) & (i == N_DEV - 1))
        def _():
            for slot in range(3, N_DEV - 1):
                pushes[slot].wait_recv()

            def add(s, carry):
                r = pl.ds(pl.multiple_of(s * chunk, chunk), chunk)
                total = acc[r, :]
                for slot in range(3, N_DEV - 1):
                    total = total + land[slot, r, :].astype(F32)
                gin_ref[r, :] = total
                return carry

            lax.fori_loop(0, chunks, add, 0)
            for cp in pushes:
                cp.wait_send()
            rs.finish([glu_ref], rs_scratch, [gglu_ref])
            ag.finish(small_refs, gath_refs, ag_sems)

    any_spec = pl.BlockSpec(memory_space=pl.ANY)
    s = jax.ShapeDtypeStruct
    outs = pl.pallas_call(
        body, name="dw_in", grid=(N_DEV, nk),
        out_shape=[s((D_MODEL, COL_W), F32), s(glu_parts.shape[1:], F32)]
        + [s((N_DEV,) + a.shape, a.dtype) for a in smalls],
        in_specs=[pl.BlockSpec((tk, D_MODEL), lambda i, k: (k, 0)),
                  pl.BlockSpec((tk, COL_W), lambda i, k: (k, target(i))), any_spec] + [any_spec] * n_small,
        out_specs=[_full((D_MODEL, COL_W)), _full(glu_parts.shape[1:])] + [any_spec] * n_small,
        scratch_shapes=[pltpu.VMEM((D_MODEL, COL_W), F32), pltpu.VMEM((N_DEV - 1, D_MODEL, COL_W), BF16),
                        pltpu.VMEM((N_DEV - 1, D_MODEL, COL_W), BF16), pltpu.SemaphoreType.DMA((N_DEV - 1,)),
                        pltpu.SemaphoreType.DMA((N_DEV - 1,))] + rs.scratch(BF16) + ag.scratch(),
        compiler_params=_params(2, VMEM_LIMIT_V7X),
    )(xn, dz, glu_parts, *smalls)
    return outs[0], outs[1], outs[2:]


SMALL = ("mix_norm", "q_norm", "k_norm", "lambda_re", "lambda_im", "log_dt", "b_re", "b_im", "c_re", "c_im",
         "d_skip", "b_glu", "ple_norm")
BIG = ("w_in", "w_glu", "w_out", "w_ple_gate", "w_ple_proj")
WEIGHTS = ("mix_norm", "w_in", "q_norm", "k_norm", "lambda_re", "lambda_im", "log_dt", "b_re", "b_im", "c_re",
           "c_im", "d_skip", "w_glu", "b_glu", "w_out", "ple_norm", "w_ple_gate", "w_ple_proj")


def kernel(x, p, mix_norm, w_in, q_norm, k_norm, lambda_re, lambda_im, log_dt, b_re, b_im, c_re, c_im, d_skip, w_glu, b_glu, w_out, ple_norm, w_ple_gate, w_ple_proj, loss_target, m_mix_norm, m_w_in, m_q_norm, m_k_norm, m_lambda_re, m_lambda_im, m_log_dt, m_b_re, m_b_im, m_c_re, m_c_im, m_d_skip, m_w_glu, m_b_glu, m_w_out, m_ple_norm, m_w_ple_gate, m_w_ple_proj, v_mix_norm, v_w_in, v_q_norm, v_k_norm, v_lambda_re, v_lambda_im, v_log_dt, v_b_re, v_b_im, v_c_re, v_c_im, v_d_skip, v_w_glu, v_b_glu, v_w_out, v_ple_norm, v_w_ple_gate, v_w_ple_proj):
    env = dict(locals())
    w = {n: env[n] for n in WEIGHTS}
    m = {n: env["m_" + n] for n in WEIGHTS}
    v = {n: env["v_" + n] for n in WEIGHTS}
    nb, seq, _ = x.shape
    t_tok = nb * seq
    x2 = x.reshape(t_tok, D_MODEL)
    tg2 = loss_target.reshape(t_tok, D_MODEL)
    p2 = p.reshape(t_tok, PLE_DIM)

    shard2d = {"w_in": (D_MODEL, COL_W), "w_glu": (SSM_W // N_DEV, SSM_W), "w_out": (D_MODEL // N_DEV, D_MODEL),
               "w_ple_gate": (D_MODEL // N_DEV, D_MODEL), "w_ple_proj": (PLE_DIM, D_MODEL // N_DEV)}
    w_sh = [w[n].reshape(shard2d[n]) for n in BIG]

    g3 = (SSM_GROUPS, 1, SSM_STATE)
    lr3, li3 = lambda_re.reshape(g3), lambda_im.reshape(g3)
    dt3 = log_dt.reshape(SSM_GROUPS, 1, 1)
    btr = b_re[0].transpose(0, 2, 1)
    bti = b_im[0].transpose(0, 2, 1)
    a_re3, a_im3, bb_re, bb_im, cc_re, cc_im = _zoh_fwd(lr3, li3, dt3, btr, bti, c_re[0], c_im[0])
    a_re, a_im = a_re3.reshape(1, N_STATE), a_im3.reshape(1, N_STATE)

    ones_bd = _head_ones()
    fold = _head_fold()
    gq_t = jnp.tile(q_norm, (1, ATTN_W // HEAD_DIM))
    gk_t = jnp.tile(k_norm, (1, ATTN_W // HEAD_DIM))

    gq2 = jnp.tile(q_norm, (1, LANES // HEAD_DIM))
    gk2 = jnp.tile(k_norm, (1, LANES // HEAD_DIM))

    z, xn, w_in_g = _in_proj(x2, mix_norm, w_sh[0])
    (o, lse, ag, qh, kh), (w_glu_g, w_out_g, w_g_g, w_p_g) = _attn_fwd(z, gq2, gk2, nb, seq, w_sh[1:])
    w_glu_f = w_glu_g.reshape(SSM_W, SSM_W)
    w_out_f = w_out_g.reshape(D_MODEL, D_MODEL)
    w_g_f = w_g_g.reshape(D_MODEL, D_MODEL)
    x_re, x_im, y, sg = _ssm_fwd(z, a_re, a_im, bb_re, bb_im, cc_re, cc_im, d_skip, w_glu_f, b_glu, nb, seq)
    dmix, dh1, loss_t, d_ple, dw_out, dw_g, dw_p = _tail(x2, tg2, ag, sg, p2, w_out_f, w_g_f, w_p_g, ple_norm)

    early_parts = [dw_out.reshape(N_DEV, D_MODEL // N_DEV, D_MODEL), dw_g.reshape(N_DEV, D_MODEL // N_DEV, D_MODEL),
                   dw_p]
    (dqh, dkh, dvb, dga), (g_out, g_g, g_p) = _attn_bwd(qh, kh, z, o, lse, dmix, nb, seq, early_parts)
    (du, dgs, dw_glu, d_bglu, d_dskip, da_re, da_im, dbb_re, dbb_im, dcc_re, dcc_im) = _ssm_bwd(
        z, dmix, y, x_re, x_im, a_re, a_im, bb_re, bb_im, cc_re, cc_im, d_skip, w_glu_f, b_glu, nb, seq)
    d_lr, d_li, d_dt, d_btr, d_bti, d_cr, d_ci = _zoh_bwd(
        lr3, li3, dt3, btr, bti, da_re.reshape(g3), da_im.reshape(g3), dbb_re, dbb_im, dcc_re, dcc_im, fold)

    swapped = ("b_re", "b_im")

    def to_own(n, a):
        a = a.reshape(a.shape[1:]) if a.ndim > 2 else a
        return a.transpose(0, 2, 1) if n in swapped else a

    def from_own(n, a):
        a = a.transpose(0, 2, 1) if n in swapped else a
        return a.reshape(w[n].shape)

    own = {n: to_own(n, w[n]).shape for n in SMALL}
    dz, gx, d_mix, d_gq, d_gk = _dz_and_dx(x2, z, dqh, dkh, dvb, dga, du, dgs, dh1, w_in_g, mix_norm, gq_t, gk_t,
                                           ones_bd, fold)
    small_g = {"mix_norm": d_mix, "q_norm": d_gq[0:1], "k_norm": d_gk[0:1], "lambda_re": d_lr, "lambda_im": d_li,
               "log_dt": d_dt, "b_re": d_btr, "b_im": d_bti, "c_re": d_cr, "c_im": d_ci,
               "d_skip": d_dskip, "b_glu": d_bglu, "ple_norm": d_ple}
    g_in, g_glu, gathered = _dw_in(xn, dz, dw_glu.reshape(N_DEV, SSM_W // N_DEV, SSM_W),
                                   [small_g[n].reshape(own[n]) for n in SMALL] + [loss_t])
    g_sh = [g_in, g_glu, g_out, g_g, g_p]
    d_sh, m_sh, v_sh = _adamw_shards(g_sh, w_sh, [m[n].reshape(shard2d[n]) for n in BIG],
                                     [v[n].reshape(shard2d[n]) for n in BIG])

    *g_small, loss_sum = _small_sum(list(gathered))
    d_small, m_small, v_small = _adamw_small(
        g_small, *[[to_own(n, src[n]) for n in SMALL] for src in (w, m, v)])

    grads, deltas, new_m, new_v = {}, {}, {}, {}
    for dst, arrs in ((grads, g_small), (deltas, d_small), (new_m, m_small), (new_v, v_small)):
        for n, a in zip(SMALL, arrs):
            dst[n] = from_own(n, a)
    for i, n in enumerate(BIG):
        grads[n] = g_sh[i].reshape(w[n].shape)
        deltas[n] = d_sh[i].reshape(w[n].shape)
        new_m[n] = m_sh[i].reshape(w[n].shape)
        new_v[n] = v_sh[i].reshape(w[n].shape)

    loss = loss_sum[0, 0]
    return (loss, gx.reshape(x.shape), *[grads[n] for n in WEIGHTS], *[deltas[n] for n in WEIGHTS],
            *[new_m[n] for n in WEIGHTS], *[new_v[n] for n in WEIGHTS])
```

```python
import math

import numpy as np
import jax
import jax.numpy as jnp
from jax import lax
from jax.experimental import pallas as pl
from jax.experimental.pallas import tpu as pltpu

F32 = jnp.float32
BF16 = jnp.bfloat16
MESH = pl.DeviceIdType.MESH
AXES = ("x", "y", "c")
N_DEV = 8

D_MODEL = 1024
HEAD_DIM = 64
ATTN_W = 512
SSM_W = 512
SSM_GROUPS = 32
SSM_GROUP = 16
SSM_STATE = 64
N_STATE = SSM_GROUPS * SSM_STATE
PLE_DIM = 256
IN_W = 3072
COL_W = IN_W // N_DEV
DILATED = ((128, 1), (512, 4), (2048, 16))
EPS = 1e-6
INV_SQRT2 = 1.0 / math.sqrt(2.0)
INV_SQRT_2PI = 1.0 / math.sqrt(2.0 * math.pi)

ADAM_LR, ADAM_B1, ADAM_B2, ADAM_EPS, ADAM_WD, ADAM_STEP = 0.001, 0.9, 0.999, 1e-08, 0.01, 10

VMEM_LIMIT_V7X = 56 * 1024 * 1024
SUBLANES = 8
LANES = 128


def _params(n_axes=None, vmem=None):
    kw = {}
    if n_axes:
        kw["dimension_semantics"] = ("arbitrary",) * n_axes
    if vmem:
        kw["vmem_limit_bytes"] = vmem
    return pltpu.CompilerParams(**kw)


def _dot(a, b):
    return jnp.dot(a, b, preferred_element_type=F32)


def _dot_nt(a, b):
    return lax.dot_general(a, b, (((1,), (1,)), ((), ())), preferred_element_type=F32)


def _dot_tn(a, b):
    return lax.dot_general(a, b, (((0,), (0,)), ((), ())), preferred_element_type=F32)


def _hdot(a, ones):
    hi = a.astype(BF16)
    lo = (a - hi.astype(F32)).astype(BF16)
    return _dot(hi, ones) + _dot(lo, ones)


def _sig(x):
    return 1.0 / (1.0 + jnp.exp(-x))


def _gelu_and_grad(y):
    cdf = 0.5 * (1.0 + lax.erf(y * INV_SQRT2))
    pdf = jnp.exp(-0.5 * y * y) * INV_SQRT_2PI
    return y * cdf, cdf + y * pdf


def _vmem():
    return pl.BlockSpec(memory_space=pltpu.VMEM)


def _full(shape):
    nd = len(shape)
    return pl.BlockSpec(shape, lambda *_: (0,) * nd)


class _AllGather:
    def __init__(self, n, cast):
        self.n, self.cast = n, cast

    def scratch(self):
        n = self.n
        return [pltpu.SemaphoreType.DMA((7 * n,)), pltpu.SemaphoreType.DMA((7 * n,)), pltpu.SemaphoreType.DMA((n,))]

    def _plan(self, src_refs, out_refs, sems):
        send_sems, recv_sems, own_sems = sems
        x, y, c = lax.axis_index("x"), lax.axis_index("y"), lax.axis_index("c")
        me, sibling = (x, y, c), (x, y, 1 - c)
        chips = [(1 - x, y), (x, 1 - y), (1 - x, 1 - y)]

        def idx(px, py, pc):
            return 4 * px + 2 * py + pc

        def copy(i, k, block, to, own_src=False):
            ref = out_refs[i].at[idx(*block)]
            return pltpu.make_async_remote_copy(
                src_ref=src_refs[i] if own_src and not self.cast else ref, dst_ref=ref,
                send_sem=send_sems.at[7 * i + k], recv_sem=recv_sems.at[7 * i + k],
                device_id=to, device_id_type=MESH)

        first, passed, arrive_ici, arrive_d2d, own = [], [], [], [], []
        for i in range(self.n):
            first.append(copy(i, 0, me, sibling, own_src=True))
            first += [copy(i, 1 + j, me, (*chip, c), own_src=True) for j, chip in enumerate(chips)]
            arrive_ici += [copy(i, 1 + j, (*chip, c), me) for j, chip in enumerate(chips)]
            passed += [copy(i, 4 + j, (*chip, c), sibling) for j, chip in enumerate(chips)]
            arrive_d2d.append(copy(i, 0, sibling, me))
            arrive_d2d += [copy(i, 4 + j, (*chip, 1 - c), me) for j, chip in enumerate(chips)]
            if not self.cast:
                own.append(pltpu.make_async_copy(src_refs[i], out_refs[i].at[idx(*me)], own_sems.at[i]))
        return idx(*me), first, passed, arrive_ici, arrive_d2d, own

    def start(self, src_refs, out_refs, sems):
        my, first, _, _, _, own = self._plan(src_refs, out_refs, sems)
        if self.cast:
            for i in range(self.n):
                out_refs[i][my] = src_refs[i][...].astype(out_refs[i].dtype)
        for cp in own + first:
            cp.start()

    def forward(self, src_refs, out_refs, sems):
        _, _, passed, arrive_ici, _, _ = self._plan(src_refs, out_refs, sems)
        for cp in arrive_ici:
            cp.wait_recv()
        for cp in passed:
            cp.start()

    def finish(self, src_refs, out_refs, sems):
        _, first, passed, _, arrive_d2d, own = self._plan(src_refs, out_refs, sems)
        for cp in own:
            cp.wait()
        for cp in arrive_d2d:
            cp.wait_recv()
        for cp in first + passed:
            cp.wait_send()


class _HostedGather:
    def __init__(self, shards):
        self.shapes = [(N_DEV,) + a.shape for a in shards]
        self.n = len(shards)
        self.ag = _AllGather(self.n, cast=True)

    def out_shape(self):
        return [jax.ShapeDtypeStruct(s, BF16) for s in self.shapes]

    def scratch(self):
        return [pltpu.VMEM(s, BF16) for s in self.shapes] + self.ag.scratch() + [pltpu.SemaphoreType.DMA((self.n,))]

    def _split(self, scratch):
        return scratch[:self.n], scratch[self.n:-1], scratch[-1]

    def start(self, src_refs, scratch):
        land, sems, _ = self._split(scratch)
        self.ag.start(src_refs, land, sems)

    def forward(self, src_refs, scratch):
        land, sems, _ = self._split(scratch)
        self.ag.forward(src_refs, land, sems)

    def finish(self, src_refs, scratch, out_refs):
        land, sems, out_sems = self._split(scratch)
        self.ag.finish(src_refs, land, sems)
        outs = [pltpu.make_async_copy(land[n], out_refs[n], out_sems.at[n]) for n in range(self.n)]
        for cp in outs:
            cp.start()
        for cp in outs:
            cp.wait()


def _all_gather(shards, out_dtypes, name):
    n = len(shards)
    ag = _AllGather(n, cast=True)

    def body(*refs):
        in_refs, out_refs, sems = refs[:n], refs[n:2 * n], refs[2 * n:]
        ag.start(in_refs, out_refs, sems)
        ag.forward(in_refs, out_refs, sems)
        ag.finish(in_refs, out_refs, sems)

    return pl.pallas_call(
        body, name=name,
        out_shape=[jax.ShapeDtypeStruct((N_DEV,) + s.shape, dt) for s, dt in zip(shards, out_dtypes)],
        in_specs=[_vmem()] * n, out_specs=[_vmem()] * n,
        scratch_shapes=ag.scratch(),
        compiler_params=_params(vmem=VMEM_LIMIT_V7X),
    )(*shards)


def _row_chunks(rows):
    chunk = 64 if rows % 64 == 0 else rows
    return chunk, rows // chunk


class _ReduceScatter:
    def __init__(self, shapes):
        self.shapes = shapes
        self.n = len(shapes)

    def scratch(self, dtype):
        return ([pltpu.VMEM(s, dtype) for s in self.shapes]
                + [pltpu.SemaphoreType.DMA((7 * self.n,)), pltpu.SemaphoreType.DMA((7 * self.n,)),
                   pltpu.SemaphoreType.DMA((self.n,))])

    def _copies(self, in_refs, land_refs, send_sems, recv_sems, own_sems):
        x, y, c = lax.axis_index("x"), lax.axis_index("y"), lax.axis_index("c")
        remote, own = [], []
        for i in range(self.n):
            for m in range(1, N_DEV):
                px = 1 - x if m & 4 else x
                py = 1 - y if m & 2 else y
                pc = 1 - c if m & 1 else c
                remote.append(pltpu.make_async_remote_copy(
                    src_ref=in_refs[i].at[4 * px + 2 * py + pc], dst_ref=land_refs[i].at[m - 1],
                    send_sem=send_sems.at[7 * i + m - 1], recv_sem=recv_sems.at[7 * i + m - 1],
                    device_id=(px, py, pc), device_id_type=MESH))
            own.append(pltpu.make_async_copy(in_refs[i].at[4 * x + 2 * y + c], land_refs[i].at[N_DEV - 1],
                                             own_sems.at[i]))
        return remote, own

    def start(self, in_refs, scratch):
        remote, own = self._copies(in_refs, scratch[:self.n], *scratch[self.n:])
        for cp in remote + own:
            cp.start()

    def finish(self, in_refs, scratch, out_refs):
        land_refs = scratch[:self.n]
        remote, own = self._copies(in_refs, land_refs, *scratch[self.n:])
        for cp in own:
            cp.wait()
        for cp in remote:
            cp.wait_recv()
        for i in range(self.n):
            chunk, steps = _row_chunks(self.shapes[i][1])

            def step(s, carry, i=i, chunk=chunk):
                r = pl.ds(pl.multiple_of(s * chunk, chunk), chunk)
                acc = land_refs[i][N_DEV - 1, r, :].astype(F32)
                for m in range(1, N_DEV):
                    acc = acc + land_refs[i][m - 1, r, :].astype(F32)
                out_refs[i][r, :] = acc
                return carry

            lax.fori_loop(0, steps, step, 0)
        for cp in remote:
            cp.wait_send()


def _reduce_scatter(parts, name):
    n = len(parts)
    rs = _ReduceScatter([p.shape for p in parts])

    def body(*refs):
        in_refs, out_refs, scratch = refs[:n], refs[n:2 * n], refs[2 * n:]
        rs.start(in_refs, scratch)
        rs.finish(in_refs, scratch, out_refs)

    return pl.pallas_call(
        body, name=name,
        out_shape=[jax.ShapeDtypeStruct(p.shape[1:], F32) for p in parts],
        in_specs=[_vmem()] * n, out_specs=[_vmem()] * n,
        scratch_shapes=rs.scratch(parts[0].dtype),
        compiler_params=_params(vmem=VMEM_LIMIT_V7X),
    )(*parts)


def _adamw_math(w, g, m, v):
    m = ADAM_B1 * m + (1.0 - ADAM_B1) * g
    v = ADAM_B2 * v + (1.0 - ADAM_B2) * (g * g)
    m_hat = m / (1.0 - ADAM_B1 ** ADAM_STEP)
    v_hat = v / (1.0 - ADAM_B2 ** ADAM_STEP)
    delta = -ADAM_LR * (m_hat / (jnp.sqrt(v_hat) + ADAM_EPS) + ADAM_WD * w)
    return delta, m, v


def _adamw_shards(gs, ws, ms, vs):
    n = len(gs)

    def body(*refs):
        g_refs, w_refs, m_refs, v_refs = (refs[k * n:(k + 1) * n] for k in range(4))
        d_out, m_out, v_out = (refs[(4 + k) * n:(5 + k) * n] for k in range(3))
        for i in range(n):
            chunk, steps = _row_chunks(gs[i].shape[0])

            def step(s, carry, i=i, chunk=chunk):
                r = pl.ds(pl.multiple_of(s * chunk, chunk), chunk)
                d, m, v = _adamw_math(w_refs[i][r, :], g_refs[i][r, :], m_refs[i][r, :], v_refs[i][r, :])
                d_out[i][r, :] = d
                m_out[i][r, :] = m
                v_out[i][r, :] = v
                return carry

            lax.fori_loop(0, steps, step, 0)

    shapes = [jax.ShapeDtypeStruct(g.shape, F32) for g in gs]
    outs = pl.pallas_call(
        body, name="adamw_shards", out_shape=shapes * 3,
        in_specs=[_vmem()] * (4 * n), out_specs=[_vmem()] * (3 * n),
        compiler_params=_params(vmem=VMEM_LIMIT_V7X),
    )(*gs, *ws, *ms, *vs)
    return outs[:n], outs[n:2 * n], outs[2 * n:]


def _small_sum(gathered):
    n = len(gathered)

    def body(*refs):
        ga_refs, out_refs = refs[:n], refs[n:]
        for i in range(n):
            def total(idx, i=i):
                g = ga_refs[i][(0,) + idx].astype(F32)
                for j in range(1, N_DEV):
                    g = g + ga_refs[i][(j,) + idx].astype(F32)
                out_refs[i][idx] = g

            if len(gathered[i].shape) == 4:
                def step(s, carry, total=total):
                    total((s,))
                    return carry

                lax.fori_loop(0, gathered[i].shape[1], step, 0)
            else:
                total((Ellipsis,))

    return pl.pallas_call(
        body, name="small_sum", out_shape=[jax.ShapeDtypeStruct(g.shape[1:], F32) for g in gathered],
        in_specs=[_vmem()] * n, out_specs=[_vmem()] * n,
        compiler_params=_params(vmem=VMEM_LIMIT_V7X),
    )(*gathered)


def _adamw_small(gs, ws, ms, vs):
    n = len(gs)

    def body(*refs):
        g_refs, w_refs, m_refs, v_refs = (refs[k * n:(k + 1) * n] for k in range(4))
        d_out, m_out, v_out = (refs[(4 + k) * n:(5 + k) * n] for k in range(3))
        for i in range(n):
            def update(idx, i=i):
                d, mm, vv = _adamw_math(w_refs[i][idx], g_refs[i][idx], m_refs[i][idx], v_refs[i][idx])
                d_out[i][idx] = d
                m_out[i][idx] = mm
                v_out[i][idx] = vv

            if len(gs[i].shape) == 3:
                def step(s, carry, update=update):
                    update(s)
                    return carry

                lax.fori_loop(0, gs[i].shape[0], step, 0)
            else:
                update(Ellipsis)

    shapes = [jax.ShapeDtypeStruct(g.shape, F32) for g in gs]
    outs = pl.pallas_call(
        body, name="adamw_small", out_shape=shapes * 3,
        in_specs=[_vmem()] * (4 * n), out_specs=[_vmem()] * (3 * n),
        compiler_params=_params(vmem=VMEM_LIMIT_V7X),
    )(*gs, *ws, *ms, *vs)
    return outs[:n], outs[n:2 * n], outs[2 * n:]


def _zoh(lr, li, logdt, btr, bti):
    dt = jnp.exp(logdt)
    mag = jnp.exp(lr * dt)
    th = li * dt
    ar = mag * jnp.cos(th)
    ai = mag * jnp.sin(th)
    den = lr * lr + li * li
    nr = ar - 1.0
    cr = (nr * lr + ai * li) / den
    ci = (ai * lr - nr * li) / den
    return ar, ai, cr * btr - ci * bti, cr * bti + ci * btr


BD_GROUPS = 8
BD_ROWS = BD_GROUPS * SSM_GROUP
BD_COLS = BD_GROUPS * SSM_STATE
N_BD = SSM_GROUPS // BD_GROUPS


def _bd_mask():
    r = lax.broadcasted_iota(jnp.int32, (BD_ROWS, BD_COLS), 0) // SSM_GROUP
    c = lax.broadcasted_iota(jnp.int32, (BD_ROWS, BD_COLS), 1) // SSM_STATE
    return r == c


def _blockdiag_store(out_ref, t):
    mask = _bd_mask()
    for j in range(N_BD):
        rows = t[j * BD_GROUPS:(j + 1) * BD_GROUPS].reshape(BD_ROWS, SSM_STATE)
        out_ref[j] = jnp.where(mask, jnp.tile(rows, (1, BD_GROUPS)), 0.0).astype(out_ref.dtype)


def _blockdiag_load(m_ref, fold):
    mask = _bd_mask()
    parts = [_hdot(jnp.where(mask, m_ref[j], 0.0), fold).reshape(BD_GROUPS, SSM_GROUP, SSM_STATE)
             for j in range(N_BD)]
    return jnp.concatenate(parts, axis=0)


def _zoh_fwd(lr, li, logdt, btr, bti, c_re, c_im):
    def body(lr_ref, li_ref, dt_ref, br_ref, bi_ref, cr_ref, ci_ref, ar_ref, ai_ref, bbr_ref, bbi_ref, ccr_ref,
             cci_ref):
        ar, ai, bbr, bbi = _zoh(lr_ref[...], li_ref[...], dt_ref[...], br_ref[...], bi_ref[...])
        ar_ref[...] = ar
        ai_ref[...] = ai
        _blockdiag_store(bbr_ref, bbr)
        _blockdiag_store(bbi_ref, bbi)
        _blockdiag_store(ccr_ref, cr_ref[...])
        _blockdiag_store(cci_ref, ci_ref[...])

    s = jax.ShapeDtypeStruct
    bd = s((N_BD, BD_ROWS, BD_COLS), BF16)
    return pl.pallas_call(
        body, name="zoh_fwd", out_shape=[s(lr.shape, F32), s(lr.shape, F32), bd, bd, bd, bd],
        in_specs=[_vmem()] * 7, out_specs=[_vmem()] * 6,
    )(lr, li, logdt, btr, bti, c_re, c_im)


def _zoh_bwd(lr, li, logdt, btr, bti, dar, dai, dbb_re, dbb_im, dcc_re, dcc_im, fold):
    def body(lr_ref, li_ref, dt_ref, br_ref, bi_ref, dar_ref, dai_ref, dbbr_ref, dbbi_ref, dccr_ref, dcci_ref,
             fold_ref, glr_ref, gli_ref, gdt_ref, gbr_ref, gbi_ref, gcr_ref, gci_ref):
        fold_m = fold_ref[...]
        _, vjp = jax.vjp(_zoh, lr_ref[...], li_ref[...], dt_ref[...], br_ref[...], bi_ref[...])
        glr, gli, gdt, gbr, gbi = vjp((dar_ref[...], dai_ref[...], _blockdiag_load(dbbr_ref, fold_m),
                                       _blockdiag_load(dbbi_ref, fold_m)))
        glr_ref[...] = glr
        gli_ref[...] = gli
        gdt_ref[...] = gdt
        gbr_ref[...] = gbr.astype(BF16)
        gbi_ref[...] = gbi.astype(BF16)
        gcr_ref[...] = _blockdiag_load(dccr_ref, fold_m).astype(BF16)
        gci_ref[...] = _blockdiag_load(dcci_ref, fold_m).astype(BF16)

    s = jax.ShapeDtypeStruct
    return pl.pallas_call(
        body, name="zoh_bwd",
        out_shape=[s(lr.shape, F32), s(lr.shape, F32), s(logdt.shape, F32)] + [s(btr.shape, BF16)] * 4,
        in_specs=[_vmem()] * 12, out_specs=[_vmem()] * 7,
    )(lr, li, logdt, btr, bti, dar, dai, dbb_re, dbb_im, dcc_re, dcc_im, fold)


def _head_ones():
    r = np.arange(ATTN_W) // HEAD_DIM
    return jnp.asarray(r[:, None] == r[None, :], dtype=BF16)


def _head_fold():
    return jnp.asarray(np.tile(np.eye(HEAD_DIM), (ATTN_W // HEAD_DIM, 1)), dtype=BF16)


def _in_proj(x2, g_mix, w_in_sh):
    t_tok = x2.shape[0]
    tm = min(1024, t_tok)
    nt = t_tok // tm
    ag_w = _AllGather(1, cast=True)
    n_sem = len(ag_w.scratch())

    def owner(i):
        x, y, c = lax.axis_index("x"), lax.axis_index("y"), lax.axis_index("c")
        rel = i // 2
        px = jnp.where((rel == 1) | (rel == 3), 1 - x, x)
        py = jnp.where((rel == 2) | (rel == 3), 1 - y, y)
        pc = jnp.where(i % 2 == 1, 1 - c, c)
        return 4 * px + 2 * py + pc

    def body(*refs):
        x_ref, g_ref, w_ref, z_ref, xn_ref, wg_ref, xn_scr, w_land = refs[:8]
        sems_w, out_sem = refs[8:8 + n_sem], refs[8 + n_sem]
        i, t = pl.program_id(0), pl.program_id(1)
        _, first, passed, arrive_ici, arrive_d2d, _ = ag_w._plan([w_ref], [w_land], sems_w)

        @pl.when((i == 0) & (t == 0))
        def _():
            ag_w.start([w_ref], [w_land], sems_w)

        @pl.when((i == 1) & (t == 0))
        def _():
            arrive_d2d[0].wait_recv()

        for n in range(3):
            @pl.when((i == 2 + 2 * n) & (t == 0))
            def _(n=n):
                arrive_ici[n].wait_recv()
                passed[n].start()

            @pl.when((i == 3 + 2 * n) & (t == 0))
            def _(n=n):
                arrive_d2d[1 + n].wait_recv()

        @pl.when(i == 0)
        def _():
            x = x_ref[...]
            r = lax.rsqrt(jnp.mean(x * x, axis=-1, keepdims=True) + EPS)
            xn = (x * r * g_ref[...]).astype(BF16)
            xn_ref[...] = xn
            xn_scr[t] = xn

        z_ref[...] = _dot(xn_scr[t], w_land[owner(i)])

        @pl.when((i == N_DEV - 1) & (t == nt - 1))
        def _():
            for cp in first + passed:
                cp.wait_send()
            out = pltpu.make_async_copy(w_land, wg_ref, out_sem)
            out.start()
            out.wait()

    s = jax.ShapeDtypeStruct
    xmap = lambda i, t: (jnp.where(i == 0, t, nt - 1), 0)
    gathered = s((N_DEV,) + w_in_sh.shape, BF16)
    return pl.pallas_call(
        body, name="in_proj", grid=(N_DEV, nt),
        out_shape=[s((t_tok, IN_W), F32), s((t_tok, D_MODEL), BF16), gathered],
        in_specs=[pl.BlockSpec((tm, D_MODEL), xmap), _full(g_mix.shape), _full(w_in_sh.shape)],
        out_specs=[pl.BlockSpec((tm, COL_W), lambda i, t: (t, owner(i))), pl.BlockSpec((tm, D_MODEL), xmap),
                   pl.BlockSpec(memory_space=pl.ANY)],
        scratch_shapes=[pltpu.VMEM((nt, tm, D_MODEL), BF16), pltpu.VMEM(gathered.shape, BF16)] + ag_w.scratch()
        + [pltpu.SemaphoreType.DMA],
        compiler_params=_params(2, VMEM_LIMIT_V7X),
    )(x2, g_mix, w_in_sh)


TQ = 128
NEG = -1e30


def _head_col(t, lm):
    return jnp.max(jnp.where(lm, t, NEG), axis=-1, keepdims=True)


def _head_masks():
    lane = lax.broadcasted_iota(jnp.int32, (1, 1, LANES), 2)
    return [(lane // HEAD_DIM) == h for h in range(LANES // HEAD_DIM)]


def _stack_heads(t3, lms):
    return jnp.concatenate([jnp.where(lm, t3, jnp.zeros_like(t3)) for lm in lms], axis=1)


def _unstack_heads(t2, lms, tq):
    out = t2[:, :tq]
    for h in range(1, len(lms)):
        out = jnp.where(lms[h], t2[:, h * tq:(h + 1) * tq], out)
    return out


def _gather_classes(ref, dil, nt, tq, dtype):
    length = nt * tq
    if dil == 1:
        return ref[...].astype(dtype).reshape(nt, tq, LANES)
    parts = [ref[pl.ds(r, length, stride=dil), :].astype(dtype).reshape(nt, tq, LANES) for r in range(dil)]
    return jnp.concatenate(parts, axis=0)


def _scatter_classes(ref, val, dil, nt, tq, add):
    length = nt * tq
    for r in range(dil):
        rows = pl.ds(r, length, stride=dil) if dil > 1 else slice(None)
        part = val[r * nt:(r + 1) * nt].reshape(length, LANES)
        ref[rows, :] = ref[rows, :] + part if add else part


def _with_prev_tile(t3, dil, nt):
    parts = []
    for r in range(dil):
        t = t3[r * nt:(r + 1) * nt]
        parts.append(jnp.concatenate([t[:1], t[:-1]], axis=0))
    prev = parts[0] if dil == 1 else jnp.concatenate(parts, axis=0)
    return jnp.concatenate([prev, t3], axis=1)


def _band_valid(dil, nt, tq):
    if nt == 1:
        shape = (dil, tq, tq)
        return lax.broadcasted_iota(jnp.int32, shape, 1) >= lax.broadcasted_iota(jnp.int32, shape, 2)
    shape = (dil * nt, tq, 2 * tq)
    b = lax.broadcasted_iota(jnp.int32, shape, 0)
    c = lax.broadcasted_iota(jnp.int32, shape, 2)
    d = tq + lax.broadcasted_iota(jnp.int32, shape, 1) - c
    return (d >= 0) & (d <= tq) & (((b & (nt - 1)) != 0) | (c >= tq))


def _window_tiling(seq, window, dil):
    length = seq // dil
    tq = min(TQ, length)
    nt = length // tq
    assert length % tq == 0 and nt & (nt - 1) == 0 and (nt == 1 or window == tq * dil)
    return nt, tq


def _bqk(a, b):
    return jnp.einsum("bqd,bkd->bqk", a, b, preferred_element_type=F32)


def _bqd(a, b):
    return jnp.einsum("bqk,bkd->bqd", a, b, preferred_element_type=F32)


def _bkd(a, b):
    return jnp.einsum("bqk,bqd->bkd", a, b, preferred_element_type=F32)


def _qk_hat(q_ref, k_ref, gq_ref, gk_ref):
    lane = lax.broadcasted_iota(jnp.int32, (1, LANES), 1)

    def norm(raw, gain, scale):
        sq = raw * raw
        r = jnp.zeros_like(raw)
        for h in range(LANES // HEAD_DIM):
            lm = (lane // HEAD_DIM) == h
            ms = jnp.sum(jnp.where(lm, sq, 0.0), axis=-1, keepdims=True) * (1.0 / HEAD_DIM)
            r = jnp.where(lm, lax.rsqrt(ms + EPS), r)
        return raw * r * gain * scale

    return norm(q_ref[...], gq_ref[...], HEAD_DIM ** -0.5), norm(k_ref[...], gk_ref[...], 1.0)


def _zblock(seq, group):
    return pl.BlockSpec((seq, LANES), lambda b, hp: (b, group * (ATTN_W // LANES) + hp))


def _attn_fwd(z, gq2, gk2, nb, seq, late_sh):
    t_tok = nb * seq
    n_win = len(DILATED)
    host = _HostedGather(late_sh)
    n_late = host.n
    n_steps = (nb, ATTN_W // LANES)

    def body(*refs):
        (q_ref, k_ref, v_ref, ga_ref, gq_ref, gk_ref), refs = refs[:6], refs[6:]
        late_refs, refs = refs[:n_late], refs[n_late:]
        (o_ref, l_ref, ag_ref, qh_ref, kh_ref), refs = refs[:5], refs[5:]
        lateg_refs, refs = refs[:n_late], refs[n_late:]
        (qf, kf, oc, lc), host_scratch = refs[:4], refs[4:]
        step = pl.program_id(0) * n_steps[1] + pl.program_id(1)
        total = n_steps[0] * n_steps[1]

        @pl.when(step == 0)
        def _():
            host.start(late_refs, host_scratch)

        @pl.when(step == total // 2)
        def _():
            host.forward(late_refs, host_scratch)

        q_hat, k_hat = _qk_hat(q_ref, k_ref, gq_ref, gk_ref)
        qh_ref[...] = q_hat.astype(BF16)
        kh_ref[...] = k_hat.astype(BF16)
        qf[...] = q_hat
        kf[...] = k_hat
        lms = _head_masks()
        for w, (window, dil) in enumerate(DILATED):
            nt, tq = _window_tiling(seq, window, dil)
            q3 = _gather_classes(qf, dil, nt, tq, BF16)
            k3 = _gather_classes(kf, dil, nt, tq, BF16)
            v3 = _gather_classes(v_ref, dil, nt, tq, BF16)
            if nt > 1:
                k3, v3 = _with_prev_tile(k3, dil, nt), _with_prev_tile(v3, dil, nt)
            valid = _band_valid(dil, nt, tq)
            valid = jnp.concatenate([valid] * len(lms), axis=1)
            s = _bqk(_stack_heads(q3, lms), k3)
            m = jnp.max(jnp.where(valid, s, NEG), axis=-1, keepdims=True)
            p = jnp.where(valid, jnp.exp(s - m), 0.0)
            den = jnp.sum(p, axis=-1, keepdims=True)
            o = _unstack_heads(_bqd(p.astype(BF16), v3) / den, lms, tq)
            lse = _unstack_heads(jnp.broadcast_to(m + jnp.log(den), s.shape[:2] + (LANES,)), lms, tq)
            _scatter_classes(oc.at[w], o, dil, nt, tq, add=False)
            _scatter_classes(lc.at[w], lse, dil, nt, tq, add=False)
        mx = lc[0]
        for w in range(1, n_win):
            mx = jnp.maximum(mx, lc[w])
        tot = jnp.zeros_like(mx)
        o = jnp.zeros_like(mx)
        for w in range(n_win):
            e = jnp.exp(lc[w] - mx)
            tot = tot + e
            o = o + e * oc[w]
        o = o / tot
        o_ref[...] = o
        l_ref[...] = mx + jnp.log(tot)
        ga = ga_ref[...]
        ag_ref[...] = (o * ga * _sig(ga)).astype(BF16)

        @pl.when(step == total - 1)
        def _():
            host.finish(late_refs, host_scratch, lateg_refs)

    blk = pl.BlockSpec((seq, LANES), lambda b, hp: (b, hp))
    s = jax.ShapeDtypeStruct
    outs = pl.pallas_call(
        body, name="attn_fwd", grid=n_steps,
        out_shape=[s((t_tok, ATTN_W), F32), s((t_tok, ATTN_W), F32)] + [s((t_tok, ATTN_W), BF16)] * 3
        + host.out_shape(),
        in_specs=[_zblock(seq, 0), _zblock(seq, 1), _zblock(seq, 2), _zblock(seq, 3), _full(gq2.shape),
                  _full(gk2.shape)] + [_full(a.shape) for a in late_sh],
        out_specs=[blk] * 5 + [pl.BlockSpec(memory_space=pl.ANY)] * n_late,
        scratch_shapes=[pltpu.VMEM((seq, LANES), F32)] * 2 + [pltpu.VMEM((n_win, seq, LANES), F32)] * 2
        + host.scratch(),
        compiler_params=_params(2, VMEM_LIMIT_V7X),
    )(z, z, z, z, gq2, gk2, *late_sh)
    return outs[:5], outs[5:]


SCAN_COLS = 512


def _to_segments(dst_ref, val):
    seg = val.shape[0] // SUBLANES
    for n in range(dst_ref.shape[0]):
        for s in range(SUBLANES):
            dst_ref[n, pl.ds(s, seg, stride=SUBLANES), :] = val[s * seg:(s + 1) * seg, n * LANES:(n + 1) * LANES]


def _from_segments(src_ref):
    seg = src_ref.shape[1] // SUBLANES
    return jnp.concatenate(
        [jnp.concatenate([src_ref[n, pl.ds(s, seg, stride=SUBLANES), :] for s in range(SUBLANES)], axis=0)
         for n in range(src_ref.shape[0])], axis=1)


def _scan_chunk(re_ref, im_ref, a_re_ref, a_im_ref, carry_re, carry_im, rows, reverse, visit=None):
    seg = rows // SUBLANES
    assert seg & (seg - 1) == 0
    rowi = lax.broadcasted_iota(jnp.int32, (SUBLANES, SCAN_COLS), 0)
    edge = (SUBLANES - 1) if reverse else 0
    last = 0 if reverse else SUBLANES - 1
    at_edge = rowi == edge

    def cmul(ar, ai, br, bi):
        return ar * br - ai * bi, ar * bi + ai * br

    for c0 in range(0, N_STATE, SCAN_COLS):
        cols = slice(c0, c0 + SCAN_COLS)
        a1r = jnp.broadcast_to(a_re_ref[:, cols], (SUBLANES, SCAN_COLS))
        a1i = jnp.broadcast_to(a_im_ref[:, cols], (SUBLANES, SCAN_COLS))
        if reverse:
            a1i = -a1i

        def block_of(i):
            j = (seg - 1 - i) if reverse else i
            return j, pl.ds(pl.multiple_of(j * SUBLANES, SUBLANES), SUBLANES)

        def local(i, carry, cols=cols, a1r=a1r, a1i=a1i):
            xr, xi = carry
            _, blk = block_of(i)
            nr, ni = cmul(a1r, a1i, xr, xi)
            xr, xi = nr + re_ref[blk, cols], ni + im_ref[blk, cols]
            re_ref[blk, cols] = xr
            im_ref[blk, cols] = xi
            return xr, xi

        zero = jnp.zeros((SUBLANES, SCAN_COLS), F32)
        er, ei = lax.fori_loop(0, seg, local, (zero, zero))

        pr, pi = a1r, a1i
        for _ in range(seg.bit_length() - 1):
            pr, pi = cmul(pr, pi, pr, pi)
        cr, ci = carry_re[:, cols], carry_im[:, cols]
        inr, ini = cmul(pr, pi, cr, ci)
        er = er + jnp.where(at_edge, inr, 0.0)
        ei = ei + jnp.where(at_edge, ini, 0.0)
        for sft in (1, 2, 4):
            shift, keep = (SUBLANES - sft, rowi < SUBLANES - sft) if reverse else (sft, rowi >= sft)
            rs = jnp.where(keep, pltpu.roll(er, shift, 0), 0.0)
            ims = jnp.where(keep, pltpu.roll(ei, shift, 0), 0.0)
            dr, di = cmul(pr, pi, rs, ims)
            er, ei = er + dr, ei + di
            pr, pi = cmul(pr, pi, pr, pi)
        carry_re[:, cols] = jnp.broadcast_to(er[last:last + 1, :], (SUBLANES, SCAN_COLS))
        carry_im[:, cols] = jnp.broadcast_to(ei[last:last + 1, :], (SUBLANES, SCAN_COLS))
        one = (SUBLANES - 1) if reverse else 1
        kr = jnp.where(at_edge, cr, pltpu.roll(er, one, 0))
        ki = jnp.where(at_edge, ci, pltpu.roll(ei, one, 0))

        def fix(i, carry, cols=cols, a1r=a1r, a1i=a1i):
            kr, ki, acc = carry
            j, blk = block_of(i)
            kr, ki = cmul(a1r, a1i, kr, ki)
            xr, xi = re_ref[blk, cols] + kr, im_ref[blk, cols] + ki
            re_ref[blk, cols] = xr
            im_ref[blk, cols] = xi
            if visit is not None:
                acc = visit(cols, j, xr, xi, acc)
            return kr, ki, acc

        _, _, acc = lax.fori_loop(0, seg, fix, (kr, ki, (zero, zero)))
        if visit is not None:
            visit(cols, None, None, None, acc)


SSM_CHUNK = 512


def _ssm_fwd(z, a_re, a_im, bb_re, bb_im, cc_re, cc_im, d_skip, w_glu, b_glu, nb, seq):
    t_tok = nb * seq
    tc = min(SSM_CHUNK, seq)
    nch = seq // tc
    grp = N_STATE // 4

    def body(u_ref, gs_ref, ar_ref, ai_ref, bbr_ref, bbi_ref, ccr_ref, cci_ref, d_ref, wg_ref, bg_ref,
             xr_ref, xi_ref, y_ref, sg_ref, car_re, car_im, seg_u, seg_y):
        @pl.when(pl.program_id(1) == 0)
        def _():
            car_re[...] = jnp.zeros_like(car_re)
            car_im[...] = jnp.zeros_like(car_im)

        u = u_ref[...]
        _to_segments(seg_u, u)
        for j in range(4):
            uj = seg_u[j].astype(BF16)
            xr_ref[:, j * grp:(j + 1) * grp] = _dot(uj, bbr_ref[j])
            xi_ref[:, j * grp:(j + 1) * grp] = _dot(uj, bbi_ref[j])
        _scan_chunk(xr_ref, xi_ref, ar_ref, ai_ref, car_re, car_im, tc, reverse=False)
        for j in range(4):
            xr = xr_ref[:, j * grp:(j + 1) * grp].astype(BF16)
            xi = xi_ref[:, j * grp:(j + 1) * grp].astype(BF16)
            seg_y[j] = _dot_nt(xr, ccr_ref[j]) - _dot_nt(xi, cci_ref[j])
        y = _from_segments(seg_y) + d_ref[...] * u
        y_ref[...] = y
        yg, _ = _gelu_and_grad(y)
        gl = _dot(yg.astype(BF16), wg_ref[...]) + bg_ref[...]
        gs = gs_ref[...]
        sg_ref[...] = (yg * _sig(gl) * gs * _sig(gs)).astype(BF16)

    umap = lambda b, ch: (b * nch + ch, 4)
    gmap = lambda b, ch: (b * nch + ch, 5)
    row = lambda b, ch: (b * nch + ch, 0)
    s = jax.ShapeDtypeStruct
    consts = [a_re, a_im, bb_re, bb_im, cc_re, cc_im, d_skip, w_glu, b_glu]
    return pl.pallas_call(
        body, name="ssm_fwd", grid=(nb, nch),
        out_shape=[s((t_tok, N_STATE), F32), s((t_tok, N_STATE), F32), s((t_tok, SSM_W), F32),
                   s((t_tok, SSM_W), BF16)],
        in_specs=[pl.BlockSpec((tc, SSM_W), umap), pl.BlockSpec((tc, SSM_W), gmap)] + [_full(c.shape) for c in consts],
        out_specs=[pl.BlockSpec((tc, N_STATE), row), pl.BlockSpec((tc, N_STATE), row),
                   pl.BlockSpec((tc, SSM_W), row), pl.BlockSpec((tc, SSM_W), row)],
        scratch_shapes=[pltpu.VMEM((SUBLANES, N_STATE), F32), pltpu.VMEM((SUBLANES, N_STATE), F32),
                        pltpu.VMEM((4, tc, LANES), F32), pltpu.VMEM((4, tc, LANES), F32)],
        compiler_params=_params(2, VMEM_LIMIT_V7X),
    )(z, z, *consts)


def _tail(x2, tg2, ag, sg, p2, w_out, w_g, w_p, g_ple):
    t_tok = x2.shape[0]
    tm = min(512, t_tok)
    nt = t_tok // tm
    half = ATTN_W

    def body(x_ref, tg_ref, ag_ref, sg_ref, p_ref, wo_ref, wg_ref, wp_ref, gp_ref,
             dmix_ref, dh1_ref, loss_ref, dgp_ref, dwo_ref, dwg_ref, dwp_ref, acc_o, acc_g, acc_p):
        i = pl.program_id(0)

        @pl.when(i == 0)
        def _():
            loss_ref[...] = jnp.zeros_like(loss_ref)
            dgp_ref[...] = jnp.zeros_like(dgp_ref)
            acc_o[...] = jnp.zeros_like(acc_o)
            acc_g[...] = jnp.zeros_like(acc_g)
            acc_p[...] = jnp.zeros_like(acc_p)

        ag_t, sg_t = ag_ref[...], sg_ref[...]
        h1 = x_ref[...] + _dot(ag_t, wo_ref[0:half, :]) + _dot(sg_t, wo_ref[half:2 * half, :])
        r2 = lax.rsqrt(jnp.mean(h1 * h1, axis=-1, keepdims=True) + EPS)
        hnorm = h1 * r2
        gp = gp_ref[...]
        hn = (hnorm * gp).astype(BF16)
        gate = _sig(_dot(hn, wg_ref[...]))
        pb = p_ref[...].astype(BF16)
        pp = jnp.concatenate([_dot(pb, wp_ref[j]) for j in range(N_DEV)], axis=-1)
        h2 = h1 + gate * pp
        err = h2 - tg_ref[...]
        loss_ref[...] += 0.5 * jnp.sum(err * err) * (1.0 / D_MODEL)
        dh2 = err * (1.0 / D_MODEL)
        dpp = (dh2 * gate).astype(BF16)
        dgpre = (dh2 * pp * gate * (1.0 - gate)).astype(BF16)
        acc_p[...] += _dot_tn(pb, dpp)
        acc_g[...] += _dot_tn(hn, dgpre)
        dhn = _dot_nt(dgpre, wg_ref[...])
        dgp_ref[...] += jnp.sum(dhn * hnorm, axis=0, keepdims=True)
        a = dhn * gp
        dh1 = dh2 + r2 * (a - hnorm * jnp.mean(a * hnorm, axis=-1, keepdims=True))
        dh1_ref[...] = dh1
        dh1b = dh1.astype(BF16)
        acc_o[0:half, :] += _dot_tn(ag_t, dh1b)
        acc_o[half:2 * half, :] += _dot_tn(sg_t, dh1b)
        dmix_ref[...] = _dot_nt(dh1b, wo_ref[...])

        @pl.when(i == nt - 1)
        def _():
            dwo_ref[...] = acc_o[...].astype(BF16)
            dwg_ref[...] = acc_g[...].astype(BF16)
            for j in range(N_DEV):
                dwp_ref[j] = acc_p[:, j * LANES:(j + 1) * LANES].astype(BF16)

    row = lambda i: (i, 0)
    s = jax.ShapeDtypeStruct
    return pl.pallas_call(
        body, name="tail_fwd_bwd", grid=(nt,),
        out_shape=[s((t_tok, D_MODEL), F32), s((t_tok, D_MODEL), F32), s((SUBLANES, LANES), F32),
                   s((1, D_MODEL), F32), s((D_MODEL, D_MODEL), BF16), s((D_MODEL, D_MODEL), BF16),
                   s((N_DEV, PLE_DIM, LANES), BF16)],
        in_specs=[pl.BlockSpec((tm, D_MODEL), row), pl.BlockSpec((tm, D_MODEL), row),
                  pl.BlockSpec((tm, half), row), pl.BlockSpec((tm, half), row), pl.BlockSpec((tm, PLE_DIM), row),
                  _full(w_out.shape), _full(w_g.shape), _full(w_p.shape), _full(g_ple.shape)],
        out_specs=[pl.BlockSpec((tm, D_MODEL), row), pl.BlockSpec((tm, D_MODEL), row), _full((SUBLANES, LANES)),
                   _full((1, D_MODEL)), _full((D_MODEL, D_MODEL)), _full((D_MODEL, D_MODEL)),
                   _full((N_DEV, PLE_DIM, LANES))],
        scratch_shapes=[pltpu.VMEM((D_MODEL, D_MODEL), F32), pltpu.VMEM((D_MODEL, D_MODEL), F32),
                        pltpu.VMEM((PLE_DIM, D_MODEL), F32)],
        compiler_params=_params(1, VMEM_LIMIT_V7X),
    )(x2, tg2, ag, sg, p2, w_out, w_g, w_p, g_ple)


def _attn_bwd(qh, kh, z, o, lse, dmix, nb, seq, parts):
    t_tok = nb * seq
    n_rs = len(parts)
    rs = _ReduceScatter([p.shape for p in parts])
    n_steps = (nb, ATTN_W // LANES)

    def body(*refs):
        (q_ref, k_ref, v_ref, ga_ref, o_ref, l_ref, da_ref), refs = refs[:7], refs[7:]
        part_refs, refs = refs[:n_rs], refs[n_rs:]
        (dq_ref, dk_ref, dv_ref, dga_ref), refs = refs[:4], refs[4:]
        g_refs, refs = refs[:n_rs], refs[n_rs:]
        (qf, kf, dof, dlf), rs_scratch = refs[:4], refs[4:]
        b, hp = pl.program_id(0), pl.program_id(1)

        @pl.when((b == 0) & (hp == 0))
        def _():
            rs.start(part_refs, rs_scratch)

        ga, o_t, da = ga_ref[...], o_ref[...], da_ref[...]
        sga = _sig(ga)
        d_o = da * ga * sga
        dga_ref[...] = da * o_t * sga * (1.0 + ga * (1.0 - sga))
        lane = lax.broadcasted_iota(jnp.int32, (1, LANES), 1)
        d_oo = d_o * o_t
        delta = jnp.zeros_like(d_oo)
        for h in range(LANES // HEAD_DIM):
            lm2 = (lane // HEAD_DIM) == h
            delta = jnp.where(lm2, jnp.sum(jnp.where(lm2, d_oo, 0.0), axis=-1, keepdims=True), delta)
        qf[...] = q_ref[...].astype(F32)
        kf[...] = k_ref[...].astype(F32)
        dof[...] = d_o
        dlf[...] = delta
        dq_ref[...] = jnp.zeros_like(dq_ref)
        dk_ref[...] = jnp.zeros_like(dk_ref)
        dv_ref[...] = jnp.zeros_like(dv_ref)
        lms = _head_masks()
        for window, dil in DILATED:
            nt, tq = _window_tiling(seq, window, dil)
            q3 = _gather_classes(qf, dil, nt, tq, BF16)
            k3 = _gather_classes(kf, dil, nt, tq, BF16)
            v3 = _gather_classes(v_ref, dil, nt, tq, BF16)
            do3 = _gather_classes(dof, dil, nt, tq, BF16)
            lt3 = _gather_classes(l_ref, dil, nt, tq, F32)
            dl3 = _gather_classes(dlf, dil, nt, tq, F32)
            if nt > 1:
                k3, v3 = _with_prev_tile(k3, dil, nt), _with_prev_tile(v3, dil, nt)
            valid = _band_valid(dil, nt, tq)
            dq = jnp.zeros(q3.shape, F32)
            dk = jnp.zeros(k3.shape, F32)
            dv = jnp.zeros(k3.shape, F32)
            for lm in lms:
                qm = jnp.where(lm, q3, jnp.zeros_like(q3))
                dom = jnp.where(lm, do3, jnp.zeros_like(do3))
                p = jnp.where(valid, jnp.exp(_bqk(qm, k3) - _head_col(lt3, lm)), 0.0)
                dv = dv + _bkd(p.astype(BF16), dom)
                ds = (p * (_bqk(dom, v3) - _head_col(dl3, lm))).astype(BF16)
                dq = dq + jnp.where(lm, _bqd(ds, k3), 0.0)
                dk = dk + _bkd(ds, qm)
            _scatter_classes(dq_ref, dq, dil, nt, tq, add=True)
            for ref, g in ((dk_ref, dk), (dv_ref, dv)):
                if nt > 1:
                    own, prev = g[:, tq:, :], g[:, :tq, :]
                    shifted = []
                    for r in range(dil):
                        t = prev[r * nt:(r + 1) * nt]
                        shifted.append(jnp.concatenate([t[1:], jnp.zeros_like(t[:1])], axis=0))
                    g = own + (shifted[0] if dil == 1 else jnp.concatenate(shifted, axis=0))
                _scatter_classes(ref, g, dil, nt, tq, add=True)

        @pl.when((b == n_steps[0] - 1) & (hp == n_steps[1] - 1))
        def _():
            rs.finish(part_refs, rs_scratch, g_refs)

    blk = pl.BlockSpec((seq, LANES), lambda b, hp: (b, hp))
    s = jax.ShapeDtypeStruct
    outs = pl.pallas_call(
        body, name="attn_bwd", grid=n_steps,
        out_shape=[s((t_tok, ATTN_W), F32)] * 4 + [s(p.shape[1:], F32) for p in parts],
        in_specs=[blk, blk, _zblock(seq, 2), _zblock(seq, 3), blk, blk, blk]
        + [pl.BlockSpec(memory_space=pl.ANY)] * n_rs,
        out_specs=[blk] * 4 + [_full(p.shape[1:]) for p in parts],
        scratch_shapes=[pltpu.VMEM((seq, LANES), F32)] * 4 + rs.scratch(parts[0].dtype),
        compiler_params=_params(2, VMEM_LIMIT_V7X),
    )(qh, kh, z, z, o, lse, dmix, *parts)
    return outs[:4], outs[4:]


def _ssm_bwd(z, dmix, y, x_re, x_im, a_re, a_im, bb_re, bb_im, cc_re, cc_im, d_skip, w_glu, b_glu, nb, seq):
    t_tok = nb * seq
    tc = min(SSM_CHUNK, seq)
    nch = seq // tc
    grp = N_STATE // 4

    def body(u_ref, gs_ref, ds_ref, y_ref, xr_ref, xi_ref, xpr_ref, xpi_ref,
             ar_ref, ai_ref, bbr_ref, bbi_ref, ccr_ref, cci_ref, d_ref, wg_ref, bg_ref,
             du_ref, dgs_ref, dwg_ref, dbg_ref, dd_ref, dar_ref, dai_ref, dbbr_ref, dbbi_ref, dccr_ref, dcci_ref,
             lam_re, lam_im, car_re, car_im, acc_wg, seg_a, seg_b, ent_re, ent_im):
        step = pl.program_id(1)
        first_chunk = step == nch - 1

        @pl.when((pl.program_id(0) == 0) & (step == 0))
        def _():
            acc_wg[...] = jnp.zeros_like(acc_wg)
            for ref in (dbg_ref, dd_ref, dar_ref, dai_ref, dbbr_ref, dbbi_ref, dccr_ref, dcci_ref):
                ref[...] = jnp.zeros_like(ref)

        @pl.when(step == 0)
        def _():
            car_re[...] = jnp.zeros_like(car_re)
            car_im[...] = jnp.zeros_like(car_im)

        u, gs, dssm, y = u_ref[...], gs_ref[...], ds_ref[...], y_ref[...]
        yg, dgelu = _gelu_and_grad(y)
        ygb = yg.astype(BF16)
        sgl = _sig(_dot(ygb, wg_ref[...]) + bg_ref[...])
        sgs = _sig(gs)
        dout = dssm * gs * sgs
        dgs_ref[...] = dssm * yg * sgl * sgs * (1.0 + gs * (1.0 - sgs))
        dgl = dout * yg * sgl * (1.0 - sgl)
        dglb = dgl.astype(BF16)
        dyg = dout * sgl + _dot_nt(dglb, wg_ref[...])
        acc_wg[...] += _dot_tn(ygb, dglb)
        dbg_ref[...] += jnp.sum(dgl, axis=0, keepdims=True)
        dy = dyg * dgelu
        dd_ref[...] += jnp.sum(dy * u, axis=0, keepdims=True)
        _to_segments(seg_a, dy)
        _to_segments(seg_b, u)
        for j in range(4):
            dyj = seg_a[j].astype(BF16)
            sl = slice(j * grp, (j + 1) * grp)
            lam_re[:, sl] = _dot(dyj, ccr_ref[j])
            lam_im[:, sl] = -_dot(dyj, cci_ref[j])
            dccr_ref[j] += _dot_tn(dyj, xr_ref[:, sl].astype(BF16))
            dcci_ref[j] -= _dot_tn(dyj, xi_ref[:, sl].astype(BF16))

        keep_prev = jnp.where(first_chunk, 0.0, 1.0)
        seg = tc // SUBLANES
        last_blk = pl.ds((seg - 1) * SUBLANES, SUBLANES)
        row0 = lax.broadcasted_iota(jnp.int32, (SUBLANES, N_STATE), 0) == 0
        for src, prev, dst in ((xr_ref, xpr_ref, ent_re), (xi_ref, xpi_ref, ent_im)):
            before = jnp.broadcast_to(prev[SUBLANES - 1:SUBLANES, :] * keep_prev, (SUBLANES, N_STATE))
            dst[...] = jnp.where(row0, before, pltpu.roll(src[last_blk, :], 1, 0))

        def visit(cols, j, lr, li, acc):
            if j is None:
                dar_ref[:, cols] += jnp.sum(acc[0], axis=0, keepdims=True)
                dai_ref[:, cols] += jnp.sum(acc[1], axis=0, keepdims=True)
                return None
            blk = pl.ds(pl.multiple_of(jnp.maximum(j - 1, 0) * SUBLANES, SUBLANES), SUBLANES)
            inside = j > 0
            xpr = jnp.where(inside, xr_ref[blk, cols], ent_re[:, cols])
            xpi = jnp.where(inside, xi_ref[blk, cols], ent_im[:, cols])
            return acc[0] + lr * xpr + li * xpi, acc[1] + li * xpr - lr * xpi

        _scan_chunk(lam_re, lam_im, ar_ref, ai_ref, car_re, car_im, tc, reverse=True, visit=visit)

        for j in range(4):
            sl = slice(j * grp, (j + 1) * grp)
            lr = lam_re[:, sl].astype(BF16)
            li = lam_im[:, sl].astype(BF16)
            uj = seg_b[j].astype(BF16)
            seg_a[j] = _dot_nt(lr, bbr_ref[j]) + _dot_nt(li, bbi_ref[j])
            dbbr_ref[j] += _dot_tn(uj, lr)
            dbbi_ref[j] += _dot_tn(uj, li)
        du_ref[...] = _from_segments(seg_a) + dy * d_ref[...]

        @pl.when((pl.program_id(0) == nb - 1) & (step == nch - 1))
        def _():
            dwg_ref[...] = acc_wg[...].astype(BF16)

    rev = lambda b, ch: b * nch + (nch - 1 - ch)
    umap = lambda b, ch: (rev(b, ch), 4)
    gmap = lambda b, ch: (rev(b, ch), 5)
    smap = lambda b, ch: (rev(b, ch), 1)
    row = lambda b, ch: (rev(b, ch), 0)
    prev = lambda b, ch: (jnp.maximum(rev(b, ch) * (tc // SUBLANES) - 1, 0), 0)
    s = jax.ShapeDtypeStruct
    consts = [a_re, a_im, bb_re, bb_im, cc_re, cc_im, d_skip, w_glu, b_glu]
    acc_shapes = [s((1, SSM_W), F32), s((1, SSM_W), F32), s((1, N_STATE), F32), s((1, N_STATE), F32),
                  s(bb_re.shape, F32), s(bb_re.shape, F32), s(cc_re.shape, F32), s(cc_re.shape, F32)]
    return pl.pallas_call(
        body, name="ssm_bwd", grid=(nb, nch),
        out_shape=[s((t_tok, SSM_W), F32), s((t_tok, SSM_W), F32), s((SSM_W, SSM_W), BF16)] + acc_shapes,
        in_specs=[pl.BlockSpec((tc, SSM_W), umap), pl.BlockSpec((tc, SSM_W), gmap), pl.BlockSpec((tc, SSM_W), smap),
                  pl.BlockSpec((tc, SSM_W), row), pl.BlockSpec((tc, N_STATE), row), pl.BlockSpec((tc, N_STATE), row),
                  pl.BlockSpec((SUBLANES, N_STATE), prev), pl.BlockSpec((SUBLANES, N_STATE), prev)]
        + [_full(c.shape) for c in consts],
        out_specs=[pl.BlockSpec((tc, SSM_W), row), pl.BlockSpec((tc, SSM_W), row), _full((SSM_W, SSM_W))]
        + [_full(a.shape) for a in acc_shapes],
        scratch_shapes=[pltpu.VMEM((tc, N_STATE), F32), pltpu.VMEM((tc, N_STATE), F32),
                        pltpu.VMEM((SUBLANES, N_STATE), F32), pltpu.VMEM((SUBLANES, N_STATE), F32),
                        pltpu.VMEM((SSM_W, SSM_W), F32), pltpu.VMEM((4, tc, LANES), F32),
                        pltpu.VMEM((4, tc, LANES), F32),
                        pltpu.VMEM((SUBLANES, N_STATE), F32), pltpu.VMEM((SUBLANES, N_STATE), F32)],
        compiler_params=_params(2, VMEM_LIMIT_V7X),
    )(z, z, dmix, y, x_re, x_im, x_re, x_im, *consts)


def _dz_and_dx(x2, z, dqh, dkh, dvb, dga, du, dgs, dh1, w_in_g, g_mix, gq_t, gk_t, ones_bd, fold):
    t_tok = x2.shape[0]
    tm = min(512, t_tok)
    nt = t_tok // tm
    a_w = ATTN_W

    def head_norm_bwd(raw, d_hat, gain, scale, ones):
        r = lax.rsqrt(_hdot(raw * raw, ones) * (1.0 / HEAD_DIM) + EPS)
        n = raw * r
        a = d_hat * gain * scale
        d_raw = r * (a - n * (_hdot(a * n, ones) * (1.0 / HEAD_DIM)))
        return d_raw, jnp.sum(d_hat * n * scale, axis=0, keepdims=True)

    def body(x_ref, q_ref, k_ref, dq_ref, dk_ref, dv_ref, dga_ref, du_ref, dgs_ref, dh1_ref, w_ref, g_ref,
             gq_ref, gk_ref, ones_ref, fold_ref, dz_ref, gx_ref, dgm_ref, dgq_ref, dgk_ref, acc_q, acc_k):
        i = pl.program_id(0)

        @pl.when(i == 0)
        def _():
            dgm_ref[...] = jnp.zeros_like(dgm_ref)
            acc_q[...] = jnp.zeros_like(acc_q)
            acc_k[...] = jnp.zeros_like(acc_k)

        ones = ones_ref[...]
        dq, sq = head_norm_bwd(q_ref[...], dq_ref[...], gq_ref[...], HEAD_DIM ** -0.5, ones)
        dk, sk = head_norm_bwd(k_ref[...], dk_ref[...], gk_ref[...], 1.0, ones)
        acc_q[...] += jnp.broadcast_to(sq, acc_q.shape)
        acc_k[...] += jnp.broadcast_to(sk, acc_k.shape)
        parts = (dq, dk, dv_ref[...], dga_ref[...], du_ref[...], dgs_ref[...])
        for n, part in enumerate(parts):
            dz_ref[:, n * a_w:(n + 1) * a_w] = part.astype(BF16)
        dxn = jnp.zeros((tm, D_MODEL), F32)
        for j in range(N_DEV):
            dxn = dxn + _dot_nt(dz_ref[:, j * COL_W:(j + 1) * COL_W], w_ref[j])
        x = x_ref[...]
        r1 = lax.rsqrt(jnp.mean(x * x, axis=-1, keepdims=True) + EPS)
        xnorm = x * r1
        dgm_ref[...] += jnp.sum(dxn * xnorm, axis=0, keepdims=True)
        a = dxn * g_ref[...]
        gx_ref[...] = dh1_ref[...] + r1 * (a - xnorm * jnp.mean(a * xnorm, axis=-1, keepdims=True))

        @pl.when(i == nt - 1)
        def _():
            dgq_ref[...] = _hdot(acc_q[...], fold_ref[...])
            dgk_ref[...] = _hdot(acc_k[...], fold_ref[...])

    row = lambda i: (i, 0)
    col = lambda n: (lambda i: (i, n))
    s = jax.ShapeDtypeStruct
    half = pl.BlockSpec((tm, a_w), row)
    return pl.pallas_call(
        body, name="dz_dx", grid=(nt,),
        out_shape=[s((t_tok, IN_W), BF16), s((t_tok, D_MODEL), F32), s((1, D_MODEL), F32),
                   s((SUBLANES, HEAD_DIM), F32), s((SUBLANES, HEAD_DIM), F32)],
        in_specs=[pl.BlockSpec((tm, D_MODEL), row), pl.BlockSpec((tm, a_w), col(0)), pl.BlockSpec((tm, a_w), col(1)),
                  half, half, half, half, half, half, pl.BlockSpec((tm, D_MODEL), row),
                  _full(w_in_g.shape), _full(g_mix.shape), _full(gq_t.shape), _full(gk_t.shape),
                  _full(ones_bd.shape), _full(fold.shape)],
        out_specs=[pl.BlockSpec((tm, IN_W), row), pl.BlockSpec((tm, D_MODEL), row), _full((1, D_MODEL)),
                   _full((SUBLANES, HEAD_DIM)), _full((SUBLANES, HEAD_DIM))],
        scratch_shapes=[pltpu.VMEM((SUBLANES, a_w), F32), pltpu.VMEM((SUBLANES, a_w), F32)],
        compiler_params=_params(1, VMEM_LIMIT_V7X),
    )(x2, z, z, dqh, dkh, dvb, dga, du, dgs, dh1, w_in_g, g_mix, gq_t, gk_t, ones_bd, fold)


def _dw_in(xn, dz, glu_parts, smalls):
    t_tok = xn.shape[0]
    tk = min(1024, t_tok)
    nk = t_tok // tk
    rs = _ReduceScatter([glu_parts.shape])
    n_small = len(smalls)
    ag = _AllGather(n_small, cast=False)
    n_rs = len(rs.scratch(BF16))

    def place():
        x, y, c = lax.axis_index("x"), lax.axis_index("y"), lax.axis_index("c")
        return x, y, c, [(1 - x, y), (x, 1 - y), (1 - x, 1 - y)]

    def target(i):
        x, y, c, _ = place()
        n = i // 2
        px = jnp.where((n == 0) | (n == 2), 1 - x, x)
        py = jnp.where((n == 1) | (n == 2), 1 - y, y)
        pc = jnp.where(i % 2 == 0, 1 - c, c)
        return 4 * px + 2 * py + pc

    chunk, chunks = _row_chunks(D_MODEL)

    def body(*refs):
        (xn_ref, dz_ref, glu_ref), refs = refs[:3], refs[3:]
        small_refs, refs = list(refs[:n_small]), refs[n_small:]
        (gin_ref, gglu_ref), refs = refs[:2], refs[2:]
        gath_refs, refs = list(refs[:n_small]), refs[n_small:]
        (acc, stage, land, send_sems, recv_sems), rest = refs[:5], refs[5:]
        rs_scratch, ag_sems = rest[:n_rs], rest[n_rs:]
        i, k = pl.program_id(0), pl.program_id(1)
        x, y, c, chips = place()

        def push(slot, to):
            return pltpu.make_async_remote_copy(
                src_ref=stage.at[slot], dst_ref=land.at[slot], send_sem=send_sems.at[slot],
                recv_sem=recv_sems.at[slot], device_id=to, device_id_type=MESH)

        pushes = [push(n, (x, y, 1 - c)) for n in range(4)] + [push(4 + n, (*chips[n], c)) for n in range(3)]

        def staged(slot, plus=None):
            def put(s, carry):
                r = pl.ds(pl.multiple_of(s * chunk, chunk), chunk)
                val = acc[r, :]
                if plus is not None:
                    val = val + land[plus, r, :].astype(F32)
                stage[slot, r, :] = val.astype(BF16)
                return carry

            lax.fori_loop(0, chunks, put, 0)

        @pl.when((i == 0) & (k == 0))
        def _():
            rs.start([glu_ref], rs_scratch)
            ag.start(small_refs, gath_refs, ag_sems)

        @pl.when((i == N_DEV // 2) & (k == 0))
        def _():
            ag.forward(small_refs, gath_refs, ag_sems)

        @pl.when(k == 0)
        def _():
            acc[...] = jnp.zeros_like(acc)

        acc[...] += _dot_tn(xn_ref[...], dz_ref[...])

        for n in range(4):
            @pl.when((k == nk - 1) & (i == 2 * n))
            def _(n=n):
                staged(n)
                pushes[n].start()

        for n in range(3):
            @pl.when((k == nk - 1) & (i == 2 * n + 1))
            def _(n=n):
                pushes[n].wait_recv()
                staged(4 + n, plus=n)
                pushes[4 + n].start()

        @pl.when((k == nk - 1) & (i == N_DEV - 1))
        def _():
            for slot in range(3, N_DEV - 1):
                pushes[slot].wait_recv()

            def add(s, carry):
                r = pl.ds(pl.multiple_of(s * chunk, chunk), chunk)
                total = acc[r, :]
                for slot in range(3, N_DEV - 1):
                    total = total + land[slot, r, :].astype(F32)
                gin_ref[r, :] = total
                return carry

            lax.fori_loop(0, chunks, add, 0)
            for cp in pushes:
                cp.wait_send()
            rs.finish([glu_ref], rs_scratch, [gglu_ref])
            ag.finish(small_refs, gath_refs, ag_sems)

    any_spec = pl.BlockSpec(memory_space=pl.ANY)
    s = jax.ShapeDtypeStruct
    outs = pl.pallas_call(
        body, name="dw_in", grid=(N_DEV, nk),
        out_shape=[s((D_MODEL, COL_W), F32), s(glu_parts.shape[1:], F32)]
        + [s((N_DEV,) + a.shape, a.dtype) for a in smalls],
        in_specs=[pl.BlockSpec((tk, D_MODEL), lambda i, k: (k, 0)),
                  pl.BlockSpec((tk, COL_W), lambda i, k: (k, target(i))), any_spec] + [any_spec] * n_small,
        out_specs=[_full((D_MODEL, COL_W)), _full(glu_parts.shape[1:])] + [any_spec] * n_small,
        scratch_shapes=[pltpu.VMEM((D_MODEL, COL_W), F32), pltpu.VMEM((N_DEV - 1, D_MODEL, COL_W), BF16),
                        pltpu.VMEM((N_DEV - 1, D_MODEL, COL_W), BF16), pltpu.SemaphoreType.DMA((N_DEV - 1,)),
                        pltpu.SemaphoreType.DMA((N_DEV - 1,))] + rs.scratch(BF16) + ag.scratch(),
        compiler_params=_params(2, VMEM_LIMIT_V7X),
    )(xn, dz, glu_parts, *smalls)
    return outs[0], outs[1], outs[2:]


SMALL = ("mix_norm", "q_norm", "k_norm", "lambda_re", "lambda_im", "log_dt", "b_re", "b_im", "c_re", "c_im",
         "d_skip", "b_glu", "ple_norm")
BIG = ("w_in", "w_glu", "w_out", "w_ple_gate", "w_ple_proj")
WEIGHTS = ("mix_norm", "w_in", "q_norm", "k_norm", "lambda_re", "lambda_im", "log_dt", "b_re", "b_im", "c_re",
           "c_im", "d_skip", "w_glu", "b_glu", "w_out", "ple_norm", "w_ple_gate", "w_ple_proj")


def kernel(x, p, mix_norm, w_in, q_norm, k_norm, lambda_re, lambda_im, log_dt, b_re, b_im, c_re, c_im, d_skip, w_glu, b_glu, w_out, ple_norm, w_ple_gate, w_ple_proj, loss_target, m_mix_norm, m_w_in, m_q_norm, m_k_norm, m_lambda_re, m_lambda_im, m_log_dt, m_b_re, m_b_im, m_c_re, m_c_im, m_d_skip, m_w_glu, m_b_glu, m_w_out, m_ple_norm, m_w_ple_gate, m_w_ple_proj, v_mix_norm, v_w_in, v_q_norm, v_k_norm, v_lambda_re, v_lambda_im, v_log_dt, v_b_re, v_b_im, v_c_re, v_c_im, v_d_skip, v_w_glu, v_b_glu, v_w_out, v_ple_norm, v_w_ple_gate, v_w_ple_proj):
    env = dict(locals())
    w = {n: env[n] for n in WEIGHTS}
    m = {n: env["m_" + n] for n in WEIGHTS}
    v = {n: env["v_" + n] for n in WEIGHTS}
    nb, seq, _ = x.shape
    t_tok = nb * seq
    x2 = x.reshape(t_tok, D_MODEL)
    tg2 = loss_target.reshape(t_tok, D_MODEL)
    p2 = p.reshape(t_tok, PLE_DIM)

    shard2d = {"w_in": (D_MODEL, COL_W), "w_glu": (SSM_W // N_DEV, SSM_W), "w_out": (D_MODEL // N_DEV, D_MODEL),
               "w_ple_gate": (D_MODEL // N_DEV, D_MODEL), "w_ple_proj": (PLE_DIM, D_MODEL // N_DEV)}
    w_sh = [w[n].reshape(shard2d[n]) for n in BIG]

    g3 = (SSM_GROUPS, 1, SSM_STATE)
    lr3, li3 = lambda_re.reshape(g3), lambda_im.reshape(g3)
    dt3 = log_dt.reshape(SSM_GROUPS, 1, 1)
    btr = b_re[0].transpose(0, 2, 1)
    bti = b_im[0].transpose(0, 2, 1)
    a_re3, a_im3, bb_re, bb_im, cc_re, cc_im = _zoh_fwd(lr3, li3, dt3, btr, bti, c_re[0], c_im[0])
    a_re, a_im = a_re3.reshape(1, N_STATE), a_im3.reshape(1, N_STATE)

    ones_bd = _head_ones()
    fold = _head_fold()
    gq_t = jnp.tile(q_norm, (1, ATTN_W // HEAD_DIM))
    gk_t = jnp.tile(k_norm, (1, ATTN_W // HEAD_DIM))

    gq2 = jnp.tile(q_norm, (1, LANES // HEAD_DIM))
    gk2 = jnp.tile(k_norm, (1, LANES // HEAD_DIM))

    z, xn, w_in_g = _in_proj(x2, mix_norm, w_sh[0])
    (o, lse, ag, qh, kh), (w_glu_g, w_out_g, w_g_g, w_p_g) = _attn_fwd(z, gq2, gk2, nb, seq, w_sh[1:])
    w_glu_f = w_glu_g.reshape(SSM_W, SSM_W)
    w_out_f = w_out_g.reshape(D_MODEL, D_MODEL)
    w_g_f = w_g_g.reshape(D_MODEL, D_MODEL)
    x_re, x_im, y, sg = _ssm_fwd(z, a_re, a_im, bb_re, bb_im, cc_re, cc_im, d_skip, w_glu_f, b_glu, nb, seq)
    dmix, dh1, loss_t, d_ple, dw_out, dw_g, dw_p = _tail(x2, tg2, ag, sg, p2, w_out_f, w_g_f, w_p_g, ple_norm)

    early_parts = [dw_out.reshape(N_DEV, D_MODEL // N_DEV, D_MODEL), dw_g.reshape(N_DEV, D_MODEL // N_DEV, D_MODEL),
                   dw_p]
    (dqh, dkh, dvb, dga), (g_out, g_g, g_p) = _attn_bwd(qh, kh, z, o, lse, dmix, nb, seq, early_parts)
    (du, dgs, dw_glu, d_bglu, d_dskip, da_re, da_im, dbb_re, dbb_im, dcc_re, dcc_im) = _ssm_bwd(
        z, dmix, y, x_re, x_im, a_re, a_im, bb_re, bb_im, cc_re, cc_im, d_skip, w_glu_f, b_glu, nb, seq)
    d_lr, d_li, d_dt, d_btr, d_bti, d_cr, d_ci = _zoh_bwd(
        lr3, li3, dt3, btr, bti, da_re.reshape(g3), da_im.reshape(g3), dbb_re, dbb_im, dcc_re, dcc_im, fold)

    swapped = ("b_re", "b_im")

    def to_own(n, a):
        a = a.reshape(a.shape[1:]) if a.ndim > 2 else a
        return a.transpose(0, 2, 1) if n in swapped else a

    def from_own(n, a):
        a = a.transpose(0, 2, 1) if n in swapped else a
        return a.reshape(w[n].shape)

    own = {n: to_own(n, w[n]).shape for n in SMALL}
    dz, gx, d_mix, d_gq, d_gk = _dz_and_dx(x2, z, dqh, dkh, dvb, dga, du, dgs, dh1, w_in_g, mix_norm, gq_t, gk_t,
                                           ones_bd, fold)
    small_g = {"mix_norm": d_mix, "q_norm": d_gq[0:1], "k_norm": d_gk[0:1], "lambda_re": d_lr, "lambda_im": d_li,
               "log_dt": d_dt, "b_re": d_btr, "b_im": d_bti, "c_re": d_cr, "c_im": d_ci,
               "d_skip": d_dskip, "b_glu": d_bglu, "ple_norm": d_ple}
    g_in, g_glu, gathered = _dw_in(xn, dz, dw_glu.reshape(N_DEV, SSM_W // N_DEV, SSM_W),
                                   [small_g[n].reshape(own[n]) for n in SMALL] + [loss_t])
    g_sh = [g_in, g_glu, g_out, g_g, g_p]
    d_sh, m_sh, v_sh = _adamw_shards(g_sh, w_sh, [m[n].reshape(shard2d[n]) for n in BIG],
                                     [v[n].reshape(shard2d[n]) for n in BIG])

    *g_small, loss_sum = _small_sum(list(gathered))
    d_small, m_small, v_small = _adamw_small(
        g_small, *[[to_own(n, src[n]) for n in SMALL] for src in (w, m, v)])

    grads, deltas, new_m, new_v = {}, {}, {}, {}
    for dst, arrs in ((grads, g_small), (deltas, d_small), (new_m, m_small), (new_v, v_small)):
        for n, a in zip(SMALL, arrs):
            dst[n] = from_own(n, a)
    for i, n in enumerate(BIG):
        grads[n] = g_sh[i].reshape(w[n].shape)
        deltas[n] = d_sh[i].reshape(w[n].shape)
        new_m[n] = m_sh[i].reshape(w[n].shape)
        new_v[n] = v_sh[i].reshape(w[n].shape)

    loss = loss_sum[0, 0]
    return (loss, gx.reshape(x.shape), *[grads[n] for n in WEIGHTS], *[deltas[n] for n in WEIGHTS],
            *[new_m[n] for n in WEIGHTS], *[new_v[n] for n in WEIGHTS])
```

```python
import math

import numpy as np
import jax
import jax.numpy as jnp
from jax import lax
from jax.experimental import pallas as pl
from jax.experimental.pallas import tpu as pltpu

F32 = jnp.float32
BF16 = jnp.bfloat16
MESH = pl.DeviceIdType.MESH
AXES = ("x", "y", "c")
N_DEV = 8

D_MODEL = 1024
HEAD_DIM = 64
ATTN_W = 512
SSM_W = 512
SSM_GROUPS = 32
SSM_GROUP = 16
SSM_STATE = 64
N_STATE = SSM_GROUPS * SSM_STATE
PLE_DIM = 256
IN_W = 3072
COL_W = IN_W // N_DEV
DILATED = ((128, 1), (512, 4), (2048, 16))
EPS = 1e-6
INV_SQRT2 = 1.0 / math.sqrt(2.0)
INV_SQRT_2PI = 1.0 / math.sqrt(2.0 * math.pi)

ADAM_LR, ADAM_B1, ADAM_B2, ADAM_EPS, ADAM_WD, ADAM_STEP = 0.001, 0.9, 0.999, 1e-08, 0.01, 10

VMEM_LIMIT_V7X = 56 * 1024 * 1024
SUBLANES = 8
LANES = 128


def _params(n_axes=None, vmem=None):
    kw = {}
    if n_axes:
        kw["dimension_semantics"] = ("arbitrary",) * n_axes
    if vmem:
        kw["vmem_limit_bytes"] = vmem
    return pltpu.CompilerParams(**kw)


def _dot(a, b):
    return jnp.dot(a, b, preferred_element_type=F32)


def _dot_nt(a, b):
    return lax.dot_general(a, b, (((1,), (1,)), ((), ())), preferred_element_type=F32)


def _dot_tn(a, b):
    return lax.dot_general(a, b, (((0,), (0,)), ((), ())), preferred_element_type=F32)


def _hdot(a, ones):
    hi = a.astype(BF16)
    lo = (a - hi.astype(F32)).astype(BF16)
    return _dot(hi, ones) + _dot(lo, ones)


def _sig(x):
    return 1.0 / (1.0 + jnp.exp(-x))


def _gelu_and_grad(y):
    cdf = 0.5 * (1.0 + lax.erf(y * INV_SQRT2))
    pdf = jnp.exp(-0.5 * y * y) * INV_SQRT_2PI
    return y * cdf, cdf + y * pdf


def _vmem():
    return pl.BlockSpec(memory_space=pltpu.VMEM)


def _full(shape):
    nd = len(shape)
    return pl.BlockSpec(shape, lambda *_: (0,) * nd)


class _AllGather:
    def __init__(self, n, cast):
        self.n, self.cast = n, cast

    def scratch(self):
        n = self.n
        return [pltpu.SemaphoreType.DMA((7 * n,)), pltpu.SemaphoreType.DMA((7 * n,)), pltpu.SemaphoreType.DMA((n,))]

    def _plan(self, src_refs, out_refs, sems):
        send_sems, recv_sems, own_sems = sems
        x, y, c = lax.axis_index("x"), lax.axis_index("y"), lax.axis_index("c")
        me, sibling = (x, y, c), (x, y, 1 - c)
        chips = [(1 - x, y), (x, 1 - y), (1 - x, 1 - y)]

        def idx(px, py, pc):
            return 4 * px + 2 * py + pc

        def copy(i, k, block, to, own_src=False):
            ref = out_refs[i].at[idx(*block)]
            return pltpu.make_async_remote_copy(
                src_ref=src_refs[i] if own_src and not self.cast else ref, dst_ref=ref,
                send_sem=send_sems.at[7 * i + k], recv_sem=recv_sems.at[7 * i + k],
                device_id=to, device_id_type=MESH)

        first, passed, arrive_ici, arrive_d2d, own = [], [], [], [], []
        for i in range(self.n):
            first.append(copy(i, 0, me, sibling, own_src=True))
            first += [copy(i, 1 + j, me, (*chip, c), own_src=True) for j, chip in enumerate(chips)]
            arrive_ici += [copy(i, 1 + j, (*chip, c), me) for j, chip in enumerate(chips)]
            passed += [copy(i, 4 + j, (*chip, c), sibling) for j, chip in enumerate(chips)]
            arrive_d2d.append(copy(i, 0, sibling, me))
            arrive_d2d += [copy(i, 4 + j, (*chip, 1 - c), me) for j, chip in enumerate(chips)]
            if not self.cast:
                own.append(pltpu.make_async_copy(src_refs[i], out_refs[i].at[idx(*me)], own_sems.at[i]))
        return idx(*me), first, passed, arrive_ici, arrive_d2d, own

    def start(self, src_refs, out_refs, sems):
        my, first, _, _, _, own = self._plan(src_refs, out_refs, sems)
        if self.cast:
            for i in range(self.n):
                out_refs[i][my] = src_refs[i][...].astype(out_refs[i].dtype)
        for cp in own + first:
            cp.start()

    def forward(self, src_refs, out_refs, sems):
        _, _, passed, arrive_ici, _, _ = self._plan(src_refs, out_refs, sems)
        for cp in arrive_ici:
            cp.wait_recv()
        for cp in passed:
            cp.start()

    def finish(self, src_refs, out_refs, sems):
        _, first, passed, _, arrive_d2d, own = self._plan(src_refs, out_refs, sems)
        for cp in own:
            cp.wait()
        for cp in arrive_d2d:
            cp.wait_recv()
        for cp in first + passed:
            cp.wait_send()


class _HostedGather:
    def __init__(self, shards):
        self.shapes = [(N_DEV,) + a.shape for a in shards]
        self.n = len(shards)
        self.ag = _AllGather(self.n, cast=True)

    def out_shape(self):
        return [jax.ShapeDtypeStruct(s, BF16) for s in self.shapes]

    def scratch(self):
        return [pltpu.VMEM(s, BF16) for s in self.shapes] + self.ag.scratch() + [pltpu.SemaphoreType.DMA((self.n,))]

    def _split(self, scratch):
        return scratch[:self.n], scratch[self.n:-1], scratch[-1]

    def start(self, src_refs, scratch):
        land, sems, _ = self._split(scratch)
        self.ag.start(src_refs, land, sems)

    def forward(self, src_refs, scratch):
        land, sems, _ = self._split(scratch)
        self.ag.forward(src_refs, land, sems)

    def finish(self, src_refs, scratch, out_refs):
        land, sems, out_sems = self._split(scratch)
        self.ag.finish(src_refs, land, sems)
        outs = [pltpu.make_async_copy(land[n], out_refs[n], out_sems.at[n]) for n in range(self.n)]
        for cp in outs:
            cp.start()
        for cp in outs:
            cp.wait()


def _all_gather(shards, out_dtypes, name):
    n = len(shards)
    ag = _AllGather(n, cast=True)

    def body(*refs):
        in_refs, out_refs, sems = refs[:n], refs[n:2 * n], refs[2 * n:]
        ag.start(in_refs, out_refs, sems)
        ag.forward(in_refs, out_refs, sems)
        ag.finish(in_refs, out_refs, sems)

    return pl.pallas_call(
        body, name=name,
        out_shape=[jax.ShapeDtypeStruct((N_DEV,) + s.shape, dt) for s, dt in zip(shards, out_dtypes)],
        in_specs=[_vmem()] * n, out_specs=[_vmem()] * n,
        scratch_shapes=ag.scratch(),
        compiler_params=_params(vmem=VMEM_LIMIT_V7X),
    )(*shards)


def _row_chunks(rows):
    chunk = 64 if rows % 64 == 0 else rows
    return chunk, rows // chunk


class _ReduceScatter:
    def __init__(self, shapes):
        self.shapes = shapes
        self.n = len(shapes)

    def scratch(self, dtype):
        return ([pltpu.VMEM(s, dtype) for s in self.shapes]
                + [pltpu.SemaphoreType.DMA((7 * self.n,)), pltpu.SemaphoreType.DMA((7 * self.n,)),
                   pltpu.SemaphoreType.DMA((self.n,))])

    def _copies(self, in_refs, land_refs, send_sems, recv_sems, own_sems):
        x, y, c = lax.axis_index("x"), lax.axis_index("y"), lax.axis_index("c")
        remote, own = [], []
        for i in range(self.n):
            for m in range(1, N_DEV):
                px = 1 - x if m & 4 else x
                py = 1 - y if m & 2 else y
                pc = 1 - c if m & 1 else c
                remote.append(pltpu.make_async_remote_copy(
                    src_ref=in_refs[i].at[4 * px + 2 * py + pc], dst_ref=land_refs[i].at[m - 1],
                    send_sem=send_sems.at[7 * i + m - 1], recv_sem=recv_sems.at[7 * i + m - 1],
                    device_id=(px, py, pc), device_id_type=MESH))
            own.append(pltpu.make_async_copy(in_refs[i].at[4 * x + 2 * y + c], land_refs[i].at[N_DEV - 1],
                                             own_sems.at[i]))
        return remote, own

    def start(self, in_refs, scratch):
        remote, own = self._copies(in_refs, scratch[:self.n], *scratch[self.n:])
        for cp in remote + own:
            cp.start()

    def finish(self, in_refs, scratch, out_refs):
        land_refs = scratch[:self.n]
        remote, own = self._copies(in_refs, land_refs, *scratch[self.n:])
        for cp in own:
            cp.wait()
        for cp in remote:
            cp.wait_recv()
        for i in range(self.n):
            chunk, steps = _row_chunks(self.shapes[i][1])

            def step(s, carry, i=i, chunk=chunk):
                r = pl.ds(pl.multiple_of(s * chunk, chunk), chunk)
                acc = land_refs[i][N_DEV - 1, r, :].astype(F32)
                for m in range(1, N_DEV):
                    acc = acc + land_refs[i][m - 1, r, :].astype(F32)
                out_refs[i][r, :] = acc
                return carry

            lax.fori_loop(0, steps, step, 0)
        for cp in remote:
            cp.wait_send()


def _reduce_scatter(parts, name):
    n = len(parts)
    rs = _ReduceScatter([p.shape for p in parts])

    def body(*refs):
        in_refs, out_refs, scratch = refs[:n], refs[n:2 * n], refs[2 * n:]
        rs.start(in_refs, scratch)
        rs.finish(in_refs, scratch, out_refs)

    return pl.pallas_call(
        body, name=name,
        out_shape=[jax.ShapeDtypeStruct(p.shape[1:], F32) for p in parts],
        in_specs=[_vmem()] * n, out_specs=[_vmem()] * n,
        scratch_shapes=rs.scratch(parts[0].dtype),
        compiler_params=_params(vmem=VMEM_LIMIT_V7X),
    )(*parts)


def _adamw_math(w, g, m, v):
    m = ADAM_B1 * m + (1.0 - ADAM_B1) * g
    v = ADAM_B2 * v + (1.0 - ADAM_B2) * (g * g)
    m_hat = m / (1.0 - ADAM_B1 ** ADAM_STEP)
    v_hat = v / (1.0 - ADAM_B2 ** ADAM_STEP)
    delta = -ADAM_LR * (m_hat / (jnp.sqrt(v_hat) + ADAM_EPS) + ADAM_WD * w)
    return delta, m, v


def _adamw_shards(gs, ws, ms, vs):
    n = len(gs)

    def body(*refs):
        g_refs, w_refs, m_refs, v_refs = (refs[k * n:(k + 1) * n] for k in range(4))
        d_out, m_out, v_out = (refs[(4 + k) * n:(5 + k) * n] for k in range(3))
        for i in range(n):
            chunk, steps = _row_chunks(gs[i].shape[0])

            def step(s, carry, i=i, chunk=chunk):
                r = pl.ds(pl.multiple_of(s * chunk, chunk), chunk)
                d, m, v = _adamw_math(w_refs[i][r, :], g_refs[i][r, :], m_refs[i][r, :], v_refs[i][r, :])
                d_out[i][r, :] = d
                m_out[i][r, :] = m
                v_out[i][r, :] = v
                return carry

            lax.fori_loop(0, steps, step, 0)

    shapes = [jax.ShapeDtypeStruct(g.shape, F32) for g in gs]
    outs = pl.pallas_call(
        body, name="adamw_shards", out_shape=shapes * 3,
        in_specs=[_vmem()] * (4 * n), out_specs=[_vmem()] * (3 * n),
        compiler_params=_params(vmem=VMEM_LIMIT_V7X),
    )(*gs, *ws, *ms, *vs)
    return outs[:n], outs[n:2 * n], outs[2 * n:]


def _small_sum(gathered):
    n = len(gathered)

    def body(*refs):
        ga_refs, out_refs = refs[:n], refs[n:]
        for i in range(n):
            def total(idx, i=i):
                g = ga_refs[i][(0,) + idx].astype(F32)
                for j in range(1, N_DEV):
                    g = g + ga_refs[i][(j,) + idx].astype(F32)
                out_refs[i][idx] = g

            if len(gathered[i].shape) == 4:
                def step(s, carry, total=total):
                    total((s,))
                    return carry

                lax.fori_loop(0, gathered[i].shape[1], step, 0)
            else:
                total((Ellipsis,))

    return pl.pallas_call(
        body, name="small_sum", out_shape=[jax.ShapeDtypeStruct(g.shape[1:], F32) for g in gathered],
        in_specs=[_vmem()] * n, out_specs=[_vmem()] * n,
        compiler_params=_params(vmem=VMEM_LIMIT_V7X),
    )(*gathered)


def _adamw_small(gs, ws, ms, vs):
    n = len(gs)

    def body(*refs):
        g_refs, w_refs, m_refs, v_refs = (refs[k * n:(k + 1) * n] for k in range(4))
        d_out, m_out, v_out = (refs[(4 + k) * n:(5 + k) * n] for k in range(3))
        for i in range(n):
            def update(idx, i=i):
                d, mm, vv = _adamw_math(w_refs[i][idx], g_refs[i][idx], m_refs[i][idx], v_refs[i][idx])
                d_out[i][idx] = d
                m_out[i][idx] = mm
                v_out[i][idx] = vv

            if len(gs[i].shape) == 3:
                def step(s, carry, update=update):
                    update(s)
                    return carry

                lax.fori_loop(0, gs[i].shape[0], step, 0)
            else:
                update(Ellipsis)

    shapes = [jax.ShapeDtypeStruct(g.shape, F32) for g in gs]
    outs = pl.pallas_call(
        body, name="adamw_small", out_shape=shapes * 3,
        in_specs=[_vmem()] * (4 * n), out_specs=[_vmem()] * (3 * n),
        compiler_params=_params(vmem=VMEM_LIMIT_V7X),
    )(*gs, *ws, *ms, *vs)
    return outs[:n], outs[n:2 * n], outs[2 * n:]


def _zoh(lr, li, logdt, btr, bti):
    dt = jnp.exp(logdt)
    mag = jnp.exp(lr * dt)
    th = li * dt
    ar = mag * jnp.cos(th)
    ai = mag * jnp.sin(th)
    den = lr * lr + li * li
    nr = ar - 1.0
    cr = (nr * lr + ai * li) / den
    ci = (ai * lr - nr * li) / den
    return ar, ai, cr * btr - ci * bti, cr * bti + ci * btr


BD_GROUPS = 8
BD_ROWS = BD_GROUPS * SSM_GROUP
BD_COLS = BD_GROUPS * SSM_STATE
N_BD = SSM_GROUPS // BD_GROUPS


def _bd_mask():
    r = lax.broadcasted_iota(jnp.int32, (BD_ROWS, BD_COLS), 0) // SSM_GROUP
    c = lax.broadcasted_iota(jnp.int32, (BD_ROWS, BD_COLS), 1) // SSM_STATE
    return r == c


def _blockdiag_store(out_ref, t):
    mask = _bd_mask()
    for j in range(N_BD):
        rows = t[j * BD_GROUPS:(j + 1) * BD_GROUPS].reshape(BD_ROWS, SSM_STATE)
        out_ref[j] = jnp.where(mask, jnp.tile(rows, (1, BD_GROUPS)), 0.0).astype(out_ref.dtype)


def _blockdiag_load(m_ref, fold):
    mask = _bd_mask()
    parts = [_hdot(jnp.where(mask, m_ref[j], 0.0), fold).reshape(BD_GROUPS, SSM_GROUP, SSM_STATE)
             for j in range(N_BD)]
    return jnp.concatenate(parts, axis=0)


def _zoh_fwd(lr, li, logdt, btr, bti, c_re, c_im):
    def body(lr_ref, li_ref, dt_ref, br_ref, bi_ref, cr_ref, ci_ref, ar_ref, ai_ref, bbr_ref, bbi_ref, ccr_ref,
             cci_ref):
        ar, ai, bbr, bbi = _zoh(lr_ref[...], li_ref[...], dt_ref[...], br_ref[...], bi_ref[...])
        ar_ref[...] = ar
        ai_ref[...] = ai
        _blockdiag_store(bbr_ref, bbr)
        _blockdiag_store(bbi_ref, bbi)
        _blockdiag_store(ccr_ref, cr_ref[...])
        _blockdiag_store(cci_ref, ci_ref[...])

    s = jax.ShapeDtypeStruct
    bd = s((N_BD, BD_ROWS, BD_COLS), BF16)
    return pl.pallas_call(
        body, name="zoh_fwd", out_shape=[s(lr.shape, F32), s(lr.shape, F32), bd, bd, bd, bd],
        in_specs=[_vmem()] * 7, out_specs=[_vmem()] * 6,
    )(lr, li, logdt, btr, bti, c_re, c_im)


def _zoh_bwd(lr, li, logdt, btr, bti, dar, dai, dbb_re, dbb_im, dcc_re, dcc_im, fold):
    def body(lr_ref, li_ref, dt_ref, br_ref, bi_ref, dar_ref, dai_ref, dbbr_ref, dbbi_ref, dccr_ref, dcci_ref,
             fold_ref, glr_ref, gli_ref, gdt_ref, gbr_ref, gbi_ref, gcr_ref, gci_ref):
        fold_m = fold_ref[...]
        _, vjp = jax.vjp(_zoh, lr_ref[...], li_ref[...], dt_ref[...], br_ref[...], bi_ref[...])
        glr, gli, gdt, gbr, gbi = vjp((dar_ref[...], dai_ref[...], _blockdiag_load(dbbr_ref, fold_m),
                                       _blockdiag_load(dbbi_ref, fold_m)))
        glr_ref[...] = glr
        gli_ref[...] = gli
        gdt_ref[...] = gdt
        gbr_ref[...] = gbr.astype(BF16)
        gbi_ref[...] = gbi.astype(BF16)
        gcr_ref[...] = _blockdiag_load(dccr_ref, fold_m).astype(BF16)
        gci_ref[...] = _blockdiag_load(dcci_ref, fold_m).astype(BF16)

    s = jax.ShapeDtypeStruct
    return pl.pallas_call(
        body, name="zoh_bwd",
        out_shape=[s(lr.shape, F32), s(lr.shape, F32), s(logdt.shape, F32)] + [s(btr.shape, BF16)] * 4,
        in_specs=[_vmem()] * 12, out_specs=[_vmem()] * 7,
    )(lr, li, logdt, btr, bti, dar, dai, dbb_re, dbb_im, dcc_re, dcc_im, fold)


def _head_ones():
    r = np.arange(ATTN_W) // HEAD_DIM
    return jnp.asarray(r[:, None] == r[None, :], dtype=BF16)


def _head_fold():
    return jnp.asarray(np.tile(np.eye(HEAD_DIM), (ATTN_W // HEAD_DIM, 1)), dtype=BF16)


def _in_proj(x2, g_mix, w_in_sh):
    t_tok = x2.shape[0]
    tm = min(1024, t_tok)
    nt = t_tok // tm
    ag_w = _AllGather(1, cast=True)
    n_sem = len(ag_w.scratch())

    def owner(i):
        x, y, c = lax.axis_index("x"), lax.axis_index("y"), lax.axis_index("c")
        rel = i // 2
        px = jnp.where((rel == 1) | (rel == 3), 1 - x, x)
        py = jnp.where((rel == 2) | (rel == 3), 1 - y, y)
        pc = jnp.where(i % 2 == 1, 1 - c, c)
        return 4 * px + 2 * py + pc

    def body(*refs):
        x_ref, g_ref, w_ref, z_ref, xn_ref, wg_ref, xn_scr, w_land = refs[:8]
        sems_w, out_sem = refs[8:8 + n_sem], refs[8 + n_sem]
        i, t = pl.program_id(0), pl.program_id(1)
        _, first, passed, arrive_ici, arrive_d2d, _ = ag_w._plan([w_ref], [w_land], sems_w)

        @pl.when((i == 0) & (t == 0))
        def _():
            ag_w.start([w_ref], [w_land], sems_w)

        @pl.when((i == 1) & (t == 0))
        def _():
            arrive_d2d[0].wait_recv()

        for n in range(3):
            @pl.when((i == 2 + 2 * n) & (t == 0))
            def _(n=n):
                arrive_ici[n].wait_recv()
                passed[n].start()

            @pl.when((i == 3 + 2 * n) & (t == 0))
            def _(n=n):
                arrive_d2d[1 + n].wait_recv()

        @pl.when(i == 0)
        def _():
            x = x_ref[...]
            r = lax.rsqrt(jnp.mean(x * x, axis=-1, keepdims=True) + EPS)
            xn = (x * r * g_ref[...]).astype(BF16)
            xn_ref[...] = xn
            xn_scr[t] = xn

        z_ref[...] = _dot(xn_scr[t], w_land[owner(i)])

        @pl.when((i == N_DEV - 1) & (t == nt - 1))
        def _():
            for cp in first + passed:
                cp.wait_send()
            out = pltpu.make_async_copy(w_land, wg_ref, out_sem)
            out.start()
            out.wait()

    s = jax.ShapeDtypeStruct
    xmap = lambda i, t: (jnp.where(i == 0, t, nt - 1), 0)
    gathered = s((N_DEV,) + w_in_sh.shape, BF16)
    return pl.pallas_call(
        body, name="in_proj", grid=(N_DEV, nt),
        out_shape=[s((t_tok, IN_W), F32), s((t_tok, D_MODEL), BF16), gathered],
        in_specs=[pl.BlockSpec((tm, D_MODEL), xmap), _full(g_mix.shape), _full(w_in_sh.shape)],
        out_specs=[pl.BlockSpec((tm, COL_W), lambda i, t: (t, owner(i))), pl.BlockSpec((tm, D_MODEL), xmap),
                   pl.BlockSpec(memory_space=pl.ANY)],
        scratch_shapes=[pltpu.VMEM((nt, tm, D_MODEL), BF16), pltpu.VMEM(gathered.shape, BF16)] + ag_w.scratch()
        + [pltpu.SemaphoreType.DMA],
        compiler_params=_params(2, VMEM_LIMIT_V7X),
    )(x2, g_mix, w_in_sh)


TQ = 128
NEG = -1e30


def _head_col(t, lm):
    return jnp.max(jnp.where(lm, t, NEG), axis=-1, keepdims=True)


def _head_masks():
    lane = lax.broadcasted_iota(jnp.int32, (1, 1, LANES), 2)
    return [(lane // HEAD_DIM) == h for h in range(LANES // HEAD_DIM)]


def _stack_heads(t3, lms):
    return jnp.concatenate([jnp.where(lm, t3, jnp.zeros_like(t3)) for lm in lms], axis=1)


def _unstack_heads(t2, lms, tq):
    out = t2[:, :tq]
    for h in range(1, len(lms)):
        out = jnp.where(lms[h], t2[:, h * tq:(h + 1) * tq], out)
    return out


def _gather_classes(ref, dil, nt, tq, dtype):
    length = nt * tq
    if dil == 1:
        return ref[...].astype(dtype).reshape(nt, tq, LANES)
    parts = [ref[pl.ds(r, length, stride=dil), :].astype(dtype).reshape(nt, tq, LANES) for r in range(dil)]
    return jnp.concatenate(parts, axis=0)


def _scatter_classes(ref, val, dil, nt, tq, add):
    length = nt * tq
    for r in range(dil):
        rows = pl.ds(r, length, stride=dil) if dil > 1 else slice(None)
        part = val[r * nt:(r + 1) * nt].reshape(length, LANES)
        ref[rows, :] = ref[rows, :] + part if add else part


def _with_prev_tile(t3, dil, nt):
    parts = []
    for r in range(dil):
        t = t3[r * nt:(r + 1) * nt]
        parts.append(jnp.concatenate([t[:1], t[:-1]], axis=0))
    prev = parts[0] if dil == 1 else jnp.concatenate(parts, axis=0)
    return jnp.concatenate([prev, t3], axis=1)


def _band_valid(dil, nt, tq):
    if nt == 1:
        shape = (dil, tq, tq)
        return lax.broadcasted_iota(jnp.int32, shape, 1) >= lax.broadcasted_iota(jnp.int32, shape, 2)
    shape = (dil * nt, tq, 2 * tq)
    b = lax.broadcasted_iota(jnp.int32, shape, 0)
    c = lax.broadcasted_iota(jnp.int32, shape, 2)
    d = tq + lax.broadcasted_iota(jnp.int32, shape, 1) - c
    return (d >= 0) & (d <= tq) & (((b & (nt - 1)) != 0) | (c >= tq))


def _window_tiling(seq, window, dil):
    length = seq // dil
    tq = min(TQ, length)
    nt = length // tq
    assert length % tq == 0 and nt & (nt - 1) == 0 and (nt == 1 or window == tq * dil)
    return nt, tq


def _bqk(a, b):
    return jnp.einsum("bqd,bkd->bqk", a, b, preferred_element_type=F32)


def _bqd(a, b):
    return jnp.einsum("bqk,bkd->bqd", a, b, preferred_element_type=F32)


def _bkd(a, b):
    return jnp.einsum("bqk,bqd->bkd", a, b, preferred_element_type=F32)


def _qk_hat(q_ref, k_ref, gq_ref, gk_ref):
    lane = lax.broadcasted_iota(jnp.int32, (1, LANES), 1)

    def norm(raw, gain, scale):
        sq = raw * raw
        r = jnp.zeros_like(raw)
        for h in range(LANES // HEAD_DIM):
            lm = (lane // HEAD_DIM) == h
            ms = jnp.sum(jnp.where(lm, sq, 0.0), axis=-1, keepdims=True) * (1.0 / HEAD_DIM)
            r = jnp.where(lm, lax.rsqrt(ms + EPS), r)
        return raw * r * gain * scale

    return norm(q_ref[...], gq_ref[...], HEAD_DIM ** -0.5), norm(k_ref[...], gk_ref[...], 1.0)


def _zblock(seq, group):
    return pl.BlockSpec((seq, LANES), lambda b, hp: (b, group * (ATTN_W // LANES) + hp))


def _attn_fwd(z, gq2, gk2, nb, seq, late_sh):
    t_tok = nb * seq
    n_win = len(DILATED)
    host = _HostedGather(late_sh)
    n_late = host.n
    n_steps = (nb, ATTN_W // LANES)

    def body(*refs):
        (q_ref, k_ref, v_ref, ga_ref, gq_ref, gk_ref), refs = refs[:6], refs[6:]
        late_refs, refs = refs[:n_late], refs[n_late:]
        (o_ref, l_ref, ag_ref, qh_ref, kh_ref), refs = refs[:5], refs[5:]
        lateg_refs, refs = refs[:n_late], refs[n_late:]
        (qf, kf, oc, lc), host_scratch = refs[:4], refs[4:]
        step = pl.program_id(0) * n_steps[1] + pl.program_id(1)
        total = n_steps[0] * n_steps[1]

        @pl.when(step == 0)
        def _():
            host.start(late_refs, host_scratch)

        @pl.when(step == total // 2)
        def _():
            host.forward(late_refs, host_scratch)

        q_hat, k_hat = _qk_hat(q_ref, k_ref, gq_ref, gk_ref)
        qh_ref[...] = q_hat.astype(BF16)
        kh_ref[...] = k_hat.astype(BF16)
        qf[...] = q_hat
        kf[...] = k_hat
        lms = _head_masks()
        for w, (window, dil) in enumerate(DILATED):
            nt, tq = _window_tiling(seq, window, dil)
            q3 = _gather_classes(qf, dil, nt, tq, BF16)
            k3 = _gather_classes(kf, dil, nt, tq, BF16)
            v3 = _gather_classes(v_ref, dil, nt, tq, BF16)
            if nt > 1:
                k3, v3 = _with_prev_tile(k3, dil, nt), _with_prev_tile(v3, dil, nt)
            valid = _band_valid(dil, nt, tq)
            valid = jnp.concatenate([valid] * len(lms), axis=1)
            s = _bqk(_stack_heads(q3, lms), k3)
            m = jnp.max(jnp.where(valid, s, NEG), axis=-1, keepdims=True)
            p = jnp.where(valid, jnp.exp(s - m), 0.0)
            den = jnp.sum(p, axis=-1, keepdims=True)
            o = _unstack_heads(_bqd(p.astype(BF16), v3) / den, lms, tq)
            lse = _unstack_heads(jnp.broadcast_to(m + jnp.log(den), s.shape[:2] + (LANES,)), lms, tq)
            _scatter_classes(oc.at[w], o, dil, nt, tq, add=False)
            _scatter_classes(lc.at[w], lse, dil, nt, tq, add=False)
        mx = lc[0]
        for w in range(1, n_win):
            mx = jnp.maximum(mx, lc[w])
        tot = jnp.zeros_like(mx)
        o = jnp.zeros_like(mx)
        for w in range(n_win):
            e = jnp.exp(lc[w] - mx)
            tot = tot + e
            o = o + e * oc[w]
        o = o / tot
        o_ref[...] = o
        l_ref[...] = mx + jnp.log(tot)
        ga = ga_ref[...]
        ag_ref[...] = (o * ga * _sig(ga)).astype(BF16)

        @pl.when(step == total - 1)
        def _():
            host.finish(late_refs, host_scratch, lateg_refs)

    blk = pl.BlockSpec((seq, LANES), lambda b, hp: (b, hp))
    s = jax.ShapeDtypeStruct
    outs = pl.pallas_call(
        body, name="attn_fwd", grid=n_steps,
        out_shape=[s((t_tok, ATTN_W), F32), s((t_tok, ATTN_W), F32)] + [s((t_tok, ATTN_W), BF16)] * 3
        + host.out_shape(),
        in_specs=[_zblock(seq, 0), _zblock(seq, 1), _zblock(seq, 2), _zblock(seq, 3), _full(gq2.shape),
                  _full(gk2.shape)] + [_full(a.shape) for a in late_sh],
        out_specs=[blk] * 5 + [pl.BlockSpec(memory_space=pl.ANY)] * n_late,
        scratch_shapes=[pltpu.VMEM((seq, LANES), F32)] * 2 + [pltpu.VMEM((n_win, seq, LANES), F32)] * 2
        + host.scratch(),
        compiler_params=_params(2, VMEM_LIMIT_V7X),
    )(z, z, z, z, gq2, gk2, *late_sh)
    return outs[:5], outs[5:]


SCAN_COLS = 512


def _to_segments(dst_ref, val):
    seg = val.shape[0] // SUBLANES
    for n in range(dst_ref.shape[0]):
        for s in range(SUBLANES):
            dst_ref[n, pl.ds(s, seg, stride=SUBLANES), :] = val[s * seg:(s + 1) * seg, n * LANES:(n + 1) * LANES]


def _from_segments(src_ref):
    seg = src_ref.shape[1] // SUBLANES
    return jnp.concatenate(
        [jnp.concatenate([src_ref[n, pl.ds(s, seg, stride=SUBLANES), :] for s in range(SUBLANES)], axis=0)
         for n in range(src_ref.shape[0])], axis=1)


def _scan_chunk(re_ref, im_ref, a_re_ref, a_im_ref, carry_re, carry_im, rows, reverse, visit=None):
    seg = rows // SUBLANES
    assert seg & (seg - 1) == 0
    rowi = lax.broadcasted_iota(jnp.int32, (SUBLANES, SCAN_COLS), 0)
    edge = (SUBLANES - 1) if reverse else 0
    last = 0 if reverse else SUBLANES - 1
    at_edge = rowi == edge

    def cmul(ar, ai, br, bi):
        return ar * br - ai * bi, ar * bi + ai * br

    for c0 in range(0, N_STATE, SCAN_COLS):
        cols = slice(c0, c0 + SCAN_COLS)
        a1r = jnp.broadcast_to(a_re_ref[:, cols], (SUBLANES, SCAN_COLS))
        a1i = jnp.broadcast_to(a_im_ref[:, cols], (SUBLANES, SCAN_COLS))
        if reverse:
            a1i = -a1i

        def block_of(i):
            j = (seg - 1 - i) if reverse else i
            return j, pl.ds(pl.multiple_of(j * SUBLANES, SUBLANES), SUBLANES)

        def local(i, carry, cols=cols, a1r=a1r, a1i=a1i):
            xr, xi = carry
            _, blk = block_of(i)
            nr, ni = cmul(a1r, a1i, xr, xi)
            xr, xi = nr + re_ref[blk, cols], ni + im_ref[blk, cols]
            re_ref[blk, cols] = xr
            im_ref[blk, cols] = xi
            return xr, xi

        zero = jnp.zeros((SUBLANES, SCAN_COLS), F32)
        er, ei = lax.fori_loop(0, seg, local, (zero, zero))

        pr, pi = a1r, a1i
        for _ in range(seg.bit_length() - 1):
            pr, pi = cmul(pr, pi, pr, pi)
        cr, ci = carry_re[:, cols], carry_im[:, cols]
        inr, ini = cmul(pr, pi, cr, ci)
        er = er + jnp.where(at_edge, inr, 0.0)
        ei = ei + jnp.where(at_edge, ini, 0.0)
        for sft in (1, 2, 4):
            shift, keep = (SUBLANES - sft, rowi < SUBLANES - sft) if reverse else (sft, rowi >= sft)
            rs = jnp.where(keep, pltpu.roll(er, shift, 0), 0.0)
            ims = jnp.where(keep, pltpu.roll(ei, shift, 0), 0.0)
            dr, di = cmul(pr, pi, rs, ims)
            er, ei = er + dr, ei + di
            pr, pi = cmul(pr, pi, pr, pi)
        carry_re[:, cols] = jnp.broadcast_to(er[last:last + 1, :], (SUBLANES, SCAN_COLS))
        carry_im[:, cols] = jnp.broadcast_to(ei[last:last + 1, :], (SUBLANES, SCAN_COLS))
        one = (SUBLANES - 1) if reverse else 1
        kr = jnp.where(at_edge, cr, pltpu.roll(er, one, 0))
        ki = jnp.where(at_edge, ci, pltpu.roll(ei, one, 0))

        def fix(i, carry, cols=cols, a1r=a1r, a1i=a1i):
            kr, ki, acc = carry
            j, blk = block_of(i)
            kr, ki = cmul(a1r, a1i, kr, ki)
            xr, xi = re_ref[blk, cols] + kr, im_ref[blk, cols] + ki
            re_ref[blk, cols] = xr
            im_ref[blk, cols] = xi
            if visit is not None:
                acc = visit(cols, j, xr, xi, acc)
            return kr, ki, acc

        _, _, acc = lax.fori_loop(0, seg, fix, (kr, ki, (zero, zero)))
        if visit is not None:
            visit(cols, None, None, None, acc)


SSM_CHUNK = 512


def _ssm_fwd(z, a_re, a_im, bb_re, bb_im, cc_re, cc_im, d_skip, w_glu, b_glu, nb, seq):
    t_tok = nb * seq
    tc = min(SSM_CHUNK, seq)
    nch = seq // tc
    grp = N_STATE // 4

    def body(u_ref, gs_ref, ar_ref, ai_ref, bbr_ref, bbi_ref, ccr_ref, cci_ref, d_ref, wg_ref, bg_ref,
             xr_ref, xi_ref, y_ref, sg_ref, car_re, car_im, seg_u, seg_y):
        @pl.when(pl.program_id(1) == 0)
        def _():
            car_re[...] = jnp.zeros_like(car_re)
            car_im[...] = jnp.zeros_like(car_im)

        u = u_ref[...]
        _to_segments(seg_u, u)
        for j in range(4):
            uj = seg_u[j].astype(BF16)
            xr_ref[:, j * grp:(j + 1) * grp] = _dot(uj, bbr_ref[j])
            xi_ref[:, j * grp:(j + 1) * grp] = _dot(uj, bbi_ref[j])
        _scan_chunk(xr_ref, xi_ref, ar_ref, ai_ref, car_re, car_im, tc, reverse=False)
        for j in range(4):
            xr = xr_ref[:, j * grp:(j + 1) * grp].astype(BF16)
            xi = xi_ref[:, j * grp:(j + 1) * grp].astype(BF16)
            seg_y[j] = _dot_nt(xr, ccr_ref[j]) - _dot_nt(xi, cci_ref[j])
        y = _from_segments(seg_y) + d_ref[...] * u
        y_ref[...] = y
        yg, _ = _gelu_and_grad(y)
        gl = _dot(yg.astype(BF16), wg_ref[...]) + bg_ref[...]
        gs = gs_ref[...]
        sg_ref[...] = (yg * _sig(gl) * gs * _sig(gs)).astype(BF16)

    umap = lambda b, ch: (b * nch + ch, 4)
    gmap = lambda b, ch: (b * nch + ch, 5)
    row = lambda b, ch: (b * nch + ch, 0)
    s = jax.ShapeDtypeStruct
    consts = [a_re, a_im, bb_re, bb_im, cc_re, cc_im, d_skip, w_glu, b_glu]
    return pl.pallas_call(
        body, name="ssm_fwd", grid=(nb, nch),
        out_shape=[s((t_tok, N_STATE), F32), s((t_tok, N_STATE), F32), s((t_tok, SSM_W), F32),
                   s((t_tok, SSM_W), BF16)],
        in_specs=[pl.BlockSpec((tc, SSM_W), umap), pl.BlockSpec((tc, SSM_W), gmap)] + [_full(c.shape) for c in consts],
        out_specs=[pl.BlockSpec((tc, N_STATE), row), pl.BlockSpec((tc, N_STATE), row),
                   pl.BlockSpec((tc, SSM_W), row), pl.BlockSpec((tc, SSM_W), row)],
        scratch_shapes=[pltpu.VMEM((SUBLANES, N_STATE), F32), pltpu.VMEM((SUBLANES, N_STATE), F32),
                        pltpu.VMEM((4, tc, LANES), F32), pltpu.VMEM((4, tc, LANES), F32)],
        compiler_params=_params(2, VMEM_LIMIT_V7X),
    )(z, z, *consts)


def _tail(x2, tg2, ag, sg, p2, w_out, w_g, w_p, g_ple):
    t_tok = x2.shape[0]
    tm = min(512, t_tok)
    nt = t_tok // tm
    half = ATTN_W

    def body(x_ref, tg_ref, ag_ref, sg_ref, p_ref, wo_ref, wg_ref, wp_ref, gp_ref,
             dmix_ref, dh1_ref, loss_ref, dgp_ref, dwo_ref, dwg_ref, dwp_ref, acc_o, acc_g, acc_p):
        i = pl.program_id(0)

        @pl.when(i == 0)
        def _():
            loss_ref[...] = jnp.zeros_like(loss_ref)
            dgp_ref[...] = jnp.zeros_like(dgp_ref)
            acc_o[...] = jnp.zeros_like(acc_o)
            acc_g[...] = jnp.zeros_like(acc_g)
            acc_p[...] = jnp.zeros_like(acc_p)

        ag_t, sg_t = ag_ref[...], sg_ref[...]
        h1 = x_ref[...] + _dot(ag_t, wo_ref[0:half, :]) + _dot(sg_t, wo_ref[half:2 * half, :])
        r2 = lax.rsqrt(jnp.mean(h1 * h1, axis=-1, keepdims=True) + EPS)
        hnorm = h1 * r2
        gp = gp_ref[...]
        hn = (hnorm * gp).astype(BF16)
        gate = _sig(_dot(hn, wg_ref[...]))
        pb = p_ref[...].astype(BF16)
        pp = jnp.concatenate([_dot(pb, wp_ref[j]) for j in range(N_DEV)], axis=-1)
        h2 = h1 + gate * pp
        err = h2 - tg_ref[...]
        loss_ref[...] += 0.5 * jnp.sum(err * err) * (1.0 / D_MODEL)
        dh2 = err * (1.0 / D_MODEL)
        dpp = (dh2 * gate).astype(BF16)
        dgpre = (dh2 * pp * gate * (1.0 - gate)).astype(BF16)
        acc_p[...] += _dot_tn(pb, dpp)
        acc_g[...] += _dot_tn(hn, dgpre)
        dhn = _dot_nt(dgpre, wg_ref[...])
        dgp_ref[...] += jnp.sum(dhn * hnorm, axis=0, keepdims=True)
        a = dhn * gp
        dh1 = dh2 + r2 * (a - hnorm * jnp.mean(a * hnorm, axis=-1, keepdims=True))
        dh1_ref[...] = dh1
        dh1b = dh1.astype(BF16)
        acc_o[0:half, :] += _dot_tn(ag_t, dh1b)
        acc_o[half:2 * half, :] += _dot_tn(sg_t, dh1b)
        dmix_ref[...] = _dot_nt(dh1b, wo_ref[...])

        @pl.when(i == nt - 1)
        def _():
            dwo_ref[...] = acc_o[...].astype(BF16)
            dwg_ref[...] = acc_g[...].astype(BF16)
            for j in range(N_DEV):
                dwp_ref[j] = acc_p[:, j * LANES:(j + 1) * LANES].astype(BF16)

    row = lambda i: (i, 0)
    s = jax.ShapeDtypeStruct
    return pl.pallas_call(
        body, name="tail_fwd_bwd", grid=(nt,),
        out_shape=[s((t_tok, D_MODEL), F32), s((t_tok, D_MODEL), F32), s((SUBLANES, LANES), F32),
                   s((1, D_MODEL), F32), s((D_MODEL, D_MODEL), BF16), s((D_MODEL, D_MODEL), BF16),
                   s((N_DEV, PLE_DIM, LANES), BF16)],
        in_specs=[pl.BlockSpec((tm, D_MODEL), row), pl.BlockSpec((tm, D_MODEL), row),
                  pl.BlockSpec((tm, half), row), pl.BlockSpec((tm, half), row), pl.BlockSpec((tm, PLE_DIM), row),
                  _full(w_out.shape), _full(w_g.shape), _full(w_p.shape), _full(g_ple.shape)],
        out_specs=[pl.BlockSpec((tm, D_MODEL), row), pl.BlockSpec((tm, D_MODEL), row), _full((SUBLANES, LANES)),
                   _full((1, D_MODEL)), _full((D_MODEL, D_MODEL)), _full((D_MODEL, D_MODEL)),
                   _full((N_DEV, PLE_DIM, LANES))],
        scratch_shapes=[pltpu.VMEM((D_MODEL, D_MODEL), F32), pltpu.VMEM((D_MODEL, D_MODEL), F32),
                        pltpu.VMEM((PLE_DIM, D_MODEL), F32)],
        compiler_params=_params(1, VMEM_LIMIT_V7X),
    )(x2, tg2, ag, sg, p2, w_out, w_g, w_p, g_ple)


def _attn_bwd(qh, kh, z, o, lse, dmix, nb, seq, parts):
    t_tok = nb * seq
    n_rs = len(parts)
    rs = _ReduceScatter([p.shape for p in parts])
    n_steps = (nb, ATTN_W // LANES)

    def body(*refs):
        (q_ref, k_ref, v_ref, ga_ref, o_ref, l_ref, da_ref), refs = refs[:7], refs[7:]
        part_refs, refs = refs[:n_rs], refs[n_rs:]
        (dq_ref, dk_ref, dv_ref, dga_ref), refs = refs[:4], refs[4:]
        g_refs, refs = refs[:n_rs], refs[n_rs:]
        (qf, kf, dof, dlf), rs_scratch = refs[:4], refs[4:]
        b, hp = pl.program_id(0), pl.program_id(1)

        @pl.when((b == 0) & (hp == 0))
        def _():
            rs.start(part_refs, rs_scratch)

        ga, o_t, da = ga_ref[...], o_ref[...], da_ref[...]
        sga = _sig(ga)
        d_o = da * ga * sga
        dga_ref[...] = da * o_t * sga * (1.0 + ga * (1.0 - sga))
        lane = lax.broadcasted_iota(jnp.int32, (1, LANES), 1)
        d_oo = d_o * o_t
        delta = jnp.zeros_like(d_oo)
        for h in range(LANES // HEAD_DIM):
            lm2 = (lane // HEAD_DIM) == h
            delta = jnp.where(lm2, jnp.sum(jnp.where(lm2, d_oo, 0.0), axis=-1, keepdims=True), delta)
        qf[...] = q_ref[...].astype(F32)
        kf[...] = k_ref[...].astype(F32)
        dof[...] = d_o
        dlf[...] = delta
        dq_ref[...] = jnp.zeros_like(dq_ref)
        dk_ref[...] = jnp.zeros_like(dk_ref)
        dv_ref[...] = jnp.zeros_like(dv_ref)
        lms = _head_masks()
        for window, dil in DILATED:
            nt, tq = _window_tiling(seq, window, dil)
            q3 = _gather_classes(qf, dil, nt, tq, BF16)
            k3 = _gather_classes(kf, dil, nt, tq, BF16)
            v3 = _gather_classes(v_ref, dil, nt, tq, BF16)
            do3 = _gather_classes(dof, dil, nt, tq, BF16)
            lt3 = _gather_classes(l_ref, dil, nt, tq, F32)
            dl3 = _gather_classes(dlf, dil, nt, tq, F32)
            if nt > 1:
                k3, v3 = _with_prev_tile(k3, dil, nt), _with_prev_tile(v3, dil, nt)
            valid = _band_valid(dil, nt, tq)
            dq = jnp.zeros(q3.shape, F32)
            dk = jnp.zeros(k3.shape, F32)
            dv = jnp.zeros(k3.shape, F32)
            for lm in lms:
                qm = jnp.where(lm, q3, jnp.zeros_like(q3))
                dom = jnp.where(lm, do3, jnp.zeros_like(do3))
                p = jnp.where(valid, jnp.exp(_bqk(qm, k3) - _head_col(lt3, lm)), 0.0)
                dv = dv + _bkd(p.astype(BF16), dom)
                ds = (p * (_bqk(dom, v3) - _head_col(dl3, lm))).astype(BF16)
                dq = dq + jnp.where(lm, _bqd(ds, k3), 0.0)
                dk = dk + _bkd(ds, qm)
            _scatter_classes(dq_ref, dq, dil, nt, tq, add=True)
            for ref, g in ((dk_ref, dk), (dv_ref, dv)):
                if nt > 1:
                    own, prev = g[:, tq:, :], g[:, :tq, :]
                    shifted = []
                    for r in range(dil):
                        t = prev[r * nt:(r + 1) * nt]
                        shifted.append(jnp.concatenate([t[1:], jnp.zeros_like(t[:1])], axis=0))
                    g = own + (shifted[0] if dil == 1 else jnp.concatenate(shifted, axis=0))
                _scatter_classes(ref, g, dil, nt, tq, add=True)

        @pl.when((b == n_steps[0] - 1) & (hp == n_steps[1] - 1))
        def _():
            rs.finish(part_refs, rs_scratch, g_refs)

    blk = pl.BlockSpec((seq, LANES), lambda b, hp: (b, hp))
    s = jax.ShapeDtypeStruct
    outs = pl.pallas_call(
        body, name="attn_bwd", grid=n_steps,
        out_shape=[s((t_tok, ATTN_W), F32)] * 4 + [s(p.shape[1:], F32) for p in parts],
        in_specs=[blk, blk, _zblock(seq, 2), _zblock(seq, 3), blk, blk, blk]
        + [pl.BlockSpec(memory_space=pl.ANY)] * n_rs,
        out_specs=[blk] * 4 + [_full(p.shape[1:]) for p in parts],
        scratch_shapes=[pltpu.VMEM((seq, LANES), F32)] * 4 + rs.scratch(parts[0].dtype),
        compiler_params=_params(2, VMEM_LIMIT_V7X),
    )(qh, kh, z, z, o, lse, dmix, *parts)
    return outs[:4], outs[4:]


def _ssm_bwd(z, dmix, y, x_re, x_im, a_re, a_im, bb_re, bb_im, cc_re, cc_im, d_skip, w_glu, b_glu, nb, seq):
    t_tok = nb * seq
    tc = min(SSM_CHUNK, seq)
    nch = seq // tc
    grp = N_STATE // 4

    def body(u_ref, gs_ref, ds_ref, y_ref, xr_ref, xi_ref, xpr_ref, xpi_ref,
             ar_ref, ai_ref, bbr_ref, bbi_ref, ccr_ref, cci_ref, d_ref, wg_ref, bg_ref,
             du_ref, dgs_ref, dwg_ref, dbg_ref, dd_ref, dar_ref, dai_ref, dbbr_ref, dbbi_ref, dccr_ref, dcci_ref,
             lam_re, lam_im, car_re, car_im, acc_wg, seg_a, seg_b, ent_re, ent_im):
        step = pl.program_id(1)
        first_chunk = step == nch - 1

        @pl.when((pl.program_id(0) == 0) & (step == 0))
        def _():
            acc_wg[...] = jnp.zeros_like(acc_wg)
            for ref in (dbg_ref, dd_ref, dar_ref, dai_ref, dbbr_ref, dbbi_ref, dccr_ref, dcci_ref):
                ref[...] = jnp.zeros_like(ref)

        @pl.when(step == 0)
        def _():
            car_re[...] = jnp.zeros_like(car_re)
            car_im[...] = jnp.zeros_like(car_im)

        u, gs, dssm, y = u_ref[...], gs_ref[...], ds_ref[...], y_ref[...]
        yg, dgelu = _gelu_and_grad(y)
        ygb = yg.astype(BF16)
        sgl = _sig(_dot(ygb, wg_ref[...]) + bg_ref[...])
        sgs = _sig(gs)
        dout = dssm * gs * sgs
        dgs_ref[...] = dssm * yg * sgl * sgs * (1.0 + gs * (1.0 - sgs))
        dgl = dout * yg * sgl * (1.0 - sgl)
        dglb = dgl.astype(BF16)
        dyg = dout * sgl + _dot_nt(dglb, wg_ref[...])
        acc_wg[...] += _dot_tn(ygb, dglb)
        dbg_ref[...] += jnp.sum(dgl, axis=0, keepdims=True)
        dy = dyg * dgelu
        dd_ref[...] += jnp.sum(dy * u, axis=0, keepdims=True)
        _to_segments(seg_a, dy)
        _to_segments(seg_b, u)
        for j in range(4):
            dyj = seg_a[j].astype(BF16)
            sl = slice(j * grp, (j + 1) * grp)
            lam_re[:, sl] = _dot(dyj, ccr_ref[j])
            lam_im[:, sl] = -_dot(dyj, cci_ref[j])
            dccr_ref[j] += _dot_tn(dyj, xr_ref[:, sl].astype(BF16))
            dcci_ref[j] -= _dot_tn(dyj, xi_ref[:, sl].astype(BF16))

        keep_prev = jnp.where(first_chunk, 0.0, 1.0)
        seg = tc // SUBLANES
        last_blk = pl.ds((seg - 1) * SUBLANES, SUBLANES)
        row0 = lax.broadcasted_iota(jnp.int32, (SUBLANES, N_STATE), 0) == 0
        for src, prev, dst in ((xr_ref, xpr_ref, ent_re), (xi_ref, xpi_ref, ent_im)):
            before = jnp.broadcast_to(prev[SUBLANES - 1:SUBLANES, :] * keep_prev, (SUBLANES, N_STATE))
            dst[...] = jnp.where(row0, before, pltpu.roll(src[last_blk, :], 1, 0))

        def visit(cols, j, lr, li, acc):
            if j is None:
                dar_ref[:, cols] += jnp.sum(acc[0], axis=0, keepdims=True)
                dai_ref[:, cols] += jnp.sum(acc[1], axis=0, keepdims=True)
                return None
            blk = pl.ds(pl.multiple_of(jnp.maximum(j - 1, 0) * SUBLANES, SUBLANES), SUBLANES)
            inside = j > 0
            xpr = jnp.where(inside, xr_ref[blk, cols], ent_re[:, cols])
            xpi = jnp.where(inside, xi_ref[blk, cols], ent_im[:, cols])
            return acc[0] + lr * xpr + li * xpi, acc[1] + li * xpr - lr * xpi

        _scan_chunk(lam_re, lam_im, ar_ref, ai_ref, car_re, car_im, tc, reverse=True, visit=visit)

        for j in range(4):
            sl = slice(j * grp, (j + 1) * grp)
            lr = lam_re[:, sl].astype(BF16)
            li = lam_im[:, sl].astype(BF16)
            uj = seg_b[j].astype(BF16)
            seg_a[j] = _dot_nt(lr, bbr_ref[j]) + _dot_nt(li, bbi_ref[j])
            dbbr_ref[j] += _dot_tn(uj, lr)
            dbbi_ref[j] += _dot_tn(uj, li)
        du_ref[...] = _from_segments(seg_a) + dy * d_ref[...]

        @pl.when((pl.program_id(0) == nb - 1) & (step == nch - 1))
        def _():
            dwg_ref[...] = acc_wg[...].astype(BF16)

    rev = lambda b, ch: b * nch + (nch - 1 - ch)
    umap = lambda b, ch: (rev(b, ch), 4)
    gmap = lambda b, ch: (rev(b, ch), 5)
    smap = lambda b, ch: (rev(b, ch), 1)
    row = lambda b, ch: (rev(b, ch), 0)
    prev = lambda b, ch: (jnp.maximum(rev(b, ch) * (tc // SUBLANES) - 1, 0), 0)
    s = jax.ShapeDtypeStruct
    consts = [a_re, a_im, bb_re, bb_im, cc_re, cc_im, d_skip, w_glu, b_glu]
    acc_shapes = [s((1, SSM_W), F32), s((1, SSM_W), F32), s((1, N_STATE), F32), s((1, N_STATE), F32),
                  s(bb_re.shape, F32), s(bb_re.shape, F32), s(cc_re.shape, F32), s(cc_re.shape, F32)]
    return pl.pallas_call(
        body, name="ssm_bwd", grid=(nb, nch),
        out_shape=[s((t_tok, SSM_W), F32), s((t_tok, SSM_W), F32), s((SSM_W, SSM_W), BF16)] + acc_shapes,
        in_specs=[pl.BlockSpec((tc, SSM_W), umap), pl.BlockSpec((tc, SSM_W), gmap), pl.BlockSpec((tc, SSM_W), smap),
                  pl.BlockSpec((tc, SSM_W), row), pl.BlockSpec((tc, N_STATE), row), pl.BlockSpec((tc, N_STATE), row),
                  pl.BlockSpec((SUBLANES, N_STATE), prev), pl.BlockSpec((SUBLANES, N_STATE), prev)]
        + [_full(c.shape) for c in consts],
        out_specs=[pl.BlockSpec((tc, SSM_W), row), pl.BlockSpec((tc, SSM_W), row), _full((SSM_W, SSM_W))]
        + [_full(a.shape) for a in acc_shapes],
        scratch_shapes=[pltpu.VMEM((tc, N_STATE), F32), pltpu.VMEM((tc, N_STATE), F32),
                        pltpu.VMEM((SUBLANES, N_STATE), F32), pltpu.VMEM((SUBLANES, N_STATE), F32),
                        pltpu.VMEM((SSM_W, SSM_W), F32), pltpu.VMEM((4, tc, LANES), F32),
                        pltpu.VMEM((4, tc, LANES), F32),
                        pltpu.VMEM((SUBLANES, N_STATE), F32), pltpu.VMEM((SUBLANES, N_STATE), F32)],
        compiler_params=_params(2, VMEM_LIMIT_V7X),
    )(z, z, dmix, y, x_re, x_im, x_re, x_im, *consts)


def _dz_and_dx(x2, z, dqh, dkh, dvb, dga, du, dgs, dh1, w_in_g, g_mix, gq_t, gk_t, ones_bd, fold):
    t_tok = x2.shape[0]
    tm = min(512, t_tok)
    nt = t_tok // tm
    a_w = ATTN_W

    def head_norm_bwd(raw, d_hat, gain, scale, ones):
        r = lax.rsqrt(_hdot(raw * raw, ones) * (1.0 / HEAD_DIM) + EPS)
        n = raw * r
        a = d_hat * gain * scale
        d_raw = r * (a - n * (_hdot(a * n, ones) * (1.0 / HEAD_DIM)))
        return d_raw, jnp.sum(d_hat * n * scale, axis=0, keepdims=True)

    def body(x_ref, q_ref, k_ref, dq_ref, dk_ref, dv_ref, dga_ref, du_ref, dgs_ref, dh1_ref, w_ref, g_ref,
             gq_ref, gk_ref, ones_ref, fold_ref, dz_ref, gx_ref, dgm_ref, dgq_ref, dgk_ref, acc_q, acc_k):
        i = pl.program_id(0)

        @pl.when(i == 0)
        def _():
            dgm_ref[...] = jnp.zeros_like(dgm_ref)
            acc_q[...] = jnp.zeros_like(acc_q)
            acc_k[...] = jnp.zeros_like(acc_k)

        ones = ones_ref[...]
        dq, sq = head_norm_bwd(q_ref[...], dq_ref[...], gq_ref[...], HEAD_DIM ** -0.5, ones)
        dk, sk = head_norm_bwd(k_ref[...], dk_ref[...], gk_ref[...], 1.0, ones)
        acc_q[...] += jnp.broadcast_to(sq, acc_q.shape)
        acc_k[...] += jnp.broadcast_to(sk, acc_k.shape)
        parts = (dq, dk, dv_ref[...], dga_ref[...], du_ref[...], dgs_ref[...])
        for n, part in enumerate(parts):
            dz_ref[:, n * a_w:(n + 1) * a_w] = part.astype(BF16)
        dxn = jnp.zeros((tm, D_MODEL), F32)
        for j in range(N_DEV):
            dxn = dxn + _dot_nt(dz_ref[:, j * COL_W:(j + 1) * COL_W], w_ref[j])
        x = x_ref[...]
        r1 = lax.rsqrt(jnp.mean(x * x, axis=-1, keepdims=True) + EPS)
        xnorm = x * r1
        dgm_ref[...] += jnp.sum(dxn * xnorm, axis=0, keepdims=True)
        a = dxn * g_ref[...]
        gx_ref[...] = dh1_ref[...] + r1 * (a - xnorm * jnp.mean(a * xnorm, axis=-1, keepdims=True))

        @pl.when(i == nt - 1)
        def _():
            dgq_ref[...] = _hdot(acc_q[...], fold_ref[...])
            dgk_ref[...] = _hdot(acc_k[...], fold_ref[...])

    row = lambda i: (i, 0)
    col = lambda n: (lambda i: (i, n))
    s = jax.ShapeDtypeStruct
    half = pl.BlockSpec((tm, a_w), row)
    return pl.pallas_call(
        body, name="dz_dx", grid=(nt,),
        out_shape=[s((t_tok, IN_W), BF16), s((t_tok, D_MODEL), F32), s((1, D_MODEL), F32),
                   s((SUBLANES, HEAD_DIM), F32), s((SUBLANES, HEAD_DIM), F32)],
        in_specs=[pl.BlockSpec((tm, D_MODEL), row), pl.BlockSpec((tm, a_w), col(0)), pl.BlockSpec((tm, a_w), col(1)),
                  half, half, half, half, half, half, pl.BlockSpec((tm, D_MODEL), row),
                  _full(w_in_g.shape), _full(g_mix.shape), _full(gq_t.shape), _full(gk_t.shape),
                  _full(ones_bd.shape), _full(fold.shape)],
        out_specs=[pl.BlockSpec((tm, IN_W), row), pl.BlockSpec((tm, D_MODEL), row), _full((1, D_MODEL)),
                   _full((SUBLANES, HEAD_DIM)), _full((SUBLANES, HEAD_DIM))],
        scratch_shapes=[pltpu.VMEM((SUBLANES, a_w), F32), pltpu.VMEM((SUBLANES, a_w), F32)],
        compiler_params=_params(1, VMEM_LIMIT_V7X),
    )(x2, z, z, dqh, dkh, dvb, dga, du, dgs, dh1, w_in_g, g_mix, gq_t, gk_t, ones_bd, fold)


def _dw_in(xn, dz, glu_parts, smalls):
    t_tok = xn.shape[0]
    tk = min(2048, t_tok)
    nk = t_tok // tk
    rs = _ReduceScatter([glu_parts.shape])
    n_small = len(smalls)
    ag = _AllGather(n_small, cast=False)
    n_rs = len(rs.scratch(BF16))

    def place():
        x, y, c = lax.axis_index("x"), lax.axis_index("y"), lax.axis_index("c")
        return x, y, c, [(1 - x, y), (x, 1 - y), (1 - x, 1 - y)]

    def target(i):
        x, y, c, _ = place()
        n = i // 2
        px = jnp.where((n == 0) | (n == 2), 1 - x, x)
        py = jnp.where((n == 1) | (n == 2), 1 - y, y)
        pc = jnp.where(i % 2 == 0, 1 - c, c)
        return 4 * px + 2 * py + pc

    chunk, chunks = _row_chunks(D_MODEL)

    def body(*refs):
        (xn_ref, dz_ref, glu_ref), refs = refs[:3], refs[3:]
        small_refs, refs = list(refs[:n_small]), refs[n_small:]
        (gin_ref, gglu_ref), refs = refs[:2], refs[2:]
        gath_refs, refs = list(refs[:n_small]), refs[n_small:]
        (acc, stage, land, send_sems, recv_sems), rest = refs[:5], refs[5:]
        rs_scratch, ag_sems = rest[:n_rs], rest[n_rs:]
        i, k = pl.program_id(0), pl.program_id(1)
        x, y, c, chips = place()

        def push(slot, to):
            return pltpu.make_async_remote_copy(
                src_ref=stage.at[slot], dst_ref=land.at[slot], send_sem=send_sems.at[slot],
                recv_sem=recv_sems.at[slot], device_id=to, device_id_type=MESH)

        pushes = [push(n, (x, y, 1 - c)) for n in range(4)] + [push(4 + n, (*chips[n], c)) for n in range(3)]

        def staged(slot, plus=None):
            def put(s, carry):
                r = pl.ds(pl.multiple_of(s * chunk, chunk), chunk)
                val = acc[r, :]
                if plus is not None:
                    val = val + land[plus, r, :].astype(F32)
                stage[slot, r, :] = val.astype(BF16)
                return carry

            lax.fori_loop(0, chunks, put, 0)

        @pl.when((i == 0) & (k == 0))
        def _():
            rs.start([glu_ref], rs_scratch)
            ag.start(small_refs, gath_refs, ag_sems)

        @pl.when((i == N_DEV // 2) & (k == 0))
        def _():
            ag.forward(small_refs, gath_refs, ag_sems)

        @pl.when(k == 0)
        def _():
            acc[...] = jnp.zeros_like(acc)

        acc[...] += _dot_tn(xn_ref[...], dz_ref[...])

        for n in range(4):
            @pl.when((k == nk - 1) & (i == 2 * n))
            def _(n=n):
                staged(n)
                pushes[n].start()

        for n in range(3):
            @pl.when((k == nk - 1) & (i == 2 * n + 1))
            def _(n=n):
                pushes[n].wait_recv()
                staged(4 + n, plus=n)
                pushes[4 + n].start()

        @pl.when((k == nk - 1) & (i == N_DEV - 1))
        def _():
            for slot in range(3, N_DEV - 1):
                pushes[slot].wait_recv()

            def add(s, carry):
                r = pl.ds(pl.multiple_of(s * chunk, chunk), chunk)
                total = acc[r, :]
                for slot in range(3, N_DEV - 1):
                    total = total + land[slot, r, :].astype(F32)
                gin_ref[r, :] = total
                return carry

            lax.fori_loop(0, chunks, add, 0)
            for cp in pushes:
                cp.wait_send()
            rs.finish([glu_ref], rs_scratch, [gglu_ref])
            ag.finish(small_refs, gath_refs, ag_sems)

    any_spec = pl.BlockSpec(memory_space=pl.ANY)
    s = jax.ShapeDtypeStruct
    outs = pl.pallas_call(
        body, name="dw_in", grid=(N_DEV, nk),
        out_shape=[s((D_MODEL, COL_W), F32), s(glu_parts.shape[1:], F32)]
        + [s((N_DEV,) + a.shape, a.dtype) for a in smalls],
        in_specs=[pl.BlockSpec((tk, D_MODEL), lambda i, k: (k, 0)),
                  pl.BlockSpec((tk, COL_W), lambda i, k: (k, target(i))), any_spec] + [any_spec] * n_small,
        out_specs=[_full((D_MODEL, COL_W)), _full(glu_parts.shape[1:])] + [any_spec] * n_small,
        scratch_shapes=[pltpu.VMEM((D_MODEL, COL_W), F32), pltpu.VMEM((N_DEV - 1, D_MODEL, COL_W), BF16),
                        pltpu.VMEM((N_DEV - 1, D_MODEL, COL_W), BF16), pltpu.SemaphoreType.DMA((N_DEV - 1,)),
                        pltpu.SemaphoreType.DMA((N_DEV - 1,))] + rs.scratch(BF16) + ag.scratch(),
        compiler_params=_params(2, VMEM_LIMIT_V7X),
    )(xn, dz, glu_parts, *smalls)
    return outs[0], outs[1], outs[2:]


SMALL = ("mix_norm", "q_norm", "k_norm", "lambda_re", "lambda_im", "log_dt", "b_re", "b_im", "c_re", "c_im",
         "d_skip", "b_glu", "ple_norm")
BIG = ("w_in", "w_glu", "w_out", "w_ple_gate", "w_ple_proj")
WEIGHTS = ("mix_norm", "w_in", "q_norm", "k_norm", "lambda_re", "lambda_im", "log_dt", "b_re", "b_im", "c_re",
           "c_im", "d_skip", "w_glu", "b_glu", "w_out", "ple_norm", "w_ple_gate", "w_ple_proj")


def kernel(x, p, mix_norm, w_in, q_norm, k_norm, lambda_re, lambda_im, log_dt, b_re, b_im, c_re, c_im, d_skip, w_glu, b_glu, w_out, ple_norm, w_ple_gate, w_ple_proj, loss_target, m_mix_norm, m_w_in, m_q_norm, m_k_norm, m_lambda_re, m_lambda_im, m_log_dt, m_b_re, m_b_im, m_c_re, m_c_im, m_d_skip, m_w_glu, m_b_glu, m_w_out, m_ple_norm, m_w_ple_gate, m_w_ple_proj, v_mix_norm, v_w_in, v_q_norm, v_k_norm, v_lambda_re, v_lambda_im, v_log_dt, v_b_re, v_b_im, v_c_re, v_c_im, v_d_skip, v_w_glu, v_b_glu, v_w_out, v_ple_norm, v_w_ple_gate, v_w_ple_proj):
    env = dict(locals())
    w = {n: env[n] for n in WEIGHTS}
    m = {n: env["m_" + n] for n in WEIGHTS}
    v = {n: env["v_" + n] for n in WEIGHTS}
    nb, seq, _ = x.shape
    t_tok = nb * seq
    x2 = x.reshape(t_tok, D_MODEL)
    tg2 = loss_target.reshape(t_tok, D_MODEL)
    p2 = p.reshape(t_tok, PLE_DIM)

    shard2d = {"w_in": (D_MODEL, COL_W), "w_glu": (SSM_W // N_DEV, SSM_W), "w_out": (D_MODEL // N_DEV, D_MODEL),
               "w_ple_gate": (D_MODEL // N_DEV, D_MODEL), "w_ple_proj": (PLE_DIM, D_MODEL // N_DEV)}
    w_sh = [w[n].reshape(shard2d[n]) for n in BIG]

    g3 = (SSM_GROUPS, 1, SSM_STATE)
    lr3, li3 = lambda_re.reshape(g3), lambda_im.reshape(g3)
    dt3 = log_dt.reshape(SSM_GROUPS, 1, 1)
    btr = b_re[0].transpose(0, 2, 1)
    bti = b_im[0].transpose(0, 2, 1)
    a_re3, a_im3, bb_re, bb_im, cc_re, cc_im = _zoh_fwd(lr3, li3, dt3, btr, bti, c_re[0], c_im[0])
    a_re, a_im = a_re3.reshape(1, N_STATE), a_im3.reshape(1, N_STATE)

    ones_bd = _head_ones()
    fold = _head_fold()
    gq_t = jnp.tile(q_norm, (1, ATTN_W // HEAD_DIM))
    gk_t = jnp.tile(k_norm, (1, ATTN_W // HEAD_DIM))

    gq2 = jnp.tile(q_norm, (1, LANES // HEAD_DIM))
    gk2 = jnp.tile(k_norm, (1, LANES // HEAD_DIM))

    z, xn, w_in_g = _in_proj(x2, mix_norm, w_sh[0])
    (o, lse, ag, qh, kh), (w_glu_g, w_out_g, w_g_g, w_p_g) = _attn_fwd(z, gq2, gk2, nb, seq, w_sh[1:])
    w_glu_f = w_glu_g.reshape(SSM_W, SSM_W)
    w_out_f = w_out_g.reshape(D_MODEL, D_MODEL)
    w_g_f = w_g_g.reshape(D_MODEL, D_MODEL)
    x_re, x_im, y, sg = _ssm_fwd(z, a_re, a_im, bb_re, bb_im, cc_re, cc_im, d_skip, w_glu_f, b_glu, nb, seq)
    dmix, dh1, loss_t, d_ple, dw_out, dw_g, dw_p = _tail(x2, tg2, ag, sg, p2, w_out_f, w_g_f, w_p_g, ple_norm)

    early_parts = [dw_out.reshape(N_DEV, D_MODEL // N_DEV, D_MODEL), dw_g.reshape(N_DEV, D_MODEL // N_DEV, D_MODEL),
                   dw_p]
    (dqh, dkh, dvb, dga), (g_out, g_g, g_p) = _attn_bwd(qh, kh, z, o, lse, dmix, nb, seq, early_parts)
    (du, dgs, dw_glu, d_bglu, d_dskip, da_re, da_im, dbb_re, dbb_im, dcc_re, dcc_im) = _ssm_bwd(
        z, dmix, y, x_re, x_im, a_re, a_im, bb_re, bb_im, cc_re, cc_im, d_skip, w_glu_f, b_glu, nb, seq)
    d_lr, d_li, d_dt, d_btr, d_bti, d_cr, d_ci = _zoh_bwd(
        lr3, li3, dt3, btr, bti, da_re.reshape(g3), da_im.reshape(g3), dbb_re, dbb_im, dcc_re, dcc_im, fold)

    swapped = ("b_re", "b_im")

    def to_own(n, a):
        a = a.reshape(a.shape[1:]) if a.ndim > 2 else a
        return a.transpose(0, 2, 1) if n in swapped else a

    def from_own(n, a):
        a = a.transpose(0, 2, 1) if n in swapped else a
        return a.reshape(w[n].shape)

    own = {n: to_own(n, w[n]).shape for n in SMALL}
    dz, gx, d_mix, d_gq, d_gk = _dz_and_dx(x2, z, dqh, dkh, dvb, dga, du, dgs, dh1, w_in_g, mix_norm, gq_t, gk_t,
                                           ones_bd, fold)
    small_g = {"mix_norm": d_mix, "q_norm": d_gq[0:1], "k_norm": d_gk[0:1], "lambda_re": d_lr, "lambda_im": d_li,
               "log_dt": d_dt, "b_re": d_btr, "b_im": d_bti, "c_re": d_cr, "c_im": d_ci,
               "d_skip": d_dskip, "b_glu": d_bglu, "ple_norm": d_ple}
    g_in, g_glu, gathered = _dw_in(xn, dz, dw_glu.reshape(N_DEV, SSM_W // N_DEV, SSM_W),
                                   [small_g[n].reshape(own[n]) for n in SMALL] + [loss_t])
    g_sh = [g_in, g_glu, g_out, g_g, g_p]
    d_sh, m_sh, v_sh = _adamw_shards(g_sh, w_sh, [m[n].reshape(shard2d[n]) for n in BIG],
                                     [v[n].reshape(shard2d[n]) for n in BIG])

    *g_small, loss_sum = _small_sum(list(gathered))
    d_small, m_small, v_small = _adamw_small(
        g_small, *[[to_own(n, src[n]) for n in SMALL] for src in (w, m, v)])

    grads, deltas, new_m, new_v = {}, {}, {}, {}
    for dst, arrs in ((grads, g_small), (deltas, d_small), (new_m, m_small), (new_v, v_small)):
        for n, a in zip(SMALL, arrs):
            dst[n] = from_own(n, a)
    for i, n in enumerate(BIG):
        grads[n] = g_sh[i].reshape(w[n].shape)
        deltas[n] = d_sh[i].reshape(w[n].shape)
        new_m[n] = m_sh[i].reshape(w[n].shape)
        new_v[n] = v_sh[i].reshape(w[n].shape)

    loss = loss_sum[0, 0]
    return (loss, gx.reshape(x.shape), *[grads[n] for n in WEIGHTS], *[deltas[n] for n in WEIGHTS],
            *[new_m[n] for n in WEIGHTS], *[new_v[n] for n in WEIGHTS])
```

```python
import math

import numpy as np
import jax
import jax.numpy as jnp
from jax import lax
from jax.experimental import pallas as pl
from jax.experimental.pallas import tpu as pltpu

F32 = jnp.float32
BF16 = jnp.bfloat16
MESH = pl.DeviceIdType.MESH
AXES = ("x", "y", "c")
N_DEV = 8

D_MODEL = 1024
HEAD_DIM = 64
ATTN_W = 512
SSM_W = 512
SSM_GROUPS = 32
SSM_GROUP = 16
SSM_STATE = 64
N_STATE = SSM_GROUPS * SSM_STATE
PLE_DIM = 256
IN_W = 3072
COL_W = IN_W // N_DEV
DILATED = ((128, 1), (512, 4), (2048, 16))
EPS = 1e-6
INV_SQRT2 = 1.0 / math.sqrt(2.0)
INV_SQRT_2PI = 1.0 / math.sqrt(2.0 * math.pi)

ADAM_LR, ADAM_B1, ADAM_B2, ADAM_EPS, ADAM_WD, ADAM_STEP = 0.001, 0.9, 0.999, 1e-08, 0.01, 10

VMEM_LIMIT_V7X = 56 * 1024 * 1024
SUBLANES = 8
LANES = 128


def _params(n_axes=None, vmem=None):
    kw = {}
    if n_axes:
        kw["dimension_semantics"] = ("arbitrary",) * n_axes
    if vmem:
        kw["vmem_limit_bytes"] = vmem
    return pltpu.CompilerParams(**kw)


def _dot(a, b):
    return jnp.dot(a, b, preferred_element_type=F32)


def _dot_nt(a, b):
    return lax.dot_general(a, b, (((1,), (1,)), ((), ())), preferred_element_type=F32)


def _dot_tn(a, b):
    return lax.dot_general(a, b, (((0,), (0,)), ((), ())), preferred_element_type=F32)


def _hdot(a, ones):
    hi = a.astype(BF16)
    lo = (a - hi.astype(F32)).astype(BF16)
    return _dot(hi, ones) + _dot(lo, ones)


def _sig(x):
    return 1.0 / (1.0 + jnp.exp(-x))


def _gelu_and_grad(y):
    cdf = 0.5 * (1.0 + lax.erf(y * INV_SQRT2))
    pdf = jnp.exp(-0.5 * y * y) * INV_SQRT_2PI
    return y * cdf, cdf + y * pdf


def _vmem():
    return pl.BlockSpec(memory_space=pltpu.VMEM)


def _full(shape):
    nd = len(shape)
    return pl.BlockSpec(shape, lambda *_: (0,) * nd)


class _AllGather:
    def __init__(self, n, cast):
        self.n, self.cast = n, cast

    def scratch(self):
        n = self.n
        return [pltpu.SemaphoreType.DMA((7 * n,)), pltpu.SemaphoreType.DMA((7 * n,)), pltpu.SemaphoreType.DMA((n,))]

    def _plan(self, src_refs, out_refs, sems):
        send_sems, recv_sems, own_sems = sems
        x, y, c = lax.axis_index("x"), lax.axis_index("y"), lax.axis_index("c")
        me, sibling = (x, y, c), (x, y, 1 - c)
        chips = [(1 - x, y), (x, 1 - y), (1 - x, 1 - y)]

        def idx(px, py, pc):
            return 4 * px + 2 * py + pc

        def copy(i, k, block, to, own_src=False):
            ref = out_refs[i].at[idx(*block)]
            return pltpu.make_async_remote_copy(
                src_ref=src_refs[i] if own_src and not self.cast else ref, dst_ref=ref,
                send_sem=send_sems.at[7 * i + k], recv_sem=recv_sems.at[7 * i + k],
                device_id=to, device_id_type=MESH)

        first, passed, arrive_ici, arrive_d2d, own = [], [], [], [], []
        for i in range(self.n):
            first.append(copy(i, 0, me, sibling, own_src=True))
            first += [copy(i, 1 + j, me, (*chip, c), own_src=True) for j, chip in enumerate(chips)]
            arrive_ici += [copy(i, 1 + j, (*chip, c), me) for j, chip in enumerate(chips)]
            passed += [copy(i, 4 + j, (*chip, c), sibling) for j, chip in enumerate(chips)]
            arrive_d2d.append(copy(i, 0, sibling, me))
            arrive_d2d += [copy(i, 4 + j, (*chip, 1 - c), me) for j, chip in enumerate(chips)]
            if not self.cast:
                own.append(pltpu.make_async_copy(src_refs[i], out_refs[i].at[idx(*me)], own_sems.at[i]))
        return idx(*me), first, passed, arrive_ici, arrive_d2d, own

    def start(self, src_refs, out_refs, sems):
        my, first, _, _, _, own = self._plan(src_refs, out_refs, sems)
        if self.cast:
            for i in range(self.n):
                out_refs[i][my] = src_refs[i][...].astype(out_refs[i].dtype)
        for cp in own + first:
            cp.start()

    def forward(self, src_refs, out_refs, sems):
        _, _, passed, arrive_ici, _, _ = self._plan(src_refs, out_refs, sems)
        for cp in arrive_ici:
            cp.wait_recv()
        for cp in passed:
            cp.start()

    def finish(self, src_refs, out_refs, sems):
        _, first, passed, _, arrive_d2d, own = self._plan(src_refs, out_refs, sems)
        for cp in own:
            cp.wait()
        for cp in arrive_d2d:
            cp.wait_recv()
        for cp in first + passed:
            cp.wait_send()


class _HostedGather:
    def __init__(self, shards):
        self.shapes = [(N_DEV,) + a.shape for a in shards]
        self.n = len(shards)
        self.ag = _AllGather(self.n, cast=True)

    def out_shape(self):
        return [jax.ShapeDtypeStruct(s, BF16) for s in self.shapes]

    def scratch(self):
        return [pltpu.VMEM(s, BF16) for s in self.shapes] + self.ag.scratch() + [pltpu.SemaphoreType.DMA((self.n,))]

    def _split(self, scratch):
        return scratch[:self.n], scratch[self.n:-1], scratch[-1]

    def start(self, src_refs, scratch):
        land, sems, _ = self._split(scratch)
        self.ag.start(src_refs, land, sems)

    def forward(self, src_refs, scratch):
        land, sems, _ = self._split(scratch)
        self.ag.forward(src_refs, land, sems)

    def finish(self, src_refs, scratch, out_refs):
        land, sems, out_sems = self._split(scratch)
        self.ag.finish(src_refs, land, sems)
        outs = [pltpu.make_async_copy(land[n], out_refs[n], out_sems.at[n]) for n in range(self.n)]
        for cp in outs:
            cp.start()
        for cp in outs:
            cp.wait()


def _all_gather(shards, out_dtypes, name):
    n = len(shards)
    ag = _AllGather(n, cast=True)

    def body(*refs):
        in_refs, out_refs, sems = refs[:n], refs[n:2 * n], refs[2 * n:]
        ag.start(in_refs, out_refs, sems)
        ag.forward(in_refs, out_refs, sems)
        ag.finish(in_refs, out_refs, sems)

    return pl.pallas_call(
        body, name=name,
        out_shape=[jax.ShapeDtypeStruct((N_DEV,) + s.shape, dt) for s, dt in zip(shards, out_dtypes)],
        in_specs=[_vmem()] * n, out_specs=[_vmem()] * n,
        scratch_shapes=ag.scratch(),
        compiler_params=_params(vmem=VMEM_LIMIT_V7X),
    )(*shards)


def _row_chunks(rows):
    chunk = 64 if rows % 64 == 0 else rows
    return chunk, rows // chunk


class _ReduceScatter:
    def __init__(self, shapes):
        self.shapes = shapes
        self.n = len(shapes)

    def scratch(self, dtype):
        return ([pltpu.VMEM(s, dtype) for s in self.shapes]
                + [pltpu.SemaphoreType.DMA((7 * self.n,)), pltpu.SemaphoreType.DMA((7 * self.n,)),
                   pltpu.SemaphoreType.DMA((self.n,))])

    def _copies(self, in_refs, land_refs, send_sems, recv_sems, own_sems):
        x, y, c = lax.axis_index("x"), lax.axis_index("y"), lax.axis_index("c")
        remote, own = [], []
        for i in range(self.n):
            for m in range(1, N_DEV):
                px = 1 - x if m & 4 else x
                py = 1 - y if m & 2 else y
                pc = 1 - c if m & 1 else c
                remote.append(pltpu.make_async_remote_copy(
                    src_ref=in_refs[i].at[4 * px + 2 * py + pc], dst_ref=land_refs[i].at[m - 1],
                    send_sem=send_sems.at[7 * i + m - 1], recv_sem=recv_sems.at[7 * i + m - 1],
                    device_id=(px, py, pc), device_id_type=MESH))
            own.append(pltpu.make_async_copy(in_refs[i].at[4 * x + 2 * y + c], land_refs[i].at[N_DEV - 1],
                                             own_sems.at[i]))
        return remote, own

    def start(self, in_refs, scratch):
        remote, own = self._copies(in_refs, scratch[:self.n], *scratch[self.n:])
        for cp in remote + own:
            cp.start()

    def finish(self, in_refs, scratch, out_refs):
        land_refs = scratch[:self.n]
        remote, own = self._copies(in_refs, land_refs, *scratch[self.n:])
        for cp in own:
            cp.wait()
        for cp in remote:
            cp.wait_recv()
        for i in range(self.n):
            chunk, steps = _row_chunks(self.shapes[i][1])

            def step(s, carry, i=i, chunk=chunk):
                r = pl.ds(pl.multiple_of(s * chunk, chunk), chunk)
                acc = land_refs[i][N_DEV - 1, r, :].astype(F32)
                for m in range(1, N_DEV):
                    acc = acc + land_refs[i][m - 1, r, :].astype(F32)
                out_refs[i][r, :] = acc
                return carry

            lax.fori_loop(0, steps, step, 0)
        for cp in remote:
            cp.wait_send()


def _reduce_scatter(parts, name):
    n = len(parts)
    rs = _ReduceScatter([p.shape for p in parts])

    def body(*refs):
        in_refs, out_refs, scratch = refs[:n], refs[n:2 * n], refs[2 * n:]
        rs.start(in_refs, scratch)
        rs.finish(in_refs, scratch, out_refs)

    return pl.pallas_call(
        body, name=name,
        out_shape=[jax.ShapeDtypeStruct(p.shape[1:], F32) for p in parts],
        in_specs=[_vmem()] * n, out_specs=[_vmem()] * n,
        scratch_shapes=rs.scratch(parts[0].dtype),
        compiler_params=_params(vmem=VMEM_LIMIT_V7X),
    )(*parts)


def _adamw_math(w, g, m, v):
    m = ADAM_B1 * m + (1.0 - ADAM_B1) * g
    v = ADAM_B2 * v + (1.0 - ADAM_B2) * (g * g)
    m_hat = m / (1.0 - ADAM_B1 ** ADAM_STEP)
    v_hat = v / (1.0 - ADAM_B2 ** ADAM_STEP)
    delta = -ADAM_LR * (m_hat / (jnp.sqrt(v_hat) + ADAM_EPS) + ADAM_WD * w)
    return delta, m, v


def _adamw_shards(gs, ws, ms, vs):
    n = len(gs)

    def body(*refs):
        g_refs, w_refs, m_refs, v_refs = (refs[k * n:(k + 1) * n] for k in range(4))
        d_out, m_out, v_out = (refs[(4 + k) * n:(5 + k) * n] for k in range(3))
        for i in range(n):
            chunk, steps = _row_chunks(gs[i].shape[0])

            def step(s, carry, i=i, chunk=chunk):
                r = pl.ds(pl.multiple_of(s * chunk, chunk), chunk)
                d, m, v = _adamw_math(w_refs[i][r, :], g_refs[i][r, :], m_refs[i][r, :], v_refs[i][r, :])
                d_out[i][r, :] = d
                m_out[i][r, :] = m
                v_out[i][r, :] = v
                return carry

            lax.fori_loop(0, steps, step, 0)

    shapes = [jax.ShapeDtypeStruct(g.shape, F32) for g in gs]
    outs = pl.pallas_call(
        body, name="adamw_shards", out_shape=shapes * 3,
        in_specs=[_vmem()] * (4 * n), out_specs=[_vmem()] * (3 * n),
        compiler_params=_params(vmem=VMEM_LIMIT_V7X),
    )(*gs, *ws, *ms, *vs)
    return outs[:n], outs[n:2 * n], outs[2 * n:]


def _small_sum(gathered):
    n = len(gathered)

    def body(*refs):
        ga_refs, out_refs = refs[:n], refs[n:]
        for i in range(n):
            def total(idx, i=i):
                g = ga_refs[i][(0,) + idx].astype(F32)
                for j in range(1, N_DEV):
                    g = g + ga_refs[i][(j,) + idx].astype(F32)
                out_refs[i][idx] = g

            if len(gathered[i].shape) == 4:
                def step(s, carry, total=total):
                    total((s,))
                    return carry

                lax.fori_loop(0, gathered[i].shape[1], step, 0)
            else:
                total((Ellipsis,))

    return pl.pallas_call(
        body, name="small_sum", out_shape=[jax.ShapeDtypeStruct(g.shape[1:], F32) for g in gathered],
        in_specs=[_vmem()] * n, out_specs=[_vmem()] * n,
        compiler_params=_params(vmem=VMEM_LIMIT_V7X),
    )(*gathered)


def _adamw_small(gs, ws, ms, vs):
    n = len(gs)

    def body(*refs):
        g_refs, w_refs, m_refs, v_refs = (refs[k * n:(k + 1) * n] for k in range(4))
        d_out, m_out, v_out = (refs[(4 + k) * n:(5 + k) * n] for k in range(3))
        for i in range(n):
            def update(idx, i=i):
                d, mm, vv = _adamw_math(w_refs[i][idx], g_refs[i][idx], m_refs[i][idx], v_refs[i][idx])
                d_out[i][idx] = d
                m_out[i][idx] = mm
                v_out[i][idx] = vv

            if len(gs[i].shape) == 3:
                def step(s, carry, update=update):
                    update(s)
                    return carry

                lax.fori_loop(0, gs[i].shape[0], step, 0)
            else:
                update(Ellipsis)

    shapes = [jax.ShapeDtypeStruct(g.shape, F32) for g in gs]
    outs = pl.pallas_call(
        body, name="adamw_small", out_shape=shapes * 3,
        in_specs=[_vmem()] * (4 * n), out_specs=[_vmem()] * (3 * n),
        compiler_params=_params(vmem=VMEM_LIMIT_V7X),
    )(*gs, *ws, *ms, *vs)
    return outs[:n], outs[n:2 * n], outs[2 * n:]


def _zoh(lr, li, logdt, btr, bti):
    dt = jnp.exp(logdt)
    mag = jnp.exp(lr * dt)
    th = li * dt
    ar = mag * jnp.cos(th)
    ai = mag * jnp.sin(th)
    den = lr * lr + li * li
    nr = ar - 1.0
    cr = (nr * lr + ai * li) / den
    ci = (ai * lr - nr * li) / den
    return ar, ai, cr * btr - ci * bti, cr * bti + ci * btr


BD_GROUPS = 8
BD_ROWS = BD_GROUPS * SSM_GROUP
BD_COLS = BD_GROUPS * SSM_STATE
N_BD = SSM_GROUPS // BD_GROUPS


def _bd_mask():
    r = lax.broadcasted_iota(jnp.int32, (BD_ROWS, BD_COLS), 0) // SSM_GROUP
    c = lax.broadcasted_iota(jnp.int32, (BD_ROWS, BD_COLS), 1) // SSM_STATE
    return r == c


def _blockdiag_store(out_ref, t):
    mask = _bd_mask()
    for j in range(N_BD):
        rows = t[j * BD_GROUPS:(j + 1) * BD_GROUPS].reshape(BD_ROWS, SSM_STATE)
        out_ref[j] = jnp.where(mask, jnp.tile(rows, (1, BD_GROUPS)), 0.0).astype(out_ref.dtype)


def _blockdiag_load(m_ref, fold):
    mask = _bd_mask()
    parts = [_hdot(jnp.where(mask, m_ref[j], 0.0), fold).reshape(BD_GROUPS, SSM_GROUP, SSM_STATE)
             for j in range(N_BD)]
    return jnp.concatenate(parts, axis=0)


def _zoh_fwd(lr, li, logdt, btr, bti, c_re, c_im):
    def body(lr_ref, li_ref, dt_ref, br_ref, bi_ref, cr_ref, ci_ref, ar_ref, ai_ref, bbr_ref, bbi_ref, ccr_ref,
             cci_ref):
        ar, ai, bbr, bbi = _zoh(lr_ref[...], li_ref[...], dt_ref[...], br_ref[...], bi_ref[...])
        ar_ref[...] = ar
        ai_ref[...] = ai
        _blockdiag_store(bbr_ref, bbr)
        _blockdiag_store(bbi_ref, bbi)
        _blockdiag_store(ccr_ref, cr_ref[...])
        _blockdiag_store(cci_ref, ci_ref[...])

    s = jax.ShapeDtypeStruct
    bd = s((N_BD, BD_ROWS, BD_COLS), BF16)
    return pl.pallas_call(
        body, name="zoh_fwd", out_shape=[s(lr.shape, F32), s(lr.shape, F32), bd, bd, bd, bd],
        in_specs=[_vmem()] * 7, out_specs=[_vmem()] * 6,
    )(lr, li, logdt, btr, bti, c_re, c_im)


def _zoh_bwd(lr, li, logdt, btr, bti, dar, dai, dbb_re, dbb_im, dcc_re, dcc_im, fold):
    def body(lr_ref, li_ref, dt_ref, br_ref, bi_ref, dar_ref, dai_ref, dbbr_ref, dbbi_ref, dccr_ref, dcci_ref,
             fold_ref, glr_ref, gli_ref, gdt_ref, gbr_ref, gbi_ref, gcr_ref, gci_ref):
        fold_m = fold_ref[...]
        _, vjp = jax.vjp(_zoh, lr_ref[...], li_ref[...], dt_ref[...], br_ref[...], bi_ref[...])
        glr, gli, gdt, gbr, gbi = vjp((dar_ref[...], dai_ref[...], _blockdiag_load(dbbr_ref, fold_m),
                                       _blockdiag_load(dbbi_ref, fold_m)))
        glr_ref[...] = glr
        gli_ref[...] = gli
        gdt_ref[...] = gdt
        gbr_ref[...] = gbr.astype(BF16)
        gbi_ref[...] = gbi.astype(BF16)
        gcr_ref[...] = _blockdiag_load(dccr_ref, fold_m).astype(BF16)
        gci_ref[...] = _blockdiag_load(dcci_ref, fold_m).astype(BF16)

    s = jax.ShapeDtypeStruct
    return pl.pallas_call(
        body, name="zoh_bwd",
        out_shape=[s(lr.shape, F32), s(lr.shape, F32), s(logdt.shape, F32)] + [s(btr.shape, BF16)] * 4,
        in_specs=[_vmem()] * 12, out_specs=[_vmem()] * 7,
    )(lr, li, logdt, btr, bti, dar, dai, dbb_re, dbb_im, dcc_re, dcc_im, fold)


def _head_ones():
    r = np.arange(ATTN_W) // HEAD_DIM
    return jnp.asarray(r[:, None] == r[None, :], dtype=BF16)


def _head_fold():
    return jnp.asarray(np.tile(np.eye(HEAD_DIM), (ATTN_W // HEAD_DIM, 1)), dtype=BF16)


def _in_proj(x2, g_mix, w_in_sh):
    t_tok = x2.shape[0]
    tm = min(1024, t_tok)
    nt = t_tok // tm
    ag_w = _AllGather(1, cast=True)
    n_sem = len(ag_w.scratch())

    def owner(i):
        x, y, c = lax.axis_index("x"), lax.axis_index("y"), lax.axis_index("c")
        rel = i // 2
        px = jnp.where((rel == 1) | (rel == 3), 1 - x, x)
        py = jnp.where((rel == 2) | (rel == 3), 1 - y, y)
        pc = jnp.where(i % 2 == 1, 1 - c, c)
        return 4 * px + 2 * py + pc

    def body(*refs):
        x_ref, g_ref, w_ref, z_ref, xn_ref, wg_ref, xn_scr, w_land = refs[:8]
        sems_w, out_sem = refs[8:8 + n_sem], refs[8 + n_sem]
        i, t = pl.program_id(0), pl.program_id(1)
        _, first, passed, arrive_ici, arrive_d2d, _ = ag_w._plan([w_ref], [w_land], sems_w)

        @pl.when((i == 0) & (t == 0))
        def _():
            ag_w.start([w_ref], [w_land], sems_w)

        @pl.when((i == 1) & (t == 0))
        def _():
            arrive_d2d[0].wait_recv()

        for n in range(3):
            @pl.when((i == 2 + 2 * n) & (t == 0))
            def _(n=n):
                arrive_ici[n].wait_recv()
                passed[n].start()

            @pl.when((i == 3 + 2 * n) & (t == 0))
            def _(n=n):
                arrive_d2d[1 + n].wait_recv()

        @pl.when(i == 0)
        def _():
            x = x_ref[...]
            r = lax.rsqrt(jnp.mean(x * x, axis=-1, keepdims=True) + EPS)
            xn = (x * r * g_ref[...]).astype(BF16)
            xn_ref[...] = xn
            xn_scr[t] = xn

        z_ref[...] = _dot(xn_scr[t], w_land[owner(i)])

        @pl.when((i == N_DEV - 1) & (t == nt - 1))
        def _():
            for cp in first + passed:
                cp.wait_send()
            out = pltpu.make_async_copy(w_land, wg_ref, out_sem)
            out.start()
            out.wait()

    s = jax.ShapeDtypeStruct
    xmap = lambda i, t: (jnp.where(i == 0, t, nt - 1), 0)
    gathered = s((N_DEV,) + w_in_sh.shape, BF16)
    return pl.pallas_call(
        body, name="in_proj", grid=(N_DEV, nt),
        out_shape=[s((t_tok, IN_W), F32), s((t_tok, D_MODEL), BF16), gathered],
        in_specs=[pl.BlockSpec((tm, D_MODEL), xmap), _full(g_mix.shape), _full(w_in_sh.shape)],
        out_specs=[pl.BlockSpec((tm, COL_W), lambda i, t: (t, owner(i))), pl.BlockSpec((tm, D_MODEL), xmap),
                   pl.BlockSpec(memory_space=pl.ANY)],
        scratch_shapes=[pltpu.VMEM((nt, tm, D_MODEL), BF16), pltpu.VMEM(gathered.shape, BF16)] + ag_w.scratch()
        + [pltpu.SemaphoreType.DMA],
        compiler_params=_params(2, VMEM_LIMIT_V7X),
    )(x2, g_mix, w_in_sh)


TQ = 128
NEG = -1e30


def _head_col(t, lm):
    return jnp.max(jnp.where(lm, t, NEG), axis=-1, keepdims=True)


def _head_masks():
    lane = lax.broadcasted_iota(jnp.int32, (1, 1, LANES), 2)
    return [(lane // HEAD_DIM) == h for h in range(LANES // HEAD_DIM)]


def _stack_heads(t3, lms):
    return jnp.concatenate([jnp.where(lm, t3, jnp.zeros_like(t3)) for lm in lms], axis=1)


def _unstack_heads(t2, lms, tq):
    out = t2[:, :tq]
    for h in range(1, len(lms)):
        out = jnp.where(lms[h], t2[:, h * tq:(h + 1) * tq], out)
    return out


def _gather_classes(ref, dil, nt, tq, dtype):
    length = nt * tq
    if dil == 1:
        return ref[...].astype(dtype).reshape(nt, tq, LANES)
    parts = [ref[pl.ds(r, length, stride=dil), :].astype(dtype).reshape(nt, tq, LANES) for r in range(dil)]
    return jnp.concatenate(parts, axis=0)


def _scatter_classes(ref, val, dil, nt, tq, add):
    length = nt * tq
    for r in range(dil):
        rows = pl.ds(r, length, stride=dil) if dil > 1 else slice(None)
        part = val[r * nt:(r + 1) * nt].reshape(length, LANES)
        ref[rows, :] = ref[rows, :] + part if add else part


def _with_prev_tile(t3, dil, nt):
    parts = []
    for r in range(dil):
        t = t3[r * nt:(r + 1) * nt]
        parts.append(jnp.concatenate([t[:1], t[:-1]], axis=0))
    prev = parts[0] if dil == 1 else jnp.concatenate(parts, axis=0)
    return jnp.concatenate([prev, t3], axis=1)


def _band_valid(dil, nt, tq):
    if nt == 1:
        shape = (dil, tq, tq)
        return lax.broadcasted_iota(jnp.int32, shape, 1) >= lax.broadcasted_iota(jnp.int32, shape, 2)
    shape = (dil * nt, tq, 2 * tq)
    b = lax.broadcasted_iota(jnp.int32, shape, 0)
    c = lax.broadcasted_iota(jnp.int32, shape, 2)
    d = tq + lax.broadcasted_iota(jnp.int32, shape, 1) - c
    return (d >= 0) & (d <= tq) & (((b & (nt - 1)) != 0) | (c >= tq))


def _window_tiling(seq, window, dil):
    length = seq // dil
    tq = min(TQ, length)
    nt = length // tq
    assert length % tq == 0 and nt & (nt - 1) == 0 and (nt == 1 or window == tq * dil)
    return nt, tq


def _bqk(a, b):
    return jnp.einsum("bqd,bkd->bqk", a, b, preferred_element_type=F32)


def _bqd(a, b):
    return jnp.einsum("bqk,bkd->bqd", a, b, preferred_element_type=F32)


def _bkd(a, b):
    return jnp.einsum("bqk,bqd->bkd", a, b, preferred_element_type=F32)


def _qk_hat(q_ref, k_ref, gq_ref, gk_ref):
    lane = lax.broadcasted_iota(jnp.int32, (1, LANES), 1)

    def norm(raw, gain, scale):
        sq = raw * raw
        r = jnp.zeros_like(raw)
        for h in range(LANES // HEAD_DIM):
            lm = (lane // HEAD_DIM) == h
            ms = jnp.sum(jnp.where(lm, sq, 0.0), axis=-1, keepdims=True) * (1.0 / HEAD_DIM)
            r = jnp.where(lm, lax.rsqrt(ms + EPS), r)
        return raw * r * gain * scale

    return norm(q_ref[...], gq_ref[...], HEAD_DIM ** -0.5), norm(k_ref[...], gk_ref[...], 1.0)


def _zblock(seq, group):
    return pl.BlockSpec((seq, LANES), lambda b, hp: (b, group * (ATTN_W // LANES) + hp))


def _attn_fwd(z, gq2, gk2, nb, seq, late_sh):
    t_tok = nb * seq
    n_win = len(DILATED)
    host = _HostedGather(late_sh)
    n_late = host.n
    n_steps = (nb, ATTN_W // LANES)

    def body(*refs):
        (q_ref, k_ref, v_ref, ga_ref, gq_ref, gk_ref), refs = refs[:6], refs[6:]
        late_refs, refs = refs[:n_late], refs[n_late:]
        (o_ref, l_ref, ag_ref, qh_ref, kh_ref), refs = refs[:5], refs[5:]
        lateg_refs, refs = refs[:n_late], refs[n_late:]
        (qf, kf, oc, lc), host_scratch = refs[:4], refs[4:]
        step = pl.program_id(0) * n_steps[1] + pl.program_id(1)
        total = n_steps[0] * n_steps[1]

        @pl.when(step == 0)
        def _():
            host.start(late_refs, host_scratch)

        @pl.when(step == total // 2)
        def _():
            host.forward(late_refs, host_scratch)

        q_hat, k_hat = _qk_hat(q_ref, k_ref, gq_ref, gk_ref)
        qh_ref[...] = q_hat.astype(BF16)
        kh_ref[...] = k_hat.astype(BF16)
        qf[...] = q_hat
        kf[...] = k_hat
        lms = _head_masks()
        for w, (window, dil) in enumerate(DILATED):
            nt, tq = _window_tiling(seq, window, dil)
            q3 = _gather_classes(qf, dil, nt, tq, BF16)
            k3 = _gather_classes(kf, dil, nt, tq, BF16)
            v3 = _gather_classes(v_ref, dil, nt, tq, BF16)
            if nt > 1:
                k3, v3 = _with_prev_tile(k3, dil, nt), _with_prev_tile(v3, dil, nt)
            valid = _band_valid(dil, nt, tq)
            valid = jnp.concatenate([valid] * len(lms), axis=1)
            s = _bqk(_stack_heads(q3, lms), k3)
            m = jnp.max(jnp.where(valid, s, NEG), axis=-1, keepdims=True)
            p = jnp.where(valid, jnp.exp(s - m), 0.0)
            den = jnp.sum(p, axis=-1, keepdims=True)
            o = _unstack_heads(_bqd(p.astype(BF16), v3) / den, lms, tq)
            lse = _unstack_heads(jnp.broadcast_to(m + jnp.log(den), s.shape[:2] + (LANES,)), lms, tq)
            _scatter_classes(oc.at[w], o, dil, nt, tq, add=False)
            _scatter_classes(lc.at[w], lse, dil, nt, tq, add=False)
        mx = lc[0]
        for w in range(1, n_win):
            mx = jnp.maximum(mx, lc[w])
        tot = jnp.zeros_like(mx)
        o = jnp.zeros_like(mx)
        for w in range(n_win):
            e = jnp.exp(lc[w] - mx)
            tot = tot + e
            o = o + e * oc[w]
        o = o / tot
        o_ref[...] = o
        l_ref[...] = mx + jnp.log(tot)
        ga = ga_ref[...]
        ag_ref[...] = (o * ga * _sig(ga)).astype(BF16)

        @pl.when(step == total - 1)
        def _():
            host.finish(late_refs, host_scratch, lateg_refs)

    blk = pl.BlockSpec((seq, LANES), lambda b, hp: (b, hp))
    s = jax.ShapeDtypeStruct
    outs = pl.pallas_call(
        body, name="attn_fwd", grid=n_steps,
        out_shape=[s((t_tok, ATTN_W), F32), s((t_tok, ATTN_W), F32)] + [s((t_tok, ATTN_W), BF16)] * 3
        + host.out_shape(),
        in_specs=[_zblock(seq, 0), _zblock(seq, 1), _zblock(seq, 2), _zblock(seq, 3), _full(gq2.shape),
                  _full(gk2.shape)] + [_full(a.shape) for a in late_sh],
        out_specs=[blk] * 5 + [pl.BlockSpec(memory_space=pl.ANY)] * n_late,
        scratch_shapes=[pltpu.VMEM((seq, LANES), F32)] * 2 + [pltpu.VMEM((n_win, seq, LANES), F32)] * 2
        + host.scratch(),
        compiler_params=_params(2, VMEM_LIMIT_V7X),
    )(z, z, z, z, gq2, gk2, *late_sh)
    return outs[:5], outs[5:]


SCAN_COLS = 512


def _to_segments(dst_ref, val):
    seg = val.shape[0] // SUBLANES
    for n in range(dst_ref.shape[0]):
        for s in range(SUBLANES):
            dst_ref[n, pl.ds(s, seg, stride=SUBLANES), :] = val[s * seg:(s + 1) * seg, n * LANES:(n + 1) * LANES]


def _from_segments(src_ref):
    seg = src_ref.shape[1] // SUBLANES
    return jnp.concatenate(
        [jnp.concatenate([src_ref[n, pl.ds(s, seg, stride=SUBLANES), :] for s in range(SUBLANES)], axis=0)
         for n in range(src_ref.shape[0])], axis=1)


def _scan_chunk(re_ref, im_ref, a_re_ref, a_im_ref, carry_re, carry_im, rows, reverse, visit=None):
    seg = rows // SUBLANES
    assert seg & (seg - 1) == 0
    rowi = lax.broadcasted_iota(jnp.int32, (SUBLANES, SCAN_COLS), 0)
    edge = (SUBLANES - 1) if reverse else 0
    last = 0 if reverse else SUBLANES - 1
    at_edge = rowi == edge

    def cmul(ar, ai, br, bi):
        return ar * br - ai * bi, ar * bi + ai * br

    for c0 in range(0, N_STATE, SCAN_COLS):
        cols = slice(c0, c0 + SCAN_COLS)
        a1r = jnp.broadcast_to(a_re_ref[:, cols], (SUBLANES, SCAN_COLS))
        a1i = jnp.broadcast_to(a_im_ref[:, cols], (SUBLANES, SCAN_COLS))
        if reverse:
            a1i = -a1i

        def block_of(i):
            j = (seg - 1 - i) if reverse else i
            return j, pl.ds(pl.multiple_of(j * SUBLANES, SUBLANES), SUBLANES)

        def local(i, carry, cols=cols, a1r=a1r, a1i=a1i):
            xr, xi = carry
            _, blk = block_of(i)
            nr, ni = cmul(a1r, a1i, xr, xi)
            xr, xi = nr + re_ref[blk, cols], ni + im_ref[blk, cols]
            re_ref[blk, cols] = xr
            im_ref[blk, cols] = xi
            return xr, xi

        zero = jnp.zeros((SUBLANES, SCAN_COLS), F32)
        er, ei = lax.fori_loop(0, seg, local, (zero, zero))

        pr, pi = a1r, a1i
        for _ in range(seg.bit_length() - 1):
            pr, pi = cmul(pr, pi, pr, pi)
        cr, ci = carry_re[:, cols], carry_im[:, cols]
        inr, ini = cmul(pr, pi, cr, ci)
        er = er + jnp.where(at_edge, inr, 0.0)
        ei = ei + jnp.where(at_edge, ini, 0.0)
        for sft in (1, 2, 4):
            shift, keep = (SUBLANES - sft, rowi < SUBLANES - sft) if reverse else (sft, rowi >= sft)
            rs = jnp.where(keep, pltpu.roll(er, shift, 0), 0.0)
            ims = jnp.where(keep, pltpu.roll(ei, shift, 0), 0.0)
            dr, di = cmul(pr, pi, rs, ims)
            er, ei = er + dr, ei + di
            pr, pi = cmul(pr, pi, pr, pi)
        carry_re[:, cols] = jnp.broadcast_to(er[last:last + 1, :], (SUBLANES, SCAN_COLS))
        carry_im[:, cols] = jnp.broadcast_to(ei[last:last + 1, :], (SUBLANES, SCAN_COLS))
        one = (SUBLANES - 1) if reverse else 1
        kr = jnp.where(at_edge, cr, pltpu.roll(er, one, 0))
        ki = jnp.where(at_edge, ci, pltpu.roll(ei, one, 0))

        def fix(i, carry, cols=cols, a1r=a1r, a1i=a1i):
            kr, ki, acc = carry
            j, blk = block_of(i)
            kr, ki = cmul(a1r, a1i, kr, ki)
            xr, xi = re_ref[blk, cols] + kr, im_ref[blk, cols] + ki
            re_ref[blk, cols] = xr
            im_ref[blk, cols] = xi
            if visit is not None:
                acc = visit(cols, j, xr, xi, acc)
            return kr, ki, acc

        _, _, acc = lax.fori_loop(0, seg, fix, (kr, ki, (zero, zero)))
        if visit is not None:
            visit(cols, None, None, None, acc)


SSM_CHUNK = 512


def _ssm_fwd(z, a_re, a_im, bb_re, bb_im, cc_re, cc_im, d_skip, w_glu, b_glu, nb, seq):
    t_tok = nb * seq
    tc = min(SSM_CHUNK, seq)
    nch = seq // tc
    grp = N_STATE // 4

    def body(u_ref, gs_ref, ar_ref, ai_ref, bbr_ref, bbi_ref, ccr_ref, cci_ref, d_ref, wg_ref, bg_ref,
             xr_ref, xi_ref, y_ref, sg_ref, car_re, car_im, seg_u, seg_y):
        @pl.when(pl.program_id(1) == 0)
        def _():
            car_re[...] = jnp.zeros_like(car_re)
            car_im[...] = jnp.zeros_like(car_im)

        u = u_ref[...]
        _to_segments(seg_u, u)
        for j in range(4):
            uj = seg_u[j].astype(BF16)
            xr_ref[:, j * grp:(j + 1) * grp] = _dot(uj, bbr_ref[j])
            xi_ref[:, j * grp:(j + 1) * grp] = _dot(uj, bbi_ref[j])
        _scan_chunk(xr_ref, xi_ref, ar_ref, ai_ref, car_re, car_im, tc, reverse=False)
        for j in range(4):
            xr = xr_ref[:, j * grp:(j + 1) * grp].astype(BF16)
            xi = xi_ref[:, j * grp:(j + 1) * grp].astype(BF16)
            seg_y[j] = _dot_nt(xr, ccr_ref[j]) - _dot_nt(xi, cci_ref[j])
        y = _from_segments(seg_y) + d_ref[...] * u
        y_ref[...] = y
        yg, _ = _gelu_and_grad(y)
        gl = _dot(yg.astype(BF16), wg_ref[...]) + bg_ref[...]
        gs = gs_ref[...]
        sg_ref[...] = (yg * _sig(gl) * gs * _sig(gs)).astype(BF16)

    umap = lambda b, ch: (b * nch + ch, 4)
    gmap = lambda b, ch: (b * nch + ch, 5)
    row = lambda b, ch: (b * nch + ch, 0)
    s = jax.ShapeDtypeStruct
    consts = [a_re, a_im, bb_re, bb_im, cc_re, cc_im, d_skip, w_glu, b_glu]
    return pl.pallas_call(
        body, name="ssm_fwd", grid=(nb, nch),
        out_shape=[s((t_tok, N_STATE), F32), s((t_tok, N_STATE), F32), s((t_tok, SSM_W), F32),
                   s((t_tok, SSM_W), BF16)],
        in_specs=[pl.BlockSpec((tc, SSM_W), umap), pl.BlockSpec((tc, SSM_W), gmap)] + [_full(c.shape) for c in consts],
        out_specs=[pl.BlockSpec((tc, N_STATE), row), pl.BlockSpec((tc, N_STATE), row),
                   pl.BlockSpec((tc, SSM_W), row), pl.BlockSpec((tc, SSM_W), row)],
        scratch_shapes=[pltpu.VMEM((SUBLANES, N_STATE), F32), pltpu.VMEM((SUBLANES, N_STATE), F32),
                        pltpu.VMEM((4, tc, LANES), F32), pltpu.VMEM((4, tc, LANES), F32)],
        compiler_params=_params(2, VMEM_LIMIT_V7X),
    )(z, z, *consts)


def _tail(x2, tg2, ag, sg, p2, w_out, w_g, w_p, g_ple):
    t_tok = x2.shape[0]
    tm = min(512, t_tok)
    nt = t_tok // tm
    half = ATTN_W

    def body(x_ref, tg_ref, ag_ref, sg_ref, p_ref, wo_ref, wg_ref, wp_ref, gp_ref,
             dmix_ref, dh1_ref, loss_ref, dgp_ref, dwo_ref, dwg_ref, dwp_ref, acc_o, acc_g, acc_p):
        i = pl.program_id(0)

        @pl.when(i == 0)
        def _():
            loss_ref[...] = jnp.zeros_like(loss_ref)
            dgp_ref[...] = jnp.zeros_like(dgp_ref)
            acc_o[...] = jnp.zeros_like(acc_o)
            acc_g[...] = jnp.zeros_like(acc_g)
            acc_p[...] = jnp.zeros_like(acc_p)

        ag_t, sg_t = ag_ref[...], sg_ref[...]
        h1 = x_ref[...] + _dot(ag_t, wo_ref[0:half, :]) + _dot(sg_t, wo_ref[half:2 * half, :])
        r2 = lax.rsqrt(jnp.mean(h1 * h1, axis=-1, keepdims=True) + EPS)
        hnorm = h1 * r2
        gp = gp_ref[...]
        hn = (hnorm * gp).astype(BF16)
        gate = _sig(_dot(hn, wg_ref[...]))
        pb = p_ref[...].astype(BF16)
        pp = jnp.concatenate([_dot(pb, wp_ref[j]) for j in range(N_DEV)], axis=-1)
        h2 = h1 + gate * pp
        err = h2 - tg_ref[...]
        loss_ref[...] += 0.5 * jnp.sum(err * err) * (1.0 / D_MODEL)
        dh2 = err * (1.0 / D_MODEL)
        dpp = (dh2 * gate).astype(BF16)
        dgpre = (dh2 * pp * gate * (1.0 - gate)).astype(BF16)
        acc_p[...] += _dot_tn(pb, dpp)
        acc_g[...] += _dot_tn(hn, dgpre)
        dhn = _dot_nt(dgpre, wg_ref[...])
        dgp_ref[...] += jnp.sum(dhn * hnorm, axis=0, keepdims=True)
        a = dhn * gp
        dh1 = dh2 + r2 * (a - hnorm * jnp.mean(a * hnorm, axis=-1, keepdims=True))
        dh1_ref[...] = dh1
        dh1b = dh1.astype(BF16)
        acc_o[0:half, :] += _dot_tn(ag_t, dh1b)
        acc_o[half:2 * half, :] += _dot_tn(sg_t, dh1b)
        dmix_ref[...] = _dot_nt(dh1b, wo_ref[...])

        @pl.when(i == nt - 1)
        def _():
            dwo_ref[...] = acc_o[...].astype(BF16)
            dwg_ref[...] = acc_g[...].astype(BF16)
            for j in range(N_DEV):
                dwp_ref[j] = acc_p[:, j * LANES:(j + 1) * LANES].astype(BF16)

    row = lambda i: (i, 0)
    s = jax.ShapeDtypeStruct
    return pl.pallas_call(
        body, name="tail_fwd_bwd", grid=(nt,),
        out_shape=[s((t_tok, D_MODEL), F32), s((t_tok, D_MODEL), F32), s((SUBLANES, LANES), F32),
                   s((1, D_MODEL), F32), s((D_MODEL, D_MODEL), BF16), s((D_MODEL, D_MODEL), BF16),
                   s((N_DEV, PLE_DIM, LANES), BF16)],
        in_specs=[pl.BlockSpec((tm, D_MODEL), row), pl.BlockSpec((tm, D_MODEL), row),
                  pl.BlockSpec((tm, half), row), pl.BlockSpec((tm, half), row), pl.BlockSpec((tm, PLE_DIM), row),
                  _full(w_out.shape), _full(w_g.shape), _full(w_p.shape), _full(g_ple.shape)],
        out_specs=[pl.BlockSpec((tm, D_MODEL), row), pl.BlockSpec((tm, D_MODEL), row), _full((SUBLANES, LANES)),
                   _full((1, D_MODEL)), _full((D_MODEL, D_MODEL)), _full((D_MODEL, D_MODEL)),
                   _full((N_DEV, PLE_DIM, LANES))],
        scratch_shapes=[pltpu.VMEM((D_MODEL, D_MODEL), F32), pltpu.VMEM((D_MODEL, D_MODEL), F32),
                        pltpu.VMEM((PLE_DIM, D_MODEL), F32)],
        compiler_params=_params(1, VMEM_LIMIT_V7X),
    )(x2, tg2, ag, sg, p2, w_out, w_g, w_p, g_ple)


def _attn_bwd(qh, kh, z, o, lse, dmix, nb, seq, parts):
    t_tok = nb * seq
    n_rs = len(parts)
    rs = _ReduceScatter([p.shape for p in parts])
    n_steps = (nb, ATTN_W // LANES)

    def body(*refs):
        (q_ref, k_ref, v_ref, ga_ref, o_ref, l_ref, da_ref), refs = refs[:7], refs[7:]
        part_refs, refs = refs[:n_rs], refs[n_rs:]
        (dq_ref, dk_ref, dv_ref, dga_ref), refs = refs[:4], refs[4:]
        g_refs, refs = refs[:n_rs], refs[n_rs:]
        (qf, kf, dof, dlf), rs_scratch = refs[:4], refs[4:]
        b, hp = pl.program_id(0), pl.program_id(1)

        @pl.when((b == 0) & (hp == 0))
        def _():
            rs.start(part_refs, rs_scratch)

        ga, o_t, da = ga_ref[...], o_ref[...], da_ref[...]
        sga = _sig(ga)
        d_o = da * ga * sga
        dga_ref[...] = da * o_t * sga * (1.0 + ga * (1.0 - sga))
        lane = lax.broadcasted_iota(jnp.int32, (1, LANES), 1)
        d_oo = d_o * o_t
        delta = jnp.zeros_like(d_oo)
        for h in range(LANES // HEAD_DIM):
            lm2 = (lane // HEAD_DIM) == h
            delta = jnp.where(lm2, jnp.sum(jnp.where(lm2, d_oo, 0.0), axis=-1, keepdims=True), delta)
        qf[...] = q_ref[...].astype(F32)
        kf[...] = k_ref[...].astype(F32)
        dof[...] = d_o
        dlf[...] = jnp.where((lane % HEAD_DIM) < HEAD_DIM // 2, l_ref[...], delta)
        lane3 = lax.broadcasted_iota(jnp.int32, (1, 1, LANES), 2)
        lower = (lane3 % HEAD_DIM) < HEAD_DIM // 2
        dq_ref[...] = jnp.zeros_like(dq_ref)
        dk_ref[...] = jnp.zeros_like(dk_ref)
        dv_ref[...] = jnp.zeros_like(dv_ref)
        lms = _head_masks()
        for window, dil in DILATED:
            nt, tq = _window_tiling(seq, window, dil)
            q3 = _gather_classes(qf, dil, nt, tq, BF16)
            k3 = _gather_classes(kf, dil, nt, tq, BF16)
            v3 = _gather_classes(v_ref, dil, nt, tq, BF16)
            do3 = _gather_classes(dof, dil, nt, tq, BF16)
            st3 = _gather_classes(dlf, dil, nt, tq, F32)
            if nt > 1:
                k3, v3 = _with_prev_tile(k3, dil, nt), _with_prev_tile(v3, dil, nt)
            valid = _band_valid(dil, nt, tq)
            dq = jnp.zeros(q3.shape, F32)
            dk = jnp.zeros(k3.shape, F32)
            dv = jnp.zeros(k3.shape, F32)
            for lm in lms:
                qm = jnp.where(lm, q3, jnp.zeros_like(q3))
                dom = jnp.where(lm, do3, jnp.zeros_like(do3))
                p = jnp.where(valid, jnp.exp(_bqk(qm, k3) - _head_col(st3, lm & lower)), 0.0)
                dv = dv + _bkd(p.astype(BF16), dom)
                ds = (p * (_bqk(dom, v3) - _head_col(st3, lm & ~lower))).astype(BF16)
                dq = dq + jnp.where(lm, _bqd(ds, k3), 0.0)
                dk = dk + _bkd(ds, qm)
            _scatter_classes(dq_ref, dq, dil, nt, tq, add=True)
            for ref, g in ((dk_ref, dk), (dv_ref, dv)):
                if nt > 1:
                    own, prev = g[:, tq:, :], g[:, :tq, :]
                    shifted = []
                    for r in range(dil):
                        t = prev[r * nt:(r + 1) * nt]
                        shifted.append(jnp.concatenate([t[1:], jnp.zeros_like(t[:1])], axis=0))
                    g = own + (shifted[0] if dil == 1 else jnp.concatenate(shifted, axis=0))
                _scatter_classes(ref, g, dil, nt, tq, add=True)

        @pl.when((b == n_steps[0] - 1) & (hp == n_steps[1] - 1))
        def _():
            rs.finish(part_refs, rs_scratch, g_refs)

    blk = pl.BlockSpec((seq, LANES), lambda b, hp: (b, hp))
    s = jax.ShapeDtypeStruct
    outs = pl.pallas_call(
        body, name="attn_bwd", grid=n_steps,
        out_shape=[s((t_tok, ATTN_W), F32)] * 4 + [s(p.shape[1:], F32) for p in parts],
        in_specs=[blk, blk, _zblock(seq, 2), _zblock(seq, 3), blk, blk, blk]
        + [pl.BlockSpec(memory_space=pl.ANY)] * n_rs,
        out_specs=[blk] * 4 + [_full(p.shape[1:]) for p in parts],
        scratch_shapes=[pltpu.VMEM((seq, LANES), F32)] * 4 + rs.scratch(parts[0].dtype),
        compiler_params=_params(2, VMEM_LIMIT_V7X),
    )(qh, kh, z, z, o, lse, dmix, *parts)
    return outs[:4], outs[4:]


def _ssm_bwd(z, dmix, y, x_re, x_im, a_re, a_im, bb_re, bb_im, cc_re, cc_im, d_skip, w_glu, b_glu, nb, seq):
    t_tok = nb * seq
    tc = min(SSM_CHUNK, seq)
    nch = seq // tc
    grp = N_STATE // 4

    def body(u_ref, gs_ref, ds_ref, y_ref, xr_ref, xi_ref, xpr_ref, xpi_ref,
             ar_ref, ai_ref, bbr_ref, bbi_ref, ccr_ref, cci_ref, d_ref, wg_ref, bg_ref,
             du_ref, dgs_ref, dwg_ref, dbg_ref, dd_ref, dar_ref, dai_ref, dbbr_ref, dbbi_ref, dccr_ref, dcci_ref,
             lam_re, lam_im, car_re, car_im, acc_wg, seg_a, seg_b, ent_re, ent_im):
        step = pl.program_id(1)
        first_chunk = step == nch - 1

        @pl.when((pl.program_id(0) == 0) & (step == 0))
        def _():
            acc_wg[...] = jnp.zeros_like(acc_wg)
            for ref in (dbg_ref, dd_ref, dar_ref, dai_ref, dbbr_ref, dbbi_ref, dccr_ref, dcci_ref):
                ref[...] = jnp.zeros_like(ref)

        @pl.when(step == 0)
        def _():
            car_re[...] = jnp.zeros_like(car_re)
            car_im[...] = jnp.zeros_like(car_im)

        u, gs, dssm, y = u_ref[...], gs_ref[...], ds_ref[...], y_ref[...]
        yg, dgelu = _gelu_and_grad(y)
        ygb = yg.astype(BF16)
        sgl = _sig(_dot(ygb, wg_ref[...]) + bg_ref[...])
        sgs = _sig(gs)
        dout = dssm * gs * sgs
        dgs_ref[...] = dssm * yg * sgl * sgs * (1.0 + gs * (1.0 - sgs))
        dgl = dout * yg * sgl * (1.0 - sgl)
        dglb = dgl.astype(BF16)
        dyg = dout * sgl + _dot_nt(dglb, wg_ref[...])
        acc_wg[...] += _dot_tn(ygb, dglb)
        dbg_ref[...] += jnp.sum(dgl, axis=0, keepdims=True)
        dy = dyg * dgelu
        dd_ref[...] += jnp.sum(dy * u, axis=0, keepdims=True)
        _to_segments(seg_a, dy)
        _to_segments(seg_b, u)
        for j in range(4):
            dyj = seg_a[j].astype(BF16)
            sl = slice(j * grp, (j + 1) * grp)
            lam_re[:, sl] = _dot(dyj, ccr_ref[j])
            lam_im[:, sl] = -_dot(dyj, cci_ref[j])
            dccr_ref[j] += _dot_tn(dyj, xr_ref[:, sl].astype(BF16))
            dcci_ref[j] -= _dot_tn(dyj, xi_ref[:, sl].astype(BF16))

        keep_prev = jnp.where(first_chunk, 0.0, 1.0)
        seg = tc // SUBLANES
        last_blk = pl.ds((seg - 1) * SUBLANES, SUBLANES)
        row0 = lax.broadcasted_iota(jnp.int32, (SUBLANES, N_STATE), 0) == 0
        for src, prev, dst in ((xr_ref, xpr_ref, ent_re), (xi_ref, xpi_ref, ent_im)):
            before = jnp.broadcast_to(prev[SUBLANES - 1:SUBLANES, :] * keep_prev, (SUBLANES, N_STATE))
            dst[...] = jnp.where(row0, before, pltpu.roll(src[last_blk, :], 1, 0))

        def visit(cols, j, lr, li, acc):
            if j is None:
                dar_ref[:, cols] += jnp.sum(acc[0], axis=0, keepdims=True)
                dai_ref[:, cols] += jnp.sum(acc[1], axis=0, keepdims=True)
                return None
            blk = pl.ds(pl.multiple_of(jnp.maximum(j - 1, 0) * SUBLANES, SUBLANES), SUBLANES)
            inside = j > 0
            xpr = jnp.where(inside, xr_ref[blk, cols], ent_re[:, cols])
            xpi = jnp.where(inside, xi_ref[blk, cols], ent_im[:, cols])
            return acc[0] + lr * xpr + li * xpi, acc[1] + li * xpr - lr * xpi

        _scan_chunk(lam_re, lam_im, ar_ref, ai_ref, car_re, car_im, tc, reverse=True, visit=visit)

        for j in range(4):
            sl = slice(j * grp, (j + 1) * grp)
            lr = lam_re[:, sl].astype(BF16)
            li = lam_im[:, sl].astype(BF16)
            uj = seg_b[j].astype(BF16)
            seg_a[j] = _dot_nt(lr, bbr_ref[j]) + _dot_nt(li, bbi_ref[j])
            dbbr_ref[j] += _dot_tn(uj, lr)
            dbbi_ref[j] += _dot_tn(uj, li)
        du_ref[...] = _from_segments(seg_a) + dy * d_ref[...]

        @pl.when((pl.program_id(0) == nb - 1) & (step == nch - 1))
        def _():
            dwg_ref[...] = acc_wg[...].astype(BF16)

    rev = lambda b, ch: b * nch + (nch - 1 - ch)
    umap = lambda b, ch: (rev(b, ch), 4)
    gmap = lambda b, ch: (rev(b, ch), 5)
    smap = lambda b, ch: (rev(b, ch), 1)
    row = lambda b, ch: (rev(b, ch), 0)
    prev = lambda b, ch: (jnp.maximum(rev(b, ch) * (tc // SUBLANES) - 1, 0), 0)
    s = jax.ShapeDtypeStruct
    consts = [a_re, a_im, bb_re, bb_im, cc_re, cc_im, d_skip, w_glu, b_glu]
    acc_shapes = [s((1, SSM_W), F32), s((1, SSM_W), F32), s((1, N_STATE), F32), s((1, N_STATE), F32),
                  s(bb_re.shape, F32), s(bb_re.shape, F32), s(cc_re.shape, F32), s(cc_re.shape, F32)]
    return pl.pallas_call(
        body, name="ssm_bwd", grid=(nb, nch),
        out_shape=[s((t_tok, SSM_W), F32), s((t_tok, SSM_W), F32), s((SSM_W, SSM_W), BF16)] + acc_shapes,
        in_specs=[pl.BlockSpec((tc, SSM_W), umap), pl.BlockSpec((tc, SSM_W), gmap), pl.BlockSpec((tc, SSM_W), smap),
                  pl.BlockSpec((tc, SSM_W), row), pl.BlockSpec((tc, N_STATE), row), pl.BlockSpec((tc, N_STATE), row),
                  pl.BlockSpec((SUBLANES, N_STATE), prev), pl.BlockSpec((SUBLANES, N_STATE), prev)]
        + [_full(c.shape) for c in consts],
        out_specs=[pl.BlockSpec((tc, SSM_W), row), pl.BlockSpec((tc, SSM_W), row), _full((SSM_W, SSM_W))]
        + [_full(a.shape) for a in acc_shapes],
        scratch_shapes=[pltpu.VMEM((tc, N_STATE), F32), pltpu.VMEM((tc, N_STATE), F32),
                        pltpu.VMEM((SUBLANES, N_STATE), F32), pltpu.VMEM((SUBLANES, N_STATE), F32),
                        pltpu.VMEM((SSM_W, SSM_W), F32), pltpu.VMEM((4, tc, LANES), F32),
                        pltpu.VMEM((4, tc, LANES), F32),
                        pltpu.VMEM((SUBLANES, N_STATE), F32), pltpu.VMEM((SUBLANES, N_STATE), F32)],
        compiler_params=_params(2, VMEM_LIMIT_V7X),
    )(z, z, dmix, y, x_re, x_im, x_re, x_im, *consts)


def _dz_and_dx(x2, z, dqh, dkh, dvb, dga, du, dgs, dh1, w_in_g, g_mix, gq_t, gk_t, ones_bd, fold):
    t_tok = x2.shape[0]
    tm = min(512, t_tok)
    nt = t_tok // tm
    a_w = ATTN_W

    def head_norm_bwd(raw, d_hat, gain, scale, ones):
        r = lax.rsqrt(_hdot(raw * raw, ones) * (1.0 / HEAD_DIM) + EPS)
        n = raw * r
        a = d_hat * gain * scale
        d_raw = r * (a - n * (_hdot(a * n, ones) * (1.0 / HEAD_DIM)))
        return d_raw, jnp.sum(d_hat * n * scale, axis=0, keepdims=True)

    def body(x_ref, q_ref, k_ref, dq_ref, dk_ref, dv_ref, dga_ref, du_ref, dgs_ref, dh1_ref, w_ref, g_ref,
             gq_ref, gk_ref, ones_ref, fold_ref, dz_ref, gx_ref, dgm_ref, dgq_ref, dgk_ref, acc_q, acc_k):
        i = pl.program_id(0)

        @pl.when(i == 0)
        def _():
            dgm_ref[...] = jnp.zeros_like(dgm_ref)
            acc_q[...] = jnp.zeros_like(acc_q)
            acc_k[...] = jnp.zeros_like(acc_k)

        ones = ones_ref[...]
        dq, sq = head_norm_bwd(q_ref[...], dq_ref[...], gq_ref[...], HEAD_DIM ** -0.5, ones)
        dk, sk = head_norm_bwd(k_ref[...], dk_ref[...], gk_ref[...], 1.0, ones)
        acc_q[...] += jnp.broadcast_to(sq, acc_q.shape)
        acc_k[...] += jnp.broadcast_to(sk, acc_k.shape)
        parts = (dq, dk, dv_ref[...], dga_ref[...], du_ref[...], dgs_ref[...])
        for n, part in enumerate(parts):
            dz_ref[:, n * a_w:(n + 1) * a_w] = part.astype(BF16)
        dxn = jnp.zeros((tm, D_MODEL), F32)
        for j in range(N_DEV):
            dxn = dxn + _dot_nt(dz_ref[:, j * COL_W:(j + 1) * COL_W], w_ref[j])
        x = x_ref[...]
        r1 = lax.rsqrt(jnp.mean(x * x, axis=-1, keepdims=True) + EPS)
        xnorm = x * r1
        dgm_ref[...] += jnp.sum(dxn * xnorm, axis=0, keepdims=True)
        a = dxn * g_ref[...]
        gx_ref[...] = dh1_ref[...] + r1 * (a - xnorm * jnp.mean(a * xnorm, axis=-1, keepdims=True))

        @pl.when(i == nt - 1)
        def _():
            dgq_ref[...] = _hdot(acc_q[...], fold_ref[...])
            dgk_ref[...] = _hdot(acc_k[...], fold_ref[...])

    row = lambda i: (i, 0)
    col = lambda n: (lambda i: (i, n))
    s = jax.ShapeDtypeStruct
    half = pl.BlockSpec((tm, a_w), row)
    return pl.pallas_call(
        body, name="dz_dx", grid=(nt,),
        out_shape=[s((t_tok, IN_W), BF16), s((t_tok, D_MODEL), F32), s((1, D_MODEL), F32),
                   s((SUBLANES, HEAD_DIM), F32), s((SUBLANES, HEAD_DIM), F32)],
        in_specs=[pl.BlockSpec((tm, D_MODEL), row), pl.BlockSpec((tm, a_w), col(0)), pl.BlockSpec((tm, a_w), col(1)),
                  half, half, half, half, half, half, pl.BlockSpec((tm, D_MODEL), row),
                  _full(w_in_g.shape), _full(g_mix.shape), _full(gq_t.shape), _full(gk_t.shape),
                  _full(ones_bd.shape), _full(fold.shape)],
        out_specs=[pl.BlockSpec((tm, IN_W), row), pl.BlockSpec((tm, D_MODEL), row), _full((1, D_MODEL)),
                   _full((SUBLANES, HEAD_DIM)), _full((SUBLANES, HEAD_DIM))],
        scratch_shapes=[pltpu.VMEM((SUBLANES, a_w), F32), pltpu.VMEM((SUBLANES, a_w), F32)],
        compiler_params=_params(1, VMEM_LIMIT_V7X),
    )(x2, z, z, dqh, dkh, dvb, dga, du, dgs, dh1, w_in_g, g_mix, gq_t, gk_t, ones_bd, fold)


def _dw_in(xn, dz, glu_parts, smalls):
    t_tok = xn.shape[0]
    tk = min(1024, t_tok)
    nk = t_tok // tk
    rs = _ReduceScatter([glu_parts.shape])
    n_small = len(smalls)
    ag = _AllGather(n_small, cast=False)
    n_rs = len(rs.scratch(BF16))

    def place():
        x, y, c = lax.axis_index("x"), lax.axis_index("y"), lax.axis_index("c")
        return x, y, c, [(1 - x, y), (x, 1 - y), (1 - x, 1 - y)]

    def target(i):
        x, y, c, _ = place()
        n = i // 2
        px = jnp.where((n == 0) | (n == 2), 1 - x, x)
        py = jnp.where((n == 1) | (n == 2), 1 - y, y)
        pc = jnp.where(i % 2 == 0, 1 - c, c)
        return 4 * px + 2 * py + pc

    chunk, chunks = _row_chunks(D_MODEL)

    def body(*refs):
        (xn_ref, dz_ref, glu_ref), refs = refs[:3], refs[3:]
        small_refs, refs = list(refs[:n_small]), refs[n_small:]
        (gin_ref, gglu_ref), refs = refs[:2], refs[2:]
        gath_refs, refs = list(refs[:n_small]), refs[n_small:]
        (acc, stage, land, send_sems, recv_sems), rest = refs[:5], refs[5:]
        rs_scratch, ag_sems = rest[:n_rs], rest[n_rs:]
        i, k = pl.program_id(0), pl.program_id(1)
        x, y, c, chips = place()

        def push(slot, to):
            return pltpu.make_async_remote_copy(
                src_ref=stage.at[slot], dst_ref=land.at[slot], send_sem=send_sems.at[slot],
                recv_sem=recv_sems.at[slot], device_id=to, device_id_type=MESH)

        pushes = [push(n, (x, y, 1 - c)) for n in range(4)] + [push(4 + n, (*chips[n], c)) for n in range(3)]

        def staged(slot, plus=None):
            def put(s, carry):
                r = pl.ds(pl.multiple_of(s * chunk, chunk), chunk)
                val = acc[r, :]
                if plus is not None:
                    val = val + land[plus, r, :].astype(F32)
                stage[slot, r, :] = val.astype(BF16)
                return carry

            lax.fori_loop(0, chunks, put, 0)

        @pl.when((i == 0) & (k == 0))
        def _():
            rs.start([glu_ref], rs_scratch)
            ag.start(small_refs, gath_refs, ag_sems)

        @pl.when((i == N_DEV // 2) & (k == 0))
        def _():
            ag.forward(small_refs, gath_refs, ag_sems)

        @pl.when(k == 0)
        def _():
            acc[...] = jnp.zeros_like(acc)

        acc[...] += _dot_tn(xn_ref[...], dz_ref[...])

        for n in range(4):
            @pl.when((k == nk - 1) & (i == 2 * n))
            def _(n=n):
                staged(n)
                pushes[n].start()

        for n in range(3):
            @pl.when((k == nk - 1) & (i == 2 * n + 1))
            def _(n=n):
                pushes[n].wait_recv()
                staged(4 + n, plus=n)
                pushes[4 + n].start()

        @pl.when((k == nk - 1) & (i == N_DEV - 1))
        def _():
            for slot in range(3, N_DEV - 1):
                pushes[slot].wait_recv()

            def add(s, carry):
                r = pl.ds(pl.multiple_of(s * chunk, chunk), chunk)
                total = acc[r, :]
                for slot in range(3, N_DEV - 1):
                    total = total + land[slot, r, :].astype(F32)
                gin_ref[r, :] = total
                return carry

            lax.fori_loop(0, chunks, add, 0)
            for cp in pushes:
                cp.wait_send()
            rs.finish([glu_ref], rs_scratch, [gglu_ref])
            ag.finish(small_refs, gath_refs, ag_sems)

    any_spec = pl.BlockSpec(memory_space=pl.ANY)
    s = jax.ShapeDtypeStruct
    outs = pl.pallas_call(
        body, name="dw_in", grid=(N_DEV, nk),
        out_shape=[s((D_MODEL, COL_W), F32), s(glu_parts.shape[1:], F32)]
        + [s((N_DEV,) + a.shape, a.dtype) for a in smalls],
        in_specs=[pl.BlockSpec((tk, D_MODEL), lambda i, k: (k, 0)),
                  pl.BlockSpec((tk, COL_W), lambda i, k: (k, target(i))), any_spec] + [any_spec] * n_small,
        out_specs=[_full((D_MODEL, COL_W)), _full(glu_parts.shape[1:])] + [any_spec] * n_small,
        scratch_shapes=[pltpu.VMEM((D_MODEL, COL_W), F32), pltpu.VMEM((N_DEV - 1, D_MODEL, COL_W), BF16),
                        pltpu.VMEM((N_DEV - 1, D_MODEL, COL_W), BF16), pltpu.SemaphoreType.DMA((N_DEV - 1,)),
                        pltpu.SemaphoreType.DMA((N_DEV - 1,))] + rs.scratch(BF16) + ag.scratch(),
        compiler_params=_params(2, VMEM_LIMIT_V7X),
    )(xn, dz, glu_parts, *smalls)
    return outs[0], outs[1], outs[2:]


SMALL = ("mix_norm", "q_norm", "k_norm", "lambda_re", "lambda_im", "log_dt", "b_re", "b_im", "c_re", "c_im",
         "d_skip", "b_glu", "ple_norm")
BIG = ("w_in", "w_glu", "w_out", "w_ple_gate", "w_ple_proj")
WEIGHTS = ("mix_norm", "w_in", "q_norm", "k_norm", "lambda_re", "lambda_im", "log_dt", "b_re", "b_im", "c_re",
           "c_im", "d_skip", "w_glu", "b_glu", "w_out", "ple_norm", "w_ple_gate", "w_ple_proj")


def kernel(x, p, mix_norm, w_in, q_norm, k_norm, lambda_re, lambda_im, log_dt, b_re, b_im, c_re, c_im, d_skip, w_glu, b_glu, w_out, ple_norm, w_ple_gate, w_ple_proj, loss_target, m_mix_norm, m_w_in, m_q_norm, m_k_norm, m_lambda_re, m_lambda_im, m_log_dt, m_b_re, m_b_im, m_c_re, m_c_im, m_d_skip, m_w_glu, m_b_glu, m_w_out, m_ple_norm, m_w_ple_gate, m_w_ple_proj, v_mix_norm, v_w_in, v_q_norm, v_k_norm, v_lambda_re, v_lambda_im, v_log_dt, v_b_re, v_b_im, v_c_re, v_c_im, v_d_skip, v_w_glu, v_b_glu, v_w_out, v_ple_norm, v_w_ple_gate, v_w_ple_proj):
    env = dict(locals())
    w = {n: env[n] for n in WEIGHTS}
    m = {n: env["m_" + n] for n in WEIGHTS}
    v = {n: env["v_" + n] for n in WEIGHTS}
    nb, seq, _ = x.shape
    t_tok = nb * seq
    x2 = x.reshape(t_tok, D_MODEL)
    tg2 = loss_target.reshape(t_tok, D_MODEL)
    p2 = p.reshape(t_tok, PLE_DIM)

    shard2d = {"w_in": (D_MODEL, COL_W), "w_glu": (SSM_W // N_DEV, SSM_W), "w_out": (D_MODEL // N_DEV, D_MODEL),
               "w_ple_gate": (D_MODEL // N_DEV, D_MODEL), "w_ple_proj": (PLE_DIM, D_MODEL // N_DEV)}
    w_sh = [w[n].reshape(shard2d[n]) for n in BIG]

    g3 = (SSM_GROUPS, 1, SSM_STATE)
    lr3, li3 = lambda_re.reshape(g3), lambda_im.reshape(g3)
    dt3 = log_dt.reshape(SSM_GROUPS, 1, 1)
    btr = b_re[0].transpose(0, 2, 1)
    bti = b_im[0].transpose(0, 2, 1)
    a_re3, a_im3, bb_re, bb_im, cc_re, cc_im = _zoh_fwd(lr3, li3, dt3, btr, bti, c_re[0], c_im[0])
    a_re, a_im = a_re3.reshape(1, N_STATE), a_im3.reshape(1, N_STATE)

    ones_bd = _head_ones()
    fold = _head_fold()
    gq_t = jnp.tile(q_norm, (1, ATTN_W // HEAD_DIM))
    gk_t = jnp.tile(k_norm, (1, ATTN_W // HEAD_DIM))

    gq2 = jnp.tile(q_norm, (1, LANES // HEAD_DIM))
    gk2 = jnp.tile(k_norm, (1, LANES // HEAD_DIM))

    z, xn, w_in_g = _in_proj(x2, mix_norm, w_sh[0])
    (o, lse, ag, qh, kh), (w_glu_g, w_out_g, w_g_g, w_p_g) = _attn_fwd(z, gq2, gk2, nb, seq, w_sh[1:])
    w_glu_f = w_glu_g.reshape(SSM_W, SSM_W)
    w_out_f = w_out_g.reshape(D_MODEL, D_MODEL)
    w_g_f = w_g_g.reshape(D_MODEL, D_MODEL)
    x_re, x_im, y, sg = _ssm_fwd(z, a_re, a_im, bb_re, bb_im, cc_re, cc_im, d_skip, w_glu_f, b_glu, nb, seq)
    dmix, dh1, loss_t, d_ple, dw_out, dw_g, dw_p = _tail(x2, tg2, ag, sg, p2, w_out_f, w_g_f, w_p_g, ple_norm)

    early_parts = [dw_out.reshape(N_DEV, D_MODEL // N_DEV, D_MODEL), dw_g.reshape(N_DEV, D_MODEL // N_DEV, D_MODEL),
                   dw_p]
    (dqh, dkh, dvb, dga), (g_out, g_g, g_p) = _attn_bwd(qh, kh, z, o, lse, dmix, nb, seq, early_parts)
    (du, dgs, dw_glu, d_bglu, d_dskip, da_re, da_im, dbb_re, dbb_im, dcc_re, dcc_im) = _ssm_bwd(
        z, dmix, y, x_re, x_im, a_re, a_im, bb_re, bb_im, cc_re, cc_im, d_skip, w_glu_f, b_glu, nb, seq)
    d_lr, d_li, d_dt, d_btr, d_bti, d_cr, d_ci = _zoh_bwd(
        lr3, li3, dt3, btr, bti, da_re.reshape(g3), da_im.reshape(g3), dbb_re, dbb_im, dcc_re, dcc_im, fold)

    swapped = ("b_re", "b_im")

    def to_own(n, a):
        a = a.reshape(a.shape[1:]) if a.ndim > 2 else a
        return a.transpose(0, 2, 1) if n in swapped else a

    def from_own(n, a):
        a = a.transpose(0, 2, 1) if n in swapped else a
        return a.reshape(w[n].shape)

    own = {n: to_own(n, w[n]).shape for n in SMALL}
    dz, gx, d_mix, d_gq, d_gk = _dz_and_dx(x2, z, dqh, dkh, dvb, dga, du, dgs, dh1, w_in_g, mix_norm, gq_t, gk_t,
                                           ones_bd, fold)
    small_g = {"mix_norm": d_mix, "q_norm": d_gq[0:1], "k_norm": d_gk[0:1], "lambda_re": d_lr, "lambda_im": d_li,
               "log_dt": d_dt, "b_re": d_btr, "b_im": d_bti, "c_re": d_cr, "c_im": d_ci,
               "d_skip": d_dskip, "b_glu": d_bglu, "ple_norm": d_ple}
    g_in, g_glu, gathered = _dw_in(xn, dz, dw_glu.reshape(N_DEV, SSM_W // N_DEV, SSM_W),
                                   [small_g[n].reshape(own[n]) for n in SMALL] + [loss_t])
    g_sh = [g_in, g_glu, g_out, g_g, g_p]
    d_sh, m_sh, v_sh = _adamw_shards(g_sh, w_sh, [m[n].reshape(shard2d[n]) for n in BIG],
                                     [v[n].reshape(shard2d[n]) for n in BIG])

    *g_small, loss_sum = _small_sum(list(gathered))
    d_small, m_small, v_small = _adamw_small(
        g_small, *[[to_own(n, src[n]) for n in SMALL] for src in (w, m, v)])

    grads, deltas, new_m, new_v = {}, {}, {}, {}
    for dst, arrs in ((grads, g_small), (deltas, d_small), (new_m, m_small), (new_v, v_small)):
        for n, a in zip(SMALL, arrs):
            dst[n] = from_own(n, a)
    for i, n in enumerate(BIG):
        grads[n] = g_sh[i].reshape(w[n].shape)
        deltas[n] = d_sh[i].reshape(w[n].shape)
        new_m[n] = m_sh[i].reshape(w[n].shape)
        new_v[n] = v_sh[i].reshape(w[n].shape)

    loss = loss_sum[0, 0]
    return (loss, gx.reshape(x.shape), *[grads[n] for n in WEIGHTS], *[deltas[n] for n in WEIGHTS],
            *[new_m[n] for n in WEIGHTS], *[new_v[n] for n in WEIGHTS])
```
